```python
import jax, jax.numpy as jnp
from jax import lax
import numpy as np

D_MODEL = 2048
BATCH = 8
SEQ = 8192
DEPTH = 1

CHUNK = 64
GMLP_BLOCK = 128
A_WIDTH = D_MODEL // 2
A_GROUPS = 8
A_HEAD = A_WIDTH // A_GROUPS
B_WIDTH = D_MODEL // 2
CONV_WIDTH = 31
D_FF = ((8 * D_MODEL // 3 + 255) // 256) * 256
LN_EPS = 1e-5
ALPHA = (2.0 * DEPTH) ** 0.25
BETA = (8.0 * DEPTH) ** -0.25
IN_COLS = 2 * A_WIDTH + 2 * B_WIDTH + 2 * D_MODEL

kernel_name = "hybrid_gmlp_conformer_conv_macaron_deepnorm"


def layer_norm(x, g, b):
    xf = x.astype(jnp.float32)
    mu = jnp.mean(xf, axis=-1, keepdims=True)
    var = jnp.mean(jnp.square(xf - mu), axis=-1, keepdims=True)
    y = (xf - mu) * lax.rsqrt(var + LN_EPS)
    return (y * g.astype(jnp.float32) + b.astype(jnp.float32)).astype(x.dtype)


def swiglu(x, w_gu, w_down):
    gate, up = jnp.split(x @ w_gu, 2, axis=-1)
    return (jax.nn.silu(gate) * up) @ w_down


def block_causal_mask(n):
    c = jnp.arange(n) // CHUNK
    return c[None, :] <= c[:, None]


def spatial_gating(u, v, ln_g, ln_b, w_s, b_s):
    bsz, seq, _ = v.shape
    v = layer_norm(v, ln_g, ln_b)
    nblk = seq // GMLP_BLOCK
    vb = v.reshape(bsz, nblk, GMLP_BLOCK, A_GROUPS, A_HEAD)
    w = jnp.where(block_causal_mask(GMLP_BLOCK)[None], w_s, jnp.zeros((), w_s.dtype))
    s = jnp.einsum('hij,bnjhd->bnihd', w, vb) + jnp.transpose(b_s)[None, None, :, :, None]
    return u * s.reshape(bsz, seq, A_WIDTH)


def conv_module(h, w_dw, b_dw, ln_g, ln_b):
    a, g = jnp.split(h, 2, axis=-1)
    z = a * jax.nn.sigmoid(g)
    z = lax.conv_general_dilated(
        z, w_dw[:, None, :], window_strides=(1,),
        padding=[(CONV_WIDTH - 1, 0)],
        dimension_numbers=('NWC', 'WIO', 'NWC'),
        feature_group_count=B_WIDTH) + b_dw
    z = layer_norm(z, ln_g, ln_b)
    return jax.nn.silu(z)


def _fwd_setup_inputs(seed: int = 0) -> dict:
    key = jax.random.key(seed)
    ks = jax.random.split(key, 32)
    n = jax.random.normal
    L = DEPTH
    def gain(k, d):
        return 1.0 + 0.01 * n(k, (L, d), jnp.float32)
    def bias(k, shape):
        return 0.01 * n(k, (L,) + shape, jnp.float32)
    return {
        "x": n(ks[0], (BATCH, SEQ, D_MODEL), jnp.float32),
        "ffn1_w_gu": n(ks[1], (L, D_MODEL, 2 * D_FF), jnp.float32) * D_MODEL ** -0.5,
        "ffn1_w_down": n(ks[2], (L, D_FF, D_MODEL), jnp.float32) * (D_FF ** -0.5 * BETA),
        "ln1_g": gain(ks[3], D_MODEL),
        "ln1_b": bias(ks[4], (D_MODEL,)),
        "w_in": n(ks[5], (L, D_MODEL, IN_COLS), jnp.float32) * D_MODEL ** -0.5,
        "b_in": bias(ks[6], (IN_COLS,)),
        "sgu_ln_g": gain(ks[7], A_WIDTH),
        "sgu_ln_b": bias(ks[8], (A_WIDTH,)),
        "sgu_w_s": n(ks[9], (L, A_GROUPS, GMLP_BLOCK, GMLP_BLOCK), jnp.float32) * (0.5 * GMLP_BLOCK ** -0.5),
        "sgu_b_s": 1.0 + 0.01 * n(ks[10], (L, A_GROUPS, GMLP_BLOCK), jnp.float32),
        "w_a_proj": n(ks[11], (L, A_WIDTH, D_MODEL), jnp.float32) * (A_WIDTH ** -0.5 * BETA),
        "conv_w_dw": n(ks[12], (L, CONV_WIDTH, B_WIDTH), jnp.float32) * CONV_WIDTH ** -0.5,
        "conv_b_dw": bias(ks[13], (B_WIDTH,)),
        "conv_ln_g": gain(ks[14], B_WIDTH),
        "conv_ln_b": bias(ks[15], (B_WIDTH,)),
        "w_b_proj": n(ks[16], (L, B_WIDTH, D_MODEL), jnp.float32) * (B_WIDTH ** -0.5 * BETA),
        "w_out": n(ks[17], (L, D_MODEL, D_MODEL), jnp.float32) * (D_MODEL ** -0.5 * BETA),
        "ln2_g": gain(ks[18], D_MODEL),
        "ln2_b": bias(ks[19], (D_MODEL,)),
        "ffn2_w_gu": n(ks[20], (L, D_MODEL, 2 * D_FF), jnp.float32) * D_MODEL ** -0.5,
        "ffn2_w_down": n(ks[21], (L, D_FF, D_MODEL), jnp.float32) * (D_FF ** -0.5 * BETA),
        "ln3_g": gain(ks[22], D_MODEL),
        "ln3_b": bias(ks[23], (D_MODEL,)),
    }


def _fwd_reference(x, ffn1_w_gu, ffn1_w_down, ln1_g, ln1_b, w_in, b_in, sgu_ln_g, sgu_ln_b,
              sgu_w_s, sgu_b_s, w_a_proj, conv_w_dw, conv_b_dw, conv_ln_g, conv_ln_b,
              w_b_proj, w_out, ln2_g, ln2_b, ffn2_w_gu, ffn2_w_down, ln3_g, ln3_b):
    for l in range(DEPTH):
        x = layer_norm(ALPHA * x + 0.5 * swiglu(x, ffn1_w_gu[l], ffn1_w_down[l]), ln1_g[l], ln1_b[l])

        proj = x @ w_in[l] + b_in[l]
        u_a, v_a, h_b, gate_logits = jnp.split(
            proj, [A_WIDTH, 2 * A_WIDTH, 2 * A_WIDTH + 2 * B_WIDTH], axis=-1)

        y_a = spatial_gating(jax.nn.gelu(u_a), jax.nn.gelu(v_a), sgu_ln_g[l], sgu_ln_b[l],
                             sgu_w_s[l], sgu_b_s[l]) @ w_a_proj[l]
        y_b = conv_module(h_b, conv_w_dw[l], conv_b_dw[l], conv_ln_g[l], conv_ln_b[l]) @ w_b_proj[l]

        g_a, g_b = jnp.split(jax.nn.sigmoid(gate_logits), 2, axis=-1)
        mix = (g_a * y_a + g_b * y_b) @ w_out[l]
        x = layer_norm(ALPHA * x + mix, ln2_g[l], ln2_b[l])

        x = layer_norm(ALPHA * x + 0.5 * swiglu(x, ffn2_w_gu[l], ffn2_w_down[l]), ln3_g[l], ln3_b[l])
    return x


import jax as _jax
import jax.numpy as _jnp

TWIN_FORMAT = 'train_step'
FWD_PARAMS = ['x', 'ffn1_w_gu', 'ffn1_w_down', 'ln1_g', 'ln1_b', 'w_in', 'b_in', 'sgu_ln_g', 'sgu_ln_b', 'sgu_w_s', 'sgu_b_s', 'w_a_proj', 'conv_w_dw', 'conv_b_dw', 'conv_ln_g', 'conv_ln_b', 'w_b_proj', 'w_out', 'ln2_g', 'ln2_b', 'ffn2_w_gu', 'ffn2_w_down', 'ln3_g', 'ln3_b']
TWIN_WEIGHTS = ['ffn1_w_gu', 'ffn1_w_down', 'ln1_g', 'ln1_b', 'w_in', 'b_in', 'sgu_ln_g', 'sgu_ln_b', 'sgu_w_s', 'sgu_b_s', 'w_a_proj', 'conv_w_dw', 'conv_b_dw', 'conv_ln_g', 'conv_ln_b', 'w_b_proj', 'w_out', 'ln2_g', 'ln2_b', 'ffn2_w_gu', 'ffn2_w_down', 'ln3_g', 'ln3_b']
TWIN_DIFF_INPUT = 'x'
TWIN_INPUTS = ['x', 'ffn1_w_gu', 'ffn1_w_down', 'ln1_g', 'ln1_b', 'w_in', 'b_in', 'sgu_ln_g', 'sgu_ln_b', 'sgu_w_s', 'sgu_b_s', 'w_a_proj', 'conv_w_dw', 'conv_b_dw', 'conv_ln_g', 'conv_ln_b', 'w_b_proj', 'w_out', 'ln2_g', 'ln2_b', 'ffn2_w_gu', 'ffn2_w_down', 'ln3_g', 'ln3_b', 'loss_target', 'm_ffn1_w_gu', 'm_ffn1_w_down', 'm_ln1_g', 'm_ln1_b', 'm_w_in', 'm_b_in', 'm_sgu_ln_g', 'm_sgu_ln_b', 'm_sgu_w_s', 'm_sgu_b_s', 'm_w_a_proj', 'm_conv_w_dw', 'm_conv_b_dw', 'm_conv_ln_g', 'm_conv_ln_b', 'm_w_b_proj', 'm_w_out', 'm_ln2_g', 'm_ln2_b', 'm_ffn2_w_gu', 'm_ffn2_w_down', 'm_ln3_g', 'm_ln3_b', 'v_ffn1_w_gu', 'v_ffn1_w_down', 'v_ln1_g', 'v_ln1_b', 'v_w_in', 'v_b_in', 'v_sgu_ln_g', 'v_sgu_ln_b', 'v_sgu_w_s', 'v_sgu_b_s', 'v_w_a_proj', 'v_conv_w_dw', 'v_conv_b_dw', 'v_conv_ln_g', 'v_conv_ln_b', 'v_w_b_proj', 'v_w_out', 'v_ln2_g', 'v_ln2_b', 'v_ffn2_w_gu', 'v_ffn2_w_down', 'v_ln3_g', 'v_ln3_b']
TWIN_OUTPUTS = ['loss', 'grad_x', 'grad_ffn1_w_gu', 'grad_ffn1_w_down', 'grad_ln1_g', 'grad_ln1_b', 'grad_w_in', 'grad_b_in', 'grad_sgu_ln_g', 'grad_sgu_ln_b', 'grad_sgu_w_s', 'grad_sgu_b_s', 'grad_w_a_proj', 'grad_conv_w_dw', 'grad_conv_b_dw', 'grad_conv_ln_g', 'grad_conv_ln_b', 'grad_w_b_proj', 'grad_w_out', 'grad_ln2_g', 'grad_ln2_b', 'grad_ffn2_w_gu', 'grad_ffn2_w_down', 'grad_ln3_g', 'grad_ln3_b', 'delta_ffn1_w_gu', 'delta_ffn1_w_down', 'delta_ln1_g', 'delta_ln1_b', 'delta_w_in', 'delta_b_in', 'delta_sgu_ln_g', 'delta_sgu_ln_b', 'delta_sgu_w_s', 'delta_sgu_b_s', 'delta_w_a_proj', 'delta_conv_w_dw', 'delta_conv_b_dw', 'delta_conv_ln_g', 'delta_conv_ln_b', 'delta_w_b_proj', 'delta_w_out', 'delta_ln2_g', 'delta_ln2_b', 'delta_ffn2_w_gu', 'delta_ffn2_w_down', 'delta_ln3_g', 'delta_ln3_b', 'new_m_ffn1_w_gu', 'new_m_ffn1_w_down', 'new_m_ln1_g', 'new_m_ln1_b', 'new_m_w_in', 'new_m_b_in', 'new_m_sgu_ln_g', 'new_m_sgu_ln_b', 'new_m_sgu_w_s', 'new_m_sgu_b_s', 'new_m_w_a_proj', 'new_m_conv_w_dw', 'new_m_conv_b_dw', 'new_m_conv_ln_g', 'new_m_conv_ln_b', 'new_m_w_b_proj', 'new_m_w_out', 'new_m_ln2_g', 'new_m_ln2_b', 'new_m_ffn2_w_gu', 'new_m_ffn2_w_down', 'new_m_ln3_g', 'new_m_ln3_b', 'new_v_ffn1_w_gu', 'new_v_ffn1_w_down', 'new_v_ln1_g', 'new_v_ln1_b', 'new_v_w_in', 'new_v_b_in', 'new_v_sgu_ln_g', 'new_v_sgu_ln_b', 'new_v_sgu_w_s', 'new_v_sgu_b_s', 'new_v_w_a_proj', 'new_v_conv_w_dw', 'new_v_conv_b_dw', 'new_v_conv_ln_g', 'new_v_conv_ln_b', 'new_v_w_b_proj', 'new_v_w_out', 'new_v_ln2_g', 'new_v_ln2_b', 'new_v_ffn2_w_gu', 'new_v_ffn2_w_down', 'new_v_ln3_g', 'new_v_ln3_b']
TWIN_LEAF_KINDS = {'loss': 'loss', 'grad_x': 'grad_x', 'grad_ffn1_w_gu': 'grad_w', 'grad_ffn1_w_down': 'grad_w', 'grad_ln1_g': 'grad_w', 'grad_ln1_b': 'grad_w', 'grad_w_in': 'grad_w', 'grad_b_in': 'grad_w', 'grad_sgu_ln_g': 'grad_w', 'grad_sgu_ln_b': 'grad_w', 'grad_sgu_w_s': 'grad_w', 'grad_sgu_b_s': 'grad_w', 'grad_w_a_proj': 'grad_w', 'grad_conv_w_dw': 'grad_w', 'grad_conv_b_dw': 'grad_w', 'grad_conv_ln_g': 'grad_w', 'grad_conv_ln_b': 'grad_w', 'grad_w_b_proj': 'grad_w', 'grad_w_out': 'grad_w', 'grad_ln2_g': 'grad_w', 'grad_ln2_b': 'grad_w', 'grad_ffn2_w_gu': 'grad_w', 'grad_ffn2_w_down': 'grad_w', 'grad_ln3_g': 'grad_w', 'grad_ln3_b': 'grad_w', 'delta_ffn1_w_gu': 'delta_w', 'delta_ffn1_w_down': 'delta_w', 'delta_ln1_g': 'delta_w', 'delta_ln1_b': 'delta_w', 'delta_w_in': 'delta_w', 'delta_b_in': 'delta_w', 'delta_sgu_ln_g': 'delta_w', 'delta_sgu_ln_b': 'delta_w', 'delta_sgu_w_s': 'delta_w', 'delta_sgu_b_s': 'delta_w', 'delta_w_a_proj': 'delta_w', 'delta_conv_w_dw': 'delta_w', 'delta_conv_b_dw': 'delta_w', 'delta_conv_ln_g': 'delta_w', 'delta_conv_ln_b': 'delta_w', 'delta_w_b_proj': 'delta_w', 'delta_w_out': 'delta_w', 'delta_ln2_g': 'delta_w', 'delta_ln2_b': 'delta_w', 'delta_ffn2_w_gu': 'delta_w', 'delta_ffn2_w_down': 'delta_w', 'delta_ln3_g': 'delta_w', 'delta_ln3_b': 'delta_w', 'new_m_ffn1_w_gu': 'new_m', 'new_m_ffn1_w_down': 'new_m', 'new_m_ln1_g': 'new_m', 'new_m_ln1_b': 'new_m', 'new_m_w_in': 'new_m', 'new_m_b_in': 'new_m', 'new_m_sgu_ln_g': 'new_m', 'new_m_sgu_ln_b': 'new_m', 'new_m_sgu_w_s': 'new_m', 'new_m_sgu_b_s': 'new_m', 'new_m_w_a_proj': 'new_m', 'new_m_conv_w_dw': 'new_m', 'new_m_conv_b_dw': 'new_m', 'new_m_conv_ln_g': 'new_m', 'new_m_conv_ln_b': 'new_m', 'new_m_w_b_proj': 'new_m', 'new_m_w_out': 'new_m', 'new_m_ln2_g': 'new_m', 'new_m_ln2_b': 'new_m', 'new_m_ffn2_w_gu': 'new_m', 'new_m_ffn2_w_down': 'new_m', 'new_m_ln3_g': 'new_m', 'new_m_ln3_b': 'new_m', 'new_v_ffn1_w_gu': 'new_v', 'new_v_ffn1_w_down': 'new_v', 'new_v_ln1_g': 'new_v', 'new_v_ln1_b': 'new_v', 'new_v_w_in': 'new_v', 'new_v_b_in': 'new_v', 'new_v_sgu_ln_g': 'new_v', 'new_v_sgu_ln_b': 'new_v', 'new_v_sgu_w_s': 'new_v', 'new_v_sgu_b_s': 'new_v', 'new_v_w_a_proj': 'new_v', 'new_v_conv_w_dw': 'new_v', 'new_v_conv_b_dw': 'new_v', 'new_v_conv_ln_g': 'new_v', 'new_v_conv_ln_b': 'new_v', 'new_v_w_b_proj': 'new_v', 'new_v_w_out': 'new_v', 'new_v_ln2_g': 'new_v', 'new_v_ln2_b': 'new_v', 'new_v_ffn2_w_gu': 'new_v', 'new_v_ffn2_w_down': 'new_v', 'new_v_ln3_g': 'new_v', 'new_v_ln3_b': 'new_v'}


def _forward(args):
    return _fwd_reference(*[args[k] for k in FWD_PARAMS])


def _output_shape():
    def fwd():
        inp = _fwd_setup_inputs(0)
        return _fwd_reference(*[inp[k] for k in FWD_PARAMS])
    out = _jax.eval_shape(fwd)
    return out.shape, out.dtype

N_MICROBATCH = 1
ADAM_LR = 0.001
ADAM_B1 = 0.9
ADAM_B2 = 0.999
ADAM_EPS = 1e-08
ADAM_WD = 0.01
ADAM_STEP = 10
PER_EXAMPLE_BATCH_AXIS = {'x': 0, 'loss_target': 0}
SHARED_INPUTS = []
_WEIGHT_DTYPES = {'ffn1_w_gu': _jnp.float32, 'ffn1_w_down': _jnp.float32, 'ln1_g': _jnp.float32, 'ln1_b': _jnp.float32, 'w_in': _jnp.float32, 'b_in': _jnp.float32, 'sgu_ln_g': _jnp.float32, 'sgu_ln_b': _jnp.float32, 'sgu_w_s': _jnp.float32, 'sgu_b_s': _jnp.float32, 'w_a_proj': _jnp.float32, 'conv_w_dw': _jnp.float32, 'conv_b_dw': _jnp.float32, 'conv_ln_g': _jnp.float32, 'conv_ln_b': _jnp.float32, 'w_b_proj': _jnp.float32, 'w_out': _jnp.float32, 'ln2_g': _jnp.float32, 'ln2_b': _jnp.float32, 'ffn2_w_gu': _jnp.float32, 'ffn2_w_down': _jnp.float32, 'ln3_g': _jnp.float32, 'ln3_b': _jnp.float32}
MOMENT_SCALE = {'ffn1_w_gu': 1.148800e-02, 'ffn1_w_down': 3.149593e-02, 'ln1_g': 4.480045e-01, 'ln1_b': 2.324487e-01, 'w_in': 1.114520e-02, 'b_in': 1.348731e-02, 'sgu_ln_g': 8.671669e-03, 'sgu_ln_b': 7.932984e-03, 'sgu_w_s': 1.616731e-02, 'sgu_b_s': 1.916526e-02, 'w_a_proj': 2.625596e-02, 'conv_w_dw': 1.775284e-02, 'conv_b_dw': 3.932761e-02, 'conv_ln_g': 2.332242e-02, 'conv_ln_b': 2.347851e-02, 'w_b_proj': 2.185887e-02, 'w_out': 3.505344e-02, 'ln2_g': 4.629158e-01, 'ln2_b': 2.341414e-01, 'ffn2_w_gu': 1.130115e-02, 'ffn2_w_down': 3.102105e-02, 'ln3_g': 3.197032e+01, 'ln3_b': 1.591462e+00}


def _to_microbatches(a, axis):
    t = _jnp.moveaxis(a, axis, 0)
    t = t.reshape((N_MICROBATCH, t.shape[0] // N_MICROBATCH) + t.shape[1:])
    return _jnp.moveaxis(t, 1, axis + 1)


def setup_inputs(seed: int = 0) -> dict:
    inp = _fwd_setup_inputs(seed)
    key = _jax.random.fold_in(_jax.random.key(seed), 7919)
    shape, _ = _output_shape()
    out = dict(inp)
    out["loss_target"] = _jax.random.normal(_jax.random.fold_in(key, 0), shape, _jnp.float32)
    for i, name in enumerate(TWIN_WEIGHTS):
        w = inp[name].astype(_jnp.float32)
        if MOMENT_SCALE is None:
            s = _jnp.sqrt(_jnp.mean(_jnp.square(w)) + 1e-30)
        else:
            s = MOMENT_SCALE[name]
        km, kv = _jax.random.split(_jax.random.fold_in(key, i + 1))
        out[name] = w
        out["m_" + name] = s * _jax.random.normal(km, w.shape, _jnp.float32)
        out["v_" + name] = (s * s) * _jax.random.uniform(kv, w.shape, _jnp.float32, 0.5, 1.5)
    if N_MICROBATCH > 1:
        for name, axis in PER_EXAMPLE_BATCH_AXIS.items():
            out[name] = _to_microbatches(out[name], axis)
    return {'x': out['x'], 'ffn1_w_gu': out['ffn1_w_gu'], 'ffn1_w_down': out['ffn1_w_down'], 'ln1_g': out['ln1_g'], 'ln1_b': out['ln1_b'], 'w_in': out['w_in'], 'b_in': out['b_in'], 'sgu_ln_g': out['sgu_ln_g'], 'sgu_ln_b': out['sgu_ln_b'], 'sgu_w_s': out['sgu_w_s'], 'sgu_b_s': out['sgu_b_s'], 'w_a_proj': out['w_a_proj'], 'conv_w_dw': out['conv_w_dw'], 'conv_b_dw': out['conv_b_dw'], 'conv_ln_g': out['conv_ln_g'], 'conv_ln_b': out['conv_ln_b'], 'w_b_proj': out['w_b_proj'], 'w_out': out['w_out'], 'ln2_g': out['ln2_g'], 'ln2_b': out['ln2_b'], 'ffn2_w_gu': out['ffn2_w_gu'], 'ffn2_w_down': out['ffn2_w_down'], 'ln3_g': out['ln3_g'], 'ln3_b': out['ln3_b'], 'loss_target': out['loss_target'], 'm_ffn1_w_gu': out['m_ffn1_w_gu'], 'm_ffn1_w_down': out['m_ffn1_w_down'], 'm_ln1_g': out['m_ln1_g'], 'm_ln1_b': out['m_ln1_b'], 'm_w_in': out['m_w_in'], 'm_b_in': out['m_b_in'], 'm_sgu_ln_g': out['m_sgu_ln_g'], 'm_sgu_ln_b': out['m_sgu_ln_b'], 'm_sgu_w_s': out['m_sgu_w_s'], 'm_sgu_b_s': out['m_sgu_b_s'], 'm_w_a_proj': out['m_w_a_proj'], 'm_conv_w_dw': out['m_conv_w_dw'], 'm_conv_b_dw': out['m_conv_b_dw'], 'm_conv_ln_g': out['m_conv_ln_g'], 'm_conv_ln_b': out['m_conv_ln_b'], 'm_w_b_proj': out['m_w_b_proj'], 'm_w_out': out['m_w_out'], 'm_ln2_g': out['m_ln2_g'], 'm_ln2_b': out['m_ln2_b'], 'm_ffn2_w_gu': out['m_ffn2_w_gu'], 'm_ffn2_w_down': out['m_ffn2_w_down'], 'm_ln3_g': out['m_ln3_g'], 'm_ln3_b': out['m_ln3_b'], 'v_ffn1_w_gu': out['v_ffn1_w_gu'], 'v_ffn1_w_down': out['v_ffn1_w_down'], 'v_ln1_g': out['v_ln1_g'], 'v_ln1_b': out['v_ln1_b'], 'v_w_in': out['v_w_in'], 'v_b_in': out['v_b_in'], 'v_sgu_ln_g': out['v_sgu_ln_g'], 'v_sgu_ln_b': out['v_sgu_ln_b'], 'v_sgu_w_s': out['v_sgu_w_s'], 'v_sgu_b_s': out['v_sgu_b_s'], 'v_w_a_proj': out['v_w_a_proj'], 'v_conv_w_dw': out['v_conv_w_dw'], 'v_conv_b_dw': out['v_conv_b_dw'], 'v_conv_ln_g': out['v_conv_ln_g'], 'v_conv_ln_b': out['v_conv_ln_b'], 'v_w_b_proj': out['v_w_b_proj'], 'v_w_out': out['v_w_out'], 'v_ln2_g': out['v_ln2_g'], 'v_ln2_b': out['v_ln2_b'], 'v_ffn2_w_gu': out['v_ffn2_w_gu'], 'v_ffn2_w_down': out['v_ffn2_w_down'], 'v_ln3_g': out['v_ln3_g'], 'v_ln3_b': out['v_ln3_b']}


def _loss(weights, diff, rest, loss_target):
    with _jax.named_scope("forward"):
        args = {**rest, TWIN_DIFF_INPUT: diff, **{k: w.astype(_WEIGHT_DTYPES[k]) for k, w in weights.items()}}
        y = _forward(args)
    with _jax.named_scope("loss_head"):
        err = _jnp.square(y.astype(_jnp.float32) - loss_target)
        return 0.5 * _jnp.sum(_jnp.mean(err, axis=-1)) if err.ndim else 0.5 * err


def _adamw(w, g, m, v):
    m = ADAM_B1 * m + (1.0 - ADAM_B1) * g
    v = ADAM_B2 * v + (1.0 - ADAM_B2) * _jnp.square(g)
    m_hat = m / (1.0 - ADAM_B1 ** ADAM_STEP)
    v_hat = v / (1.0 - ADAM_B2 ** ADAM_STEP)
    delta = -ADAM_LR * (m_hat / (_jnp.sqrt(v_hat) + ADAM_EPS) + ADAM_WD * w)
    return delta, m, v


def reference(x, ffn1_w_gu, ffn1_w_down, ln1_g, ln1_b, w_in, b_in, sgu_ln_g, sgu_ln_b, sgu_w_s, sgu_b_s, w_a_proj, conv_w_dw, conv_b_dw, conv_ln_g, conv_ln_b, w_b_proj, w_out, ln2_g, ln2_b, ffn2_w_gu, ffn2_w_down, ln3_g, ln3_b, loss_target, m_ffn1_w_gu, m_ffn1_w_down, m_ln1_g, m_ln1_b, m_w_in, m_b_in, m_sgu_ln_g, m_sgu_ln_b, m_sgu_w_s, m_sgu_b_s, m_w_a_proj, m_conv_w_dw, m_conv_b_dw, m_conv_ln_g, m_conv_ln_b, m_w_b_proj, m_w_out, m_ln2_g, m_ln2_b, m_ffn2_w_gu, m_ffn2_w_down, m_ln3_g, m_ln3_b, v_ffn1_w_gu, v_ffn1_w_down, v_ln1_g, v_ln1_b, v_w_in, v_b_in, v_sgu_ln_g, v_sgu_ln_b, v_sgu_w_s, v_sgu_b_s, v_w_a_proj, v_conv_w_dw, v_conv_b_dw, v_conv_ln_g, v_conv_ln_b, v_w_b_proj, v_w_out, v_ln2_g, v_ln2_b, v_ffn2_w_gu, v_ffn2_w_down, v_ln3_g, v_ln3_b):
    given = dict(x=x, ffn1_w_gu=ffn1_w_gu, ffn1_w_down=ffn1_w_down, ln1_g=ln1_g, ln1_b=ln1_b, w_in=w_in, b_in=b_in, sgu_ln_g=sgu_ln_g, sgu_ln_b=sgu_ln_b, sgu_w_s=sgu_w_s, sgu_b_s=sgu_b_s, w_a_proj=w_a_proj, conv_w_dw=conv_w_dw, conv_b_dw=conv_b_dw, conv_ln_g=conv_ln_g, conv_ln_b=conv_ln_b, w_b_proj=w_b_proj, w_out=w_out, ln2_g=ln2_g, ln2_b=ln2_b, ffn2_w_gu=ffn2_w_gu, ffn2_w_down=ffn2_w_down, ln3_g=ln3_g, ln3_b=ln3_b, loss_target=loss_target, m_ffn1_w_gu=m_ffn1_w_gu, m_ffn1_w_down=m_ffn1_w_down, m_ln1_g=m_ln1_g, m_ln1_b=m_ln1_b, m_w_in=m_w_in, m_b_in=m_b_in, m_sgu_ln_g=m_sgu_ln_g, m_sgu_ln_b=m_sgu_ln_b, m_sgu_w_s=m_sgu_w_s, m_sgu_b_s=m_sgu_b_s, m_w_a_proj=m_w_a_proj, m_conv_w_dw=m_conv_w_dw, m_conv_b_dw=m_conv_b_dw, m_conv_ln_g=m_conv_ln_g, m_conv_ln_b=m_conv_ln_b, m_w_b_proj=m_w_b_proj, m_w_out=m_w_out, m_ln2_g=m_ln2_g, m_ln2_b=m_ln2_b, m_ffn2_w_gu=m_ffn2_w_gu, m_ffn2_w_down=m_ffn2_w_down, m_ln3_g=m_ln3_g, m_ln3_b=m_ln3_b, v_ffn1_w_gu=v_ffn1_w_gu, v_ffn1_w_down=v_ffn1_w_down, v_ln1_g=v_ln1_g, v_ln1_b=v_ln1_b, v_w_in=v_w_in, v_b_in=v_b_in, v_sgu_ln_g=v_sgu_ln_g, v_sgu_ln_b=v_sgu_ln_b, v_sgu_w_s=v_sgu_w_s, v_sgu_b_s=v_sgu_b_s, v_w_a_proj=v_w_a_proj, v_conv_w_dw=v_conv_w_dw, v_conv_b_dw=v_conv_b_dw, v_conv_ln_g=v_conv_ln_g, v_conv_ln_b=v_conv_ln_b, v_w_b_proj=v_w_b_proj, v_w_out=v_w_out, v_ln2_g=v_ln2_g, v_ln2_b=v_ln2_b, v_ffn2_w_gu=v_ffn2_w_gu, v_ffn2_w_down=v_ffn2_w_down, v_ln3_g=v_ln3_g, v_ln3_b=v_ln3_b)
    weights = {n: given[n] for n in TWIN_WEIGHTS}
    shared = {n: given[n] for n in SHARED_INPUTS}
    per_example = {n: given[n] for n in ['x']}
    grad_fn = _jax.value_and_grad(_loss, argnums=(0, 1))

    def one_microbatch(ex, loss_target):
        ex = dict(ex)
        diff = ex.pop(TWIN_DIFF_INPUT)
        return grad_fn(weights, diff, {**shared, **ex}, loss_target)

    if N_MICROBATCH == 1:
        loss, (grad_w, grad_x) = one_microbatch(per_example, given["loss_target"])
    else:
        def body(carry, xs):
            loss_sum, grad_sum = carry
            l_k, (gw_k, gx_k) = one_microbatch(xs[0], xs[1])
            with _jax.named_scope("update"):
                return (loss_sum + l_k, _jax.tree.map(_jnp.add, grad_sum, gw_k)), gx_k

        init = (_jnp.zeros((), _jnp.float32), _jax.tree.map(_jnp.zeros_like, weights))
        (loss, grad_w), grad_x = _jax.lax.scan(body, init, (per_example, given["loss_target"]))
    with _jax.named_scope("update"):
        delta_w, new_m, new_v = {}, {}, {}
        for n in TWIN_WEIGHTS:
            delta_w[n], new_m[n], new_v[n] = _adamw(weights[n], grad_w[n], given["m_" + n], given["v_" + n])
    return (loss, grad_x, *[grad_w[n] for n in TWIN_WEIGHTS], *[delta_w[n] for n in TWIN_WEIGHTS],
            *[new_m[n] for n in TWIN_WEIGHTS], *[new_v[n] for n in TWIN_WEIGHTS])
```

```python
import functools
import math

import jax
import jax.numpy as jnp
from jax import lax
from jax.experimental import pallas as pl
from jax.experimental.pallas import tpu as pltpu

BF16 = jnp.bfloat16
F32 = jnp.float32

LN_EPS = 1e-5
ALPHA = 2.0 ** 0.25
SGU_BLOCK = 128
SGU_CHUNK = 64
HALO = 32
ADAM_LR = 0.001
ADAM_B1 = 0.9
ADAM_B2 = 0.999
ADAM_EPS = 1e-08
ADAM_WD = 0.01
ADAM_STEP = 10
N_CHIPS = 4
VMEM_LIMIT = 52 * 1024 * 1024
MESH = pl.DeviceIdType.MESH

_GELU_C0 = math.sqrt(2.0 / math.pi)
_GELU_C1 = 0.044715


def _cparams(sem):
    return pltpu.CompilerParams(dimension_semantics=sem, vmem_limit_bytes=VMEM_LIMIT)


def _gelu_parts(x):
    x2 = x * x
    t = jnp.tanh(_GELU_C0 * (x + _GELU_C1 * (x2 * x)))
    return 0.5 * (1.0 + t), t, x2


def _gelu(x):
    cdf, _, _ = _gelu_parts(x)
    return x * cdf


def _gelu_and_grad(x):
    cdf, t, x2 = _gelu_parts(x)
    grad = cdf + x * (0.5 * (1.0 - t * t)) * (_GELU_C0 * (1.0 + (3.0 * _GELU_C1) * x2))
    return x * cdf, grad


def _silu_grad(x):
    s = jax.nn.sigmoid(x)
    return s * (1.0 + x * (1.0 - s))


def _row_stats(x):
    mu = jnp.mean(x, axis=-1, keepdims=True)
    xc = x - mu
    var = jnp.mean(xc * xc, axis=-1, keepdims=True)
    rstd = lax.rsqrt(var + LN_EPS)
    return xc * rstd, rstd


def _ln_bwd_rows(dy, xhat, rstd, g):
    dxh = dy * g
    m1 = jnp.mean(dxh, axis=-1, keepdims=True)
    m2 = jnp.mean(dxh * xhat, axis=-1, keepdims=True)
    return rstd * (dxh - m1 - xhat * m2)


def _colsum(v):
    return jnp.sum(v, axis=0, keepdims=True)


def _mm(name, pairs, *, ta=False, tb=False, M, N, tm, tn, tk, nk, epilogue, outs, extras=(), sum_pairs=True,
        alias=None):
    ni, nj = M // tm, N // tn
    assert ni * tm == M and nj * tn == N
    n_p = len(pairs)
    n_acc = 1 if sum_pairs else n_p
    in_specs, args = [], []
    for a, b, ak, bk, bj in pairs:
        if ta:
            in_specs.append(pl.BlockSpec((tk, tm), lambda i, j, k, ak=ak: (k + ak, i)))
        else:
            in_specs.append(pl.BlockSpec((tm, tk), lambda i, j, k, ak=ak: (i, k + ak)))
        if tb:
            in_specs.append(pl.BlockSpec((tn, tk), lambda i, j, k, bk=bk, bj=bj: (j + bj, k + bk)))
        else:
            in_specs.append(pl.BlockSpec((tk, tn), lambda i, j, k, bk=bk, bj=bj: (k + bk, j + bj)))
        args += [a, b]
    for arr, kind, jo in extras:
        if kind == "mn":
            in_specs.append(pl.BlockSpec((tm, tn), lambda i, j, k, jo=jo: (i, j + jo)))
        else:
            in_specs.append(pl.BlockSpec((1, tn), lambda i, j, k, jo=jo: (0, j + jo)))
        args.append(arr)
    io_alias = {}
    if alias is not None:
        in_specs.append(pl.BlockSpec(memory_space=pl.ANY))
        io_alias = {len(args): alias[1]}
        args.append(alias[0])
    out_specs = [pl.BlockSpec((tm, tn), lambda i, j, k, jo=jo: (i, j + jo)) for _, _, jo in outs]
    out_shape = [jax.ShapeDtypeStruct((M, nc), dt) for nc, dt, _ in outs]
    n_ex, n_out = len(extras), len(outs)
    dn = (((0 if ta else 1,), (1 if tb else 0,)), ((), ()))

    def body(*refs):
        ab = refs[: 2 * n_p]
        ex = refs[2 * n_p: 2 * n_p + n_ex]
        o0 = 2 * n_p + n_ex + (1 if alias is not None else 0)
        out_refs = refs[o0: o0 + n_out]
        acc_refs = refs[o0 + n_out:]

        def dots():
            res = []
            for p in range(n_p):
                a = ab[2 * p][...]
                b = ab[2 * p + 1][...]
                res.append(lax.dot_general(a.astype(BF16), b.astype(BF16), dn, preferred_element_type=F32))
            if sum_pairs:
                tot = res[0]
                for r in res[1:]:
                    tot = tot + r
                res = [tot]
            return res

        def finish(accs):
            tiles = epilogue(accs, [e[...] for e in ex])
            for r, t in zip(out_refs, tiles, strict=True):
                r[...] = t.astype(r.dtype)

        if nk == 1:
            finish(dots())
        else:
            k = pl.program_id(2)
            d = dots()

            @pl.when(k == 0)
            def _():
                for r, v in zip(acc_refs, d, strict=True):
                    r[...] = v

            @pl.when(k > 0)
            def _():
                for r, v in zip(acc_refs, d, strict=True):
                    r[...] += v

            @pl.when(k == nk - 1)
            def _():
                finish([r[...] for r in acc_refs])

    scratch = [pltpu.VMEM((tm, tn), F32) for _ in range(n_acc)] if nk > 1 else []
    res = pl.pallas_call(
        body, name=name, grid=(ni, nj, nk), in_specs=in_specs, out_specs=out_specs, out_shape=out_shape,
        scratch_shapes=scratch, input_output_aliases=io_alias,
        compiler_params=_cparams(("parallel", "parallel", "arbitrary")),
    )(*args)
    return res


def _wgrad(name, a, b, *, M, N, T, tm, tn, tk, into=None, joff=0, ncols=None):
    ncols = N if ncols is None else ncols
    outs = [(ncols, F32, joff), (ncols, BF16, joff)]
    epi = lambda accs, ex: [accs[0], accs[0]]
    if into is None:
        return _mm(name, [(a, b, 0, 0, 0)], ta=True, M=M, N=N, tm=tm, tn=tn, tk=tk, nk=T // tk, epilogue=epi, outs=outs)
    f32_arr, bf_arr = into
    return _mm2alias(name, a, b, M=M, N=N, T=T, tm=tm, tn=tn, tk=tk, joff=joff, f32_arr=f32_arr, bf_arr=bf_arr)


def _mm2alias(name, a, b, *, M, N, T, tm, tn, tk, joff, f32_arr, bf_arr):
    ni, nj, nk = M // tm, N // tn, T // tk
    dn = (((0,), (0,)), ((), ()))

    def body(a_ref, b_ref, _f, _h, of_ref, oh_ref, acc_ref):
        k = pl.program_id(2)
        d = lax.dot_general(a_ref[...].astype(BF16), b_ref[...].astype(BF16), dn, preferred_element_type=F32)

        @pl.when(k == 0)
        def _():
            acc_ref[...] = d

        @pl.when(k > 0)
        def _():
            acc_ref[...] += d

        @pl.when(k == nk - 1)
        def _():
            of_ref[...] = acc_ref[...]
            oh_ref[...] = acc_ref[...].astype(BF16)

    ospec = pl.BlockSpec((tm, tn), lambda i, j, k: (i, j + joff))
    return pl.pallas_call(
        body, name=name, grid=(ni, nj, nk),
        in_specs=[pl.BlockSpec((tk, tm), lambda i, j, k: (k, i)), pl.BlockSpec((tk, tn), lambda i, j, k: (k, j)),
                  pl.BlockSpec(memory_space=pl.ANY), pl.BlockSpec(memory_space=pl.ANY)],
        out_specs=[ospec, ospec],
        out_shape=[jax.ShapeDtypeStruct(f32_arr.shape, F32), jax.ShapeDtypeStruct(bf_arr.shape, BF16)],
        scratch_shapes=[pltpu.VMEM((tm, tn), F32)], input_output_aliases={2: 0, 3: 1},
        compiler_params=_cparams(("parallel", "parallel", "arbitrary")),
    )(a, b, f32_arr, bf_arr)


def _rows(tm, c, cb=0):
    return pl.BlockSpec((tm, c), lambda i, cb=cb: (i, cb))


def _whole(shape):
    nd = len(shape)
    return pl.BlockSpec(shape, lambda i, nd=nd: (0,) * nd)


def _cast_bf16(name, x, tm):
    t, d = x.shape

    def body(x_ref, o_ref):
        o_ref[...] = x_ref[...].astype(BF16)

    return pl.pallas_call(body, name=name, grid=(t // tm,), in_specs=[_rows(tm, d)], out_specs=_rows(tm, d),
                          out_shape=jax.ShapeDtypeStruct((t, d), BF16), compiler_params=_cparams(("parallel",)))(x)


def _ln_fwd(name, xres, f, g, b, cf, tm):
    t, d = xres.shape

    def body(x_ref, f_ref, g_ref, b_ref, y_ref, yb_ref, xh_ref, rs_ref):
        r = ALPHA * x_ref[...] + cf * f_ref[...]
        xhat, rstd = _row_stats(r)
        y = xhat * g_ref[...] + b_ref[...]
        y_ref[...] = y
        yb_ref[...] = y.astype(BF16)
        xh_ref[...] = xhat
        rs_ref[...] = rstd

    return pl.pallas_call(
        body, name=name, grid=(t // tm,), in_specs=[_rows(tm, d), _rows(tm, d), _whole((1, d)), _whole((1, d))],
        out_specs=[_rows(tm, d), _rows(tm, d), _rows(tm, d), _rows(tm, 1)],
        out_shape=[jax.ShapeDtypeStruct((t, d), F32), jax.ShapeDtypeStruct((t, d), BF16),
                   jax.ShapeDtypeStruct((t, d), F32), jax.ShapeDtypeStruct((t, 1), F32)],
        compiler_params=_cparams(("parallel",)))(xres, f, g, b)


def _ln_bwd(name, dy, xhat, rstd, g, scale, tm):
    t, d = dy.shape

    def body(dy_ref, xh_ref, rs_ref, g_ref, dr_ref, drb_ref, dg_ref, db_ref):
        i = pl.program_id(0)
        dy_v, xh = dy_ref[...], xh_ref[...]
        dr = _ln_bwd_rows(dy_v, xh, rs_ref[...], g_ref[...])
        dr_ref[...] = dr
        drb_ref[...] = (scale * dr).astype(BF16)

        @pl.when(i == 0)
        def _():
            dg_ref[...] = jnp.zeros_like(dg_ref)
            db_ref[...] = jnp.zeros_like(db_ref)

        dg_ref[...] += _colsum(dy_v * xh)
        db_ref[...] += _colsum(dy_v)

    return pl.pallas_call(
        body, name=name, grid=(t // tm,), in_specs=[_rows(tm, d), _rows(tm, d), _rows(tm, 1), _whole((1, d))],
        out_specs=[_rows(tm, d), _rows(tm, d), _whole((1, d)), _whole((1, d))],
        out_shape=[jax.ShapeDtypeStruct((t, d), F32), jax.ShapeDtypeStruct((t, d), BF16),
                   jax.ShapeDtypeStruct((1, d), F32), jax.ShapeDtypeStruct((1, d), F32)],
        compiler_params=_cparams(("arbitrary",)))(dy, xhat, rstd, g)


def _ln3_loss(name, xres, f, g, b, target, tm):
    t, d = xres.shape

    def body(x_ref, f_ref, g_ref, b_ref, tg_ref, dr_ref, drb_ref, loss_ref, dg_ref, db_ref):
        i = pl.program_id(0)
        r = ALPHA * x_ref[...] + 0.5 * f_ref[...]
        xhat, rstd = _row_stats(r)
        gv = g_ref[...]
        y = xhat * gv + b_ref[...]
        err = y - tg_ref[...]
        dy = err * (1.0 / d)
        dr = _ln_bwd_rows(dy, xhat, rstd, gv)
        dr_ref[...] = dr
        drb_ref[...] = (0.5 * dr).astype(BF16)
        part = 0.5 * jnp.sum(jnp.mean(err * err, axis=-1, keepdims=True), axis=0, keepdims=True)

        @pl.when(i == 0)
        def _():
            loss_ref[...] = jnp.zeros_like(loss_ref)
            dg_ref[...] = jnp.zeros_like(dg_ref)
            db_ref[...] = jnp.zeros_like(db_ref)

        loss_ref[...] += jnp.broadcast_to(part, loss_ref.shape)
        dg_ref[...] += _colsum(dy * xhat)
        db_ref[...] += _colsum(dy)

    return pl.pallas_call(
        body, name=name, grid=(t // tm,),
        in_specs=[_rows(tm, d), _rows(tm, d), _whole((1, d)), _whole((1, d)), _rows(tm, d)],
        out_specs=[_rows(tm, d), _rows(tm, d), _whole((8, 128)), _whole((1, d)), _whole((1, d))],
        out_shape=[jax.ShapeDtypeStruct((t, d), F32), jax.ShapeDtypeStruct((t, d), BF16),
                   jax.ShapeDtypeStruct((8, 128), F32), jax.ShapeDtypeStruct((1, d), F32),
                   jax.ShapeDtypeStruct((1, d), F32)],
        compiler_params=_cparams(("arbitrary",)))(xres, f, g, b, target)


def _colsum_rows(name, x, tm):
    t, d = x.shape

    def body(x_ref, o_ref):
        @pl.when(pl.program_id(0) == 0)
        def _():
            o_ref[...] = jnp.zeros_like(o_ref)

        o_ref[...] += _colsum(x_ref[...].astype(F32))

    return pl.pallas_call(body, name=name, grid=(t // tm,), in_specs=[_rows(tm, d)], out_specs=_whole((1, d)),
                          out_shape=jax.ShapeDtypeStruct((1, d), F32), compiler_params=_cparams(("arbitrary",)))(x)


def _sgu_mask():
    sh = SGU_CHUNK.bit_length() - 1
    r = lax.shift_right_logical(lax.broadcasted_iota(jnp.int32, (SGU_BLOCK, SGU_BLOCK), 0), sh)
    c = lax.shift_right_logical(lax.broadcasted_iota(jnp.int32, (SGU_BLOCK, SGU_BLOCK), 1), sh)
    return c <= r


def _sgu_fwd(name, p, lng, lnb, wm, bst, tm):
    t = p.shape[0]
    n_grp, w = wm.shape[0], lng.shape[1]
    hd = w // n_grp
    nblk = tm // SGU_BLOCK

    def body(uv_ref, h_ref, g_ref, b_ref, wm_ref, bs_ref, sa_ref, z_ref, vn_s):
        xhat, _ = _row_stats(_gelu(uv_ref[:, w:]))
        vn_s[...] = (xhat * g_ref[...] + b_ref[...]).astype(BF16)
        mask = _sgu_mask()
        for h in range(n_grp):
            wh = jnp.where(mask, wm_ref[h], 0.0).astype(BF16)
            bcol = bs_ref[:, h:h + 1]
            cs = slice(h * hd, (h + 1) * hd)
            for n in range(nblk):
                rs = slice(n * SGU_BLOCK, (n + 1) * SGU_BLOCK)
                s = jnp.dot(wh, vn_s[rs, cs], preferred_element_type=F32) + bcol
                sa_ref[rs, cs] = (_gelu(uv_ref[rs, cs]) * s).astype(BF16)
        z_ref[...] = h_ref[:, :w] * jax.nn.sigmoid(h_ref[:, w:])

    return pl.pallas_call(
        body, name=name, grid=(t // tm,),
        in_specs=[_rows(tm, 2 * w, 0), _rows(tm, 2 * w, 1), _whole((1, w)), _whole((1, w)), _whole(wm.shape),
                  _whole(bst.shape)],
        out_specs=[_rows(tm, w), _rows(tm, w)],
        out_shape=[jax.ShapeDtypeStruct((t, w), BF16), jax.ShapeDtypeStruct((t, w), F32)],
        scratch_shapes=[pltpu.VMEM((tm, w), BF16)], compiler_params=_cparams(("parallel",)))(p, p, lng, lnb, wm, bst)


def _sgu_bwd(name, p, dsa, lng, lnb, wm, bst, tm):
    t = p.shape[0]
    n_grp, w = wm.shape[0], lng.shape[1]
    hd = w // n_grp
    nblk = tm // SGU_BLOCK

    def body(uv_ref, dsa_ref, g_ref, b_ref, wm_ref, bs_ref, dp_ref, dwm_ref, dbs_ref, dg_ref, db_ref,
             vn_s, ug_s, dvn_s, dug_s):
        i = pl.program_id(0)

        @pl.when(i == 0)
        def _():
            dwm_ref[...] = jnp.zeros_like(dwm_ref)
            dbs_ref[...] = jnp.zeros_like(dbs_ref)
            dg_ref[...] = jnp.zeros_like(dg_ref)
            db_ref[...] = jnp.zeros_like(db_ref)

        ug, dgelu_u = _gelu_and_grad(uv_ref[:, :w])
        ug_s[...] = ug
        vg, dgelu_v = _gelu_and_grad(uv_ref[:, w:])
        xhat, rstd = _row_stats(vg)
        gv = g_ref[...]
        vn_s[...] = (xhat * gv + b_ref[...]).astype(BF16)
        mask = _sgu_mask()
        for h in range(n_grp):
            wh = jnp.where(mask, wm_ref[h], 0.0).astype(BF16)
            bcol = bs_ref[:, h:h + 1]
            cs = slice(h * hd, (h + 1) * hd)
            dw_h = jnp.zeros((SGU_BLOCK, SGU_BLOCK), F32)
            dbs_h = jnp.zeros((SGU_BLOCK, 1), F32)
            for n in range(nblk):
                rs = slice(n * SGU_BLOCK, (n + 1) * SGU_BLOCK)
                vblk = vn_s[rs, cs]
                s = jnp.dot(wh, vblk, preferred_element_type=F32) + bcol
                dsa_blk = dsa_ref[rs, cs]
                dug_s[rs, cs] = dsa_blk * s
                ds = dsa_blk * ug_s[rs, cs]
                dsb = ds.astype(BF16)
                dvn_s[rs, cs] = lax.dot_general(wh, dsb, (((0,), (0,)), ((), ())), preferred_element_type=F32)
                dw_h = dw_h + lax.dot_general(dsb, vblk, (((1,), (1,)), ((), ())), preferred_element_type=F32)
                dbs_h = dbs_h + jnp.sum(ds, axis=1, keepdims=True)
            dwm_ref[h] += jnp.where(mask, dw_h, 0.0)
            dbs_ref[:, h:h + 1] += dbs_h
        dvn = dvn_s[...]
        dg_ref[...] += _colsum(dvn * xhat)
        db_ref[...] += _colsum(dvn)
        dvg = _ln_bwd_rows(dvn, xhat, rstd, gv)
        dp_ref[:, :w] = (dug_s[...] * dgelu_u).astype(BF16)
        dp_ref[:, w:] = (dvg * dgelu_v).astype(BF16)

    return pl.pallas_call(
        body, name=name, grid=(t // tm,),
        in_specs=[_rows(tm, 2 * w, 0), _rows(tm, w), _whole((1, w)), _whole((1, w)), _whole(wm.shape), _whole(bst.shape)],
        out_specs=[_rows(tm, 2 * w), _whole(wm.shape), _whole(bst.shape), _whole((1, w)), _whole((1, w))],
        out_shape=[jax.ShapeDtypeStruct((t, 2 * w), BF16), jax.ShapeDtypeStruct(wm.shape, F32),
                   jax.ShapeDtypeStruct(bst.shape, F32), jax.ShapeDtypeStruct((1, w), F32), jax.ShapeDtypeStruct((1, w), F32)],
        scratch_shapes=[pltpu.VMEM((tm, w), BF16), pltpu.VMEM((tm, w), F32), pltpu.VMEM((tm, w), F32),
                        pltpu.VMEM((tm, w), F32)],
        compiler_params=_cparams(("arbitrary",)))(p, dsa, lng, lnb, wm, bst)


def _halo_prev(tm, c):
    return pl.BlockSpec((HALO, c), lambda i: (jnp.maximum(i * (tm // HALO) - 1, 0), 0))


def _halo_next(tm, c, t):
    last = t // HALO - 1
    return pl.BlockSpec((HALO, c), lambda i: (jnp.minimum((i + 1) * (tm // HALO), last), 0))


def _conv_fwd(name, z, wdw, kw, bdw, lng, lnb, tm):
    t, c = z.shape
    lead = HALO - (kw - 1)

    def body(zp_ref, z_ref, w_ref, bdw_ref, g_ref, b_ref, zc_ref, sb_ref, buf):
        i = pl.program_id(0)
        buf[:HALO, :] = jnp.where(i > 0, zp_ref[...], 0.0)
        buf[HALO:, :] = z_ref[...]
        acc = jnp.zeros((tm, c), F32) + bdw_ref[...]
        for k in range(kw):
            acc = acc + w_ref[k:k + 1, :] * buf[lead + k: lead + k + tm, :]
        zc_ref[...] = acc
        xhat, _ = _row_stats(acc)
        zn = xhat * g_ref[...] + b_ref[...]
        sb_ref[...] = (zn * jax.nn.sigmoid(zn)).astype(BF16)

    return pl.pallas_call(
        body, name=name, grid=(t // tm,),
        in_specs=[_halo_prev(tm, c), _rows(tm, c), _whole(wdw.shape), _whole((1, c)), _whole((1, c)), _whole((1, c))],
        out_specs=[_rows(tm, c), _rows(tm, c)],
        out_shape=[jax.ShapeDtypeStruct((t, c), F32), jax.ShapeDtypeStruct((t, c), BF16)],
        scratch_shapes=[pltpu.VMEM((tm + HALO, c), F32)], compiler_params=_cparams(("parallel",)))(z, z, wdw, bdw, lng, lnb)


def _conv_ln_bwd(name, dsb, zc, lng, lnb, tm):
    t, c = zc.shape

    def body(dsb_ref, zc_ref, g_ref, b_ref, dzc_ref, dg_ref, db_ref, dbdw_ref):
        i = pl.program_id(0)
        xhat, rstd = _row_stats(zc_ref[...])
        gv = g_ref[...]
        zn = xhat * gv + b_ref[...]
        dzn = dsb_ref[...] * _silu_grad(zn)
        dzc = _ln_bwd_rows(dzn, xhat, rstd, gv)
        dzc_ref[...] = dzc

        @pl.when(i == 0)
        def _():
            dg_ref[...] = jnp.zeros_like(dg_ref)
            db_ref[...] = jnp.zeros_like(db_ref)
            dbdw_ref[...] = jnp.zeros_like(dbdw_ref)

        dg_ref[...] += _colsum(dzn * xhat)
        db_ref[...] += _colsum(dzn)
        dbdw_ref[...] += _colsum(dzc)

    return pl.pallas_call(
        body, name=name, grid=(t // tm,), in_specs=[_rows(tm, c), _rows(tm, c), _whole((1, c)), _whole((1, c))],
        out_specs=[_rows(tm, c), _whole((1, c)), _whole((1, c)), _whole((1, c))],
        out_shape=[jax.ShapeDtypeStruct((t, c), F32)] + [jax.ShapeDtypeStruct((1, c), F32)] * 3,
        compiler_params=_cparams(("arbitrary",)))(dsb, zc, lng, lnb)


def _conv_bwd(name, dzc, z, p, wdw, kw, tm):
    t, c = z.shape
    lead = HALO - (kw - 1)
    n_i = t // tm

    def body(dzc_ref, dzn_ref, zp_ref, z_ref, h_ref, w_ref, dp_ref, dw_ref, dbuf, zbuf):
        i = pl.program_id(0)

        @pl.when(i == 0)
        def _():
            dw_ref[...] = jnp.zeros_like(dw_ref)

        dzc = dzc_ref[...]
        dbuf[:tm, :] = dzc
        dbuf[tm:, :] = jnp.where(i < n_i - 1, dzn_ref[...], 0.0)
        zbuf[:HALO, :] = jnp.where(i > 0, zp_ref[...], 0.0)
        zbuf[HALO:, :] = z_ref[...]
        dz = jnp.zeros((tm, c), F32)
        for k in range(kw):
            dz = dz + w_ref[k:k + 1, :] * dbuf[kw - 1 - k: kw - 1 - k + tm, :]
            dw_ref[k:k + 1, :] += _colsum(dzc * zbuf[lead + k: lead + k + tm, :])
        a, g = h_ref[:, :c], h_ref[:, c:]
        sg = jax.nn.sigmoid(g)
        dp_ref[:, :c] = (dz * sg).astype(BF16)
        dp_ref[:, c:] = (dz * a * (sg * (1.0 - sg))).astype(BF16)

    return pl.pallas_call(
        body, name=name, grid=(n_i,),
        in_specs=[_rows(tm, c), _halo_next(tm, c, t), _halo_prev(tm, c), _rows(tm, c), _rows(tm, 2 * c, 1), _whole(wdw.shape)],
        out_specs=[_rows(tm, 2 * c), _whole((HALO, c))],
        out_shape=[jax.ShapeDtypeStruct((t, 2 * c), BF16), jax.ShapeDtypeStruct((HALO, c), F32)],
        scratch_shapes=[pltpu.VMEM((tm + HALO, c), F32), pltpu.VMEM((tm + HALO, c), F32)],
        compiler_params=_cparams(("arbitrary",)))(dzc, dzc, z, z, p, wdw)


def _adamw(name, w, g, m, v, tr):
    r, c = w.shape
    c1 = 1.0 - ADAM_B1 ** ADAM_STEP
    c2 = 1.0 - ADAM_B2 ** ADAM_STEP

    def body(w_ref, g_ref, m_ref, v_ref, d_ref, mo_ref, vo_ref):
        gv = g_ref[...]
        mn = ADAM_B1 * m_ref[...] + (1.0 - ADAM_B1) * gv
        vn = ADAM_B2 * v_ref[...] + (1.0 - ADAM_B2) * (gv * gv)
        d_ref[...] = -ADAM_LR * ((mn / c1) / (jnp.sqrt(vn / c2) + ADAM_EPS) + ADAM_WD * w_ref[...])
        mo_ref[...] = mn
        vo_ref[...] = vn

    spec = _rows(tr, c)
    return pl.pallas_call(body, name=name, grid=(r // tr,), in_specs=[spec] * 4, out_specs=[spec] * 3,
                          out_shape=[jax.ShapeDtypeStruct((r, c), F32)] * 3, compiler_params=_cparams(("parallel",)))(w, g, m, v)


def _pair_sum(name, mine, recv, hc, *, out_rows, tr, row_map):
    c_ = mine.shape[1]

    def body(hc_ref, a_ref, b_ref, of_ref, oh_ref):
        s = a_ref[...] + b_ref[...].astype(F32)
        of_ref[...] = s
        oh_ref[...] = s.astype(BF16)

    ispec = pl.BlockSpec((tr, c_), lambda i, hc_ref: (row_map(i, hc_ref[0]), 0))
    ospec = pl.BlockSpec((tr, c_), lambda i, hc_ref: (i, 0))
    return pl.pallas_call(
        body, name=name,
        grid_spec=pltpu.PrefetchScalarGridSpec(num_scalar_prefetch=1, grid=(out_rows // tr,), in_specs=[ispec, ispec],
                                               out_specs=[ospec, ospec]),
        out_shape=[jax.ShapeDtypeStruct((out_rows, c_), F32), jax.ShapeDtypeStruct((out_rows, c_), BF16)],
        compiler_params=_cparams(("parallel",)))(hc, mine, recv)


def _final_sum(name, cs32, recv2, qc, *, part_shape, out_shape, tr, in_map, out_map):
    pr, pc = part_shape

    def body(qc_ref, a_ref, r_ref, o_ref):
        o_ref[...] = ((a_ref[...] + r_ref[0].astype(F32)) + r_ref[1].astype(F32)) + r_ref[2].astype(F32)

    return pl.pallas_call(
        body, name=name,
        grid_spec=pltpu.PrefetchScalarGridSpec(
            num_scalar_prefetch=1, grid=(pr // tr,),
            in_specs=[pl.BlockSpec((tr, pc), lambda i, qc_ref: in_map(i, qc_ref[0], qc_ref[1])),
                      pl.BlockSpec((3, tr, pc), lambda i, qc_ref: (0, i, 0))],
            out_specs=pl.BlockSpec((tr, pc), lambda i, qc_ref: out_map(i, qc_ref[0], qc_ref[1]))),
        out_shape=jax.ShapeDtypeStruct(out_shape, F32), compiler_params=_cparams(("parallel",)))(qc, cs32, recv2)


def _sum8(name, slots):
    def body(s_ref, o_ref):
        acc = s_ref[0]
        for d in range(1, 8):
            acc = acc + s_ref[d]
        o_ref[...] = acc

    return pl.pallas_call(body, name=name, out_shape=jax.ShapeDtypeStruct(slots.shape[1:], F32),
                          in_specs=[pl.BlockSpec(memory_space=pltpu.VMEM)], out_specs=pl.BlockSpec(memory_space=pltpu.VMEM),
                          compiler_params=pltpu.CompilerParams(vmem_limit_bytes=VMEM_LIMIT))(slots)


def _exchange(name, ro, rw, new, plan, n_remote, n_local):
    n_ro, n_rw, n_new = len(ro), len(rw), len(new)

    def body(*refs):
        ro_refs = refs[:n_ro]
        o0 = n_ro + n_rw
        rw_refs = refs[o0: o0 + n_rw]
        new_refs = refs[o0 + n_rw: o0 + n_rw + n_new]
        send_sems, recv_sems, loc_sems = refs[o0 + n_rw + n_new:]
        x, y, c = lax.axis_index("x"), lax.axis_index("y"), lax.axis_index("c")
        remote, local = plan(ro_refs, rw_refs, new_refs, x, y, c)
        assert len(remote) == n_remote and len(local) == n_local
        lcs = [pltpu.make_async_copy(s, d, loc_sems.at[k]) for k, (s, d) in enumerate(local)]
        rcs = [pltpu.make_async_remote_copy(src_ref=s, dst_ref=d, send_sem=send_sems.at[k], recv_sem=recv_sems.at[k],
                                            device_id=dev, device_id_type=MESH) for k, (s, d, dev) in enumerate(remote)]
        for cp in lcs + rcs:
            cp.start()
        for cp in rcs:
            cp.wait_send()
        for cp in rcs:
            cp.wait_recv()
        for cp in lcs:
            cp.wait()

    any_spec = pl.BlockSpec(memory_space=pl.ANY)
    out_shape = [jax.ShapeDtypeStruct(a.shape, a.dtype) for a in rw] + list(new)
    res = pl.pallas_call(
        body, name=name, in_specs=[any_spec] * (n_ro + n_rw), out_specs=[any_spec] * (n_rw + n_new), out_shape=out_shape,
        input_output_aliases={n_ro + k: k for k in range(n_rw)},
        scratch_shapes=[pltpu.SemaphoreType.DMA((n_remote,)), pltpu.SemaphoreType.DMA((n_remote,)),
                        pltpu.SemaphoreType.DMA((max(n_local, 1),))],
    )(*ro, *rw)
    return list(res)


def _chips(x, y):
    return [(1 - x, y), (x, 1 - y), (1 - x, 1 - y)]


class _Big:
    def __init__(self, name, w, m, v, ax):
        self.name, self.w, self.m, self.v, self.ax = name, w, m, v, ax
        sr, sc = w.shape
        self.R, self.C = (sr * N_CHIPS, sc) if ax == 0 else (sr, sc * N_CHIPS)
        self.sr, self.sc = sr, sc
        self.hr = sr // 2

    def slot(self, ref, q, h=None):
        if self.ax == 1:
            cols = pl.ds(pl.multiple_of(q * self.sc, 128), self.sc)
            return ref.at[:, cols] if h is None else ref.at[pl.ds(pl.multiple_of(h * self.hr, 16), self.hr), cols]
        if h is None:
            return ref.at[pl.ds(pl.multiple_of(q * self.sr, 16), self.sr), :]
        return ref.at[pl.ds(pl.multiple_of(q * self.sr + h * self.hr, 16), self.hr), :]

    def half(self, ref, h):
        return ref.at[pl.ds(pl.multiple_of(h * self.hr, 16), self.hr), :]

    def part(self, ref, q):
        if self.ax == 1:
            return ref.at[:, pl.ds(pl.multiple_of(q * self.sc, 128), self.sc)]
        return ref.at[pl.ds(pl.multiple_of(q * self.hr, 16), self.hr), :]

    @property
    def part_shape(self):
        return (self.hr, self.sc)


def _gather_weights(bigs, shards16):
    def plan1(ro, rw, new, x, y, c):
        q = 2 * x + y
        remote, local = [], []
        for b, sh, full in zip(bigs, ro, new, strict=True):
            local.append((sh, b.slot(full, q)))
            for cx, cy in _chips(x, y):
                remote.append((b.half(sh, c), b.slot(full, q, c), (cx, cy, c)))
        return remote, local

    fulls = _exchange("gather_ici", shards16, [], [jax.ShapeDtypeStruct((b.R, b.C), BF16) for b in bigs], plan1,
                      3 * len(bigs), len(bigs))

    def plan2(ro, rw, new, x, y, c):
        remote = []
        for b, full in zip(bigs, rw, strict=True):
            for cx, cy in _chips(x, y):
                piece = b.slot(full, 2 * cx + cy, c)
                remote.append((piece, piece, (x, y, 1 - c)))
        return remote, []

    return _exchange("gather_d2d", [], fulls, [], plan2, 3 * len(bigs), 0)


def _reduce_scatter(tag, bigs, g32s, g16s, hc, qc):
    n = len(bigs)

    def plan1(ro, rw, new, x, y, c):
        remote = []
        for b, g16, rc in zip(bigs, ro, new, strict=True):
            if b.ax == 1:
                rows = pl.ds(pl.multiple_of((1 - c) * b.hr, 16), b.hr)
                remote.append((g16.at[rows, :], rc.at[rows, :], (x, y, 1 - c)))
            else:
                for q in range(N_CHIPS):
                    remote.append((b.slot(g16, q, 1 - c), b.slot(rc, q, 1 - c), (x, y, 1 - c)))
        return remote, []

    n1 = sum(1 if b.ax == 1 else N_CHIPS for b in bigs)
    recv1 = _exchange(f"rs_pair_{tag}", g16s, [], [jax.ShapeDtypeStruct((b.R, b.C), BF16) for b in bigs], plan1, n1, 0)

    cs32s, cs16s = [], []
    for b, g32, rc in zip(bigs, g32s, recv1, strict=True):
        tr = _tile_rows(b.hr, b.C)
        if b.ax == 1:
            nb = b.hr // tr
            row_map = lambda i, c, nb=nb: c * nb + i
        else:
            nb = b.hr // tr
            row_map = lambda i, c, nb=nb: ((i // nb) * 2 + c) * nb + i % nb
        a32, a16 = _pair_sum(f"rs_pairsum_{tag}_{b.name}", g32, rc, hc, out_rows=b.R // 2, tr=tr, row_map=row_map)
        cs32s.append(a32)
        cs16s.append(a16)

    def plan2(ro, rw, new, x, y, c):
        remote = []
        for b, cs16, r2 in zip(bigs, ro, new, strict=True):
            for k, (cx, cy) in enumerate(_chips(x, y)):
                remote.append((b.part(cs16, 2 * cx + cy), r2.at[k], (cx, cy, c)))
        return remote, []

    recv2 = _exchange(f"rs_ici_{tag}", cs16s, [], [jax.ShapeDtypeStruct((3,) + b.part_shape, BF16) for b in bigs], plan2,
                      3 * n, 0)

    halves = []
    for b, cs32, r2 in zip(bigs, cs32s, recv2, strict=True):
        tr = _tile_rows(b.hr, b.sc)
        nb = b.hr // tr
        if b.ax == 1:
            in_map = lambda i, q, c: (i, q)
        else:
            in_map = lambda i, q, c, nb=nb: (q * nb + i, 0)
        out_map = lambda i, q, c, nb=nb: (c * nb + i, 0)
        halves.append(_final_sum(f"rs_final_{tag}_{b.name}", cs32, r2, qc, part_shape=b.part_shape,
                                 out_shape=(b.sr, b.sc), tr=tr, in_map=in_map, out_map=out_map))

    def plan3(ro, rw, new, x, y, c):
        remote = []
        for b, g in zip(bigs, rw, strict=True):
            remote.append((b.half(g, c), b.half(g, c), (x, y, 1 - c)))
        return remote, []

    return _exchange(f"rs_share_{tag}", [], halves, [], plan3, n, 0)


def _allgather_small(packed):
    nr = packed.shape[0]

    def plan(ro, rw, new, x, y, c):
        me = 4 * x + 2 * y + c
        remote = []
        for fx in (0, 1):
            for fy in (0, 1):
                for fc in (0, 1):
                    if fx or fy or fc:
                        dev = (1 - x if fx else x, 1 - y if fy else y, 1 - c if fc else c)
                        remote.append((ro[0], new[0].at[me], dev))
        return remote, [(ro[0], new[0].at[me])]

    return _exchange("small_allgather", [packed], [], [jax.ShapeDtypeStruct((8, nr, 128), F32)], plan, 7, 1)[0]


def _pick(n, want):
    if n <= want:
        return n
    for t in range(want, 15, -16):
        if t % 16 == 0 and n % t == 0:
            return t
    raise ValueError(f"no tile for {n} (want {want})")


def _tile_rows(nrows, ncols, budget=2 * 1024 * 1024):
    return _pick(nrows, max(16, (budget // (4 * ncols)) // 16 * 16))


def _pick128(n, want):
    if n <= want:
        return n
    for t in range(want, 127, -128):
        if n % t == 0:
            return t
    raise ValueError(f"no lane tile for {n} (want {want})")


def _pack_rows(parts):
    out, spans, r0 = [], [], 0
    for p in parts:
        flat = p.reshape(-1).astype(F32)
        n = flat.shape[0]
        rows = -(-n // 1024) * 8
        flat = jnp.pad(flat, (0, rows * 128 - n))
        out.append(flat.reshape(rows, 128))
        spans.append((r0, rows, n))
        r0 += rows
    return jnp.concatenate(out, axis=0), spans


def _unpack_rows(packed, spans, shapes):
    res = []
    for (r0, rows, n), shp in zip(spans, shapes, strict=True):
        res.append(packed[r0:r0 + rows].reshape(-1)[:n].reshape(shp))
    return res


def _ffn_fwd(tag, xb, wgu, wd, T, D, F):
    tm = _pick(T, 1024)
    tn = _pick128(F, 512)
    nf = F // tn

    def epi_up(accs, ex):
        g, u = accs
        return [g, u, (g * jax.nn.sigmoid(g)) * u]

    gte, up, act = _mm(f"{tag}_up", [(xb, wgu, 0, 0, 0), (xb, wgu, 0, 0, nf)], M=T, N=F, tm=tm, tn=tn, tk=D, nk=1,
                       epilogue=epi_up, outs=[(F, BF16, 0)] * 3, sum_pairs=False)
    tnd = _pick128(D, 512)
    (fo,) = _mm(f"{tag}_down", [(act, wd, 0, 0, 0)], M=T, N=D, tm=tm, tn=tnd, tk=F, nk=1,
                epilogue=lambda accs, ex: [accs[0]], outs=[(D, F32, 0)])
    return gte, up, act, fo


def _ffn_bwd(tag, dr, drb_half, xb, gte, up, act, wgu, wd, T, D, F):
    tm = _pick(T, 1024)
    tn = _pick128(F, 512)
    nf = F // tn

    def epi_da(accs, ex):
        da = accs[0]
        g, u = ex[0].astype(F32), ex[1].astype(F32)
        s = jax.nn.sigmoid(g)
        return [da * u * (s * (1.0 + g * (1.0 - s))), da * (g * s)]

    dg, du = _mm(f"{tag}_dact", [(drb_half, wd, 0, 0, 0)], tb=True, M=T, N=F, tm=tm, tn=tn, tk=D, nk=1, epilogue=epi_da,
                 outs=[(F, BF16, 0)] * 2, extras=[(gte, "mn", 0), (up, "mn", 0)])
    tk = _pick(T, 1024)
    dwd32, dwd16 = _wgrad(f"{tag}_dwdown", act, drb_half, M=F, N=D, T=T, tm=_pick128(F, 512), tn=D, tk=tk)
    tkf = _pick128(F, 512)
    (dx,) = _mm(f"{tag}_dx", [(dg, wgu, 0, 0, 0), (du, wgu, 0, F // tkf, 0)], tb=True, M=T, N=D, tm=_pick(T, 512), tn=D, tk=tkf,
                nk=F // tkf, epilogue=lambda accs, ex: [accs[0] + ALPHA * ex[0]], outs=[(D, F32, 0)], extras=[(dr, "mn", 0)])
    dwgu32, dwgu16 = _wgrad(f"{tag}_dwgate", xb, dg, M=D, N=F, T=T, tm=D, tn=tn, tk=tk, ncols=2 * F)
    dwgu32, dwgu16 = _wgrad(f"{tag}_dwup", xb, du, M=D, N=F, T=T, tm=D, tn=tn, tk=tk, into=(dwgu32, dwgu16), joff=nf)
    return dx, (dwgu32, dwgu16), (dwd32, dwd16)


def kernel(x, ffn1_w_gu, ffn1_w_down, ln1_g, ln1_b, w_in, b_in, sgu_ln_g, sgu_ln_b, sgu_w_s, sgu_b_s, w_a_proj, conv_w_dw, conv_b_dw, conv_ln_g, conv_ln_b, w_b_proj, w_out, ln2_g, ln2_b, ffn2_w_gu, ffn2_w_down, ln3_g, ln3_b, loss_target, m_ffn1_w_gu, m_ffn1_w_down, m_ln1_g, m_ln1_b, m_w_in, m_b_in, m_sgu_ln_g, m_sgu_ln_b, m_sgu_w_s, m_sgu_b_s, m_w_a_proj, m_conv_w_dw, m_conv_b_dw, m_conv_ln_g, m_conv_ln_b, m_w_b_proj, m_w_out, m_ln2_g, m_ln2_b, m_ffn2_w_gu, m_ffn2_w_down, m_ln3_g, m_ln3_b, v_ffn1_w_gu, v_ffn1_w_down, v_ln1_g, v_ln1_b, v_w_in, v_b_in, v_sgu_ln_g, v_sgu_ln_b, v_sgu_w_s, v_sgu_b_s, v_w_a_proj, v_conv_w_dw, v_conv_b_dw, v_conv_ln_g, v_conv_ln_b, v_w_b_proj, v_w_out, v_ln2_g, v_ln2_b, v_ffn2_w_gu, v_ffn2_w_down, v_ln3_g, v_ln3_b):
    args = dict(locals())
    assert x.shape[0] == 1 and ffn1_w_gu.shape[0] == 1
    T, D = x.shape[1], x.shape[2]
    F = ffn1_w_down.shape[1] * N_CHIPS
    W = sgu_ln_g.shape[1]
    KW = conv_w_dw.shape[1]
    assert KW - 1 <= HALO and T % SGU_BLOCK == 0

    mx, my, mc = lax.axis_index("x"), lax.axis_index("y"), lax.axis_index("c")
    q = 2 * mx + my
    hc = jnp.reshape(mc, (1,)).astype(jnp.int32)
    qc = jnp.stack([q, mc]).astype(jnp.int32)

    big_names = [("ffn1_w_gu", 1), ("ffn1_w_down", 0), ("w_in", 1), ("w_a_proj", 1), ("w_b_proj", 1), ("w_out", 0),
                 ("ffn2_w_gu", 1), ("ffn2_w_down", 0)]
    bigs = {n: _Big(n, args[n][0], args["m_" + n][0], args["v_" + n][0], ax) for n, ax in big_names}
    blist = [bigs[n] for n, _ in big_names]

    shards16 = [_cast_bf16(f"cast_{b.name}", b.w, _tile_rows(b.sr, b.sc)) for b in blist]
    fulls = dict(zip([b.name for b in blist], _gather_weights(blist, shards16), strict=True))

    x2d = x[0]
    tgt = loss_target[0]
    tm_r = _pick(T, 256)

    xb = _cast_bf16("cast_x", x2d, tm_r)
    g1, u1, a1, fo1 = _ffn_fwd("ffn1", xb, fulls["ffn1_w_gu"], fulls["ffn1_w_down"], T, D, F)
    x1, x1b, xh1, rs1 = _ln_fwd("ln1", x2d, fo1, ln1_g, ln1_b, 0.5, tm_r)

    tm = _pick(T, 1024)
    tn = _pick128(D, 1024)
    (proj,) = _mm("in_proj", [(x1b, fulls["w_in"], 0, 0, 0)], M=T, N=4 * D, tm=tm, tn=tn, tk=D, nk=1,
                  epilogue=lambda accs, ex: [accs[0] + ex[0]], outs=[(4 * D, F32, 0)], extras=[(b_in, "n", 0)])
    wm = sgu_w_s[0]
    bst = sgu_b_s[0].T
    tm_s = _pick(T, 256)
    sa, z = _sgu_fwd("sgu_fwd", proj, sgu_ln_g, sgu_ln_b, wm, bst, tm_s)
    wdw_full = _allgather_conv_w(conv_w_dw[0])
    zc, sb = _conv_fwd("conv_fwd", z, wdw_full, KW, conv_b_dw, conv_ln_g, conv_ln_b, tm_s)

    def epi_mix(accs, ex):
        ya, yb = accs
        return [jax.nn.sigmoid(ex[0]) * ya + jax.nn.sigmoid(ex[1]) * yb, ya, yb]

    nj = D // tn
    tng = _pick128(D, 512)
    njg = D // tng
    mixin, ya, yb = _mm("branch_proj", [(sa, fulls["w_a_proj"], 0, 0, 0), (sb, fulls["w_b_proj"], 0, 0, 0)], M=T, N=D,
                        tm=tm, tn=tng, tk=W, nk=1, epilogue=epi_mix, outs=[(D, BF16, 0)] * 3, sum_pairs=False,
                        extras=[(proj, "mn", 2 * njg), (proj, "mn", 3 * njg)])
    (mix,) = _mm("out_proj", [(mixin, fulls["w_out"], 0, 0, 0)], M=T, N=D, tm=tm, tn=tn, tk=D, nk=1,
                 epilogue=lambda accs, ex: [accs[0]], outs=[(D, F32, 0)])
    x2, x2b, xh2, rs2 = _ln_fwd("ln2", x1, mix, ln2_g, ln2_b, 1.0, tm_r)
    g2, u2, a2, fo2 = _ffn_fwd("ffn2", x2b, fulls["ffn2_w_gu"], fulls["ffn2_w_down"], T, D, F)
    dr3, dr3h, loss_part, dln3_g, dln3_b = _ln3_loss("ln3_loss", x2, fo2, ln3_g, ln3_b, tgt, tm_r)

    dx2, dwgu2, dwd2 = _ffn_bwd("ffn2", dr3, dr3h, x2b, g2, u2, a2, fulls["ffn2_w_gu"], fulls["ffn2_w_down"], T, D, F)
    red2 = _reduce_scatter("ffn2", [bigs["ffn2_w_gu"], bigs["ffn2_w_down"]], [dwgu2[0], dwd2[0]], [dwgu2[1], dwd2[1]], hc, qc)

    dr2, dr2b, dln2_g, dln2_b = _ln_bwd("ln2_bwd", dx2, xh2, rs2, ln2_g, 1.0, tm_r)

    def epi_dmix(accs, ex):
        dm = accs[0]
        ga, gb = jax.nn.sigmoid(ex[0]), jax.nn.sigmoid(ex[1])
        ya_, yb_ = ex[2].astype(F32), ex[3].astype(F32)
        return [dm * ga, dm * gb, dm * ya_ * (ga * (1.0 - ga)), dm * yb_ * (gb * (1.0 - gb))]

    dya, dyb, dla, dlb = _mm("out_proj_bwd", [(dr2b, fulls["w_out"], 0, 0, 0)], tb=True, M=T, N=D, tm=tm, tn=tng, tk=D, nk=1,
                             epilogue=epi_dmix, outs=[(D, BF16, 0)] * 4,
                             extras=[(proj, "mn", 2 * njg), (proj, "mn", 3 * njg), (ya, "mn", 0), (yb, "mn", 0)])
    tk = _pick(T, 2048)
    dwout = _wgrad("dw_out", mixin, dr2b, M=D, N=D, T=T, tm=tn, tn=tn, tk=tk)
    tnw = _pick128(W, 1024)
    ident = lambda accs, ex: [accs[0]]
    (dsa,) = _mm("a_proj_bwd", [(dya, fulls["w_a_proj"], 0, 0, 0)], tb=True, M=T, N=W, tm=tm, tn=tnw, tk=D, nk=1,
                 epilogue=ident, outs=[(W, F32, 0)])
    (dsb,) = _mm("b_proj_bwd", [(dyb, fulls["w_b_proj"], 0, 0, 0)], tb=True, M=T, N=W, tm=tm, tn=tnw, tk=D, nk=1,
                 epilogue=ident, outs=[(W, F32, 0)])
    dwa = _wgrad("dw_a_proj", sa, dya, M=W, N=D, T=T, tm=tnw, tn=tn, tk=tk)
    dwb = _wgrad("dw_b_proj", sb, dyb, M=W, N=D, T=T, tm=tnw, tn=tn, tk=tk)

    tm_b = _pick(T, 256)
    dpa, dwm, dbst, dsgu_g, dsgu_b = _sgu_bwd("sgu_bwd", proj, dsa, sgu_ln_g, sgu_ln_b, wm, bst, tm_b)
    dzc, dcln_g, dcln_b, dbdw = _conv_ln_bwd("conv_ln_bwd", dsb, zc, conv_ln_g, conv_ln_b, tm_b)
    dpb, dwdw = _conv_bwd("conv_bwd", dzc, z, proj, wdw_full, KW, tm_b)

    dps = [dpa, dpb, dla, dlb]
    db_in = jnp.concatenate([_colsum_rows(f"db_in_{k}", dp, tm_r) for k, dp in enumerate(dps)], axis=1)
    tkd = _pick128(D, 512)
    nkd = D // tkd
    (dx1,) = _mm("in_proj_bwd", [(dp, fulls["w_in"], 0, k * nkd, 0) for k, dp in enumerate(dps)], tb=True, M=T, N=D,
                 tm=_pick(T, 512), tn=D, tk=tkd, nk=nkd, epilogue=lambda accs, ex: [accs[0] + ALPHA * ex[0]], outs=[(D, F32, 0)],
                 extras=[(dr2, "mn", 0)])
    dwin = _wgrad("dw_in_0", x1b, dps[0], M=D, N=D, T=T, tm=tn, tn=tn, tk=tk, ncols=4 * D)
    for k in range(1, 4):
        dwin = _wgrad(f"dw_in_{k}", x1b, dps[k], M=D, N=D, T=T, tm=tn, tn=tn, tk=tk, into=dwin, joff=k * nj)
    mix_names = ["w_in", "w_a_proj", "w_b_proj", "w_out"]
    mix_grads = [dwin, dwa, dwb, dwout]
    redm = _reduce_scatter("mix", [bigs[n] for n in mix_names], [g[0] for g in mix_grads], [g[1] for g in mix_grads], hc, qc)

    dr1, dr1h, dln1_g, dln1_b = _ln_bwd("ln1_bwd", dx1, xh1, rs1, ln1_g, 0.5, tm_r)
    dx, dwgu1, dwd1 = _ffn_bwd("ffn1", dr1, dr1h, xb, g1, u1, a1, fulls["ffn1_w_gu"], fulls["ffn1_w_down"], T, D, F)
    red1 = _reduce_scatter("ffn1", [bigs["ffn1_w_gu"], bigs["ffn1_w_down"]], [dwgu1[0], dwd1[0]], [dwgu1[1], dwd1[1]], hc, qc)

    grads = {"ffn1_w_gu": red1[0], "ffn1_w_down": red1[1], "w_in": redm[0], "w_a_proj": redm[1], "w_b_proj": redm[2],
             "w_out": redm[3], "ffn2_w_gu": red2[0], "ffn2_w_down": red2[1]}

    small_names = ["ln1_g", "ln1_b", "b_in", "sgu_ln_g", "sgu_ln_b", "sgu_w_s", "sgu_b_s", "conv_w_dw", "conv_b_dw",
                   "conv_ln_g", "conv_ln_b", "ln2_g", "ln2_b", "ln3_g", "ln3_b"]
    small_parts = {"ln1_g": dln1_g, "ln1_b": dln1_b, "b_in": db_in, "sgu_ln_g": dsgu_g, "sgu_ln_b": dsgu_b, "sgu_w_s": dwm,
                   "sgu_b_s": dbst.T, "conv_w_dw": dwdw[:KW], "conv_b_dw": dbdw, "conv_ln_g": dcln_g, "conv_ln_b": dcln_b,
                   "ln2_g": dln2_g, "ln2_b": dln2_b, "ln3_g": dln3_g, "ln3_b": dln3_b}
    packed, spans = _pack_rows([small_parts[n] for n in small_names])
    gsum = _sum8("small_sum", _allgather_small(packed))
    full_shapes = [args[n].shape if n != "conv_w_dw" else (1, KW, W) for n in small_names]
    gsmall = dict(zip(small_names, _unpack_rows(gsum, spans, full_shapes), strict=True))
    cs = conv_w_dw.shape[2]
    gsmall["conv_w_dw"] = lax.dynamic_slice_in_dim(gsmall["conv_w_dw"], q * cs, cs, axis=2)

    outs_g, outs_d, outs_m, outs_v = {}, {}, {}, {}
    for b in blist:
        g = grads[b.name]
        d, mn, vn = _adamw(f"adamw_{b.name}", b.w, g, b.m, b.v, _tile_rows(b.sr, b.sc, 1024 * 1024))
        outs_g[b.name], outs_d[b.name], outs_m[b.name], outs_v[b.name] = g[None], d[None], mn[None], vn[None]
    pw, spans2 = _pack_rows([args[n] for n in small_names])
    pg, _ = _pack_rows([gsmall[n] for n in small_names])
    pm, _ = _pack_rows([args["m_" + n] for n in small_names])
    pv, _ = _pack_rows([args["v_" + n] for n in small_names])
    pd, pmn, pvn = _adamw("adamw_small", pw, pg, pm, pv, pw.shape[0])
    shapes2 = [args[n].shape for n in small_names]
    for dst, src in ((outs_d, pd), (outs_m, pmn), (outs_v, pvn)):
        dst.update(zip(small_names, _unpack_rows(src, spans2, shapes2), strict=True))
    outs_g.update(gsmall)

    loss = lax.psum(loss_part[0, 0], ("x", "y", "c"))
    order = ["ffn1_w_gu", "ffn1_w_down", "ln1_g", "ln1_b", "w_in", "b_in", "sgu_ln_g", "sgu_ln_b", "sgu_w_s", "sgu_b_s",
             "w_a_proj", "conv_w_dw", "conv_b_dw", "conv_ln_g", "conv_ln_b", "w_b_proj", "w_out", "ln2_g", "ln2_b",
             "ffn2_w_gu", "ffn2_w_down", "ln3_g", "ln3_b"]
    return (loss, dx[None], *[outs_g[n] for n in order], *[outs_d[n] for n in order], *[outs_m[n] for n in order],
            *[outs_v[n] for n in order])


def _allgather_conv_w(w_shard):
    kw, cs = w_shard.shape
    padded = jnp.pad(w_shard, ((0, HALO - kw), (0, 0)))

    def plan(ro, rw, new, x, y, c):
        q = 2 * x + y
        cols = pl.ds(pl.multiple_of(q * cs, 128), cs)
        remote = [(ro[0], new[0].at[:, cols], (cx, cy, c)) for cx, cy in _chips(x, y)]
        return remote, [(ro[0], new[0].at[:, cols])]

    return _exchange("conv_w_allgather", [padded], [], [jax.ShapeDtypeStruct((HALO, cs * N_CHIPS), F32)], plan, 3, 1)[0]
```

```python
import math

import jax
import jax.numpy as jnp
from jax import lax
from jax.experimental import pallas as pl
from jax.experimental.pallas import tpu as pltpu

BF16 = jnp.bfloat16
F32 = jnp.float32

LN_EPS = 1e-5
ALPHA = 2.0 ** 0.25
SGU_BLOCK = 128
SGU_CHUNK = 64
HALO = 32
SUBLANES = 8
LANES = 128
CONV_ROWS = 32
ADAM_LR = 0.001
ADAM_B1 = 0.9
ADAM_B2 = 0.999
ADAM_EPS = 1e-08
ADAM_WD = 0.01
ADAM_STEP = 10
N_CHIPS = 4
VMEM_LIMIT = 52 * 1024 * 1024
MESH = pl.DeviceIdType.MESH

_GELU_C0 = math.sqrt(2.0 / math.pi)
_GELU_C1 = 0.044715


def _cparams(sem):
    return pltpu.CompilerParams(dimension_semantics=sem, vmem_limit_bytes=VMEM_LIMIT)


def _gelu_parts(x):
    x2 = x * x
    t = jnp.tanh(_GELU_C0 * (x + _GELU_C1 * (x2 * x)))
    return 0.5 * (1.0 + t), t, x2


def _gelu(x):
    cdf, _, _ = _gelu_parts(x)
    return x * cdf


def _gelu_and_grad(x):
    cdf, t, x2 = _gelu_parts(x)
    grad = cdf + x * (0.5 * (1.0 - t * t)) * (_GELU_C0 * (1.0 + (3.0 * _GELU_C1) * x2))
    return x * cdf, grad


def _silu_grad(x):
    s = jax.nn.sigmoid(x)
    return s * (1.0 + x * (1.0 - s))


def _row_stats(x):
    mu = jnp.mean(x, axis=-1, keepdims=True)
    xc = x - mu
    var = jnp.mean(xc * xc, axis=-1, keepdims=True)
    rstd = lax.rsqrt(var + LN_EPS)
    return xc * rstd, rstd


def _ln_bwd_rows(dy, xhat, rstd, g):
    dxh = dy * g
    m1 = jnp.mean(dxh, axis=-1, keepdims=True)
    m2 = jnp.mean(dxh * xhat, axis=-1, keepdims=True)
    return rstd * (dxh - m1 - xhat * m2)


def _colsum(v):
    return jnp.sum(v, axis=0, keepdims=True)


class _Carry:
    def __init__(self, ro, rw, new, plan, n_remote, n_local=0):
        self.ro, self.rw, self.new, self.plan = list(ro), list(rw), list(new), plan
        self.n_remote, self.n_local = n_remote, n_local

    def sems(self):
        return [pltpu.SemaphoreType.DMA((self.n_remote,)), pltpu.SemaphoreType.DMA((self.n_remote,)),
                pltpu.SemaphoreType.DMA((max(self.n_local, 1),))]

    def copies(self, ro_refs, rw_refs, new_refs, send_sems, recv_sems, loc_sems):
        x, y, c = lax.axis_index("x"), lax.axis_index("y"), lax.axis_index("c")
        remote, local = self.plan(ro_refs, rw_refs, new_refs, x, y, c)
        assert len(remote) == self.n_remote and len(local) == self.n_local
        lcs = [pltpu.make_async_copy(s, d, loc_sems.at[k]) for k, (s, d) in enumerate(local)]
        rcs = [pltpu.make_async_remote_copy(src_ref=s, dst_ref=d, send_sem=send_sems.at[k], recv_sem=recv_sems.at[k],
                                            device_id=dev, device_id_type=MESH) for k, (s, d, dev) in enumerate(remote)]
        return lcs, rcs

    def out_shape(self):
        return [jax.ShapeDtypeStruct(a.shape, a.dtype) for a in self.rw] + self.new


def _start_all(lcs, rcs):
    for cp in lcs + rcs:
        cp.start()


def _wait_all(lcs, rcs):
    for cp in rcs:
        cp.wait_send()
    for cp in rcs:
        cp.wait_recv()
    for cp in lcs:
        cp.wait()


def _merge(*cs):
    ro = [a for c in cs for a in c.ro]
    rw = [a for c in cs for a in c.rw]
    new = [a for c in cs for a in c.new]

    def plan(ro_refs, rw_refs, new_refs, x, y, c):
        remote, local, a, b, d = [], [], 0, 0, 0
        for cc in cs:
            r, l = cc.plan(ro_refs[a:a + len(cc.ro)], rw_refs[b:b + len(cc.rw)], new_refs[d:d + len(cc.new)], x, y, c)
            a, b, d = a + len(cc.ro), b + len(cc.rw), d + len(cc.new)
            remote += r
            local += l
        return remote, local

    def unpack(couts):
        res, b, d = [], 0, len(rw)
        for cc in cs:
            res.append(list(couts[b:b + len(cc.rw)]) + list(couts[d:d + len(cc.new)]))
            b, d = b + len(cc.rw), d + len(cc.new)
        return res

    return _Carry(ro, rw, new, plan, sum(c.n_remote for c in cs), sum(c.n_local for c in cs)), unpack


_ANY = pl.BlockSpec(memory_space=pl.ANY)


def _exchange(name, carry):
    n_ro, n_rw, n_new = len(carry.ro), len(carry.rw), len(carry.new)

    def body(*refs):
        o0 = n_ro + n_rw
        lcs, rcs = carry.copies(refs[:n_ro], refs[o0:o0 + n_rw], refs[o0 + n_rw:o0 + n_rw + n_new], *refs[o0 + n_rw + n_new:])
        _start_all(lcs, rcs)
        _wait_all(lcs, rcs)

    return list(pl.pallas_call(
        body, name=name, in_specs=[_ANY] * (n_ro + n_rw), out_specs=[_ANY] * (n_rw + n_new), out_shape=carry.out_shape(),
        input_output_aliases={n_ro + k: k for k in range(n_rw)}, scratch_shapes=carry.sems())(*carry.ro, *carry.rw))


def _call(body, *, name, grid, in_specs, out_specs, out_shape, args, scratch=(), sem, carry=None, aliases=None):
    in_specs, out_specs, out_shape, scratch = list(in_specs), list(out_specs), list(out_shape), list(scratch)
    if carry is None:
        return list(pl.pallas_call(body, name=name, grid=grid, in_specs=in_specs, out_specs=out_specs, out_shape=out_shape,
                                   scratch_shapes=scratch, input_output_aliases=aliases or {},
                                   compiler_params=_cparams(sem))(*args))
    n_in, n_out, n_scr = len(in_specs), len(out_specs), len(scratch)
    n_ro, n_rw, n_new = len(carry.ro), len(carry.rw), len(carry.new)

    def wrapped(*refs):
        ins = refs[:n_in]
        ro_refs = refs[n_in:n_in + n_ro]
        o0 = n_in + n_ro + n_rw
        outs = refs[o0:o0 + n_out]
        rw_refs = refs[o0 + n_out:o0 + n_out + n_rw]
        new_refs = refs[o0 + n_out + n_rw:o0 + n_out + n_rw + n_new]
        s0 = o0 + n_out + n_rw + n_new
        scr = refs[s0:s0 + n_scr]
        sems = refs[s0 + n_scr:]
        first = pl.program_id(0) == 0
        last = pl.program_id(0) == grid[0] - 1
        for d in range(1, len(grid)):
            first = jnp.logical_and(first, pl.program_id(d) == 0)
            last = jnp.logical_and(last, pl.program_id(d) == grid[d] - 1)

        @pl.when(first)
        def _():
            _start_all(*carry.copies(ro_refs, rw_refs, new_refs, *sems))

        body(*ins, *outs, *scr)

        @pl.when(last)
        def _():
            _wait_all(*carry.copies(ro_refs, rw_refs, new_refs, *sems))

    al = dict(aliases or {})
    al.update({n_in + n_ro + k: n_out + k for k in range(n_rw)})
    res = pl.pallas_call(
        wrapped, name=name, grid=grid, in_specs=in_specs + [_ANY] * (n_ro + n_rw), out_specs=out_specs + [_ANY] * (n_rw + n_new),
        out_shape=out_shape + carry.out_shape(), scratch_shapes=scratch + carry.sems(), input_output_aliases=al,
        compiler_params=_cparams(("arbitrary",) * len(grid)))(*args, *carry.ro, *carry.rw)
    return list(res[:n_out]), list(res[n_out:])


def _mm(name, pairs, *, ta=False, tb=False, M, N, tm, tn, tk, nk, epilogue, outs, extras=(), sum_pairs=True, carry=None):
    ni, nj = M // tm, N // tn
    assert ni * tm == M and nj * tn == N
    n_p = len(pairs)
    n_acc = 1 if sum_pairs else n_p
    in_specs, args = [], []
    for a, b, ak, bk, bj in pairs:
        if ta:
            in_specs.append(pl.BlockSpec((tk, tm), lambda i, j, k, ak=ak: (k + ak, i)))
        else:
            in_specs.append(pl.BlockSpec((tm, tk), lambda i, j, k, ak=ak: (i, k + ak)))
        if tb:
            in_specs.append(pl.BlockSpec((tn, tk), lambda i, j, k, bk=bk, bj=bj: (j + bj, k + bk)))
        else:
            in_specs.append(pl.BlockSpec((tk, tn), lambda i, j, k, bk=bk, bj=bj: (k + bk, j + bj)))
        args += [a, b]
    for arr, kind, jo in extras:
        if kind == "mn":
            in_specs.append(pl.BlockSpec((tm, tn), lambda i, j, k, jo=jo: (i, j + jo)))
        else:
            in_specs.append(pl.BlockSpec((1, tn), lambda i, j, k, jo=jo: (0, j + jo)))
        args.append(arr)
    out_specs = [pl.BlockSpec((tm, tn), lambda i, j, k, jo=jo: (i, j + jo)) for _, _, jo in outs]
    out_shape = [jax.ShapeDtypeStruct((M, nc), dt) for nc, dt, _ in outs]
    n_ex, n_out = len(extras), len(outs)
    dn = (((0 if ta else 1,), (1 if tb else 0,)), ((), ()))

    def body(*refs):
        ab = refs[: 2 * n_p]
        ex = refs[2 * n_p: 2 * n_p + n_ex]
        o0 = 2 * n_p + n_ex
        out_refs = refs[o0: o0 + n_out]
        acc_refs = refs[o0 + n_out:]

        def dots():
            res = []
            for p in range(n_p):
                a = ab[2 * p][...]
                b = ab[2 * p + 1][...]
                res.append(lax.dot_general(a.astype(BF16), b.astype(BF16), dn, preferred_element_type=F32))
            if sum_pairs:
                tot = res[0]
                for r in res[1:]:
                    tot = tot + r
                res = [tot]
            return res

        def finish(accs):
            tiles = epilogue(accs, [e[...] for e in ex])
            for r, t in zip(out_refs, tiles, strict=True):
                r[...] = t.astype(r.dtype)

        if nk == 1:
            finish(dots())
        else:
            k = pl.program_id(2)
            d = dots()

            @pl.when(k == 0)
            def _():
                for r, v in zip(acc_refs, d, strict=True):
                    r[...] = v

            @pl.when(k > 0)
            def _():
                for r, v in zip(acc_refs, d, strict=True):
                    r[...] += v

            @pl.when(k == nk - 1)
            def _():
                finish([r[...] for r in acc_refs])

    scratch = [pltpu.VMEM((tm, tn), F32) for _ in range(n_acc)] if nk > 1 else []
    return _call(body, name=name, grid=(ni, nj, nk), in_specs=in_specs, out_specs=out_specs, out_shape=out_shape, args=args,
                 scratch=scratch, sem=("parallel", "parallel", "arbitrary"), carry=carry)


def _wgrad(name, a, b, *, M, N, T, tm, tn, tk, into=None, joff=0, ncols=None, carry=None):
    ncols = N if ncols is None else ncols
    ni, nj, nk = M // tm, N // tn, T // tk
    dn = (((0,), (0,)), ((), ()))

    def body(a_ref, b_ref, *rest):
        of_ref, oh_ref, acc_ref = rest[-3:]
        k = pl.program_id(2)
        d = lax.dot_general(a_ref[...].astype(BF16), b_ref[...].astype(BF16), dn, preferred_element_type=F32)

        @pl.when(k == 0)
        def _():
            acc_ref[...] = d

        @pl.when(k > 0)
        def _():
            acc_ref[...] += d

        @pl.when(k == nk - 1)
        def _():
            of_ref[...] = acc_ref[...]
            oh_ref[...] = acc_ref[...].astype(BF16)

    ospec = pl.BlockSpec((tm, tn), lambda i, j, k: (i, j + joff))
    in_specs = [pl.BlockSpec((tk, tm), lambda i, j, k: (k, i)), pl.BlockSpec((tk, tn), lambda i, j, k: (k, j))]
    args, aliases = [a, b], None
    if into is not None:
        in_specs += [_ANY, _ANY]
        args += list(into)
        aliases = {2: 0, 3: 1}
    return _call(body, name=name, grid=(ni, nj, nk), in_specs=in_specs, out_specs=[ospec, ospec],
                 out_shape=[jax.ShapeDtypeStruct((M, ncols), F32), jax.ShapeDtypeStruct((M, ncols), BF16)], args=args,
                 scratch=[pltpu.VMEM((tm, tn), F32)], sem=("parallel", "parallel", "arbitrary"), carry=carry, aliases=aliases)


def _rows(tm, c, cb=0):
    return pl.BlockSpec((tm, c), lambda i, cb=cb: (i, cb))


def _whole(shape):
    nd = len(shape)
    return pl.BlockSpec(shape, lambda i, nd=nd: (0,) * nd)


def _cast_bf16(name, x, tm, carry=None):
    t, d = x.shape

    def body(x_ref, o_ref):
        o_ref[...] = x_ref[...].astype(BF16)

    return _call(body, name=name, grid=(t // tm,), in_specs=[_rows(tm, d)], out_specs=[_rows(tm, d)],
                 out_shape=[jax.ShapeDtypeStruct((t, d), BF16)], args=[x], sem=("parallel",), carry=carry)


def _ln_fwd(name, xres, f, g, b, cf, tm, carry=None):
    t, d = xres.shape

    def body(x_ref, f_ref, g_ref, b_ref, y_ref, yb_ref, xh_ref, rs_ref):
        r = ALPHA * x_ref[...] + cf * f_ref[...]
        xhat, rstd = _row_stats(r)
        y = xhat * g_ref[...] + b_ref[...]
        y_ref[...] = y
        yb_ref[...] = y.astype(BF16)
        xh_ref[...] = xhat
        rs_ref[...] = rstd

    return _call(
        body, name=name, grid=(t // tm,), in_specs=[_rows(tm, d), _rows(tm, d), _whole((1, d)), _whole((1, d))],
        out_specs=[_rows(tm, d), _rows(tm, d), _rows(tm, d), _rows(tm, 1)],
        out_shape=[jax.ShapeDtypeStruct((t, d), F32), jax.ShapeDtypeStruct((t, d), BF16),
                   jax.ShapeDtypeStruct((t, d), F32), jax.ShapeDtypeStruct((t, 1), F32)],
        args=[xres, f, g, b], sem=("parallel",), carry=carry)


def _ln_bwd(name, dy, xhat, rstd, g, scale, tm, carry=None):
    t, d = dy.shape

    def body(dy_ref, xh_ref, rs_ref, g_ref, dr_ref, drb_ref, dg_ref, db_ref):
        i = pl.program_id(0)
        dy_v, xh = dy_ref[...], xh_ref[...]
        dr = _ln_bwd_rows(dy_v, xh, rs_ref[...], g_ref[...])
        dr_ref[...] = dr
        drb_ref[...] = (scale * dr).astype(BF16)

        @pl.when(i == 0)
        def _():
            dg_ref[...] = jnp.zeros_like(dg_ref)
            db_ref[...] = jnp.zeros_like(db_ref)

        dg_ref[...] += _colsum(dy_v * xh)
        db_ref[...] += _colsum(dy_v)

    return _call(
        body, name=name, grid=(t // tm,), in_specs=[_rows(tm, d), _rows(tm, d), _rows(tm, 1), _whole((1, d))],
        out_specs=[_rows(tm, d), _rows(tm, d), _whole((1, d)), _whole((1, d))],
        out_shape=[jax.ShapeDtypeStruct((t, d), F32), jax.ShapeDtypeStruct((t, d), BF16),
                   jax.ShapeDtypeStruct((1, d), F32), jax.ShapeDtypeStruct((1, d), F32)],
        args=[dy, xhat, rstd, g], sem=("arbitrary",), carry=carry)


def _ln3_loss(name, xres, f, g, b, target, tm):
    t, d = xres.shape

    def body(x_ref, f_ref, g_ref, b_ref, tg_ref, dr_ref, drb_ref, loss_ref, dg_ref, db_ref):
        i = pl.program_id(0)
        r = ALPHA * x_ref[...] + 0.5 * f_ref[...]
        xhat, rstd = _row_stats(r)
        gv = g_ref[...]
        y = xhat * gv + b_ref[...]
        err = y - tg_ref[...]
        dy = err * (1.0 / d)
        dr = _ln_bwd_rows(dy, xhat, rstd, gv)
        dr_ref[...] = dr
        drb_ref[...] = (0.5 * dr).astype(BF16)
        part = 0.5 * jnp.sum(jnp.mean(err * err, axis=-1, keepdims=True), axis=0, keepdims=True)

        @pl.when(i == 0)
        def _():
            loss_ref[...] = jnp.zeros_like(loss_ref)
            dg_ref[...] = jnp.zeros_like(dg_ref)
            db_ref[...] = jnp.zeros_like(db_ref)

        loss_ref[...] += jnp.broadcast_to(part, loss_ref.shape)
        dg_ref[...] += _colsum(dy * xhat)
        db_ref[...] += _colsum(dy)

    return _call(
        body, name=name, grid=(t // tm,),
        in_specs=[_rows(tm, d), _rows(tm, d), _whole((1, d)), _whole((1, d)), _rows(tm, d)],
        out_specs=[_rows(tm, d), _rows(tm, d), _whole((8, 128)), _whole((1, d)), _whole((1, d))],
        out_shape=[jax.ShapeDtypeStruct((t, d), F32), jax.ShapeDtypeStruct((t, d), BF16),
                   jax.ShapeDtypeStruct((8, 128), F32), jax.ShapeDtypeStruct((1, d), F32),
                   jax.ShapeDtypeStruct((1, d), F32)],
        args=[xres, f, g, b, target], sem=("arbitrary",))


def _colsum_rows(name, x, tm):
    t, d = x.shape

    def body(x_ref, o_ref):
        @pl.when(pl.program_id(0) == 0)
        def _():
            o_ref[...] = jnp.zeros_like(o_ref)

        o_ref[...] += _colsum(x_ref[...].astype(F32))

    return _call(body, name=name, grid=(t // tm,), in_specs=[_rows(tm, d)], out_specs=[_whole((1, d))],
                 out_shape=[jax.ShapeDtypeStruct((1, d), F32)], args=[x], sem=("arbitrary",))[0]


def _sgu_mask():
    sh = SGU_CHUNK.bit_length() - 1
    r = lax.shift_right_logical(lax.broadcasted_iota(jnp.int32, (SGU_BLOCK, SGU_BLOCK), 0), sh)
    c = lax.shift_right_logical(lax.broadcasted_iota(jnp.int32, (SGU_BLOCK, SGU_BLOCK), 1), sh)
    return c <= r


def _sgu_fwd(name, p, lng, lnb, wm, bst, tm):
    t = p.shape[0]
    n_grp, w = wm.shape[0], lng.shape[1]
    hd = w // n_grp
    nblk = tm // SGU_BLOCK

    def body(uv_ref, h_ref, g_ref, b_ref, wm_ref, bs_ref, sa_ref, z_ref, vn_s):
        xhat, _ = _row_stats(_gelu(uv_ref[:, w:]))
        vn_s[...] = (xhat * g_ref[...] + b_ref[...]).astype(BF16)
        mask = _sgu_mask()
        for h in range(n_grp):
            wh = jnp.where(mask, wm_ref[h], 0.0).astype(BF16)
            bcol = bs_ref[:, h:h + 1]
            cs = slice(h * hd, (h + 1) * hd)
            for n in range(nblk):
                rs = slice(n * SGU_BLOCK, (n + 1) * SGU_BLOCK)
                s = jnp.dot(wh, vn_s[rs, cs], preferred_element_type=F32) + bcol
                sa_ref[rs, cs] = (_gelu(uv_ref[rs, cs]) * s).astype(BF16)
        z_ref[...] = h_ref[:, :w] * jax.nn.sigmoid(h_ref[:, w:])

    return _call(
        body, name=name, grid=(t // tm,),
        in_specs=[_rows(tm, 2 * w, 0), _rows(tm, 2 * w, 1), _whole((1, w)), _whole((1, w)), _whole(wm.shape),
                  _whole(bst.shape)],
        out_specs=[_rows(tm, w), _rows(tm, w)],
        out_shape=[jax.ShapeDtypeStruct((t, w), BF16), jax.ShapeDtypeStruct((t, w), F32)],
        args=[p, p, lng, lnb, wm, bst], scratch=[pltpu.VMEM((tm, w), BF16)], sem=("parallel",))


def _sgu_bwd(name, p, dsa, lng, lnb, wm, bst, tm):
    t = p.shape[0]
    n_grp, w = wm.shape[0], lng.shape[1]
    hd = w // n_grp
    nblk = tm // SGU_BLOCK

    def body(uv_ref, dsa_ref, g_ref, b_ref, wm_ref, bs_ref, dp_ref, dwm_ref, dbs_ref, dg_ref, db_ref,
             vn_s, ug_s, dvn_s, dug_s):
        i = pl.program_id(0)

        @pl.when(i == 0)
        def _():
            dwm_ref[...] = jnp.zeros_like(dwm_ref)
            dbs_ref[...] = jnp.zeros_like(dbs_ref)
            dg_ref[...] = jnp.zeros_like(dg_ref)
            db_ref[...] = jnp.zeros_like(db_ref)

        ug, dgelu_u = _gelu_and_grad(uv_ref[:, :w])
        ug_s[...] = ug
        vg, dgelu_v = _gelu_and_grad(uv_ref[:, w:])
        xhat, rstd = _row_stats(vg)
        gv = g_ref[...]
        vn_s[...] = (xhat * gv + b_ref[...]).astype(BF16)
        mask = _sgu_mask()
        for h in range(n_grp):
            wh = jnp.where(mask, wm_ref[h], 0.0).astype(BF16)
            bcol = bs_ref[:, h:h + 1]
            cs = slice(h * hd, (h + 1) * hd)
            dw_h = jnp.zeros((SGU_BLOCK, SGU_BLOCK), F32)
            dbs_h = jnp.zeros((SGU_BLOCK, 1), F32)
            for n in range(nblk):
                rs = slice(n * SGU_BLOCK, (n + 1) * SGU_BLOCK)
                vblk = vn_s[rs, cs]
                s = jnp.dot(wh, vblk, preferred_element_type=F32) + bcol
                dsa_blk = dsa_ref[rs, cs]
                dug_s[rs, cs] = dsa_blk * s
                ds = dsa_blk * ug_s[rs, cs]
                dsb = ds.astype(BF16)
                dvn_s[rs, cs] = lax.dot_general(wh, dsb, (((0,), (0,)), ((), ())), preferred_element_type=F32)
                dw_h = dw_h + lax.dot_general(dsb, vblk, (((1,), (1,)), ((), ())), preferred_element_type=F32)
                dbs_h = dbs_h + jnp.sum(ds, axis=1, keepdims=True)
            dwm_ref[h] += jnp.where(mask, dw_h, 0.0)
            dbs_ref[:, h:h + 1] += dbs_h
        dvn = dvn_s[...]
        dg_ref[...] += _colsum(dvn * xhat)
        db_ref[...] += _colsum(dvn)
        dvg = _ln_bwd_rows(dvn, xhat, rstd, gv)
        dp_ref[:, :w] = (dug_s[...] * dgelu_u).astype(BF16)
        dp_ref[:, w:] = (dvg * dgelu_v).astype(BF16)

    return _call(
        body, name=name, grid=(t // tm,),
        in_specs=[_rows(tm, 2 * w, 0), _rows(tm, w), _whole((1, w)), _whole((1, w)), _whole(wm.shape), _whole(bst.shape)],
        out_specs=[_rows(tm, 2 * w), _whole(wm.shape), _whole(bst.shape), _whole((1, w)), _whole((1, w))],
        out_shape=[jax.ShapeDtypeStruct((t, 2 * w), BF16), jax.ShapeDtypeStruct(wm.shape, F32),
                   jax.ShapeDtypeStruct(bst.shape, F32), jax.ShapeDtypeStruct((1, w), F32), jax.ShapeDtypeStruct((1, w), F32)],
        args=[p, dsa, lng, lnb, wm, bst],
        scratch=[pltpu.VMEM((tm, w), BF16), pltpu.VMEM((tm, w), F32), pltpu.VMEM((tm, w), F32), pltpu.VMEM((tm, w), F32)],
        sem=("arbitrary",))


def _halo_prev(tm, c):
    return pl.BlockSpec((HALO, c), lambda i: (jnp.maximum(i * (tm // HALO) - 1, 0), 0))


def _halo_next(tm, c, t):
    last = t // HALO - 1
    return pl.BlockSpec((HALO, c), lambda i: (jnp.minimum((i + 1) * (tm // HALO), last), 0))


def _shifted_copies(sh, n):
    for r in range(1, SUBLANES):
        sh[r, :n - SUBLANES, :] = sh[0, r:r + n - SUBLANES, :]


def _tap(sh, r0, o, rows, cols):
    return sh[o % SUBLANES, pl.ds(pl.multiple_of(r0 + (o - o % SUBLANES), SUBLANES), rows), cols]


def _conv_fwd(name, z, wdw, kw, bdw, lng, lnb, tm, carry=None):
    t, c = z.shape
    lead = HALO - (kw - 1)
    n = tm + HALO

    def body(zp_ref, z_ref, w_ref, bdw_ref, g_ref, b_ref, zc_ref, sb_ref, sh):
        i = pl.program_id(0)
        sh[0, :HALO, :] = jnp.where(i > 0, zp_ref[...], 0.0)
        sh[0, HALO:, :] = z_ref[...]
        _shifted_copies(sh, n)
        bias = bdw_ref[...]

        def chunk(ci, _):
            r0 = pl.multiple_of(ci * CONV_ROWS, CONV_ROWS)
            acc = jnp.broadcast_to(bias, (CONV_ROWS, c))
            for k in range(kw):
                acc = acc + w_ref[k:k + 1, :] * _tap(sh, r0, lead + k, CONV_ROWS, slice(None))
            zc_ref[pl.ds(r0, CONV_ROWS), :] = acc
            return 0

        lax.fori_loop(0, tm // CONV_ROWS, chunk, 0)
        xhat, _ = _row_stats(zc_ref[...])
        zn = xhat * g_ref[...] + b_ref[...]
        sb_ref[...] = (zn * jax.nn.sigmoid(zn)).astype(BF16)

    return _call(
        body, name=name, grid=(t // tm,),
        in_specs=[_halo_prev(tm, c), _rows(tm, c), _whole(wdw.shape), _whole((1, c)), _whole((1, c)), _whole((1, c))],
        out_specs=[_rows(tm, c), _rows(tm, c)],
        out_shape=[jax.ShapeDtypeStruct((t, c), F32), jax.ShapeDtypeStruct((t, c), BF16)],
        args=[z, z, wdw, bdw, lng, lnb], scratch=[pltpu.VMEM((SUBLANES, n, c), F32)], sem=("parallel",), carry=carry)


def _conv_ln_bwd(name, dsb, zc, lng, lnb, tm):
    t, c = zc.shape

    def body(dsb_ref, zc_ref, g_ref, b_ref, dzc_ref, dg_ref, db_ref, dbdw_ref):
        i = pl.program_id(0)
        xhat, rstd = _row_stats(zc_ref[...])
        gv = g_ref[...]
        zn = xhat * gv + b_ref[...]
        dzn = dsb_ref[...] * _silu_grad(zn)
        dzc = _ln_bwd_rows(dzn, xhat, rstd, gv)
        dzc_ref[...] = dzc

        @pl.when(i == 0)
        def _():
            dg_ref[...] = jnp.zeros_like(dg_ref)
            db_ref[...] = jnp.zeros_like(db_ref)
            dbdw_ref[...] = jnp.zeros_like(dbdw_ref)

        dg_ref[...] += _colsum(dzn * xhat)
        db_ref[...] += _colsum(dzn)
        dbdw_ref[...] += _colsum(dzc)

    return _call(
        body, name=name, grid=(t // tm,), in_specs=[_rows(tm, c), _rows(tm, c), _whole((1, c)), _whole((1, c))],
        out_specs=[_rows(tm, c), _whole((1, c)), _whole((1, c)), _whole((1, c))],
        out_shape=[jax.ShapeDtypeStruct((t, c), F32)] + [jax.ShapeDtypeStruct((1, c), F32)] * 3,
        args=[dsb, zc, lng, lnb], sem=("arbitrary",))


def _conv_bwd(name, dzc, z, p, wdw, kw, tm):
    t, c = z.shape
    n_i = t // tm
    n = tm + HALO

    def body(dzc_ref, dzn_ref, z_ref, h_ref, w_ref, dp_ref, dw_ref, sh, dz_s):
        i = pl.program_id(0)

        @pl.when(i == 0)
        def _():
            dw_ref[...] = jnp.zeros_like(dw_ref)

        sh[0, :tm, :] = dzc_ref[...]
        sh[0, tm:, :] = jnp.where(i < n_i - 1, dzn_ref[...], 0.0)
        _shifted_copies(sh, n)
        for lc in range(c // LANES):
            cols = slice(lc * LANES, (lc + 1) * LANES)

            def rowv(rv, accs, cols=cols):
                r0 = pl.multiple_of(rv * SUBLANES, SUBLANES)
                zv = z_ref[pl.ds(r0, SUBLANES), cols]
                dz = jnp.zeros((SUBLANES, LANES), F32)
                new = []
                for k in range(kw):
                    s = _tap(sh, r0, kw - 1 - k, SUBLANES, cols)
                    dz = dz + w_ref[k:k + 1, cols] * s
                    new.append(accs[k] + zv * s)
                dz_s[pl.ds(r0, SUBLANES), cols] = dz
                return tuple(new)

            accs = lax.fori_loop(0, tm // SUBLANES, rowv, tuple(jnp.zeros((SUBLANES, LANES), F32) for _ in range(kw)))
            for k in range(kw):
                dw_ref[k:k + 1, cols] += _colsum(accs[k])
        dz = dz_s[...]
        a, g = h_ref[:, :c], h_ref[:, c:]
        sg = jax.nn.sigmoid(g)
        dp_ref[:, :c] = (dz * sg).astype(BF16)
        dp_ref[:, c:] = (dz * a * (sg * (1.0 - sg))).astype(BF16)

    return _call(
        body, name=name, grid=(n_i,),
        in_specs=[_rows(tm, c), _halo_next(tm, c, t), _rows(tm, c), _rows(tm, 2 * c, 1), _whole(wdw.shape)],
        out_specs=[_rows(tm, 2 * c), _whole((HALO, c))],
        out_shape=[jax.ShapeDtypeStruct((t, 2 * c), BF16), jax.ShapeDtypeStruct((HALO, c), F32)],
        args=[dzc, dzc, z, p, wdw], scratch=[pltpu.VMEM((SUBLANES, n, c), F32), pltpu.VMEM((tm, c), F32)],
        sem=("arbitrary",))


def _adamw(name, w, g, m, v, tr, carry=None):
    r, c = w.shape
    c1 = 1.0 - ADAM_B1 ** ADAM_STEP
    c2 = 1.0 - ADAM_B2 ** ADAM_STEP

    def body(w_ref, g_ref, m_ref, v_ref, d_ref, mo_ref, vo_ref):
        gv = g_ref[...]
        mn = ADAM_B1 * m_ref[...] + (1.0 - ADAM_B1) * gv
        vn = ADAM_B2 * v_ref[...] + (1.0 - ADAM_B2) * (gv * gv)
        d_ref[...] = -ADAM_LR * ((mn / c1) / (jnp.sqrt(vn / c2) + ADAM_EPS) + ADAM_WD * w_ref[...])
        mo_ref[...] = mn
        vo_ref[...] = vn

    spec = _rows(tr, c)
    return _call(body, name=name, grid=(r // tr,), in_specs=[spec] * 4, out_specs=[spec] * 3,
                 out_shape=[jax.ShapeDtypeStruct((r, c), F32)] * 3, args=[w, g, m, v], sem=("parallel",), carry=carry)


def _pair_sum(name, mine, recv, hc, *, out_rows, tr, row_map):
    c_ = mine.shape[1]

    def body(hc_ref, a_ref, b_ref, of_ref, oh_ref):
        s = a_ref[...] + b_ref[...].astype(F32)
        of_ref[...] = s
        oh_ref[...] = s.astype(BF16)

    ispec = pl.BlockSpec((tr, c_), lambda i, hc_ref: (row_map(i, hc_ref[0]), 0))
    ospec = pl.BlockSpec((tr, c_), lambda i, hc_ref: (i, 0))
    return pl.pallas_call(
        body, name=name,
        grid_spec=pltpu.PrefetchScalarGridSpec(num_scalar_prefetch=1, grid=(out_rows // tr,), in_specs=[ispec, ispec],
                                               out_specs=[ospec, ospec]),
        out_shape=[jax.ShapeDtypeStruct((out_rows, c_), F32), jax.ShapeDtypeStruct((out_rows, c_), BF16)],
        compiler_params=_cparams(("parallel",)))(hc, mine, recv)


def _final_sum(name, cs32, recv2, qc, *, part_shape, out_shape, tr, in_map, out_map):
    pr, pc = part_shape

    def body(qc_ref, a_ref, r_ref, o_ref):
        o_ref[...] = ((a_ref[...] + r_ref[0].astype(F32)) + r_ref[1].astype(F32)) + r_ref[2].astype(F32)

    return pl.pallas_call(
        body, name=name,
        grid_spec=pltpu.PrefetchScalarGridSpec(
            num_scalar_prefetch=1, grid=(pr // tr,),
            in_specs=[pl.BlockSpec((tr, pc), lambda i, qc_ref: in_map(i, qc_ref[0], qc_ref[1])),
                      pl.BlockSpec((3, tr, pc), lambda i, qc_ref: (0, i, 0))],
            out_specs=pl.BlockSpec((tr, pc), lambda i, qc_ref: out_map(i, qc_ref[0], qc_ref[1]))),
        out_shape=jax.ShapeDtypeStruct(out_shape, F32), compiler_params=_cparams(("parallel",)))(qc, cs32, recv2)


def _sum8(name, slots):
    def body(s_ref, o_ref):
        acc = s_ref[0]
        for d in range(1, 8):
            acc = acc + s_ref[d]
        o_ref[...] = acc

    return pl.pallas_call(body, name=name, out_shape=jax.ShapeDtypeStruct(slots.shape[1:], F32),
                          in_specs=[pl.BlockSpec(memory_space=pltpu.VMEM)], out_specs=pl.BlockSpec(memory_space=pltpu.VMEM),
                          compiler_params=pltpu.CompilerParams(vmem_limit_bytes=VMEM_LIMIT))(slots)


def _chips(x, y):
    return [(1 - x, y), (x, 1 - y), (1 - x, 1 - y)]


class _Big:
    def __init__(self, name, w, m, v, ax):
        self.name, self.w, self.m, self.v, self.ax = name, w, m, v, ax
        sr, sc = w.shape
        self.R, self.C = (sr * N_CHIPS, sc) if ax == 0 else (sr, sc * N_CHIPS)
        self.sr, self.sc = sr, sc
        self.hr = sr // 2

    def slot(self, ref, q, h=None):
        if self.ax == 1:
            cols = pl.ds(pl.multiple_of(q * self.sc, 128), self.sc)
            return ref.at[:, cols] if h is None else ref.at[pl.ds(pl.multiple_of(h * self.hr, 16), self.hr), cols]
        if h is None:
            return ref.at[pl.ds(pl.multiple_of(q * self.sr, 16), self.sr), :]
        return ref.at[pl.ds(pl.multiple_of(q * self.sr + h * self.hr, 16), self.hr), :]

    def half(self, ref, h):
        return ref.at[pl.ds(pl.multiple_of(h * self.hr, 16), self.hr), :]

    def part(self, ref, q):
        if self.ax == 1:
            return ref.at[:, pl.ds(pl.multiple_of(q * self.sc, 128), self.sc)]
        return ref.at[pl.ds(pl.multiple_of(q * self.hr, 16), self.hr), :]

    @property
    def part_shape(self):
        return (self.hr, self.sc)

    def gather_ici(self, shard16):
        def plan(ro, rw, new, x, y, c):
            q = 2 * x + y
            remote = [(self.half(ro[0], c), self.slot(new[0], q, c), (cx, cy, c)) for cx, cy in _chips(x, y)]
            return remote, [(ro[0], self.slot(new[0], q))]

        return _Carry([shard16], [], [jax.ShapeDtypeStruct((self.R, self.C), BF16)], plan, 3, 1)

    def gather_d2d(self, full):
        def plan(ro, rw, new, x, y, c):
            remote = []
            for cx, cy in _chips(x, y):
                piece = self.slot(rw[0], 2 * cx + cy, c)
                remote.append((piece, piece, (x, y, 1 - c)))
            return remote, []

        return _Carry([], [full], [], plan, 3)

    def rs_pair(self, g16):
        def plan(ro, rw, new, x, y, c):
            sib = (x, y, 1 - c)
            if self.ax == 1:
                rows = pl.ds(pl.multiple_of((1 - c) * self.hr, 16), self.hr)
                return [(ro[0].at[rows, :], new[0].at[rows, :], sib)], []
            return [(self.slot(ro[0], q, 1 - c), self.slot(new[0], q, 1 - c), sib) for q in range(N_CHIPS)], []

        return _Carry([g16], [], [jax.ShapeDtypeStruct((self.R, self.C), BF16)], plan, 1 if self.ax == 1 else N_CHIPS)

    def rs_pairsum(self, tag, g32, recv1, hc):
        tr = _tile_rows(self.hr, self.C)
        nb = self.hr // tr
        if self.ax == 1:
            row_map = lambda i, c: c * nb + i
        else:
            row_map = lambda i, c: ((i // nb) * 2 + c) * nb + i % nb
        return _pair_sum(f"rs_pairsum_{tag}", g32, recv1, hc, out_rows=self.R // 2, tr=tr, row_map=row_map)

    def rs_ici(self, cs16):
        def plan(ro, rw, new, x, y, c):
            return [(self.part(ro[0], 2 * cx + cy), new[0].at[k], (cx, cy, c)) for k, (cx, cy) in enumerate(_chips(x, y))], []

        return _Carry([cs16], [], [jax.ShapeDtypeStruct((3,) + self.part_shape, BF16)], plan, 3)

    def rs_final(self, tag, cs32, recv2, qc):
        tr = _tile_rows(self.hr, self.sc)
        nb = self.hr // tr
        if self.ax == 1:
            in_map = lambda i, q, c: (i, q)
        else:
            in_map = lambda i, q, c: (q * nb + i, 0)
        out_map = lambda i, q, c: (c * nb + i, 0)
        return _final_sum(f"rs_final_{tag}", cs32, recv2, qc, part_shape=self.part_shape, out_shape=(self.sr, self.sc),
                          tr=tr, in_map=in_map, out_map=out_map)

    def rs_share(self, ghalf):
        def plan(ro, rw, new, x, y, c):
            piece = self.half(rw[0], c)
            return [(piece, piece, (x, y, 1 - c))], []

        return _Carry([], [ghalf], [], plan, 1)


def _small_allgather(packed):
    nr = packed.shape[0]

    def plan(ro, rw, new, x, y, c):
        me = 4 * x + 2 * y + c
        remote = []
        for fx in (0, 1):
            for fy in (0, 1):
                for fc in (0, 1):
                    if fx or fy or fc:
                        dev = (1 - x if fx else x, 1 - y if fy else y, 1 - c if fc else c)
                        remote.append((ro[0], new[0].at[me], dev))
        return remote, [(ro[0], new[0].at[me])]

    return _Carry([packed], [], [jax.ShapeDtypeStruct((8, nr, 128), F32)], plan, 7, 1)


def _conv_w_allgather(padded, cs):
    def plan(ro, rw, new, x, y, c):
        cols = pl.ds(pl.multiple_of((2 * x + y) * cs, 128), cs)
        remote = [(ro[0], new[0].at[:, cols], (cx, cy, c)) for cx, cy in _chips(x, y)]
        return remote, [(ro[0], new[0].at[:, cols])]

    return _Carry([padded], [], [jax.ShapeDtypeStruct((HALO, cs * N_CHIPS), F32)], plan, 3, 1)


def _pick(n, want):
    if n <= want:
        return n
    for t in range(want, 15, -16):
        if t % 16 == 0 and n % t == 0:
            return t
    raise ValueError(f"no tile for {n} (want {want})")


def _tile_rows(nrows, ncols, budget=2 * 1024 * 1024):
    return _pick(nrows, max(16, (budget // (4 * ncols)) // 16 * 16))


def _pick128(n, want):
    if n <= want:
        return n
    for t in range(want, 127, -128):
        if n % t == 0:
            return t
    raise ValueError(f"no lane tile for {n} (want {want})")


def _pack_rows(parts):
    out, spans, r0 = [], [], 0
    for p in parts:
        flat = p.reshape(-1).astype(F32)
        n = flat.shape[0]
        rows = -(-n // 1024) * 8
        flat = jnp.pad(flat, (0, rows * 128 - n))
        out.append(flat.reshape(rows, 128))
        spans.append((r0, rows, n))
        r0 += rows
    return jnp.concatenate(out, axis=0), spans


def _unpack_rows(packed, spans, shapes):
    res = []
    for (r0, rows, n), shp in zip(spans, shapes, strict=True):
        res.append(packed[r0:r0 + rows].reshape(-1)[:n].reshape(shp))
    return res


def _ident(accs, ex):
    return [accs[0]]


def kernel(x, ffn1_w_gu, ffn1_w_down, ln1_g, ln1_b, w_in, b_in, sgu_ln_g, sgu_ln_b, sgu_w_s, sgu_b_s, w_a_proj, conv_w_dw, conv_b_dw, conv_ln_g, conv_ln_b, w_b_proj, w_out, ln2_g, ln2_b, ffn2_w_gu, ffn2_w_down, ln3_g, ln3_b, loss_target, m_ffn1_w_gu, m_ffn1_w_down, m_ln1_g, m_ln1_b, m_w_in, m_b_in, m_sgu_ln_g, m_sgu_ln_b, m_sgu_w_s, m_sgu_b_s, m_w_a_proj, m_conv_w_dw, m_conv_b_dw, m_conv_ln_g, m_conv_ln_b, m_w_b_proj, m_w_out, m_ln2_g, m_ln2_b, m_ffn2_w_gu, m_ffn2_w_down, m_ln3_g, m_ln3_b, v_ffn1_w_gu, v_ffn1_w_down, v_ln1_g, v_ln1_b, v_w_in, v_b_in, v_sgu_ln_g, v_sgu_ln_b, v_sgu_w_s, v_sgu_b_s, v_w_a_proj, v_conv_w_dw, v_conv_b_dw, v_conv_ln_g, v_conv_ln_b, v_w_b_proj, v_w_out, v_ln2_g, v_ln2_b, v_ffn2_w_gu, v_ffn2_w_down, v_ln3_g, v_ln3_b):
    args = dict(locals())
    assert x.shape[0] == 1 and ffn1_w_gu.shape[0] == 1
    T, D = x.shape[1], x.shape[2]
    F = ffn1_w_down.shape[1] * N_CHIPS
    W = sgu_ln_g.shape[1]
    KW = conv_w_dw.shape[1]
    assert KW - 1 <= HALO and T % SGU_BLOCK == 0

    mx, my, mc = lax.axis_index("x"), lax.axis_index("y"), lax.axis_index("c")
    q = 2 * mx + my
    hc = jnp.reshape(mc, (1,)).astype(jnp.int32)
    qc = jnp.stack([q, mc]).astype(jnp.int32)

    big_names = [("ffn1_w_gu", 1), ("ffn1_w_down", 0), ("w_in", 1), ("w_a_proj", 1), ("w_b_proj", 1), ("w_out", 0),
                 ("ffn2_w_gu", 1), ("ffn2_w_down", 0)]
    B = {n: _Big(n, args[n][0], args["m_" + n][0], args["v_" + n][0], ax) for n, ax in big_names}
    sh16 = {n: _cast_bf16(f"cast_{n}", b.w, _tile_rows(b.sr, b.sc))[0] for n, b in B.items()}

    x2d = x[0]
    tgt = loss_target[0]
    tm_r = _pick(T, 256)
    tm = _pick(T, 1024)
    tm_h = _pick(T, 512)
    tn = _pick128(D, 1024)
    nj = D // tn
    tng = _pick128(D, 512)
    njg = D // tng
    tnf = _pick128(F, 512)
    nf = F // tnf
    tk = _pick(T, 1024)
    tk2 = _pick(T, 2048)
    tnw = _pick128(W, 1024)
    tkd = _pick128(D, 512)
    nkd = D // tkd

    def ffn_up(tag, xb_, wgu, carry=None):
        def epi(accs, ex):
            g, u = accs
            return [g, u, (g * jax.nn.sigmoid(g)) * u]

        return _mm(f"{tag}_up", [(xb_, wgu, 0, 0, 0), (xb_, wgu, 0, 0, nf)], M=T, N=F, tm=tm, tn=tnf, tk=D, nk=1, epilogue=epi,
                   outs=[(F, BF16, 0)] * 3, sum_pairs=False, carry=carry)

    def ffn_down(tag, act, wd, carry=None):
        return _mm(f"{tag}_down", [(act, wd, 0, 0, 0)], M=T, N=D, tm=tm, tn=_pick128(D, 512), tk=F, nk=1, epilogue=_ident,
                   outs=[(D, F32, 0)], carry=carry)

    def ffn_dact(tag, drh, wd, gte, up, carry=None):
        def epi(accs, ex):
            da = accs[0]
            g, u = ex[0].astype(F32), ex[1].astype(F32)
            s = jax.nn.sigmoid(g)
            return [da * u * (s * (1.0 + g * (1.0 - s))), da * (g * s)]

        return _mm(f"{tag}_dact", [(drh, wd, 0, 0, 0)], tb=True, M=T, N=F, tm=tm, tn=tnf, tk=D, nk=1, epilogue=epi,
                   outs=[(F, BF16, 0)] * 2, extras=[(gte, "mn", 0), (up, "mn", 0)], carry=carry)

    def ffn_dwdown(tag, act, drh, carry=None):
        return _wgrad(f"{tag}_dwdown", act, drh, M=F, N=D, T=T, tm=tnf, tn=D, tk=tk, carry=carry)

    def ffn_dwgate(tag, xb_, dg, carry=None):
        return _wgrad(f"{tag}_dwgate", xb_, dg, M=D, N=F, T=T, tm=D, tn=tnf, tk=tk, ncols=2 * F, carry=carry)

    def ffn_dwup(tag, xb_, du, into, carry=None):
        return _wgrad(f"{tag}_dwup", xb_, du, M=D, N=F, T=T, tm=D, tn=tnf, tk=tk, ncols=2 * F, into=into, joff=nf, carry=carry)

    def ffn_dx(tag, parts, wgu, addends, carry=None):
        pairs = [(da, wgu, 0, (F // tnf) * (1 if which == "up" else 0), 0) for which, da in parts]

        def epi(accs, ex):
            tot = accs[0] + ALPHA * ex[0]
            for e in ex[1:]:
                tot = tot + e
            return [tot]

        return _mm(f"{tag}_dx_{'_'.join(w_ for w_, _ in parts)}", pairs, tb=True, M=T, N=D, tm=tm_h, tn=D, tk=tnf, nk=F // tnf,
                   epilogue=epi if addends else _ident, outs=[(D, F32, 0)], extras=[(a, "mn", 0) for a in addends], carry=carry)

    wdw_pad = jnp.pad(conv_w_dw[0], ((0, HALO - KW), (0, 0)))
    c0, un = _merge(B["ffn1_w_gu"].gather_ici(sh16["ffn1_w_gu"]), _conv_w_allgather(wdw_pad, conv_w_dw.shape[2]))
    (wgu1,), (wdw_full,) = un(_exchange("gather_first", c0))
    (xb,), (wgu1,) = _cast_bf16("cast_x", x2d, tm_r, carry=B["ffn1_w_gu"].gather_d2d(wgu1))

    c, un = _merge(B["ffn1_w_down"].gather_ici(sh16["ffn1_w_down"]), B["w_in"].gather_ici(sh16["w_in"]))
    (g1, u1, a1), co = ffn_up("ffn1", xb, wgu1, carry=c)
    (wd1,), (win,) = un(co)
    (wd1,) = _exchange("gather_d2d_ffn1_w_down", B["ffn1_w_down"].gather_d2d(wd1))
    c, un = _merge(B["w_a_proj"].gather_ici(sh16["w_a_proj"]), B["w_b_proj"].gather_ici(sh16["w_b_proj"]),
                   B["w_out"].gather_ici(sh16["w_out"]), B["w_in"].gather_d2d(win))
    (fo1,), co = ffn_down("ffn1", a1, wd1, carry=c)
    (wa,), (wb,), (wout,), (win,) = un(co)
    c, un = _merge(B["w_a_proj"].gather_d2d(wa), B["w_b_proj"].gather_d2d(wb), B["w_out"].gather_d2d(wout))
    (x1, x1b, xh1, rs1), co = _ln_fwd("ln1", x2d, fo1, ln1_g, ln1_b, 0.5, tm_r, carry=c)
    (wa,), (wb,), (wout,) = un(co)

    (proj,), (wgu2,) = _mm("in_proj", [(x1b, win, 0, 0, 0)], M=T, N=4 * D, tm=tm, tn=tn, tk=D, nk=1,
                           epilogue=lambda accs, ex: [accs[0] + ex[0]], outs=[(4 * D, F32, 0)], extras=[(b_in, "n", 0)],
                           carry=B["ffn2_w_gu"].gather_ici(sh16["ffn2_w_gu"]))
    wm = sgu_w_s[0]
    bst = sgu_b_s[0].T
    sa, z = _sgu_fwd("sgu_fwd", proj, sgu_ln_g, sgu_ln_b, wm, bst, tm_r)
    (zc, sb), (wd2,) = _conv_fwd("conv_fwd", z, wdw_full, KW, conv_b_dw, conv_ln_g, conv_ln_b, tm_r,
                                 carry=B["ffn2_w_down"].gather_ici(sh16["ffn2_w_down"]))

    def epi_mix(accs, ex):
        ya_, yb_ = accs
        return [jax.nn.sigmoid(ex[0]) * ya_ + jax.nn.sigmoid(ex[1]) * yb_, ya_, yb_]

    (mixin, ya, yb), (wgu2,) = _mm("branch_proj", [(sa, wa, 0, 0, 0), (sb, wb, 0, 0, 0)], M=T, N=D, tm=tm, tn=tng, tk=W, nk=1,
                                   epilogue=epi_mix, outs=[(D, BF16, 0)] * 3, sum_pairs=False,
                                   extras=[(proj, "mn", 2 * njg), (proj, "mn", 3 * njg)], carry=B["ffn2_w_gu"].gather_d2d(wgu2))
    (mix,), (wd2,) = _mm("out_proj", [(mixin, wout, 0, 0, 0)], M=T, N=D, tm=tm, tn=tn, tk=D, nk=1, epilogue=_ident,
                         outs=[(D, F32, 0)], carry=B["ffn2_w_down"].gather_d2d(wd2))
    x2, x2b, xh2, rs2 = _ln_fwd("ln2", x1, mix, ln2_g, ln2_b, 1.0, tm_r)
    g2, u2, a2 = ffn_up("ffn2", x2b, wgu2)
    (fo2,) = ffn_down("ffn2", a2, wd2)
    dr3, dr3h, loss_part, dln3_g, dln3_b = _ln3_loss("ln3_loss", x2, fo2, ln3_g, ln3_b, tgt, tm_r)

    b_gu2, b_d2 = B["ffn2_w_gu"], B["ffn2_w_down"]
    dg2, du2 = ffn_dact("ffn2", dr3h, wd2, g2, u2)
    dwd2 = ffn_dwdown("ffn2", a2, dr3h)
    dwgu2, (r1_d2,) = ffn_dwgate("ffn2", x2b, dg2, carry=b_d2.rs_pair(dwd2[1]))
    dwgu2 = ffn_dwup("ffn2", x2b, du2, dwgu2)
    cs_d2 = b_d2.rs_pairsum("ffn2_w_down", dwd2[0], r1_d2, hc)
    c, un = _merge(b_gu2.rs_pair(dwgu2[1]), b_d2.rs_ici(cs_d2[1]))
    (dx2,), co = ffn_dx("ffn2", [("gate", dg2), ("up", du2)], wgu2, [dr3], carry=c)
    (r1_gu2,), (r2_d2,) = un(co)
    cs_gu2 = b_gu2.rs_pairsum("ffn2_w_gu", dwgu2[0], r1_gu2, hc)
    gh_d2 = b_d2.rs_final("ffn2_w_down", cs_d2[0], r2_d2, qc)

    (dr2, dr2b, dln2_g, dln2_b), (g_d2,) = _ln_bwd("ln2_bwd", dx2, xh2, rs2, ln2_g, 1.0, tm_r, carry=b_d2.rs_share(gh_d2))

    def epi_dmix(accs, ex):
        dm = accs[0]
        ga, gb = jax.nn.sigmoid(ex[0]), jax.nn.sigmoid(ex[1])
        ya_, yb_ = ex[2].astype(F32), ex[3].astype(F32)
        return [dm * ga, dm * gb, dm * ya_ * (ga * (1.0 - ga)), dm * yb_ * (gb * (1.0 - gb))]

    dya, dyb, dla, dlb = _mm("out_proj_bwd", [(dr2b, wout, 0, 0, 0)], tb=True, M=T, N=D, tm=tm, tn=tng, tk=D, nk=1,
                             epilogue=epi_dmix, outs=[(D, BF16, 0)] * 4,
                             extras=[(proj, "mn", 2 * njg), (proj, "mn", 3 * njg), (ya, "mn", 0), (yb, "mn", 0)])
    dwout = _wgrad("dw_out", mixin, dr2b, M=D, N=D, T=T, tm=tn, tn=tn, tk=tk2)
    (dsa,) = _mm("a_proj_bwd", [(dya, wa, 0, 0, 0)], tb=True, M=T, N=W, tm=tm, tn=tnw, tk=D, nk=1, epilogue=_ident,
                 outs=[(W, F32, 0)])
    (dsb,) = _mm("b_proj_bwd", [(dyb, wb, 0, 0, 0)], tb=True, M=T, N=W, tm=tm, tn=tnw, tk=D, nk=1, epilogue=_ident,
                 outs=[(W, F32, 0)])
    dwa = _wgrad("dw_a_proj", sa, dya, M=W, N=D, T=T, tm=tnw, tn=tn, tk=tk2)
    dwb = _wgrad("dw_b_proj", sb, dyb, M=W, N=D, T=T, tm=tnw, tn=tn, tk=tk2)

    dpa, dwm, dbst, dsgu_g, dsgu_b = _sgu_bwd("sgu_bwd", proj, dsa, sgu_ln_g, sgu_ln_b, wm, bst, tm_r)
    dzc, dcln_g, dcln_b, dbdw = _conv_ln_bwd("conv_ln_bwd", dsb, zc, conv_ln_g, conv_ln_b, tm_r)
    dpb, dwdw = _conv_bwd("conv_bwd", dzc, z, proj, wdw_full, KW, tm_r)

    dps = [dpa, dpb, dla, dlb]
    db_in = jnp.concatenate([_colsum_rows(f"db_in_{k}", dp, tm_r) for k, dp in enumerate(dps)], axis=1)
    (dx1,), (r2_gu2,) = _mm("in_proj_bwd", [(dp, win, 0, k * nkd, 0) for k, dp in enumerate(dps)], tb=True, M=T, N=D, tm=tm_h,
                            tn=D, tk=tkd, nk=nkd, epilogue=lambda accs, ex: [accs[0] + ALPHA * ex[0]], outs=[(D, F32, 0)],
                            extras=[(dr2, "mn", 0)], carry=b_gu2.rs_ici(cs_gu2[1]))
    gh_gu2 = b_gu2.rs_final("ffn2_w_gu", cs_gu2[0], r2_gu2, qc)
    dwin, (g_gu2,) = _wgrad("dw_in_0", x1b, dps[0], M=D, N=D, T=T, tm=tn, tn=tn, tk=tk2, ncols=4 * D, carry=b_gu2.rs_share(gh_gu2))
    for k in range(1, 4):
        dwin = _wgrad(f"dw_in_{k}", x1b, dps[k], M=D, N=D, T=T, tm=tn, tn=tn, tk=tk2, ncols=4 * D, into=dwin, joff=k * nj)

    mix_names = ["w_in", "w_a_proj", "w_b_proj", "w_out"]
    mix_grads = dict(zip(mix_names, [dwin, dwa, dwb, dwout], strict=True))
    c, un = _merge(*[B[n].rs_pair(mix_grads[n][1]) for n in mix_names])
    (dr1, dr1h, dln1_g, dln1_b), co = _ln_bwd("ln1_bwd", dx1, xh1, rs1, ln1_g, 0.5, tm_r, carry=c)
    cs_mix = {n: B[n].rs_pairsum(n, mix_grads[n][0], r1, hc) for n, (r1,) in zip(mix_names, un(co), strict=True)}

    small_names = ["ln1_g", "ln1_b", "b_in", "sgu_ln_g", "sgu_ln_b", "sgu_w_s", "sgu_b_s", "conv_w_dw", "conv_b_dw",
                   "conv_ln_g", "conv_ln_b", "ln2_g", "ln2_b", "ln3_g", "ln3_b"]
    small_parts = {"ln1_g": dln1_g, "ln1_b": dln1_b, "b_in": db_in, "sgu_ln_g": dsgu_g, "sgu_ln_b": dsgu_b, "sgu_w_s": dwm,
                   "sgu_b_s": dbst.T, "conv_w_dw": dwdw[:KW], "conv_b_dw": dbdw, "conv_ln_g": dcln_g, "conv_ln_b": dcln_b,
                   "ln2_g": dln2_g, "ln2_b": dln2_b, "ln3_g": dln3_g, "ln3_b": dln3_b}
    packed, spans = _pack_rows([small_parts[n] for n in small_names])

    b_gu1, b_d1 = B["ffn1_w_gu"], B["ffn1_w_down"]
    c, un = _merge(*[B[n].rs_ici(cs_mix[n][1]) for n in mix_names], _small_allgather(packed))
    (dg1, du1), co = ffn_dact("ffn1", dr1h, wd1, g1, u1, carry=c)
    *r2_mix, (small_slots,) = un(co)
    gh_mix = [B[n].rs_final(n, cs_mix[n][0], r2, qc) for n, (r2,) in zip(mix_names, r2_mix, strict=True)]
    c, un = _merge(*[B[n].rs_share(gh) for n, gh in zip(mix_names, gh_mix, strict=True)])
    dwgu1, co = ffn_dwgate("ffn1", xb, dg1, carry=c)
    g_mix = {n: g for n, (g,) in zip(mix_names, un(co), strict=True)}
    dwgu1 = ffn_dwup("ffn1", xb, du1, dwgu1)
    dwd1, (r1_gu1,) = ffn_dwdown("ffn1", a1, dr1h, carry=b_gu1.rs_pair(dwgu1[1]))
    cs_gu1 = b_gu1.rs_pairsum("ffn1_w_gu", dwgu1[0], r1_gu1, hc)
    c, un = _merge(b_gu1.rs_ici(cs_gu1[1]), b_d1.rs_pair(dwd1[1]))
    (dx_gate,), co = ffn_dx("ffn1", [("gate", dg1)], wgu1, [], carry=c)
    (r2_gu1,), (r1_d1,) = un(co)
    cs_d1 = b_d1.rs_pairsum("ffn1_w_down", dwd1[0], r1_d1, hc)
    gh_gu1 = b_gu1.rs_final("ffn1_w_gu", cs_gu1[0], r2_gu1, qc)
    c, un = _merge(b_d1.rs_ici(cs_d1[1]), b_gu1.rs_share(gh_gu1))
    (dx,), co = ffn_dx("ffn1", [("up", du1)], wgu1, [dr1, dx_gate], carry=c)
    (r2_d1,), (g_gu1,) = un(co)
    gh_d1 = b_d1.rs_final("ffn1_w_down", cs_d1[0], r2_d1, qc)

    grads = {"ffn1_w_gu": g_gu1, "ffn2_w_gu": g_gu2, "ffn2_w_down": g_d2, **g_mix}
    outs_g, outs_d, outs_m, outs_v = {}, {}, {}, {}

    def adamw_big(n, g, carry=None):
        b = B[n]
        return _adamw(f"adamw_{n}", b.w, g, b.m, b.v, _tile_rows(b.sr, b.sc, 1024 * 1024), carry=carry)

    upd = {}
    upd["ffn2_w_gu"], (grads["ffn1_w_down"],) = adamw_big("ffn2_w_gu", grads["ffn2_w_gu"], carry=b_d1.rs_share(gh_d1))
    for n, _ in big_names:
        if n not in upd:
            upd[n] = adamw_big(n, grads[n])
        d_, m_, v_ = upd[n]
        outs_g[n], outs_d[n], outs_m[n], outs_v[n] = grads[n][None], d_[None], m_[None], v_[None]
    gsum = _sum8("small_sum", small_slots)
    full_shapes = [args[n].shape if n != "conv_w_dw" else (1, KW, W) for n in small_names]
    gsmall = dict(zip(small_names, _unpack_rows(gsum, spans, full_shapes), strict=True))
    cs = conv_w_dw.shape[2]
    gsmall["conv_w_dw"] = lax.dynamic_slice_in_dim(gsmall["conv_w_dw"], q * cs, cs, axis=2)
    pw, spans2 = _pack_rows([args[n] for n in small_names])
    pg, _ = _pack_rows([gsmall[n] for n in small_names])
    pm, _ = _pack_rows([args["m_" + n] for n in small_names])
    pv, _ = _pack_rows([args["v_" + n] for n in small_names])
    pd, pmn, pvn = _adamw("adamw_small", pw, pg, pm, pv, pw.shape[0])
    shapes2 = [args[n].shape for n in small_names]
    for dst, src in ((outs_d, pd), (outs_m, pmn), (outs_v, pvn)):
        dst.update(zip(small_names, _unpack_rows(src, spans2, shapes2), strict=True))
    outs_g.update(gsmall)

    loss = lax.psum(loss_part[0, 0], ("x", "y", "c"))
    order = ["ffn1_w_gu", "ffn1_w_down", "ln1_g", "ln1_b", "w_in", "b_in", "sgu_ln_g", "sgu_ln_b", "sgu_w_s", "sgu_b_s",
             "w_a_proj", "conv_w_dw", "conv_b_dw", "conv_ln_g", "conv_ln_b", "w_b_proj", "w_out", "ln2_g", "ln2_b",
             "ffn2_w_gu", "ffn2_w_down", "ln3_g", "ln3_b"]
    return (loss, dx[None], *[outs_g[n] for n in order], *[outs_d[n] for n in order], *[outs_m[n] for n in order],
            *[outs_v[n] for n in order])
```

```python
import math

import jax
import jax.numpy as jnp
from jax import lax
from jax.experimental import pallas as pl
from jax.experimental.pallas import tpu as pltpu

BF16 = jnp.bfloat16
F32 = jnp.float32

LN_EPS = 1e-5
ALPHA = 2.0 ** 0.25
SGU_BLOCK = 128
SGU_CHUNK = 64
HALO = 32
SUBLANES = 8
LANES = 128
CONV_ROWS = 32
ADAM_LR = 0.001
ADAM_B1 = 0.9
ADAM_B2 = 0.999
ADAM_EPS = 1e-08
ADAM_WD = 0.01
ADAM_STEP = 10
N_CHIPS = 4
VMEM_LIMIT = 52 * 1024 * 1024
MESH = pl.DeviceIdType.MESH

_GELU_C0 = math.sqrt(2.0 / math.pi)
_GELU_C1 = 0.044715


def _cparams(sem):
    return pltpu.CompilerParams(dimension_semantics=sem, vmem_limit_bytes=VMEM_LIMIT)


def _gelu_parts(x):
    x2 = x * x
    t = jnp.tanh(_GELU_C0 * (x + _GELU_C1 * (x2 * x)))
    return 0.5 * (1.0 + t), t, x2


def _gelu(x):
    cdf, _, _ = _gelu_parts(x)
    return x * cdf


def _gelu_and_grad(x):
    cdf, t, x2 = _gelu_parts(x)
    grad = cdf + x * (0.5 * (1.0 - t * t)) * (_GELU_C0 * (1.0 + (3.0 * _GELU_C1) * x2))
    return x * cdf, grad


def _silu_grad(x):
    s = jax.nn.sigmoid(x)
    return s * (1.0 + x * (1.0 - s))


def _row_stats(x):
    mu = jnp.mean(x, axis=-1, keepdims=True)
    xc = x - mu
    var = jnp.mean(xc * xc, axis=-1, keepdims=True)
    rstd = lax.rsqrt(var + LN_EPS)
    return xc * rstd, rstd


def _ln_bwd_rows(dy, xhat, rstd, g):
    dxh = dy * g
    m1 = jnp.mean(dxh, axis=-1, keepdims=True)
    m2 = jnp.mean(dxh * xhat, axis=-1, keepdims=True)
    return rstd * (dxh - m1 - xhat * m2)


def _colsum(v):
    return jnp.sum(v, axis=0, keepdims=True)


class _Carry:
    def __init__(self, ro, rw, new, plan, n_remote, n_local=0):
        self.ro, self.rw, self.new, self.plan = list(ro), list(rw), list(new), plan
        self.n_remote, self.n_local = n_remote, n_local

    def sems(self):
        return [pltpu.SemaphoreType.DMA((self.n_remote,)), pltpu.SemaphoreType.DMA((self.n_remote,)),
                pltpu.SemaphoreType.DMA((max(self.n_local, 1),))]

    def copies(self, ro_refs, rw_refs, new_refs, send_sems, recv_sems, loc_sems):
        x, y, c = lax.axis_index("x"), lax.axis_index("y"), lax.axis_index("c")
        remote, local = self.plan(ro_refs, rw_refs, new_refs, x, y, c)
        assert len(remote) == self.n_remote and len(local) == self.n_local
        lcs = [pltpu.make_async_copy(s, d, loc_sems.at[k]) for k, (s, d) in enumerate(local)]
        rcs = [pltpu.make_async_remote_copy(src_ref=s, dst_ref=d, send_sem=send_sems.at[k], recv_sem=recv_sems.at[k],
                                            device_id=dev, device_id_type=MESH) for k, (s, d, dev) in enumerate(remote)]
        return lcs, rcs

    def out_shape(self):
        return [jax.ShapeDtypeStruct(a.shape, a.dtype) for a in self.rw] + self.new


def _start_all(lcs, rcs):
    for cp in lcs + rcs:
        cp.start()


def _wait_all(lcs, rcs):
    for cp in rcs:
        cp.wait_send()
    for cp in rcs:
        cp.wait_recv()
    for cp in lcs:
        cp.wait()


def _merge(*cs):
    ro = [a for c in cs for a in c.ro]
    rw = [a for c in cs for a in c.rw]
    new = [a for c in cs for a in c.new]

    def plan(ro_refs, rw_refs, new_refs, x, y, c):
        remote, local, a, b, d = [], [], 0, 0, 0
        for cc in cs:
            r, l = cc.plan(ro_refs[a:a + len(cc.ro)], rw_refs[b:b + len(cc.rw)], new_refs[d:d + len(cc.new)], x, y, c)
            a, b, d = a + len(cc.ro), b + len(cc.rw), d + len(cc.new)
            remote += r
            local += l
        return remote, local

    def unpack(couts):
        res, b, d = [], 0, len(rw)
        for cc in cs:
            res.append(list(couts[b:b + len(cc.rw)]) + list(couts[d:d + len(cc.new)]))
            b, d = b + len(cc.rw), d + len(cc.new)
        return res

    return _Carry(ro, rw, new, plan, sum(c.n_remote for c in cs), sum(c.n_local for c in cs)), unpack


_ANY = pl.BlockSpec(memory_space=pl.ANY)


def _exchange(name, carry):
    n_ro, n_rw, n_new = len(carry.ro), len(carry.rw), len(carry.new)

    def body(*refs):
        o0 = n_ro + n_rw
        lcs, rcs = carry.copies(refs[:n_ro], refs[o0:o0 + n_rw], refs[o0 + n_rw:o0 + n_rw + n_new], *refs[o0 + n_rw + n_new:])
        _start_all(lcs, rcs)
        _wait_all(lcs, rcs)

    return list(pl.pallas_call(
        body, name=name, in_specs=[_ANY] * (n_ro + n_rw), out_specs=[_ANY] * (n_rw + n_new), out_shape=carry.out_shape(),
        input_output_aliases={n_ro + k: k for k in range(n_rw)}, scratch_shapes=carry.sems())(*carry.ro, *carry.rw))


def _call(body, *, name, grid, in_specs, out_specs, out_shape, args, scratch=(), sem, carry=None, aliases=None):
    in_specs, out_specs, out_shape, scratch = list(in_specs), list(out_specs), list(out_shape), list(scratch)
    if carry is None:
        return list(pl.pallas_call(body, name=name, grid=grid, in_specs=in_specs, out_specs=out_specs, out_shape=out_shape,
                                   scratch_shapes=scratch, input_output_aliases=aliases or {},
                                   compiler_params=_cparams(sem))(*args))
    n_in, n_out, n_scr = len(in_specs), len(out_specs), len(scratch)
    n_ro, n_rw, n_new = len(carry.ro), len(carry.rw), len(carry.new)

    def wrapped(*refs):
        ins = refs[:n_in]
        ro_refs = refs[n_in:n_in + n_ro]
        o0 = n_in + n_ro + n_rw
        outs = refs[o0:o0 + n_out]
        rw_refs = refs[o0 + n_out:o0 + n_out + n_rw]
        new_refs = refs[o0 + n_out + n_rw:o0 + n_out + n_rw + n_new]
        s0 = o0 + n_out + n_rw + n_new
        scr = refs[s0:s0 + n_scr]
        sems = refs[s0 + n_scr:]
        first = pl.program_id(0) == 0
        last = pl.program_id(0) == grid[0] - 1
        for d in range(1, len(grid)):
            first = jnp.logical_and(first, pl.program_id(d) == 0)
            last = jnp.logical_and(last, pl.program_id(d) == grid[d] - 1)

        @pl.when(first)
        def _():
            _start_all(*carry.copies(ro_refs, rw_refs, new_refs, *sems))

        body(*ins, *outs, *scr)

        @pl.when(last)
        def _():
            _wait_all(*carry.copies(ro_refs, rw_refs, new_refs, *sems))

    al = dict(aliases or {})
    al.update({n_in + n_ro + k: n_out + k for k in range(n_rw)})
    res = pl.pallas_call(
        wrapped, name=name, grid=grid, in_specs=in_specs + [_ANY] * (n_ro + n_rw), out_specs=out_specs + [_ANY] * (n_rw + n_new),
        out_shape=out_shape + carry.out_shape(), scratch_shapes=scratch + carry.sems(), input_output_aliases=al,
        compiler_params=_cparams(("arbitrary",) * len(grid)))(*args, *carry.ro, *carry.rw)
    return list(res[:n_out]), list(res[n_out:])


def _mm(name, pairs, *, ta=False, tb=False, M, N, tm, tn, tk, nk, epilogue, outs, extras=(), sum_pairs=True, carry=None,
        sub=None):
    ni, nj = M // tm, N // tn
    assert ni * tm == M and nj * tn == N
    n_p = len(pairs)
    n_acc = 1 if sum_pairs else n_p
    in_specs, args = [], []
    for a, b, ak, bk, bj in pairs:
        if ta:
            in_specs.append(pl.BlockSpec((tk, tm), lambda i, j, k, ak=ak: (k + ak, i)))
        else:
            in_specs.append(pl.BlockSpec((tm, tk), lambda i, j, k, ak=ak: (i, k + ak)))
        if tb:
            in_specs.append(pl.BlockSpec((tn, tk), lambda i, j, k, bk=bk, bj=bj: (j + bj, k + bk)))
        else:
            in_specs.append(pl.BlockSpec((tk, tn), lambda i, j, k, bk=bk, bj=bj: (k + bk, j + bj)))
        args += [a, b]
    for arr, kind, jo in extras:
        if kind == "mn":
            in_specs.append(pl.BlockSpec((tm, tn), lambda i, j, k, jo=jo: (i, j + jo)))
        else:
            in_specs.append(pl.BlockSpec((1, tn), lambda i, j, k, jo=jo: (0, j + jo)))
        args.append(arr)
    out_specs = [pl.BlockSpec((tm, tn), lambda i, j, k, jo=jo: (i, j + jo)) for _, _, jo in outs]
    out_shape = [jax.ShapeDtypeStruct((M, nc), dt) for nc, dt, _ in outs]
    n_ex, n_out = len(extras), len(outs)
    dn = (((0 if ta else 1,), (1 if tb else 0,)), ((), ()))

    def body(*refs):
        ab = refs[: 2 * n_p]
        ex = refs[2 * n_p: 2 * n_p + n_ex]
        o0 = 2 * n_p + n_ex
        out_refs = refs[o0: o0 + n_out]
        acc_refs = refs[o0 + n_out:]

        def dots():
            res = []
            for p in range(n_p):
                a = ab[2 * p][...]
                b = ab[2 * p + 1][...]
                res.append(lax.dot_general(a.astype(BF16), b.astype(BF16), dn, preferred_element_type=F32))
            if sum_pairs:
                tot = res[0]
                for r in res[1:]:
                    tot = tot + r
                res = [tot]
            return res

        def finish(accs):
            tiles = epilogue(accs, [e[...] for e in ex])
            for r, t in zip(out_refs, tiles, strict=True):
                r[...] = t.astype(r.dtype)

        if nk == 1 and sub is not None and tn > sub:
            for s in range(tn // sub):
                cs = slice(s * sub, (s + 1) * sub)
                res = []
                for p in range(n_p):
                    b = ab[2 * p + 1][cs, :] if tb else ab[2 * p + 1][:, cs]
                    res.append(lax.dot_general(ab[2 * p][...].astype(BF16), b.astype(BF16), dn, preferred_element_type=F32))
                if sum_pairs:
                    tot = res[0]
                    for r in res[1:]:
                        tot = tot + r
                    res = [tot]
                tiles = epilogue(res, [e[:, cs] for e in ex])
                for r, t in zip(out_refs, tiles, strict=True):
                    r[:, cs] = t.astype(r.dtype)
        elif nk == 1:
            finish(dots())
        else:
            k = pl.program_id(2)
            d = dots()

            @pl.when(k == 0)
            def _():
                for r, v in zip(acc_refs, d, strict=True):
                    r[...] = v

            @pl.when(k > 0)
            def _():
                for r, v in zip(acc_refs, d, strict=True):
                    r[...] += v

            @pl.when(k == nk - 1)
            def _():
                finish([r[...] for r in acc_refs])

    scratch = [pltpu.VMEM((tm, tn), F32) for _ in range(n_acc)] if nk > 1 else []
    return _call(body, name=name, grid=(ni, nj, nk), in_specs=in_specs, out_specs=out_specs, out_shape=out_shape, args=args,
                 scratch=scratch, sem=("parallel", "parallel", "arbitrary"), carry=carry)


def _wgrad(name, a, b, *, M, N, T, tm, tn, tk, into=None, joff=0, ncols=None, carry=None):
    ncols = N if ncols is None else ncols
    ni, nj, nk = M // tm, N // tn, T // tk
    dn = (((0,), (0,)), ((), ()))

    def body(a_ref, b_ref, *rest):
        d = lax.dot_general(a_ref[...].astype(BF16), b_ref[...].astype(BF16), dn, preferred_element_type=F32)
        if nk == 1:
            of_ref, oh_ref = rest[-2:]
            of_ref[...] = d
            oh_ref[...] = d.astype(BF16)
            return
        of_ref, oh_ref, acc_ref = rest[-3:]
        k = pl.program_id(2)

        @pl.when(k == 0)
        def _():
            acc_ref[...] = d

        @pl.when(k > 0)
        def _():
            acc_ref[...] += d

        @pl.when(k == nk - 1)
        def _():
            of_ref[...] = acc_ref[...]
            oh_ref[...] = acc_ref[...].astype(BF16)

    ospec = pl.BlockSpec((tm, tn), lambda i, j, k: (i, j + joff))
    in_specs = [pl.BlockSpec((tk, tm), lambda i, j, k: (k, i)), pl.BlockSpec((tk, tn), lambda i, j, k: (k, j))]
    args, aliases = [a, b], None
    if into is not None:
        in_specs += [_ANY, _ANY]
        args += list(into)
        aliases = {2: 0, 3: 1}
    return _call(body, name=name, grid=(ni, nj, nk), in_specs=in_specs, out_specs=[ospec, ospec],
                 out_shape=[jax.ShapeDtypeStruct((M, ncols), F32), jax.ShapeDtypeStruct((M, ncols), BF16)], args=args,
                 scratch=[pltpu.VMEM((tm, tn), F32)] if nk > 1 else [], sem=("parallel", "parallel", "arbitrary"), carry=carry,
                 aliases=aliases)


def _rows(tm, c, cb=0):
    return pl.BlockSpec((tm, c), lambda i, cb=cb: (i, cb))


def _whole(shape):
    nd = len(shape)
    return pl.BlockSpec(shape, lambda i, nd=nd: (0,) * nd)


def _cast_bf16(name, x, tm, carry=None):
    t, d = x.shape

    def body(x_ref, o_ref):
        o_ref[...] = x_ref[...].astype(BF16)

    return _call(body, name=name, grid=(t // tm,), in_specs=[_rows(tm, d)], out_specs=[_rows(tm, d)],
                 out_shape=[jax.ShapeDtypeStruct((t, d), BF16)], args=[x], sem=("parallel",), carry=carry)


def _ln_fwd(name, xres, f, g, b, cf, tm, carry=None):
    t, d = xres.shape

    def body(x_ref, f_ref, g_ref, b_ref, y_ref, yb_ref, xh_ref, rs_ref):
        r = ALPHA * x_ref[...] + cf * f_ref[...]
        xhat, rstd = _row_stats(r)
        y = xhat * g_ref[...] + b_ref[...]
        y_ref[...] = y
        yb_ref[...] = y.astype(BF16)
        xh_ref[...] = xhat
        rs_ref[...] = rstd

    return _call(
        body, name=name, grid=(t // tm,), in_specs=[_rows(tm, d), _rows(tm, d), _whole((1, d)), _whole((1, d))],
        out_specs=[_rows(tm, d), _rows(tm, d), _rows(tm, d), _rows(tm, 1)],
        out_shape=[jax.ShapeDtypeStruct((t, d), F32), jax.ShapeDtypeStruct((t, d), BF16),
                   jax.ShapeDtypeStruct((t, d), F32), jax.ShapeDtypeStruct((t, 1), F32)],
        args=[xres, f, g, b], sem=("parallel",), carry=carry)


def _ln_bwd(name, dy, xhat, rstd, g, scale, tm, carry=None):
    t, d = dy.shape

    def body(dy_ref, xh_ref, rs_ref, g_ref, dr_ref, drb_ref, dg_ref, db_ref):
        i = pl.program_id(0)
        dy_v, xh = dy_ref[...], xh_ref[...]
        dr = _ln_bwd_rows(dy_v, xh, rs_ref[...], g_ref[...])
        dr_ref[...] = dr
        drb_ref[...] = (scale * dr).astype(BF16)

        @pl.when(i == 0)
        def _():
            dg_ref[...] = jnp.zeros_like(dg_ref)
            db_ref[...] = jnp.zeros_like(db_ref)

        dg_ref[...] += _colsum(dy_v * xh)
        db_ref[...] += _colsum(dy_v)

    return _call(
        body, name=name, grid=(t // tm,), in_specs=[_rows(tm, d), _rows(tm, d), _rows(tm, 1), _whole((1, d))],
        out_specs=[_rows(tm, d), _rows(tm, d), _whole((1, d)), _whole((1, d))],
        out_shape=[jax.ShapeDtypeStruct((t, d), F32), jax.ShapeDtypeStruct((t, d), BF16),
                   jax.ShapeDtypeStruct((1, d), F32), jax.ShapeDtypeStruct((1, d), F32)],
        args=[dy, xhat, rstd, g], sem=("arbitrary",), carry=carry)


def _ln3_loss(name, xres, f, g, b, target, tm):
    t, d = xres.shape

    def body(x_ref, f_ref, g_ref, b_ref, tg_ref, dr_ref, drb_ref, loss_ref, dg_ref, db_ref):
        i = pl.program_id(0)
        r = ALPHA * x_ref[...] + 0.5 * f_ref[...]
        xhat, rstd = _row_stats(r)
        gv = g_ref[...]
        y = xhat * gv + b_ref[...]
        err = y - tg_ref[...]
        dy = err * (1.0 / d)
        dr = _ln_bwd_rows(dy, xhat, rstd, gv)
        dr_ref[...] = dr
        drb_ref[...] = (0.5 * dr).astype(BF16)
        part = 0.5 * jnp.sum(jnp.mean(err * err, axis=-1, keepdims=True), axis=0, keepdims=True)

        @pl.when(i == 0)
        def _():
            loss_ref[...] = jnp.zeros_like(loss_ref)
            dg_ref[...] = jnp.zeros_like(dg_ref)
            db_ref[...] = jnp.zeros_like(db_ref)

        loss_ref[...] += jnp.broadcast_to(part, loss_ref.shape)
        dg_ref[...] += _colsum(dy * xhat)
        db_ref[...] += _colsum(dy)

    return _call(
        body, name=name, grid=(t // tm,),
        in_specs=[_rows(tm, d), _rows(tm, d), _whole((1, d)), _whole((1, d)), _rows(tm, d)],
        out_specs=[_rows(tm, d), _rows(tm, d), _whole((8, 128)), _whole((1, d)), _whole((1, d))],
        out_shape=[jax.ShapeDtypeStruct((t, d), F32), jax.ShapeDtypeStruct((t, d), BF16),
                   jax.ShapeDtypeStruct((8, 128), F32), jax.ShapeDtypeStruct((1, d), F32),
                   jax.ShapeDtypeStruct((1, d), F32)],
        args=[xres, f, g, b, target], sem=("arbitrary",))


def _colsum_rows(name, x, tm):
    t, d = x.shape

    def body(x_ref, o_ref):
        @pl.when(pl.program_id(0) == 0)
        def _():
            o_ref[...] = jnp.zeros_like(o_ref)

        o_ref[...] += _colsum(x_ref[...].astype(F32))

    return _call(body, name=name, grid=(t // tm,), in_specs=[_rows(tm, d)], out_specs=[_whole((1, d))],
                 out_shape=[jax.ShapeDtypeStruct((1, d), F32)], args=[x], sem=("arbitrary",))[0]


def _sgu_mask():
    sh = SGU_CHUNK.bit_length() - 1
    r = lax.shift_right_logical(lax.broadcasted_iota(jnp.int32, (SGU_BLOCK, SGU_BLOCK), 0), sh)
    c = lax.shift_right_logical(lax.broadcasted_iota(jnp.int32, (SGU_BLOCK, SGU_BLOCK), 1), sh)
    return c <= r


def _sgu_fwd(name, p, lng, lnb, wm, bst, tm):
    t = p.shape[0]
    n_grp, w = wm.shape[0], lng.shape[1]
    hd = w // n_grp
    nblk = tm // SGU_BLOCK

    def body(uv_ref, h_ref, g_ref, b_ref, wm_ref, bs_ref, sa_ref, z_ref, vn_s):
        xhat, _ = _row_stats(_gelu(uv_ref[:, w:]))
        vn_s[...] = (xhat * g_ref[...] + b_ref[...]).astype(BF16)
        mask = _sgu_mask()
        for h in range(n_grp):
            wh = jnp.where(mask, wm_ref[h], 0.0).astype(BF16)
            bcol = bs_ref[:, h:h + 1]
            cs = slice(h * hd, (h + 1) * hd)
            for n in range(nblk):
                rs = slice(n * SGU_BLOCK, (n + 1) * SGU_BLOCK)
                s = jnp.dot(wh, vn_s[rs, cs], preferred_element_type=F32) + bcol
                sa_ref[rs, cs] = (_gelu(uv_ref[rs, cs]) * s).astype(BF16)
        z_ref[...] = h_ref[:, :w] * jax.nn.sigmoid(h_ref[:, w:])

    return _call(
        body, name=name, grid=(t // tm,),
        in_specs=[_rows(tm, 2 * w, 0), _rows(tm, 2 * w, 1), _whole((1, w)), _whole((1, w)), _whole(wm.shape),
                  _whole(bst.shape)],
        out_specs=[_rows(tm, w), _rows(tm, w)],
        out_shape=[jax.ShapeDtypeStruct((t, w), BF16), jax.ShapeDtypeStruct((t, w), F32)],
        args=[p, p, lng, lnb, wm, bst], scratch=[pltpu.VMEM((tm, w), BF16)], sem=("parallel",))


def _sgu_bwd(name, p, dsa, lng, lnb, wm, bst, tm):
    t = p.shape[0]
    n_grp, w = wm.shape[0], lng.shape[1]
    hd = w // n_grp
    nblk = tm // SGU_BLOCK

    def body(uv_ref, dsa_ref, g_ref, b_ref, wm_ref, bs_ref, dp_ref, dwm_ref, dbs_ref, dg_ref, db_ref,
             vn_s, ug_s, dvn_s, dug_s):
        i = pl.program_id(0)

        @pl.when(i == 0)
        def _():
            dwm_ref[...] = jnp.zeros_like(dwm_ref)
            dbs_ref[...] = jnp.zeros_like(dbs_ref)
            dg_ref[...] = jnp.zeros_like(dg_ref)
            db_ref[...] = jnp.zeros_like(db_ref)

        ug, dgelu_u = _gelu_and_grad(uv_ref[:, :w])
        ug_s[...] = ug
        vg, dgelu_v = _gelu_and_grad(uv_ref[:, w:])
        xhat, rstd = _row_stats(vg)
        gv = g_ref[...]
        vn_s[...] = (xhat * gv + b_ref[...]).astype(BF16)
        mask = _sgu_mask()
        for h in range(n_grp):
            wh = jnp.where(mask, wm_ref[h], 0.0).astype(BF16)
            bcol = bs_ref[:, h:h + 1]
            cs = slice(h * hd, (h + 1) * hd)
            dw_h = jnp.zeros((SGU_BLOCK, SGU_BLOCK), F32)
            dbs_h = jnp.zeros((SGU_BLOCK, 1), F32)
            for n in range(nblk):
                rs = slice(n * SGU_BLOCK, (n + 1) * SGU_BLOCK)
                vblk = vn_s[rs, cs]
                s = jnp.dot(wh, vblk, preferred_element_type=F32) + bcol
                dsa_blk = dsa_ref[rs, cs]
                dug_s[rs, cs] = dsa_blk * s
                ds = dsa_blk * ug_s[rs, cs]
                dsb = ds.astype(BF16)
                dvn_s[rs, cs] = lax.dot_general(wh, dsb, (((0,), (0,)), ((), ())), preferred_element_type=F32)
                dw_h = dw_h + lax.dot_general(dsb, vblk, (((1,), (1,)), ((), ())), preferred_element_type=F32)
                dbs_h = dbs_h + jnp.sum(ds, axis=1, keepdims=True)
            dwm_ref[h] += jnp.where(mask, dw_h, 0.0)
            dbs_ref[:, h:h + 1] += dbs_h
        dvn = dvn_s[...]
        dg_ref[...] += _colsum(dvn * xhat)
        db_ref[...] += _colsum(dvn)
        dvg = _ln_bwd_rows(dvn, xhat, rstd, gv)
        dp_ref[:, :w] = (dug_s[...] * dgelu_u).astype(BF16)
        dp_ref[:, w:] = (dvg * dgelu_v).astype(BF16)

    return _call(
        body, name=name, grid=(t // tm,),
        in_specs=[_rows(tm, 2 * w, 0), _rows(tm, w), _whole((1, w)), _whole((1, w)), _whole(wm.shape), _whole(bst.shape)],
        out_specs=[_rows(tm, 2 * w), _whole(wm.shape), _whole(bst.shape), _whole((1, w)), _whole((1, w))],
        out_shape=[jax.ShapeDtypeStruct((t, 2 * w), BF16), jax.ShapeDtypeStruct(wm.shape, F32),
                   jax.ShapeDtypeStruct(bst.shape, F32), jax.ShapeDtypeStruct((1, w), F32), jax.ShapeDtypeStruct((1, w), F32)],
        args=[p, dsa, lng, lnb, wm, bst],
        scratch=[pltpu.VMEM((tm, w), BF16), pltpu.VMEM((tm, w), F32), pltpu.VMEM((tm, w), F32), pltpu.VMEM((tm, w), F32)],
        sem=("arbitrary",))


def _halo_prev(tm, c):
    return pl.BlockSpec((HALO, c), lambda i: (jnp.maximum(i * (tm // HALO) - 1, 0), 0))


def _halo_next(tm, c, t):
    last = t // HALO - 1
    return pl.BlockSpec((HALO, c), lambda i: (jnp.minimum((i + 1) * (tm // HALO), last), 0))


def _shifted_copies(sh, n):
    for r in range(1, SUBLANES):
        sh[r, :n - SUBLANES, :] = sh[0, r:r + n - SUBLANES, :]


def _row_broadcasts(wb, w_ref, kw):
    for k in range(kw):
        wb[k] = jnp.broadcast_to(w_ref[k:k + 1, :], wb.shape[1:])


def _tap(sh, r0, o, rows, cols):
    return sh[o % SUBLANES, pl.ds(pl.multiple_of(r0 + (o - o % SUBLANES), SUBLANES), rows), cols]


def _conv_fwd(name, z, wdw, kw, bdw, lng, lnb, tm, carry=None):
    t, c = z.shape
    lead = HALO - (kw - 1)
    n = tm + HALO

    def body(zp_ref, z_ref, w_ref, bdw_ref, g_ref, b_ref, zc_ref, sb_ref, sh, wb):
        i = pl.program_id(0)
        sh[0, :HALO, :] = jnp.where(i > 0, zp_ref[...], 0.0)
        sh[0, HALO:, :] = z_ref[...]
        _shifted_copies(sh, n)
        _row_broadcasts(wb, w_ref, kw)
        bias = jnp.broadcast_to(bdw_ref[...], (SUBLANES, c))
        groups = CONV_ROWS // SUBLANES

        def chunk(ci, _):
            r0 = pl.multiple_of(ci * CONV_ROWS, CONV_ROWS)
            accs = [bias] * groups
            for k in range(kw):
                wk = wb[k]
                tp = _tap(sh, r0, lead + k, CONV_ROWS, slice(None))
                accs = [accs[g] + wk * tp[g * SUBLANES:(g + 1) * SUBLANES] for g in range(groups)]
            zc_ref[pl.ds(r0, CONV_ROWS), :] = jnp.concatenate(accs, axis=0)
            return 0

        lax.fori_loop(0, tm // CONV_ROWS, chunk, 0)
        xhat, _ = _row_stats(zc_ref[...])
        zn = xhat * g_ref[...] + b_ref[...]
        sb_ref[...] = (zn * jax.nn.sigmoid(zn)).astype(BF16)

    return _call(
        body, name=name, grid=(t // tm,),
        in_specs=[_halo_prev(tm, c), _rows(tm, c), _whole(wdw.shape), _whole((1, c)), _whole((1, c)), _whole((1, c))],
        out_specs=[_rows(tm, c), _rows(tm, c)],
        out_shape=[jax.ShapeDtypeStruct((t, c), F32), jax.ShapeDtypeStruct((t, c), BF16)],
        args=[z, z, wdw, bdw, lng, lnb], scratch=[pltpu.VMEM((SUBLANES, n, c), F32), pltpu.VMEM((HALO, SUBLANES, c), F32)],
        sem=("parallel",), carry=carry)


def _conv_ln_bwd(name, dsb, zc, lng, lnb, tm):
    t, c = zc.shape

    def body(dsb_ref, zc_ref, g_ref, b_ref, dzc_ref, dg_ref, db_ref, dbdw_ref):
        i = pl.program_id(0)
        xhat, rstd = _row_stats(zc_ref[...])
        gv = g_ref[...]
        zn = xhat * gv + b_ref[...]
        dzn = dsb_ref[...] * _silu_grad(zn)
        dzc = _ln_bwd_rows(dzn, xhat, rstd, gv)
        dzc_ref[...] = dzc

        @pl.when(i == 0)
        def _():
            dg_ref[...] = jnp.zeros_like(dg_ref)
            db_ref[...] = jnp.zeros_like(db_ref)
            dbdw_ref[...] = jnp.zeros_like(dbdw_ref)

        dg_ref[...] += _colsum(dzn * xhat)
        db_ref[...] += _colsum(dzn)
        dbdw_ref[...] += _colsum(dzc)

    return _call(
        body, name=name, grid=(t // tm,), in_specs=[_rows(tm, c), _rows(tm, c), _whole((1, c)), _whole((1, c))],
        out_specs=[_rows(tm, c), _whole((1, c)), _whole((1, c)), _whole((1, c))],
        out_shape=[jax.ShapeDtypeStruct((t, c), F32)] + [jax.ShapeDtypeStruct((1, c), F32)] * 3,
        args=[dsb, zc, lng, lnb], sem=("arbitrary",))


def _conv_bwd(name, dzc, z, p, wdw, kw, tm):
    t, c = z.shape
    n_i = t // tm
    n = tm + HALO

    def body(dzc_ref, dzn_ref, z_ref, h_ref, w_ref, dp_ref, dw_ref, sh, dz_s, wb):
        i = pl.program_id(0)

        @pl.when(i == 0)
        def _():
            dw_ref[...] = jnp.zeros_like(dw_ref)

        sh[0, :tm, :] = dzc_ref[...]
        sh[0, tm:, :] = jnp.where(i < n_i - 1, dzn_ref[...], 0.0)
        _shifted_copies(sh, n)
        _row_broadcasts(wb, w_ref, kw)
        pair = 2 * SUBLANES
        for lc in range(c // LANES):
            cols = slice(lc * LANES, (lc + 1) * LANES)

            def rowv(rv, accs, cols=cols):
                r0 = pl.multiple_of(rv * pair, pair)
                zv = z_ref[pl.ds(r0, pair), cols]
                z0, z1 = zv[:SUBLANES], zv[SUBLANES:]
                ways = 4
                dz0, dz1, new = [None] * ways, [None] * ways, []
                for k in range(kw):
                    s = _tap(sh, r0, kw - 1 - k, pair, cols)
                    s0, s1 = s[:SUBLANES], s[SUBLANES:]
                    wk = wb[k, :, cols]
                    j = k % ways
                    dz0[j] = wk * s0 if dz0[j] is None else dz0[j] + wk * s0
                    dz1[j] = wk * s1 if dz1[j] is None else dz1[j] + wk * s1
                    new.append((accs[k] + z0 * s0) + z1 * s1)
                dz_s[pl.ds(r0, pair), cols] = jnp.concatenate([(dz0[0] + dz0[1]) + (dz0[2] + dz0[3]),
                                                               (dz1[0] + dz1[1]) + (dz1[2] + dz1[3])], axis=0)
                return tuple(new)

            accs = lax.fori_loop(0, tm // pair, rowv, tuple(jnp.zeros((SUBLANES, LANES), F32) for _ in range(kw)))
            for k in range(kw):
                dw_ref[k:k + 1, cols] += _colsum(accs[k])
        dz = dz_s[...]
        a, g = h_ref[:, :c], h_ref[:, c:]
        sg = jax.nn.sigmoid(g)
        dp_ref[:, :c] = (dz * sg).astype(BF16)
        dp_ref[:, c:] = (dz * a * (sg * (1.0 - sg))).astype(BF16)

    return _call(
        body, name=name, grid=(n_i,),
        in_specs=[_rows(tm, c), _halo_next(tm, c, t), _rows(tm, c), _rows(tm, 2 * c, 1), _whole(wdw.shape)],
        out_specs=[_rows(tm, 2 * c), _whole((HALO, c))],
        out_shape=[jax.ShapeDtypeStruct((t, 2 * c), BF16), jax.ShapeDtypeStruct((HALO, c), F32)],
        args=[dzc, dzc, z, p, wdw],
        scratch=[pltpu.VMEM((SUBLANES, n, c), F32), pltpu.VMEM((tm, c), F32), pltpu.VMEM((HALO, SUBLANES, c), F32)],
        sem=("arbitrary",))


def _adamw(name, w, g, m, v, tr, carry=None):
    r, c = w.shape
    c1 = 1.0 - ADAM_B1 ** ADAM_STEP
    c2 = 1.0 - ADAM_B2 ** ADAM_STEP

    def body(w_ref, g_ref, m_ref, v_ref, d_ref, mo_ref, vo_ref):
        gv = g_ref[...]
        mn = ADAM_B1 * m_ref[...] + (1.0 - ADAM_B1) * gv
        vn = ADAM_B2 * v_ref[...] + (1.0 - ADAM_B2) * (gv * gv)
        d_ref[...] = -ADAM_LR * ((mn / c1) / (jnp.sqrt(vn / c2) + ADAM_EPS) + ADAM_WD * w_ref[...])
        mo_ref[...] = mn
        vo_ref[...] = vn

    spec = _rows(tr, c)
    return _call(body, name=name, grid=(r // tr,), in_specs=[spec] * 4, out_specs=[spec] * 3,
                 out_shape=[jax.ShapeDtypeStruct((r, c), F32)] * 3, args=[w, g, m, v], sem=("parallel",), carry=carry)


def _pair_sum(name, mine, recv, hc, *, out_rows, tr, row_map):
    c_ = mine.shape[1]

    def body(hc_ref, a_ref, b_ref, of_ref, oh_ref):
        s = a_ref[...] + b_ref[...].astype(F32)
        of_ref[...] = s
        oh_ref[...] = s.astype(BF16)

    ispec = pl.BlockSpec((tr, c_), lambda i, hc_ref: (row_map(i, hc_ref[0]), 0))
    ospec = pl.BlockSpec((tr, c_), lambda i, hc_ref: (i, 0))
    return pl.pallas_call(
        body, name=name,
        grid_spec=pltpu.PrefetchScalarGridSpec(num_scalar_prefetch=1, grid=(out_rows // tr,), in_specs=[ispec, ispec],
                                               out_specs=[ospec, ospec]),
        out_shape=[jax.ShapeDtypeStruct((out_rows, c_), F32), jax.ShapeDtypeStruct((out_rows, c_), BF16)],
        compiler_params=_cparams(("parallel",)))(hc, mine, recv)


def _final_sum(name, cs32, recv2, qc, *, part_shape, out_shape, tr, in_map, out_map):
    pr, pc = part_shape

    def body(qc_ref, a_ref, r_ref, o_ref):
        o_ref[...] = ((a_ref[...] + r_ref[0].astype(F32)) + r_ref[1].astype(F32)) + r_ref[2].astype(F32)

    return pl.pallas_call(
        body, name=name,
        grid_spec=pltpu.PrefetchScalarGridSpec(
            num_scalar_prefetch=1, grid=(pr // tr,),
            in_specs=[pl.BlockSpec((tr, pc), lambda i, qc_ref: in_map(i, qc_ref[0], qc_ref[1])),
                      pl.BlockSpec((3, tr, pc), lambda i, qc_ref: (0, i, 0))],
            out_specs=pl.BlockSpec((tr, pc), lambda i, qc_ref: out_map(i, qc_ref[0], qc_ref[1]))),
        out_shape=jax.ShapeDtypeStruct(out_shape, F32), compiler_params=_cparams(("parallel",)))(qc, cs32, recv2)


def _sum8(name, slots):
    def body(s_ref, o_ref):
        acc = s_ref[0]
        for d in range(1, 8):
            acc = acc + s_ref[d]
        o_ref[...] = acc

    return pl.pallas_call(body, name=name, out_shape=jax.ShapeDtypeStruct(slots.shape[1:], F32),
                          in_specs=[pl.BlockSpec(memory_space=pltpu.VMEM)], out_specs=pl.BlockSpec(memory_space=pltpu.VMEM),
                          compiler_params=pltpu.CompilerParams(vmem_limit_bytes=VMEM_LIMIT))(slots)


def _chips(x, y):
    return [(1 - x, y), (x, 1 - y), (1 - x, 1 - y)]


class _Big:
    def __init__(self, name, w, m, v, ax):
        self.name, self.w, self.m, self.v, self.ax = name, w, m, v, ax
        sr, sc = w.shape
        self.R, self.C = (sr * N_CHIPS, sc) if ax == 0 else (sr, sc * N_CHIPS)
        self.sr, self.sc = sr, sc
        self.hr = sr // 2

    def slot(self, ref, q, h=None):
        if self.ax == 1:
            cols = pl.ds(pl.multiple_of(q * self.sc, 128), self.sc)
            return ref.at[:, cols] if h is None else ref.at[pl.ds(pl.multiple_of(h * self.hr, 16), self.hr), cols]
        if h is None:
            return ref.at[pl.ds(pl.multiple_of(q * self.sr, 16), self.sr), :]
        return ref.at[pl.ds(pl.multiple_of(q * self.sr + h * self.hr, 16), self.hr), :]

    def half(self, ref, h):
        return ref.at[pl.ds(pl.multiple_of(h * self.hr, 16), self.hr), :]

    def part(self, ref, q):
        if self.ax == 1:
            return ref.at[:, pl.ds(pl.multiple_of(q * self.sc, 128), self.sc)]
        return ref.at[pl.ds(pl.multiple_of(q * self.hr, 16), self.hr), :]

    @property
    def part_shape(self):
        return (self.hr, self.sc)

    def gather_ici(self, shard16):
        def plan(ro, rw, new, x, y, c):
            q = 2 * x + y
            remote = [(self.half(ro[0], c), self.slot(new[0], q, c), (cx, cy, c)) for cx, cy in _chips(x, y)]
            return remote, [(ro[0], self.slot(new[0], q))]

        return _Carry([shard16], [], [jax.ShapeDtypeStruct((self.R, self.C), BF16)], plan, 3, 1)

    def gather_d2d(self, full):
        def plan(ro, rw, new, x, y, c):
            remote = []
            for cx, cy in _chips(x, y):
                piece = self.slot(rw[0], 2 * cx + cy, c)
                remote.append((piece, piece, (x, y, 1 - c)))
            return remote, []

        return _Carry([], [full], [], plan, 3)

    def rs_pair(self, g16):
        def plan(ro, rw, new, x, y, c):
            sib = (x, y, 1 - c)
            if self.ax == 1:
                rows = pl.ds(pl.multiple_of((1 - c) * self.hr, 16), self.hr)
                return [(ro[0].at[rows, :], new[0].at[rows, :], sib)], []
            return [(self.slot(ro[0], q, 1 - c), self.slot(new[0], q, 1 - c), sib) for q in range(N_CHIPS)], []

        return _Carry([g16], [], [jax.ShapeDtypeStruct((self.R, self.C), BF16)], plan, 1 if self.ax == 1 else N_CHIPS)

    def rs_pairsum(self, tag, g32, recv1, hc):
        tr = _tile_rows(self.hr, self.C)
        nb = self.hr // tr
        if self.ax == 1:
            row_map = lambda i, c: c * nb + i
        else:
            row_map = lambda i, c: ((i // nb) * 2 + c) * nb + i % nb
        return _pair_sum(f"rs_pairsum_{tag}", g32, recv1, hc, out_rows=self.R // 2, tr=tr, row_map=row_map)

    def rs_ici(self, cs16):
        def plan(ro, rw, new, x, y, c):
            return [(self.part(ro[0], 2 * cx + cy), new[0].at[k], (cx, cy, c)) for k, (cx, cy) in enumerate(_chips(x, y))], []

        return _Carry([cs16], [], [jax.ShapeDtypeStruct((3,) + self.part_shape, BF16)], plan, 3)

    def rs_final(self, tag, cs32, recv2, qc):
        tr = _tile_rows(self.hr, self.sc)
        nb = self.hr // tr
        if self.ax == 1:
            in_map = lambda i, q, c: (i, q)
        else:
            in_map = lambda i, q, c: (q * nb + i, 0)
        out_map = lambda i, q, c: (c * nb + i, 0)
        return _final_sum(f"rs_final_{tag}", cs32, recv2, qc, part_shape=self.part_shape, out_shape=(self.sr, self.sc),
                          tr=tr, in_map=in_map, out_map=out_map)

    def rs_share(self, ghalf):
        def plan(ro, rw, new, x, y, c):
            piece = self.half(rw[0], c)
            return [(piece, piece, (x, y, 1 - c))], []

        return _Carry([], [ghalf], [], plan, 1)


def _small_allgather(packed):
    nr = packed.shape[0]

    def plan(ro, rw, new, x, y, c):
        me = 4 * x + 2 * y + c
        remote = []
        for fx in (0, 1):
            for fy in (0, 1):
                for fc in (0, 1):
                    if fx or fy or fc:
                        dev = (1 - x if fx else x, 1 - y if fy else y, 1 - c if fc else c)
                        remote.append((ro[0], new[0].at[me], dev))
        return remote, [(ro[0], new[0].at[me])]

    return _Carry([packed], [], [jax.ShapeDtypeStruct((8, nr, 128), F32)], plan, 7, 1)


def _conv_w_allgather(padded, cs):
    def plan(ro, rw, new, x, y, c):
        cols = pl.ds(pl.multiple_of((2 * x + y) * cs, 128), cs)
        remote = [(ro[0], new[0].at[:, cols], (cx, cy, c)) for cx, cy in _chips(x, y)]
        return remote, [(ro[0], new[0].at[:, cols])]

    return _Carry([padded], [], [jax.ShapeDtypeStruct((HALO, cs * N_CHIPS), F32)], plan, 3, 1)


def _pick(n, want):
    if n <= want:
        return n
    for t in range(want, 15, -16):
        if t % 16 == 0 and n % t == 0:
            return t
    raise ValueError(f"no tile for {n} (want {want})")


def _tile_rows(nrows, ncols, budget=2 * 1024 * 1024):
    return _pick(nrows, max(16, (budget // (4 * ncols)) // 16 * 16))


def _pick128(n, want):
    if n <= want:
        return n
    for t in range(want, 127, -128):
        if n % t == 0:
            return t
    raise ValueError(f"no lane tile for {n} (want {want})")


def _pack_rows(parts):
    out, spans, r0 = [], [], 0
    for p in parts:
        flat = p.reshape(-1).astype(F32)
        n = flat.shape[0]
        rows = -(-n // 1024) * 8
        flat = jnp.pad(flat, (0, rows * 128 - n))
        out.append(flat.reshape(rows, 128))
        spans.append((r0, rows, n))
        r0 += rows
    return jnp.concatenate(out, axis=0), spans


def _unpack_rows(packed, spans, shapes):
    res = []
    for (r0, rows, n), shp in zip(spans, shapes, strict=True):
        res.append(packed[r0:r0 + rows].reshape(-1)[:n].reshape(shp))
    return res


def _ident(accs, ex):
    return [accs[0]]


def kernel(x, ffn1_w_gu, ffn1_w_down, ln1_g, ln1_b, w_in, b_in, sgu_ln_g, sgu_ln_b, sgu_w_s, sgu_b_s, w_a_proj, conv_w_dw, conv_b_dw, conv_ln_g, conv_ln_b, w_b_proj, w_out, ln2_g, ln2_b, ffn2_w_gu, ffn2_w_down, ln3_g, ln3_b, loss_target, m_ffn1_w_gu, m_ffn1_w_down, m_ln1_g, m_ln1_b, m_w_in, m_b_in, m_sgu_ln_g, m_sgu_ln_b, m_sgu_w_s, m_sgu_b_s, m_w_a_proj, m_conv_w_dw, m_conv_b_dw, m_conv_ln_g, m_conv_ln_b, m_w_b_proj, m_w_out, m_ln2_g, m_ln2_b, m_ffn2_w_gu, m_ffn2_w_down, m_ln3_g, m_ln3_b, v_ffn1_w_gu, v_ffn1_w_down, v_ln1_g, v_ln1_b, v_w_in, v_b_in, v_sgu_ln_g, v_sgu_ln_b, v_sgu_w_s, v_sgu_b_s, v_w_a_proj, v_conv_w_dw, v_conv_b_dw, v_conv_ln_g, v_conv_ln_b, v_w_b_proj, v_w_out, v_ln2_g, v_ln2_b, v_ffn2_w_gu, v_ffn2_w_down, v_ln3_g, v_ln3_b):
    args = dict(locals())
    assert x.shape[0] == 1 and ffn1_w_gu.shape[0] == 1
    T, D = x.shape[1], x.shape[2]
    F = ffn1_w_down.shape[1] * N_CHIPS
    W = sgu_ln_g.shape[1]
    KW = conv_w_dw.shape[1]
    assert KW - 1 <= HALO and T % SGU_BLOCK == 0

    mx, my, mc = lax.axis_index("x"), lax.axis_index("y"), lax.axis_index("c")
    q = 2 * mx + my
    hc = jnp.reshape(mc, (1,)).astype(jnp.int32)
    qc = jnp.stack([q, mc]).astype(jnp.int32)

    big_names = [("ffn1_w_gu", 1), ("ffn1_w_down", 0), ("w_in", 1), ("w_a_proj", 1), ("w_b_proj", 1), ("w_out", 0),
                 ("ffn2_w_gu", 1), ("ffn2_w_down", 0)]
    B = {n: _Big(n, args[n][0], args["m_" + n][0], args["v_" + n][0], ax) for n, ax in big_names}
    sh16 = {n: _cast_bf16(f"cast_{n}", b.w, _tile_rows(b.sr, b.sc))[0] for n, b in B.items()}

    x2d = x[0]
    tgt = loss_target[0]
    tm_r = _pick(T, 256)
    tm = _pick(T, 1024)
    tm_h = _pick(T, 512)
    tn = _pick128(D, 1024)
    nj = D // tn
    tng = _pick128(D, 512)
    njg = D // tng
    tnf = _pick128(F, 512)
    nf = F // tnf
    tnw = _pick128(W, 1024)
    tnd = _pick128(D, 512)
    tmw = _pick128(W, 512)
    SUB = 256

    def ffn_up(tag, xb_, wgu, carry=None):
        def epi(accs, ex):
            g, u = accs
            return [g, u, (g * jax.nn.sigmoid(g)) * u]

        return _mm(f"{tag}_up", [(xb_, wgu, 0, 0, 0), (xb_, wgu, 0, 0, nf)], M=T, N=F, tm=tm, tn=tnf, tk=D, nk=1, epilogue=epi,
                   outs=[(F, BF16, 0)] * 3, sum_pairs=False, carry=carry, sub=SUB)

    def ffn_down(tag, act, wd, carry=None):
        return _mm(f"{tag}_down", [(act, wd, 0, 0, 0)], M=T, N=D, tm=tm, tn=_pick128(D, 512), tk=F, nk=1, epilogue=_ident,
                   outs=[(D, F32, 0)], carry=carry)

    def ffn_dact(tag, drh, wd, gte, up, carry=None):
        def epi(accs, ex):
            da = accs[0]
            g, u = ex[0].astype(F32), ex[1].astype(F32)
            s = jax.nn.sigmoid(g)
            return [da * u * (s * (1.0 + g * (1.0 - s))), da * (g * s)]

        return _mm(f"{tag}_dact", [(drh, wd, 0, 0, 0)], tb=True, M=T, N=F, tm=tm, tn=tnf, tk=D, nk=1, epilogue=epi,
                   outs=[(F, BF16, 0)] * 2, extras=[(gte, "mn", 0), (up, "mn", 0)], carry=carry, sub=SUB)

    def ffn_dwdown(tag, act, drh, carry=None):
        return _wgrad(f"{tag}_dwdown", act, drh, M=F, N=D, T=T, tm=tnf, tn=tnd, tk=T, carry=carry)

    def ffn_dwgate(tag, xb_, dg, carry=None):
        return _wgrad(f"{tag}_dwgate", xb_, dg, M=D, N=F, T=T, tm=tnd, tn=tnf, tk=T, ncols=2 * F, carry=carry)

    def ffn_dwup(tag, xb_, du, into, carry=None):
        return _wgrad(f"{tag}_dwup", xb_, du, M=D, N=F, T=T, tm=tnd, tn=tnf, tk=T, ncols=2 * F, into=into, joff=nf, carry=carry)

    def ffn_dx(tag, parts, wgu, addends, carry=None):
        pairs = [(da, wgu, 0, 1 if which == "up" else 0, 0) for which, da in parts]

        def epi(accs, ex):
            tot = accs[0] + ALPHA * ex[0]
            for e in ex[1:]:
                tot = tot + e
            return [tot]

        return _mm(f"{tag}_dx_{'_'.join(w_ for w_, _ in parts)}", pairs, tb=True, M=T, N=D, tm=tm_h,
                   tn=_pick128(D, 512 // len(parts)), tk=F, nk=1, epilogue=epi if addends else _ident, outs=[(D, F32, 0)],
                   extras=[(a, "mn", 0) for a in addends], carry=carry)

    wdw_pad = jnp.pad(conv_w_dw[0], ((0, HALO - KW), (0, 0)))
    c0, un = _merge(B["ffn1_w_gu"].gather_ici(sh16["ffn1_w_gu"]), _conv_w_allgather(wdw_pad, conv_w_dw.shape[2]))
    (wgu1,), (wdw_full,) = un(_exchange("gather_first", c0))
    (xb,), (wgu1,) = _cast_bf16("cast_x", x2d, tm_r, carry=B["ffn1_w_gu"].gather_d2d(wgu1))

    c, un = _merge(B["ffn1_w_down"].gather_ici(sh16["ffn1_w_down"]), B["w_in"].gather_ici(sh16["w_in"]))
    (g1, u1, a1), co = ffn_up("ffn1", xb, wgu1, carry=c)
    (wd1,), (win,) = un(co)
    (wd1,) = _exchange("gather_d2d_ffn1_w_down", B["ffn1_w_down"].gather_d2d(wd1))
    c, un = _merge(B["w_a_proj"].gather_ici(sh16["w_a_proj"]), B["w_b_proj"].gather_ici(sh16["w_b_proj"]),
                   B["w_out"].gather_ici(sh16["w_out"]), B["w_in"].gather_d2d(win))
    (fo1,), co = ffn_down("ffn1", a1, wd1, carry=c)
    (wa,), (wb,), (wout,), (win,) = un(co)
    c, un = _merge(B["w_a_proj"].gather_d2d(wa), B["w_b_proj"].gather_d2d(wb), B["w_out"].gather_d2d(wout))
    (x1, x1b, xh1, rs1), co = _ln_fwd("ln1", x2d, fo1, ln1_g, ln1_b, 0.5, tm_r, carry=c)
    (wa,), (wb,), (wout,) = un(co)

    (proj,), (wgu2,) = _mm("in_proj", [(x1b, win, 0, 0, 0)], M=T, N=4 * D, tm=tm, tn=tn, tk=D, nk=1,
                           epilogue=lambda accs, ex: [accs[0] + ex[0]], outs=[(4 * D, F32, 0)], extras=[(b_in, "n", 0)],
                           carry=B["ffn2_w_gu"].gather_ici(sh16["ffn2_w_gu"]))
    wm = sgu_w_s[0]
    bst = sgu_b_s[0].T
    sa, z = _sgu_fwd("sgu_fwd", proj, sgu_ln_g, sgu_ln_b, wm, bst, tm_r)
    (zc, sb), (wd2,) = _conv_fwd("conv_fwd", z, wdw_full, KW, conv_b_dw, conv_ln_g, conv_ln_b, tm_r,
                                 carry=B["ffn2_w_down"].gather_ici(sh16["ffn2_w_down"]))

    def epi_mix(accs, ex):
        ya_, yb_ = accs
        return [jax.nn.sigmoid(ex[0]) * ya_ + jax.nn.sigmoid(ex[1]) * yb_, ya_, yb_]

    (mixin, ya, yb), (wgu2,) = _mm("branch_proj", [(sa, wa, 0, 0, 0), (sb, wb, 0, 0, 0)], M=T, N=D, tm=tm, tn=tng, tk=W, nk=1,
                                   epilogue=epi_mix, outs=[(D, BF16, 0)] * 3, sum_pairs=False, sub=SUB,
                                   extras=[(proj, "mn", 2 * njg), (proj, "mn", 3 * njg)], carry=B["ffn2_w_gu"].gather_d2d(wgu2))
    (mix,), (wd2,) = _mm("out_proj", [(mixin, wout, 0, 0, 0)], M=T, N=D, tm=tm, tn=tn, tk=D, nk=1, epilogue=_ident,
                         outs=[(D, F32, 0)], carry=B["ffn2_w_down"].gather_d2d(wd2))
    x2, x2b, xh2, rs2 = _ln_fwd("ln2", x1, mix, ln2_g, ln2_b, 1.0, tm_r)
    g2, u2, a2 = ffn_up("ffn2", x2b, wgu2)
    (fo2,) = ffn_down("ffn2", a2, wd2)
    dr3, dr3h, loss_part, dln3_g, dln3_b = _ln3_loss("ln3_loss", x2, fo2, ln3_g, ln3_b, tgt, tm_r)

    b_gu2, b_d2 = B["ffn2_w_gu"], B["ffn2_w_down"]
    dg2, du2 = ffn_dact("ffn2", dr3h, wd2, g2, u2)
    dwd2 = ffn_dwdown("ffn2", a2, dr3h)
    dwgu2, (r1_d2,) = ffn_dwgate("ffn2", x2b, dg2, carry=b_d2.rs_pair(dwd2[1]))
    dwgu2 = ffn_dwup("ffn2", x2b, du2, dwgu2)
    cs_d2 = b_d2.rs_pairsum("ffn2_w_down", dwd2[0], r1_d2, hc)
    c, un = _merge(b_gu2.rs_pair(dwgu2[1]), b_d2.rs_ici(cs_d2[1]))
    (dx2,), co = ffn_dx("ffn2", [("gate", dg2), ("up", du2)], wgu2, [dr3], carry=c)
    (r1_gu2,), (r2_d2,) = un(co)
    cs_gu2 = b_gu2.rs_pairsum("ffn2_w_gu", dwgu2[0], r1_gu2, hc)
    gh_d2 = b_d2.rs_final("ffn2_w_down", cs_d2[0], r2_d2, qc)

    (dr2, dr2b, dln2_g, dln2_b), (g_d2,) = _ln_bwd("ln2_bwd", dx2, xh2, rs2, ln2_g, 1.0, tm_r, carry=b_d2.rs_share(gh_d2))

    def epi_dmix(accs, ex):
        dm = accs[0]
        ga, gb = jax.nn.sigmoid(ex[0]), jax.nn.sigmoid(ex[1])
        ya_, yb_ = ex[2].astype(F32), ex[3].astype(F32)
        return [dm * ga, dm * gb, dm * ya_ * (ga * (1.0 - ga)), dm * yb_ * (gb * (1.0 - gb))]

    dya, dyb, dla, dlb = _mm("out_proj_bwd", [(dr2b, wout, 0, 0, 0)], tb=True, M=T, N=D, tm=tm, tn=tng, tk=D, nk=1,
                             epilogue=epi_dmix, outs=[(D, BF16, 0)] * 4, sub=SUB,
                             extras=[(proj, "mn", 2 * njg), (proj, "mn", 3 * njg), (ya, "mn", 0), (yb, "mn", 0)])
    dwout = _wgrad("dw_out", mixin, dr2b, M=D, N=D, T=T, tm=tnd, tn=tnd, tk=T)
    (dsa,) = _mm("a_proj_bwd", [(dya, wa, 0, 0, 0)], tb=True, M=T, N=W, tm=tm, tn=tnw, tk=D, nk=1, epilogue=_ident,
                 outs=[(W, F32, 0)])
    (dsb,) = _mm("b_proj_bwd", [(dyb, wb, 0, 0, 0)], tb=True, M=T, N=W, tm=tm, tn=tnw, tk=D, nk=1, epilogue=_ident,
                 outs=[(W, F32, 0)])
    dwa = _wgrad("dw_a_proj", sa, dya, M=W, N=D, T=T, tm=tmw, tn=tnd, tk=T)
    dwb = _wgrad("dw_b_proj", sb, dyb, M=W, N=D, T=T, tm=tmw, tn=tnd, tk=T)

    dpa, dwm, dbst, dsgu_g, dsgu_b = _sgu_bwd("sgu_bwd", proj, dsa, sgu_ln_g, sgu_ln_b, wm, bst, tm_r)
    dzc, dcln_g, dcln_b, dbdw = _conv_ln_bwd("conv_ln_bwd", dsb, zc, conv_ln_g, conv_ln_b, tm_r)
    dpb, dwdw = _conv_bwd("conv_bwd", dzc, z, proj, wdw_full, KW, tm_r)

    dps = [dpa, dpb, dla, dlb]
    db_in = jnp.concatenate([_colsum_rows(f"db_in_{k}", dp, tm_r) for k, dp in enumerate(dps)], axis=1)
    (dx1,), (r2_gu2,) = _mm("in_proj_bwd", [(dp, win, 0, k, 0) for k, dp in enumerate(dps)], tb=True, M=T, N=D, tm=tm_h,
                            tn=tnd, tk=D, nk=1, epilogue=lambda accs, ex: [accs[0] + ALPHA * ex[0]], outs=[(D, F32, 0)],
                            extras=[(dr2, "mn", 0)], carry=b_gu2.rs_ici(cs_gu2[1]))
    gh_gu2 = b_gu2.rs_final("ffn2_w_gu", cs_gu2[0], r2_gu2, qc)
    dwin, (g_gu2,) = _wgrad("dw_in_0", x1b, dps[0], M=D, N=D, T=T, tm=tnd, tn=tnd, tk=T, ncols=4 * D,
                            carry=b_gu2.rs_share(gh_gu2))
    for k in range(1, 4):
        dwin = _wgrad(f"dw_in_{k}", x1b, dps[k], M=D, N=D, T=T, tm=tnd, tn=tnd, tk=T, ncols=4 * D, into=dwin,
                      joff=k * (D // tnd))

    mix_names = ["w_in", "w_a_proj", "w_b_proj", "w_out"]
    mix_grads = dict(zip(mix_names, [dwin, dwa, dwb, dwout], strict=True))
    c, un = _merge(*[B[n].rs_pair(mix_grads[n][1]) for n in mix_names])
    (dr1, dr1h, dln1_g, dln1_b), co = _ln_bwd("ln1_bwd", dx1, xh1, rs1, ln1_g, 0.5, tm_r, carry=c)
    cs_mix = {n: B[n].rs_pairsum(n, mix_grads[n][0], r1, hc) for n, (r1,) in zip(mix_names, un(co), strict=True)}

    small_names = ["ln1_g", "ln1_b", "b_in", "sgu_ln_g", "sgu_ln_b", "sgu_w_s", "sgu_b_s", "conv_w_dw", "conv_b_dw",
                   "conv_ln_g", "conv_ln_b", "ln2_g", "ln2_b", "ln3_g", "ln3_b"]
    small_parts = {"ln1_g": dln1_g, "ln1_b": dln1_b, "b_in": db_in, "sgu_ln_g": dsgu_g, "sgu_ln_b": dsgu_b, "sgu_w_s": dwm,
                   "sgu_b_s": dbst.T, "conv_w_dw": dwdw[:KW], "conv_b_dw": dbdw, "conv_ln_g": dcln_g, "conv_ln_b": dcln_b,
                   "ln2_g": dln2_g, "ln2_b": dln2_b, "ln3_g": dln3_g, "ln3_b": dln3_b}
    packed, spans = _pack_rows([small_parts[n] for n in small_names])

    b_gu1, b_d1 = B["ffn1_w_gu"], B["ffn1_w_down"]
    c, un = _merge(*[B[n].rs_ici(cs_mix[n][1]) for n in mix_names], _small_allgather(packed))
    (dg1, du1), co = ffn_dact("ffn1", dr1h, wd1, g1, u1, carry=c)
    *r2_mix, (small_slots,) = un(co)
    gh_mix = [B[n].rs_final(n, cs_mix[n][0], r2, qc) for n, (r2,) in zip(mix_names, r2_mix, strict=True)]
    c, un = _merge(*[B[n].rs_share(gh) for n, gh in zip(mix_names, gh_mix, strict=True)])
    dwgu1, co = ffn_dwgate("ffn1", xb, dg1, carry=c)
    g_mix = {n: g for n, (g,) in zip(mix_names, un(co), strict=True)}
    dwgu1 = ffn_dwup("ffn1", xb, du1, dwgu1)
    dwd1, (r1_gu1,) = ffn_dwdown("ffn1", a1, dr1h, carry=b_gu1.rs_pair(dwgu1[1]))
    cs_gu1 = b_gu1.rs_pairsum("ffn1_w_gu", dwgu1[0], r1_gu1, hc)
    c, un = _merge(b_gu1.rs_ici(cs_gu1[1]), b_d1.rs_pair(dwd1[1]))
    (dx_gate,), co = ffn_dx("ffn1", [("gate", dg1)], wgu1, [], carry=c)
    (r2_gu1,), (r1_d1,) = un(co)
    cs_d1 = b_d1.rs_pairsum("ffn1_w_down", dwd1[0], r1_d1, hc)
    gh_gu1 = b_gu1.rs_final("ffn1_w_gu", cs_gu1[0], r2_gu1, qc)
    c, un = _merge(b_d1.rs_ici(cs_d1[1]), b_gu1.rs_share(gh_gu1))
    (dx,), co = ffn_dx("ffn1", [("up", du1)], wgu1, [dr1, dx_gate], carry=c)
    (r2_d1,), (g_gu1,) = un(co)
    gh_d1 = b_d1.rs_final("ffn1_w_down", cs_d1[0], r2_d1, qc)

    grads = {"ffn1_w_gu": g_gu1, "ffn2_w_gu": g_gu2, "ffn2_w_down": g_d2, **g_mix}
    outs_g, outs_d, outs_m, outs_v = {}, {}, {}, {}

    def adamw_big(n, g, carry=None):
        b = B[n]
        return _adamw(f"adamw_{n}", b.w, g, b.m, b.v, _tile_rows(b.sr, b.sc, 1024 * 1024), carry=carry)

    upd = {}
    upd["ffn2_w_gu"], (grads["ffn1_w_down"],) = adamw_big("ffn2_w_gu", grads["ffn2_w_gu"], carry=b_d1.rs_share(gh_d1))
    for n, _ in big_names:
        if n not in upd:
            upd[n] = adamw_big(n, grads[n])
        d_, m_, v_ = upd[n]
        outs_g[n], outs_d[n], outs_m[n], outs_v[n] = grads[n][None], d_[None], m_[None], v_[None]
    gsum = _sum8("small_sum", small_slots)
    full_shapes = [args[n].shape if n != "conv_w_dw" else (1, KW, W) for n in small_names]
    gsmall = dict(zip(small_names, _unpack_rows(gsum, spans, full_shapes), strict=True))
    cs = conv_w_dw.shape[2]
    gsmall["conv_w_dw"] = lax.dynamic_slice_in_dim(gsmall["conv_w_dw"], q * cs, cs, axis=2)
    pw, spans2 = _pack_rows([args[n] for n in small_names])
    pg, _ = _pack_rows([gsmall[n] for n in small_names])
    pm, _ = _pack_rows([args["m_" + n] for n in small_names])
    pv, _ = _pack_rows([args["v_" + n] for n in small_names])
    pd, pmn, pvn = _adamw("adamw_small", pw, pg, pm, pv, pw.shape[0])
    shapes2 = [args[n].shape for n in small_names]
    for dst, src in ((outs_d, pd), (outs_m, pmn), (outs_v, pvn)):
        dst.update(zip(small_names, _unpack_rows(src, spans2, shapes2), strict=True))
    outs_g.update(gsmall)

    loss = lax.psum(loss_part[0, 0], ("x", "y", "c"))
    order = ["ffn1_w_gu", "ffn1_w_down", "ln1_g", "ln1_b", "w_in", "b_in", "sgu_ln_g", "sgu_ln_b", "sgu_w_s", "sgu_b_s",
             "w_a_proj", "conv_w_dw", "conv_b_dw", "conv_ln_g", "conv_ln_b", "w_b_proj", "w_out", "ln2_g", "ln2_b",
             "ffn2_w_gu", "ffn2_w_down", "ln3_g", "ln3_b"]
    return (loss, dx[None], *[outs_g[n] for n in order], *[outs_d[n] for n in order], *[outs_m[n] for n in order],
            *[outs_v[n] for n in order])
```

```python
import math

import jax
import jax.numpy as jnp
from jax import lax
from jax.experimental import pallas as pl
from jax.experimental.pallas import tpu as pltpu

BF16 = jnp.bfloat16
F32 = jnp.float32

LN_EPS = 1e-5
ALPHA = 2.0 ** 0.25
SGU_BLOCK = 128
SGU_CHUNK = 64
HALO = 32
SUBLANES = 8
LANES = 128
CONV_ROWS = 32
ADAM_LR = 0.001
ADAM_B1 = 0.9
ADAM_B2 = 0.999
ADAM_EPS = 1e-08
ADAM_WD = 0.01
ADAM_STEP = 10
N_CHIPS = 4
VMEM_LIMIT = 52 * 1024 * 1024
MESH = pl.DeviceIdType.MESH

_GELU_C0 = math.sqrt(2.0 / math.pi)
_GELU_C1 = 0.044715


def _cparams(sem):
    return pltpu.CompilerParams(dimension_semantics=sem, vmem_limit_bytes=VMEM_LIMIT)


def _gelu_parts(x):
    x2 = x * x
    t = jnp.tanh(_GELU_C0 * (x + _GELU_C1 * (x2 * x)))
    return 0.5 * (1.0 + t), t, x2


def _gelu(x):
    cdf, _, _ = _gelu_parts(x)
    return x * cdf


def _gelu_and_grad(x):
    cdf, t, x2 = _gelu_parts(x)
    grad = cdf + x * (0.5 * (1.0 - t * t)) * (_GELU_C0 * (1.0 + (3.0 * _GELU_C1) * x2))
    return x * cdf, grad


def _silu_grad(x):
    s = jax.nn.sigmoid(x)
    return s * (1.0 + x * (1.0 - s))


def _row_stats(x):
    mu = jnp.mean(x, axis=-1, keepdims=True)
    xc = x - mu
    var = jnp.mean(xc * xc, axis=-1, keepdims=True)
    rstd = lax.rsqrt(var + LN_EPS)
    return xc * rstd, rstd


def _ln_bwd_rows(dy, xhat, rstd, g):
    dxh = dy * g
    m1 = jnp.mean(dxh, axis=-1, keepdims=True)
    m2 = jnp.mean(dxh * xhat, axis=-1, keepdims=True)
    return rstd * (dxh - m1 - xhat * m2)


def _colsum(v):
    return jnp.sum(v, axis=0, keepdims=True)


class _Carry:
    def __init__(self, ro, rw, new, plan, n_remote, n_local=0):
        self.ro, self.rw, self.new, self.plan = list(ro), list(rw), list(new), plan
        self.n_remote, self.n_local = n_remote, n_local

    def sems(self):
        return [pltpu.SemaphoreType.DMA((self.n_remote,)), pltpu.SemaphoreType.DMA((self.n_remote,)),
                pltpu.SemaphoreType.DMA((max(self.n_local, 1),))]

    def copies(self, ro_refs, rw_refs, new_refs, send_sems, recv_sems, loc_sems):
        x, y, c = lax.axis_index("x"), lax.axis_index("y"), lax.axis_index("c")
        remote, local = self.plan(ro_refs, rw_refs, new_refs, x, y, c)
        assert len(remote) == self.n_remote and len(local) == self.n_local
        lcs = [pltpu.make_async_copy(s, d, loc_sems.at[k]) for k, (s, d) in enumerate(local)]
        rcs = [pltpu.make_async_remote_copy(src_ref=s, dst_ref=d, send_sem=send_sems.at[k], recv_sem=recv_sems.at[k],
                                            device_id=dev, device_id_type=MESH) for k, (s, d, dev) in enumerate(remote)]
        return lcs, rcs

    def out_shape(self):
        return [jax.ShapeDtypeStruct(a.shape, a.dtype) for a in self.rw] + self.new


def _start_all(lcs, rcs):
    for cp in lcs + rcs:
        cp.start()


def _wait_all(lcs, rcs):
    for cp in rcs:
        cp.wait_send()
    for cp in rcs:
        cp.wait_recv()
    for cp in lcs:
        cp.wait()


def _merge(*cs):
    ro = [a for c in cs for a in c.ro]
    rw = [a for c in cs for a in c.rw]
    new = [a for c in cs for a in c.new]

    def plan(ro_refs, rw_refs, new_refs, x, y, c):
        remote, local, a, b, d = [], [], 0, 0, 0
        for cc in cs:
            r, l = cc.plan(ro_refs[a:a + len(cc.ro)], rw_refs[b:b + len(cc.rw)], new_refs[d:d + len(cc.new)], x, y, c)
            a, b, d = a + len(cc.ro), b + len(cc.rw), d + len(cc.new)
            remote += r
            local += l
        return remote, local

    def unpack(couts):
        res, b, d = [], 0, len(rw)
        for cc in cs:
            res.append(list(couts[b:b + len(cc.rw)]) + list(couts[d:d + len(cc.new)]))
            b, d = b + len(cc.rw), d + len(cc.new)
        return res

    return _Carry(ro, rw, new, plan, sum(c.n_remote for c in cs), sum(c.n_local for c in cs)), unpack


_ANY = pl.BlockSpec(memory_space=pl.ANY)


def _exchange(name, carry):
    n_ro, n_rw, n_new = len(carry.ro), len(carry.rw), len(carry.new)

    def body(*refs):
        o0 = n_ro + n_rw
        lcs, rcs = carry.copies(refs[:n_ro], refs[o0:o0 + n_rw], refs[o0 + n_rw:o0 + n_rw + n_new], *refs[o0 + n_rw + n_new:])
        _start_all(lcs, rcs)
        _wait_all(lcs, rcs)

    return list(pl.pallas_call(
        body, name=name, in_specs=[_ANY] * (n_ro + n_rw), out_specs=[_ANY] * (n_rw + n_new), out_shape=carry.out_shape(),
        input_output_aliases={n_ro + k: k for k in range(n_rw)}, scratch_shapes=carry.sems())(*carry.ro, *carry.rw))


def _call(body, *, name, grid, in_specs, out_specs, out_shape, args, scratch=(), sem, carry=None, aliases=None):
    in_specs, out_specs, out_shape, scratch = list(in_specs), list(out_specs), list(out_shape), list(scratch)
    if carry is None:
        return list(pl.pallas_call(body, name=name, grid=grid, in_specs=in_specs, out_specs=out_specs, out_shape=out_shape,
                                   scratch_shapes=scratch, input_output_aliases=aliases or {},
                                   compiler_params=_cparams(sem))(*args))
    n_in, n_out, n_scr = len(in_specs), len(out_specs), len(scratch)
    n_ro, n_rw, n_new = len(carry.ro), len(carry.rw), len(carry.new)

    def wrapped(*refs):
        ins = refs[:n_in]
        ro_refs = refs[n_in:n_in + n_ro]
        o0 = n_in + n_ro + n_rw
        outs = refs[o0:o0 + n_out]
        rw_refs = refs[o0 + n_out:o0 + n_out + n_rw]
        new_refs = refs[o0 + n_out + n_rw:o0 + n_out + n_rw + n_new]
        s0 = o0 + n_out + n_rw + n_new
        scr = refs[s0:s0 + n_scr]
        sems = refs[s0 + n_scr:]
        first = pl.program_id(0) == 0
        last = pl.program_id(0) == grid[0] - 1
        for d in range(1, len(grid)):
            first = jnp.logical_and(first, pl.program_id(d) == 0)
            last = jnp.logical_and(last, pl.program_id(d) == grid[d] - 1)

        @pl.when(first)
        def _():
            _start_all(*carry.copies(ro_refs, rw_refs, new_refs, *sems))

        body(*ins, *outs, *scr)

        @pl.when(last)
        def _():
            _wait_all(*carry.copies(ro_refs, rw_refs, new_refs, *sems))

    al = dict(aliases or {})
    al.update({n_in + n_ro + k: n_out + k for k in range(n_rw)})
    res = pl.pallas_call(
        wrapped, name=name, grid=grid, in_specs=in_specs + [_ANY] * (n_ro + n_rw), out_specs=out_specs + [_ANY] * (n_rw + n_new),
        out_shape=out_shape + carry.out_shape(), scratch_shapes=scratch + carry.sems(), input_output_aliases=al,
        compiler_params=_cparams(("arbitrary",) * len(grid)))(*args, *carry.ro, *carry.rw)
    return list(res[:n_out]), list(res[n_out:])


def _mm(name, pairs, *, ta=False, tb=False, M, N, tm, tn, tk, nk, epilogue, outs, extras=(), sum_pairs=True, carry=None,
        sub=None):
    ni, nj = M // tm, N // tn
    assert ni * tm == M and nj * tn == N
    n_p = len(pairs)
    n_acc = 1 if sum_pairs else n_p
    in_specs, args = [], []
    for a, b, ak, bk, bj in pairs:
        if ta:
            in_specs.append(pl.BlockSpec((tk, tm), lambda i, j, k, ak=ak: (k + ak, i)))
        else:
            in_specs.append(pl.BlockSpec((tm, tk), lambda i, j, k, ak=ak: (i, k + ak)))
        if tb:
            in_specs.append(pl.BlockSpec((tn, tk), lambda i, j, k, bk=bk, bj=bj: (j + bj, k + bk)))
        else:
            in_specs.append(pl.BlockSpec((tk, tn), lambda i, j, k, bk=bk, bj=bj: (k + bk, j + bj)))
        args += [a, b]
    for arr, kind, jo in extras:
        if kind == "mn":
            in_specs.append(pl.BlockSpec((tm, tn), lambda i, j, k, jo=jo: (i, j + jo)))
        else:
            in_specs.append(pl.BlockSpec((1, tn), lambda i, j, k, jo=jo: (0, j + jo)))
        args.append(arr)
    out_specs = [pl.BlockSpec((tm, tn), lambda i, j, k, jo=jo: (i, j + jo)) for _, _, jo in outs]
    out_shape = [jax.ShapeDtypeStruct((M, nc), dt) for nc, dt, _ in outs]
    n_ex, n_out = len(extras), len(outs)
    dn = (((0 if ta else 1,), (1 if tb else 0,)), ((), ()))

    def body(*refs):
        ab = refs[: 2 * n_p]
        ex = refs[2 * n_p: 2 * n_p + n_ex]
        o0 = 2 * n_p + n_ex
        out_refs = refs[o0: o0 + n_out]
        acc_refs = refs[o0 + n_out:]

        def dots():
            res = []
            for p in range(n_p):
                a = ab[2 * p][...]
                b = ab[2 * p + 1][...]
                res.append(lax.dot_general(a.astype(BF16), b.astype(BF16), dn, preferred_element_type=F32))
            if sum_pairs:
                tot = res[0]
                for r in res[1:]:
                    tot = tot + r
                res = [tot]
            return res

        def finish(accs):
            tiles = epilogue(accs, [e[...] for e in ex])
            for r, t in zip(out_refs, tiles, strict=True):
                r[...] = t.astype(r.dtype)

        if nk == 1 and sub is not None and tn > sub:
            for s in range(tn // sub):
                cs = slice(s * sub, (s + 1) * sub)
                res = []
                for p in range(n_p):
                    b = ab[2 * p + 1][cs, :] if tb else ab[2 * p + 1][:, cs]
                    res.append(lax.dot_general(ab[2 * p][...].astype(BF16), b.astype(BF16), dn, preferred_element_type=F32))
                if sum_pairs:
                    tot = res[0]
                    for r in res[1:]:
                        tot = tot + r
                    res = [tot]
                tiles = epilogue(res, [e[:, cs] for e in ex])
                for r, t in zip(out_refs, tiles, strict=True):
                    r[:, cs] = t.astype(r.dtype)
        elif nk == 1:
            finish(dots())
        else:
            k = pl.program_id(2)
            d = dots()

            @pl.when(k == 0)
            def _():
                for r, v in zip(acc_refs, d, strict=True):
                    r[...] = v

            @pl.when(k > 0)
            def _():
                for r, v in zip(acc_refs, d, strict=True):
                    r[...] += v

            @pl.when(k == nk - 1)
            def _():
                finish([r[...] for r in acc_refs])

    scratch = [pltpu.VMEM((tm, tn), F32) for _ in range(n_acc)] if nk > 1 else []
    return _call(body, name=name, grid=(ni, nj, nk), in_specs=in_specs, out_specs=out_specs, out_shape=out_shape, args=args,
                 scratch=scratch, sem=("parallel", "parallel", "arbitrary"), carry=carry)


def _wgrad(name, a, b, *, M, N, T, tm, tn, tk, into=None, joff=0, ncols=None, carry=None):
    ncols = N if ncols is None else ncols
    ni, nj, nk = M // tm, N // tn, T // tk
    dn = (((0,), (0,)), ((), ()))

    def body(a_ref, b_ref, *rest):
        d = lax.dot_general(a_ref[...].astype(BF16), b_ref[...].astype(BF16), dn, preferred_element_type=F32)
        if nk == 1:
            of_ref, oh_ref = rest[-2:]
            of_ref[...] = d
            oh_ref[...] = d.astype(BF16)
            return
        of_ref, oh_ref, acc_ref = rest[-3:]
        k = pl.program_id(2)

        @pl.when(k == 0)
        def _():
            acc_ref[...] = d

        @pl.when(k > 0)
        def _():
            acc_ref[...] += d

        @pl.when(k == nk - 1)
        def _():
            of_ref[...] = acc_ref[...]
            oh_ref[...] = acc_ref[...].astype(BF16)

    ospec = pl.BlockSpec((tm, tn), lambda i, j, k: (i, j + joff))
    in_specs = [pl.BlockSpec((tk, tm), lambda i, j, k: (k, i)), pl.BlockSpec((tk, tn), lambda i, j, k: (k, j))]
    args, aliases = [a, b], None
    if into is not None:
        in_specs += [_ANY, _ANY]
        args += list(into)
        aliases = {2: 0, 3: 1}
    return _call(body, name=name, grid=(ni, nj, nk), in_specs=in_specs, out_specs=[ospec, ospec],
                 out_shape=[jax.ShapeDtypeStruct((M, ncols), F32), jax.ShapeDtypeStruct((M, ncols), BF16)], args=args,
                 scratch=[pltpu.VMEM((tm, tn), F32)] if nk > 1 else [], sem=("parallel", "parallel", "arbitrary"), carry=carry,
                 aliases=aliases)


def _rows(tm, c, cb=0):
    return pl.BlockSpec((tm, c), lambda i, cb=cb: (i, cb))


def _whole(shape):
    nd = len(shape)
    return pl.BlockSpec(shape, lambda i, nd=nd: (0,) * nd)


def _cast_bf16(name, x, tm, carry=None):
    t, d = x.shape

    def body(x_ref, o_ref):
        o_ref[...] = x_ref[...].astype(BF16)

    return _call(body, name=name, grid=(t // tm,), in_specs=[_rows(tm, d)], out_specs=[_rows(tm, d)],
                 out_shape=[jax.ShapeDtypeStruct((t, d), BF16)], args=[x], sem=("parallel",), carry=carry)


def _ln_fwd(name, xres, f, g, b, cf, tm, carry=None):
    t, d = xres.shape

    def body(x_ref, f_ref, g_ref, b_ref, y_ref, yb_ref, xh_ref, rs_ref):
        r = ALPHA * x_ref[...] + cf * f_ref[...]
        xhat, rstd = _row_stats(r)
        y = xhat * g_ref[...] + b_ref[...]
        y_ref[...] = y
        yb_ref[...] = y.astype(BF16)
        xh_ref[...] = xhat
        rs_ref[...] = rstd

    return _call(
        body, name=name, grid=(t // tm,), in_specs=[_rows(tm, d), _rows(tm, d), _whole((1, d)), _whole((1, d))],
        out_specs=[_rows(tm, d), _rows(tm, d), _rows(tm, d), _rows(tm, 1)],
        out_shape=[jax.ShapeDtypeStruct((t, d), F32), jax.ShapeDtypeStruct((t, d), BF16),
                   jax.ShapeDtypeStruct((t, d), F32), jax.ShapeDtypeStruct((t, 1), F32)],
        args=[xres, f, g, b], sem=("parallel",), carry=carry)


def _ln_bwd(name, dy, xhat, rstd, g, scale, tm, carry=None):
    t, d = dy.shape

    def body(dy_ref, xh_ref, rs_ref, g_ref, dr_ref, drb_ref, dg_ref, db_ref):
        i = pl.program_id(0)
        dy_v, xh = dy_ref[...], xh_ref[...]
        dr = _ln_bwd_rows(dy_v, xh, rs_ref[...], g_ref[...])
        dr_ref[...] = dr
        drb_ref[...] = (scale * dr).astype(BF16)

        @pl.when(i == 0)
        def _():
            dg_ref[...] = jnp.zeros_like(dg_ref)
            db_ref[...] = jnp.zeros_like(db_ref)

        dg_ref[...] += _colsum(dy_v * xh)
        db_ref[...] += _colsum(dy_v)

    return _call(
        body, name=name, grid=(t // tm,), in_specs=[_rows(tm, d), _rows(tm, d), _rows(tm, 1), _whole((1, d))],
        out_specs=[_rows(tm, d), _rows(tm, d), _whole((1, d)), _whole((1, d))],
        out_shape=[jax.ShapeDtypeStruct((t, d), F32), jax.ShapeDtypeStruct((t, d), BF16),
                   jax.ShapeDtypeStruct((1, d), F32), jax.ShapeDtypeStruct((1, d), F32)],
        args=[dy, xhat, rstd, g], sem=("arbitrary",), carry=carry)


def _ln3_loss(name, xres, f, g, b, target, tm):
    t, d = xres.shape

    def body(x_ref, f_ref, g_ref, b_ref, tg_ref, dr_ref, drb_ref, loss_ref, dg_ref, db_ref):
        i = pl.program_id(0)
        r = ALPHA * x_ref[...] + 0.5 * f_ref[...]
        xhat, rstd = _row_stats(r)
        gv = g_ref[...]
        y = xhat * gv + b_ref[...]
        err = y - tg_ref[...]
        dy = err * (1.0 / d)
        dr = _ln_bwd_rows(dy, xhat, rstd, gv)
        dr_ref[...] = dr
        drb_ref[...] = (0.5 * dr).astype(BF16)
        part = 0.5 * jnp.sum(jnp.mean(err * err, axis=-1, keepdims=True), axis=0, keepdims=True)

        @pl.when(i == 0)
        def _():
            loss_ref[...] = jnp.zeros_like(loss_ref)
            dg_ref[...] = jnp.zeros_like(dg_ref)
            db_ref[...] = jnp.zeros_like(db_ref)

        loss_ref[...] += jnp.broadcast_to(part, loss_ref.shape)
        dg_ref[...] += _colsum(dy * xhat)
        db_ref[...] += _colsum(dy)

    return _call(
        body, name=name, grid=(t // tm,),
        in_specs=[_rows(tm, d), _rows(tm, d), _whole((1, d)), _whole((1, d)), _rows(tm, d)],
        out_specs=[_rows(tm, d), _rows(tm, d), _whole((8, 128)), _whole((1, d)), _whole((1, d))],
        out_shape=[jax.ShapeDtypeStruct((t, d), F32), jax.ShapeDtypeStruct((t, d), BF16),
                   jax.ShapeDtypeStruct((8, 128), F32), jax.ShapeDtypeStruct((1, d), F32),
                   jax.ShapeDtypeStruct((1, d), F32)],
        args=[xres, f, g, b, target], sem=("arbitrary",))


def _colsum_rows(name, x, tm):
    t, d = x.shape

    def body(x_ref, o_ref):
        @pl.when(pl.program_id(0) == 0)
        def _():
            o_ref[...] = jnp.zeros_like(o_ref)

        o_ref[...] += _colsum(x_ref[...].astype(F32))

    return _call(body, name=name, grid=(t // tm,), in_specs=[_rows(tm, d)], out_specs=[_whole((1, d))],
                 out_shape=[jax.ShapeDtypeStruct((1, d), F32)], args=[x], sem=("arbitrary",))[0]


def _sgu_mask():
    sh = SGU_CHUNK.bit_length() - 1
    r = lax.shift_right_logical(lax.broadcasted_iota(jnp.int32, (SGU_BLOCK, SGU_BLOCK), 0), sh)
    c = lax.shift_right_logical(lax.broadcasted_iota(jnp.int32, (SGU_BLOCK, SGU_BLOCK), 1), sh)
    return c <= r


def _sgu_fwd(name, p, lng, lnb, wm, bst, tm):
    t = p.shape[0]
    n_grp, w = wm.shape[0], lng.shape[1]
    hd = w // n_grp
    nblk = tm // SGU_BLOCK

    def body(uv_ref, h_ref, g_ref, b_ref, wm_ref, bs_ref, sa_ref, z_ref, vn_s):
        xhat, _ = _row_stats(_gelu(uv_ref[:, w:]))
        vn_s[...] = (xhat * g_ref[...] + b_ref[...]).astype(BF16)
        mask = _sgu_mask()
        for h in range(n_grp):
            wh = jnp.where(mask, wm_ref[h], 0.0).astype(BF16)
            bcol = bs_ref[:, h:h + 1]
            cs = slice(h * hd, (h + 1) * hd)
            for n in range(nblk):
                rs = slice(n * SGU_BLOCK, (n + 1) * SGU_BLOCK)
                s = jnp.dot(wh, vn_s[rs, cs], preferred_element_type=F32) + bcol
                sa_ref[rs, cs] = (_gelu(uv_ref[rs, cs]) * s).astype(BF16)
        z_ref[...] = h_ref[:, :w] * jax.nn.sigmoid(h_ref[:, w:])

    return _call(
        body, name=name, grid=(t // tm,),
        in_specs=[_rows(tm, 2 * w, 0), _rows(tm, 2 * w, 1), _whole((1, w)), _whole((1, w)), _whole(wm.shape),
                  _whole(bst.shape)],
        out_specs=[_rows(tm, w), _rows(tm, w)],
        out_shape=[jax.ShapeDtypeStruct((t, w), BF16), jax.ShapeDtypeStruct((t, w), F32)],
        args=[p, p, lng, lnb, wm, bst], scratch=[pltpu.VMEM((tm, w), BF16)], sem=("parallel",))


def _sgu_bwd(name, p, dsa, lng, lnb, wm, bst, tm):
    t = p.shape[0]
    n_grp, w = wm.shape[0], lng.shape[1]
    hd = w // n_grp
    nblk = tm // SGU_BLOCK

    def body(uv_ref, dsa_ref, g_ref, b_ref, wm_ref, bs_ref, dp_ref, dwm_ref, dbs_ref, dg_ref, db_ref, dbin_ref,
             vn_s, ug_s, dvn_s, dug_s):
        i = pl.program_id(0)

        @pl.when(i == 0)
        def _():
            dwm_ref[...] = jnp.zeros_like(dwm_ref)
            dbs_ref[...] = jnp.zeros_like(dbs_ref)
            dg_ref[...] = jnp.zeros_like(dg_ref)
            db_ref[...] = jnp.zeros_like(db_ref)
            dbin_ref[...] = jnp.zeros_like(dbin_ref)

        ug, dgelu_u = _gelu_and_grad(uv_ref[:, :w])
        ug_s[...] = ug
        vg, dgelu_v = _gelu_and_grad(uv_ref[:, w:])
        xhat, rstd = _row_stats(vg)
        gv = g_ref[...]
        vn_s[...] = (xhat * gv + b_ref[...]).astype(BF16)
        mask = _sgu_mask()
        for h in range(n_grp):
            wh = jnp.where(mask, wm_ref[h], 0.0).astype(BF16)
            bcol = bs_ref[:, h:h + 1]
            cs = slice(h * hd, (h + 1) * hd)
            dw_h = jnp.zeros((SGU_BLOCK, SGU_BLOCK), F32)
            dbs_h = jnp.zeros((SGU_BLOCK, 1), F32)
            for n in range(nblk):
                rs = slice(n * SGU_BLOCK, (n + 1) * SGU_BLOCK)
                vblk = vn_s[rs, cs]
                s = jnp.dot(wh, vblk, preferred_element_type=F32) + bcol
                dsa_blk = dsa_ref[rs, cs]
                dug_s[rs, cs] = dsa_blk * s
                ds = dsa_blk * ug_s[rs, cs]
                dsb = ds.astype(BF16)
                dvn_s[rs, cs] = lax.dot_general(wh, dsb, (((0,), (0,)), ((), ())), preferred_element_type=F32)
                dw_h = dw_h + lax.dot_general(dsb, vblk, (((1,), (1,)), ((), ())), preferred_element_type=F32)
                dbs_h = dbs_h + jnp.sum(ds, axis=1, keepdims=True)
            dwm_ref[h] += jnp.where(mask, dw_h, 0.0)
            dbs_ref[:, h:h + 1] += dbs_h
        dvn = dvn_s[...]
        dg_ref[...] += _colsum(dvn * xhat)
        db_ref[...] += _colsum(dvn)
        dvg = _ln_bwd_rows(dvn, xhat, rstd, gv)
        du = dug_s[...] * dgelu_u
        dv = dvg * dgelu_v
        dp_ref[:, :w] = du.astype(BF16)
        dp_ref[:, w:] = dv.astype(BF16)
        dbin_ref[:, :w] += _colsum(du)
        dbin_ref[:, w:] += _colsum(dv)

    return _call(
        body, name=name, grid=(t // tm,),
        in_specs=[_rows(tm, 2 * w, 0), _rows(tm, w), _whole((1, w)), _whole((1, w)), _whole(wm.shape), _whole(bst.shape)],
        out_specs=[_rows(tm, 2 * w), _whole(wm.shape), _whole(bst.shape), _whole((1, w)), _whole((1, w)), _whole((1, 2 * w))],
        out_shape=[jax.ShapeDtypeStruct((t, 2 * w), BF16), jax.ShapeDtypeStruct(wm.shape, F32),
                   jax.ShapeDtypeStruct(bst.shape, F32), jax.ShapeDtypeStruct((1, w), F32), jax.ShapeDtypeStruct((1, w), F32),
                   jax.ShapeDtypeStruct((1, 2 * w), F32)],
        args=[p, dsa, lng, lnb, wm, bst],
        scratch=[pltpu.VMEM((tm, w), BF16), pltpu.VMEM((tm, w), F32), pltpu.VMEM((tm, w), F32), pltpu.VMEM((tm, w), F32)],
        sem=("arbitrary",))


def _halo_prev(tm, c):
    return pl.BlockSpec((HALO, c), lambda i: (jnp.maximum(i * (tm // HALO) - 1, 0), 0))


def _halo_next(tm, c, t):
    last = t // HALO - 1
    return pl.BlockSpec((HALO, c), lambda i: (jnp.minimum((i + 1) * (tm // HALO), last), 0))


def _shifted_copies(sh, n):
    for r in range(1, SUBLANES):
        sh[r, :n - SUBLANES, :] = sh[0, r:r + n - SUBLANES, :]


def _row_broadcasts(wb, w_ref, kw):
    for k in range(kw):
        wb[k] = jnp.broadcast_to(w_ref[k:k + 1, :], wb.shape[1:])


def _tap(sh, r0, o, rows, cols):
    return sh[o % SUBLANES, pl.ds(pl.multiple_of(r0 + (o - o % SUBLANES), SUBLANES), rows), cols]


def _conv_fwd(name, z, wdw, kw, bdw, lng, lnb, tm, carry=None):
    t, c = z.shape
    lead = HALO - (kw - 1)
    n = tm + HALO

    def body(zp_ref, z_ref, w_ref, bdw_ref, g_ref, b_ref, zc_ref, sb_ref, sh, wb):
        i = pl.program_id(0)
        sh[0, :HALO, :] = jnp.where(i > 0, zp_ref[...], 0.0)
        sh[0, HALO:, :] = z_ref[...]
        _shifted_copies(sh, n)
        _row_broadcasts(wb, w_ref, kw)
        bias = jnp.broadcast_to(bdw_ref[...], (SUBLANES, c))
        groups = CONV_ROWS // SUBLANES

        def chunk(ci, _):
            r0 = pl.multiple_of(ci * CONV_ROWS, CONV_ROWS)
            accs = [bias] * groups
            for k in range(kw):
                wk = wb[k]
                tp = _tap(sh, r0, lead + k, CONV_ROWS, slice(None))
                accs = [accs[g] + wk * tp[g * SUBLANES:(g + 1) * SUBLANES] for g in range(groups)]
            zc_ref[pl.ds(r0, CONV_ROWS), :] = jnp.concatenate(accs, axis=0)
            return 0

        lax.fori_loop(0, tm // CONV_ROWS, chunk, 0)
        xhat, _ = _row_stats(zc_ref[...])
        zn = xhat * g_ref[...] + b_ref[...]
        sb_ref[...] = (zn * jax.nn.sigmoid(zn)).astype(BF16)

    return _call(
        body, name=name, grid=(t // tm,),
        in_specs=[_halo_prev(tm, c), _rows(tm, c), _whole(wdw.shape), _whole((1, c)), _whole((1, c)), _whole((1, c))],
        out_specs=[_rows(tm, c), _rows(tm, c)],
        out_shape=[jax.ShapeDtypeStruct((t, c), F32), jax.ShapeDtypeStruct((t, c), BF16)],
        args=[z, z, wdw, bdw, lng, lnb], scratch=[pltpu.VMEM((SUBLANES, n, c), F32), pltpu.VMEM((HALO, SUBLANES, c), F32)],
        sem=("parallel",), carry=carry)


def _conv_ln_bwd(name, dsb, zc, lng, lnb, tm):
    t, c = zc.shape

    def body(dsb_ref, zc_ref, g_ref, b_ref, dzc_ref, dg_ref, db_ref, dbdw_ref):
        i = pl.program_id(0)
        xhat, rstd = _row_stats(zc_ref[...])
        gv = g_ref[...]
        zn = xhat * gv + b_ref[...]
        dzn = dsb_ref[...] * _silu_grad(zn)
        dzc = _ln_bwd_rows(dzn, xhat, rstd, gv)
        dzc_ref[...] = dzc

        @pl.when(i == 0)
        def _():
            dg_ref[...] = jnp.zeros_like(dg_ref)
            db_ref[...] = jnp.zeros_like(db_ref)
            dbdw_ref[...] = jnp.zeros_like(dbdw_ref)

        dg_ref[...] += _colsum(dzn * xhat)
        db_ref[...] += _colsum(dzn)
        dbdw_ref[...] += _colsum(dzc)

    return _call(
        body, name=name, grid=(t // tm,), in_specs=[_rows(tm, c), _rows(tm, c), _whole((1, c)), _whole((1, c))],
        out_specs=[_rows(tm, c), _whole((1, c)), _whole((1, c)), _whole((1, c))],
        out_shape=[jax.ShapeDtypeStruct((t, c), F32)] + [jax.ShapeDtypeStruct((1, c), F32)] * 3,
        args=[dsb, zc, lng, lnb], sem=("arbitrary",))


def _conv_bwd(name, dzc, z, p, wdw, kw, tm):
    t, c = z.shape
    n_i = t // tm
    n = tm + HALO

    def body(dzc_ref, dzn_ref, z_ref, h_ref, w_ref, dp_ref, dw_ref, dbin_ref, sh, dz_s, wb):
        i = pl.program_id(0)

        @pl.when(i == 0)
        def _():
            dw_ref[...] = jnp.zeros_like(dw_ref)

        sh[0, :tm, :] = dzc_ref[...]
        sh[0, tm:, :] = jnp.where(i < n_i - 1, dzn_ref[...], 0.0)
        _shifted_copies(sh, n)
        _row_broadcasts(wb, w_ref, kw)
        pair = 2 * SUBLANES
        for lc in range(c // LANES):
            cols = slice(lc * LANES, (lc + 1) * LANES)

            def rowv(rv, accs, cols=cols):
                r0 = pl.multiple_of(rv * pair, pair)
                zv = z_ref[pl.ds(r0, pair), cols]
                z0, z1 = zv[:SUBLANES], zv[SUBLANES:]
                ways = 4
                dz0, dz1, new = [None] * ways, [None] * ways, []
                for k in range(kw):
                    s = _tap(sh, r0, kw - 1 - k, pair, cols)
                    s0, s1 = s[:SUBLANES], s[SUBLANES:]
                    wk = wb[k, :, cols]
                    j = k % ways
                    dz0[j] = wk * s0 if dz0[j] is None else dz0[j] + wk * s0
                    dz1[j] = wk * s1 if dz1[j] is None else dz1[j] + wk * s1
                    new.append((accs[k] + z0 * s0) + z1 * s1)
                dz_s[pl.ds(r0, pair), cols] = jnp.concatenate([(dz0[0] + dz0[1]) + (dz0[2] + dz0[3]),
                                                               (dz1[0] + dz1[1]) + (dz1[2] + dz1[3])], axis=0)
                return tuple(new)

            accs = lax.fori_loop(0, tm // pair, rowv, tuple(jnp.zeros((SUBLANES, LANES), F32) for _ in range(kw)))
            for k in range(kw):
                dw_ref[k:k + 1, cols] += _colsum(accs[k])
        dz = dz_s[...]
        a, g = h_ref[:, :c], h_ref[:, c:]
        sg = jax.nn.sigmoid(g)
        da = dz * sg
        dg = dz * a * (sg * (1.0 - sg))
        dp_ref[:, :c] = da.astype(BF16)
        dp_ref[:, c:] = dg.astype(BF16)

        @pl.when(i == 0)
        def _():
            dbin_ref[...] = jnp.zeros_like(dbin_ref)

        dbin_ref[:, :c] += _colsum(da)
        dbin_ref[:, c:] += _colsum(dg)

    return _call(
        body, name=name, grid=(n_i,),
        in_specs=[_rows(tm, c), _halo_next(tm, c, t), _rows(tm, c), _rows(tm, 2 * c, 1), _whole(wdw.shape)],
        out_specs=[_rows(tm, 2 * c), _whole((HALO, c)), _whole((1, 2 * c))],
        out_shape=[jax.ShapeDtypeStruct((t, 2 * c), BF16), jax.ShapeDtypeStruct((HALO, c), F32),
                   jax.ShapeDtypeStruct((1, 2 * c), F32)],
        args=[dzc, dzc, z, p, wdw],
        scratch=[pltpu.VMEM((SUBLANES, n, c), F32), pltpu.VMEM((tm, c), F32), pltpu.VMEM((HALO, SUBLANES, c), F32)],
        sem=("arbitrary",))


def _adamw(name, w, g, m, v, tr, carry=None):
    r, c = w.shape
    c1 = 1.0 - ADAM_B1 ** ADAM_STEP
    c2 = 1.0 - ADAM_B2 ** ADAM_STEP

    def body(w_ref, g_ref, m_ref, v_ref, d_ref, mo_ref, vo_ref):
        gv = g_ref[...]
        mn = ADAM_B1 * m_ref[...] + (1.0 - ADAM_B1) * gv
        vn = ADAM_B2 * v_ref[...] + (1.0 - ADAM_B2) * (gv * gv)
        d_ref[...] = -ADAM_LR * ((mn / c1) / (jnp.sqrt(vn / c2) + ADAM_EPS) + ADAM_WD * w_ref[...])
        mo_ref[...] = mn
        vo_ref[...] = vn

    spec = _rows(tr, c)
    return _call(body, name=name, grid=(r // tr,), in_specs=[spec] * 4, out_specs=[spec] * 3,
                 out_shape=[jax.ShapeDtypeStruct((r, c), F32)] * 3, args=[w, g, m, v], sem=("parallel",), carry=carry)


def _pair_sum(name, mine, recv, hc, *, out_rows, tr, row_map):
    c_ = mine.shape[1]

    def body(hc_ref, a_ref, b_ref, oh_ref):
        oh_ref[...] = (a_ref[...] + b_ref[...].astype(F32)).astype(BF16)

    ispec = pl.BlockSpec((tr, c_), lambda i, hc_ref: (row_map(i, hc_ref[0]), 0))
    ospec = pl.BlockSpec((tr, c_), lambda i, hc_ref: (i, 0))
    return pl.pallas_call(
        body, name=name,
        grid_spec=pltpu.PrefetchScalarGridSpec(num_scalar_prefetch=1, grid=(out_rows // tr,), in_specs=[ispec, ispec],
                                               out_specs=ospec),
        out_shape=jax.ShapeDtypeStruct((out_rows, c_), BF16), compiler_params=_cparams(("parallel",)))(hc, mine, recv)


def _final_sum(name, mine, recv1, recv2, qc, *, part_shape, out_shape, tr, in_map, out_map):
    pr, pc = part_shape

    def body(qc_ref, a_ref, b_ref, r_ref, o_ref):
        own = a_ref[...] + b_ref[...].astype(F32)
        o_ref[...] = ((own + r_ref[0].astype(F32)) + r_ref[1].astype(F32)) + r_ref[2].astype(F32)

    ispec = pl.BlockSpec((tr, pc), lambda i, qc_ref: in_map(i, qc_ref[0], qc_ref[1]))
    return pl.pallas_call(
        body, name=name,
        grid_spec=pltpu.PrefetchScalarGridSpec(
            num_scalar_prefetch=1, grid=(pr // tr,),
            in_specs=[ispec, ispec, pl.BlockSpec((3, tr, pc), lambda i, qc_ref: (0, i, 0))],
            out_specs=pl.BlockSpec((tr, pc), lambda i, qc_ref: out_map(i, qc_ref[0], qc_ref[1]))),
        out_shape=jax.ShapeDtypeStruct(out_shape, F32), compiler_params=_cparams(("parallel",)))(qc, mine, recv1, recv2)


def _sum8(name, slots):
    def body(s_ref, o_ref):
        acc = s_ref[0]
        for d in range(1, 8):
            acc = acc + s_ref[d]
        o_ref[...] = acc

    return pl.pallas_call(body, name=name, out_shape=jax.ShapeDtypeStruct(slots.shape[1:], F32),
                          in_specs=[pl.BlockSpec(memory_space=pltpu.VMEM)], out_specs=pl.BlockSpec(memory_space=pltpu.VMEM),
                          compiler_params=pltpu.CompilerParams(vmem_limit_bytes=VMEM_LIMIT))(slots)


def _chips(x, y):
    return [(1 - x, y), (x, 1 - y), (1 - x, 1 - y)]


class _Big:
    def __init__(self, name, w, m, v, ax):
        self.name, self.w, self.m, self.v, self.ax = name, w, m, v, ax
        sr, sc = w.shape
        self.R, self.C = (sr * N_CHIPS, sc) if ax == 0 else (sr, sc * N_CHIPS)
        self.sr, self.sc = sr, sc
        self.hr = sr // 2

    def slot(self, ref, q, h=None):
        if self.ax == 1:
            cols = pl.ds(pl.multiple_of(q * self.sc, 128), self.sc)
            return ref.at[:, cols] if h is None else ref.at[pl.ds(pl.multiple_of(h * self.hr, 16), self.hr), cols]
        if h is None:
            return ref.at[pl.ds(pl.multiple_of(q * self.sr, 16), self.sr), :]
        return ref.at[pl.ds(pl.multiple_of(q * self.sr + h * self.hr, 16), self.hr), :]

    def half(self, ref, h):
        return ref.at[pl.ds(pl.multiple_of(h * self.hr, 16), self.hr), :]

    def part(self, ref, q):
        if self.ax == 1:
            return ref.at[:, pl.ds(pl.multiple_of(q * self.sc, 128), self.sc)]
        return ref.at[pl.ds(pl.multiple_of(q * self.hr, 16), self.hr), :]

    @property
    def part_shape(self):
        return (self.hr, self.sc)

    def gather_ici(self, shard16, piece=(0, 1), full=None):
        k, n = piece
        pr = self.hr // n
        assert pr * n == self.hr and pr % 16 == 0

        def plan(ro, rw, new, x, y, c):
            q = 2 * x + y
            dst_full = new[0] if full is None else rw[0]
            r0 = c * self.hr + k * pr
            src = ro[0].at[pl.ds(pl.multiple_of(r0, 16), pr), :]
            if self.ax == 1:
                dst = dst_full.at[pl.ds(pl.multiple_of(r0, 16), pr), pl.ds(pl.multiple_of(q * self.sc, 128), self.sc)]
            else:
                dst = dst_full.at[pl.ds(pl.multiple_of(q * self.sr + r0, 16), pr), :]
            remote = [(src, dst, (cx, cy, c)) for cx, cy in _chips(x, y)]
            return remote, ([(ro[0], self.slot(dst_full, q))] if full is None else [])

        if full is None:
            return _Carry([shard16], [], [jax.ShapeDtypeStruct((self.R, self.C), BF16)], plan, 3, 1)
        return _Carry([shard16], [full], [], plan, 3, 0)

    def gather_d2d(self, full):
        def plan(ro, rw, new, x, y, c):
            remote = []
            for cx, cy in _chips(x, y):
                piece = self.slot(rw[0], 2 * cx + cy, c)
                remote.append((piece, piece, (x, y, 1 - c)))
            return remote, []

        return _Carry([], [full], [], plan, 3)

    def rs_pair(self, g16):
        def plan(ro, rw, new, x, y, c):
            sib = (x, y, 1 - c)
            if self.ax == 1:
                rows = pl.ds(pl.multiple_of((1 - c) * self.hr, 16), self.hr)
                return [(ro[0].at[rows, :], new[0].at[rows, :], sib)], []
            return [(self.slot(ro[0], q, 1 - c), self.slot(new[0], q, 1 - c), sib) for q in range(N_CHIPS)], []

        return _Carry([g16], [], [jax.ShapeDtypeStruct((self.R, self.C), BF16)], plan, 1 if self.ax == 1 else N_CHIPS)

    def rs_pairsum(self, tag, g32, recv1, hc):
        tr = _tile_rows(self.hr, self.C)
        nb = self.hr // tr
        if self.ax == 1:
            row_map = lambda i, c: c * nb + i
        else:
            row_map = lambda i, c: ((i // nb) * 2 + c) * nb + i % nb
        return _pair_sum(f"rs_pairsum_{tag}", g32, recv1, hc, out_rows=self.R // 2, tr=tr, row_map=row_map)

    def rs_ici(self, cs16):
        def plan(ro, rw, new, x, y, c):
            return [(self.part(ro[0], 2 * cx + cy), new[0].at[k], (cx, cy, c)) for k, (cx, cy) in enumerate(_chips(x, y))], []

        return _Carry([cs16], [], [jax.ShapeDtypeStruct((3,) + self.part_shape, BF16)], plan, 3)

    def rs_final(self, tag, g32, recv1, recv2, qc):
        tr = _tile_rows(self.hr, self.sc)
        nb = self.hr // tr
        if self.ax == 1:
            in_map = lambda i, q, c: (c * nb + i, q)
        else:
            in_map = lambda i, q, c: ((q * 2 + c) * nb + i, 0)
        out_map = lambda i, q, c: (c * nb + i, 0)
        return _final_sum(f"rs_final_{tag}", g32, recv1, recv2, qc, part_shape=self.part_shape, out_shape=(self.sr, self.sc),
                          tr=tr, in_map=in_map, out_map=out_map)

    def rs_share(self, ghalf):
        def plan(ro, rw, new, x, y, c):
            piece = self.half(rw[0], c)
            return [(piece, piece, (x, y, 1 - c))], []

        return _Carry([], [ghalf], [], plan, 1)


def _small_allgather(packed):
    nr = packed.shape[0]

    def plan(ro, rw, new, x, y, c):
        me = 4 * x + 2 * y + c
        remote = []
        for fx in (0, 1):
            for fy in (0, 1):
                for fc in (0, 1):
                    if fx or fy or fc:
                        dev = (1 - x if fx else x, 1 - y if fy else y, 1 - c if fc else c)
                        remote.append((ro[0], new[0].at[me], dev))
        return remote, [(ro[0], new[0].at[me])]

    return _Carry([packed], [], [jax.ShapeDtypeStruct((8, nr, 128), F32)], plan, 7, 1)


def _conv_w_allgather(padded, cs):
    def plan(ro, rw, new, x, y, c):
        cols = pl.ds(pl.multiple_of((2 * x + y) * cs, 128), cs)
        remote = [(ro[0], new[0].at[:, cols], (cx, cy, c)) for cx, cy in _chips(x, y)]
        return remote, [(ro[0], new[0].at[:, cols])]

    return _Carry([padded], [], [jax.ShapeDtypeStruct((HALO, cs * N_CHIPS), F32)], plan, 3, 1)


def _pick(n, want):
    if n <= want:
        return n
    for t in range(want, 15, -16):
        if t % 16 == 0 and n % t == 0:
            return t
    raise ValueError(f"no tile for {n} (want {want})")


def _tile_rows(nrows, ncols, budget=2 * 1024 * 1024):
    return _pick(nrows, max(16, (budget // (4 * ncols)) // 16 * 16))


def _pick128(n, want):
    if n <= want:
        return n
    for t in range(want, 127, -128):
        if n % t == 0:
            return t
    raise ValueError(f"no lane tile for {n} (want {want})")


def _pack_rows(parts):
    out, spans, r0 = [], [], 0
    for p in parts:
        flat = p.reshape(-1).astype(F32)
        n = flat.shape[0]
        rows = -(-n // 1024) * 8
        flat = jnp.pad(flat, (0, rows * 128 - n))
        out.append(flat.reshape(rows, 128))
        spans.append((r0, rows, n))
        r0 += rows
    return jnp.concatenate(out, axis=0), spans


def _unpack_rows(packed, spans, shapes):
    res = []
    for (r0, rows, n), shp in zip(spans, shapes, strict=True):
        res.append(packed[r0:r0 + rows].reshape(-1)[:n].reshape(shp))
    return res


def _ident(accs, ex):
    return [accs[0]]


def kernel(x, ffn1_w_gu, ffn1_w_down, ln1_g, ln1_b, w_in, b_in, sgu_ln_g, sgu_ln_b, sgu_w_s, sgu_b_s, w_a_proj, conv_w_dw, conv_b_dw, conv_ln_g, conv_ln_b, w_b_proj, w_out, ln2_g, ln2_b, ffn2_w_gu, ffn2_w_down, ln3_g, ln3_b, loss_target, m_ffn1_w_gu, m_ffn1_w_down, m_ln1_g, m_ln1_b, m_w_in, m_b_in, m_sgu_ln_g, m_sgu_ln_b, m_sgu_w_s, m_sgu_b_s, m_w_a_proj, m_conv_w_dw, m_conv_b_dw, m_conv_ln_g, m_conv_ln_b, m_w_b_proj, m_w_out, m_ln2_g, m_ln2_b, m_ffn2_w_gu, m_ffn2_w_down, m_ln3_g, m_ln3_b, v_ffn1_w_gu, v_ffn1_w_down, v_ln1_g, v_ln1_b, v_w_in, v_b_in, v_sgu_ln_g, v_sgu_ln_b, v_sgu_w_s, v_sgu_b_s, v_w_a_proj, v_conv_w_dw, v_conv_b_dw, v_conv_ln_g, v_conv_ln_b, v_w_b_proj, v_w_out, v_ln2_g, v_ln2_b, v_ffn2_w_gu, v_ffn2_w_down, v_ln3_g, v_ln3_b):
    args = dict(locals())
    assert x.shape[0] == 1 and ffn1_w_gu.shape[0] == 1
    T, D = x.shape[1], x.shape[2]
    F = ffn1_w_down.shape[1] * N_CHIPS
    W = sgu_ln_g.shape[1]
    KW = conv_w_dw.shape[1]
    assert KW - 1 <= HALO and T % SGU_BLOCK == 0

    mx, my, mc = lax.axis_index("x"), lax.axis_index("y"), lax.axis_index("c")
    q = 2 * mx + my
    hc = jnp.reshape(mc, (1,)).astype(jnp.int32)
    qc = jnp.stack([q, mc]).astype(jnp.int32)

    big_names = [("ffn1_w_gu", 1), ("ffn1_w_down", 0), ("w_in", 1), ("w_a_proj", 1), ("w_b_proj", 1), ("w_out", 0),
                 ("ffn2_w_gu", 1), ("ffn2_w_down", 0)]
    B = {n: _Big(n, args[n][0], args["m_" + n][0], args["v_" + n][0], ax) for n, ax in big_names}
    sh16 = {n: _cast_bf16(f"cast_{n}", b.w, _tile_rows(b.sr, b.sc))[0] for n, b in B.items()}

    x2d = x[0]
    tgt = loss_target[0]
    tm_r = _pick(T, 256)
    tm = _pick(T, 1024)
    tm_h = _pick(T, 512)
    tn = _pick128(D, 1024)
    nj = D // tn
    tng = _pick128(D, 512)
    njg = D // tng
    tnf = _pick128(F, 512)
    nf = F // tnf
    tnw = _pick128(W, 1024)
    tnd = _pick128(D, 512)
    tmw = _pick128(W, 512)
    SUB = 256

    def ffn_up(tag, xb_, wgu, carry=None):
        def epi(accs, ex):
            g, u = accs
            s = jax.nn.sigmoid(g)
            sg = g * s
            return [u * (s * (1.0 + g * (1.0 - s))), sg, sg * u]

        return _mm(f"{tag}_up", [(xb_, wgu, 0, 0, 0), (xb_, wgu, 0, 0, nf)], M=T, N=F, tm=tm, tn=tnf, tk=D, nk=1, epilogue=epi,
                   outs=[(F, BF16, 0)] * 3, sum_pairs=False, carry=carry, sub=SUB)

    def ffn_down(tag, act, wd, carry=None):
        return _mm(f"{tag}_down", [(act, wd, 0, 0, 0)], M=T, N=D, tm=tm, tn=_pick128(D, 512), tk=F, nk=1, epilogue=_ident,
                   outs=[(D, F32, 0)], carry=carry)

    def ffn_dact(tag, drh, wd, dgate_f, dup_f, carry=None):
        def epi(accs, ex):
            da = accs[0]
            return [da * ex[0].astype(F32), da * ex[1].astype(F32)]

        return _mm(f"{tag}_dact", [(drh, wd, 0, 0, 0)], tb=True, M=T, N=F, tm=tm, tn=tnf, tk=D, nk=1, epilogue=epi,
                   outs=[(F, BF16, 0)] * 2, extras=[(dgate_f, "mn", 0), (dup_f, "mn", 0)], carry=carry, sub=SUB)

    def ffn_dwdown(tag, act, drh, carry=None):
        return _wgrad(f"{tag}_dwdown", act, drh, M=F, N=D, T=T, tm=tnf, tn=tnd, tk=T, carry=carry)

    def ffn_dwgate(tag, xb_, dg, carry=None):
        return _wgrad(f"{tag}_dwgate", xb_, dg, M=D, N=F, T=T, tm=tnd, tn=tnf, tk=T, ncols=2 * F, carry=carry)

    def ffn_dwup(tag, xb_, du, into, carry=None):
        return _wgrad(f"{tag}_dwup", xb_, du, M=D, N=F, T=T, tm=tnd, tn=tnf, tk=T, ncols=2 * F, into=into, joff=nf, carry=carry)

    def ffn_dx(tag, parts, wgu, addends, carry=None):
        pairs = [(da, wgu, 0, 1 if which == "up" else 0, 0) for which, da in parts]

        def epi(accs, ex):
            tot = accs[0] + ALPHA * ex[0]
            for e in ex[1:]:
                tot = tot + e
            return [tot]

        return _mm(f"{tag}_dx_{'_'.join(w_ for w_, _ in parts)}", pairs, tb=True, M=T, N=D, tm=tm_h,
                   tn=_pick128(D, 512 // len(parts)), tk=F, nk=1, epilogue=epi if addends else _ident, outs=[(D, F32, 0)],
                   extras=[(a, "mn", 0) for a in addends], carry=carry)

    wdw_pad = jnp.pad(conv_w_dw[0], ((0, HALO - KW), (0, 0)))
    c0, un = _merge(B["ffn1_w_gu"].gather_ici(sh16["ffn1_w_gu"]), _conv_w_allgather(wdw_pad, conv_w_dw.shape[2]))
    (wgu1,), (wdw_full,) = un(_exchange("gather_first", c0))
    (xb,), (wgu1,) = _cast_bf16("cast_x", x2d, tm_r, carry=B["ffn1_w_gu"].gather_d2d(wgu1))

    c, un = _merge(B["ffn1_w_down"].gather_ici(sh16["ffn1_w_down"]), B["w_in"].gather_ici(sh16["w_in"]))
    (g1, u1, a1), co = ffn_up("ffn1", xb, wgu1, carry=c)
    (wd1,), (win,) = un(co)
    (wd1,) = _exchange("gather_d2d_ffn1_w_down", B["ffn1_w_down"].gather_d2d(wd1))
    b_gu2, b_d2 = B["ffn2_w_gu"], B["ffn2_w_down"]
    s_gu2, s_d2 = sh16["ffn2_w_gu"], sh16["ffn2_w_down"]
    c, un = _merge(B["w_a_proj"].gather_ici(sh16["w_a_proj"]), B["w_b_proj"].gather_ici(sh16["w_b_proj"]),
                   B["w_out"].gather_ici(sh16["w_out"]), B["w_in"].gather_d2d(win), b_gu2.gather_ici(s_gu2, (0, 4)))
    (fo1,), co = ffn_down("ffn1", a1, wd1, carry=c)
    (wa,), (wb,), (wout,), (win,), (wgu2,) = un(co)
    c, un = _merge(B["w_a_proj"].gather_d2d(wa), B["w_b_proj"].gather_d2d(wb), B["w_out"].gather_d2d(wout),
                   b_gu2.gather_ici(s_gu2, (1, 4), wgu2))
    (x1, x1b, xh1, rs1), co = _ln_fwd("ln1", x2d, fo1, ln1_g, ln1_b, 0.5, tm_r, carry=c)
    (wa,), (wb,), (wout,), (wgu2,) = un(co)

    c, un = _merge(b_gu2.gather_ici(s_gu2, (1, 2), wgu2), b_d2.gather_ici(s_d2, (0, 2)))
    (proj,), co = _mm("in_proj", [(x1b, win, 0, 0, 0)], M=T, N=4 * D, tm=tm, tn=tn, tk=D, nk=1,
                      epilogue=lambda accs, ex: [accs[0] + ex[0]], outs=[(4 * D, F32, 0)], extras=[(b_in, "n", 0)], carry=c)
    (wgu2,), (wd2,) = un(co)
    wm = sgu_w_s[0]
    bst = sgu_b_s[0].T
    sa, z = _sgu_fwd("sgu_fwd", proj, sgu_ln_g, sgu_ln_b, wm, bst, tm_r)
    (zc, sb), (wd2,) = _conv_fwd("conv_fwd", z, wdw_full, KW, conv_b_dw, conv_ln_g, conv_ln_b, tm_r,
                                 carry=b_d2.gather_ici(s_d2, (1, 2), wd2))

    def epi_mix(accs, ex):
        ya_, yb_ = accs
        return [jax.nn.sigmoid(ex[0]) * ya_ + jax.nn.sigmoid(ex[1]) * yb_, ya_, yb_]

    (mixin, ya, yb), (wgu2,) = _mm("branch_proj", [(sa, wa, 0, 0, 0), (sb, wb, 0, 0, 0)], M=T, N=D, tm=tm, tn=tng, tk=W, nk=1,
                                   epilogue=epi_mix, outs=[(D, BF16, 0)] * 3, sum_pairs=False, sub=SUB,
                                   extras=[(proj, "mn", 2 * njg), (proj, "mn", 3 * njg)], carry=B["ffn2_w_gu"].gather_d2d(wgu2))
    (mix,), (wd2,) = _mm("out_proj", [(mixin, wout, 0, 0, 0)], M=T, N=D, tm=tm, tn=tn, tk=D, nk=1, epilogue=_ident,
                         outs=[(D, F32, 0)], carry=B["ffn2_w_down"].gather_d2d(wd2))
    x2, x2b, xh2, rs2 = _ln_fwd("ln2", x1, mix, ln2_g, ln2_b, 1.0, tm_r)
    g2, u2, a2 = ffn_up("ffn2", x2b, wgu2)
    (fo2,) = ffn_down("ffn2", a2, wd2)
    dr3, dr3h, loss_part, dln3_g, dln3_b = _ln3_loss("ln3_loss", x2, fo2, ln3_g, ln3_b, tgt, tm_r)

    b_gu2, b_d2 = B["ffn2_w_gu"], B["ffn2_w_down"]
    dg2, du2 = ffn_dact("ffn2", dr3h, wd2, g2, u2)
    dwd2 = ffn_dwdown("ffn2", a2, dr3h)
    dwgu2, (r1_d2,) = ffn_dwgate("ffn2", x2b, dg2, carry=b_d2.rs_pair(dwd2[1]))
    dwgu2 = ffn_dwup("ffn2", x2b, du2, dwgu2)
    cs_d2 = b_d2.rs_pairsum("ffn2_w_down", dwd2[0], r1_d2, hc)
    c, un = _merge(b_gu2.rs_pair(dwgu2[1]), b_d2.rs_ici(cs_d2))
    (dx2,), co = ffn_dx("ffn2", [("gate", dg2), ("up", du2)], wgu2, [dr3], carry=c)
    (r1_gu2,), (r2_d2,) = un(co)
    cs_gu2 = b_gu2.rs_pairsum("ffn2_w_gu", dwgu2[0], r1_gu2, hc)
    gh_d2 = b_d2.rs_final("ffn2_w_down", dwd2[0], r1_d2, r2_d2, qc)

    (dr2, dr2b, dln2_g, dln2_b), (g_d2,) = _ln_bwd("ln2_bwd", dx2, xh2, rs2, ln2_g, 1.0, tm_r, carry=b_d2.rs_share(gh_d2))

    def epi_dmix(accs, ex):
        dm = accs[0]
        ga, gb = jax.nn.sigmoid(ex[0]), jax.nn.sigmoid(ex[1])
        ya_, yb_ = ex[2].astype(F32), ex[3].astype(F32)
        return [dm * ga, dm * gb, dm * ya_ * (ga * (1.0 - ga)), dm * yb_ * (gb * (1.0 - gb))]

    dya, dyb, dla, dlb = _mm("out_proj_bwd", [(dr2b, wout, 0, 0, 0)], tb=True, M=T, N=D, tm=tm, tn=tng, tk=D, nk=1,
                             epilogue=epi_dmix, outs=[(D, BF16, 0)] * 4, sub=SUB,
                             extras=[(proj, "mn", 2 * njg), (proj, "mn", 3 * njg), (ya, "mn", 0), (yb, "mn", 0)])
    dwout = _wgrad("dw_out", mixin, dr2b, M=D, N=D, T=T, tm=tnd, tn=tnd, tk=T)
    (dsa,) = _mm("a_proj_bwd", [(dya, wa, 0, 0, 0)], tb=True, M=T, N=W, tm=tm, tn=tnw, tk=D, nk=1, epilogue=_ident,
                 outs=[(W, F32, 0)])
    (dsb,) = _mm("b_proj_bwd", [(dyb, wb, 0, 0, 0)], tb=True, M=T, N=W, tm=tm, tn=tnw, tk=D, nk=1, epilogue=_ident,
                 outs=[(W, F32, 0)])
    dwa = _wgrad("dw_a_proj", sa, dya, M=W, N=D, T=T, tm=tmw, tn=tnd, tk=T)
    dwb = _wgrad("dw_b_proj", sb, dyb, M=W, N=D, T=T, tm=tmw, tn=tnd, tk=T)

    dpa, dwm, dbst, dsgu_g, dsgu_b, dbin_a = _sgu_bwd("sgu_bwd", proj, dsa, sgu_ln_g, sgu_ln_b, wm, bst, tm_r)
    dzc, dcln_g, dcln_b, dbdw = _conv_ln_bwd("conv_ln_bwd", dsb, zc, conv_ln_g, conv_ln_b, tm_r)
    dpb, dwdw, dbin_b = _conv_bwd("conv_bwd", dzc, z, proj, wdw_full, KW, tm_r)

    dps = [dpa, dpb, dla, dlb]
    db_in = jnp.concatenate([dbin_a, dbin_b, _colsum_rows("db_in_gate_a", dla, tm_r), _colsum_rows("db_in_gate_b", dlb, tm_r)],
                            axis=1)
    (dx1,), (r2_gu2,) = _mm("in_proj_bwd", [(dp, win, 0, k, 0) for k, dp in enumerate(dps)], tb=True, M=T, N=D, tm=tm_h,
                            tn=tnd, tk=D, nk=1, epilogue=lambda accs, ex: [accs[0] + ALPHA * ex[0]], outs=[(D, F32, 0)],
                            extras=[(dr2, "mn", 0)], carry=b_gu2.rs_ici(cs_gu2))
    gh_gu2 = b_gu2.rs_final("ffn2_w_gu", dwgu2[0], r1_gu2, r2_gu2, qc)
    dwin, (g_gu2,) = _wgrad("dw_in_0", x1b, dps[0], M=D, N=D, T=T, tm=tnd, tn=tnd, tk=T, ncols=4 * D,
                            carry=b_gu2.rs_share(gh_gu2))
    for k in range(1, 4):
        dwin = _wgrad(f"dw_in_{k}", x1b, dps[k], M=D, N=D, T=T, tm=tnd, tn=tnd, tk=T, ncols=4 * D, into=dwin,
                      joff=k * (D // tnd))

    mix_names = ["w_in", "w_a_proj", "w_b_proj", "w_out"]
    mix_grads = dict(zip(mix_names, [dwin, dwa, dwb, dwout], strict=True))
    c, un = _merge(*[B[n].rs_pair(mix_grads[n][1]) for n in mix_names])
    (dr1, dr1h, dln1_g, dln1_b), co = _ln_bwd("ln1_bwd", dx1, xh1, rs1, ln1_g, 0.5, tm_r, carry=c)
    r1_mix = {n: r1 for n, (r1,) in zip(mix_names, un(co), strict=True)}
    cs_mix = {n: B[n].rs_pairsum(n, mix_grads[n][0], r1_mix[n], hc) for n in mix_names}

    small_names = ["ln1_g", "ln1_b", "b_in", "sgu_ln_g", "sgu_ln_b", "sgu_w_s", "sgu_b_s", "conv_w_dw", "conv_b_dw",
                   "conv_ln_g", "conv_ln_b", "ln2_g", "ln2_b", "ln3_g", "ln3_b"]
    small_parts = {"ln1_g": dln1_g, "ln1_b": dln1_b, "b_in": db_in, "sgu_ln_g": dsgu_g, "sgu_ln_b": dsgu_b, "sgu_w_s": dwm,
                   "sgu_b_s": dbst.T, "conv_w_dw": dwdw[:KW], "conv_b_dw": dbdw, "conv_ln_g": dcln_g, "conv_ln_b": dcln_b,
                   "ln2_g": dln2_g, "ln2_b": dln2_b, "ln3_g": dln3_g, "ln3_b": dln3_b}
    packed, spans = _pack_rows([small_parts[n] for n in small_names])

    b_gu1, b_d1 = B["ffn1_w_gu"], B["ffn1_w_down"]
    c, un = _merge(*[B[n].rs_ici(cs_mix[n]) for n in mix_names], _small_allgather(packed))
    (dg1, du1), co = ffn_dact("ffn1", dr1h, wd1, g1, u1, carry=c)
    *r2_mix, (small_slots,) = un(co)
    gh_mix = [B[n].rs_final(n, mix_grads[n][0], r1_mix[n], r2, qc) for n, (r2,) in zip(mix_names, r2_mix, strict=True)]
    c, un = _merge(*[B[n].rs_share(gh) for n, gh in zip(mix_names, gh_mix, strict=True)])
    dwgu1, co = ffn_dwgate("ffn1", xb, dg1, carry=c)
    g_mix = {n: g for n, (g,) in zip(mix_names, un(co), strict=True)}
    dwgu1 = ffn_dwup("ffn1", xb, du1, dwgu1)
    dwd1, (r1_gu1,) = ffn_dwdown("ffn1", a1, dr1h, carry=b_gu1.rs_pair(dwgu1[1]))
    cs_gu1 = b_gu1.rs_pairsum("ffn1_w_gu", dwgu1[0], r1_gu1, hc)
    c, un = _merge(b_gu1.rs_ici(cs_gu1), b_d1.rs_pair(dwd1[1]))
    (dx_gate,), co = ffn_dx("ffn1", [("gate", dg1)], wgu1, [], carry=c)
    (r2_gu1,), (r1_d1,) = un(co)
    cs_d1 = b_d1.rs_pairsum("ffn1_w_down", dwd1[0], r1_d1, hc)
    gh_gu1 = b_gu1.rs_final("ffn1_w_gu", dwgu1[0], r1_gu1, r2_gu1, qc)
    c, un = _merge(b_d1.rs_ici(cs_d1), b_gu1.rs_share(gh_gu1))
    (dx,), co = ffn_dx("ffn1", [("up", du1)], wgu1, [dr1, dx_gate], carry=c)
    (r2_d1,), (g_gu1,) = un(co)
    gh_d1 = b_d1.rs_final("ffn1_w_down", dwd1[0], r1_d1, r2_d1, qc)

    grads = {"ffn1_w_gu": g_gu1, "ffn2_w_gu": g_gu2, "ffn2_w_down": g_d2, **g_mix}
    outs_g, outs_d, outs_m, outs_v = {}, {}, {}, {}

    def adamw_big(n, g, carry=None):
        b = B[n]
        return _adamw(f"adamw_{n}", b.w, g, b.m, b.v, _tile_rows(b.sr, b.sc, 1024 * 1024), carry=carry)

    upd = {}
    upd["w_a_proj"], (grads["ffn1_w_down"],) = adamw_big("w_a_proj", grads["w_a_proj"], carry=b_d1.rs_share(gh_d1))
    for n, _ in big_names:
        if n not in upd:
            upd[n] = adamw_big(n, grads[n])
        d_, m_, v_ = upd[n]
        outs_g[n], outs_d[n], outs_m[n], outs_v[n] = grads[n][None], d_[None], m_[None], v_[None]
    gsum = _sum8("small_sum", small_slots)
    full_shapes = [args[n].shape if n != "conv_w_dw" else (1, KW, W) for n in small_names]
    gsmall = dict(zip(small_names, _unpack_rows(gsum, spans, full_shapes), strict=True))
    cs = conv_w_dw.shape[2]
    gsmall["conv_w_dw"] = lax.dynamic_slice_in_dim(gsmall["conv_w_dw"], q * cs, cs, axis=2)
    pw, spans2 = _pack_rows([args[n] for n in small_names])
    pg, _ = _pack_rows([gsmall[n] for n in small_names])
    pm, _ = _pack_rows([args["m_" + n] for n in small_names])
    pv, _ = _pack_rows([args["v_" + n] for n in small_names])
    pd, pmn, pvn = _adamw("adamw_small", pw, pg, pm, pv, pw.shape[0])
    shapes2 = [args[n].shape for n in small_names]
    for dst, src in ((outs_d, pd), (outs_m, pmn), (outs_v, pvn)):
        dst.update(zip(small_names, _unpack_rows(src, spans2, shapes2), strict=True))
    outs_g.update(gsmall)

    loss = lax.psum(loss_part[0, 0], ("x", "y", "c"))
    order = ["ffn1_w_gu", "ffn1_w_down", "ln1_g", "ln1_b", "w_in", "b_in", "sgu_ln_g", "sgu_ln_b", "sgu_w_s", "sgu_b_s",
             "w_a_proj", "conv_w_dw", "conv_b_dw", "conv_ln_g", "conv_ln_b", "w_b_proj", "w_out", "ln2_g", "ln2_b",
             "ffn2_w_gu", "ffn2_w_down", "ln3_g", "ln3_b"]
    return (loss, dx[None], *[outs_g[n] for n in order], *[outs_d[n] for n in order], *[outs_m[n] for n in order],
            *[outs_v[n] for n in order])
```

```python
import math

import jax
import jax.numpy as jnp
from jax import lax
from jax.experimental import pallas as pl
from jax.experimental.pallas import tpu as pltpu

BF16 = jnp.bfloat16
F32 = jnp.float32

LN_EPS = 1e-5
ALPHA = 2.0 ** 0.25
SGU_BLOCK = 128
SGU_CHUNK = 64
HALO = 32
SUBLANES = 8
LANES = 128
CONV_ROWS = 32
ADAM_LR = 0.001
ADAM_B1 = 0.9
ADAM_B2 = 0.999
ADAM_EPS = 1e-08
ADAM_WD = 0.01
ADAM_STEP = 10
N_CHIPS = 4
VMEM_LIMIT = 52 * 1024 * 1024
MESH = pl.DeviceIdType.MESH

_GELU_C0 = math.sqrt(2.0 / math.pi)
_GELU_C1 = 0.044715


def _cparams(sem):
    return pltpu.CompilerParams(dimension_semantics=sem, vmem_limit_bytes=VMEM_LIMIT)


def _gelu_parts(x):
    x2 = x * x
    t = jnp.tanh(_GELU_C0 * (x + _GELU_C1 * (x2 * x)))
    return 0.5 * (1.0 + t), t, x2


def _gelu(x):
    cdf, _, _ = _gelu_parts(x)
    return x * cdf


def _gelu_and_grad(x):
    cdf, t, x2 = _gelu_parts(x)
    grad = cdf + x * (0.5 * (1.0 - t * t)) * (_GELU_C0 * (1.0 + (3.0 * _GELU_C1) * x2))
    return x * cdf, grad


def _silu_grad(x):
    s = jax.nn.sigmoid(x)
    return s * (1.0 + x * (1.0 - s))


def _row_stats(x):
    mu = jnp.mean(x, axis=-1, keepdims=True)
    xc = x - mu
    var = jnp.mean(xc * xc, axis=-1, keepdims=True)
    rstd = lax.rsqrt(var + LN_EPS)
    return xc * rstd, rstd


def _ln_bwd_rows(dy, xhat, rstd, g):
    dxh = dy * g
    m1 = jnp.mean(dxh, axis=-1, keepdims=True)
    m2 = jnp.mean(dxh * xhat, axis=-1, keepdims=True)
    return rstd * (dxh - m1 - xhat * m2)


def _colsum(v):
    return jnp.sum(v, axis=0, keepdims=True)


class _Carry:
    def __init__(self, ro, rw, new, plan, n_remote, n_local=0):
        self.ro, self.rw, self.new, self.plan = list(ro), list(rw), list(new), plan
        self.n_remote, self.n_local = n_remote, n_local

    def sems(self):
        return [pltpu.SemaphoreType.DMA((self.n_remote,)), pltpu.SemaphoreType.DMA((self.n_remote,)),
                pltpu.SemaphoreType.DMA((max(self.n_local, 1),))]

    def copies(self, ro_refs, rw_refs, new_refs, send_sems, recv_sems, loc_sems):
        x, y, c = lax.axis_index("x"), lax.axis_index("y"), lax.axis_index("c")
        remote, local = self.plan(ro_refs, rw_refs, new_refs, x, y, c)
        assert len(remote) == self.n_remote and len(local) == self.n_local
        lcs = [pltpu.make_async_copy(s, d, loc_sems.at[k]) for k, (s, d) in enumerate(local)]
        rcs = [pltpu.make_async_remote_copy(src_ref=s, dst_ref=d, send_sem=send_sems.at[k], recv_sem=recv_sems.at[k],
                                            device_id=dev, device_id_type=MESH) for k, (s, d, dev) in enumerate(remote)]
        return lcs, rcs

    def out_shape(self):
        return [jax.ShapeDtypeStruct(a.shape, a.dtype) for a in self.rw] + self.new


def _start_all(lcs, rcs):
    for cp in lcs + rcs:
        cp.start()


def _wait_all(lcs, rcs):
    for cp in rcs:
        cp.wait_send()
    for cp in rcs:
        cp.wait_recv()
    for cp in lcs:
        cp.wait()


def _merge(*cs):
    ro = [a for c in cs for a in c.ro]
    rw = [a for c in cs for a in c.rw]
    new = [a for c in cs for a in c.new]

    def plan(ro_refs, rw_refs, new_refs, x, y, c):
        remote, local, a, b, d = [], [], 0, 0, 0
        for cc in cs:
            r, l = cc.plan(ro_refs[a:a + len(cc.ro)], rw_refs[b:b + len(cc.rw)], new_refs[d:d + len(cc.new)], x, y, c)
            a, b, d = a + len(cc.ro), b + len(cc.rw), d + len(cc.new)
            remote += r
            local += l
        return remote, local

    def unpack(couts):
        res, b, d = [], 0, len(rw)
        for cc in cs:
            res.append(list(couts[b:b + len(cc.rw)]) + list(couts[d:d + len(cc.new)]))
            b, d = b + len(cc.rw), d + len(cc.new)
        return res

    return _Carry(ro, rw, new, plan, sum(c.n_remote for c in cs), sum(c.n_local for c in cs)), unpack


_ANY = pl.BlockSpec(memory_space=pl.ANY)


def _exchange(name, carry):
    n_ro, n_rw, n_new = len(carry.ro), len(carry.rw), len(carry.new)

    def body(*refs):
        o0 = n_ro + n_rw
        lcs, rcs = carry.copies(refs[:n_ro], refs[o0:o0 + n_rw], refs[o0 + n_rw:o0 + n_rw + n_new], *refs[o0 + n_rw + n_new:])
        _start_all(lcs, rcs)
        _wait_all(lcs, rcs)

    return list(pl.pallas_call(
        body, name=name, in_specs=[_ANY] * (n_ro + n_rw), out_specs=[_ANY] * (n_rw + n_new), out_shape=carry.out_shape(),
        input_output_aliases={n_ro + k: k for k in range(n_rw)}, scratch_shapes=carry.sems())(*carry.ro, *carry.rw))


def _call(body, *, name, grid, in_specs, out_specs, out_shape, args, scratch=(), sem, carry=None, aliases=None):
    in_specs, out_specs, out_shape, scratch = list(in_specs), list(out_specs), list(out_shape), list(scratch)
    if carry is None:
        return list(pl.pallas_call(body, name=name, grid=grid, in_specs=in_specs, out_specs=out_specs, out_shape=out_shape,
                                   scratch_shapes=scratch, input_output_aliases=aliases or {},
                                   compiler_params=_cparams(sem))(*args))
    n_in, n_out, n_scr = len(in_specs), len(out_specs), len(scratch)
    n_ro, n_rw, n_new = len(carry.ro), len(carry.rw), len(carry.new)

    def wrapped(*refs):
        ins = refs[:n_in]
        ro_refs = refs[n_in:n_in + n_ro]
        o0 = n_in + n_ro + n_rw
        outs = refs[o0:o0 + n_out]
        rw_refs = refs[o0 + n_out:o0 + n_out + n_rw]
        new_refs = refs[o0 + n_out + n_rw:o0 + n_out + n_rw + n_new]
        s0 = o0 + n_out + n_rw + n_new
        scr = refs[s0:s0 + n_scr]
        sems = refs[s0 + n_scr:]
        first = pl.program_id(0) == 0
        last = pl.program_id(0) == grid[0] - 1
        for d in range(1, len(grid)):
            first = jnp.logical_and(first, pl.program_id(d) == 0)
            last = jnp.logical_and(last, pl.program_id(d) == grid[d] - 1)

        @pl.when(first)
        def _():
            _start_all(*carry.copies(ro_refs, rw_refs, new_refs, *sems))

        body(*ins, *outs, *scr)

        @pl.when(last)
        def _():
            _wait_all(*carry.copies(ro_refs, rw_refs, new_refs, *sems))

    al = dict(aliases or {})
    al.update({n_in + n_ro + k: n_out + k for k in range(n_rw)})
    res = pl.pallas_call(
        wrapped, name=name, grid=grid, in_specs=in_specs + [_ANY] * (n_ro + n_rw), out_specs=out_specs + [_ANY] * (n_rw + n_new),
        out_shape=out_shape + carry.out_shape(), scratch_shapes=scratch + carry.sems(), input_output_aliases=al,
        compiler_params=_cparams(("arbitrary",) * len(grid)))(*args, *carry.ro, *carry.rw)
    return list(res[:n_out]), list(res[n_out:])


def _mm(name, pairs, *, ta=False, tb=False, M, N, tm, tn, tk, nk, epilogue, outs, extras=(), sum_pairs=True, carry=None,
        sub=None):
    ni, nj = M // tm, N // tn
    assert ni * tm == M and nj * tn == N
    n_p = len(pairs)
    n_acc = 1 if sum_pairs else n_p
    in_specs, args = [], []
    for a, b, ak, bk, bj in pairs:
        if ta:
            in_specs.append(pl.BlockSpec((tk, tm), lambda i, j, k, ak=ak: (k + ak, i)))
        else:
            in_specs.append(pl.BlockSpec((tm, tk), lambda i, j, k, ak=ak: (i, k + ak)))
        if tb:
            in_specs.append(pl.BlockSpec((tn, tk), lambda i, j, k, bk=bk, bj=bj: (j + bj, k + bk)))
        else:
            in_specs.append(pl.BlockSpec((tk, tn), lambda i, j, k, bk=bk, bj=bj: (k + bk, j + bj)))
        args += [a, b]
    for arr, kind, jo in extras:
        if kind == "mn":
            in_specs.append(pl.BlockSpec((tm, tn), lambda i, j, k, jo=jo: (i, j + jo)))
        else:
            in_specs.append(pl.BlockSpec((1, tn), lambda i, j, k, jo=jo: (0, j + jo)))
        args.append(arr)
    out_specs = [pl.BlockSpec((tm, tn), lambda i, j, k, jo=jo: (i, j + jo)) for _, _, jo in outs]
    out_shape = [jax.ShapeDtypeStruct((M, nc), dt) for nc, dt, _ in outs]
    n_ex, n_out = len(extras), len(outs)
    dn = (((0 if ta else 1,), (1 if tb else 0,)), ((), ()))

    def body(*refs):
        ab = refs[: 2 * n_p]
        ex = refs[2 * n_p: 2 * n_p + n_ex]
        o0 = 2 * n_p + n_ex
        out_refs = refs[o0: o0 + n_out]
        acc_refs = refs[o0 + n_out:]

        def dots():
            res = []
            for p in range(n_p):
                a = ab[2 * p][...]
                b = ab[2 * p + 1][...]
                res.append(lax.dot_general(a.astype(BF16), b.astype(BF16), dn, preferred_element_type=F32))
            if sum_pairs:
                tot = res[0]
                for r in res[1:]:
                    tot = tot + r
                res = [tot]
            return res

        def finish(accs):
            tiles = epilogue(accs, [e[...] for e in ex])
            for r, t in zip(out_refs, tiles, strict=True):
                r[...] = t.astype(r.dtype)

        if nk == 1 and sub is not None and tn > sub:
            for s in range(tn // sub):
                cs = slice(s * sub, (s + 1) * sub)
                res = []
                for p in range(n_p):
                    b = ab[2 * p + 1][cs, :] if tb else ab[2 * p + 1][:, cs]
                    res.append(lax.dot_general(ab[2 * p][...].astype(BF16), b.astype(BF16), dn, preferred_element_type=F32))
                if sum_pairs:
                    tot = res[0]
                    for r in res[1:]:
                        tot = tot + r
                    res = [tot]
                tiles = epilogue(res, [e[:, cs] for e in ex])
                for r, t in zip(out_refs, tiles, strict=True):
                    r[:, cs] = t.astype(r.dtype)
        elif nk == 1:
            finish(dots())
        else:
            k = pl.program_id(2)
            d = dots()

            @pl.when(k == 0)
            def _():
                for r, v in zip(acc_refs, d, strict=True):
                    r[...] = v

            @pl.when(k > 0)
            def _():
                for r, v in zip(acc_refs, d, strict=True):
                    r[...] += v

            @pl.when(k == nk - 1)
            def _():
                finish([r[...] for r in acc_refs])

    scratch = [pltpu.VMEM((tm, tn), F32) for _ in range(n_acc)] if nk > 1 else []
    return _call(body, name=name, grid=(ni, nj, nk), in_specs=in_specs, out_specs=out_specs, out_shape=out_shape, args=args,
                 scratch=scratch, sem=("parallel", "parallel", "arbitrary"), carry=carry)


def _wgrad(name, a, b, *, M, N, T, tm, tn, tk, into=None, joff=0, ncols=None, carry=None):
    ncols = N if ncols is None else ncols
    ni, nj, nk = M // tm, N // tn, T // tk
    dn = (((0,), (0,)), ((), ()))

    def body(a_ref, b_ref, *rest):
        d = lax.dot_general(a_ref[...].astype(BF16), b_ref[...].astype(BF16), dn, preferred_element_type=F32)
        if nk == 1:
            of_ref, oh_ref = rest[-2:]
            of_ref[...] = d
            oh_ref[...] = d.astype(BF16)
            return
        of_ref, oh_ref, acc_ref = rest[-3:]
        k = pl.program_id(2)

        @pl.when(k == 0)
        def _():
            acc_ref[...] = d

        @pl.when(k > 0)
        def _():
            acc_ref[...] += d

        @pl.when(k == nk - 1)
        def _():
            of_ref[...] = acc_ref[...]
            oh_ref[...] = acc_ref[...].astype(BF16)

    ospec = pl.BlockSpec((tm, tn), lambda i, j, k: (i, j + joff))
    in_specs = [pl.BlockSpec((tk, tm), lambda i, j, k: (k, i)), pl.BlockSpec((tk, tn), lambda i, j, k: (k, j))]
    args, aliases = [a, b], None
    if into is not None:
        in_specs += [_ANY, _ANY]
        args += list(into)
        aliases = {2: 0, 3: 1}
    return _call(body, name=name, grid=(ni, nj, nk), in_specs=in_specs, out_specs=[ospec, ospec],
                 out_shape=[jax.ShapeDtypeStruct((M, ncols), F32), jax.ShapeDtypeStruct((M, ncols), BF16)], args=args,
                 scratch=[pltpu.VMEM((tm, tn), F32)] if nk > 1 else [], sem=("parallel", "parallel", "arbitrary"), carry=carry,
                 aliases=aliases)


def _rows(tm, c, cb=0):
    return pl.BlockSpec((tm, c), lambda i, cb=cb: (i, cb))


def _whole(shape):
    nd = len(shape)
    return pl.BlockSpec(shape, lambda i, nd=nd: (0,) * nd)


def _cast_bf16(name, x, tm, carry=None):
    t, d = x.shape

    def body(x_ref, o_ref):
        o_ref[...] = x_ref[...].astype(BF16)

    return _call(body, name=name, grid=(t // tm,), in_specs=[_rows(tm, d)], out_specs=[_rows(tm, d)],
                 out_shape=[jax.ShapeDtypeStruct((t, d), BF16)], args=[x], sem=("parallel",), carry=carry)


def _ln_fwd(name, xres, f, g, b, cf, tm, carry=None):
    t, d = xres.shape

    def body(x_ref, f_ref, g_ref, b_ref, y_ref, yb_ref, xh_ref, rs_ref):
        r = ALPHA * x_ref[...] + cf * f_ref[...]
        xhat, rstd = _row_stats(r)
        y = xhat * g_ref[...] + b_ref[...]
        y_ref[...] = y
        yb_ref[...] = y.astype(BF16)
        xh_ref[...] = xhat
        rs_ref[...] = rstd

    return _call(
        body, name=name, grid=(t // tm,), in_specs=[_rows(tm, d), _rows(tm, d), _whole((1, d)), _whole((1, d))],
        out_specs=[_rows(tm, d), _rows(tm, d), _rows(tm, d), _rows(tm, 1)],
        out_shape=[jax.ShapeDtypeStruct((t, d), F32), jax.ShapeDtypeStruct((t, d), BF16),
                   jax.ShapeDtypeStruct((t, d), F32), jax.ShapeDtypeStruct((t, 1), F32)],
        args=[xres, f, g, b], sem=("parallel",), carry=carry)


def _ln_bwd(name, dy, xhat, rstd, g, scale, tm, carry=None):
    t, d = dy.shape

    def body(dy_ref, xh_ref, rs_ref, g_ref, dr_ref, drb_ref, dg_ref, db_ref):
        i = pl.program_id(0)
        dy_v, xh = dy_ref[...], xh_ref[...]
        dr = _ln_bwd_rows(dy_v, xh, rs_ref[...], g_ref[...])
        dr_ref[...] = dr
        drb_ref[...] = (scale * dr).astype(BF16)

        @pl.when(i == 0)
        def _():
            dg_ref[...] = jnp.zeros_like(dg_ref)
            db_ref[...] = jnp.zeros_like(db_ref)

        dg_ref[...] += _colsum(dy_v * xh)
        db_ref[...] += _colsum(dy_v)

    return _call(
        body, name=name, grid=(t // tm,), in_specs=[_rows(tm, d), _rows(tm, d), _rows(tm, 1), _whole((1, d))],
        out_specs=[_rows(tm, d), _rows(tm, d), _whole((1, d)), _whole((1, d))],
        out_shape=[jax.ShapeDtypeStruct((t, d), F32), jax.ShapeDtypeStruct((t, d), BF16),
                   jax.ShapeDtypeStruct((1, d), F32), jax.ShapeDtypeStruct((1, d), F32)],
        args=[dy, xhat, rstd, g], sem=("arbitrary",), carry=carry)


def _ln3_loss(name, xres, f, g, b, target, tm):
    t, d = xres.shape

    def body(x_ref, f_ref, g_ref, b_ref, tg_ref, dr_ref, drb_ref, loss_ref, dg_ref, db_ref):
        i = pl.program_id(0)
        r = ALPHA * x_ref[...] + 0.5 * f_ref[...]
        xhat, rstd = _row_stats(r)
        gv = g_ref[...]
        y = xhat * gv + b_ref[...]
        err = y - tg_ref[...]
        dy = err * (1.0 / d)
        dr = _ln_bwd_rows(dy, xhat, rstd, gv)
        dr_ref[...] = dr
        drb_ref[...] = (0.5 * dr).astype(BF16)
        part = 0.5 * jnp.sum(jnp.mean(err * err, axis=-1, keepdims=True), axis=0, keepdims=True)

        @pl.when(i == 0)
        def _():
            loss_ref[...] = jnp.zeros_like(loss_ref)
            dg_ref[...] = jnp.zeros_like(dg_ref)
            db_ref[...] = jnp.zeros_like(db_ref)

        loss_ref[...] += jnp.broadcast_to(part, loss_ref.shape)
        dg_ref[...] += _colsum(dy * xhat)
        db_ref[...] += _colsum(dy)

    return _call(
        body, name=name, grid=(t // tm,),
        in_specs=[_rows(tm, d), _rows(tm, d), _whole((1, d)), _whole((1, d)), _rows(tm, d)],
        out_specs=[_rows(tm, d), _rows(tm, d), _whole((8, 128)), _whole((1, d)), _whole((1, d))],
        out_shape=[jax.ShapeDtypeStruct((t, d), F32), jax.ShapeDtypeStruct((t, d), BF16),
                   jax.ShapeDtypeStruct((8, 128), F32), jax.ShapeDtypeStruct((1, d), F32),
                   jax.ShapeDtypeStruct((1, d), F32)],
        args=[xres, f, g, b, target], sem=("arbitrary",))


def _colsum_rows(name, x, tm):
    t, d = x.shape

    def body(x_ref, o_ref):
        @pl.when(pl.program_id(0) == 0)
        def _():
            o_ref[...] = jnp.zeros_like(o_ref)

        o_ref[...] += _colsum(x_ref[...].astype(F32))

    return _call(body, name=name, grid=(t // tm,), in_specs=[_rows(tm, d)], out_specs=[_whole((1, d))],
                 out_shape=[jax.ShapeDtypeStruct((1, d), F32)], args=[x], sem=("arbitrary",))[0]


def _sgu_mask():
    sh = SGU_CHUNK.bit_length() - 1
    r = lax.shift_right_logical(lax.broadcasted_iota(jnp.int32, (SGU_BLOCK, SGU_BLOCK), 0), sh)
    c = lax.shift_right_logical(lax.broadcasted_iota(jnp.int32, (SGU_BLOCK, SGU_BLOCK), 1), sh)
    return c <= r


def _sgu_fwd(name, p, lng, lnb, wm, bst, tm):
    t = p.shape[0]
    n_grp, w = wm.shape[0], lng.shape[1]
    hd = w // n_grp
    nblk = tm // SGU_BLOCK

    def body(uv_ref, h_ref, g_ref, b_ref, wm_ref, bs_ref, sa_ref, z_ref, vn_s):
        xhat, _ = _row_stats(_gelu(uv_ref[:, w:]))
        vn_s[...] = (xhat * g_ref[...] + b_ref[...]).astype(BF16)
        mask = _sgu_mask()
        for h in range(n_grp):
            wh = jnp.where(mask, wm_ref[h], 0.0).astype(BF16)
            bcol = bs_ref[:, h:h + 1]
            cs = slice(h * hd, (h + 1) * hd)
            for n in range(nblk):
                rs = slice(n * SGU_BLOCK, (n + 1) * SGU_BLOCK)
                s = jnp.dot(wh, vn_s[rs, cs], preferred_element_type=F32) + bcol
                sa_ref[rs, cs] = (_gelu(uv_ref[rs, cs]) * s).astype(BF16)
        z_ref[...] = h_ref[:, :w] * jax.nn.sigmoid(h_ref[:, w:])

    return _call(
        body, name=name, grid=(t // tm,),
        in_specs=[_rows(tm, 2 * w, 0), _rows(tm, 2 * w, 1), _whole((1, w)), _whole((1, w)), _whole(wm.shape),
                  _whole(bst.shape)],
        out_specs=[_rows(tm, w), _rows(tm, w)],
        out_shape=[jax.ShapeDtypeStruct((t, w), BF16), jax.ShapeDtypeStruct((t, w), F32)],
        args=[p, p, lng, lnb, wm, bst], scratch=[pltpu.VMEM((tm, w), BF16)], sem=("parallel",))


def _sgu_bwd(name, p, dsa, lng, lnb, wm, bst, tm):
    t = p.shape[0]
    n_grp, w = wm.shape[0], lng.shape[1]
    hd = w // n_grp
    nblk = tm // SGU_BLOCK

    def body(uv_ref, dsa_ref, g_ref, b_ref, wm_ref, bs_ref, dp_ref, dwm_ref, dbs_ref, dg_ref, db_ref, dbin_ref,
             vn_s, ug_s, dvn_s, dug_s):
        i = pl.program_id(0)

        @pl.when(i == 0)
        def _():
            dwm_ref[...] = jnp.zeros_like(dwm_ref)
            dbs_ref[...] = jnp.zeros_like(dbs_ref)
            dg_ref[...] = jnp.zeros_like(dg_ref)
            db_ref[...] = jnp.zeros_like(db_ref)
            dbin_ref[...] = jnp.zeros_like(dbin_ref)

        ug, dgelu_u = _gelu_and_grad(uv_ref[:, :w])
        ug_s[...] = ug
        vg, dgelu_v = _gelu_and_grad(uv_ref[:, w:])
        xhat, rstd = _row_stats(vg)
        gv = g_ref[...]
        vn_s[...] = (xhat * gv + b_ref[...]).astype(BF16)
        mask = _sgu_mask()
        for h in range(n_grp):
            wh = jnp.where(mask, wm_ref[h], 0.0).astype(BF16)
            bcol = bs_ref[:, h:h + 1]
            cs = slice(h * hd, (h + 1) * hd)
            dw_h = jnp.zeros((SGU_BLOCK, SGU_BLOCK), F32)
            dbs_h = jnp.zeros((SGU_BLOCK, 1), F32)
            for n in range(nblk):
                rs = slice(n * SGU_BLOCK, (n + 1) * SGU_BLOCK)
                vblk = vn_s[rs, cs]
                s = jnp.dot(wh, vblk, preferred_element_type=F32) + bcol
                dsa_blk = dsa_ref[rs, cs]
                dug_s[rs, cs] = dsa_blk * s
                ds = dsa_blk * ug_s[rs, cs]
                dsb = ds.astype(BF16)
                dvn_s[rs, cs] = lax.dot_general(wh, dsb, (((0,), (0,)), ((), ())), preferred_element_type=F32)
                dw_h = dw_h + lax.dot_general(dsb, vblk, (((1,), (1,)), ((), ())), preferred_element_type=F32)
                dbs_h = dbs_h + jnp.sum(ds, axis=1, keepdims=True)
            dwm_ref[h] += jnp.where(mask, dw_h, 0.0)
            dbs_ref[:, h:h + 1] += dbs_h
        dvn = dvn_s[...]
        dg_ref[...] += _colsum(dvn * xhat)
        db_ref[...] += _colsum(dvn)
        dvg = _ln_bwd_rows(dvn, xhat, rstd, gv)
        du = dug_s[...] * dgelu_u
        dv = dvg * dgelu_v
        dp_ref[:, :w] = du.astype(BF16)
        dp_ref[:, w:] = dv.astype(BF16)
        dbin_ref[:, :w] += _colsum(du)
        dbin_ref[:, w:] += _colsum(dv)

    return _call(
        body, name=name, grid=(t // tm,),
        in_specs=[_rows(tm, 2 * w, 0), _rows(tm, w), _whole((1, w)), _whole((1, w)), _whole(wm.shape), _whole(bst.shape)],
        out_specs=[_rows(tm, 2 * w), _whole(wm.shape), _whole(bst.shape), _whole((1, w)), _whole((1, w)), _whole((1, 2 * w))],
        out_shape=[jax.ShapeDtypeStruct((t, 2 * w), BF16), jax.ShapeDtypeStruct(wm.shape, F32),
                   jax.ShapeDtypeStruct(bst.shape, F32), jax.ShapeDtypeStruct((1, w), F32), jax.ShapeDtypeStruct((1, w), F32),
                   jax.ShapeDtypeStruct((1, 2 * w), F32)],
        args=[p, dsa, lng, lnb, wm, bst],
        scratch=[pltpu.VMEM((tm, w), BF16), pltpu.VMEM((tm, w), F32), pltpu.VMEM((tm, w), F32), pltpu.VMEM((tm, w), F32)],
        sem=("arbitrary",))


def _halo_prev(tm, c):
    return pl.BlockSpec((HALO, c), lambda i: (jnp.maximum(i * (tm // HALO) - 1, 0), 0))


def _halo_next(tm, c, t):
    last = t // HALO - 1
    return pl.BlockSpec((HALO, c), lambda i: (jnp.minimum((i + 1) * (tm // HALO), last), 0))


def _shifted_copies(sh, n):
    for r in range(1, SUBLANES):
        sh[r, :n - SUBLANES, :] = sh[0, r:r + n - SUBLANES, :]


def _row_broadcasts(wb, w_ref, kw):
    for k in range(kw):
        wb[k] = jnp.broadcast_to(w_ref[k:k + 1, :], wb.shape[1:])


def _tap(sh, r0, o, rows, cols):
    return sh[o % SUBLANES, pl.ds(pl.multiple_of(r0 + (o - o % SUBLANES), SUBLANES), rows), cols]


def _conv_fwd(name, z, wdw, kw, bdw, lng, lnb, tm, carry=None):
    t, c = z.shape
    lead = HALO - (kw - 1)
    n = tm + HALO

    def body(zp_ref, z_ref, w_ref, bdw_ref, g_ref, b_ref, zc_ref, sb_ref, sh, wb):
        i = pl.program_id(0)
        sh[0, :HALO, :] = jnp.where(i > 0, zp_ref[...], 0.0)
        sh[0, HALO:, :] = z_ref[...]
        _shifted_copies(sh, n)
        _row_broadcasts(wb, w_ref, kw)
        bias = jnp.broadcast_to(bdw_ref[...], (SUBLANES, c))
        groups = CONV_ROWS // SUBLANES

        def chunk(ci, _):
            r0 = pl.multiple_of(ci * CONV_ROWS, CONV_ROWS)
            accs = [bias] * groups
            for k in range(kw):
                wk = wb[k]
                tp = _tap(sh, r0, lead + k, CONV_ROWS, slice(None))
                accs = [accs[g] + wk * tp[g * SUBLANES:(g + 1) * SUBLANES] for g in range(groups)]
            zc_ref[pl.ds(r0, CONV_ROWS), :] = jnp.concatenate(accs, axis=0)
            return 0

        lax.fori_loop(0, tm // CONV_ROWS, chunk, 0)
        xhat, _ = _row_stats(zc_ref[...])
        zn = xhat * g_ref[...] + b_ref[...]
        sb_ref[...] = (zn * jax.nn.sigmoid(zn)).astype(BF16)

    return _call(
        body, name=name, grid=(t // tm,),
        in_specs=[_halo_prev(tm, c), _rows(tm, c), _whole(wdw.shape), _whole((1, c)), _whole((1, c)), _whole((1, c))],
        out_specs=[_rows(tm, c), _rows(tm, c)],
        out_shape=[jax.ShapeDtypeStruct((t, c), F32), jax.ShapeDtypeStruct((t, c), BF16)],
        args=[z, z, wdw, bdw, lng, lnb], scratch=[pltpu.VMEM((SUBLANES, n, c), F32), pltpu.VMEM((HALO, SUBLANES, c), F32)],
        sem=("parallel",), carry=carry)


def _conv_ln_bwd(name, dsb, zc, lng, lnb, tm):
    t, c = zc.shape

    def body(dsb_ref, zc_ref, g_ref, b_ref, dzc_ref, dg_ref, db_ref, dbdw_ref):
        i = pl.program_id(0)
        xhat, rstd = _row_stats(zc_ref[...])
        gv = g_ref[...]
        zn = xhat * gv + b_ref[...]
        dzn = dsb_ref[...] * _silu_grad(zn)
        dzc = _ln_bwd_rows(dzn, xhat, rstd, gv)
        dzc_ref[...] = dzc

        @pl.when(i == 0)
        def _():
            dg_ref[...] = jnp.zeros_like(dg_ref)
            db_ref[...] = jnp.zeros_like(db_ref)
            dbdw_ref[...] = jnp.zeros_like(dbdw_ref)

        dg_ref[...] += _colsum(dzn * xhat)
        db_ref[...] += _colsum(dzn)
        dbdw_ref[...] += _colsum(dzc)

    return _call(
        body, name=name, grid=(t // tm,), in_specs=[_rows(tm, c), _rows(tm, c), _whole((1, c)), _whole((1, c))],
        out_specs=[_rows(tm, c), _whole((1, c)), _whole((1, c)), _whole((1, c))],
        out_shape=[jax.ShapeDtypeStruct((t, c), F32)] + [jax.ShapeDtypeStruct((1, c), F32)] * 3,
        args=[dsb, zc, lng, lnb], sem=("arbitrary",))


def _conv_bwd(name, dzc, z, p, wdw, kw, tm):
    t, c = z.shape
    n_i = t // tm
    n = tm + HALO

    def body(dzc_ref, dzn_ref, z_ref, h_ref, w_ref, dp_ref, dw_ref, dbin_ref, sh, dz_s, wb):
        i = pl.program_id(0)

        @pl.when(i == 0)
        def _():
            dw_ref[...] = jnp.zeros_like(dw_ref)

        sh[0, :tm, :] = dzc_ref[...]
        sh[0, tm:, :] = jnp.where(i < n_i - 1, dzn_ref[...], 0.0)
        _shifted_copies(sh, n)
        _row_broadcasts(wb, w_ref, kw)
        pair = 2 * SUBLANES
        for lc in range(c // LANES):
            cols = slice(lc * LANES, (lc + 1) * LANES)

            def rowv(rv, accs, cols=cols):
                r0 = pl.multiple_of(rv * pair, pair)
                zv = z_ref[pl.ds(r0, pair), cols]
                z0, z1 = zv[:SUBLANES], zv[SUBLANES:]
                ways = 4
                dz0, dz1, new = [None] * ways, [None] * ways, []
                for k in range(kw):
                    s = _tap(sh, r0, kw - 1 - k, pair, cols)
                    s0, s1 = s[:SUBLANES], s[SUBLANES:]
                    wk = wb[k, :, cols]
                    j = k % ways
                    dz0[j] = wk * s0 if dz0[j] is None else dz0[j] + wk * s0
                    dz1[j] = wk * s1 if dz1[j] is None else dz1[j] + wk * s1
                    new.append((accs[k] + z0 * s0) + z1 * s1)
                dz_s[pl.ds(r0, pair), cols] = jnp.concatenate([(dz0[0] + dz0[1]) + (dz0[2] + dz0[3]),
                                                               (dz1[0] + dz1[1]) + (dz1[2] + dz1[3])], axis=0)
                return tuple(new)

            accs = lax.fori_loop(0, tm // pair, rowv, tuple(jnp.zeros((SUBLANES, LANES), F32) for _ in range(kw)))
            for k in range(kw):
                dw_ref[k:k + 1, cols] += _colsum(accs[k])
        dz = dz_s[...]
        a, g = h_ref[:, :c], h_ref[:, c:]
        sg = jax.nn.sigmoid(g)
        da = dz * sg
        dg = dz * a * (sg * (1.0 - sg))
        dp_ref[:, :c] = da.astype(BF16)
        dp_ref[:, c:] = dg.astype(BF16)

        @pl.when(i == 0)
        def _():
            dbin_ref[...] = jnp.zeros_like(dbin_ref)

        dbin_ref[:, :c] += _colsum(da)
        dbin_ref[:, c:] += _colsum(dg)

    return _call(
        body, name=name, grid=(n_i,),
        in_specs=[_rows(tm, c), _halo_next(tm, c, t), _rows(tm, c), _rows(tm, 2 * c, 1), _whole(wdw.shape)],
        out_specs=[_rows(tm, 2 * c), _whole((HALO, c)), _whole((1, 2 * c))],
        out_shape=[jax.ShapeDtypeStruct((t, 2 * c), BF16), jax.ShapeDtypeStruct((HALO, c), F32),
                   jax.ShapeDtypeStruct((1, 2 * c), F32)],
        args=[dzc, dzc, z, p, wdw],
        scratch=[pltpu.VMEM((SUBLANES, n, c), F32), pltpu.VMEM((tm, c), F32), pltpu.VMEM((HALO, SUBLANES, c), F32)],
        sem=("arbitrary",))


def _adamw(name, w, g, m, v, tr, carry=None):
    r, c = w.shape
    c1 = 1.0 - ADAM_B1 ** ADAM_STEP
    c2 = 1.0 - ADAM_B2 ** ADAM_STEP

    def body(w_ref, g_ref, m_ref, v_ref, d_ref, mo_ref, vo_ref):
        gv = g_ref[...]
        mn = ADAM_B1 * m_ref[...] + (1.0 - ADAM_B1) * gv
        vn = ADAM_B2 * v_ref[...] + (1.0 - ADAM_B2) * (gv * gv)
        d_ref[...] = -ADAM_LR * ((mn / c1) / (jnp.sqrt(vn / c2) + ADAM_EPS) + ADAM_WD * w_ref[...])
        mo_ref[...] = mn
        vo_ref[...] = vn

    spec = _rows(tr, c)
    return _call(body, name=name, grid=(r // tr,), in_specs=[spec] * 4, out_specs=[spec] * 3,
                 out_shape=[jax.ShapeDtypeStruct((r, c), F32)] * 3, args=[w, g, m, v], sem=("parallel",), carry=carry)


def _pair_sum(name, mine, recv, hc, *, out_rows, tr, row_map):
    c_ = mine.shape[1]

    def body(hc_ref, a_ref, b_ref, oh_ref):
        oh_ref[...] = (a_ref[...] + b_ref[...].astype(F32)).astype(BF16)

    ispec = pl.BlockSpec((tr, c_), lambda i, hc_ref: (row_map(i, hc_ref[0]), 0))
    ospec = pl.BlockSpec((tr, c_), lambda i, hc_ref: (i, 0))
    return pl.pallas_call(
        body, name=name,
        grid_spec=pltpu.PrefetchScalarGridSpec(num_scalar_prefetch=1, grid=(out_rows // tr,), in_specs=[ispec, ispec],
                                               out_specs=ospec),
        out_shape=jax.ShapeDtypeStruct((out_rows, c_), BF16), compiler_params=_cparams(("parallel",)))(hc, mine, recv)


def _final_sum(name, mine, recv1, recv2, qc, *, part_shape, out_shape, tr, in_map, out_map):
    pr, pc = part_shape

    def body(qc_ref, a_ref, b_ref, r_ref, o_ref):
        own = a_ref[...] + b_ref[...].astype(F32)
        o_ref[...] = ((own + r_ref[0].astype(F32)) + r_ref[1].astype(F32)) + r_ref[2].astype(F32)

    ispec = pl.BlockSpec((tr, pc), lambda i, qc_ref: in_map(i, qc_ref[0], qc_ref[1]))
    return pl.pallas_call(
        body, name=name,
        grid_spec=pltpu.PrefetchScalarGridSpec(
            num_scalar_prefetch=1, grid=(pr // tr,),
            in_specs=[ispec, ispec, pl.BlockSpec((3, tr, pc), lambda i, qc_ref: (0, i, 0))],
            out_specs=pl.BlockSpec((tr, pc), lambda i, qc_ref: out_map(i, qc_ref[0], qc_ref[1]))),
        out_shape=jax.ShapeDtypeStruct(out_shape, F32), compiler_params=_cparams(("parallel",)))(qc, mine, recv1, recv2)


def _sum8(name, slots):
    def body(s_ref, o_ref):
        acc = s_ref[0]
        for d in range(1, 8):
            acc = acc + s_ref[d]
        o_ref[...] = acc

    return pl.pallas_call(body, name=name, out_shape=jax.ShapeDtypeStruct(slots.shape[1:], F32),
                          in_specs=[pl.BlockSpec(memory_space=pltpu.VMEM)], out_specs=pl.BlockSpec(memory_space=pltpu.VMEM),
                          compiler_params=pltpu.CompilerParams(vmem_limit_bytes=VMEM_LIMIT))(slots)


def _chips(x, y):
    return [(1 - x, y), (x, 1 - y), (1 - x, 1 - y)]


class _Big:
    def __init__(self, name, w, m, v, ax):
        self.name, self.w, self.m, self.v, self.ax = name, w, m, v, ax
        sr, sc = w.shape
        self.R, self.C = (sr * N_CHIPS, sc) if ax == 0 else (sr, sc * N_CHIPS)
        self.sr, self.sc = sr, sc
        self.hr = sr // 2

    def slot(self, ref, q, h=None):
        if self.ax == 1:
            cols = pl.ds(pl.multiple_of(q * self.sc, 128), self.sc)
            return ref.at[:, cols] if h is None else ref.at[pl.ds(pl.multiple_of(h * self.hr, 16), self.hr), cols]
        if h is None:
            return ref.at[pl.ds(pl.multiple_of(q * self.sr, 16), self.sr), :]
        return ref.at[pl.ds(pl.multiple_of(q * self.sr + h * self.hr, 16), self.hr), :]

    def half(self, ref, h):
        return ref.at[pl.ds(pl.multiple_of(h * self.hr, 16), self.hr), :]

    def part(self, ref, q):
        if self.ax == 1:
            return ref.at[:, pl.ds(pl.multiple_of(q * self.sc, 128), self.sc)]
        return ref.at[pl.ds(pl.multiple_of(q * self.hr, 16), self.hr), :]

    @property
    def part_shape(self):
        return (self.hr, self.sc)

    def cast_into_full(self, qc):
        tr = _tile_rows(self.sr, self.sc)
        nb = self.sr // tr

        def body(qc_ref, x_ref, o_ref):
            o_ref[...] = x_ref[...].astype(BF16)

        if self.ax == 1:
            ospec = pl.BlockSpec((tr, self.sc), lambda i, qc_ref: (i, qc_ref[0]))
        else:
            ospec = pl.BlockSpec((tr, self.sc), lambda i, qc_ref: (qc_ref[0] * nb + i, 0))
        return pl.pallas_call(
            body, name=f"cast_{self.name}",
            grid_spec=pltpu.PrefetchScalarGridSpec(num_scalar_prefetch=1, grid=(nb,),
                                                   in_specs=[pl.BlockSpec((tr, self.sc), lambda i, qc_ref: (i, 0))],
                                                   out_specs=ospec),
            out_shape=jax.ShapeDtypeStruct((self.R, self.C), BF16), compiler_params=_cparams(("parallel",)))(qc, self.w)

    def gather_ici(self, full, piece=(0, 1)):
        k, n = piece
        pr = self.hr // n
        assert pr * n == self.hr and pr % 16 == 0

        def plan(ro, rw, new, x, y, c):
            q = 2 * x + y
            r0 = c * self.hr + k * pr
            if self.ax == 1:
                mine = rw[0].at[pl.ds(pl.multiple_of(r0, 16), pr), pl.ds(pl.multiple_of(q * self.sc, 128), self.sc)]
            else:
                mine = rw[0].at[pl.ds(pl.multiple_of(q * self.sr + r0, 16), pr), :]
            return [(mine, mine, (cx, cy, c)) for cx, cy in _chips(x, y)], []

        return _Carry([], [full], [], plan, 3, 0)

    def gather_d2d(self, full):
        def plan(ro, rw, new, x, y, c):
            remote = []
            for cx, cy in _chips(x, y):
                piece = self.slot(rw[0], 2 * cx + cy, c)
                remote.append((piece, piece, (x, y, 1 - c)))
            return remote, []

        return _Carry([], [full], [], plan, 3)

    def rs_pair(self, g16):
        def plan(ro, rw, new, x, y, c):
            sib = (x, y, 1 - c)
            if self.ax == 1:
                rows = pl.ds(pl.multiple_of((1 - c) * self.hr, 16), self.hr)
                return [(ro[0].at[rows, :], new[0].at[rows, :], sib)], []
            return [(self.slot(ro[0], q, 1 - c), self.slot(new[0], q, 1 - c), sib) for q in range(N_CHIPS)], []

        return _Carry([g16], [], [jax.ShapeDtypeStruct((self.R, self.C), BF16)], plan, 1 if self.ax == 1 else N_CHIPS)

    def rs_pairsum(self, tag, g32, recv1, hc):
        tr = _tile_rows(self.hr, self.C)
        nb = self.hr // tr
        if self.ax == 1:
            row_map = lambda i, c: c * nb + i
        else:
            row_map = lambda i, c: ((i // nb) * 2 + c) * nb + i % nb
        return _pair_sum(f"rs_pairsum_{tag}", g32, recv1, hc, out_rows=self.R // 2, tr=tr, row_map=row_map)

    def rs_ici(self, cs16):
        def plan(ro, rw, new, x, y, c):
            return [(self.part(ro[0], 2 * cx + cy), new[0].at[k], (cx, cy, c)) for k, (cx, cy) in enumerate(_chips(x, y))], []

        return _Carry([cs16], [], [jax.ShapeDtypeStruct((3,) + self.part_shape, BF16)], plan, 3)

    def rs_final(self, tag, g32, recv1, recv2, qc):
        tr = _tile_rows(self.hr, self.sc)
        nb = self.hr // tr
        if self.ax == 1:
            in_map = lambda i, q, c: (c * nb + i, q)
        else:
            in_map = lambda i, q, c: ((q * 2 + c) * nb + i, 0)
        out_map = lambda i, q, c: (c * nb + i, 0)
        return _final_sum(f"rs_final_{tag}", g32, recv1, recv2, qc, part_shape=self.part_shape, out_shape=(self.sr, self.sc),
                          tr=tr, in_map=in_map, out_map=out_map)

    def rs_share(self, ghalf):
        def plan(ro, rw, new, x, y, c):
            piece = self.half(rw[0], c)
            return [(piece, piece, (x, y, 1 - c))], []

        return _Carry([], [ghalf], [], plan, 1)


def _small_allgather(packed):
    nr = packed.shape[0]

    def plan(ro, rw, new, x, y, c):
        me = 4 * x + 2 * y + c
        remote = []
        for fx in (0, 1):
            for fy in (0, 1):
                for fc in (0, 1):
                    if fx or fy or fc:
                        dev = (1 - x if fx else x, 1 - y if fy else y, 1 - c if fc else c)
                        remote.append((ro[0], new[0].at[me], dev))
        return remote, [(ro[0], new[0].at[me])]

    return _Carry([packed], [], [jax.ShapeDtypeStruct((8, nr, 128), F32)], plan, 7, 1)


def _conv_w_allgather(padded, cs):
    def plan(ro, rw, new, x, y, c):
        cols = pl.ds(pl.multiple_of((2 * x + y) * cs, 128), cs)
        remote = [(ro[0], new[0].at[:, cols], (cx, cy, c)) for cx, cy in _chips(x, y)]
        return remote, [(ro[0], new[0].at[:, cols])]

    return _Carry([padded], [], [jax.ShapeDtypeStruct((HALO, cs * N_CHIPS), F32)], plan, 3, 1)


def _pick(n, want):
    if n <= want:
        return n
    for t in range(want, 15, -16):
        if t % 16 == 0 and n % t == 0:
            return t
    raise ValueError(f"no tile for {n} (want {want})")


def _tile_rows(nrows, ncols, budget=2 * 1024 * 1024):
    return _pick(nrows, max(16, (budget // (4 * ncols)) // 16 * 16))


def _pick128(n, want):
    if n <= want:
        return n
    for t in range(want, 127, -128):
        if n % t == 0:
            return t
    raise ValueError(f"no lane tile for {n} (want {want})")


def _pack_rows(parts):
    out, spans, r0 = [], [], 0
    for p in parts:
        flat = p.reshape(-1).astype(F32)
        n = flat.shape[0]
        rows = -(-n // 1024) * 8
        flat = jnp.pad(flat, (0, rows * 128 - n))
        out.append(flat.reshape(rows, 128))
        spans.append((r0, rows, n))
        r0 += rows
    return jnp.concatenate(out, axis=0), spans


def _unpack_rows(packed, spans, shapes):
    res = []
    for (r0, rows, n), shp in zip(spans, shapes, strict=True):
        res.append(packed[r0:r0 + rows].reshape(-1)[:n].reshape(shp))
    return res


def _ident(accs, ex):
    return [accs[0]]


def kernel(x, ffn1_w_gu, ffn1_w_down, ln1_g, ln1_b, w_in, b_in, sgu_ln_g, sgu_ln_b, sgu_w_s, sgu_b_s, w_a_proj, conv_w_dw, conv_b_dw, conv_ln_g, conv_ln_b, w_b_proj, w_out, ln2_g, ln2_b, ffn2_w_gu, ffn2_w_down, ln3_g, ln3_b, loss_target, m_ffn1_w_gu, m_ffn1_w_down, m_ln1_g, m_ln1_b, m_w_in, m_b_in, m_sgu_ln_g, m_sgu_ln_b, m_sgu_w_s, m_sgu_b_s, m_w_a_proj, m_conv_w_dw, m_conv_b_dw, m_conv_ln_g, m_conv_ln_b, m_w_b_proj, m_w_out, m_ln2_g, m_ln2_b, m_ffn2_w_gu, m_ffn2_w_down, m_ln3_g, m_ln3_b, v_ffn1_w_gu, v_ffn1_w_down, v_ln1_g, v_ln1_b, v_w_in, v_b_in, v_sgu_ln_g, v_sgu_ln_b, v_sgu_w_s, v_sgu_b_s, v_w_a_proj, v_conv_w_dw, v_conv_b_dw, v_conv_ln_g, v_conv_ln_b, v_w_b_proj, v_w_out, v_ln2_g, v_ln2_b, v_ffn2_w_gu, v_ffn2_w_down, v_ln3_g, v_ln3_b):
    args = dict(locals())
    assert x.shape[0] == 1 and ffn1_w_gu.shape[0] == 1
    T, D = x.shape[1], x.shape[2]
    F = ffn1_w_down.shape[1] * N_CHIPS
    W = sgu_ln_g.shape[1]
    KW = conv_w_dw.shape[1]
    assert KW - 1 <= HALO and T % SGU_BLOCK == 0

    mx, my, mc = lax.axis_index("x"), lax.axis_index("y"), lax.axis_index("c")
    q = 2 * mx + my
    hc = jnp.reshape(mc, (1,)).astype(jnp.int32)
    qc = jnp.stack([q, mc]).astype(jnp.int32)

    big_names = [("ffn1_w_gu", 1), ("ffn1_w_down", 0), ("w_in", 1), ("w_a_proj", 1), ("w_b_proj", 1), ("w_out", 0),
                 ("ffn2_w_gu", 1), ("ffn2_w_down", 0)]
    B = {n: _Big(n, args[n][0], args["m_" + n][0], args["v_" + n][0], ax) for n, ax in big_names}
    own = {n: b.cast_into_full(qc) for n, b in B.items()}

    x2d = x[0]
    tgt = loss_target[0]
    tm_r = _pick(T, 256)
    tm = _pick(T, 1024)
    tm_h = _pick(T, 512)
    tn = _pick128(D, 1024)
    nj = D // tn
    tng = _pick128(D, 512)
    njg = D // tng
    tnf = _pick128(F, 512)
    nf = F // tnf
    tnw = _pick128(W, 1024)
    tnd = _pick128(D, 512)
    tmw = _pick128(W, 512)
    SUB = 256

    def ffn_up(tag, xb_, wgu, carry=None):
        def epi(accs, ex):
            g, u = accs
            s = jax.nn.sigmoid(g)
            sg = g * s
            return [u * (s * (1.0 + g * (1.0 - s))), sg, sg * u]

        return _mm(f"{tag}_up", [(xb_, wgu, 0, 0, 0), (xb_, wgu, 0, 0, nf)], M=T, N=F, tm=tm, tn=tnf, tk=D, nk=1, epilogue=epi,
                   outs=[(F, BF16, 0)] * 3, sum_pairs=False, carry=carry, sub=SUB)

    def ffn_down(tag, act, wd, carry=None):
        return _mm(f"{tag}_down", [(act, wd, 0, 0, 0)], M=T, N=D, tm=tm, tn=_pick128(D, 512), tk=F, nk=1, epilogue=_ident,
                   outs=[(D, F32, 0)], carry=carry)

    def ffn_dact(tag, drh, wd, dgate_f, dup_f, carry=None):
        def epi(accs, ex):
            da = accs[0]
            return [da * ex[0].astype(F32), da * ex[1].astype(F32)]

        return _mm(f"{tag}_dact", [(drh, wd, 0, 0, 0)], tb=True, M=T, N=F, tm=tm, tn=tnf, tk=D, nk=1, epilogue=epi,
                   outs=[(F, BF16, 0)] * 2, extras=[(dgate_f, "mn", 0), (dup_f, "mn", 0)], carry=carry, sub=SUB)

    def ffn_dwdown(tag, act, drh, carry=None):
        return _wgrad(f"{tag}_dwdown", act, drh, M=F, N=D, T=T, tm=tnf, tn=tnd, tk=T, carry=carry)

    def ffn_dwgate(tag, xb_, dg, carry=None):
        return _wgrad(f"{tag}_dwgate", xb_, dg, M=D, N=F, T=T, tm=tnd, tn=tnf, tk=T, ncols=2 * F, carry=carry)

    def ffn_dwup(tag, xb_, du, into, carry=None):
        return _wgrad(f"{tag}_dwup", xb_, du, M=D, N=F, T=T, tm=tnd, tn=tnf, tk=T, ncols=2 * F, into=into, joff=nf, carry=carry)

    def ffn_dx(tag, parts, wgu, addends, carry=None):
        pairs = [(da, wgu, 0, 1 if which == "up" else 0, 0) for which, da in parts]

        def epi(accs, ex):
            tot = accs[0] + ALPHA * ex[0]
            for e in ex[1:]:
                tot = tot + e
            return [tot]

        return _mm(f"{tag}_dx_{'_'.join(w_ for w_, _ in parts)}", pairs, tb=True, M=T, N=D, tm=tm_h,
                   tn=_pick128(D, 512 // len(parts)), tk=F, nk=1, epilogue=epi if addends else _ident, outs=[(D, F32, 0)],
                   extras=[(a, "mn", 0) for a in addends], carry=carry)

    wdw_pad = jnp.pad(conv_w_dw[0], ((0, HALO - KW), (0, 0)))
    c0, un = _merge(B["ffn1_w_gu"].gather_ici(own["ffn1_w_gu"]), _conv_w_allgather(wdw_pad, conv_w_dw.shape[2]))
    (wgu1,), (wdw_full,) = un(_exchange("gather_first", c0))
    (xb,), (wgu1,) = _cast_bf16("cast_x", x2d, tm_r, carry=B["ffn1_w_gu"].gather_d2d(wgu1))

    c, un = _merge(B["ffn1_w_down"].gather_ici(own["ffn1_w_down"]), B["w_in"].gather_ici(own["w_in"]))
    (g1, u1, a1), co = ffn_up("ffn1", xb, wgu1, carry=c)
    (wd1,), (win,) = un(co)
    (wd1,) = _exchange("gather_d2d_ffn1_w_down", B["ffn1_w_down"].gather_d2d(wd1))
    b_gu2, b_d2 = B["ffn2_w_gu"], B["ffn2_w_down"]
    c, un = _merge(B["w_a_proj"].gather_ici(own["w_a_proj"]), B["w_b_proj"].gather_ici(own["w_b_proj"]),
                   B["w_out"].gather_ici(own["w_out"]), B["w_in"].gather_d2d(win), b_gu2.gather_ici(own["ffn2_w_gu"], (0, 4)))
    (fo1,), co = ffn_down("ffn1", a1, wd1, carry=c)
    (wa,), (wb,), (wout,), (win,), (wgu2,) = un(co)
    c, un = _merge(B["w_a_proj"].gather_d2d(wa), B["w_b_proj"].gather_d2d(wb), B["w_out"].gather_d2d(wout),
                   b_gu2.gather_ici(wgu2, (1, 4)))
    (x1, x1b, xh1, rs1), co = _ln_fwd("ln1", x2d, fo1, ln1_g, ln1_b, 0.5, tm_r, carry=c)
    (wa,), (wb,), (wout,), (wgu2,) = un(co)

    c, un = _merge(b_gu2.gather_ici(wgu2, (1, 2)), b_d2.gather_ici(own["ffn2_w_down"], (0, 2)))
    (proj,), co = _mm("in_proj", [(x1b, win, 0, 0, 0)], M=T, N=4 * D, tm=tm, tn=tn, tk=D, nk=1,
                      epilogue=lambda accs, ex: [accs[0] + ex[0]], outs=[(4 * D, F32, 0)], extras=[(b_in, "n", 0)], carry=c)
    (wgu2,), (wd2,) = un(co)
    wm = sgu_w_s[0]
    bst = sgu_b_s[0].T
    sa, z = _sgu_fwd("sgu_fwd", proj, sgu_ln_g, sgu_ln_b, wm, bst, tm_r)
    (zc, sb), (wd2,) = _conv_fwd("conv_fwd", z, wdw_full, KW, conv_b_dw, conv_ln_g, conv_ln_b, tm_r,
                                 carry=b_d2.gather_ici(wd2, (1, 2)))

    def epi_mix(accs, ex):
        ya_, yb_ = accs
        return [jax.nn.sigmoid(ex[0]) * ya_ + jax.nn.sigmoid(ex[1]) * yb_, ya_, yb_]

    (mixin, ya, yb), (wgu2,) = _mm("branch_proj", [(sa, wa, 0, 0, 0), (sb, wb, 0, 0, 0)], M=T, N=D, tm=tm, tn=tng, tk=W, nk=1,
                                   epilogue=epi_mix, outs=[(D, BF16, 0)] * 3, sum_pairs=False, sub=SUB,
                                   extras=[(proj, "mn", 2 * njg), (proj, "mn", 3 * njg)], carry=B["ffn2_w_gu"].gather_d2d(wgu2))
    (mix,), (wd2,) = _mm("out_proj", [(mixin, wout, 0, 0, 0)], M=T, N=D, tm=tm, tn=tn, tk=D, nk=1, epilogue=_ident,
                         outs=[(D, F32, 0)], carry=B["ffn2_w_down"].gather_d2d(wd2))
    x2, x2b, xh2, rs2 = _ln_fwd("ln2", x1, mix, ln2_g, ln2_b, 1.0, tm_r)
    g2, u2, a2 = ffn_up("ffn2", x2b, wgu2)
    (fo2,) = ffn_down("ffn2", a2, wd2)
    dr3, dr3h, loss_part, dln3_g, dln3_b = _ln3_loss("ln3_loss", x2, fo2, ln3_g, ln3_b, tgt, tm_r)

    b_gu2, b_d2 = B["ffn2_w_gu"], B["ffn2_w_down"]
    dg2, du2 = ffn_dact("ffn2", dr3h, wd2, g2, u2)
    dwd2 = ffn_dwdown("ffn2", a2, dr3h)
    dwgu2, (r1_d2,) = ffn_dwgate("ffn2", x2b, dg2, carry=b_d2.rs_pair(dwd2[1]))
    dwgu2 = ffn_dwup("ffn2", x2b, du2, dwgu2)
    cs_d2 = b_d2.rs_pairsum("ffn2_w_down", dwd2[0], r1_d2, hc)
    c, un = _merge(b_gu2.rs_pair(dwgu2[1]), b_d2.rs_ici(cs_d2))
    (dx2,), co = ffn_dx("ffn2", [("gate", dg2), ("up", du2)], wgu2, [dr3], carry=c)
    (r1_gu2,), (r2_d2,) = un(co)
    cs_gu2 = b_gu2.rs_pairsum("ffn2_w_gu", dwgu2[0], r1_gu2, hc)
    gh_d2 = b_d2.rs_final("ffn2_w_down", dwd2[0], r1_d2, r2_d2, qc)

    (dr2, dr2b, dln2_g, dln2_b), (g_d2,) = _ln_bwd("ln2_bwd", dx2, xh2, rs2, ln2_g, 1.0, tm_r, carry=b_d2.rs_share(gh_d2))

    def epi_dmix(accs, ex):
        dm = accs[0]
        ga, gb = jax.nn.sigmoid(ex[0]), jax.nn.sigmoid(ex[1])
        ya_, yb_ = ex[2].astype(F32), ex[3].astype(F32)
        return [dm * ga, dm * gb, dm * ya_ * (ga * (1.0 - ga)), dm * yb_ * (gb * (1.0 - gb))]

    dya, dyb, dla, dlb = _mm("out_proj_bwd", [(dr2b, wout, 0, 0, 0)], tb=True, M=T, N=D, tm=tm, tn=tng, tk=D, nk=1,
                             epilogue=epi_dmix, outs=[(D, BF16, 0)] * 4, sub=SUB,
                             extras=[(proj, "mn", 2 * njg), (proj, "mn", 3 * njg), (ya, "mn", 0), (yb, "mn", 0)])
    dwout = _wgrad("dw_out", mixin, dr2b, M=D, N=D, T=T, tm=tnd, tn=tnd, tk=T)
    (dsa,) = _mm("a_proj_bwd", [(dya, wa, 0, 0, 0)], tb=True, M=T, N=W, tm=tm, tn=tnw, tk=D, nk=1, epilogue=_ident,
                 outs=[(W, F32, 0)])
    (dsb,) = _mm("b_proj_bwd", [(dyb, wb, 0, 0, 0)], tb=True, M=T, N=W, tm=tm, tn=tnw, tk=D, nk=1, epilogue=_ident,
                 outs=[(W, F32, 0)])
    dwa = _wgrad("dw_a_proj", sa, dya, M=W, N=D, T=T, tm=tmw, tn=tnd, tk=T)
    dwb = _wgrad("dw_b_proj", sb, dyb, M=W, N=D, T=T, tm=tmw, tn=tnd, tk=T)

    dpa, dwm, dbst, dsgu_g, dsgu_b, dbin_a = _sgu_bwd("sgu_bwd", proj, dsa, sgu_ln_g, sgu_ln_b, wm, bst, tm_r)
    dzc, dcln_g, dcln_b, dbdw = _conv_ln_bwd("conv_ln_bwd", dsb, zc, conv_ln_g, conv_ln_b, tm_r)
    dpb, dwdw, dbin_b = _conv_bwd("conv_bwd", dzc, z, proj, wdw_full, KW, tm_r)

    dps = [dpa, dpb, dla, dlb]
    db_in = jnp.concatenate([dbin_a, dbin_b, _colsum_rows("db_in_gate_a", dla, tm_r), _colsum_rows("db_in_gate_b", dlb, tm_r)],
                            axis=1)
    (dx1,), (r2_gu2,) = _mm("in_proj_bwd", [(dp, win, 0, k, 0) for k, dp in enumerate(dps)], tb=True, M=T, N=D, tm=tm_h,
                            tn=tnd, tk=D, nk=1, epilogue=lambda accs, ex: [accs[0] + ALPHA * ex[0]], outs=[(D, F32, 0)],
                            extras=[(dr2, "mn", 0)], carry=b_gu2.rs_ici(cs_gu2))
    gh_gu2 = b_gu2.rs_final("ffn2_w_gu", dwgu2[0], r1_gu2, r2_gu2, qc)
    dwin, (g_gu2,) = _wgrad("dw_in_0", x1b, dps[0], M=D, N=D, T=T, tm=tnd, tn=tnd, tk=T, ncols=4 * D,
                            carry=b_gu2.rs_share(gh_gu2))
    for k in range(1, 4):
        dwin = _wgrad(f"dw_in_{k}", x1b, dps[k], M=D, N=D, T=T, tm=tnd, tn=tnd, tk=T, ncols=4 * D, into=dwin,
                      joff=k * (D // tnd))

    mix_names = ["w_in", "w_a_proj", "w_b_proj", "w_out"]
    mix_grads = dict(zip(mix_names, [dwin, dwa, dwb, dwout], strict=True))
    c, un = _merge(*[B[n].rs_pair(mix_grads[n][1]) for n in mix_names])
    (dr1, dr1h, dln1_g, dln1_b), co = _ln_bwd("ln1_bwd", dx1, xh1, rs1, ln1_g, 0.5, tm_r, carry=c)
    r1_mix = {n: r1 for n, (r1,) in zip(mix_names, un(co), strict=True)}
    cs_mix = {n: B[n].rs_pairsum(n, mix_grads[n][0], r1_mix[n], hc) for n in mix_names}

    small_names = ["ln1_g", "ln1_b", "b_in", "sgu_ln_g", "sgu_ln_b", "sgu_w_s", "sgu_b_s", "conv_w_dw", "conv_b_dw",
                   "conv_ln_g", "conv_ln_b", "ln2_g", "ln2_b", "ln3_g", "ln3_b"]
    small_parts = {"ln1_g": dln1_g, "ln1_b": dln1_b, "b_in": db_in, "sgu_ln_g": dsgu_g, "sgu_ln_b": dsgu_b, "sgu_w_s": dwm,
                   "sgu_b_s": dbst.T, "conv_w_dw": dwdw[:KW], "conv_b_dw": dbdw, "conv_ln_g": dcln_g, "conv_ln_b": dcln_b,
                   "ln2_g": dln2_g, "ln2_b": dln2_b, "ln3_g": dln3_g, "ln3_b": dln3_b}
    packed, spans = _pack_rows([small_parts[n] for n in small_names])

    b_gu1, b_d1 = B["ffn1_w_gu"], B["ffn1_w_down"]
    c, un = _merge(*[B[n].rs_ici(cs_mix[n]) for n in mix_names], _small_allgather(packed))
    (dg1, du1), co = ffn_dact("ffn1", dr1h, wd1, g1, u1, carry=c)
    *r2_mix, (small_slots,) = un(co)
    gh_mix = [B[n].rs_final(n, mix_grads[n][0], r1_mix[n], r2, qc) for n, (r2,) in zip(mix_names, r2_mix, strict=True)]
    c, un = _merge(*[B[n].rs_share(gh) for n, gh in zip(mix_names, gh_mix, strict=True)])
    dwgu1, co = ffn_dwgate("ffn1", xb, dg1, carry=c)
    g_mix = {n: g for n, (g,) in zip(mix_names, un(co), strict=True)}
    dwgu1 = ffn_dwup("ffn1", xb, du1, dwgu1)
    dwd1, (r1_gu1,) = ffn_dwdown("ffn1", a1, dr1h, carry=b_gu1.rs_pair(dwgu1[1]))
    cs_gu1 = b_gu1.rs_pairsum("ffn1_w_gu", dwgu1[0], r1_gu1, hc)
    c, un = _merge(b_gu1.rs_ici(cs_gu1), b_d1.rs_pair(dwd1[1]))
    (dx_gate,), co = ffn_dx("ffn1", [("gate", dg1)], wgu1, [], carry=c)
    (r2_gu1,), (r1_d1,) = un(co)
    cs_d1 = b_d1.rs_pairsum("ffn1_w_down", dwd1[0], r1_d1, hc)
    gh_gu1 = b_gu1.rs_final("ffn1_w_gu", dwgu1[0], r1_gu1, r2_gu1, qc)
    c, un = _merge(b_d1.rs_ici(cs_d1), b_gu1.rs_share(gh_gu1))
    (dx,), co = ffn_dx("ffn1", [("up", du1)], wgu1, [dr1, dx_gate], carry=c)
    (r2_d1,), (g_gu1,) = un(co)
    gh_d1 = b_d1.rs_final("ffn1_w_down", dwd1[0], r1_d1, r2_d1, qc)

    grads = {"ffn1_w_gu": g_gu1, "ffn2_w_gu": g_gu2, "ffn2_w_down": g_d2, **g_mix}
    outs_g, outs_d, outs_m, outs_v = {}, {}, {}, {}

    def adamw_big(n, g, carry=None):
        b = B[n]
        return _adamw(f"adamw_{n}", b.w, g, b.m, b.v, _tile_rows(b.sr, b.sc, 1024 * 1024), carry=carry)

    upd = {}
    upd["w_a_proj"], (grads["ffn1_w_down"],) = adamw_big("w_a_proj", grads["w_a_proj"], carry=b_d1.rs_share(gh_d1))
    for n, _ in big_names:
        if n not in upd:
            upd[n] = adamw_big(n, grads[n])
        d_, m_, v_ = upd[n]
        outs_g[n], outs_d[n], outs_m[n], outs_v[n] = grads[n][None], d_[None], m_[None], v_[None]
    gsum = _sum8("small_sum", small_slots)
    full_shapes = [args[n].shape if n != "conv_w_dw" else (1, KW, W) for n in small_names]
    gsmall = dict(zip(small_names, _unpack_rows(gsum, spans, full_shapes), strict=True))
    cs = conv_w_dw.shape[2]
    gsmall["conv_w_dw"] = lax.dynamic_slice_in_dim(gsmall["conv_w_dw"], q * cs, cs, axis=2)
    pw, spans2 = _pack_rows([args[n] for n in small_names])
    pg, _ = _pack_rows([gsmall[n] for n in small_names])
    pm, _ = _pack_rows([args["m_" + n] for n in small_names])
    pv, _ = _pack_rows([args["v_" + n] for n in small_names])
    pd, pmn, pvn = _adamw("adamw_small", pw, pg, pm, pv, pw.shape[0])
    shapes2 = [args[n].shape for n in small_names]
    for dst, src in ((outs_d, pd), (outs_m, pmn), (outs_v, pvn)):
        dst.update(zip(small_names, _unpack_rows(src, spans2, shapes2), strict=True))
    outs_g.update(gsmall)

    loss = lax.psum(loss_part[0, 0], ("x", "y", "c"))
    order = ["ffn1_w_gu", "ffn1_w_down", "ln1_g", "ln1_b", "w_in", "b_in", "sgu_ln_g", "sgu_ln_b", "sgu_w_s", "sgu_b_s",
             "w_a_proj", "conv_w_dw", "conv_b_dw", "conv_ln_g", "conv_ln_b", "w_b_proj", "w_out", "ln2_g", "ln2_b",
             "ffn2_w_gu", "ffn2_w_down", "ln3_g", "ln3_b"]
    return (loss, dx[None], *[outs_g[n] for n in order], *[outs_d[n] for n in order], *[outs_m[n] for n in order],
            *[outs_v[n] for n in order])
```

```python
import math

import jax
import jax.numpy as jnp
from jax import lax
from jax.experimental import pallas as pl
from jax.experimental.pallas import tpu as pltpu

BF16 = jnp.bfloat16
F32 = jnp.float32

LN_EPS = 1e-5
ALPHA = 2.0 ** 0.25
SGU_BLOCK = 128
SGU_CHUNK = 64
HALO = 32
SUBLANES = 8
LANES = 128
CONV_ROWS = 32
ADAM_LR = 0.001
ADAM_B1 = 0.9
ADAM_B2 = 0.999
ADAM_EPS = 1e-08
ADAM_WD = 0.01
ADAM_STEP = 10
N_CHIPS = 4
VMEM_LIMIT = 52 * 1024 * 1024
MESH = pl.DeviceIdType.MESH

_GELU_C0 = math.sqrt(2.0 / math.pi)
_GELU_C1 = 0.044715


def _cparams(sem):
    return pltpu.CompilerParams(dimension_semantics=sem, vmem_limit_bytes=VMEM_LIMIT)


def _gelu_parts(x):
    x2 = x * x
    t = jnp.tanh(_GELU_C0 * (x + _GELU_C1 * (x2 * x)))
    return 0.5 * (1.0 + t), t, x2


def _gelu(x):
    cdf, _, _ = _gelu_parts(x)
    return x * cdf


def _gelu_and_grad(x):
    cdf, t, x2 = _gelu_parts(x)
    grad = cdf + x * (0.5 * (1.0 - t * t)) * (_GELU_C0 * (1.0 + (3.0 * _GELU_C1) * x2))
    return x * cdf, grad


def _silu_grad(x):
    s = jax.nn.sigmoid(x)
    return s * (1.0 + x * (1.0 - s))


def _row_stats(x):
    mu = jnp.mean(x, axis=-1, keepdims=True)
    xc = x - mu
    var = jnp.mean(xc * xc, axis=-1, keepdims=True)
    rstd = lax.rsqrt(var + LN_EPS)
    return xc * rstd, rstd


def _ln_bwd_rows(dy, xhat, rstd, g):
    dxh = dy * g
    m1 = jnp.mean(dxh, axis=-1, keepdims=True)
    m2 = jnp.mean(dxh * xhat, axis=-1, keepdims=True)
    return rstd * (dxh - m1 - xhat * m2)


def _colsum(v):
    return jnp.sum(v, axis=0, keepdims=True)


class _Carry:
    def __init__(self, ro, rw, new, plan, n_remote, n_local=0):
        self.ro, self.rw, self.new, self.plan = list(ro), list(rw), list(new), plan
        self.n_remote, self.n_local = n_remote, n_local

    def sems(self):
        return [pltpu.SemaphoreType.DMA((self.n_remote,)), pltpu.SemaphoreType.DMA((self.n_remote,)),
                pltpu.SemaphoreType.DMA((max(self.n_local, 1),))]

    def copies(self, ro_refs, rw_refs, new_refs, send_sems, recv_sems, loc_sems):
        x, y, c = lax.axis_index("x"), lax.axis_index("y"), lax.axis_index("c")
        remote, local = self.plan(ro_refs, rw_refs, new_refs, x, y, c)
        assert len(remote) == self.n_remote and len(local) == self.n_local
        lcs = [pltpu.make_async_copy(s, d, loc_sems.at[k]) for k, (s, d) in enumerate(local)]
        rcs = [pltpu.make_async_remote_copy(src_ref=s, dst_ref=d, send_sem=send_sems.at[k], recv_sem=recv_sems.at[k],
                                            device_id=dev, device_id_type=MESH) for k, (s, d, dev) in enumerate(remote)]
        return lcs, rcs

    def out_shape(self):
        return [jax.ShapeDtypeStruct(a.shape, a.dtype) for a in self.rw] + self.new


def _start_all(lcs, rcs):
    for cp in lcs + rcs:
        cp.start()


def _wait_all(lcs, rcs):
    for cp in rcs:
        cp.wait_send()
    for cp in rcs:
        cp.wait_recv()
    for cp in lcs:
        cp.wait()


def _merge(*cs):
    ro = [a for c in cs for a in c.ro]
    rw = [a for c in cs for a in c.rw]
    new = [a for c in cs for a in c.new]

    def plan(ro_refs, rw_refs, new_refs, x, y, c):
        remote, local, a, b, d = [], [], 0, 0, 0
        for cc in cs:
            r, l = cc.plan(ro_refs[a:a + len(cc.ro)], rw_refs[b:b + len(cc.rw)], new_refs[d:d + len(cc.new)], x, y, c)
            a, b, d = a + len(cc.ro), b + len(cc.rw), d + len(cc.new)
            remote += r
            local += l
        return remote, local

    def unpack(couts):
        res, b, d = [], 0, len(rw)
        for cc in cs:
            res.append(list(couts[b:b + len(cc.rw)]) + list(couts[d:d + len(cc.new)]))
            b, d = b + len(cc.rw), d + len(cc.new)
        return res

    return _Carry(ro, rw, new, plan, sum(c.n_remote for c in cs), sum(c.n_local for c in cs)), unpack


_ANY = pl.BlockSpec(memory_space=pl.ANY)


def _exchange(name, carry):
    n_ro, n_rw, n_new = len(carry.ro), len(carry.rw), len(carry.new)

    def body(*refs):
        o0 = n_ro + n_rw
        lcs, rcs = carry.copies(refs[:n_ro], refs[o0:o0 + n_rw], refs[o0 + n_rw:o0 + n_rw + n_new], *refs[o0 + n_rw + n_new:])
        _start_all(lcs, rcs)
        _wait_all(lcs, rcs)

    return list(pl.pallas_call(
        body, name=name, in_specs=[_ANY] * (n_ro + n_rw), out_specs=[_ANY] * (n_rw + n_new), out_shape=carry.out_shape(),
        input_output_aliases={n_ro + k: k for k in range(n_rw)}, scratch_shapes=carry.sems())(*carry.ro, *carry.rw))


def _call(body, *, name, grid, in_specs, out_specs, out_shape, args, scratch=(), sem, carry=None, aliases=None):
    in_specs, out_specs, out_shape, scratch = list(in_specs), list(out_specs), list(out_shape), list(scratch)
    if carry is None:
        return list(pl.pallas_call(body, name=name, grid=grid, in_specs=in_specs, out_specs=out_specs, out_shape=out_shape,
                                   scratch_shapes=scratch, input_output_aliases=aliases or {},
                                   compiler_params=_cparams(sem))(*args))
    n_in, n_out, n_scr = len(in_specs), len(out_specs), len(scratch)
    n_ro, n_rw, n_new = len(carry.ro), len(carry.rw), len(carry.new)

    def wrapped(*refs):
        ins = refs[:n_in]
        ro_refs = refs[n_in:n_in + n_ro]
        o0 = n_in + n_ro + n_rw
        outs = refs[o0:o0 + n_out]
        rw_refs = refs[o0 + n_out:o0 + n_out + n_rw]
        new_refs = refs[o0 + n_out + n_rw:o0 + n_out + n_rw + n_new]
        s0 = o0 + n_out + n_rw + n_new
        scr = refs[s0:s0 + n_scr]
        sems = refs[s0 + n_scr:]
        first = pl.program_id(0) == 0
        last = pl.program_id(0) == grid[0] - 1
        for d in range(1, len(grid)):
            first = jnp.logical_and(first, pl.program_id(d) == 0)
            last = jnp.logical_and(last, pl.program_id(d) == grid[d] - 1)

        @pl.when(first)
        def _():
            _start_all(*carry.copies(ro_refs, rw_refs, new_refs, *sems))

        body(*ins, *outs, *scr)

        @pl.when(last)
        def _():
            _wait_all(*carry.copies(ro_refs, rw_refs, new_refs, *sems))

    al = dict(aliases or {})
    al.update({n_in + n_ro + k: n_out + k for k in range(n_rw)})
    res = pl.pallas_call(
        wrapped, name=name, grid=grid, in_specs=in_specs + [_ANY] * (n_ro + n_rw), out_specs=out_specs + [_ANY] * (n_rw + n_new),
        out_shape=out_shape + carry.out_shape(), scratch_shapes=scratch + carry.sems(), input_output_aliases=al,
        compiler_params=_cparams(("arbitrary",) * len(grid)))(*args, *carry.ro, *carry.rw)
    return list(res[:n_out]), list(res[n_out:])


def _mm(name, pairs, *, ta=False, tb=False, M, N, tm, tn, tk, nk, epilogue, outs, extras=(), sum_pairs=True, carry=None,
        sub=None):
    ni, nj = M // tm, N // tn
    assert ni * tm == M and nj * tn == N
    n_p = len(pairs)
    n_acc = 1 if sum_pairs else n_p
    in_specs, args = [], []
    for a, b, ak, bk, bj in pairs:
        if ta:
            in_specs.append(pl.BlockSpec((tk, tm), lambda i, j, k, ak=ak: (k + ak, i)))
        else:
            in_specs.append(pl.BlockSpec((tm, tk), lambda i, j, k, ak=ak: (i, k + ak)))
        if tb:
            in_specs.append(pl.BlockSpec((tn, tk), lambda i, j, k, bk=bk, bj=bj: (j + bj, k + bk)))
        else:
            in_specs.append(pl.BlockSpec((tk, tn), lambda i, j, k, bk=bk, bj=bj: (k + bk, j + bj)))
        args += [a, b]
    for arr, kind, jo in extras:
        if kind == "mn":
            in_specs.append(pl.BlockSpec((tm, tn), lambda i, j, k, jo=jo: (i, j + jo)))
        else:
            in_specs.append(pl.BlockSpec((1, tn), lambda i, j, k, jo=jo: (0, j + jo)))
        args.append(arr)
    out_specs = [pl.BlockSpec((tm, tn), lambda i, j, k, jo=jo: (i, j + jo)) for _, _, jo in outs]
    out_shape = [jax.ShapeDtypeStruct((M, nc), dt) for nc, dt, _ in outs]
    n_ex, n_out = len(extras), len(outs)
    dn = (((0 if ta else 1,), (1 if tb else 0,)), ((), ()))

    def body(*refs):
        ab = refs[: 2 * n_p]
        ex = refs[2 * n_p: 2 * n_p + n_ex]
        o0 = 2 * n_p + n_ex
        out_refs = refs[o0: o0 + n_out]
        acc_refs = refs[o0 + n_out:]

        def dots():
            res = []
            for p in range(n_p):
                a = ab[2 * p][...]
                b = ab[2 * p + 1][...]
                res.append(lax.dot_general(a.astype(BF16), b.astype(BF16), dn, preferred_element_type=F32))
            if sum_pairs:
                tot = res[0]
                for r in res[1:]:
                    tot = tot + r
                res = [tot]
            return res

        def finish(accs):
            tiles = epilogue(accs, [e[...] for e in ex])
            for r, t in zip(out_refs, tiles, strict=True):
                r[...] = t.astype(r.dtype)

        if nk == 1 and sub is not None and tn > sub:
            for s in range(tn // sub):
                cs = slice(s * sub, (s + 1) * sub)
                res = []
                for p in range(n_p):
                    b = ab[2 * p + 1][cs, :] if tb else ab[2 * p + 1][:, cs]
                    res.append(lax.dot_general(ab[2 * p][...].astype(BF16), b.astype(BF16), dn, preferred_element_type=F32))
                if sum_pairs:
                    tot = res[0]
                    for r in res[1:]:
                        tot = tot + r
                    res = [tot]
                tiles = epilogue(res, [e[:, cs] for e in ex])
                for r, t in zip(out_refs, tiles, strict=True):
                    r[:, cs] = t.astype(r.dtype)
        elif nk == 1:
            finish(dots())
        else:
            k = pl.program_id(2)
            d = dots()

            @pl.when(k == 0)
            def _():
                for r, v in zip(acc_refs, d, strict=True):
                    r[...] = v

            @pl.when(k > 0)
            def _():
                for r, v in zip(acc_refs, d, strict=True):
                    r[...] += v

            @pl.when(k == nk - 1)
            def _():
                finish([r[...] for r in acc_refs])

    scratch = [pltpu.VMEM((tm, tn), F32) for _ in range(n_acc)] if nk > 1 else []
    return _call(body, name=name, grid=(ni, nj, nk), in_specs=in_specs, out_specs=out_specs, out_shape=out_shape, args=args,
                 scratch=scratch, sem=("parallel", "parallel", "arbitrary"), carry=carry)


def _wgrad(name, a, b, *, M, N, T, tm, tn, tk, into=None, joff=0, ncols=None, carry=None):
    ncols = N if ncols is None else ncols
    ni, nj, nk = M // tm, N // tn, T // tk
    dn = (((0,), (0,)), ((), ()))

    def body(a_ref, b_ref, *rest):
        d = lax.dot_general(a_ref[...].astype(BF16), b_ref[...].astype(BF16), dn, preferred_element_type=F32)
        if nk == 1:
            of_ref, oh_ref = rest[-2:]
            of_ref[...] = d
            oh_ref[...] = d.astype(BF16)
            return
        of_ref, oh_ref, acc_ref = rest[-3:]
        k = pl.program_id(2)

        @pl.when(k == 0)
        def _():
            acc_ref[...] = d

        @pl.when(k > 0)
        def _():
            acc_ref[...] += d

        @pl.when(k == nk - 1)
        def _():
            of_ref[...] = acc_ref[...]
            oh_ref[...] = acc_ref[...].astype(BF16)

    ospec = pl.BlockSpec((tm, tn), lambda i, j, k: (i, j + joff))
    in_specs = [pl.BlockSpec((tk, tm), lambda i, j, k: (k, i)), pl.BlockSpec((tk, tn), lambda i, j, k: (k, j))]
    args, aliases = [a, b], None
    if into is not None:
        in_specs += [_ANY, _ANY]
        args += list(into)
        aliases = {2: 0, 3: 1}
    return _call(body, name=name, grid=(ni, nj, nk), in_specs=in_specs, out_specs=[ospec, ospec],
                 out_shape=[jax.ShapeDtypeStruct((M, ncols), F32), jax.ShapeDtypeStruct((M, ncols), BF16)], args=args,
                 scratch=[pltpu.VMEM((tm, tn), F32)] if nk > 1 else [], sem=("parallel", "parallel", "arbitrary"), carry=carry,
                 aliases=aliases)


def _rows(tm, c, cb=0):
    return pl.BlockSpec((tm, c), lambda i, cb=cb: (i, cb))


def _whole(shape):
    nd = len(shape)
    return pl.BlockSpec(shape, lambda i, nd=nd: (0,) * nd)


def _cast_bf16(name, x, tm, carry=None):
    t, d = x.shape

    def body(x_ref, o_ref):
        o_ref[...] = x_ref[...].astype(BF16)

    return _call(body, name=name, grid=(t // tm,), in_specs=[_rows(tm, d)], out_specs=[_rows(tm, d)],
                 out_shape=[jax.ShapeDtypeStruct((t, d), BF16)], args=[x], sem=("parallel",), carry=carry)


def _ln_fwd(name, xres, f, g, b, cf, tm, carry=None):
    t, d = xres.shape

    def body(x_ref, f_ref, g_ref, b_ref, y_ref, yb_ref, xh_ref, rs_ref):
        r = ALPHA * x_ref[...] + cf * f_ref[...]
        xhat, rstd = _row_stats(r)
        y = xhat * g_ref[...] + b_ref[...]
        y_ref[...] = y
        yb_ref[...] = y.astype(BF16)
        xh_ref[...] = xhat
        rs_ref[...] = rstd

    return _call(
        body, name=name, grid=(t // tm,), in_specs=[_rows(tm, d), _rows(tm, d), _whole((1, d)), _whole((1, d))],
        out_specs=[_rows(tm, d), _rows(tm, d), _rows(tm, d), _rows(tm, 1)],
        out_shape=[jax.ShapeDtypeStruct((t, d), F32), jax.ShapeDtypeStruct((t, d), BF16),
                   jax.ShapeDtypeStruct((t, d), F32), jax.ShapeDtypeStruct((t, 1), F32)],
        args=[xres, f, g, b], sem=("parallel",), carry=carry)


def _ln_bwd(name, dy, xhat, rstd, g, scale, tm, carry=None):
    t, d = dy.shape

    def body(dy_ref, xh_ref, rs_ref, g_ref, dr_ref, drb_ref, dg_ref, db_ref):
        i = pl.program_id(0)
        dy_v, xh = dy_ref[...], xh_ref[...]
        dr = _ln_bwd_rows(dy_v, xh, rs_ref[...], g_ref[...])
        dr_ref[...] = dr
        drb_ref[...] = (scale * dr).astype(BF16)

        @pl.when(i == 0)
        def _():
            dg_ref[...] = jnp.zeros_like(dg_ref)
            db_ref[...] = jnp.zeros_like(db_ref)

        dg_ref[...] += _colsum(dy_v * xh)
        db_ref[...] += _colsum(dy_v)

    return _call(
        body, name=name, grid=(t // tm,), in_specs=[_rows(tm, d), _rows(tm, d), _rows(tm, 1), _whole((1, d))],
        out_specs=[_rows(tm, d), _rows(tm, d), _whole((1, d)), _whole((1, d))],
        out_shape=[jax.ShapeDtypeStruct((t, d), F32), jax.ShapeDtypeStruct((t, d), BF16),
                   jax.ShapeDtypeStruct((1, d), F32), jax.ShapeDtypeStruct((1, d), F32)],
        args=[dy, xhat, rstd, g], sem=("arbitrary",), carry=carry)


def _ln3_loss(name, xres, f, g, b, target, tm):
    t, d = xres.shape

    def body(x_ref, f_ref, g_ref, b_ref, tg_ref, dr_ref, drb_ref, loss_ref, dg_ref, db_ref):
        i = pl.program_id(0)
        r = ALPHA * x_ref[...] + 0.5 * f_ref[...]
        xhat, rstd = _row_stats(r)
        gv = g_ref[...]
        y = xhat * gv + b_ref[...]
        err = y - tg_ref[...]
        dy = err * (1.0 / d)
        dr = _ln_bwd_rows(dy, xhat, rstd, gv)
        dr_ref[...] = dr
        drb_ref[...] = (0.5 * dr).astype(BF16)
        part = 0.5 * jnp.sum(jnp.mean(err * err, axis=-1, keepdims=True), axis=0, keepdims=True)

        @pl.when(i == 0)
        def _():
            loss_ref[...] = jnp.zeros_like(loss_ref)
            dg_ref[...] = jnp.zeros_like(dg_ref)
            db_ref[...] = jnp.zeros_like(db_ref)

        loss_ref[...] += jnp.broadcast_to(part, loss_ref.shape)
        dg_ref[...] += _colsum(dy * xhat)
        db_ref[...] += _colsum(dy)

    return _call(
        body, name=name, grid=(t // tm,),
        in_specs=[_rows(tm, d), _rows(tm, d), _whole((1, d)), _whole((1, d)), _rows(tm, d)],
        out_specs=[_rows(tm, d), _rows(tm, d), _whole((8, 128)), _whole((1, d)), _whole((1, d))],
        out_shape=[jax.ShapeDtypeStruct((t, d), F32), jax.ShapeDtypeStruct((t, d), BF16),
                   jax.ShapeDtypeStruct((8, 128), F32), jax.ShapeDtypeStruct((1, d), F32),
                   jax.ShapeDtypeStruct((1, d), F32)],
        args=[xres, f, g, b, target], sem=("arbitrary",))


def _colsum_rows(name, x, tm):
    t, d = x.shape

    def body(x_ref, o_ref):
        @pl.when(pl.program_id(0) == 0)
        def _():
            o_ref[...] = jnp.zeros_like(o_ref)

        o_ref[...] += _colsum(x_ref[...].astype(F32))

    return _call(body, name=name, grid=(t // tm,), in_specs=[_rows(tm, d)], out_specs=[_whole((1, d))],
                 out_shape=[jax.ShapeDtypeStruct((1, d), F32)], args=[x], sem=("arbitrary",))[0]


def _sgu_mask():
    sh = SGU_CHUNK.bit_length() - 1
    r = lax.shift_right_logical(lax.broadcasted_iota(jnp.int32, (SGU_BLOCK, SGU_BLOCK), 0), sh)
    c = lax.shift_right_logical(lax.broadcasted_iota(jnp.int32, (SGU_BLOCK, SGU_BLOCK), 1), sh)
    return c <= r


def _sgu_fwd(name, p, lng, lnb, wm, bst, tm):
    t = p.shape[0]
    n_grp, w = wm.shape[0], lng.shape[1]
    hd = w // n_grp
    nblk = tm // SGU_BLOCK

    def body(uv_ref, h_ref, g_ref, b_ref, wm_ref, bs_ref, sa_ref, z_ref, vn_s):
        xhat, _ = _row_stats(_gelu(uv_ref[:, w:]))
        vn_s[...] = (xhat * g_ref[...] + b_ref[...]).astype(BF16)
        mask = _sgu_mask()
        for h in range(n_grp):
            wh = jnp.where(mask, wm_ref[h], 0.0).astype(BF16)
            bcol = bs_ref[:, h:h + 1]
            cs = slice(h * hd, (h + 1) * hd)
            for n in range(nblk):
                rs = slice(n * SGU_BLOCK, (n + 1) * SGU_BLOCK)
                s = jnp.dot(wh, vn_s[rs, cs], preferred_element_type=F32) + bcol
                sa_ref[rs, cs] = (_gelu(uv_ref[rs, cs]) * s).astype(BF16)
        z_ref[...] = h_ref[:, :w] * jax.nn.sigmoid(h_ref[:, w:])

    return _call(
        body, name=name, grid=(t // tm,),
        in_specs=[_rows(tm, 2 * w, 0), _rows(tm, 2 * w, 1), _whole((1, w)), _whole((1, w)), _whole(wm.shape),
                  _whole(bst.shape)],
        out_specs=[_rows(tm, w), _rows(tm, w)],
        out_shape=[jax.ShapeDtypeStruct((t, w), BF16), jax.ShapeDtypeStruct((t, w), F32)],
        args=[p, p, lng, lnb, wm, bst], scratch=[pltpu.VMEM((tm, w), BF16)], sem=("parallel",))


def _sgu_bwd(name, p, dsa, lng, lnb, wm, bst, tm):
    t = p.shape[0]
    n_grp, w = wm.shape[0], lng.shape[1]
    hd = w // n_grp
    nblk = tm // SGU_BLOCK

    def body(uv_ref, dsa_ref, g_ref, b_ref, wm_ref, bs_ref, dp_ref, dwm_ref, dbs_ref, dg_ref, db_ref, dbin_ref,
             vn_s, ug_s, dvn_s, dug_s):
        i = pl.program_id(0)

        @pl.when(i == 0)
        def _():
            dwm_ref[...] = jnp.zeros_like(dwm_ref)
            dbs_ref[...] = jnp.zeros_like(dbs_ref)
            dg_ref[...] = jnp.zeros_like(dg_ref)
            db_ref[...] = jnp.zeros_like(db_ref)
            dbin_ref[...] = jnp.zeros_like(dbin_ref)

        ug, dgelu_u = _gelu_and_grad(uv_ref[:, :w])
        ug_s[...] = ug
        vg, dgelu_v = _gelu_and_grad(uv_ref[:, w:])
        xhat, rstd = _row_stats(vg)
        gv = g_ref[...]
        vn_s[...] = (xhat * gv + b_ref[...]).astype(BF16)
        mask = _sgu_mask()
        for h in range(n_grp):
            wh = jnp.where(mask, wm_ref[h], 0.0).astype(BF16)
            bcol = bs_ref[:, h:h + 1]
            cs = slice(h * hd, (h + 1) * hd)
            dw_h = jnp.zeros((SGU_BLOCK, SGU_BLOCK), F32)
            dbs_h = jnp.zeros((SGU_BLOCK, 1), F32)
            for n in range(nblk):
                rs = slice(n * SGU_BLOCK, (n + 1) * SGU_BLOCK)
                vblk = vn_s[rs, cs]
                s = jnp.dot(wh, vblk, preferred_element_type=F32) + bcol
                dsa_blk = dsa_ref[rs, cs]
                dug_s[rs, cs] = dsa_blk * s
                ds = dsa_blk * ug_s[rs, cs]
                dsb = ds.astype(BF16)
                dvn_s[rs, cs] = lax.dot_general(wh, dsb, (((0,), (0,)), ((), ())), preferred_element_type=F32)
                dw_h = dw_h + lax.dot_general(dsb, vblk, (((1,), (1,)), ((), ())), preferred_element_type=F32)
                dbs_h = dbs_h + jnp.sum(ds, axis=1, keepdims=True)
            dwm_ref[h] += jnp.where(mask, dw_h, 0.0)
            dbs_ref[:, h:h + 1] += dbs_h
        dvn = dvn_s[...]
        dg_ref[...] += _colsum(dvn * xhat)
        db_ref[...] += _colsum(dvn)
        dvg = _ln_bwd_rows(dvn, xhat, rstd, gv)
        du = dug_s[...] * dgelu_u
        dv = dvg * dgelu_v
        dp_ref[:, :w] = du.astype(BF16)
        dp_ref[:, w:] = dv.astype(BF16)
        dbin_ref[:, :w] += _colsum(du)
        dbin_ref[:, w:] += _colsum(dv)

    return _call(
        body, name=name, grid=(t // tm,),
        in_specs=[_rows(tm, 2 * w, 0), _rows(tm, w), _whole((1, w)), _whole((1, w)), _whole(wm.shape), _whole(bst.shape)],
        out_specs=[_rows(tm, 2 * w), _whole(wm.shape), _whole(bst.shape), _whole((1, w)), _whole((1, w)), _whole((1, 2 * w))],
        out_shape=[jax.ShapeDtypeStruct((t, 2 * w), BF16), jax.ShapeDtypeStruct(wm.shape, F32),
                   jax.ShapeDtypeStruct(bst.shape, F32), jax.ShapeDtypeStruct((1, w), F32), jax.ShapeDtypeStruct((1, w), F32),
                   jax.ShapeDtypeStruct((1, 2 * w), F32)],
        args=[p, dsa, lng, lnb, wm, bst],
        scratch=[pltpu.VMEM((tm, w), BF16), pltpu.VMEM((tm, w), F32), pltpu.VMEM((tm, w), F32), pltpu.VMEM((tm, w), F32)],
        sem=("arbitrary",))


def _halo_prev(tm, c):
    return pl.BlockSpec((HALO, c), lambda i: (jnp.maximum(i * (tm // HALO) - 1, 0), 0))


def _halo_next(tm, c, t):
    last = t // HALO - 1
    return pl.BlockSpec((HALO, c), lambda i: (jnp.minimum((i + 1) * (tm // HALO), last), 0))


def _shifted_copies(sh, n):
    for r in range(1, SUBLANES):
        sh[r, :n - SUBLANES, :] = sh[0, r:r + n - SUBLANES, :]


def _row_broadcasts(wb, w_ref, kw):
    for k in range(kw):
        wb[k] = jnp.broadcast_to(w_ref[k:k + 1, :], wb.shape[1:])


def _tap(sh, r0, o, rows, cols):
    return sh[o % SUBLANES, pl.ds(pl.multiple_of(r0 + (o - o % SUBLANES), SUBLANES), rows), cols]


def _conv_fwd(name, z, wdw, kw, bdw, lng, lnb, tm, carry=None):
    t, c = z.shape
    lead = HALO - (kw - 1)
    n = tm + HALO

    def body(zp_ref, z_ref, w_ref, bdw_ref, g_ref, b_ref, zc_ref, sb_ref, sh, wb):
        i = pl.program_id(0)
        sh[0, :HALO, :] = jnp.where(i > 0, zp_ref[...], 0.0)
        sh[0, HALO:, :] = z_ref[...]
        _shifted_copies(sh, n)
        _row_broadcasts(wb, w_ref, kw)
        bias = jnp.broadcast_to(bdw_ref[...], (SUBLANES, c))
        groups = CONV_ROWS // SUBLANES

        def chunk(ci, _):
            r0 = pl.multiple_of(ci * CONV_ROWS, CONV_ROWS)
            accs = [bias] * groups
            for k in range(kw):
                wk = wb[k]
                tp = _tap(sh, r0, lead + k, CONV_ROWS, slice(None))
                accs = [accs[g] + wk * tp[g * SUBLANES:(g + 1) * SUBLANES] for g in range(groups)]
            zc_ref[pl.ds(r0, CONV_ROWS), :] = jnp.concatenate(accs, axis=0)
            return 0

        lax.fori_loop(0, tm // CONV_ROWS, chunk, 0)
        xhat, _ = _row_stats(zc_ref[...])
        zn = xhat * g_ref[...] + b_ref[...]
        sb_ref[...] = (zn * jax.nn.sigmoid(zn)).astype(BF16)

    return _call(
        body, name=name, grid=(t // tm,),
        in_specs=[_halo_prev(tm, c), _rows(tm, c), _whole(wdw.shape), _whole((1, c)), _whole((1, c)), _whole((1, c))],
        out_specs=[_rows(tm, c), _rows(tm, c)],
        out_shape=[jax.ShapeDtypeStruct((t, c), F32), jax.ShapeDtypeStruct((t, c), BF16)],
        args=[z, z, wdw, bdw, lng, lnb], scratch=[pltpu.VMEM((SUBLANES, n, c), F32), pltpu.VMEM((HALO, SUBLANES, c), F32)],
        sem=("parallel",), carry=carry)


def _conv_ln_bwd(name, dsb, zc, lng, lnb, tm):
    t, c = zc.shape

    def body(dsb_ref, zc_ref, g_ref, b_ref, dzc_ref, dg_ref, db_ref, dbdw_ref):
        i = pl.program_id(0)
        xhat, rstd = _row_stats(zc_ref[...])
        gv = g_ref[...]
        zn = xhat * gv + b_ref[...]
        dzn = dsb_ref[...] * _silu_grad(zn)
        dzc = _ln_bwd_rows(dzn, xhat, rstd, gv)
        dzc_ref[...] = dzc

        @pl.when(i == 0)
        def _():
            dg_ref[...] = jnp.zeros_like(dg_ref)
            db_ref[...] = jnp.zeros_like(db_ref)
            dbdw_ref[...] = jnp.zeros_like(dbdw_ref)

        dg_ref[...] += _colsum(dzn * xhat)
        db_ref[...] += _colsum(dzn)
        dbdw_ref[...] += _colsum(dzc)

    return _call(
        body, name=name, grid=(t // tm,), in_specs=[_rows(tm, c), _rows(tm, c), _whole((1, c)), _whole((1, c))],
        out_specs=[_rows(tm, c), _whole((1, c)), _whole((1, c)), _whole((1, c))],
        out_shape=[jax.ShapeDtypeStruct((t, c), F32)] + [jax.ShapeDtypeStruct((1, c), F32)] * 3,
        args=[dsb, zc, lng, lnb], sem=("arbitrary",))


def _conv_bwd(name, dzc, z, p, wdw, kw, tm):
    t, c = z.shape
    n_i = t // tm
    n = tm + HALO

    def body(dzc_ref, dzn_ref, z_ref, h_ref, w_ref, dp_ref, dw_ref, dbin_ref, sh, dz_s, wb):
        i = pl.program_id(0)

        @pl.when(i == 0)
        def _():
            dw_ref[...] = jnp.zeros_like(dw_ref)

        sh[0, :tm, :] = dzc_ref[...]
        sh[0, tm:, :] = jnp.where(i < n_i - 1, dzn_ref[...], 0.0)
        _shifted_copies(sh, n)
        _row_broadcasts(wb, w_ref, kw)
        nv = 4
        rows = nv * SUBLANES
        ways = 2
        for lc in range(c // LANES):
            cols = slice(lc * LANES, (lc + 1) * LANES)

            def rowv(rv, accs, cols=cols):
                r0 = pl.multiple_of(rv * rows, rows)
                zv = z_ref[pl.ds(r0, rows), cols]
                zs = [zv[v * SUBLANES:(v + 1) * SUBLANES] for v in range(nv)]
                dz = [[None] * ways for _ in range(nv)]
                new = []
                for k in range(kw):
                    s = _tap(sh, r0, kw - 1 - k, rows, cols)
                    wk = wb[k, :, cols]
                    j = k % ways
                    acc = accs[k]
                    for v in range(nv):
                        sv = s[v * SUBLANES:(v + 1) * SUBLANES]
                        dz[v][j] = wk * sv if dz[v][j] is None else dz[v][j] + wk * sv
                        acc = acc + zs[v] * sv
                    new.append(acc)
                dz_s[pl.ds(r0, rows), cols] = jnp.concatenate([dz[v][0] + dz[v][1] for v in range(nv)], axis=0)
                return tuple(new)

            accs = lax.fori_loop(0, tm // rows, rowv, tuple(jnp.zeros((SUBLANES, LANES), F32) for _ in range(kw)))
            for k in range(kw):
                dw_ref[k:k + 1, cols] += _colsum(accs[k])
        dz = dz_s[...]
        a, g = h_ref[:, :c], h_ref[:, c:]
        sg = jax.nn.sigmoid(g)
        da = dz * sg
        dg = dz * a * (sg * (1.0 - sg))
        dp_ref[:, :c] = da.astype(BF16)
        dp_ref[:, c:] = dg.astype(BF16)

        @pl.when(i == 0)
        def _():
            dbin_ref[...] = jnp.zeros_like(dbin_ref)

        dbin_ref[:, :c] += _colsum(da)
        dbin_ref[:, c:] += _colsum(dg)

    return _call(
        body, name=name, grid=(n_i,),
        in_specs=[_rows(tm, c), _halo_next(tm, c, t), _rows(tm, c), _rows(tm, 2 * c, 1), _whole(wdw.shape)],
        out_specs=[_rows(tm, 2 * c), _whole((HALO, c)), _whole((1, 2 * c))],
        out_shape=[jax.ShapeDtypeStruct((t, 2 * c), BF16), jax.ShapeDtypeStruct((HALO, c), F32),
                   jax.ShapeDtypeStruct((1, 2 * c), F32)],
        args=[dzc, dzc, z, p, wdw],
        scratch=[pltpu.VMEM((SUBLANES, n, c), F32), pltpu.VMEM((tm, c), F32), pltpu.VMEM((HALO, SUBLANES, c), F32)],
        sem=("arbitrary",))


def _adamw(name, w, g, m, v, tr, carry=None):
    r, c = w.shape
    c1 = 1.0 - ADAM_B1 ** ADAM_STEP
    c2 = 1.0 - ADAM_B2 ** ADAM_STEP

    def body(w_ref, g_ref, m_ref, v_ref, d_ref, mo_ref, vo_ref):
        gv = g_ref[...]
        mn = ADAM_B1 * m_ref[...] + (1.0 - ADAM_B1) * gv
        vn = ADAM_B2 * v_ref[...] + (1.0 - ADAM_B2) * (gv * gv)
        d_ref[...] = -ADAM_LR * ((mn / c1) / (jnp.sqrt(vn / c2) + ADAM_EPS) + ADAM_WD * w_ref[...])
        mo_ref[...] = mn
        vo_ref[...] = vn

    spec = _rows(tr, c)
    return _call(body, name=name, grid=(r // tr,), in_specs=[spec] * 4, out_specs=[spec] * 3,
                 out_shape=[jax.ShapeDtypeStruct((r, c), F32)] * 3, args=[w, g, m, v], sem=("parallel",), carry=carry)


def _pair_sum(name, mine, recv, hc, *, out_rows, tr, row_map):
    c_ = mine.shape[1]

    def body(hc_ref, a_ref, b_ref, oh_ref):
        oh_ref[...] = (a_ref[...] + b_ref[...].astype(F32)).astype(BF16)

    ispec = pl.BlockSpec((tr, c_), lambda i, hc_ref: (row_map(i, hc_ref[0]), 0))
    ospec = pl.BlockSpec((tr, c_), lambda i, hc_ref: (i, 0))
    return pl.pallas_call(
        body, name=name,
        grid_spec=pltpu.PrefetchScalarGridSpec(num_scalar_prefetch=1, grid=(out_rows // tr,), in_specs=[ispec, ispec],
                                               out_specs=ospec),
        out_shape=jax.ShapeDtypeStruct((out_rows, c_), BF16), compiler_params=_cparams(("parallel",)))(hc, mine, recv)


def _final_sum(name, mine, recv1, recv2, qc, *, part_shape, out_shape, tr, in_map, out_map):
    pr, pc = part_shape

    def body(qc_ref, a_ref, b_ref, r_ref, o_ref):
        own = a_ref[...] + b_ref[...].astype(F32)
        o_ref[...] = ((own + r_ref[0].astype(F32)) + r_ref[1].astype(F32)) + r_ref[2].astype(F32)

    ispec = pl.BlockSpec((tr, pc), lambda i, qc_ref: in_map(i, qc_ref[0], qc_ref[1]))
    return pl.pallas_call(
        body, name=name,
        grid_spec=pltpu.PrefetchScalarGridSpec(
            num_scalar_prefetch=1, grid=(pr // tr,),
            in_specs=[ispec, ispec, pl.BlockSpec((3, tr, pc), lambda i, qc_ref: (0, i, 0))],
            out_specs=pl.BlockSpec((tr, pc), lambda i, qc_ref: out_map(i, qc_ref[0], qc_ref[1]))),
        out_shape=jax.ShapeDtypeStruct(out_shape, F32), compiler_params=_cparams(("parallel",)))(qc, mine, recv1, recv2)


def _sum8(name, slots):
    def body(s_ref, o_ref):
        acc = s_ref[0]
        for d in range(1, 8):
            acc = acc + s_ref[d]
        o_ref[...] = acc

    return pl.pallas_call(body, name=name, out_shape=jax.ShapeDtypeStruct(slots.shape[1:], F32),
                          in_specs=[pl.BlockSpec(memory_space=pltpu.VMEM)], out_specs=pl.BlockSpec(memory_space=pltpu.VMEM),
                          compiler_params=pltpu.CompilerParams(vmem_limit_bytes=VMEM_LIMIT))(slots)


def _chips(x, y):
    return [(1 - x, y), (x, 1 - y), (1 - x, 1 - y)]


class _Big:
    def __init__(self, name, w, m, v, ax):
        self.name, self.w, self.m, self.v, self.ax = name, w, m, v, ax
        sr, sc = w.shape
        self.R, self.C = (sr * N_CHIPS, sc) if ax == 0 else (sr, sc * N_CHIPS)
        self.sr, self.sc = sr, sc
        self.hr = sr // 2

    def slot(self, ref, q, h=None):
        if self.ax == 1:
            cols = pl.ds(pl.multiple_of(q * self.sc, 128), self.sc)
            return ref.at[:, cols] if h is None else ref.at[pl.ds(pl.multiple_of(h * self.hr, 16), self.hr), cols]
        if h is None:
            return ref.at[pl.ds(pl.multiple_of(q * self.sr, 16), self.sr), :]
        return ref.at[pl.ds(pl.multiple_of(q * self.sr + h * self.hr, 16), self.hr), :]

    def half(self, ref, h):
        return ref.at[pl.ds(pl.multiple_of(h * self.hr, 16), self.hr), :]

    def part(self, ref, q):
        if self.ax == 1:
            return ref.at[:, pl.ds(pl.multiple_of(q * self.sc, 128), self.sc)]
        return ref.at[pl.ds(pl.multiple_of(q * self.hr, 16), self.hr), :]

    @property
    def part_shape(self):
        return (self.hr, self.sc)

    def cast_into_full(self, qc):
        tr = _tile_rows(self.sr, self.sc)
        nb = self.sr // tr

        def body(qc_ref, x_ref, o_ref):
            o_ref[...] = x_ref[...].astype(BF16)

        if self.ax == 1:
            ospec = pl.BlockSpec((tr, self.sc), lambda i, qc_ref: (i, qc_ref[0]))
        else:
            ospec = pl.BlockSpec((tr, self.sc), lambda i, qc_ref: (qc_ref[0] * nb + i, 0))
        return pl.pallas_call(
            body, name=f"cast_{self.name}",
            grid_spec=pltpu.PrefetchScalarGridSpec(num_scalar_prefetch=1, grid=(nb,),
                                                   in_specs=[pl.BlockSpec((tr, self.sc), lambda i, qc_ref: (i, 0))],
                                                   out_specs=ospec),
            out_shape=jax.ShapeDtypeStruct((self.R, self.C), BF16), compiler_params=_cparams(("parallel",)))(qc, self.w)

    def gather_ici(self, full, piece=(0, 1)):
        k, n = piece
        pr = self.hr // n
        assert pr * n == self.hr and pr % 16 == 0

        def plan(ro, rw, new, x, y, c):
            q = 2 * x + y
            r0 = c * self.hr + k * pr
            if self.ax == 1:
                mine = rw[0].at[pl.ds(pl.multiple_of(r0, 16), pr), pl.ds(pl.multiple_of(q * self.sc, 128), self.sc)]
            else:
                mine = rw[0].at[pl.ds(pl.multiple_of(q * self.sr + r0, 16), pr), :]
            return [(mine, mine, (cx, cy, c)) for cx, cy in _chips(x, y)], []

        return _Carry([], [full], [], plan, 3, 0)

    def gather_d2d(self, full):
        def plan(ro, rw, new, x, y, c):
            remote = []
            for cx, cy in _chips(x, y):
                piece = self.slot(rw[0], 2 * cx + cy, c)
                remote.append((piece, piece, (x, y, 1 - c)))
            return remote, []

        return _Carry([], [full], [], plan, 3)

    def rs_pair(self, g16):
        def plan(ro, rw, new, x, y, c):
            sib = (x, y, 1 - c)
            if self.ax == 1:
                rows = pl.ds(pl.multiple_of((1 - c) * self.hr, 16), self.hr)
                return [(ro[0].at[rows, :], new[0].at[rows, :], sib)], []
            return [(self.slot(ro[0], q, 1 - c), self.slot(new[0], q, 1 - c), sib) for q in range(N_CHIPS)], []

        return _Carry([g16], [], [jax.ShapeDtypeStruct((self.R, self.C), BF16)], plan, 1 if self.ax == 1 else N_CHIPS)

    def rs_pairsum(self, tag, g32, recv1, hc):
        tr = _tile_rows(self.hr, self.C)
        nb = self.hr // tr
        if self.ax == 1:
            row_map = lambda i, c: c * nb + i
        else:
            row_map = lambda i, c: ((i // nb) * 2 + c) * nb + i % nb
        return _pair_sum(f"rs_pairsum_{tag}", g32, recv1, hc, out_rows=self.R // 2, tr=tr, row_map=row_map)

    def rs_ici(self, cs16):
        def plan(ro, rw, new, x, y, c):
            return [(self.part(ro[0], 2 * cx + cy), new[0].at[k], (cx, cy, c)) for k, (cx, cy) in enumerate(_chips(x, y))], []

        return _Carry([cs16], [], [jax.ShapeDtypeStruct((3,) + self.part_shape, BF16)], plan, 3)

    def rs_final(self, tag, g32, recv1, recv2, qc):
        tr = _tile_rows(self.hr, self.sc)
        nb = self.hr // tr
        if self.ax == 1:
            in_map = lambda i, q, c: (c * nb + i, q)
        else:
            in_map = lambda i, q, c: ((q * 2 + c) * nb + i, 0)
        out_map = lambda i, q, c: (c * nb + i, 0)
        return _final_sum(f"rs_final_{tag}", g32, recv1, recv2, qc, part_shape=self.part_shape, out_shape=(self.sr, self.sc),
                          tr=tr, in_map=in_map, out_map=out_map)

    def rs_share(self, ghalf):
        def plan(ro, rw, new, x, y, c):
            piece = self.half(rw[0], c)
            return [(piece, piece, (x, y, 1 - c))], []

        return _Carry([], [ghalf], [], plan, 1)


def _small_allgather(packed):
    nr = packed.shape[0]

    def plan(ro, rw, new, x, y, c):
        me = 4 * x + 2 * y + c
        remote = []
        for fx in (0, 1):
            for fy in (0, 1):
                for fc in (0, 1):
                    if fx or fy or fc:
                        dev = (1 - x if fx else x, 1 - y if fy else y, 1 - c if fc else c)
                        remote.append((ro[0], new[0].at[me], dev))
        return remote, [(ro[0], new[0].at[me])]

    return _Carry([packed], [], [jax.ShapeDtypeStruct((8, nr, 128), F32)], plan, 7, 1)


def _conv_w_allgather(padded, cs):
    def plan(ro, rw, new, x, y, c):
        cols = pl.ds(pl.multiple_of((2 * x + y) * cs, 128), cs)
        remote = [(ro[0], new[0].at[:, cols], (cx, cy, c)) for cx, cy in _chips(x, y)]
        return remote, [(ro[0], new[0].at[:, cols])]

    return _Carry([padded], [], [jax.ShapeDtypeStruct((HALO, cs * N_CHIPS), F32)], plan, 3, 1)


def _pick(n, want):
    if n <= want:
        return n
    for t in range(want, 15, -16):
        if t % 16 == 0 and n % t == 0:
            return t
    raise ValueError(f"no tile for {n} (want {want})")


def _tile_rows(nrows, ncols, budget=2 * 1024 * 1024):
    return _pick(nrows, max(16, (budget // (4 * ncols)) // 16 * 16))


def _pick128(n, want):
    if n <= want:
        return n
    for t in range(want, 127, -128):
        if n % t == 0:
            return t
    raise ValueError(f"no lane tile for {n} (want {want})")


def _pack_rows(parts):
    out, spans, r0 = [], [], 0
    for p in parts:
        flat = p.reshape(-1).astype(F32)
        n = flat.shape[0]
        rows = -(-n // 1024) * 8
        flat = jnp.pad(flat, (0, rows * 128 - n))
        out.append(flat.reshape(rows, 128))
        spans.append((r0, rows, n))
        r0 += rows
    return jnp.concatenate(out, axis=0), spans


def _unpack_rows(packed, spans, shapes):
    res = []
    for (r0, rows, n), shp in zip(spans, shapes, strict=True):
        res.append(packed[r0:r0 + rows].reshape(-1)[:n].reshape(shp))
    return res


def _ident(accs, ex):
    return [accs[0]]


def kernel(x, ffn1_w_gu, ffn1_w_down, ln1_g, ln1_b, w_in, b_in, sgu_ln_g, sgu_ln_b, sgu_w_s, sgu_b_s, w_a_proj, conv_w_dw, conv_b_dw, conv_ln_g, conv_ln_b, w_b_proj, w_out, ln2_g, ln2_b, ffn2_w_gu, ffn2_w_down, ln3_g, ln3_b, loss_target, m_ffn1_w_gu, m_ffn1_w_down, m_ln1_g, m_ln1_b, m_w_in, m_b_in, m_sgu_ln_g, m_sgu_ln_b, m_sgu_w_s, m_sgu_b_s, m_w_a_proj, m_conv_w_dw, m_conv_b_dw, m_conv_ln_g, m_conv_ln_b, m_w_b_proj, m_w_out, m_ln2_g, m_ln2_b, m_ffn2_w_gu, m_ffn2_w_down, m_ln3_g, m_ln3_b, v_ffn1_w_gu, v_ffn1_w_down, v_ln1_g, v_ln1_b, v_w_in, v_b_in, v_sgu_ln_g, v_sgu_ln_b, v_sgu_w_s, v_sgu_b_s, v_w_a_proj, v_conv_w_dw, v_conv_b_dw, v_conv_ln_g, v_conv_ln_b, v_w_b_proj, v_w_out, v_ln2_g, v_ln2_b, v_ffn2_w_gu, v_ffn2_w_down, v_ln3_g, v_ln3_b):
    args = dict(locals())
    assert x.shape[0] == 1 and ffn1_w_gu.shape[0] == 1
    T, D = x.shape[1], x.shape[2]
    F = ffn1_w_down.shape[1] * N_CHIPS
    W = sgu_ln_g.shape[1]
    KW = conv_w_dw.shape[1]
    assert KW - 1 <= HALO and T % SGU_BLOCK == 0

    mx, my, mc = lax.axis_index("x"), lax.axis_index("y"), lax.axis_index("c")
    q = 2 * mx + my
    hc = jnp.reshape(mc, (1,)).astype(jnp.int32)
    qc = jnp.stack([q, mc]).astype(jnp.int32)

    big_names = [("ffn1_w_gu", 1), ("ffn1_w_down", 0), ("w_in", 1), ("w_a_proj", 1), ("w_b_proj", 1), ("w_out", 0),
                 ("ffn2_w_gu", 1), ("ffn2_w_down", 0)]
    B = {n: _Big(n, args[n][0], args["m_" + n][0], args["v_" + n][0], ax) for n, ax in big_names}
    own = {n: b.cast_into_full(qc) for n, b in B.items()}

    x2d = x[0]
    tgt = loss_target[0]
    tm_r = _pick(T, 256)
    tm_ln = _pick(T, 512)
    tm = _pick(T, 1024)
    tm_h = _pick(T, 512)
    tn = _pick128(D, 1024)
    nj = D // tn
    tng = _pick128(D, 512)
    njg = D // tng
    tnf = _pick128(F, 512)
    nf = F // tnf
    tnw = _pick128(W, 1024)
    tnd = _pick128(D, 512)
    tmw = _pick128(W, 512)
    SUB = 256

    def ffn_up(tag, xb_, wgu, carry=None):
        def epi(accs, ex):
            g, u = accs
            s = jax.nn.sigmoid(g)
            sg = g * s
            return [u * (s * (1.0 + g * (1.0 - s))), sg, sg * u]

        return _mm(f"{tag}_up", [(xb_, wgu, 0, 0, 0), (xb_, wgu, 0, 0, nf)], M=T, N=F, tm=tm, tn=tnf, tk=D, nk=1, epilogue=epi,
                   outs=[(F, BF16, 0)] * 3, sum_pairs=False, carry=carry, sub=SUB)

    def ffn_down(tag, act, wd, carry=None):
        return _mm(f"{tag}_down", [(act, wd, 0, 0, 0)], M=T, N=D, tm=tm, tn=_pick128(D, 512), tk=F, nk=1, epilogue=_ident,
                   outs=[(D, F32, 0)], carry=carry)

    def ffn_dact(tag, drh, wd, dgate_f, dup_f, carry=None):
        def epi(accs, ex):
            da = accs[0]
            return [da * ex[0].astype(F32), da * ex[1].astype(F32)]

        return _mm(f"{tag}_dact", [(drh, wd, 0, 0, 0)], tb=True, M=T, N=F, tm=tm, tn=tnf, tk=D, nk=1, epilogue=epi,
                   outs=[(F, BF16, 0)] * 2, extras=[(dgate_f, "mn", 0), (dup_f, "mn", 0)], carry=carry, sub=SUB)

    def ffn_dwdown(tag, act, drh, carry=None):
        return _wgrad(f"{tag}_dwdown", act, drh, M=F, N=D, T=T, tm=tnf, tn=tnd, tk=T, carry=carry)

    def ffn_dwgate(tag, xb_, dg, carry=None):
        return _wgrad(f"{tag}_dwgate", xb_, dg, M=D, N=F, T=T, tm=tnd, tn=tnf, tk=T, ncols=2 * F, carry=carry)

    def ffn_dwup(tag, xb_, du, into, carry=None):
        return _wgrad(f"{tag}_dwup", xb_, du, M=D, N=F, T=T, tm=tnd, tn=tnf, tk=T, ncols=2 * F, into=into, joff=nf, carry=carry)

    def ffn_dx(tag, parts, wgu, addends, carry=None):
        pairs = [(da, wgu, 0, 1 if which == "up" else 0, 0) for which, da in parts]

        def epi(accs, ex):
            tot = accs[0] + ALPHA * ex[0]
            for e in ex[1:]:
                tot = tot + e
            return [tot]

        return _mm(f"{tag}_dx_{'_'.join(w_ for w_, _ in parts)}", pairs, tb=True, M=T, N=D, tm=tm_h,
                   tn=_pick128(D, 512 // len(parts)), tk=F, nk=1, epilogue=epi if addends else _ident, outs=[(D, F32, 0)],
                   extras=[(a, "mn", 0) for a in addends], carry=carry)

    wdw_pad = jnp.pad(conv_w_dw[0], ((0, HALO - KW), (0, 0)))
    c0, un = _merge(B["ffn1_w_gu"].gather_ici(own["ffn1_w_gu"]), _conv_w_allgather(wdw_pad, conv_w_dw.shape[2]))
    (wgu1,), (wdw_full,) = un(_exchange("gather_first", c0))
    (xb,), (wgu1,) = _cast_bf16("cast_x", x2d, tm_r, carry=B["ffn1_w_gu"].gather_d2d(wgu1))

    c, un = _merge(B["ffn1_w_down"].gather_ici(own["ffn1_w_down"]), B["w_in"].gather_ici(own["w_in"]))
    (g1, u1, a1), co = ffn_up("ffn1", xb, wgu1, carry=c)
    (wd1,), (win,) = un(co)
    (wd1,) = _exchange("gather_d2d_ffn1_w_down", B["ffn1_w_down"].gather_d2d(wd1))
    b_gu2, b_d2 = B["ffn2_w_gu"], B["ffn2_w_down"]
    c, un = _merge(B["w_a_proj"].gather_ici(own["w_a_proj"]), B["w_b_proj"].gather_ici(own["w_b_proj"]),
                   B["w_out"].gather_ici(own["w_out"]), B["w_in"].gather_d2d(win), b_gu2.gather_ici(own["ffn2_w_gu"], (0, 4)))
    (fo1,), co = ffn_down("ffn1", a1, wd1, carry=c)
    (wa,), (wb,), (wout,), (win,), (wgu2,) = un(co)
    c, un = _merge(B["w_a_proj"].gather_d2d(wa), B["w_b_proj"].gather_d2d(wb), B["w_out"].gather_d2d(wout),
                   b_gu2.gather_ici(wgu2, (1, 4)))
    (x1, x1b, xh1, rs1), co = _ln_fwd("ln1", x2d, fo1, ln1_g, ln1_b, 0.5, tm_ln, carry=c)
    (wa,), (wb,), (wout,), (wgu2,) = un(co)

    c, un = _merge(b_gu2.gather_ici(wgu2, (1, 2)), b_d2.gather_ici(own["ffn2_w_down"], (0, 2)))
    (proj,), co = _mm("in_proj", [(x1b, win, 0, 0, 0)], M=T, N=4 * D, tm=tm, tn=tn, tk=D, nk=1,
                      epilogue=lambda accs, ex: [accs[0] + ex[0]], outs=[(4 * D, F32, 0)], extras=[(b_in, "n", 0)], carry=c)
    (wgu2,), (wd2,) = un(co)
    wm = sgu_w_s[0]
    bst = sgu_b_s[0].T
    sa, z = _sgu_fwd("sgu_fwd", proj, sgu_ln_g, sgu_ln_b, wm, bst, tm_r)
    (zc, sb), (wd2,) = _conv_fwd("conv_fwd", z, wdw_full, KW, conv_b_dw, conv_ln_g, conv_ln_b, tm_r,
                                 carry=b_d2.gather_ici(wd2, (1, 2)))

    def epi_mix(accs, ex):
        ya_, yb_ = accs
        ga, gb = jax.nn.sigmoid(ex[0]), jax.nn.sigmoid(ex[1])
        return [ga * ya_ + gb * yb_, ga, gb, ya_ * (ga * (1.0 - ga)), yb_ * (gb * (1.0 - gb))]

    (mixin, gate_a, gate_b, dlog_a, dlog_b), (wgu2,) = _mm(
        "branch_proj", [(sa, wa, 0, 0, 0), (sb, wb, 0, 0, 0)], M=T, N=D, tm=tm, tn=tng, tk=W, nk=1, epilogue=epi_mix,
        outs=[(D, BF16, 0)] * 5, sum_pairs=False, sub=SUB, extras=[(proj, "mn", 2 * njg), (proj, "mn", 3 * njg)],
        carry=B["ffn2_w_gu"].gather_d2d(wgu2))
    (mix,), (wd2,) = _mm("out_proj", [(mixin, wout, 0, 0, 0)], M=T, N=D, tm=tm, tn=tn, tk=D, nk=1, epilogue=_ident,
                         outs=[(D, F32, 0)], carry=B["ffn2_w_down"].gather_d2d(wd2))
    x2, x2b, xh2, rs2 = _ln_fwd("ln2", x1, mix, ln2_g, ln2_b, 1.0, tm_ln)
    g2, u2, a2 = ffn_up("ffn2", x2b, wgu2)
    (fo2,) = ffn_down("ffn2", a2, wd2)
    dr3, dr3h, loss_part, dln3_g, dln3_b = _ln3_loss("ln3_loss", x2, fo2, ln3_g, ln3_b, tgt, tm_r)

    b_gu2, b_d2 = B["ffn2_w_gu"], B["ffn2_w_down"]
    dg2, du2 = ffn_dact("ffn2", dr3h, wd2, g2, u2)
    dwd2 = ffn_dwdown("ffn2", a2, dr3h)
    dwgu2, (r1_d2,) = ffn_dwgate("ffn2", x2b, dg2, carry=b_d2.rs_pair(dwd2[1]))
    dwgu2 = ffn_dwup("ffn2", x2b, du2, dwgu2)
    cs_d2 = b_d2.rs_pairsum("ffn2_w_down", dwd2[0], r1_d2, hc)
    c, un = _merge(b_gu2.rs_pair(dwgu2[1]), b_d2.rs_ici(cs_d2))
    (dx2,), co = ffn_dx("ffn2", [("gate", dg2), ("up", du2)], wgu2, [dr3], carry=c)
    (r1_gu2,), (r2_d2,) = un(co)
    cs_gu2 = b_gu2.rs_pairsum("ffn2_w_gu", dwgu2[0], r1_gu2, hc)
    gh_d2 = b_d2.rs_final("ffn2_w_down", dwd2[0], r1_d2, r2_d2, qc)

    (dr2, dr2b, dln2_g, dln2_b), (g_d2,) = _ln_bwd("ln2_bwd", dx2, xh2, rs2, ln2_g, 1.0, tm_ln, carry=b_d2.rs_share(gh_d2))

    def epi_dmix(accs, ex):
        return [accs[0] * e.astype(F32) for e in ex]

    dya, dyb, dla, dlb = _mm("out_proj_bwd", [(dr2b, wout, 0, 0, 0)], tb=True, M=T, N=D, tm=tm, tn=tng, tk=D, nk=1,
                             epilogue=epi_dmix, outs=[(D, BF16, 0)] * 4, sub=SUB,
                             extras=[(gate_a, "mn", 0), (gate_b, "mn", 0), (dlog_a, "mn", 0), (dlog_b, "mn", 0)])
    dwout = _wgrad("dw_out", mixin, dr2b, M=D, N=D, T=T, tm=tnd, tn=tnd, tk=T)
    (dsa,) = _mm("a_proj_bwd", [(dya, wa, 0, 0, 0)], tb=True, M=T, N=W, tm=tm, tn=tnw, tk=D, nk=1, epilogue=_ident,
                 outs=[(W, F32, 0)])
    (dsb,) = _mm("b_proj_bwd", [(dyb, wb, 0, 0, 0)], tb=True, M=T, N=W, tm=tm, tn=tnw, tk=D, nk=1, epilogue=_ident,
                 outs=[(W, F32, 0)])
    dwa = _wgrad("dw_a_proj", sa, dya, M=W, N=D, T=T, tm=tmw, tn=tnd, tk=T)
    dwb = _wgrad("dw_b_proj", sb, dyb, M=W, N=D, T=T, tm=tmw, tn=tnd, tk=T)

    dpa, dwm, dbst, dsgu_g, dsgu_b, dbin_a = _sgu_bwd("sgu_bwd", proj, dsa, sgu_ln_g, sgu_ln_b, wm, bst, tm_r)
    dzc, dcln_g, dcln_b, dbdw = _conv_ln_bwd("conv_ln_bwd", dsb, zc, conv_ln_g, conv_ln_b, tm_r)
    dpb, dwdw, dbin_b = _conv_bwd("conv_bwd", dzc, z, proj, wdw_full, KW, tm_r)

    dps = [dpa, dpb, dla, dlb]
    db_in = jnp.concatenate([dbin_a, dbin_b, _colsum_rows("db_in_gate_a", dla, tm_r), _colsum_rows("db_in_gate_b", dlb, tm_r)],
                            axis=1)
    (dx1,), (r2_gu2,) = _mm("in_proj_bwd", [(dp, win, 0, k, 0) for k, dp in enumerate(dps)], tb=True, M=T, N=D, tm=tm_h,
                            tn=tnd, tk=D, nk=1, epilogue=lambda accs, ex: [accs[0] + ALPHA * ex[0]], outs=[(D, F32, 0)],
                            extras=[(dr2, "mn", 0)], carry=b_gu2.rs_ici(cs_gu2))
    gh_gu2 = b_gu2.rs_final("ffn2_w_gu", dwgu2[0], r1_gu2, r2_gu2, qc)
    dwin, (g_gu2,) = _wgrad("dw_in_0", x1b, dps[0], M=D, N=D, T=T, tm=tnd, tn=tnd, tk=T, ncols=4 * D,
                            carry=b_gu2.rs_share(gh_gu2))
    for k in range(1, 4):
        dwin = _wgrad(f"dw_in_{k}", x1b, dps[k], M=D, N=D, T=T, tm=tnd, tn=tnd, tk=T, ncols=4 * D, into=dwin,
                      joff=k * (D // tnd))

    mix_names = ["w_in", "w_a_proj", "w_b_proj", "w_out"]
    mix_grads = dict(zip(mix_names, [dwin, dwa, dwb, dwout], strict=True))
    c, un = _merge(*[B[n].rs_pair(mix_grads[n][1]) for n in mix_names])
    (dr1, dr1h, dln1_g, dln1_b), co = _ln_bwd("ln1_bwd", dx1, xh1, rs1, ln1_g, 0.5, tm_ln, carry=c)
    r1_mix = {n: r1 for n, (r1,) in zip(mix_names, un(co), strict=True)}
    cs_mix = {n: B[n].rs_pairsum(n, mix_grads[n][0], r1_mix[n], hc) for n in mix_names}

    small_names = ["ln1_g", "ln1_b", "b_in", "sgu_ln_g", "sgu_ln_b", "sgu_w_s", "sgu_b_s", "conv_w_dw", "conv_b_dw",
                   "conv_ln_g", "conv_ln_b", "ln2_g", "ln2_b", "ln3_g", "ln3_b"]
    small_parts = {"ln1_g": dln1_g, "ln1_b": dln1_b, "b_in": db_in, "sgu_ln_g": dsgu_g, "sgu_ln_b": dsgu_b, "sgu_w_s": dwm,
                   "sgu_b_s": dbst.T, "conv_w_dw": dwdw[:KW], "conv_b_dw": dbdw, "conv_ln_g": dcln_g, "conv_ln_b": dcln_b,
                   "ln2_g": dln2_g, "ln2_b": dln2_b, "ln3_g": dln3_g, "ln3_b": dln3_b}
    packed, spans = _pack_rows([small_parts[n] for n in small_names])

    b_gu1, b_d1 = B["ffn1_w_gu"], B["ffn1_w_down"]
    c, un = _merge(B["w_in"].rs_ici(cs_mix["w_in"]), _small_allgather(packed))
    (dg1, du1), co = ffn_dact("ffn1", dr1h, wd1, g1, u1, carry=c)
    (r2_win,), (small_slots,) = un(co)
    c, un = _merge(*[B[n].rs_ici(cs_mix[n]) for n in mix_names[1:]])
    dwgu1, co = ffn_dwgate("ffn1", xb, dg1, carry=c)
    r2_mix = [[r2_win]] + un(co)
    gh_mix = [B[n].rs_final(n, mix_grads[n][0], r1_mix[n], r2, qc) for n, (r2,) in zip(mix_names, r2_mix, strict=True)]
    c, un = _merge(*[B[n].rs_share(gh) for n, gh in zip(mix_names, gh_mix, strict=True)])
    dwgu1, co = ffn_dwup("ffn1", xb, du1, dwgu1, carry=c)
    g_mix = {n: g for n, (g,) in zip(mix_names, un(co), strict=True)}
    dwd1, (r1_gu1,) = ffn_dwdown("ffn1", a1, dr1h, carry=b_gu1.rs_pair(dwgu1[1]))
    cs_gu1 = b_gu1.rs_pairsum("ffn1_w_gu", dwgu1[0], r1_gu1, hc)
    c, un = _merge(b_gu1.rs_ici(cs_gu1), b_d1.rs_pair(dwd1[1]))
    (dx_gate,), co = ffn_dx("ffn1", [("gate", dg1)], wgu1, [], carry=c)
    (r2_gu1,), (r1_d1,) = un(co)
    cs_d1 = b_d1.rs_pairsum("ffn1_w_down", dwd1[0], r1_d1, hc)
    gh_gu1 = b_gu1.rs_final("ffn1_w_gu", dwgu1[0], r1_gu1, r2_gu1, qc)
    c, un = _merge(b_d1.rs_ici(cs_d1), b_gu1.rs_share(gh_gu1))
    (dx,), co = ffn_dx("ffn1", [("up", du1)], wgu1, [dr1, dx_gate], carry=c)
    (r2_d1,), (g_gu1,) = un(co)
    gh_d1 = b_d1.rs_final("ffn1_w_down", dwd1[0], r1_d1, r2_d1, qc)

    grads = {"ffn1_w_gu": g_gu1, "ffn2_w_gu": g_gu2, "ffn2_w_down": g_d2, **g_mix}
    outs_g, outs_d, outs_m, outs_v = {}, {}, {}, {}

    def adamw_big(n, g, carry=None):
        b = B[n]
        return _adamw(f"adamw_{n}", b.w, g, b.m, b.v, _tile_rows(b.sr, b.sc, 1024 * 1024), carry=carry)

    upd = {}
    upd["w_a_proj"], (grads["ffn1_w_down"],) = adamw_big("w_a_proj", grads["w_a_proj"], carry=b_d1.rs_share(gh_d1))
    for n, _ in big_names:
        if n not in upd:
            upd[n] = adamw_big(n, grads[n])
        d_, m_, v_ = upd[n]
        outs_g[n], outs_d[n], outs_m[n], outs_v[n] = grads[n][None], d_[None], m_[None], v_[None]
    gsum = _sum8("small_sum", small_slots)
    full_shapes = [args[n].shape if n != "conv_w_dw" else (1, KW, W) for n in small_names]
    gsmall = dict(zip(small_names, _unpack_rows(gsum, spans, full_shapes), strict=True))
    cs = conv_w_dw.shape[2]
    gsmall["conv_w_dw"] = lax.dynamic_slice_in_dim(gsmall["conv_w_dw"], q * cs, cs, axis=2)
    pw, spans2 = _pack_rows([args[n] for n in small_names])
    pg, _ = _pack_rows([gsmall[n] for n in small_names])
    pm, _ = _pack_rows([args["m_" + n] for n in small_names])
    pv, _ = _pack_rows([args["v_" + n] for n in small_names])
    pd, pmn, pvn = _adamw("adamw_small", pw, pg, pm, pv, pw.shape[0])
    shapes2 = [args[n].shape for n in small_names]
    for dst, src in ((outs_d, pd), (outs_m, pmn), (outs_v, pvn)):
        dst.update(zip(small_names, _unpack_rows(src, spans2, shapes2), strict=True))
    outs_g.update(gsmall)

    loss = lax.psum(loss_part[0, 0], ("x", "y", "c"))
    order = ["ffn1_w_gu", "ffn1_w_down", "ln1_g", "ln1_b", "w_in", "b_in", "sgu_ln_g", "sgu_ln_b", "sgu_w_s", "sgu_b_s",
             "w_a_proj", "conv_w_dw", "conv_b_dw", "conv_ln_g", "conv_ln_b", "w_b_proj", "w_out", "ln2_g", "ln2_b",
             "ffn2_w_gu", "ffn2_w_down", "ln3_g", "ln3_b"]
    return (loss, dx[None], *[outs_g[n] for n in order], *[outs_d[n] for n in order], *[outs_m[n] for n in order],
            *[outs_v[n] for n in order])
```

```python
import math

import jax
import jax.numpy as jnp
from jax import lax
from jax.experimental import pallas as pl
from jax.experimental.pallas import tpu as pltpu

BF16 = jnp.bfloat16
F32 = jnp.float32

LN_EPS = 1e-5
ALPHA = 2.0 ** 0.25
SGU_BLOCK = 128
SGU_CHUNK = 64
HALO = 32
SUBLANES = 8
LANES = 128
CONV_ROWS = 32
ADAM_LR = 0.001
ADAM_B1 = 0.9
ADAM_B2 = 0.999
ADAM_EPS = 1e-08
ADAM_WD = 0.01
ADAM_STEP = 10
N_CHIPS = 4
VMEM_LIMIT = 52 * 1024 * 1024
MESH = pl.DeviceIdType.MESH

_GELU_C0 = math.sqrt(2.0 / math.pi)
_GELU_C1 = 0.044715


def _cparams(sem):
    return pltpu.CompilerParams(dimension_semantics=sem, vmem_limit_bytes=VMEM_LIMIT)


def _gelu_parts(x):
    x2 = x * x
    t = jnp.tanh(_GELU_C0 * (x + _GELU_C1 * (x2 * x)))
    return 0.5 * (1.0 + t), t, x2


def _gelu(x):
    cdf, _, _ = _gelu_parts(x)
    return x * cdf


def _gelu_and_grad(x):
    cdf, t, x2 = _gelu_parts(x)
    grad = cdf + x * (0.5 * (1.0 - t * t)) * (_GELU_C0 * (1.0 + (3.0 * _GELU_C1) * x2))
    return x * cdf, grad


def _silu_grad(x):
    s = jax.nn.sigmoid(x)
    return s * (1.0 + x * (1.0 - s))


def _row_stats(x):
    mu = jnp.mean(x, axis=-1, keepdims=True)
    xc = x - mu
    var = jnp.mean(xc * xc, axis=-1, keepdims=True)
    rstd = lax.rsqrt(var + LN_EPS)
    return xc * rstd, rstd


def _ln_bwd_rows(dy, xhat, rstd, g):
    dxh = dy * g
    m1 = jnp.mean(dxh, axis=-1, keepdims=True)
    m2 = jnp.mean(dxh * xhat, axis=-1, keepdims=True)
    return rstd * (dxh - m1 - xhat * m2)


def _colsum(v):
    return jnp.sum(v, axis=0, keepdims=True)


class _Carry:
    def __init__(self, ro, rw, new, plan, n_remote, n_local=0):
        self.ro, self.rw, self.new, self.plan = list(ro), list(rw), list(new), plan
        self.n_remote, self.n_local = n_remote, n_local

    def sems(self):
        return [pltpu.SemaphoreType.DMA((self.n_remote,)), pltpu.SemaphoreType.DMA((self.n_remote,)),
                pltpu.SemaphoreType.DMA((max(self.n_local, 1),))]

    def copies(self, ro_refs, rw_refs, new_refs, send_sems, recv_sems, loc_sems):
        x, y, c = lax.axis_index("x"), lax.axis_index("y"), lax.axis_index("c")
        remote, local = self.plan(ro_refs, rw_refs, new_refs, x, y, c)
        assert len(remote) == self.n_remote and len(local) == self.n_local
        lcs = [pltpu.make_async_copy(s, d, loc_sems.at[k]) for k, (s, d) in enumerate(local)]
        rcs = [pltpu.make_async_remote_copy(src_ref=s, dst_ref=d, send_sem=send_sems.at[k], recv_sem=recv_sems.at[k],
                                            device_id=dev, device_id_type=MESH) for k, (s, d, dev) in enumerate(remote)]
        return lcs, rcs

    def out_shape(self):
        return [jax.ShapeDtypeStruct(a.shape, a.dtype) for a in self.rw] + self.new


def _start_all(lcs, rcs):
    for cp in lcs + rcs:
        cp.start()


def _wait_all(lcs, rcs):
    for cp in rcs:
        cp.wait_send()
    for cp in rcs:
        cp.wait_recv()
    for cp in lcs:
        cp.wait()


def _merge(*cs):
    ro = [a for c in cs for a in c.ro]
    rw = [a for c in cs for a in c.rw]
    new = [a for c in cs for a in c.new]

    def plan(ro_refs, rw_refs, new_refs, x, y, c):
        remote, local, a, b, d = [], [], 0, 0, 0
        for cc in cs:
            r, l = cc.plan(ro_refs[a:a + len(cc.ro)], rw_refs[b:b + len(cc.rw)], new_refs[d:d + len(cc.new)], x, y, c)
            a, b, d = a + len(cc.ro), b + len(cc.rw), d + len(cc.new)
            remote += r
            local += l
        return remote, local

    def unpack(couts):
        res, b, d = [], 0, len(rw)
        for cc in cs:
            res.append(list(couts[b:b + len(cc.rw)]) + list(couts[d:d + len(cc.new)]))
            b, d = b + len(cc.rw), d + len(cc.new)
        return res

    return _Carry(ro, rw, new, plan, sum(c.n_remote for c in cs), sum(c.n_local for c in cs)), unpack


_ANY = pl.BlockSpec(memory_space=pl.ANY)


def _exchange(name, carry):
    n_ro, n_rw, n_new = len(carry.ro), len(carry.rw), len(carry.new)

    def body(*refs):
        o0 = n_ro + n_rw
        lcs, rcs = carry.copies(refs[:n_ro], refs[o0:o0 + n_rw], refs[o0 + n_rw:o0 + n_rw + n_new], *refs[o0 + n_rw + n_new:])
        _start_all(lcs, rcs)
        _wait_all(lcs, rcs)

    return list(pl.pallas_call(
        body, name=name, in_specs=[_ANY] * (n_ro + n_rw), out_specs=[_ANY] * (n_rw + n_new), out_shape=carry.out_shape(),
        input_output_aliases={n_ro + k: k for k in range(n_rw)}, scratch_shapes=carry.sems())(*carry.ro, *carry.rw))


def _call(body, *, name, grid, in_specs, out_specs, out_shape, args, scratch=(), sem, carry=None, aliases=None):
    in_specs, out_specs, out_shape, scratch = list(in_specs), list(out_specs), list(out_shape), list(scratch)
    if carry is None:
        return list(pl.pallas_call(body, name=name, grid=grid, in_specs=in_specs, out_specs=out_specs, out_shape=out_shape,
                                   scratch_shapes=scratch, input_output_aliases=aliases or {},
                                   compiler_params=_cparams(sem))(*args))
    n_in, n_out, n_scr = len(in_specs), len(out_specs), len(scratch)
    n_ro, n_rw, n_new = len(carry.ro), len(carry.rw), len(carry.new)

    def wrapped(*refs):
        ins = refs[:n_in]
        ro_refs = refs[n_in:n_in + n_ro]
        o0 = n_in + n_ro + n_rw
        outs = refs[o0:o0 + n_out]
        rw_refs = refs[o0 + n_out:o0 + n_out + n_rw]
        new_refs = refs[o0 + n_out + n_rw:o0 + n_out + n_rw + n_new]
        s0 = o0 + n_out + n_rw + n_new
        scr = refs[s0:s0 + n_scr]
        sems = refs[s0 + n_scr:]
        first = pl.program_id(0) == 0
        last = pl.program_id(0) == grid[0] - 1
        for d in range(1, len(grid)):
            first = jnp.logical_and(first, pl.program_id(d) == 0)
            last = jnp.logical_and(last, pl.program_id(d) == grid[d] - 1)

        @pl.when(first)
        def _():
            _start_all(*carry.copies(ro_refs, rw_refs, new_refs, *sems))

        body(*ins, *outs, *scr)

        @pl.when(last)
        def _():
            _wait_all(*carry.copies(ro_refs, rw_refs, new_refs, *sems))

    al = dict(aliases or {})
    al.update({n_in + n_ro + k: n_out + k for k in range(n_rw)})
    res = pl.pallas_call(
        wrapped, name=name, grid=grid, in_specs=in_specs + [_ANY] * (n_ro + n_rw), out_specs=out_specs + [_ANY] * (n_rw + n_new),
        out_shape=out_shape + carry.out_shape(), scratch_shapes=scratch + carry.sems(), input_output_aliases=al,
        compiler_params=_cparams(("arbitrary",) * len(grid)))(*args, *carry.ro, *carry.rw)
    return list(res[:n_out]), list(res[n_out:])


def _mm(name, pairs, *, ta=False, tb=False, M, N, tm, tn, tk, nk, epilogue, outs, extras=(), sum_pairs=True, carry=None,
        sub=None):
    ni, nj = M // tm, N // tn
    assert ni * tm == M and nj * tn == N
    n_p = len(pairs)
    n_acc = 1 if sum_pairs else n_p
    in_specs, args = [], []
    for a, b, ak, bk, bj in pairs:
        if ta:
            in_specs.append(pl.BlockSpec((tk, tm), lambda i, j, k, ak=ak: (k + ak, i)))
        else:
            in_specs.append(pl.BlockSpec((tm, tk), lambda i, j, k, ak=ak: (i, k + ak)))
        if tb:
            in_specs.append(pl.BlockSpec((tn, tk), lambda i, j, k, bk=bk, bj=bj: (j + bj, k + bk)))
        else:
            in_specs.append(pl.BlockSpec((tk, tn), lambda i, j, k, bk=bk, bj=bj: (k + bk, j + bj)))
        args += [a, b]
    for arr, kind, jo in extras:
        if kind == "mn":
            in_specs.append(pl.BlockSpec((tm, tn), lambda i, j, k, jo=jo: (i, j + jo)))
        else:
            in_specs.append(pl.BlockSpec((1, tn), lambda i, j, k, jo=jo: (0, j + jo)))
        args.append(arr)
    out_specs = [pl.BlockSpec((tm, tn), lambda i, j, k, jo=jo: (i, j + jo)) for _, _, jo in outs]
    out_shape = [jax.ShapeDtypeStruct((M, nc), dt) for nc, dt, _ in outs]
    n_ex, n_out = len(extras), len(outs)
    dn = (((0 if ta else 1,), (1 if tb else 0,)), ((), ()))

    def body(*refs):
        ab = refs[: 2 * n_p]
        ex = refs[2 * n_p: 2 * n_p + n_ex]
        o0 = 2 * n_p + n_ex
        out_refs = refs[o0: o0 + n_out]
        acc_refs = refs[o0 + n_out:]

        def dots():
            res = []
            for p in range(n_p):
                a = ab[2 * p][...]
                b = ab[2 * p + 1][...]
                res.append(lax.dot_general(a.astype(BF16), b.astype(BF16), dn, preferred_element_type=F32))
            if sum_pairs:
                tot = res[0]
                for r in res[1:]:
                    tot = tot + r
                res = [tot]
            return res

        def finish(accs):
            tiles = epilogue(accs, [e[...] for e in ex])
            for r, t in zip(out_refs, tiles, strict=True):
                r[...] = t.astype(r.dtype)

        if nk == 1 and sub is not None and tn > sub:
            for s in range(tn // sub):
                cs = slice(s * sub, (s + 1) * sub)
                res = []
                for p in range(n_p):
                    b = ab[2 * p + 1][cs, :] if tb else ab[2 * p + 1][:, cs]
                    res.append(lax.dot_general(ab[2 * p][...].astype(BF16), b.astype(BF16), dn, preferred_element_type=F32))
                if sum_pairs:
                    tot = res[0]
                    for r in res[1:]:
                        tot = tot + r
                    res = [tot]
                tiles = epilogue(res, [e[:, cs] for e in ex])
                for r, t in zip(out_refs, tiles, strict=True):
                    r[:, cs] = t.astype(r.dtype)
        elif nk == 1:
            finish(dots())
        else:
            k = pl.program_id(2)
            d = dots()

            @pl.when(k == 0)
            def _():
                for r, v in zip(acc_refs, d, strict=True):
                    r[...] = v

            @pl.when(k > 0)
            def _():
                for r, v in zip(acc_refs, d, strict=True):
                    r[...] += v

            @pl.when(k == nk - 1)
            def _():
                finish([r[...] for r in acc_refs])

    scratch = [pltpu.VMEM((tm, tn), F32) for _ in range(n_acc)] if nk > 1 else []
    return _call(body, name=name, grid=(ni, nj, nk), in_specs=in_specs, out_specs=out_specs, out_shape=out_shape, args=args,
                 scratch=scratch, sem=("parallel", "parallel", "arbitrary"), carry=carry)


def _wgrad(name, a, b, *, M, N, T, tm, tn, tk, into=None, joff=0, ncols=None, carry=None):
    ncols = N if ncols is None else ncols
    ni, nj, nk = M // tm, N // tn, T // tk
    dn = (((0,), (0,)), ((), ()))

    def body(a_ref, b_ref, *rest):
        d = lax.dot_general(a_ref[...].astype(BF16), b_ref[...].astype(BF16), dn, preferred_element_type=F32)
        if nk == 1:
            of_ref, oh_ref = rest[-2:]
            of_ref[...] = d
            oh_ref[...] = d.astype(BF16)
            return
        of_ref, oh_ref, acc_ref = rest[-3:]
        k = pl.program_id(2)

        @pl.when(k == 0)
        def _():
            acc_ref[...] = d

        @pl.when(k > 0)
        def _():
            acc_ref[...] += d

        @pl.when(k == nk - 1)
        def _():
            of_ref[...] = acc_ref[...]
            oh_ref[...] = acc_ref[...].astype(BF16)

    ospec = pl.BlockSpec((tm, tn), lambda i, j, k: (i, j + joff))
    in_specs = [pl.BlockSpec((tk, tm), lambda i, j, k: (k, i)), pl.BlockSpec((tk, tn), lambda i, j, k: (k, j))]
    args, aliases = [a, b], None
    if into is not None:
        in_specs += [_ANY, _ANY]
        args += list(into)
        aliases = {2: 0, 3: 1}
    return _call(body, name=name, grid=(ni, nj, nk), in_specs=in_specs, out_specs=[ospec, ospec],
                 out_shape=[jax.ShapeDtypeStruct((M, ncols), F32), jax.ShapeDtypeStruct((M, ncols), BF16)], args=args,
                 scratch=[pltpu.VMEM((tm, tn), F32)] if nk > 1 else [], sem=("parallel", "parallel", "arbitrary"), carry=carry,
                 aliases=aliases)


def _rows(tm, c, cb=0):
    return pl.BlockSpec((tm, c), lambda i, cb=cb: (i, cb))


def _whole(shape):
    nd = len(shape)
    return pl.BlockSpec(shape, lambda i, nd=nd: (0,) * nd)


def _cast_bf16(name, x, tm, carry=None):
    t, d = x.shape

    def body(x_ref, o_ref):
        o_ref[...] = x_ref[...].astype(BF16)

    return _call(body, name=name, grid=(t // tm,), in_specs=[_rows(tm, d)], out_specs=[_rows(tm, d)],
                 out_shape=[jax.ShapeDtypeStruct((t, d), BF16)], args=[x], sem=("parallel",), carry=carry)


def _residual(x_ref, prev_refs):
    if not prev_refs:
        return x_ref[...]
    return x_ref[...] * prev_refs[0][...] + prev_refs[1][...]


def _ln_fwd(name, xres, f, g, b, cf, tm, carry=None, prev=()):
    t, d = xres.shape
    n_prev = len(prev)

    def body(x_ref, f_ref, g_ref, b_ref, *rest):
        yb_ref, xh_ref, rs_ref = rest[n_prev:]
        r = ALPHA * _residual(x_ref, rest[:n_prev]) + cf * f_ref[...]
        xhat, rstd = _row_stats(r)
        yb_ref[...] = (xhat * g_ref[...] + b_ref[...]).astype(BF16)
        xh_ref[...] = xhat
        rs_ref[...] = rstd

    return _call(
        body, name=name, grid=(t // tm,),
        in_specs=[_rows(tm, d), _rows(tm, d), _whole((1, d)), _whole((1, d))] + [_whole((1, d))] * n_prev,
        out_specs=[_rows(tm, d), _rows(tm, d), _rows(tm, 1)],
        out_shape=[jax.ShapeDtypeStruct((t, d), BF16), jax.ShapeDtypeStruct((t, d), F32), jax.ShapeDtypeStruct((t, 1), F32)],
        args=[xres, f, g, b, *prev], sem=("parallel",), carry=carry)


def _ln_bwd(name, dy, xhat, rstd, g, scale, tm, carry=None):
    t, d = dy.shape

    def body(dy_ref, xh_ref, rs_ref, g_ref, dr_ref, drb_ref, dg_ref, db_ref):
        i = pl.program_id(0)
        dy_v, xh = dy_ref[...], xh_ref[...]
        dr = _ln_bwd_rows(dy_v, xh, rs_ref[...], g_ref[...])
        dr_ref[...] = dr
        drb_ref[...] = (scale * dr).astype(BF16)

        @pl.when(i == 0)
        def _():
            dg_ref[...] = jnp.zeros_like(dg_ref)
            db_ref[...] = jnp.zeros_like(db_ref)

        dg_ref[...] += _colsum(dy_v * xh)
        db_ref[...] += _colsum(dy_v)

    return _call(
        body, name=name, grid=(t // tm,), in_specs=[_rows(tm, d), _rows(tm, d), _rows(tm, 1), _whole((1, d))],
        out_specs=[_rows(tm, d), _rows(tm, d), _whole((1, d)), _whole((1, d))],
        out_shape=[jax.ShapeDtypeStruct((t, d), F32), jax.ShapeDtypeStruct((t, d), BF16),
                   jax.ShapeDtypeStruct((1, d), F32), jax.ShapeDtypeStruct((1, d), F32)],
        args=[dy, xhat, rstd, g], sem=("arbitrary",), carry=carry)


def _ln3_loss(name, xres, f, g, b, target, tm, prev):
    t, d = xres.shape

    def body(x_ref, f_ref, g_ref, b_ref, tg_ref, pg_ref, pb_ref, dr_ref, drb_ref, loss_ref, dg_ref, db_ref):
        i = pl.program_id(0)
        r = ALPHA * _residual(x_ref, (pg_ref, pb_ref)) + 0.5 * f_ref[...]
        xhat, rstd = _row_stats(r)
        gv = g_ref[...]
        y = xhat * gv + b_ref[...]
        err = y - tg_ref[...]
        dy = err * (1.0 / d)
        dr = _ln_bwd_rows(dy, xhat, rstd, gv)
        dr_ref[...] = dr
        drb_ref[...] = (0.5 * dr).astype(BF16)
        part = 0.5 * jnp.sum(jnp.mean(err * err, axis=-1, keepdims=True), axis=0, keepdims=True)

        @pl.when(i == 0)
        def _():
            loss_ref[...] = jnp.zeros_like(loss_ref)
            dg_ref[...] = jnp.zeros_like(dg_ref)
            db_ref[...] = jnp.zeros_like(db_ref)

        loss_ref[...] += jnp.broadcast_to(part, loss_ref.shape)
        dg_ref[...] += _colsum(dy * xhat)
        db_ref[...] += _colsum(dy)

    return _call(
        body, name=name, grid=(t // tm,),
        in_specs=[_rows(tm, d), _rows(tm, d), _whole((1, d)), _whole((1, d)), _rows(tm, d), _whole((1, d)), _whole((1, d))],
        out_specs=[_rows(tm, d), _rows(tm, d), _whole((8, 128)), _whole((1, d)), _whole((1, d))],
        out_shape=[jax.ShapeDtypeStruct((t, d), F32), jax.ShapeDtypeStruct((t, d), BF16),
                   jax.ShapeDtypeStruct((8, 128), F32), jax.ShapeDtypeStruct((1, d), F32),
                   jax.ShapeDtypeStruct((1, d), F32)],
        args=[xres, f, g, b, target, *prev], sem=("arbitrary",))


def _colsum_rows(name, x, tm):
    t, d = x.shape

    def body(x_ref, o_ref):
        @pl.when(pl.program_id(0) == 0)
        def _():
            o_ref[...] = jnp.zeros_like(o_ref)

        o_ref[...] += _colsum(x_ref[...].astype(F32))

    return _call(body, name=name, grid=(t // tm,), in_specs=[_rows(tm, d)], out_specs=[_whole((1, d))],
                 out_shape=[jax.ShapeDtypeStruct((1, d), F32)], args=[x], sem=("arbitrary",))[0]


def _sgu_mask():
    sh = SGU_CHUNK.bit_length() - 1
    r = lax.shift_right_logical(lax.broadcasted_iota(jnp.int32, (SGU_BLOCK, SGU_BLOCK), 0), sh)
    c = lax.shift_right_logical(lax.broadcasted_iota(jnp.int32, (SGU_BLOCK, SGU_BLOCK), 1), sh)
    return c <= r


def _sgu_fwd(name, p, lng, lnb, wm, bst, tm):
    t = p.shape[0]
    n_grp, w = wm.shape[0], lng.shape[1]
    hd = w // n_grp
    nblk = tm // SGU_BLOCK

    def body(uv_ref, h_ref, g_ref, b_ref, wm_ref, bs_ref, sa_ref, z_ref, vn_s):
        xhat, _ = _row_stats(_gelu(uv_ref[:, w:]))
        vn_s[...] = (xhat * g_ref[...] + b_ref[...]).astype(BF16)
        mask = _sgu_mask()
        for h in range(n_grp):
            wh = jnp.where(mask, wm_ref[h], 0.0).astype(BF16)
            bcol = bs_ref[:, h:h + 1]
            cs = slice(h * hd, (h + 1) * hd)
            for n in range(nblk):
                rs = slice(n * SGU_BLOCK, (n + 1) * SGU_BLOCK)
                s = jnp.dot(wh, vn_s[rs, cs], preferred_element_type=F32) + bcol
                sa_ref[rs, cs] = (_gelu(uv_ref[rs, cs]) * s).astype(BF16)
        z_ref[...] = h_ref[:, :w] * jax.nn.sigmoid(h_ref[:, w:])

    return _call(
        body, name=name, grid=(t // tm,),
        in_specs=[_rows(tm, 2 * w, 0), _rows(tm, 2 * w, 1), _whole((1, w)), _whole((1, w)), _whole(wm.shape),
                  _whole(bst.shape)],
        out_specs=[_rows(tm, w), _rows(tm, w)],
        out_shape=[jax.ShapeDtypeStruct((t, w), BF16), jax.ShapeDtypeStruct((t, w), F32)],
        args=[p, p, lng, lnb, wm, bst], scratch=[pltpu.VMEM((tm, w), BF16)], sem=("parallel",))


def _sgu_bwd(name, p, dsa, lng, lnb, wm, bst, tm):
    t = p.shape[0]
    n_grp, w = wm.shape[0], lng.shape[1]
    hd = w // n_grp
    nblk = tm // SGU_BLOCK

    def body(uv_ref, dsa_ref, g_ref, b_ref, wm_ref, bs_ref, dp_ref, dwm_ref, dbs_ref, dg_ref, db_ref, dbin_ref,
             vn_s, ug_s, dvn_s, dug_s):
        i = pl.program_id(0)

        @pl.when(i == 0)
        def _():
            dwm_ref[...] = jnp.zeros_like(dwm_ref)
            dbs_ref[...] = jnp.zeros_like(dbs_ref)
            dg_ref[...] = jnp.zeros_like(dg_ref)
            db_ref[...] = jnp.zeros_like(db_ref)
            dbin_ref[...] = jnp.zeros_like(dbin_ref)

        ug, dgelu_u = _gelu_and_grad(uv_ref[:, :w])
        ug_s[...] = ug
        vg, dgelu_v = _gelu_and_grad(uv_ref[:, w:])
        xhat, rstd = _row_stats(vg)
        gv = g_ref[...]
        vn_s[...] = (xhat * gv + b_ref[...]).astype(BF16)
        mask = _sgu_mask()
        for h in range(n_grp):
            wh = jnp.where(mask, wm_ref[h], 0.0).astype(BF16)
            bcol = bs_ref[:, h:h + 1]
            cs = slice(h * hd, (h + 1) * hd)
            dw_h = jnp.zeros((SGU_BLOCK, SGU_BLOCK), F32)
            dbs_h = jnp.zeros((SGU_BLOCK, 1), F32)
            for n in range(nblk):
                rs = slice(n * SGU_BLOCK, (n + 1) * SGU_BLOCK)
                vblk = vn_s[rs, cs]
                s = jnp.dot(wh, vblk, preferred_element_type=F32) + bcol
                dsa_blk = dsa_ref[rs, cs]
                dug_s[rs, cs] = dsa_blk * s
                ds = dsa_blk * ug_s[rs, cs]
                dsb = ds.astype(BF16)
                dvn_s[rs, cs] = lax.dot_general(wh, dsb, (((0,), (0,)), ((), ())), preferred_element_type=F32)
                dw_h = dw_h + lax.dot_general(dsb, vblk, (((1,), (1,)), ((), ())), preferred_element_type=F32)
                dbs_h = dbs_h + jnp.sum(ds, axis=1, keepdims=True)
            dwm_ref[h] += jnp.where(mask, dw_h, 0.0)
            dbs_ref[:, h:h + 1] += dbs_h
        dvn = dvn_s[...]
        dg_ref[...] += _colsum(dvn * xhat)
        db_ref[...] += _colsum(dvn)
        dvg = _ln_bwd_rows(dvn, xhat, rstd, gv)
        du = dug_s[...] * dgelu_u
        dv = dvg * dgelu_v
        dp_ref[:, :w] = du.astype(BF16)
        dp_ref[:, w:] = dv.astype(BF16)
        dbin_ref[:, :w] += _colsum(du)
        dbin_ref[:, w:] += _colsum(dv)

    return _call(
        body, name=name, grid=(t // tm,),
        in_specs=[_rows(tm, 2 * w, 0), _rows(tm, w), _whole((1, w)), _whole((1, w)), _whole(wm.shape), _whole(bst.shape)],
        out_specs=[_rows(tm, 2 * w), _whole(wm.shape), _whole(bst.shape), _whole((1, w)), _whole((1, w)), _whole((1, 2 * w))],
        out_shape=[jax.ShapeDtypeStruct((t, 2 * w), BF16), jax.ShapeDtypeStruct(wm.shape, F32),
                   jax.ShapeDtypeStruct(bst.shape, F32), jax.ShapeDtypeStruct((1, w), F32), jax.ShapeDtypeStruct((1, w), F32),
                   jax.ShapeDtypeStruct((1, 2 * w), F32)],
        args=[p, dsa, lng, lnb, wm, bst],
        scratch=[pltpu.VMEM((tm, w), BF16), pltpu.VMEM((tm, w), F32), pltpu.VMEM((tm, w), F32), pltpu.VMEM((tm, w), F32)],
        sem=("arbitrary",))


def _halo_prev(tm, c):
    return pl.BlockSpec((HALO, c), lambda i: (jnp.maximum(i * (tm // HALO) - 1, 0), 0))


def _halo_next(tm, c, t):
    last = t // HALO - 1
    return pl.BlockSpec((HALO, c), lambda i: (jnp.minimum((i + 1) * (tm // HALO), last), 0))


def _shifted_copies(sh, n):
    for r in range(1, SUBLANES):
        sh[r, :n - SUBLANES, :] = sh[0, r:r + n - SUBLANES, :]


def _row_broadcasts(wb, w_ref, kw):
    for k in range(kw):
        wb[k] = jnp.broadcast_to(w_ref[k:k + 1, :], wb.shape[1:])


def _tap(sh, r0, o, rows, cols):
    return sh[o % SUBLANES, pl.ds(pl.multiple_of(r0 + (o - o % SUBLANES), SUBLANES), rows), cols]


def _conv_fwd(name, z, wdw, kw, bdw, lng, lnb, tm, carry=None):
    t, c = z.shape
    lead = HALO - (kw - 1)
    n = tm + HALO

    def body(zp_ref, z_ref, w_ref, bdw_ref, g_ref, b_ref, zc_ref, sb_ref, sh, wb):
        i = pl.program_id(0)
        sh[0, :HALO, :] = jnp.where(i > 0, zp_ref[...], 0.0)
        sh[0, HALO:, :] = z_ref[...]
        _shifted_copies(sh, n)
        _row_broadcasts(wb, w_ref, kw)
        bias = jnp.broadcast_to(bdw_ref[...], (SUBLANES, c))
        groups = CONV_ROWS // SUBLANES

        def chunk(ci, _):
            r0 = pl.multiple_of(ci * CONV_ROWS, CONV_ROWS)
            accs = [bias] * groups
            for k in range(kw):
                wk = wb[k]
                tp = _tap(sh, r0, lead + k, CONV_ROWS, slice(None))
                accs = [accs[g] + wk * tp[g * SUBLANES:(g + 1) * SUBLANES] for g in range(groups)]
            zc_ref[pl.ds(r0, CONV_ROWS), :] = jnp.concatenate(accs, axis=0)
            return 0

        lax.fori_loop(0, tm // CONV_ROWS, chunk, 0)
        xhat, _ = _row_stats(zc_ref[...])
        zn = xhat * g_ref[...] + b_ref[...]
        sb_ref[...] = (zn * jax.nn.sigmoid(zn)).astype(BF16)

    return _call(
        body, name=name, grid=(t // tm,),
        in_specs=[_halo_prev(tm, c), _rows(tm, c), _whole(wdw.shape), _whole((1, c)), _whole((1, c)), _whole((1, c))],
        out_specs=[_rows(tm, c), _rows(tm, c)],
        out_shape=[jax.ShapeDtypeStruct((t, c), F32), jax.ShapeDtypeStruct((t, c), BF16)],
        args=[z, z, wdw, bdw, lng, lnb], scratch=[pltpu.VMEM((SUBLANES, n, c), F32), pltpu.VMEM((HALO, SUBLANES, c), F32)],
        sem=("parallel",), carry=carry)


def _conv_ln_bwd(name, dsb, zc, lng, lnb, tm):
    t, c = zc.shape

    def body(dsb_ref, zc_ref, g_ref, b_ref, dzc_ref, dg_ref, db_ref, dbdw_ref):
        i = pl.program_id(0)
        xhat, rstd = _row_stats(zc_ref[...])
        gv = g_ref[...]
        zn = xhat * gv + b_ref[...]
        dzn = dsb_ref[...] * _silu_grad(zn)
        dzc = _ln_bwd_rows(dzn, xhat, rstd, gv)
        dzc_ref[...] = dzc

        @pl.when(i == 0)
        def _():
            dg_ref[...] = jnp.zeros_like(dg_ref)
            db_ref[...] = jnp.zeros_like(db_ref)
            dbdw_ref[...] = jnp.zeros_like(dbdw_ref)

        dg_ref[...] += _colsum(dzn * xhat)
        db_ref[...] += _colsum(dzn)
        dbdw_ref[...] += _colsum(dzc)

    return _call(
        body, name=name, grid=(t // tm,), in_specs=[_rows(tm, c), _rows(tm, c), _whole((1, c)), _whole((1, c))],
        out_specs=[_rows(tm, c), _whole((1, c)), _whole((1, c)), _whole((1, c))],
        out_shape=[jax.ShapeDtypeStruct((t, c), F32)] + [jax.ShapeDtypeStruct((1, c), F32)] * 3,
        args=[dsb, zc, lng, lnb], sem=("arbitrary",))


def _conv_bwd(name, dzc, z, p, wdw, kw, tm):
    t, c = z.shape
    n_i = t // tm
    n = tm + HALO

    def body(dzc_ref, dzn_ref, z_ref, h_ref, w_ref, dp_ref, dw_ref, dbin_ref, sh, dz_s, wb):
        i = pl.program_id(0)

        @pl.when(i == 0)
        def _():
            dw_ref[...] = jnp.zeros_like(dw_ref)

        sh[0, :tm, :] = dzc_ref[...]
        sh[0, tm:, :] = jnp.where(i < n_i - 1, dzn_ref[...], 0.0)
        _shifted_copies(sh, n)
        _row_broadcasts(wb, w_ref, kw)
        nv = 4
        rows = nv * SUBLANES
        ways = 2
        for lc in range(c // LANES):
            cols = slice(lc * LANES, (lc + 1) * LANES)

            def rowv(rv, accs, cols=cols):
                r0 = pl.multiple_of(rv * rows, rows)
                zv = z_ref[pl.ds(r0, rows), cols]
                zs = [zv[v * SUBLANES:(v + 1) * SUBLANES] for v in range(nv)]
                dz = [[None] * ways for _ in range(nv)]
                new = []
                for k in range(kw):
                    s = _tap(sh, r0, kw - 1 - k, rows, cols)
                    wk = wb[k, :, cols]
                    j = k % ways
                    acc = accs[k]
                    for v in range(nv):
                        sv = s[v * SUBLANES:(v + 1) * SUBLANES]
                        dz[v][j] = wk * sv if dz[v][j] is None else dz[v][j] + wk * sv
                        acc = acc + zs[v] * sv
                    new.append(acc)
                dz_s[pl.ds(r0, rows), cols] = jnp.concatenate([dz[v][0] + dz[v][1] for v in range(nv)], axis=0)
                return tuple(new)

            accs = lax.fori_loop(0, tm // rows, rowv, tuple(jnp.zeros((SUBLANES, LANES), F32) for _ in range(kw)))
            for k in range(kw):
                dw_ref[k:k + 1, cols] += _colsum(accs[k])
        dz = dz_s[...]
        a, g = h_ref[:, :c], h_ref[:, c:]
        sg = jax.nn.sigmoid(g)
        da = dz * sg
        dg = dz * a * (sg * (1.0 - sg))
        dp_ref[:, :c] = da.astype(BF16)
        dp_ref[:, c:] = dg.astype(BF16)

        @pl.when(i == 0)
        def _():
            dbin_ref[...] = jnp.zeros_like(dbin_ref)

        dbin_ref[:, :c] += _colsum(da)
        dbin_ref[:, c:] += _colsum(dg)

    return _call(
        body, name=name, grid=(n_i,),
        in_specs=[_rows(tm, c), _halo_next(tm, c, t), _rows(tm, c), _rows(tm, 2 * c, 1), _whole(wdw.shape)],
        out_specs=[_rows(tm, 2 * c), _whole((HALO, c)), _whole((1, 2 * c))],
        out_shape=[jax.ShapeDtypeStruct((t, 2 * c), BF16), jax.ShapeDtypeStruct((HALO, c), F32),
                   jax.ShapeDtypeStruct((1, 2 * c), F32)],
        args=[dzc, dzc, z, p, wdw],
        scratch=[pltpu.VMEM((SUBLANES, n, c), F32), pltpu.VMEM((tm, c), F32), pltpu.VMEM((HALO, SUBLANES, c), F32)],
        sem=("arbitrary",))


def _adamw(name, w, g, m, v, tr, carry=None):
    r, c = w.shape
    c1 = 1.0 - ADAM_B1 ** ADAM_STEP
    c2 = 1.0 - ADAM_B2 ** ADAM_STEP

    def body(w_ref, g_ref, m_ref, v_ref, go_ref, d_ref, mo_ref, vo_ref):
        gv = g_ref[...]
        mn = ADAM_B1 * m_ref[...] + (1.0 - ADAM_B1) * gv
        vn = ADAM_B2 * v_ref[...] + (1.0 - ADAM_B2) * (gv * gv)
        go_ref[...] = gv
        d_ref[...] = -ADAM_LR * ((mn / c1) / (jnp.sqrt(vn / c2) + ADAM_EPS) + ADAM_WD * w_ref[...])
        mo_ref[...] = mn
        vo_ref[...] = vn

    spec = _rows(tr, c)
    return _call(body, name=name, grid=(r // tr,), in_specs=[spec] * 4, out_specs=[spec] * 4,
                 out_shape=[jax.ShapeDtypeStruct((r, c), F32)] * 4, args=[w, g, m, v], sem=("parallel",), carry=carry)


def _pair_sum(name, mine, recv, hc, *, out_rows, tr, row_map):
    c_ = mine.shape[1]

    def body(hc_ref, a_ref, b_ref, oh_ref):
        oh_ref[...] = (a_ref[...] + b_ref[...].astype(F32)).astype(BF16)

    ispec = pl.BlockSpec((tr, c_), lambda i, hc_ref: (row_map(i, hc_ref[0]), 0))
    ospec = pl.BlockSpec((tr, c_), lambda i, hc_ref: (i, 0))
    return pl.pallas_call(
        body, name=name,
        grid_spec=pltpu.PrefetchScalarGridSpec(num_scalar_prefetch=1, grid=(out_rows // tr,), in_specs=[ispec, ispec],
                                               out_specs=ospec),
        out_shape=jax.ShapeDtypeStruct((out_rows, c_), BF16), compiler_params=_cparams(("parallel",)))(hc, mine, recv)


def _final_sum(name, mine, recv1, recv2, qc, *, part_shape, out_shape, tr, in_map, out_map):
    pr, pc = part_shape

    def body(qc_ref, a_ref, b_ref, r_ref, o_ref):
        own = a_ref[...] + b_ref[...].astype(F32)
        o_ref[...] = ((own + r_ref[0].astype(F32)) + r_ref[1].astype(F32)) + r_ref[2].astype(F32)

    ispec = pl.BlockSpec((tr, pc), lambda i, qc_ref: in_map(i, qc_ref[0], qc_ref[1]))
    return pl.pallas_call(
        body, name=name,
        grid_spec=pltpu.PrefetchScalarGridSpec(
            num_scalar_prefetch=1, grid=(pr // tr,),
            in_specs=[ispec, ispec, pl.BlockSpec((3, tr, pc), lambda i, qc_ref: (0, i, 0))],
            out_specs=pl.BlockSpec((tr, pc), lambda i, qc_ref: out_map(i, qc_ref[0], qc_ref[1]))),
        out_shape=jax.ShapeDtypeStruct(out_shape, F32), compiler_params=_cparams(("parallel",)))(qc, mine, recv1, recv2)


def _sum8(name, slots):
    def body(s_ref, o_ref):
        acc = s_ref[0]
        for d in range(1, 8):
            acc = acc + s_ref[d]
        o_ref[...] = acc

    return pl.pallas_call(body, name=name, out_shape=jax.ShapeDtypeStruct(slots.shape[1:], F32),
                          in_specs=[pl.BlockSpec(memory_space=pltpu.VMEM)], out_specs=pl.BlockSpec(memory_space=pltpu.VMEM),
                          compiler_params=pltpu.CompilerParams(vmem_limit_bytes=VMEM_LIMIT))(slots)


def _chips(x, y):
    return [(1 - x, y), (x, 1 - y), (1 - x, 1 - y)]


class _Big:
    def __init__(self, name, w, m, v, ax):
        self.name, self.w, self.m, self.v, self.ax = name, w, m, v, ax
        sr, sc = w.shape
        self.R, self.C = (sr * N_CHIPS, sc) if ax == 0 else (sr, sc * N_CHIPS)
        self.sr, self.sc = sr, sc
        self.hr = sr // 2

    def slot(self, ref, q, h=None):
        if self.ax == 1:
            cols = pl.ds(pl.multiple_of(q * self.sc, 128), self.sc)
            return ref.at[:, cols] if h is None else ref.at[pl.ds(pl.multiple_of(h * self.hr, 16), self.hr), cols]
        if h is None:
            return ref.at[pl.ds(pl.multiple_of(q * self.sr, 16), self.sr), :]
        return ref.at[pl.ds(pl.multiple_of(q * self.sr + h * self.hr, 16), self.hr), :]

    def half(self, ref, h):
        return ref.at[pl.ds(pl.multiple_of(h * self.hr, 16), self.hr), :]

    def part(self, ref, q):
        if self.ax == 1:
            return ref.at[:, pl.ds(pl.multiple_of(q * self.sc, 128), self.sc)]
        return ref.at[pl.ds(pl.multiple_of(q * self.hr, 16), self.hr), :]

    @property
    def part_shape(self):
        return (self.hr, self.sc)

    def cast_into_full(self, qc):
        tr = _tile_rows(self.sr, self.sc)
        nb = self.sr // tr

        def body(qc_ref, x_ref, o_ref):
            o_ref[...] = x_ref[...].astype(BF16)

        if self.ax == 1:
            ospec = pl.BlockSpec((tr, self.sc), lambda i, qc_ref: (i, qc_ref[0]))
        else:
            ospec = pl.BlockSpec((tr, self.sc), lambda i, qc_ref: (qc_ref[0] * nb + i, 0))
        return pl.pallas_call(
            body, name=f"cast_{self.name}",
            grid_spec=pltpu.PrefetchScalarGridSpec(num_scalar_prefetch=1, grid=(nb,),
                                                   in_specs=[pl.BlockSpec((tr, self.sc), lambda i, qc_ref: (i, 0))],
                                                   out_specs=ospec),
            out_shape=jax.ShapeDtypeStruct((self.R, self.C), BF16), compiler_params=_cparams(("parallel",)))(qc, self.w)

    def gather_ici(self, full, piece=(0, 1)):
        k, n = piece
        pr = self.hr // n
        assert pr * n == self.hr and pr % 16 == 0

        def plan(ro, rw, new, x, y, c):
            q = 2 * x + y
            r0 = c * self.hr + k * pr
            if self.ax == 1:
                mine = rw[0].at[pl.ds(pl.multiple_of(r0, 16), pr), pl.ds(pl.multiple_of(q * self.sc, 128), self.sc)]
            else:
                mine = rw[0].at[pl.ds(pl.multiple_of(q * self.sr + r0, 16), pr), :]
            return [(mine, mine, (cx, cy, c)) for cx, cy in _chips(x, y)], []

        return _Carry([], [full], [], plan, 3, 0)

    def gather_d2d(self, full):
        def plan(ro, rw, new, x, y, c):
            remote = []
            for cx, cy in _chips(x, y):
                piece = self.slot(rw[0], 2 * cx + cy, c)
                remote.append((piece, piece, (x, y, 1 - c)))
            return remote, []

        return _Carry([], [full], [], plan, 3)

    def rs_pair(self, g16):
        def plan(ro, rw, new, x, y, c):
            sib = (x, y, 1 - c)
            if self.ax == 1:
                rows = pl.ds(pl.multiple_of((1 - c) * self.hr, 16), self.hr)
                return [(ro[0].at[rows, :], new[0].at[rows, :], sib)], []
            return [(self.slot(ro[0], q, 1 - c), self.slot(new[0], q, 1 - c), sib) for q in range(N_CHIPS)], []

        return _Carry([g16], [], [jax.ShapeDtypeStruct((self.R, self.C), BF16)], plan, 1 if self.ax == 1 else N_CHIPS)

    def rs_pairsum(self, tag, g32, recv1, hc):
        tr = _tile_rows(self.hr, self.C)
        nb = self.hr // tr
        if self.ax == 1:
            row_map = lambda i, c: c * nb + i
        else:
            row_map = lambda i, c: ((i // nb) * 2 + c) * nb + i % nb
        return _pair_sum(f"rs_pairsum_{tag}", g32, recv1, hc, out_rows=self.R // 2, tr=tr, row_map=row_map)

    def rs_ici(self, cs16):
        def plan(ro, rw, new, x, y, c):
            return [(self.part(ro[0], 2 * cx + cy), new[0].at[k], (cx, cy, c)) for k, (cx, cy) in enumerate(_chips(x, y))], []

        return _Carry([cs16], [], [jax.ShapeDtypeStruct((3,) + self.part_shape, BF16)], plan, 3)

    def rs_final(self, tag, g32, recv1, recv2, qc):
        tr = _tile_rows(self.hr, self.sc)
        nb = self.hr // tr
        if self.ax == 1:
            in_map = lambda i, q, c: (c * nb + i, q)
        else:
            in_map = lambda i, q, c: ((q * 2 + c) * nb + i, 0)
        out_map = lambda i, q, c: (c * nb + i, 0)
        return _final_sum(f"rs_final_{tag}", g32, recv1, recv2, qc, part_shape=self.part_shape, out_shape=(self.sr, self.sc),
                          tr=tr, in_map=in_map, out_map=out_map)

    def rs_share(self, ghalf):
        def plan(ro, rw, new, x, y, c):
            piece = self.half(rw[0], c)
            return [(piece, piece, (x, y, 1 - c))], []

        return _Carry([], [ghalf], [], plan, 1)


def _small_allgather(packed):
    nr = packed.shape[0]

    def plan(ro, rw, new, x, y, c):
        me = 4 * x + 2 * y + c
        remote = []
        for fx in (0, 1):
            for fy in (0, 1):
                for fc in (0, 1):
                    if fx or fy or fc:
                        dev = (1 - x if fx else x, 1 - y if fy else y, 1 - c if fc else c)
                        remote.append((ro[0], new[0].at[me], dev))
        return remote, [(ro[0], new[0].at[me])]

    return _Carry([packed], [], [jax.ShapeDtypeStruct((8, nr, 128), F32)], plan, 7, 1)


def _conv_w_allgather(padded, cs):
    def plan(ro, rw, new, x, y, c):
        cols = pl.ds(pl.multiple_of((2 * x + y) * cs, 128), cs)
        remote = [(ro[0], new[0].at[:, cols], (cx, cy, c)) for cx, cy in _chips(x, y)]
        return remote, [(ro[0], new[0].at[:, cols])]

    return _Carry([padded], [], [jax.ShapeDtypeStruct((HALO, cs * N_CHIPS), F32)], plan, 3, 1)


def _pick(n, want):
    if n <= want:
        return n
    for t in range(want, 15, -16):
        if t % 16 == 0 and n % t == 0:
            return t
    raise ValueError(f"no tile for {n} (want {want})")


def _tile_rows(nrows, ncols, budget=2 * 1024 * 1024):
    return _pick(nrows, max(16, (budget // (4 * ncols)) // 16 * 16))


def _pick128(n, want):
    if n <= want:
        return n
    for t in range(want, 127, -128):
        if n % t == 0:
            return t
    raise ValueError(f"no lane tile for {n} (want {want})")


def _pack_rows(parts):
    out, spans, r0 = [], [], 0
    for p in parts:
        flat = p.reshape(-1).astype(F32)
        n = flat.shape[0]
        rows = -(-n // 1024) * 8
        flat = jnp.pad(flat, (0, rows * 128 - n))
        out.append(flat.reshape(rows, 128))
        spans.append((r0, rows, n))
        r0 += rows
    return jnp.concatenate(out, axis=0), spans


def _unpack_rows(packed, spans, shapes):
    res = []
    for (r0, rows, n), shp in zip(spans, shapes, strict=True):
        res.append(packed[r0:r0 + rows].reshape(-1)[:n].reshape(shp))
    return res


def _ident(accs, ex):
    return [accs[0]]


def kernel(x, ffn1_w_gu, ffn1_w_down, ln1_g, ln1_b, w_in, b_in, sgu_ln_g, sgu_ln_b, sgu_w_s, sgu_b_s, w_a_proj, conv_w_dw, conv_b_dw, conv_ln_g, conv_ln_b, w_b_proj, w_out, ln2_g, ln2_b, ffn2_w_gu, ffn2_w_down, ln3_g, ln3_b, loss_target, m_ffn1_w_gu, m_ffn1_w_down, m_ln1_g, m_ln1_b, m_w_in, m_b_in, m_sgu_ln_g, m_sgu_ln_b, m_sgu_w_s, m_sgu_b_s, m_w_a_proj, m_conv_w_dw, m_conv_b_dw, m_conv_ln_g, m_conv_ln_b, m_w_b_proj, m_w_out, m_ln2_g, m_ln2_b, m_ffn2_w_gu, m_ffn2_w_down, m_ln3_g, m_ln3_b, v_ffn1_w_gu, v_ffn1_w_down, v_ln1_g, v_ln1_b, v_w_in, v_b_in, v_sgu_ln_g, v_sgu_ln_b, v_sgu_w_s, v_sgu_b_s, v_w_a_proj, v_conv_w_dw, v_conv_b_dw, v_conv_ln_g, v_conv_ln_b, v_w_b_proj, v_w_out, v_ln2_g, v_ln2_b, v_ffn2_w_gu, v_ffn2_w_down, v_ln3_g, v_ln3_b):
    args = dict(locals())
    assert x.shape[0] == 1 and ffn1_w_gu.shape[0] == 1
    T, D = x.shape[1], x.shape[2]
    F = ffn1_w_down.shape[1] * N_CHIPS
    W = sgu_ln_g.shape[1]
    KW = conv_w_dw.shape[1]
    assert KW - 1 <= HALO and T % SGU_BLOCK == 0

    mx, my, mc = lax.axis_index("x"), lax.axis_index("y"), lax.axis_index("c")
    q = 2 * mx + my
    hc = jnp.reshape(mc, (1,)).astype(jnp.int32)
    qc = jnp.stack([q, mc]).astype(jnp.int32)

    big_names = [("ffn1_w_gu", 1), ("ffn1_w_down", 0), ("w_in", 1), ("w_a_proj", 1), ("w_b_proj", 1), ("w_out", 0),
                 ("ffn2_w_gu", 1), ("ffn2_w_down", 0)]
    B = {n: _Big(n, args[n][0], args["m_" + n][0], args["v_" + n][0], ax) for n, ax in big_names}
    own = {n: b.cast_into_full(qc) for n, b in B.items()}

    x2d = x[0]
    tgt = loss_target[0]
    tm_r = _pick(T, 256)
    tm_ln = _pick(T, 512)
    tm = _pick(T, 1024)
    tm_h = _pick(T, 512)
    tn = _pick128(D, 1024)
    nj = D // tn
    tng = _pick128(D, 512)
    njg = D // tng
    tnf = _pick128(F, 512)
    nf = F // tnf
    tnw = _pick128(W, 1024)
    tnd = _pick128(D, 512)
    tmw = _pick128(W, 512)
    SUB = 256

    def ffn_up(tag, xb_, wgu, carry=None):
        def epi(accs, ex):
            g, u = accs
            s = jax.nn.sigmoid(g)
            sg = g * s
            return [u * (s * (1.0 + g * (1.0 - s))), sg, sg * u]

        return _mm(f"{tag}_up", [(xb_, wgu, 0, 0, 0), (xb_, wgu, 0, 0, nf)], M=T, N=F, tm=tm, tn=tnf, tk=D, nk=1, epilogue=epi,
                   outs=[(F, BF16, 0)] * 3, sum_pairs=False, carry=carry, sub=SUB)

    def ffn_down(tag, act, wd, carry=None):
        return _mm(f"{tag}_down", [(act, wd, 0, 0, 0)], M=T, N=D, tm=tm, tn=_pick128(D, 512), tk=F, nk=1, epilogue=_ident,
                   outs=[(D, F32, 0)], carry=carry)

    def ffn_dact(tag, drh, wd, dgate_f, dup_f, carry=None):
        def epi(accs, ex):
            da = accs[0]
            return [da * ex[0].astype(F32), da * ex[1].astype(F32)]

        return _mm(f"{tag}_dact", [(drh, wd, 0, 0, 0)], tb=True, M=T, N=F, tm=tm, tn=tnf, tk=D, nk=1, epilogue=epi,
                   outs=[(F, BF16, 0)] * 2, extras=[(dgate_f, "mn", 0), (dup_f, "mn", 0)], carry=carry, sub=SUB)

    def ffn_dwdown(tag, act, drh, carry=None):
        return _wgrad(f"{tag}_dwdown", act, drh, M=F, N=D, T=T, tm=tnf, tn=tnd, tk=T, carry=carry)

    def ffn_dwgate(tag, xb_, dg, carry=None):
        return _wgrad(f"{tag}_dwgate", xb_, dg, M=D, N=F, T=T, tm=tnd, tn=tnf, tk=T, ncols=2 * F, carry=carry)

    def ffn_dwup(tag, xb_, du, into, carry=None):
        return _wgrad(f"{tag}_dwup", xb_, du, M=D, N=F, T=T, tm=tnd, tn=tnf, tk=T, ncols=2 * F, into=into, joff=nf, carry=carry)

    def ffn_dx(tag, parts, wgu, addends, carry=None):
        pairs = [(da, wgu, 0, 1 if which == "up" else 0, 0) for which, da in parts]

        def epi(accs, ex):
            tot = accs[0] + ALPHA * ex[0]
            for e in ex[1:]:
                tot = tot + e
            return [tot]

        return _mm(f"{tag}_dx_{'_'.join(w_ for w_, _ in parts)}", pairs, tb=True, M=T, N=D, tm=tm_h,
                   tn=_pick128(D, 512 // len(parts)), tk=F, nk=1, epilogue=epi if addends else _ident, outs=[(D, F32, 0)],
                   extras=[(a, "mn", 0) for a in addends], carry=carry)

    wdw_pad = jnp.pad(conv_w_dw[0], ((0, HALO - KW), (0, 0)))
    c0, un = _merge(B["ffn1_w_gu"].gather_ici(own["ffn1_w_gu"]), _conv_w_allgather(wdw_pad, conv_w_dw.shape[2]))
    (wgu1,), (wdw_full,) = un(_exchange("gather_first", c0))
    (xb,), (wgu1,) = _cast_bf16("cast_x", x2d, tm_r, carry=B["ffn1_w_gu"].gather_d2d(wgu1))

    c, un = _merge(B["ffn1_w_down"].gather_ici(own["ffn1_w_down"]), B["w_in"].gather_ici(own["w_in"]))
    (g1, u1, a1), co = ffn_up("ffn1", xb, wgu1, carry=c)
    (wd1,), (win,) = un(co)
    (wd1,) = _exchange("gather_d2d_ffn1_w_down", B["ffn1_w_down"].gather_d2d(wd1))
    b_gu2, b_d2 = B["ffn2_w_gu"], B["ffn2_w_down"]
    c, un = _merge(B["w_a_proj"].gather_ici(own["w_a_proj"]), B["w_b_proj"].gather_ici(own["w_b_proj"]),
                   B["w_out"].gather_ici(own["w_out"]), B["w_in"].gather_d2d(win), b_gu2.gather_ici(own["ffn2_w_gu"], (0, 4)))
    (fo1,), co = ffn_down("ffn1", a1, wd1, carry=c)
    (wa,), (wb,), (wout,), (win,), (wgu2,) = un(co)
    c, un = _merge(B["w_a_proj"].gather_d2d(wa), B["w_b_proj"].gather_d2d(wb), B["w_out"].gather_d2d(wout),
                   b_gu2.gather_ici(wgu2, (1, 4)))
    (x1b, xh1, rs1), co = _ln_fwd("ln1", x2d, fo1, ln1_g, ln1_b, 0.5, tm_ln, carry=c)
    (wa,), (wb,), (wout,), (wgu2,) = un(co)

    c, un = _merge(b_gu2.gather_ici(wgu2, (1, 2)), b_d2.gather_ici(own["ffn2_w_down"], (0, 2)))
    (proj,), co = _mm("in_proj", [(x1b, win, 0, 0, 0)], M=T, N=4 * D, tm=tm, tn=tn, tk=D, nk=1,
                      epilogue=lambda accs, ex: [accs[0] + ex[0]], outs=[(4 * D, F32, 0)], extras=[(b_in, "n", 0)], carry=c)
    (wgu2,), (wd2,) = un(co)
    wm = sgu_w_s[0]
    bst = sgu_b_s[0].T
    sa, z = _sgu_fwd("sgu_fwd", proj, sgu_ln_g, sgu_ln_b, wm, bst, tm_r)
    (zc, sb), (wd2,) = _conv_fwd("conv_fwd", z, wdw_full, KW, conv_b_dw, conv_ln_g, conv_ln_b, tm_r,
                                 carry=b_d2.gather_ici(wd2, (1, 2)))

    def epi_mix(accs, ex):
        ya_, yb_ = accs
        ga, gb = jax.nn.sigmoid(ex[0]), jax.nn.sigmoid(ex[1])
        return [ga * ya_ + gb * yb_, ga, gb, ya_ * (ga * (1.0 - ga)), yb_ * (gb * (1.0 - gb))]

    mixin, gate_a, gate_b, dlog_a, dlog_b = _mm(
        "branch_proj", [(sa, wa, 0, 0, 0), (sb, wb, 0, 0, 0)], M=T, N=D, tm=tm, tn=tng, tk=W, nk=1, epilogue=epi_mix,
        outs=[(D, BF16, 0)] * 5, sum_pairs=False, sub=SUB, extras=[(proj, "mn", 2 * njg), (proj, "mn", 3 * njg)])
    c, un = _merge(b_gu2.gather_d2d(wgu2), b_d2.gather_d2d(wd2))
    (mix,), co = _mm("out_proj", [(mixin, wout, 0, 0, 0)], M=T, N=D, tm=tm, tn=tn, tk=D, nk=1, epilogue=_ident,
                     outs=[(D, F32, 0)], carry=c)
    (wgu2,), (wd2,) = un(co)
    x2b, xh2, rs2 = _ln_fwd("ln2", xh1, mix, ln2_g, ln2_b, 1.0, tm_ln, prev=(ln1_g, ln1_b))
    g2, u2, a2 = ffn_up("ffn2", x2b, wgu2)
    (fo2,) = ffn_down("ffn2", a2, wd2)
    dr3, dr3h, loss_part, dln3_g, dln3_b = _ln3_loss("ln3_loss", xh2, fo2, ln3_g, ln3_b, tgt, tm_r, (ln2_g, ln2_b))

    b_gu2, b_d2 = B["ffn2_w_gu"], B["ffn2_w_down"]
    dg2, du2 = ffn_dact("ffn2", dr3h, wd2, g2, u2)
    dwd2 = ffn_dwdown("ffn2", a2, dr3h)
    dwgu2, (r1_d2,) = ffn_dwgate("ffn2", x2b, dg2, carry=b_d2.rs_pair(dwd2[1]))
    dwgu2 = ffn_dwup("ffn2", x2b, du2, dwgu2)
    cs_d2 = b_d2.rs_pairsum("ffn2_w_down", dwd2[0], r1_d2, hc)
    c, un = _merge(b_gu2.rs_pair(dwgu2[1]), b_d2.rs_ici(cs_d2))
    (dx2,), co = ffn_dx("ffn2", [("gate", dg2), ("up", du2)], wgu2, [dr3], carry=c)
    (r1_gu2,), (r2_d2,) = un(co)
    cs_gu2 = b_gu2.rs_pairsum("ffn2_w_gu", dwgu2[0], r1_gu2, hc)
    gh_d2 = b_d2.rs_final("ffn2_w_down", dwd2[0], r1_d2, r2_d2, qc)

    dr2, dr2b, dln2_g, dln2_b = _ln_bwd("ln2_bwd", dx2, xh2, rs2, ln2_g, 1.0, tm_ln)

    def epi_dmix(accs, ex):
        return [accs[0] * e.astype(F32) for e in ex]

    dya, dyb, dla, dlb = _mm("out_proj_bwd", [(dr2b, wout, 0, 0, 0)], tb=True, M=T, N=D, tm=tm, tn=tng, tk=D, nk=1,
                             epilogue=epi_dmix, outs=[(D, BF16, 0)] * 4, sub=SUB,
                             extras=[(gate_a, "mn", 0), (gate_b, "mn", 0), (dlog_a, "mn", 0), (dlog_b, "mn", 0)])
    dwout = _wgrad("dw_out", mixin, dr2b, M=D, N=D, T=T, tm=tnd, tn=tnd, tk=T)
    (dsa,) = _mm("a_proj_bwd", [(dya, wa, 0, 0, 0)], tb=True, M=T, N=W, tm=tm, tn=tnw, tk=D, nk=1, epilogue=_ident,
                 outs=[(W, F32, 0)])
    (dsb,) = _mm("b_proj_bwd", [(dyb, wb, 0, 0, 0)], tb=True, M=T, N=W, tm=tm, tn=tnw, tk=D, nk=1, epilogue=_ident,
                 outs=[(W, F32, 0)])
    dwa = _wgrad("dw_a_proj", sa, dya, M=W, N=D, T=T, tm=tmw, tn=tnd, tk=T)
    dwb = _wgrad("dw_b_proj", sb, dyb, M=W, N=D, T=T, tm=tmw, tn=tnd, tk=T)

    dpa, dwm, dbst, dsgu_g, dsgu_b, dbin_a = _sgu_bwd("sgu_bwd", proj, dsa, sgu_ln_g, sgu_ln_b, wm, bst, tm_r)
    dzc, dcln_g, dcln_b, dbdw = _conv_ln_bwd("conv_ln_bwd", dsb, zc, conv_ln_g, conv_ln_b, tm_r)
    dpb, dwdw, dbin_b = _conv_bwd("conv_bwd", dzc, z, proj, wdw_full, KW, tm_r)

    dps = [dpa, dpb, dla, dlb]
    db_in = jnp.concatenate([dbin_a, dbin_b, _colsum_rows("db_in_gate_a", dla, tm_r), _colsum_rows("db_in_gate_b", dlb, tm_r)],
                            axis=1)
    c, un = _merge(b_gu2.rs_ici(cs_gu2), b_d2.rs_share(gh_d2))
    (dx1,), co = _mm("in_proj_bwd", [(dp, win, 0, k, 0) for k, dp in enumerate(dps)], tb=True, M=T, N=D, tm=tm_h,
                     tn=tnd, tk=D, nk=1, epilogue=lambda accs, ex: [accs[0] + ALPHA * ex[0]], outs=[(D, F32, 0)],
                     extras=[(dr2, "mn", 0)], carry=c)
    (r2_gu2,), (g_d2,) = un(co)
    gh_gu2 = b_gu2.rs_final("ffn2_w_gu", dwgu2[0], r1_gu2, r2_gu2, qc)
    dwin = _wgrad("dw_in_0", x1b, dps[0], M=D, N=D, T=T, tm=tnd, tn=tnd, tk=T, ncols=4 * D)
    for k in range(1, 4):
        dwin = _wgrad(f"dw_in_{k}", x1b, dps[k], M=D, N=D, T=T, tm=tnd, tn=tnd, tk=T, ncols=4 * D, into=dwin,
                      joff=k * (D // tnd))

    mix_names = ["w_in", "w_a_proj", "w_b_proj", "w_out"]
    mix_grads = dict(zip(mix_names, [dwin, dwa, dwb, dwout], strict=True))
    c, un = _merge(*[B[n].rs_pair(mix_grads[n][1]) for n in mix_names], b_gu2.rs_share(gh_gu2))
    (dr1, dr1h, dln1_g, dln1_b), co = _ln_bwd("ln1_bwd", dx1, xh1, rs1, ln1_g, 0.5, tm_ln, carry=c)
    *r1s, (g_gu2,) = un(co)
    r1_mix = {n: r1 for n, (r1,) in zip(mix_names, r1s, strict=True)}
    cs_mix = {n: B[n].rs_pairsum(n, mix_grads[n][0], r1_mix[n], hc) for n in mix_names}

    small_names = ["ln1_g", "ln1_b", "b_in", "sgu_ln_g", "sgu_ln_b", "sgu_w_s", "sgu_b_s", "conv_w_dw", "conv_b_dw",
                   "conv_ln_g", "conv_ln_b", "ln2_g", "ln2_b", "ln3_g", "ln3_b"]
    small_parts = {"ln1_g": dln1_g, "ln1_b": dln1_b, "b_in": db_in, "sgu_ln_g": dsgu_g, "sgu_ln_b": dsgu_b, "sgu_w_s": dwm,
                   "sgu_b_s": dbst.T, "conv_w_dw": dwdw[:KW], "conv_b_dw": dbdw, "conv_ln_g": dcln_g, "conv_ln_b": dcln_b,
                   "ln2_g": dln2_g, "ln2_b": dln2_b, "ln3_g": dln3_g, "ln3_b": dln3_b}
    packed, spans = _pack_rows([small_parts[n] for n in small_names])

    b_gu1, b_d1 = B["ffn1_w_gu"], B["ffn1_w_down"]
    c, un = _merge(B["w_in"].rs_ici(cs_mix["w_in"]), _small_allgather(packed))
    (dg1, du1), co = ffn_dact("ffn1", dr1h, wd1, g1, u1, carry=c)
    (r2_win,), (small_slots,) = un(co)
    c, un = _merge(*[B[n].rs_ici(cs_mix[n]) for n in mix_names[1:]])
    dwgu1, co = ffn_dwgate("ffn1", xb, dg1, carry=c)
    r2_mix = [[r2_win]] + un(co)
    gh_mix = [B[n].rs_final(n, mix_grads[n][0], r1_mix[n], r2, qc) for n, (r2,) in zip(mix_names, r2_mix, strict=True)]
    dwgu1 = ffn_dwup("ffn1", xb, du1, dwgu1)
    c, un = _merge(*[B[n].rs_share(gh) for n, gh in zip(mix_names, gh_mix, strict=True)], b_gu1.rs_pair(dwgu1[1]))
    dwd1, co = ffn_dwdown("ffn1", a1, dr1h, carry=c)
    *g_mixs, (r1_gu1,) = un(co)
    g_mix = {n: g for n, (g,) in zip(mix_names, g_mixs, strict=True)}
    cs_gu1 = b_gu1.rs_pairsum("ffn1_w_gu", dwgu1[0], r1_gu1, hc)
    c, un = _merge(b_gu1.rs_ici(cs_gu1), b_d1.rs_pair(dwd1[1]))
    (dx_gate,), co = ffn_dx("ffn1", [("gate", dg1)], wgu1, [], carry=c)
    (r2_gu1,), (r1_d1,) = un(co)
    cs_d1 = b_d1.rs_pairsum("ffn1_w_down", dwd1[0], r1_d1, hc)
    gh_gu1 = b_gu1.rs_final("ffn1_w_gu", dwgu1[0], r1_gu1, r2_gu1, qc)
    c, un = _merge(b_d1.rs_ici(cs_d1), b_gu1.rs_share(gh_gu1))
    (dx,), co = ffn_dx("ffn1", [("up", du1)], wgu1, [dr1, dx_gate], carry=c)
    (r2_d1,), (g_gu1,) = un(co)
    gh_d1 = b_d1.rs_final("ffn1_w_down", dwd1[0], r1_d1, r2_d1, qc)

    grads = {"ffn1_w_gu": g_gu1, "ffn2_w_gu": g_gu2, "ffn2_w_down": g_d2, **g_mix}
    outs_g, outs_d, outs_m, outs_v = {}, {}, {}, {}

    def adamw_big(n, g, carry=None):
        b = B[n]
        return _adamw(f"adamw_{n}", b.w, g, b.m, b.v, _tile_rows(b.sr, b.sc, 1024 * 1024), carry=carry)

    upd = {}
    upd["w_a_proj"], (grads["ffn1_w_down"],) = adamw_big("w_a_proj", grads["w_a_proj"], carry=b_d1.rs_share(gh_d1))
    for n, _ in big_names:
        if n not in upd:
            upd[n] = adamw_big(n, grads[n])
        g_, d_, m_, v_ = upd[n]
        outs_g[n], outs_d[n], outs_m[n], outs_v[n] = g_[None], d_[None], m_[None], v_[None]
    gsum = _sum8("small_sum", small_slots)
    full_shapes = [args[n].shape if n != "conv_w_dw" else (1, KW, W) for n in small_names]
    gsmall = dict(zip(small_names, _unpack_rows(gsum, spans, full_shapes), strict=True))
    cs = conv_w_dw.shape[2]
    gsmall["conv_w_dw"] = lax.dynamic_slice_in_dim(gsmall["conv_w_dw"], q * cs, cs, axis=2)
    pw, spans2 = _pack_rows([args[n] for n in small_names])
    pg, _ = _pack_rows([gsmall[n] for n in small_names])
    pm, _ = _pack_rows([args["m_" + n] for n in small_names])
    pv, _ = _pack_rows([args["v_" + n] for n in small_names])
    _, pd, pmn, pvn = _adamw("adamw_small", pw, pg, pm, pv, pw.shape[0])
    shapes2 = [args[n].shape for n in small_names]
    for dst, src in ((outs_d, pd), (outs_m, pmn), (outs_v, pvn)):
        dst.update(zip(small_names, _unpack_rows(src, spans2, shapes2), strict=True))
    outs_g.update(gsmall)

    loss = lax.psum(loss_part[0, 0], ("x", "y", "c"))
    order = ["ffn1_w_gu", "ffn1_w_down", "ln1_g", "ln1_b", "w_in", "b_in", "sgu_ln_g", "sgu_ln_b", "sgu_w_s", "sgu_b_s",
             "w_a_proj", "conv_w_dw", "conv_b_dw", "conv_ln_g", "conv_ln_b", "w_b_proj", "w_out", "ln2_g", "ln2_b",
             "ffn2_w_gu", "ffn2_w_down", "ln3_g", "ln3_b"]
    return (loss, dx[None], *[outs_g[n] for n in order], *[outs_d[n] for n in order], *[outs_m[n] for n in order],
            *[outs_v[n] for n in order])
```

```python
import math

import jax
import jax.numpy as jnp
from jax import lax
from jax.experimental import pallas as pl
from jax.experimental.pallas import tpu as pltpu

BF16 = jnp.bfloat16
F32 = jnp.float32

LN_EPS = 1e-5
ALPHA = 2.0 ** 0.25
SGU_BLOCK = 128
SGU_CHUNK = 64
HALO = 32
SUBLANES = 8
LANES = 128
CONV_ROWS = 32
ADAM_LR = 0.001
ADAM_B1 = 0.9
ADAM_B2 = 0.999
ADAM_EPS = 1e-08
ADAM_WD = 0.01
ADAM_STEP = 10
N_CHIPS = 4
VMEM_LIMIT = 52 * 1024 * 1024
MESH = pl.DeviceIdType.MESH

_GELU_C0 = math.sqrt(2.0 / math.pi)
_GELU_C1 = 0.044715


def _cparams(sem):
    return pltpu.CompilerParams(dimension_semantics=sem, vmem_limit_bytes=VMEM_LIMIT)


def _gelu_parts(x):
    x2 = x * x
    t = jnp.tanh(_GELU_C0 * (x + _GELU_C1 * (x2 * x)))
    return 0.5 * (1.0 + t), t, x2


def _gelu(x):
    cdf, _, _ = _gelu_parts(x)
    return x * cdf


def _gelu_and_grad(x):
    cdf, t, x2 = _gelu_parts(x)
    grad = cdf + x * (0.5 * (1.0 - t * t)) * (_GELU_C0 * (1.0 + (3.0 * _GELU_C1) * x2))
    return x * cdf, grad


def _silu_grad(x):
    s = jax.nn.sigmoid(x)
    return s * (1.0 + x * (1.0 - s))


def _row_stats(x):
    mu = jnp.mean(x, axis=-1, keepdims=True)
    xc = x - mu
    var = jnp.mean(xc * xc, axis=-1, keepdims=True)
    rstd = lax.rsqrt(var + LN_EPS)
    return xc * rstd, rstd


def _ln_bwd_rows(dy, xhat, rstd, g):
    dxh = dy * g
    m1 = jnp.mean(dxh, axis=-1, keepdims=True)
    m2 = jnp.mean(dxh * xhat, axis=-1, keepdims=True)
    return rstd * (dxh - m1 - xhat * m2)


def _colsum(v):
    return jnp.sum(v, axis=0, keepdims=True)


class _Carry:
    def __init__(self, ro, rw, new, plan, n_remote, n_local=0):
        self.ro, self.rw, self.new, self.plan = list(ro), list(rw), list(new), plan
        self.n_remote, self.n_local = n_remote, n_local

    def sems(self):
        return [pltpu.SemaphoreType.DMA((self.n_remote,)), pltpu.SemaphoreType.DMA((self.n_remote,)),
                pltpu.SemaphoreType.DMA((max(self.n_local, 1),))]

    def copies(self, ro_refs, rw_refs, new_refs, send_sems, recv_sems, loc_sems):
        x, y, c = lax.axis_index("x"), lax.axis_index("y"), lax.axis_index("c")
        remote, local = self.plan(ro_refs, rw_refs, new_refs, x, y, c)
        assert len(remote) == self.n_remote and len(local) == self.n_local
        lcs = [pltpu.make_async_copy(s, d, loc_sems.at[k]) for k, (s, d) in enumerate(local)]
        rcs = [pltpu.make_async_remote_copy(src_ref=s, dst_ref=d, send_sem=send_sems.at[k], recv_sem=recv_sems.at[k],
                                            device_id=dev, device_id_type=MESH) for k, (s, d, dev) in enumerate(remote)]
        return lcs, rcs

    def out_shape(self):
        return [jax.ShapeDtypeStruct(a.shape, a.dtype) for a in self.rw] + self.new


def _start_all(lcs, rcs):
    for cp in lcs + rcs:
        cp.start()


def _wait_all(lcs, rcs):
    for cp in rcs:
        cp.wait_send()
    for cp in rcs:
        cp.wait_recv()
    for cp in lcs:
        cp.wait()


def _merge(*cs):
    ro = [a for c in cs for a in c.ro]
    rw = [a for c in cs for a in c.rw]
    new = [a for c in cs for a in c.new]

    def plan(ro_refs, rw_refs, new_refs, x, y, c):
        remote, local, a, b, d = [], [], 0, 0, 0
        for cc in cs:
            r, l = cc.plan(ro_refs[a:a + len(cc.ro)], rw_refs[b:b + len(cc.rw)], new_refs[d:d + len(cc.new)], x, y, c)
            a, b, d = a + len(cc.ro), b + len(cc.rw), d + len(cc.new)
            remote += r
            local += l
        return remote, local

    def unpack(couts):
        res, b, d = [], 0, len(rw)
        for cc in cs:
            res.append(list(couts[b:b + len(cc.rw)]) + list(couts[d:d + len(cc.new)]))
            b, d = b + len(cc.rw), d + len(cc.new)
        return res

    return _Carry(ro, rw, new, plan, sum(c.n_remote for c in cs), sum(c.n_local for c in cs)), unpack


_ANY = pl.BlockSpec(memory_space=pl.ANY)


def _exchange(name, carry):
    n_ro, n_rw, n_new = len(carry.ro), len(carry.rw), len(carry.new)

    def body(*refs):
        o0 = n_ro + n_rw
        lcs, rcs = carry.copies(refs[:n_ro], refs[o0:o0 + n_rw], refs[o0 + n_rw:o0 + n_rw + n_new], *refs[o0 + n_rw + n_new:])
        _start_all(lcs, rcs)
        _wait_all(lcs, rcs)

    return list(pl.pallas_call(
        body, name=name, in_specs=[_ANY] * (n_ro + n_rw), out_specs=[_ANY] * (n_rw + n_new), out_shape=carry.out_shape(),
        input_output_aliases={n_ro + k: k for k in range(n_rw)}, scratch_shapes=carry.sems())(*carry.ro, *carry.rw))


def _call(body, *, name, grid, in_specs, out_specs, out_shape, args, scratch=(), sem, carry=None, aliases=None):
    in_specs, out_specs, out_shape, scratch = list(in_specs), list(out_specs), list(out_shape), list(scratch)
    if carry is None:
        return list(pl.pallas_call(body, name=name, grid=grid, in_specs=in_specs, out_specs=out_specs, out_shape=out_shape,
                                   scratch_shapes=scratch, input_output_aliases=aliases or {},
                                   compiler_params=_cparams(sem))(*args))
    n_in, n_out, n_scr = len(in_specs), len(out_specs), len(scratch)
    n_ro, n_rw, n_new = len(carry.ro), len(carry.rw), len(carry.new)

    def wrapped(*refs):
        ins = refs[:n_in]
        ro_refs = refs[n_in:n_in + n_ro]
        o0 = n_in + n_ro + n_rw
        outs = refs[o0:o0 + n_out]
        rw_refs = refs[o0 + n_out:o0 + n_out + n_rw]
        new_refs = refs[o0 + n_out + n_rw:o0 + n_out + n_rw + n_new]
        s0 = o0 + n_out + n_rw + n_new
        scr = refs[s0:s0 + n_scr]
        sems = refs[s0 + n_scr:]
        first = pl.program_id(0) == 0
        last = pl.program_id(0) == grid[0] - 1
        for d in range(1, len(grid)):
            first = jnp.logical_and(first, pl.program_id(d) == 0)
            last = jnp.logical_and(last, pl.program_id(d) == grid[d] - 1)

        @pl.when(first)
        def _():
            _start_all(*carry.copies(ro_refs, rw_refs, new_refs, *sems))

        body(*ins, *outs, *scr)

        @pl.when(last)
        def _():
            _wait_all(*carry.copies(ro_refs, rw_refs, new_refs, *sems))

    al = dict(aliases or {})
    al.update({n_in + n_ro + k: n_out + k for k in range(n_rw)})
    res = pl.pallas_call(
        wrapped, name=name, grid=grid, in_specs=in_specs + [_ANY] * (n_ro + n_rw), out_specs=out_specs + [_ANY] * (n_rw + n_new),
        out_shape=out_shape + carry.out_shape(), scratch_shapes=scratch + carry.sems(), input_output_aliases=al,
        compiler_params=_cparams(("arbitrary",) * len(grid)))(*args, *carry.ro, *carry.rw)
    return list(res[:n_out]), list(res[n_out:])


def _mm(name, pairs, *, ta=False, tb=False, M, N, tm, tn, tk, nk, epilogue, outs, extras=(), sum_pairs=True, carry=None,
        sub=None):
    ni, nj = M // tm, N // tn
    assert ni * tm == M and nj * tn == N
    n_p = len(pairs)
    n_acc = 1 if sum_pairs else n_p
    in_specs, args = [], []
    for a, b, ak, bk, bj in pairs:
        if ta:
            in_specs.append(pl.BlockSpec((tk, tm), lambda i, j, k, ak=ak: (k + ak, i)))
        else:
            in_specs.append(pl.BlockSpec((tm, tk), lambda i, j, k, ak=ak: (i, k + ak)))
        if tb:
            in_specs.append(pl.BlockSpec((tn, tk), lambda i, j, k, bk=bk, bj=bj: (j + bj, k + bk)))
        else:
            in_specs.append(pl.BlockSpec((tk, tn), lambda i, j, k, bk=bk, bj=bj: (k + bk, j + bj)))
        args += [a, b]
    for arr, kind, jo in extras:
        if kind == "mn":
            in_specs.append(pl.BlockSpec((tm, tn), lambda i, j, k, jo=jo: (i, j + jo)))
        else:
            in_specs.append(pl.BlockSpec((1, tn), lambda i, j, k, jo=jo: (0, j + jo)))
        args.append(arr)
    out_specs = [pl.BlockSpec((tm, tn), lambda i, j, k, jo=jo: (i, j + jo)) for _, _, jo in outs]
    out_shape = [jax.ShapeDtypeStruct((M, nc), dt) for nc, dt, _ in outs]
    n_ex, n_out = len(extras), len(outs)
    dn = (((0 if ta else 1,), (1 if tb else 0,)), ((), ()))

    def body(*refs):
        ab = refs[: 2 * n_p]
        ex = refs[2 * n_p: 2 * n_p + n_ex]
        o0 = 2 * n_p + n_ex
        out_refs = refs[o0: o0 + n_out]
        acc_refs = refs[o0 + n_out:]

        def dots():
            res = []
            for p in range(n_p):
                a = ab[2 * p][...]
                b = ab[2 * p + 1][...]
                res.append(lax.dot_general(a.astype(BF16), b.astype(BF16), dn, preferred_element_type=F32))
            if sum_pairs:
                tot = res[0]
                for r in res[1:]:
                    tot = tot + r
                res = [tot]
            return res

        def finish(accs):
            tiles = epilogue(accs, [e[...] for e in ex])
            for r, t in zip(out_refs, tiles, strict=True):
                r[...] = t.astype(r.dtype)

        if nk == 1 and sub is not None and tn > sub:
            for s in range(tn // sub):
                cs = slice(s * sub, (s + 1) * sub)
                res = []
                for p in range(n_p):
                    b = ab[2 * p + 1][cs, :] if tb else ab[2 * p + 1][:, cs]
                    res.append(lax.dot_general(ab[2 * p][...].astype(BF16), b.astype(BF16), dn, preferred_element_type=F32))
                if sum_pairs:
                    tot = res[0]
                    for r in res[1:]:
                        tot = tot + r
                    res = [tot]
                tiles = epilogue(res, [e[:, cs] for e in ex])
                for r, t in zip(out_refs, tiles, strict=True):
                    r[:, cs] = t.astype(r.dtype)
        elif nk == 1:
            finish(dots())
        else:
            k = pl.program_id(2)
            d = dots()

            @pl.when(k == 0)
            def _():
                for r, v in zip(acc_refs, d, strict=True):
                    r[...] = v

            @pl.when(k > 0)
            def _():
                for r, v in zip(acc_refs, d, strict=True):
                    r[...] += v

            @pl.when(k == nk - 1)
            def _():
                finish([r[...] for r in acc_refs])

    scratch = [pltpu.VMEM((tm, tn), F32) for _ in range(n_acc)] if nk > 1 else []
    return _call(body, name=name, grid=(ni, nj, nk), in_specs=in_specs, out_specs=out_specs, out_shape=out_shape, args=args,
                 scratch=scratch, sem=("parallel", "parallel", "arbitrary"), carry=carry)


def _wgrad(name, a, b, *, M, N, T, tm, tn, tk, into=None, joff=0, ncols=None, carry=None):
    ncols = N if ncols is None else ncols
    ni, nj, nk = M // tm, N // tn, T // tk
    dn = (((0,), (0,)), ((), ()))

    def body(a_ref, b_ref, *rest):
        d = lax.dot_general(a_ref[...].astype(BF16), b_ref[...].astype(BF16), dn, preferred_element_type=F32)
        if nk == 1:
            of_ref, oh_ref = rest[-2:]
            of_ref[...] = d
            oh_ref[...] = d.astype(BF16)
            return
        of_ref, oh_ref, acc_ref = rest[-3:]
        k = pl.program_id(2)

        @pl.when(k == 0)
        def _():
            acc_ref[...] = d

        @pl.when(k > 0)
        def _():
            acc_ref[...] += d

        @pl.when(k == nk - 1)
        def _():
            of_ref[...] = acc_ref[...]
            oh_ref[...] = acc_ref[...].astype(BF16)

    ospec = pl.BlockSpec((tm, tn), lambda i, j, k: (i, j + joff))
    in_specs = [pl.BlockSpec((tk, tm), lambda i, j, k: (k, i)), pl.BlockSpec((tk, tn), lambda i, j, k: (k, j))]
    args, aliases = [a, b], None
    if into is not None:
        in_specs += [_ANY, _ANY]
        args += list(into)
        aliases = {2: 0, 3: 1}
    return _call(body, name=name, grid=(ni, nj, nk), in_specs=in_specs, out_specs=[ospec, ospec],
                 out_shape=[jax.ShapeDtypeStruct((M, ncols), F32), jax.ShapeDtypeStruct((M, ncols), BF16)], args=args,
                 scratch=[pltpu.VMEM((tm, tn), F32)] if nk > 1 else [], sem=("parallel", "parallel", "arbitrary"), carry=carry,
                 aliases=aliases)


def _rows(tm, c, cb=0):
    return pl.BlockSpec((tm, c), lambda i, cb=cb: (i, cb))


def _whole(shape):
    nd = len(shape)
    return pl.BlockSpec(shape, lambda i, nd=nd: (0,) * nd)


def _cast_bf16(name, x, tm, carry=None):
    t, d = x.shape

    def body(x_ref, o_ref):
        o_ref[...] = x_ref[...].astype(BF16)

    return _call(body, name=name, grid=(t // tm,), in_specs=[_rows(tm, d)], out_specs=[_rows(tm, d)],
                 out_shape=[jax.ShapeDtypeStruct((t, d), BF16)], args=[x], sem=("parallel",), carry=carry)


def _residual(x_ref, prev_refs):
    if not prev_refs:
        return x_ref[...]
    return x_ref[...] * prev_refs[0][...] + prev_refs[1][...]


def _ln_fwd(name, xres, f, g, b, cf, tm, carry=None, prev=()):
    t, d = xres.shape
    n_prev = len(prev)

    def body(x_ref, f_ref, g_ref, b_ref, *rest):
        yb_ref, xh_ref, rs_ref = rest[n_prev:]
        r = ALPHA * _residual(x_ref, rest[:n_prev]) + cf * f_ref[...]
        xhat, rstd = _row_stats(r)
        yb_ref[...] = (xhat * g_ref[...] + b_ref[...]).astype(BF16)
        xh_ref[...] = xhat
        rs_ref[...] = rstd

    return _call(
        body, name=name, grid=(t // tm,),
        in_specs=[_rows(tm, d), _rows(tm, d), _whole((1, d)), _whole((1, d))] + [_whole((1, d))] * n_prev,
        out_specs=[_rows(tm, d), _rows(tm, d), _rows(tm, 1)],
        out_shape=[jax.ShapeDtypeStruct((t, d), BF16), jax.ShapeDtypeStruct((t, d), F32), jax.ShapeDtypeStruct((t, 1), F32)],
        args=[xres, f, g, b, *prev], sem=("parallel",), carry=carry)


def _ln_bwd(name, dy, xhat, rstd, g, scale, tm, carry=None):
    t, d = dy.shape

    def body(dy_ref, xh_ref, rs_ref, g_ref, dr_ref, drb_ref, dg_ref, db_ref):
        i = pl.program_id(0)
        dy_v, xh = dy_ref[...], xh_ref[...]
        dr = _ln_bwd_rows(dy_v, xh, rs_ref[...], g_ref[...])
        dr_ref[...] = dr
        drb_ref[...] = (scale * dr).astype(BF16)

        @pl.when(i == 0)
        def _():
            dg_ref[...] = jnp.zeros_like(dg_ref)
            db_ref[...] = jnp.zeros_like(db_ref)

        dg_ref[...] += _colsum(dy_v * xh)
        db_ref[...] += _colsum(dy_v)

    return _call(
        body, name=name, grid=(t // tm,), in_specs=[_rows(tm, d), _rows(tm, d), _rows(tm, 1), _whole((1, d))],
        out_specs=[_rows(tm, d), _rows(tm, d), _whole((1, d)), _whole((1, d))],
        out_shape=[jax.ShapeDtypeStruct((t, d), F32), jax.ShapeDtypeStruct((t, d), BF16),
                   jax.ShapeDtypeStruct((1, d), F32), jax.ShapeDtypeStruct((1, d), F32)],
        args=[dy, xhat, rstd, g], sem=("arbitrary",), carry=carry)


def _ln3_loss(name, xres, f, g, b, target, tm, prev):
    t, d = xres.shape

    def body(x_ref, f_ref, g_ref, b_ref, tg_ref, pg_ref, pb_ref, dr_ref, drb_ref, loss_ref, dg_ref, db_ref):
        i = pl.program_id(0)
        r = ALPHA * _residual(x_ref, (pg_ref, pb_ref)) + 0.5 * f_ref[...]
        xhat, rstd = _row_stats(r)
        gv = g_ref[...]
        y = xhat * gv + b_ref[...]
        err = y - tg_ref[...]
        dy = err * (1.0 / d)
        dr = _ln_bwd_rows(dy, xhat, rstd, gv)
        dr_ref[...] = dr
        drb_ref[...] = (0.5 * dr).astype(BF16)
        part = 0.5 * jnp.sum(jnp.mean(err * err, axis=-1, keepdims=True), axis=0, keepdims=True)

        @pl.when(i == 0)
        def _():
            loss_ref[...] = jnp.zeros_like(loss_ref)
            dg_ref[...] = jnp.zeros_like(dg_ref)
            db_ref[...] = jnp.zeros_like(db_ref)

        loss_ref[...] += jnp.broadcast_to(part, loss_ref.shape)
        dg_ref[...] += _colsum(dy * xhat)
        db_ref[...] += _colsum(dy)

    return _call(
        body, name=name, grid=(t // tm,),
        in_specs=[_rows(tm, d), _rows(tm, d), _whole((1, d)), _whole((1, d)), _rows(tm, d), _whole((1, d)), _whole((1, d))],
        out_specs=[_rows(tm, d), _rows(tm, d), _whole((8, 128)), _whole((1, d)), _whole((1, d))],
        out_shape=[jax.ShapeDtypeStruct((t, d), F32), jax.ShapeDtypeStruct((t, d), BF16),
                   jax.ShapeDtypeStruct((8, 128), F32), jax.ShapeDtypeStruct((1, d), F32),
                   jax.ShapeDtypeStruct((1, d), F32)],
        args=[xres, f, g, b, target, *prev], sem=("arbitrary",))


def _colsum_rows(name, x, tm):
    t, d = x.shape

    def body(x_ref, o_ref):
        @pl.when(pl.program_id(0) == 0)
        def _():
            o_ref[...] = jnp.zeros_like(o_ref)

        o_ref[...] += _colsum(x_ref[...].astype(F32))

    return _call(body, name=name, grid=(t // tm,), in_specs=[_rows(tm, d)], out_specs=[_whole((1, d))],
                 out_shape=[jax.ShapeDtypeStruct((1, d), F32)], args=[x], sem=("arbitrary",))[0]


def _sgu_mask():
    sh = SGU_CHUNK.bit_length() - 1
    r = lax.shift_right_logical(lax.broadcasted_iota(jnp.int32, (SGU_BLOCK, SGU_BLOCK), 0), sh)
    c = lax.shift_right_logical(lax.broadcasted_iota(jnp.int32, (SGU_BLOCK, SGU_BLOCK), 1), sh)
    return c <= r


def _sgu_fwd(name, p, lng, lnb, wm, bst, tm):
    t = p.shape[0]
    n_grp, w = wm.shape[0], lng.shape[1]
    hd = w // n_grp
    nblk = tm // SGU_BLOCK

    def body(uv_ref, h_ref, g_ref, b_ref, wm_ref, bs_ref, sa_ref, z_ref, vn_s):
        xhat, _ = _row_stats(_gelu(uv_ref[:, w:]))
        vn_s[...] = (xhat * g_ref[...] + b_ref[...]).astype(BF16)
        mask = _sgu_mask()
        for h in range(n_grp):
            wh = jnp.where(mask, wm_ref[h], 0.0).astype(BF16)
            bcol = bs_ref[:, h:h + 1]
            cs = slice(h * hd, (h + 1) * hd)
            for n in range(nblk):
                rs = slice(n * SGU_BLOCK, (n + 1) * SGU_BLOCK)
                s = jnp.dot(wh, vn_s[rs, cs], preferred_element_type=F32) + bcol
                sa_ref[rs, cs] = (_gelu(uv_ref[rs, cs]) * s).astype(BF16)
        z_ref[...] = h_ref[:, :w] * jax.nn.sigmoid(h_ref[:, w:])

    return _call(
        body, name=name, grid=(t // tm,),
        in_specs=[_rows(tm, 2 * w, 0), _rows(tm, 2 * w, 1), _whole((1, w)), _whole((1, w)), _whole(wm.shape),
                  _whole(bst.shape)],
        out_specs=[_rows(tm, w), _rows(tm, w)],
        out_shape=[jax.ShapeDtypeStruct((t, w), BF16), jax.ShapeDtypeStruct((t, w), F32)],
        args=[p, p, lng, lnb, wm, bst], scratch=[pltpu.VMEM((tm, w), BF16)], sem=("parallel",))


def _sgu_bwd(name, p, dsa, lng, lnb, wm, bst, tm):
    t = p.shape[0]
    n_grp, w = wm.shape[0], lng.shape[1]
    hd = w // n_grp
    nblk = tm // SGU_BLOCK

    def body(uv_ref, dsa_ref, g_ref, b_ref, wm_ref, bs_ref, dp_ref, dwm_ref, dbs_ref, dg_ref, db_ref, dbin_ref,
             vn_s, ug_s, dvn_s, dug_s):
        i = pl.program_id(0)

        @pl.when(i == 0)
        def _():
            dwm_ref[...] = jnp.zeros_like(dwm_ref)
            dbs_ref[...] = jnp.zeros_like(dbs_ref)
            dg_ref[...] = jnp.zeros_like(dg_ref)
            db_ref[...] = jnp.zeros_like(db_ref)
            dbin_ref[...] = jnp.zeros_like(dbin_ref)

        ug, dgelu_u = _gelu_and_grad(uv_ref[:, :w])
        ug_s[...] = ug
        vg, dgelu_v = _gelu_and_grad(uv_ref[:, w:])
        xhat, rstd = _row_stats(vg)
        gv = g_ref[...]
        vn_s[...] = (xhat * gv + b_ref[...]).astype(BF16)
        mask = _sgu_mask()
        for h in range(n_grp):
            wh = jnp.where(mask, wm_ref[h], 0.0).astype(BF16)
            bcol = bs_ref[:, h:h + 1]
            cs = slice(h * hd, (h + 1) * hd)
            dw_h = jnp.zeros((SGU_BLOCK, SGU_BLOCK), F32)
            dbs_h = jnp.zeros((SGU_BLOCK, 1), F32)
            for n in range(nblk):
                rs = slice(n * SGU_BLOCK, (n + 1) * SGU_BLOCK)
                vblk = vn_s[rs, cs]
                s = jnp.dot(wh, vblk, preferred_element_type=F32) + bcol
                dsa_blk = dsa_ref[rs, cs]
                dug_s[rs, cs] = dsa_blk * s
                ds = dsa_blk * ug_s[rs, cs]
                dsb = ds.astype(BF16)
                dvn_s[rs, cs] = lax.dot_general(wh, dsb, (((0,), (0,)), ((), ())), preferred_element_type=F32)
                dw_h = dw_h + lax.dot_general(dsb, vblk, (((1,), (1,)), ((), ())), preferred_element_type=F32)
                dbs_h = dbs_h + jnp.sum(ds, axis=1, keepdims=True)
            dwm_ref[h] += jnp.where(mask, dw_h, 0.0)
            dbs_ref[:, h:h + 1] += dbs_h
        dvn = dvn_s[...]
        dg_ref[...] += _colsum(dvn * xhat)
        db_ref[...] += _colsum(dvn)
        dvg = _ln_bwd_rows(dvn, xhat, rstd, gv)
        du = dug_s[...] * dgelu_u
        dv = dvg * dgelu_v
        dp_ref[:, :w] = du.astype(BF16)
        dp_ref[:, w:] = dv.astype(BF16)
        dbin_ref[:, :w] += _colsum(du)
        dbin_ref[:, w:] += _colsum(dv)

    return _call(
        body, name=name, grid=(t // tm,),
        in_specs=[_rows(tm, 2 * w, 0), _rows(tm, w), _whole((1, w)), _whole((1, w)), _whole(wm.shape), _whole(bst.shape)],
        out_specs=[_rows(tm, 2 * w), _whole(wm.shape), _whole(bst.shape), _whole((1, w)), _whole((1, w)), _whole((1, 2 * w))],
        out_shape=[jax.ShapeDtypeStruct((t, 2 * w), BF16), jax.ShapeDtypeStruct(wm.shape, F32),
                   jax.ShapeDtypeStruct(bst.shape, F32), jax.ShapeDtypeStruct((1, w), F32), jax.ShapeDtypeStruct((1, w), F32),
                   jax.ShapeDtypeStruct((1, 2 * w), F32)],
        args=[p, dsa, lng, lnb, wm, bst],
        scratch=[pltpu.VMEM((tm, w), BF16), pltpu.VMEM((tm, w), F32), pltpu.VMEM((tm, w), F32), pltpu.VMEM((tm, w), F32)],
        sem=("arbitrary",))


def _halo_prev(tm, c):
    return pl.BlockSpec((HALO, c), lambda i: (jnp.maximum(i * (tm // HALO) - 1, 0), 0))


def _halo_next(tm, c, t):
    last = t // HALO - 1
    return pl.BlockSpec((HALO, c), lambda i: (jnp.minimum((i + 1) * (tm // HALO), last), 0))


def _shifted_copies(sh, n):
    for r in range(1, SUBLANES):
        sh[r, :n - SUBLANES, :] = sh[0, r:r + n - SUBLANES, :]


def _row_broadcasts(wb, w_ref, kw):
    for k in range(kw):
        wb[k] = jnp.broadcast_to(w_ref[k:k + 1, :], wb.shape[1:])


def _tap(sh, r0, o, rows, cols):
    return sh[o % SUBLANES, pl.ds(pl.multiple_of(r0 + (o - o % SUBLANES), SUBLANES), rows), cols]


def _conv_fwd(name, z, wdw, kw, bdw, lng, lnb, tm, carry=None):
    t, c = z.shape
    lead = HALO - (kw - 1)
    n = tm + HALO

    def body(zp_ref, z_ref, w_ref, bdw_ref, g_ref, b_ref, zc_ref, sb_ref, sh, wb):
        i = pl.program_id(0)
        sh[0, :HALO, :] = jnp.where(i > 0, zp_ref[...], 0.0)
        sh[0, HALO:, :] = z_ref[...]
        _shifted_copies(sh, n)
        _row_broadcasts(wb, w_ref, kw)
        bias = jnp.broadcast_to(bdw_ref[...], (SUBLANES, c))
        groups = CONV_ROWS // SUBLANES

        def chunk(ci, _):
            r0 = pl.multiple_of(ci * CONV_ROWS, CONV_ROWS)
            accs = [bias] * groups
            for k in range(kw):
                wk = wb[k]
                tp = _tap(sh, r0, lead + k, CONV_ROWS, slice(None))
                accs = [accs[g] + wk * tp[g * SUBLANES:(g + 1) * SUBLANES] for g in range(groups)]
            zc_ref[pl.ds(r0, CONV_ROWS), :] = jnp.concatenate(accs, axis=0)
            return 0

        lax.fori_loop(0, tm // CONV_ROWS, chunk, 0)
        xhat, _ = _row_stats(zc_ref[...])
        zn = xhat * g_ref[...] + b_ref[...]
        sb_ref[...] = (zn * jax.nn.sigmoid(zn)).astype(BF16)

    return _call(
        body, name=name, grid=(t // tm,),
        in_specs=[_halo_prev(tm, c), _rows(tm, c), _whole(wdw.shape), _whole((1, c)), _whole((1, c)), _whole((1, c))],
        out_specs=[_rows(tm, c), _rows(tm, c)],
        out_shape=[jax.ShapeDtypeStruct((t, c), F32), jax.ShapeDtypeStruct((t, c), BF16)],
        args=[z, z, wdw, bdw, lng, lnb], scratch=[pltpu.VMEM((SUBLANES, n, c), F32), pltpu.VMEM((HALO, SUBLANES, c), F32)],
        sem=("parallel",), carry=carry)


def _conv_ln_bwd(name, dsb, zc, lng, lnb, tm):
    t, c = zc.shape

    def body(dsb_ref, zc_ref, g_ref, b_ref, dzc_ref, dg_ref, db_ref, dbdw_ref):
        i = pl.program_id(0)
        xhat, rstd = _row_stats(zc_ref[...])
        gv = g_ref[...]
        zn = xhat * gv + b_ref[...]
        dzn = dsb_ref[...] * _silu_grad(zn)
        dzc = _ln_bwd_rows(dzn, xhat, rstd, gv)
        dzc_ref[...] = dzc

        @pl.when(i == 0)
        def _():
            dg_ref[...] = jnp.zeros_like(dg_ref)
            db_ref[...] = jnp.zeros_like(db_ref)
            dbdw_ref[...] = jnp.zeros_like(dbdw_ref)

        dg_ref[...] += _colsum(dzn * xhat)
        db_ref[...] += _colsum(dzn)
        dbdw_ref[...] += _colsum(dzc)

    return _call(
        body, name=name, grid=(t // tm,), in_specs=[_rows(tm, c), _rows(tm, c), _whole((1, c)), _whole((1, c))],
        out_specs=[_rows(tm, c), _whole((1, c)), _whole((1, c)), _whole((1, c))],
        out_shape=[jax.ShapeDtypeStruct((t, c), F32)] + [jax.ShapeDtypeStruct((1, c), F32)] * 3,
        args=[dsb, zc, lng, lnb], sem=("arbitrary",))


def _conv_bwd(name, dzc, z, p, wdw, kw, tm):
    t, c = z.shape
    n_i = t // tm
    n = tm + HALO

    def body(dzc_ref, dzn_ref, z_ref, h_ref, w_ref, dp_ref, dw_ref, dbin_ref, sh, dz_s, wb):
        i = pl.program_id(0)

        @pl.when(i == 0)
        def _():
            dw_ref[...] = jnp.zeros_like(dw_ref)

        sh[0, :tm, :] = dzc_ref[...]
        sh[0, tm:, :] = jnp.where(i < n_i - 1, dzn_ref[...], 0.0)
        _shifted_copies(sh, n)
        _row_broadcasts(wb, w_ref, kw)
        nv = 4
        rows = nv * SUBLANES
        ways = 2
        for lc in range(c // LANES):
            cols = slice(lc * LANES, (lc + 1) * LANES)

            def rowv(rv, accs, cols=cols):
                r0 = pl.multiple_of(rv * rows, rows)
                zv = z_ref[pl.ds(r0, rows), cols]
                zs = [zv[v * SUBLANES:(v + 1) * SUBLANES] for v in range(nv)]
                dz = [[None] * ways for _ in range(nv)]
                new = []
                for k in range(kw):
                    s = _tap(sh, r0, kw - 1 - k, rows, cols)
                    wk = wb[k, :, cols]
                    j = k % ways
                    acc = accs[k]
                    for v in range(nv):
                        sv = s[v * SUBLANES:(v + 1) * SUBLANES]
                        dz[v][j] = wk * sv if dz[v][j] is None else dz[v][j] + wk * sv
                        acc = acc + zs[v] * sv
                    new.append(acc)
                dz_s[pl.ds(r0, rows), cols] = jnp.concatenate([dz[v][0] + dz[v][1] for v in range(nv)], axis=0)
                return tuple(new)

            accs = lax.fori_loop(0, tm // rows, rowv, tuple(jnp.zeros((SUBLANES, LANES), F32) for _ in range(kw)))
            for k in range(kw):
                dw_ref[k:k + 1, cols] += _colsum(accs[k])
        dz = dz_s[...]
        a, g = h_ref[:, :c], h_ref[:, c:]
        sg = jax.nn.sigmoid(g)
        da = dz * sg
        dg = dz * a * (sg * (1.0 - sg))
        dp_ref[:, :c] = da.astype(BF16)
        dp_ref[:, c:] = dg.astype(BF16)

        @pl.when(i == 0)
        def _():
            dbin_ref[...] = jnp.zeros_like(dbin_ref)

        dbin_ref[:, :c] += _colsum(da)
        dbin_ref[:, c:] += _colsum(dg)

    return _call(
        body, name=name, grid=(n_i,),
        in_specs=[_rows(tm, c), _halo_next(tm, c, t), _rows(tm, c), _rows(tm, 2 * c, 1), _whole(wdw.shape)],
        out_specs=[_rows(tm, 2 * c), _whole((HALO, c)), _whole((1, 2 * c))],
        out_shape=[jax.ShapeDtypeStruct((t, 2 * c), BF16), jax.ShapeDtypeStruct((HALO, c), F32),
                   jax.ShapeDtypeStruct((1, 2 * c), F32)],
        args=[dzc, dzc, z, p, wdw],
        scratch=[pltpu.VMEM((SUBLANES, n, c), F32), pltpu.VMEM((tm, c), F32), pltpu.VMEM((HALO, SUBLANES, c), F32)],
        sem=("arbitrary",))


def _adamw(name, w, g, m, v, tr, carry=None):
    r, c = w.shape
    c1 = 1.0 - ADAM_B1 ** ADAM_STEP
    c2 = 1.0 - ADAM_B2 ** ADAM_STEP

    def body(w_ref, g_ref, m_ref, v_ref, go_ref, d_ref, mo_ref, vo_ref):
        gv = g_ref[...]
        mn = ADAM_B1 * m_ref[...] + (1.0 - ADAM_B1) * gv
        vn = ADAM_B2 * v_ref[...] + (1.0 - ADAM_B2) * (gv * gv)
        go_ref[...] = gv
        d_ref[...] = -ADAM_LR * ((mn / c1) / (jnp.sqrt(vn / c2) + ADAM_EPS) + ADAM_WD * w_ref[...])
        mo_ref[...] = mn
        vo_ref[...] = vn

    spec = _rows(tr, c)
    return _call(body, name=name, grid=(r // tr,), in_specs=[spec] * 4, out_specs=[spec] * 4,
                 out_shape=[jax.ShapeDtypeStruct((r, c), F32)] * 4, args=[w, g, m, v], sem=("parallel",), carry=carry)


_RELATION_XOR = (2, 1, 3)


def _pair_sum(name, mine, recv, qc, *, part_shape, tr, in_map):
    pr, pc = part_shape

    def body(qc_ref, a_ref, b_ref, oh_ref):
        oh_ref[...] = (a_ref[...] + b_ref[...].astype(F32)).astype(BF16)

    def imap(k, i, qc_ref):
        part = qc_ref[0]
        for kk, m in enumerate(_RELATION_XOR):
            part = jnp.where(k == kk, jnp.bitwise_xor(qc_ref[0], m), part)
        return in_map(i, part, qc_ref[1])

    ispec = pl.BlockSpec((tr, pc), imap)
    ospec = pl.BlockSpec((None, tr, pc), lambda k, i, qc_ref: (k, i, 0))
    return pl.pallas_call(
        body, name=name,
        grid_spec=pltpu.PrefetchScalarGridSpec(num_scalar_prefetch=1, grid=(3, pr // tr), in_specs=[ispec, ispec],
                                               out_specs=ospec),
        out_shape=jax.ShapeDtypeStruct((3, pr, pc), BF16), compiler_params=_cparams(("parallel", "parallel")))(qc, mine, recv)


def _final_sum(name, mine, recv1, recv2, qc, *, part_shape, out_shape, tr, in_map, out_map):
    pr, pc = part_shape

    def body(qc_ref, a_ref, b_ref, r_ref, o_ref):
        own = a_ref[...] + b_ref[...].astype(F32)
        o_ref[...] = ((own + r_ref[0].astype(F32)) + r_ref[1].astype(F32)) + r_ref[2].astype(F32)

    ispec = pl.BlockSpec((tr, pc), lambda i, qc_ref: in_map(i, qc_ref[0], qc_ref[1]))
    return pl.pallas_call(
        body, name=name,
        grid_spec=pltpu.PrefetchScalarGridSpec(
            num_scalar_prefetch=1, grid=(pr // tr,),
            in_specs=[ispec, ispec, pl.BlockSpec((3, tr, pc), lambda i, qc_ref: (0, i, 0))],
            out_specs=pl.BlockSpec((tr, pc), lambda i, qc_ref: out_map(i, qc_ref[0], qc_ref[1]))),
        out_shape=jax.ShapeDtypeStruct(out_shape, F32), compiler_params=_cparams(("parallel",)))(qc, mine, recv1, recv2)


def _sum8(name, slots):
    def body(s_ref, o_ref):
        acc = s_ref[0]
        for d in range(1, 8):
            acc = acc + s_ref[d]
        o_ref[...] = acc

    return pl.pallas_call(body, name=name, out_shape=jax.ShapeDtypeStruct(slots.shape[1:], F32),
                          in_specs=[pl.BlockSpec(memory_space=pltpu.VMEM)], out_specs=pl.BlockSpec(memory_space=pltpu.VMEM),
                          compiler_params=pltpu.CompilerParams(vmem_limit_bytes=VMEM_LIMIT))(slots)


def _chips(x, y):
    return [(1 - x, y), (x, 1 - y), (1 - x, 1 - y)]


class _Big:
    def __init__(self, name, w, m, v, ax):
        self.name, self.w, self.m, self.v, self.ax = name, w, m, v, ax
        sr, sc = w.shape
        self.R, self.C = (sr * N_CHIPS, sc) if ax == 0 else (sr, sc * N_CHIPS)
        self.sr, self.sc = sr, sc
        self.hr = sr // 2

    def slot(self, ref, q, h=None):
        if self.ax == 1:
            cols = pl.ds(pl.multiple_of(q * self.sc, 128), self.sc)
            return ref.at[:, cols] if h is None else ref.at[pl.ds(pl.multiple_of(h * self.hr, 16), self.hr), cols]
        if h is None:
            return ref.at[pl.ds(pl.multiple_of(q * self.sr, 16), self.sr), :]
        return ref.at[pl.ds(pl.multiple_of(q * self.sr + h * self.hr, 16), self.hr), :]

    def half(self, ref, h):
        return ref.at[pl.ds(pl.multiple_of(h * self.hr, 16), self.hr), :]

    @property
    def part_shape(self):
        return (self.hr, self.sc)

    def cast_into_full(self, qc):
        tr = _tile_rows(self.sr, self.sc)
        nb = self.sr // tr

        def body(qc_ref, x_ref, o_ref):
            o_ref[...] = x_ref[...].astype(BF16)

        if self.ax == 1:
            ospec = pl.BlockSpec((tr, self.sc), lambda i, qc_ref: (i, qc_ref[0]))
        else:
            ospec = pl.BlockSpec((tr, self.sc), lambda i, qc_ref: (qc_ref[0] * nb + i, 0))
        return pl.pallas_call(
            body, name=f"cast_{self.name}",
            grid_spec=pltpu.PrefetchScalarGridSpec(num_scalar_prefetch=1, grid=(nb,),
                                                   in_specs=[pl.BlockSpec((tr, self.sc), lambda i, qc_ref: (i, 0))],
                                                   out_specs=ospec),
            out_shape=jax.ShapeDtypeStruct((self.R, self.C), BF16), compiler_params=_cparams(("parallel",)))(qc, self.w)

    def gather_ici(self, full, piece=(0, 1)):
        k, n = piece
        pr = self.hr // n
        assert pr * n == self.hr and pr % 16 == 0

        def plan(ro, rw, new, x, y, c):
            q = 2 * x + y
            r0 = c * self.hr + k * pr
            if self.ax == 1:
                mine = rw[0].at[pl.ds(pl.multiple_of(r0, 16), pr), pl.ds(pl.multiple_of(q * self.sc, 128), self.sc)]
            else:
                mine = rw[0].at[pl.ds(pl.multiple_of(q * self.sr + r0, 16), pr), :]
            return [(mine, mine, (cx, cy, c)) for cx, cy in _chips(x, y)], []

        return _Carry([], [full], [], plan, 3, 0)

    def gather_d2d(self, full):
        def plan(ro, rw, new, x, y, c):
            remote = []
            for cx, cy in _chips(x, y):
                piece = self.slot(rw[0], 2 * cx + cy, c)
                remote.append((piece, piece, (x, y, 1 - c)))
            return remote, []

        return _Carry([], [full], [], plan, 3)

    def rs_pair(self, g16):
        def plan(ro, rw, new, x, y, c):
            sib = (x, y, 1 - c)
            if self.ax == 1:
                rows = pl.ds(pl.multiple_of((1 - c) * self.hr, 16), self.hr)
                return [(ro[0].at[rows, :], new[0].at[rows, :], sib)], []
            return [(self.slot(ro[0], q, 1 - c), self.slot(new[0], q, 1 - c), sib) for q in range(N_CHIPS)], []

        return _Carry([g16], [], [jax.ShapeDtypeStruct((self.R, self.C), BF16)], plan, 1 if self.ax == 1 else N_CHIPS)

    def _piece_map(self, nb):
        if self.ax == 1:
            return lambda i, q, c: (c * nb + i, q)
        return lambda i, q, c: ((q * 2 + c) * nb + i, 0)

    def rs_pairsum(self, tag, g32, recv1, qc):
        tr = _tile_rows(self.hr, self.sc)
        return _pair_sum(f"rs_pairsum_{tag}", g32, recv1, qc, part_shape=self.part_shape, tr=tr,
                         in_map=self._piece_map(self.hr // tr))

    def rs_ici(self, cs16):
        def plan(ro, rw, new, x, y, c):
            return [(ro[0].at[k], new[0].at[k], (cx, cy, c)) for k, (cx, cy) in enumerate(_chips(x, y))], []

        return _Carry([cs16], [], [jax.ShapeDtypeStruct((3,) + self.part_shape, BF16)], plan, 3)

    def rs_final(self, tag, g32, recv1, recv2, qc):
        tr = _tile_rows(self.hr, self.sc)
        nb = self.hr // tr
        in_map = self._piece_map(nb)
        out_map = lambda i, q, c: (c * nb + i, 0)
        return _final_sum(f"rs_final_{tag}", g32, recv1, recv2, qc, part_shape=self.part_shape, out_shape=(self.sr, self.sc),
                          tr=tr, in_map=in_map, out_map=out_map)

    def rs_share(self, ghalf):
        def plan(ro, rw, new, x, y, c):
            piece = self.half(rw[0], c)
            return [(piece, piece, (x, y, 1 - c))], []

        return _Carry([], [ghalf], [], plan, 1)


def _small_allgather(packed):
    nr = packed.shape[0]

    def plan(ro, rw, new, x, y, c):
        me = 4 * x + 2 * y + c
        remote = []
        for fx in (0, 1):
            for fy in (0, 1):
                for fc in (0, 1):
                    if fx or fy or fc:
                        dev = (1 - x if fx else x, 1 - y if fy else y, 1 - c if fc else c)
                        remote.append((ro[0], new[0].at[me], dev))
        return remote, [(ro[0], new[0].at[me])]

    return _Carry([packed], [], [jax.ShapeDtypeStruct((8, nr, 128), F32)], plan, 7, 1)


def _conv_w_allgather(padded, cs):
    def plan(ro, rw, new, x, y, c):
        cols = pl.ds(pl.multiple_of((2 * x + y) * cs, 128), cs)
        remote = [(ro[0], new[0].at[:, cols], (cx, cy, c)) for cx, cy in _chips(x, y)]
        return remote, [(ro[0], new[0].at[:, cols])]

    return _Carry([padded], [], [jax.ShapeDtypeStruct((HALO, cs * N_CHIPS), F32)], plan, 3, 1)


def _pick(n, want):
    if n <= want:
        return n
    for t in range(want, 15, -16):
        if t % 16 == 0 and n % t == 0:
            return t
    raise ValueError(f"no tile for {n} (want {want})")


def _tile_rows(nrows, ncols, budget=2 * 1024 * 1024):
    return _pick(nrows, max(16, (budget // (4 * ncols)) // 16 * 16))


def _pick128(n, want):
    if n <= want:
        return n
    for t in range(want, 127, -128):
        if n % t == 0:
            return t
    raise ValueError(f"no lane tile for {n} (want {want})")


def _pack_rows(parts):
    out, spans, r0 = [], [], 0
    for p in parts:
        flat = p.reshape(-1).astype(F32)
        n = flat.shape[0]
        rows = -(-n // 1024) * 8
        flat = jnp.pad(flat, (0, rows * 128 - n))
        out.append(flat.reshape(rows, 128))
        spans.append((r0, rows, n))
        r0 += rows
    return jnp.concatenate(out, axis=0), spans


def _unpack_rows(packed, spans, shapes):
    res = []
    for (r0, rows, n), shp in zip(spans, shapes, strict=True):
        res.append(packed[r0:r0 + rows].reshape(-1)[:n].reshape(shp))
    return res


def _ident(accs, ex):
    return [accs[0]]


def kernel(x, ffn1_w_gu, ffn1_w_down, ln1_g, ln1_b, w_in, b_in, sgu_ln_g, sgu_ln_b, sgu_w_s, sgu_b_s, w_a_proj, conv_w_dw, conv_b_dw, conv_ln_g, conv_ln_b, w_b_proj, w_out, ln2_g, ln2_b, ffn2_w_gu, ffn2_w_down, ln3_g, ln3_b, loss_target, m_ffn1_w_gu, m_ffn1_w_down, m_ln1_g, m_ln1_b, m_w_in, m_b_in, m_sgu_ln_g, m_sgu_ln_b, m_sgu_w_s, m_sgu_b_s, m_w_a_proj, m_conv_w_dw, m_conv_b_dw, m_conv_ln_g, m_conv_ln_b, m_w_b_proj, m_w_out, m_ln2_g, m_ln2_b, m_ffn2_w_gu, m_ffn2_w_down, m_ln3_g, m_ln3_b, v_ffn1_w_gu, v_ffn1_w_down, v_ln1_g, v_ln1_b, v_w_in, v_b_in, v_sgu_ln_g, v_sgu_ln_b, v_sgu_w_s, v_sgu_b_s, v_w_a_proj, v_conv_w_dw, v_conv_b_dw, v_conv_ln_g, v_conv_ln_b, v_w_b_proj, v_w_out, v_ln2_g, v_ln2_b, v_ffn2_w_gu, v_ffn2_w_down, v_ln3_g, v_ln3_b):
    args = dict(locals())
    assert x.shape[0] == 1 and ffn1_w_gu.shape[0] == 1
    T, D = x.shape[1], x.shape[2]
    F = ffn1_w_down.shape[1] * N_CHIPS
    W = sgu_ln_g.shape[1]
    KW = conv_w_dw.shape[1]
    assert KW - 1 <= HALO and T % SGU_BLOCK == 0

    mx, my, mc = lax.axis_index("x"), lax.axis_index("y"), lax.axis_index("c")
    q = 2 * mx + my
    qc = jnp.stack([q, mc]).astype(jnp.int32)

    big_names = [("ffn1_w_gu", 1), ("ffn1_w_down", 0), ("w_in", 1), ("w_a_proj", 1), ("w_b_proj", 1), ("w_out", 0),
                 ("ffn2_w_gu", 1), ("ffn2_w_down", 0)]
    B = {n: _Big(n, args[n][0], args["m_" + n][0], args["v_" + n][0], ax) for n, ax in big_names}
    own = {n: b.cast_into_full(qc) for n, b in B.items()}

    x2d = x[0]
    tgt = loss_target[0]
    tm_r = _pick(T, 256)
    tm_ln = _pick(T, 512)
    tm = _pick(T, 1024)
    tm_h = _pick(T, 512)
    tn = _pick128(D, 1024)
    nj = D // tn
    tng = _pick128(D, 512)
    njg = D // tng
    tnf = _pick128(F, 512)
    nf = F // tnf
    tnw = _pick128(W, 1024)
    tnd = _pick128(D, 512)
    tmw = _pick128(W, 512)
    SUB = 256

    def ffn_up(tag, xb_, wgu, carry=None):
        def epi(accs, ex):
            g, u = accs
            s = jax.nn.sigmoid(g)
            sg = g * s
            return [u * (s * (1.0 + g * (1.0 - s))), sg, sg * u]

        return _mm(f"{tag}_up", [(xb_, wgu, 0, 0, 0), (xb_, wgu, 0, 0, nf)], M=T, N=F, tm=tm, tn=tnf, tk=D, nk=1, epilogue=epi,
                   outs=[(F, BF16, 0)] * 3, sum_pairs=False, carry=carry, sub=SUB)

    def ffn_down(tag, act, wd, carry=None):
        return _mm(f"{tag}_down", [(act, wd, 0, 0, 0)], M=T, N=D, tm=tm, tn=_pick128(D, 512), tk=F, nk=1, epilogue=_ident,
                   outs=[(D, F32, 0)], carry=carry)

    def ffn_dact(tag, drh, wd, dgate_f, dup_f, carry=None):
        def epi(accs, ex):
            da = accs[0]
            return [da * ex[0].astype(F32), da * ex[1].astype(F32)]

        return _mm(f"{tag}_dact", [(drh, wd, 0, 0, 0)], tb=True, M=T, N=F, tm=_pick(T, 2048), tn=tnf, tk=D, nk=1, epilogue=epi,
                   outs=[(F, BF16, 0)] * 2, extras=[(dgate_f, "mn", 0), (dup_f, "mn", 0)], carry=carry, sub=SUB)

    def ffn_dwdown(tag, act, drh, carry=None):
        return _wgrad(f"{tag}_dwdown", act, drh, M=F, N=D, T=T, tm=tnf, tn=tnd, tk=T, carry=carry)

    def ffn_dwgate(tag, xb_, dg, carry=None):
        return _wgrad(f"{tag}_dwgate", xb_, dg, M=D, N=F, T=T, tm=tnd, tn=tnf, tk=T, ncols=2 * F, carry=carry)

    def ffn_dwup(tag, xb_, du, into, carry=None):
        return _wgrad(f"{tag}_dwup", xb_, du, M=D, N=F, T=T, tm=tnd, tn=tnf, tk=T, ncols=2 * F, into=into, joff=nf, carry=carry)

    def ffn_dx(tag, which, da, wgu, addends, carry=None):
        def epi(accs, ex):
            tot = accs[0] + ALPHA * ex[0]
            for e in ex[1:]:
                tot = tot + e
            return [tot]

        return _mm(f"{tag}_dx_{which}", [(da, wgu, 0, 1 if which == "up" else 0, 0)], tb=True, M=T, N=D, tm=tm,
                   tn=_pick128(D, 256), tk=F, nk=1, epilogue=epi if addends else _ident, outs=[(D, F32, 0)],
                   extras=[(a, "mn", 0) for a in addends], carry=carry)

    wdw_pad = jnp.pad(conv_w_dw[0], ((0, HALO - KW), (0, 0)))
    c0, un = _merge(B["ffn1_w_gu"].gather_ici(own["ffn1_w_gu"]), _conv_w_allgather(wdw_pad, conv_w_dw.shape[2]))
    (wgu1,), (wdw_full,) = un(_exchange("gather_first", c0))
    (xb,), (wgu1,) = _cast_bf16("cast_x", x2d, tm_r, carry=B["ffn1_w_gu"].gather_d2d(wgu1))

    c, un = _merge(B["ffn1_w_down"].gather_ici(own["ffn1_w_down"]), B["w_in"].gather_ici(own["w_in"]))
    (g1, u1, a1), co = ffn_up("ffn1", xb, wgu1, carry=c)
    (wd1,), (win,) = un(co)
    (wd1,) = _exchange("gather_d2d_ffn1_w_down", B["ffn1_w_down"].gather_d2d(wd1))
    b_gu2, b_d2 = B["ffn2_w_gu"], B["ffn2_w_down"]
    c, un = _merge(B["w_a_proj"].gather_ici(own["w_a_proj"]), B["w_b_proj"].gather_ici(own["w_b_proj"]),
                   B["w_out"].gather_ici(own["w_out"]), B["w_in"].gather_d2d(win), b_gu2.gather_ici(own["ffn2_w_gu"], (0, 4)))
    (fo1,), co = ffn_down("ffn1", a1, wd1, carry=c)
    (wa,), (wb,), (wout,), (win,), (wgu2,) = un(co)
    c, un = _merge(B["w_a_proj"].gather_d2d(wa), B["w_b_proj"].gather_d2d(wb), B["w_out"].gather_d2d(wout),
                   b_gu2.gather_ici(wgu2, (1, 4)))
    (x1b, xh1, rs1), co = _ln_fwd("ln1", x2d, fo1, ln1_g, ln1_b, 0.5, tm_ln, carry=c)
    (wa,), (wb,), (wout,), (wgu2,) = un(co)

    c, un = _merge(b_gu2.gather_ici(wgu2, (1, 2)), b_d2.gather_ici(own["ffn2_w_down"], (0, 2)))
    (proj,), co = _mm("in_proj", [(x1b, win, 0, 0, 0)], M=T, N=4 * D, tm=tm, tn=tn, tk=D, nk=1,
                      epilogue=lambda accs, ex: [accs[0] + ex[0]], outs=[(4 * D, F32, 0)], extras=[(b_in, "n", 0)], carry=c)
    (wgu2,), (wd2,) = un(co)
    wm = sgu_w_s[0]
    bst = sgu_b_s[0].T
    sa, z = _sgu_fwd("sgu_fwd", proj, sgu_ln_g, sgu_ln_b, wm, bst, tm_r)
    (zc, sb), (wd2,) = _conv_fwd("conv_fwd", z, wdw_full, KW, conv_b_dw, conv_ln_g, conv_ln_b, tm_r,
                                 carry=b_d2.gather_ici(wd2, (1, 2)))

    def epi_mix(accs, ex):
        ya_, yb_ = accs
        ga, gb = jax.nn.sigmoid(ex[0]), jax.nn.sigmoid(ex[1])
        return [ga * ya_ + gb * yb_, ga, gb, ya_ * (ga * (1.0 - ga)), yb_ * (gb * (1.0 - gb))]

    mixin, gate_a, gate_b, dlog_a, dlog_b = _mm(
        "branch_proj", [(sa, wa, 0, 0, 0), (sb, wb, 0, 0, 0)], M=T, N=D, tm=tm, tn=tng, tk=W, nk=1, epilogue=epi_mix,
        outs=[(D, BF16, 0)] * 5, sum_pairs=False, sub=SUB, extras=[(proj, "mn", 2 * njg), (proj, "mn", 3 * njg)])
    c, un = _merge(b_gu2.gather_d2d(wgu2), b_d2.gather_d2d(wd2))
    (mix,), co = _mm("out_proj", [(mixin, wout, 0, 0, 0)], M=T, N=D, tm=tm, tn=tn, tk=D, nk=1, epilogue=_ident,
                     outs=[(D, F32, 0)], carry=c)
    (wgu2,), (wd2,) = un(co)
    x2b, xh2, rs2 = _ln_fwd("ln2", xh1, mix, ln2_g, ln2_b, 1.0, tm_ln, prev=(ln1_g, ln1_b))
    g2, u2, a2 = ffn_up("ffn2", x2b, wgu2)
    (fo2,) = ffn_down("ffn2", a2, wd2)
    dr3, dr3h, loss_part, dln3_g, dln3_b = _ln3_loss("ln3_loss", xh2, fo2, ln3_g, ln3_b, tgt, tm_r, (ln2_g, ln2_b))

    b_gu2, b_d2 = B["ffn2_w_gu"], B["ffn2_w_down"]
    dg2, du2 = ffn_dact("ffn2", dr3h, wd2, g2, u2)
    dwd2 = ffn_dwdown("ffn2", a2, dr3h)
    dwgu2, (r1_d2,) = ffn_dwgate("ffn2", x2b, dg2, carry=b_d2.rs_pair(dwd2[1]))
    dwgu2 = ffn_dwup("ffn2", x2b, du2, dwgu2)
    cs_d2 = b_d2.rs_pairsum("ffn2_w_down", dwd2[0], r1_d2, qc)
    c, un = _merge(b_gu2.rs_pair(dwgu2[1]), b_d2.rs_ici(cs_d2))
    (dx2_gate,), co = ffn_dx("ffn2", "gate", dg2, wgu2, [], carry=c)
    (r1_gu2,), (r2_d2,) = un(co)
    (dx2,) = ffn_dx("ffn2", "up", du2, wgu2, [dr3, dx2_gate])
    cs_gu2 = b_gu2.rs_pairsum("ffn2_w_gu", dwgu2[0], r1_gu2, qc)
    gh_d2 = b_d2.rs_final("ffn2_w_down", dwd2[0], r1_d2, r2_d2, qc)

    dr2, dr2b, dln2_g, dln2_b = _ln_bwd("ln2_bwd", dx2, xh2, rs2, ln2_g, 1.0, tm_ln)

    def epi_dmix(accs, ex):
        return [accs[0] * e.astype(F32) for e in ex]

    dya, dyb, dla, dlb = _mm("out_proj_bwd", [(dr2b, wout, 0, 0, 0)], tb=True, M=T, N=D, tm=tm, tn=tng, tk=D, nk=1,
                             epilogue=epi_dmix, outs=[(D, BF16, 0)] * 4, sub=SUB,
                             extras=[(gate_a, "mn", 0), (gate_b, "mn", 0), (dlog_a, "mn", 0), (dlog_b, "mn", 0)])
    dwout = _wgrad("dw_out", mixin, dr2b, M=D, N=D, T=T, tm=tnd, tn=tnd, tk=T)
    (dsa,) = _mm("a_proj_bwd", [(dya, wa, 0, 0, 0)], tb=True, M=T, N=W, tm=tm, tn=tnw, tk=D, nk=1, epilogue=_ident,
                 outs=[(W, F32, 0)])
    (dsb,) = _mm("b_proj_bwd", [(dyb, wb, 0, 0, 0)], tb=True, M=T, N=W, tm=tm, tn=tnw, tk=D, nk=1, epilogue=_ident,
                 outs=[(W, F32, 0)])
    dwa = _wgrad("dw_a_proj", sa, dya, M=W, N=D, T=T, tm=tmw, tn=tnd, tk=T)
    dwb = _wgrad("dw_b_proj", sb, dyb, M=W, N=D, T=T, tm=tmw, tn=tnd, tk=T)

    dpa, dwm, dbst, dsgu_g, dsgu_b, dbin_a = _sgu_bwd("sgu_bwd", proj, dsa, sgu_ln_g, sgu_ln_b, wm, bst, tm_r)
    dzc, dcln_g, dcln_b, dbdw = _conv_ln_bwd("conv_ln_bwd", dsb, zc, conv_ln_g, conv_ln_b, tm_r)
    dpb, dwdw, dbin_b = _conv_bwd("conv_bwd", dzc, z, proj, wdw_full, KW, tm_r)

    dps = [dpa, dpb, dla, dlb]
    db_in = jnp.concatenate([dbin_a, dbin_b, _colsum_rows("db_in_gate_a", dla, tm_r), _colsum_rows("db_in_gate_b", dlb, tm_r)],
                            axis=1)
    c, un = _merge(b_gu2.rs_ici(cs_gu2), b_d2.rs_share(gh_d2))
    (dx1,), co = _mm("in_proj_bwd", [(dp, win, 0, k, 0) for k, dp in enumerate(dps)], tb=True, M=T, N=D, tm=tm_h,
                     tn=tnd, tk=D, nk=1, epilogue=lambda accs, ex: [accs[0] + ALPHA * ex[0]], outs=[(D, F32, 0)],
                     extras=[(dr2, "mn", 0)], carry=c)
    (r2_gu2,), (g_d2,) = un(co)
    gh_gu2 = b_gu2.rs_final("ffn2_w_gu", dwgu2[0], r1_gu2, r2_gu2, qc)
    dwin = _wgrad("dw_in_0", x1b, dps[0], M=D, N=D, T=T, tm=tnd, tn=tnd, tk=T, ncols=4 * D)
    for k in range(1, 4):
        dwin = _wgrad(f"dw_in_{k}", x1b, dps[k], M=D, N=D, T=T, tm=tnd, tn=tnd, tk=T, ncols=4 * D, into=dwin,
                      joff=k * (D // tnd))

    mix_names = ["w_in", "w_a_proj", "w_b_proj", "w_out"]
    mix_grads = dict(zip(mix_names, [dwin, dwa, dwb, dwout], strict=True))
    c, un = _merge(*[B[n].rs_pair(mix_grads[n][1]) for n in mix_names], b_gu2.rs_share(gh_gu2))
    (dr1, dr1h, dln1_g, dln1_b), co = _ln_bwd("ln1_bwd", dx1, xh1, rs1, ln1_g, 0.5, tm_ln, carry=c)
    *r1s, (g_gu2,) = un(co)
    r1_mix = {n: r1 for n, (r1,) in zip(mix_names, r1s, strict=True)}
    cs_mix = {n: B[n].rs_pairsum(n, mix_grads[n][0], r1_mix[n], qc) for n in mix_names}

    small_names = ["ln1_g", "ln1_b", "b_in", "sgu_ln_g", "sgu_ln_b", "sgu_w_s", "sgu_b_s", "conv_w_dw", "conv_b_dw",
                   "conv_ln_g", "conv_ln_b", "ln2_g", "ln2_b", "ln3_g", "ln3_b"]
    small_parts = {"ln1_g": dln1_g, "ln1_b": dln1_b, "b_in": db_in, "sgu_ln_g": dsgu_g, "sgu_ln_b": dsgu_b, "sgu_w_s": dwm,
                   "sgu_b_s": dbst.T, "conv_w_dw": dwdw[:KW], "conv_b_dw": dbdw, "conv_ln_g": dcln_g, "conv_ln_b": dcln_b,
                   "ln2_g": dln2_g, "ln2_b": dln2_b, "ln3_g": dln3_g, "ln3_b": dln3_b}
    packed, spans = _pack_rows([small_parts[n] for n in small_names])

    b_gu1, b_d1 = B["ffn1_w_gu"], B["ffn1_w_down"]
    c, un = _merge(B["w_in"].rs_ici(cs_mix["w_in"]), _small_allgather(packed))
    (dg1, du1), co = ffn_dact("ffn1", dr1h, wd1, g1, u1, carry=c)
    (r2_win,), (small_slots,) = un(co)
    c, un = _merge(*[B[n].rs_ici(cs_mix[n]) for n in mix_names[1:]])
    dwgu1, co = ffn_dwgate("ffn1", xb, dg1, carry=c)
    r2_mix = [[r2_win]] + un(co)
    gh_mix = [B[n].rs_final(n, mix_grads[n][0], r1_mix[n], r2, qc) for n, (r2,) in zip(mix_names, r2_mix, strict=True)]
    dwgu1 = ffn_dwup("ffn1", xb, du1, dwgu1)
    c, un = _merge(*[B[n].rs_share(gh) for n, gh in zip(mix_names, gh_mix, strict=True)], b_gu1.rs_pair(dwgu1[1]))
    dwd1, co = ffn_dwdown("ffn1", a1, dr1h, carry=c)
    *g_mixs, (r1_gu1,) = un(co)
    g_mix = {n: g for n, (g,) in zip(mix_names, g_mixs, strict=True)}
    cs_gu1 = b_gu1.rs_pairsum("ffn1_w_gu", dwgu1[0], r1_gu1, qc)
    c, un = _merge(b_gu1.rs_ici(cs_gu1), b_d1.rs_pair(dwd1[1]))
    (dx_gate,), co = ffn_dx("ffn1", "gate", dg1, wgu1, [], carry=c)
    (r2_gu1,), (r1_d1,) = un(co)
    cs_d1 = b_d1.rs_pairsum("ffn1_w_down", dwd1[0], r1_d1, qc)
    gh_gu1 = b_gu1.rs_final("ffn1_w_gu", dwgu1[0], r1_gu1, r2_gu1, qc)
    c, un = _merge(b_d1.rs_ici(cs_d1), b_gu1.rs_share(gh_gu1))
    (dx,), co = ffn_dx("ffn1", "up", du1, wgu1, [dr1, dx_gate], carry=c)
    (r2_d1,), (g_gu1,) = un(co)
    gh_d1 = b_d1.rs_final("ffn1_w_down", dwd1[0], r1_d1, r2_d1, qc)

    grads = {"ffn1_w_gu": g_gu1, "ffn2_w_gu": g_gu2, "ffn2_w_down": g_d2, **g_mix}
    outs_g, outs_d, outs_m, outs_v = {}, {}, {}, {}

    def adamw_big(n, g, carry=None):
        b = B[n]
        return _adamw(f"adamw_{n}", b.w, g, b.m, b.v, _tile_rows(b.sr, b.sc, 1024 * 1024), carry=carry)

    upd = {}
    upd["w_a_proj"], (grads["ffn1_w_down"],) = adamw_big("w_a_proj", grads["w_a_proj"], carry=b_d1.rs_share(gh_d1))
    for n, _ in big_names:
        if n not in upd:
            upd[n] = adamw_big(n, grads[n])
        g_, d_, m_, v_ = upd[n]
        outs_g[n], outs_d[n], outs_m[n], outs_v[n] = g_[None], d_[None], m_[None], v_[None]
    gsum = _sum8("small_sum", small_slots)
    full_shapes = [args[n].shape if n != "conv_w_dw" else (1, KW, W) for n in small_names]
    gsmall = dict(zip(small_names, _unpack_rows(gsum, spans, full_shapes), strict=True))
    cs = conv_w_dw.shape[2]
    gsmall["conv_w_dw"] = lax.dynamic_slice_in_dim(gsmall["conv_w_dw"], q * cs, cs, axis=2)
    pw, spans2 = _pack_rows([args[n] for n in small_names])
    pg, _ = _pack_rows([gsmall[n] for n in small_names])
    pm, _ = _pack_rows([args["m_" + n] for n in small_names])
    pv, _ = _pack_rows([args["v_" + n] for n in small_names])
    _, pd, pmn, pvn = _adamw("adamw_small", pw, pg, pm, pv, pw.shape[0])
    shapes2 = [args[n].shape for n in small_names]
    for dst, src in ((outs_d, pd), (outs_m, pmn), (outs_v, pvn)):
        dst.update(zip(small_names, _unpack_rows(src, spans2, shapes2), strict=True))
    outs_g.update(gsmall)

    loss = lax.psum(loss_part[0, 0], ("x", "y", "c"))
    order = ["ffn1_w_gu", "ffn1_w_down", "ln1_g", "ln1_b", "w_in", "b_in", "sgu_ln_g", "sgu_ln_b", "sgu_w_s", "sgu_b_s",
             "w_a_proj", "conv_w_dw", "conv_b_dw", "conv_ln_g", "conv_ln_b", "w_b_proj", "w_out", "ln2_g", "ln2_b",
             "ffn2_w_gu", "ffn2_w_down", "ln3_g", "ln3_b"]
    return (loss, dx[None], *[outs_g[n] for n in order], *[outs_d[n] for n in order], *[outs_m[n] for n in order],
            *[outs_v[n] for n in order])
```

```python
import math

import jax
import jax.numpy as jnp
from jax import lax
from jax.experimental import pallas as pl
from jax.experimental.pallas import tpu as pltpu

BF16 = jnp.bfloat16
F32 = jnp.float32

LN_EPS = 1e-5
ALPHA = 2.0 ** 0.25
SGU_BLOCK = 128
SGU_CHUNK = 64
HALO = 32
SUBLANES = 8
LANES = 128
CONV_ROWS = 32
ADAM_LR = 0.001
ADAM_B1 = 0.9
ADAM_B2 = 0.999
ADAM_EPS = 1e-08
ADAM_WD = 0.01
ADAM_STEP = 10
N_CHIPS = 4
VMEM_LIMIT = 52 * 1024 * 1024
MESH = pl.DeviceIdType.MESH

_GELU_C0 = math.sqrt(2.0 / math.pi)
_GELU_C1 = 0.044715


def _cparams(sem):
    return pltpu.CompilerParams(dimension_semantics=sem, vmem_limit_bytes=VMEM_LIMIT)


def _gelu_parts(x):
    x2 = x * x
    t = jnp.tanh(_GELU_C0 * (x + _GELU_C1 * (x2 * x)))
    return 0.5 * (1.0 + t), t, x2


def _gelu(x):
    cdf, _, _ = _gelu_parts(x)
    return x * cdf


def _gelu_and_grad(x):
    cdf, t, x2 = _gelu_parts(x)
    grad = cdf + x * (0.5 * (1.0 - t * t)) * (_GELU_C0 * (1.0 + (3.0 * _GELU_C1) * x2))
    return x * cdf, grad


def _silu_grad(x):
    s = jax.nn.sigmoid(x)
    return s * (1.0 + x * (1.0 - s))


def _row_stats(x):
    mu = jnp.mean(x, axis=-1, keepdims=True)
    xc = x - mu
    var = jnp.mean(xc * xc, axis=-1, keepdims=True)
    rstd = lax.rsqrt(var + LN_EPS)
    return xc * rstd, rstd


def _ln_bwd_rows(dy, xhat, rstd, g):
    dxh = dy * g
    m1 = jnp.mean(dxh, axis=-1, keepdims=True)
    m2 = jnp.mean(dxh * xhat, axis=-1, keepdims=True)
    return rstd * (dxh - m1 - xhat * m2)


def _colsum(v):
    return jnp.sum(v, axis=0, keepdims=True)


class _Carry:
    def __init__(self, ro, rw, new, plan, n_remote, n_local=0):
        self.ro, self.rw, self.new, self.plan = list(ro), list(rw), list(new), plan
        self.n_remote, self.n_local = n_remote, n_local

    def sems(self):
        return [pltpu.SemaphoreType.DMA((self.n_remote,)), pltpu.SemaphoreType.DMA((self.n_remote,)),
                pltpu.SemaphoreType.DMA((max(self.n_local, 1),))]

    def copies(self, ro_refs, rw_refs, new_refs, send_sems, recv_sems, loc_sems):
        x, y, c = lax.axis_index("x"), lax.axis_index("y"), lax.axis_index("c")
        remote, local = self.plan(ro_refs, rw_refs, new_refs, x, y, c)
        assert len(remote) == self.n_remote and len(local) == self.n_local
        lcs = [pltpu.make_async_copy(s, d, loc_sems.at[k]) for k, (s, d) in enumerate(local)]
        rcs = [pltpu.make_async_remote_copy(src_ref=s, dst_ref=d, send_sem=send_sems.at[k], recv_sem=recv_sems.at[k],
                                            device_id=dev, device_id_type=MESH) for k, (s, d, dev) in enumerate(remote)]
        return lcs, rcs

    def out_shape(self):
        return [jax.ShapeDtypeStruct(a.shape, a.dtype) for a in self.rw] + self.new


def _start_all(lcs, rcs):
    for cp in lcs + rcs:
        cp.start()


def _wait_all(lcs, rcs):
    for cp in rcs:
        cp.wait_send()
    for cp in rcs:
        cp.wait_recv()
    for cp in lcs:
        cp.wait()


def _merge(*cs):
    ro = [a for c in cs for a in c.ro]
    rw = [a for c in cs for a in c.rw]
    new = [a for c in cs for a in c.new]

    def plan(ro_refs, rw_refs, new_refs, x, y, c):
        remote, local, a, b, d = [], [], 0, 0, 0
        for cc in cs:
            r, l = cc.plan(ro_refs[a:a + len(cc.ro)], rw_refs[b:b + len(cc.rw)], new_refs[d:d + len(cc.new)], x, y, c)
            a, b, d = a + len(cc.ro), b + len(cc.rw), d + len(cc.new)
            remote += r
            local += l
        return remote, local

    def unpack(couts):
        res, b, d = [], 0, len(rw)
        for cc in cs:
            res.append(list(couts[b:b + len(cc.rw)]) + list(couts[d:d + len(cc.new)]))
            b, d = b + len(cc.rw), d + len(cc.new)
        return res

    return _Carry(ro, rw, new, plan, sum(c.n_remote for c in cs), sum(c.n_local for c in cs)), unpack


_ANY = pl.BlockSpec(memory_space=pl.ANY)


def _exchange(name, carry):
    n_ro, n_rw, n_new = len(carry.ro), len(carry.rw), len(carry.new)

    def body(*refs):
        o0 = n_ro + n_rw
        lcs, rcs = carry.copies(refs[:n_ro], refs[o0:o0 + n_rw], refs[o0 + n_rw:o0 + n_rw + n_new], *refs[o0 + n_rw + n_new:])
        _start_all(lcs, rcs)
        _wait_all(lcs, rcs)

    return list(pl.pallas_call(
        body, name=name, in_specs=[_ANY] * (n_ro + n_rw), out_specs=[_ANY] * (n_rw + n_new), out_shape=carry.out_shape(),
        input_output_aliases={n_ro + k: k for k in range(n_rw)}, scratch_shapes=carry.sems())(*carry.ro, *carry.rw))


def _call(body, *, name, grid, in_specs, out_specs, out_shape, args, scratch=(), sem, carry=None, aliases=None):
    in_specs, out_specs, out_shape, scratch = list(in_specs), list(out_specs), list(out_shape), list(scratch)
    if carry is None:
        return list(pl.pallas_call(body, name=name, grid=grid, in_specs=in_specs, out_specs=out_specs, out_shape=out_shape,
                                   scratch_shapes=scratch, input_output_aliases=aliases or {},
                                   compiler_params=_cparams(sem))(*args))
    n_in, n_out, n_scr = len(in_specs), len(out_specs), len(scratch)
    n_ro, n_rw, n_new = len(carry.ro), len(carry.rw), len(carry.new)

    def wrapped(*refs):
        ins = refs[:n_in]
        ro_refs = refs[n_in:n_in + n_ro]
        o0 = n_in + n_ro + n_rw
        outs = refs[o0:o0 + n_out]
        rw_refs = refs[o0 + n_out:o0 + n_out + n_rw]
        new_refs = refs[o0 + n_out + n_rw:o0 + n_out + n_rw + n_new]
        s0 = o0 + n_out + n_rw + n_new
        scr = refs[s0:s0 + n_scr]
        sems = refs[s0 + n_scr:]
        first = pl.program_id(0) == 0
        last = pl.program_id(0) == grid[0] - 1
        for d in range(1, len(grid)):
            first = jnp.logical_and(first, pl.program_id(d) == 0)
            last = jnp.logical_and(last, pl.program_id(d) == grid[d] - 1)

        @pl.when(first)
        def _():
            _start_all(*carry.copies(ro_refs, rw_refs, new_refs, *sems))

        body(*ins, *outs, *scr)

        @pl.when(last)
        def _():
            _wait_all(*carry.copies(ro_refs, rw_refs, new_refs, *sems))

    al = dict(aliases or {})
    al.update({n_in + n_ro + k: n_out + k for k in range(n_rw)})
    res = pl.pallas_call(
        wrapped, name=name, grid=grid, in_specs=in_specs + [_ANY] * (n_ro + n_rw), out_specs=out_specs + [_ANY] * (n_rw + n_new),
        out_shape=out_shape + carry.out_shape(), scratch_shapes=scratch + carry.sems(), input_output_aliases=al,
        compiler_params=_cparams(("arbitrary",) * len(grid)))(*args, *carry.ro, *carry.rw)
    return list(res[:n_out]), list(res[n_out:])


def _mm(name, pairs, *, ta=False, tb=False, M, N, tm, tn, tk, nk, epilogue, outs, extras=(), sum_pairs=True, carry=None,
        sub=None):
    ni, nj = M // tm, N // tn
    assert ni * tm == M and nj * tn == N
    n_p = len(pairs)
    n_acc = 1 if sum_pairs else n_p
    in_specs, args = [], []
    for a, b, ak, bk, bj in pairs:
        if ta:
            in_specs.append(pl.BlockSpec((tk, tm), lambda i, j, k, ak=ak: (k + ak, i)))
        else:
            in_specs.append(pl.BlockSpec((tm, tk), lambda i, j, k, ak=ak: (i, k + ak)))
        if tb:
            in_specs.append(pl.BlockSpec((tn, tk), lambda i, j, k, bk=bk, bj=bj: (j + bj, k + bk)))
        else:
            in_specs.append(pl.BlockSpec((tk, tn), lambda i, j, k, bk=bk, bj=bj: (k + bk, j + bj)))
        args += [a, b]
    for arr, kind, jo in extras:
        if kind == "mn":
            in_specs.append(pl.BlockSpec((tm, tn), lambda i, j, k, jo=jo: (i, j + jo)))
        else:
            in_specs.append(pl.BlockSpec((1, tn), lambda i, j, k, jo=jo: (0, j + jo)))
        args.append(arr)
    out_rows = [o[3] if len(o) > 3 else tm for o in outs]
    out_specs = [pl.BlockSpec((r, tn), lambda i, j, k, jo=o[2]: (i, j + jo)) for o, r in zip(outs, out_rows, strict=True)]
    out_shape = [jax.ShapeDtypeStruct((ni * r, o[0]), o[1]) for o, r in zip(outs, out_rows, strict=True)]
    n_ex, n_out = len(extras), len(outs)
    dn = (((0 if ta else 1,), (1 if tb else 0,)), ((), ()))

    def body(*refs):
        ab = refs[: 2 * n_p]
        ex = refs[2 * n_p: 2 * n_p + n_ex]
        o0 = 2 * n_p + n_ex
        out_refs = refs[o0: o0 + n_out]
        acc_refs = refs[o0 + n_out:]

        def dots():
            res = []
            for p in range(n_p):
                a = ab[2 * p][...]
                b = ab[2 * p + 1][...]
                res.append(lax.dot_general(a.astype(BF16), b.astype(BF16), dn, preferred_element_type=F32))
            if sum_pairs:
                tot = res[0]
                for r in res[1:]:
                    tot = tot + r
                res = [tot]
            return res

        def finish(accs):
            tiles = epilogue(accs, [e[...] for e in ex])
            for r, t in zip(out_refs, tiles, strict=True):
                r[...] = t.astype(r.dtype)

        if nk == 1 and sub is not None and tn > sub:
            for s in range(tn // sub):
                cs = slice(s * sub, (s + 1) * sub)
                res = []
                for p in range(n_p):
                    b = ab[2 * p + 1][cs, :] if tb else ab[2 * p + 1][:, cs]
                    res.append(lax.dot_general(ab[2 * p][...].astype(BF16), b.astype(BF16), dn, preferred_element_type=F32))
                if sum_pairs:
                    tot = res[0]
                    for r in res[1:]:
                        tot = tot + r
                    res = [tot]
                tiles = epilogue(res, [e[:, cs] for e in ex])
                for r, t in zip(out_refs, tiles, strict=True):
                    r[:, cs] = t.astype(r.dtype)
        elif nk == 1:
            finish(dots())
        else:
            k = pl.program_id(2)
            d = dots()

            @pl.when(k == 0)
            def _():
                for r, v in zip(acc_refs, d, strict=True):
                    r[...] = v

            @pl.when(k > 0)
            def _():
                for r, v in zip(acc_refs, d, strict=True):
                    r[...] += v

            @pl.when(k == nk - 1)
            def _():
                finish([r[...] for r in acc_refs])

    scratch = [pltpu.VMEM((tm, tn), F32) for _ in range(n_acc)] if nk > 1 else []
    return _call(body, name=name, grid=(ni, nj, nk), in_specs=in_specs, out_specs=out_specs, out_shape=out_shape, args=args,
                 scratch=scratch, sem=("parallel", "parallel", "arbitrary"), carry=carry)


def _wgrad(name, a, b, *, M, N, T, tm, tn, tk, into=None, joff=0, ncols=None, carry=None):
    ncols = N if ncols is None else ncols
    ni, nj, nk = M // tm, N // tn, T // tk
    dn = (((0,), (0,)), ((), ()))

    def body(a_ref, b_ref, *rest):
        d = lax.dot_general(a_ref[...].astype(BF16), b_ref[...].astype(BF16), dn, preferred_element_type=F32)
        if nk == 1:
            of_ref, oh_ref = rest[-2:]
            of_ref[...] = d
            oh_ref[...] = d.astype(BF16)
            return
        of_ref, oh_ref, acc_ref = rest[-3:]
        k = pl.program_id(2)

        @pl.when(k == 0)
        def _():
            acc_ref[...] = d

        @pl.when(k > 0)
        def _():
            acc_ref[...] += d

        @pl.when(k == nk - 1)
        def _():
            of_ref[...] = acc_ref[...]
            oh_ref[...] = acc_ref[...].astype(BF16)

    ospec = pl.BlockSpec((tm, tn), lambda i, j, k: (i, j + joff))
    in_specs = [pl.BlockSpec((tk, tm), lambda i, j, k: (k, i)), pl.BlockSpec((tk, tn), lambda i, j, k: (k, j))]
    args, aliases = [a, b], None
    if into is not None:
        in_specs += [_ANY, _ANY]
        args += list(into)
        aliases = {2: 0, 3: 1}
    return _call(body, name=name, grid=(ni, nj, nk), in_specs=in_specs, out_specs=[ospec, ospec],
                 out_shape=[jax.ShapeDtypeStruct((M, ncols), F32), jax.ShapeDtypeStruct((M, ncols), BF16)], args=args,
                 scratch=[pltpu.VMEM((tm, tn), F32)] if nk > 1 else [], sem=("parallel", "parallel", "arbitrary"), carry=carry,
                 aliases=aliases)


def _rows(tm, c, cb=0):
    return pl.BlockSpec((tm, c), lambda i, cb=cb: (i, cb))


def _whole(shape):
    nd = len(shape)
    return pl.BlockSpec(shape, lambda i, nd=nd: (0,) * nd)


def _cast_bf16(name, x, tm, carry=None):
    t, d = x.shape

    def body(x_ref, o_ref):
        o_ref[...] = x_ref[...].astype(BF16)

    return _call(body, name=name, grid=(t // tm,), in_specs=[_rows(tm, d)], out_specs=[_rows(tm, d)],
                 out_shape=[jax.ShapeDtypeStruct((t, d), BF16)], args=[x], sem=("parallel",), carry=carry)


def _residual(x_ref, prev_refs):
    if not prev_refs:
        return x_ref[...]
    return x_ref[...] * prev_refs[0][...] + prev_refs[1][...]


def _ln_fwd(name, xres, f, g, b, cf, tm, carry=None, prev=()):
    t, d = xres.shape
    n_prev = len(prev)

    def body(x_ref, f_ref, g_ref, b_ref, *rest):
        yb_ref, xh_ref, rs_ref = rest[n_prev:]
        r = ALPHA * _residual(x_ref, rest[:n_prev]) + cf * f_ref[...]
        xhat, rstd = _row_stats(r)
        yb_ref[...] = (xhat * g_ref[...] + b_ref[...]).astype(BF16)
        xh_ref[...] = xhat
        rs_ref[...] = rstd

    return _call(
        body, name=name, grid=(t // tm,),
        in_specs=[_rows(tm, d), _rows(tm, d), _whole((1, d)), _whole((1, d))] + [_whole((1, d))] * n_prev,
        out_specs=[_rows(tm, d), _rows(tm, d), _rows(tm, 1)],
        out_shape=[jax.ShapeDtypeStruct((t, d), BF16), jax.ShapeDtypeStruct((t, d), F32), jax.ShapeDtypeStruct((t, 1), F32)],
        args=[xres, f, g, b, *prev], sem=("parallel",), carry=carry)


def _ln_bwd(name, dy, xhat, rstd, g, scale, tm, carry=None):
    t, d = dy.shape

    def body(dy_ref, xh_ref, rs_ref, g_ref, dr_ref, drb_ref, dg_ref, db_ref):
        i = pl.program_id(0)
        dy_v, xh = dy_ref[...], xh_ref[...]
        dr = _ln_bwd_rows(dy_v, xh, rs_ref[...], g_ref[...])
        dr_ref[...] = dr
        drb_ref[...] = (scale * dr).astype(BF16)

        @pl.when(i == 0)
        def _():
            dg_ref[...] = jnp.zeros_like(dg_ref)
            db_ref[...] = jnp.zeros_like(db_ref)

        dg_ref[...] += _colsum(dy_v * xh)
        db_ref[...] += _colsum(dy_v)

    return _call(
        body, name=name, grid=(t // tm,), in_specs=[_rows(tm, d), _rows(tm, d), _rows(tm, 1), _whole((1, d))],
        out_specs=[_rows(tm, d), _rows(tm, d), _whole((1, d)), _whole((1, d))],
        out_shape=[jax.ShapeDtypeStruct((t, d), F32), jax.ShapeDtypeStruct((t, d), BF16),
                   jax.ShapeDtypeStruct((1, d), F32), jax.ShapeDtypeStruct((1, d), F32)],
        args=[dy, xhat, rstd, g], sem=("arbitrary",), carry=carry)


def _ln3_loss(name, xres, f, g, b, target, tm, prev):
    t, d = xres.shape

    def body(x_ref, f_ref, g_ref, b_ref, tg_ref, pg_ref, pb_ref, dr_ref, drb_ref, loss_ref, dg_ref, db_ref):
        i = pl.program_id(0)
        r = ALPHA * _residual(x_ref, (pg_ref, pb_ref)) + 0.5 * f_ref[...]
        xhat, rstd = _row_stats(r)
        gv = g_ref[...]
        y = xhat * gv + b_ref[...]
        err = y - tg_ref[...]
        dy = err * (1.0 / d)
        dr = _ln_bwd_rows(dy, xhat, rstd, gv)
        dr_ref[...] = dr
        drb_ref[...] = (0.5 * dr).astype(BF16)
        part = 0.5 * jnp.sum(jnp.mean(err * err, axis=-1, keepdims=True), axis=0, keepdims=True)

        @pl.when(i == 0)
        def _():
            loss_ref[...] = jnp.zeros_like(loss_ref)
            dg_ref[...] = jnp.zeros_like(dg_ref)
            db_ref[...] = jnp.zeros_like(db_ref)

        loss_ref[...] += jnp.broadcast_to(part, loss_ref.shape)
        dg_ref[...] += _colsum(dy * xhat)
        db_ref[...] += _colsum(dy)

    return _call(
        body, name=name, grid=(t // tm,),
        in_specs=[_rows(tm, d), _rows(tm, d), _whole((1, d)), _whole((1, d)), _rows(tm, d), _whole((1, d)), _whole((1, d))],
        out_specs=[_rows(tm, d), _rows(tm, d), _whole((8, 128)), _whole((1, d)), _whole((1, d))],
        out_shape=[jax.ShapeDtypeStruct((t, d), F32), jax.ShapeDtypeStruct((t, d), BF16),
                   jax.ShapeDtypeStruct((8, 128), F32), jax.ShapeDtypeStruct((1, d), F32),
                   jax.ShapeDtypeStruct((1, d), F32)],
        args=[xres, f, g, b, target, *prev], sem=("arbitrary",))


def _colsum_rows(name, x, tm):
    t, d = x.shape

    def body(x_ref, o_ref):
        @pl.when(pl.program_id(0) == 0)
        def _():
            o_ref[...] = jnp.zeros_like(o_ref)

        o_ref[...] += _colsum(x_ref[...].astype(F32))

    return _call(body, name=name, grid=(t // tm,), in_specs=[_rows(tm, d)], out_specs=[_whole((1, d))],
                 out_shape=[jax.ShapeDtypeStruct((1, d), F32)], args=[x], sem=("arbitrary",))[0]


def _sgu_mask():
    sh = SGU_CHUNK.bit_length() - 1
    r = lax.shift_right_logical(lax.broadcasted_iota(jnp.int32, (SGU_BLOCK, SGU_BLOCK), 0), sh)
    c = lax.shift_right_logical(lax.broadcasted_iota(jnp.int32, (SGU_BLOCK, SGU_BLOCK), 1), sh)
    return c <= r


def _sgu_fwd(name, p, lng, lnb, wm, bst, tm):
    t = p.shape[0]
    n_grp, w = wm.shape[0], lng.shape[1]
    hd = w // n_grp
    nblk = tm // SGU_BLOCK

    def body(uv_ref, h_ref, g_ref, b_ref, wm_ref, bs_ref, sa_ref, z_ref, vn_s):
        xhat, _ = _row_stats(_gelu(uv_ref[:, w:]))
        vn_s[...] = (xhat * g_ref[...] + b_ref[...]).astype(BF16)
        mask = _sgu_mask()
        for h in range(n_grp):
            wh = jnp.where(mask, wm_ref[h], 0.0).astype(BF16)
            bcol = bs_ref[:, h:h + 1]
            cs = slice(h * hd, (h + 1) * hd)
            for n in range(nblk):
                rs = slice(n * SGU_BLOCK, (n + 1) * SGU_BLOCK)
                s = jnp.dot(wh, vn_s[rs, cs], preferred_element_type=F32) + bcol
                sa_ref[rs, cs] = (_gelu(uv_ref[rs, cs]) * s).astype(BF16)
        z_ref[...] = h_ref[:, :w] * jax.nn.sigmoid(h_ref[:, w:])

    return _call(
        body, name=name, grid=(t // tm,),
        in_specs=[_rows(tm, 2 * w, 0), _rows(tm, 2 * w, 1), _whole((1, w)), _whole((1, w)), _whole(wm.shape),
                  _whole(bst.shape)],
        out_specs=[_rows(tm, w), _rows(tm, w)],
        out_shape=[jax.ShapeDtypeStruct((t, w), BF16), jax.ShapeDtypeStruct((t, w), F32)],
        args=[p, p, lng, lnb, wm, bst], scratch=[pltpu.VMEM((tm, w), BF16)], sem=("parallel",))


def _sgu_bwd(name, p, dsa, lng, lnb, wm, bst, tm):
    t = p.shape[0]
    n_grp, w = wm.shape[0], lng.shape[1]
    hd = w // n_grp
    nblk = tm // SGU_BLOCK

    def body(uv_ref, dsa_ref, g_ref, b_ref, wm_ref, bs_ref, dp_ref, dwm_ref, dbs_ref, dg_ref, db_ref, dbin_ref,
             vn_s, ug_s, dvn_s, dug_s):
        i = pl.program_id(0)

        @pl.when(i == 0)
        def _():
            dwm_ref[...] = jnp.zeros_like(dwm_ref)
            dbs_ref[...] = jnp.zeros_like(dbs_ref)
            dg_ref[...] = jnp.zeros_like(dg_ref)
            db_ref[...] = jnp.zeros_like(db_ref)
            dbin_ref[...] = jnp.zeros_like(dbin_ref)

        ug, dgelu_u = _gelu_and_grad(uv_ref[:, :w])
        ug_s[...] = ug
        vg, dgelu_v = _gelu_and_grad(uv_ref[:, w:])
        xhat, rstd = _row_stats(vg)
        gv = g_ref[...]
        vn_s[...] = (xhat * gv + b_ref[...]).astype(BF16)
        mask = _sgu_mask()
        for h in range(n_grp):
            wh = jnp.where(mask, wm_ref[h], 0.0).astype(BF16)
            bcol = bs_ref[:, h:h + 1]
            cs = slice(h * hd, (h + 1) * hd)
            dw_h = jnp.zeros((SGU_BLOCK, SGU_BLOCK), F32)
            dbs_h = jnp.zeros((SGU_BLOCK, 1), F32)
            for n in range(nblk):
                rs = slice(n * SGU_BLOCK, (n + 1) * SGU_BLOCK)
                vblk = vn_s[rs, cs]
                s = jnp.dot(wh, vblk, preferred_element_type=F32) + bcol
                dsa_blk = dsa_ref[rs, cs]
                dug_s[rs, cs] = dsa_blk * s
                ds = dsa_blk * ug_s[rs, cs]
                dsb = ds.astype(BF16)
                dvn_s[rs, cs] = lax.dot_general(wh, dsb, (((0,), (0,)), ((), ())), preferred_element_type=F32)
                dw_h = dw_h + lax.dot_general(dsb, vblk, (((1,), (1,)), ((), ())), preferred_element_type=F32)
                dbs_h = dbs_h + jnp.sum(ds, axis=1, keepdims=True)
            dwm_ref[h] += jnp.where(mask, dw_h, 0.0)
            dbs_ref[:, h:h + 1] += dbs_h
        dvn = dvn_s[...]
        dg_ref[...] += _colsum(dvn * xhat)
        db_ref[...] += _colsum(dvn)
        dvg = _ln_bwd_rows(dvn, xhat, rstd, gv)
        du = dug_s[...] * dgelu_u
        dv = dvg * dgelu_v
        dp_ref[:, :w] = du.astype(BF16)
        dp_ref[:, w:] = dv.astype(BF16)
        dbin_ref[:, :w] += _colsum(du)
        dbin_ref[:, w:] += _colsum(dv)

    return _call(
        body, name=name, grid=(t // tm,),
        in_specs=[_rows(tm, 2 * w, 0), _rows(tm, w), _whole((1, w)), _whole((1, w)), _whole(wm.shape), _whole(bst.shape)],
        out_specs=[_rows(tm, 2 * w), _whole(wm.shape), _whole(bst.shape), _whole((1, w)), _whole((1, w)), _whole((1, 2 * w))],
        out_shape=[jax.ShapeDtypeStruct((t, 2 * w), BF16), jax.ShapeDtypeStruct(wm.shape, F32),
                   jax.ShapeDtypeStruct(bst.shape, F32), jax.ShapeDtypeStruct((1, w), F32), jax.ShapeDtypeStruct((1, w), F32),
                   jax.ShapeDtypeStruct((1, 2 * w), F32)],
        args=[p, dsa, lng, lnb, wm, bst],
        scratch=[pltpu.VMEM((tm, w), BF16), pltpu.VMEM((tm, w), F32), pltpu.VMEM((tm, w), F32), pltpu.VMEM((tm, w), F32)],
        sem=("arbitrary",))


def _halo_prev(tm, c):
    return pl.BlockSpec((HALO, c), lambda i: (jnp.maximum(i * (tm // HALO) - 1, 0), 0))


def _halo_next(tm, c, t):
    last = t // HALO - 1
    return pl.BlockSpec((HALO, c), lambda i: (jnp.minimum((i + 1) * (tm // HALO), last), 0))


def _shifted_copies(sh, n):
    for r in range(1, SUBLANES):
        sh[r, :n - SUBLANES, :] = sh[0, r:r + n - SUBLANES, :]


def _row_broadcasts(wb, w_ref, kw):
    for k in range(kw):
        wb[k] = jnp.broadcast_to(w_ref[k:k + 1, :], wb.shape[1:])


def _tap(sh, r0, o, rows, cols):
    return sh[o % SUBLANES, pl.ds(pl.multiple_of(r0 + (o - o % SUBLANES), SUBLANES), rows), cols]


def _conv_fwd(name, z, wdw, kw, bdw, lng, lnb, tm, carry=None):
    t, c = z.shape
    lead = HALO - (kw - 1)
    n = tm + HALO

    def body(zp_ref, z_ref, w_ref, bdw_ref, g_ref, b_ref, zc_ref, sb_ref, sh, wb):
        i = pl.program_id(0)
        sh[0, :HALO, :] = jnp.where(i > 0, zp_ref[...], 0.0)
        sh[0, HALO:, :] = z_ref[...]
        _shifted_copies(sh, n)
        _row_broadcasts(wb, w_ref, kw)
        bias = jnp.broadcast_to(bdw_ref[...], (SUBLANES, c))
        groups = CONV_ROWS // SUBLANES

        def chunk(ci, _):
            r0 = pl.multiple_of(ci * CONV_ROWS, CONV_ROWS)
            accs = [bias] * groups
            for k in range(kw):
                wk = wb[k]
                tp = _tap(sh, r0, lead + k, CONV_ROWS, slice(None))
                accs = [accs[g] + wk * tp[g * SUBLANES:(g + 1) * SUBLANES] for g in range(groups)]
            zc_ref[pl.ds(r0, CONV_ROWS), :] = jnp.concatenate(accs, axis=0)
            return 0

        lax.fori_loop(0, tm // CONV_ROWS, chunk, 0)
        xhat, _ = _row_stats(zc_ref[...])
        zn = xhat * g_ref[...] + b_ref[...]
        sb_ref[...] = (zn * jax.nn.sigmoid(zn)).astype(BF16)

    return _call(
        body, name=name, grid=(t // tm,),
        in_specs=[_halo_prev(tm, c), _rows(tm, c), _whole(wdw.shape), _whole((1, c)), _whole((1, c)), _whole((1, c))],
        out_specs=[_rows(tm, c), _rows(tm, c)],
        out_shape=[jax.ShapeDtypeStruct((t, c), F32), jax.ShapeDtypeStruct((t, c), BF16)],
        args=[z, z, wdw, bdw, lng, lnb], scratch=[pltpu.VMEM((SUBLANES, n, c), F32), pltpu.VMEM((HALO, SUBLANES, c), F32)],
        sem=("parallel",), carry=carry)


def _conv_ln_bwd(name, dsb, zc, lng, lnb, tm):
    t, c = zc.shape

    def body(dsb_ref, zc_ref, g_ref, b_ref, dzc_ref, dg_ref, db_ref, dbdw_ref):
        i = pl.program_id(0)
        xhat, rstd = _row_stats(zc_ref[...])
        gv = g_ref[...]
        zn = xhat * gv + b_ref[...]
        dzn = dsb_ref[...] * _silu_grad(zn)
        dzc = _ln_bwd_rows(dzn, xhat, rstd, gv)
        dzc_ref[...] = dzc

        @pl.when(i == 0)
        def _():
            dg_ref[...] = jnp.zeros_like(dg_ref)
            db_ref[...] = jnp.zeros_like(db_ref)
            dbdw_ref[...] = jnp.zeros_like(dbdw_ref)

        dg_ref[...] += _colsum(dzn * xhat)
        db_ref[...] += _colsum(dzn)
        dbdw_ref[...] += _colsum(dzc)

    return _call(
        body, name=name, grid=(t // tm,), in_specs=[_rows(tm, c), _rows(tm, c), _whole((1, c)), _whole((1, c))],
        out_specs=[_rows(tm, c), _whole((1, c)), _whole((1, c)), _whole((1, c))],
        out_shape=[jax.ShapeDtypeStruct((t, c), F32)] + [jax.ShapeDtypeStruct((1, c), F32)] * 3,
        args=[dsb, zc, lng, lnb], sem=("arbitrary",))


def _conv_bwd(name, dzc, z, p, wdw, kw, tm):
    t, c = z.shape
    n_i = t // tm
    n = tm + HALO

    def body(dzc_ref, dzn_ref, z_ref, h_ref, w_ref, dp_ref, dw_ref, dbin_ref, sh, dz_s, wb):
        i = pl.program_id(0)

        @pl.when(i == 0)
        def _():
            dw_ref[...] = jnp.zeros_like(dw_ref)

        sh[0, :tm, :] = dzc_ref[...]
        sh[0, tm:, :] = jnp.where(i < n_i - 1, dzn_ref[...], 0.0)
        _shifted_copies(sh, n)
        _row_broadcasts(wb, w_ref, kw)
        nv = 4
        rows = nv * SUBLANES
        ways = 2
        for lc in range(c // LANES):
            cols = slice(lc * LANES, (lc + 1) * LANES)

            def rowv(rv, accs, cols=cols):
                r0 = pl.multiple_of(rv * rows, rows)
                zv = z_ref[pl.ds(r0, rows), cols]
                zs = [zv[v * SUBLANES:(v + 1) * SUBLANES] for v in range(nv)]
                dz = [[None] * ways for _ in range(nv)]
                new = []
                for k in range(kw):
                    s = _tap(sh, r0, kw - 1 - k, rows, cols)
                    wk = wb[k, :, cols]
                    j = k % ways
                    acc = accs[k]
                    for v in range(nv):
                        sv = s[v * SUBLANES:(v + 1) * SUBLANES]
                        dz[v][j] = wk * sv if dz[v][j] is None else dz[v][j] + wk * sv
                        acc = acc + zs[v] * sv
                    new.append(acc)
                dz_s[pl.ds(r0, rows), cols] = jnp.concatenate([dz[v][0] + dz[v][1] for v in range(nv)], axis=0)
                return tuple(new)

            accs = lax.fori_loop(0, tm // rows, rowv, tuple(jnp.zeros((SUBLANES, LANES), F32) for _ in range(kw)))
            for k in range(kw):
                dw_ref[k:k + 1, cols] += _colsum(accs[k])
        dz = dz_s[...]
        a, g = h_ref[:, :c], h_ref[:, c:]
        sg = jax.nn.sigmoid(g)
        da = dz * sg
        dg = dz * a * (sg * (1.0 - sg))
        dp_ref[:, :c] = da.astype(BF16)
        dp_ref[:, c:] = dg.astype(BF16)

        @pl.when(i == 0)
        def _():
            dbin_ref[...] = jnp.zeros_like(dbin_ref)

        dbin_ref[:, :c] += _colsum(da)
        dbin_ref[:, c:] += _colsum(dg)

    return _call(
        body, name=name, grid=(n_i,),
        in_specs=[_rows(tm, c), _halo_next(tm, c, t), _rows(tm, c), _rows(tm, 2 * c, 1), _whole(wdw.shape)],
        out_specs=[_rows(tm, 2 * c), _whole((HALO, c)), _whole((1, 2 * c))],
        out_shape=[jax.ShapeDtypeStruct((t, 2 * c), BF16), jax.ShapeDtypeStruct((HALO, c), F32),
                   jax.ShapeDtypeStruct((1, 2 * c), F32)],
        args=[dzc, dzc, z, p, wdw],
        scratch=[pltpu.VMEM((SUBLANES, n, c), F32), pltpu.VMEM((tm, c), F32), pltpu.VMEM((HALO, SUBLANES, c), F32)],
        sem=("arbitrary",))


def _adamw(name, w, g, m, v, tr, carry=None):
    r, c = w.shape
    c1 = 1.0 - ADAM_B1 ** ADAM_STEP
    c2 = 1.0 - ADAM_B2 ** ADAM_STEP

    def body(w_ref, g_ref, m_ref, v_ref, go_ref, d_ref, mo_ref, vo_ref):
        gv = g_ref[...]
        mn = ADAM_B1 * m_ref[...] + (1.0 - ADAM_B1) * gv
        vn = ADAM_B2 * v_ref[...] + (1.0 - ADAM_B2) * (gv * gv)
        go_ref[...] = gv
        d_ref[...] = -ADAM_LR * ((mn / c1) / (jnp.sqrt(vn / c2) + ADAM_EPS) + ADAM_WD * w_ref[...])
        mo_ref[...] = mn
        vo_ref[...] = vn

    spec = _rows(tr, c)
    return _call(body, name=name, grid=(r // tr,), in_specs=[spec] * 4, out_specs=[spec] * 4,
                 out_shape=[jax.ShapeDtypeStruct((r, c), F32)] * 4, args=[w, g, m, v], sem=("parallel",), carry=carry)


_RELATION_XOR = (2, 1, 3)


def _pair_sum(name, mine, recv, qc, *, part_shape, tr, in_map):
    pr, pc = part_shape

    def body(qc_ref, a_ref, b_ref, oh_ref):
        oh_ref[...] = (a_ref[...] + b_ref[...].astype(F32)).astype(BF16)

    def imap(k, i, qc_ref):
        part = qc_ref[0]
        for kk, m in enumerate(_RELATION_XOR):
            part = jnp.where(k == kk, jnp.bitwise_xor(qc_ref[0], m), part)
        return in_map(i, part, qc_ref[1])

    ispec = pl.BlockSpec((tr, pc), imap)
    ospec = pl.BlockSpec((None, tr, pc), lambda k, i, qc_ref: (k, i, 0))
    return pl.pallas_call(
        body, name=name,
        grid_spec=pltpu.PrefetchScalarGridSpec(num_scalar_prefetch=1, grid=(3, pr // tr), in_specs=[ispec, ispec],
                                               out_specs=ospec),
        out_shape=jax.ShapeDtypeStruct((3, pr, pc), BF16), compiler_params=_cparams(("parallel", "parallel")))(qc, mine, recv)


def _final_sum(name, mine, recv1, recv2, qc, *, part_shape, out_shape, tr, in_map, out_map):
    pr, pc = part_shape

    def body(qc_ref, a_ref, b_ref, r_ref, o_ref):
        own = a_ref[...] + b_ref[...].astype(F32)
        o_ref[...] = ((own + r_ref[0].astype(F32)) + r_ref[1].astype(F32)) + r_ref[2].astype(F32)

    ispec = pl.BlockSpec((tr, pc), lambda i, qc_ref: in_map(i, qc_ref[0], qc_ref[1]))
    return pl.pallas_call(
        body, name=name,
        grid_spec=pltpu.PrefetchScalarGridSpec(
            num_scalar_prefetch=1, grid=(pr // tr,),
            in_specs=[ispec, ispec, pl.BlockSpec((3, tr, pc), lambda i, qc_ref: (0, i, 0))],
            out_specs=pl.BlockSpec((tr, pc), lambda i, qc_ref: out_map(i, qc_ref[0], qc_ref[1]))),
        out_shape=jax.ShapeDtypeStruct(out_shape, F32), compiler_params=_cparams(("parallel",)))(qc, mine, recv1, recv2)


def _sum8(name, slots):
    def body(s_ref, o_ref):
        acc = s_ref[0]
        for d in range(1, 8):
            acc = acc + s_ref[d]
        o_ref[...] = acc

    return pl.pallas_call(body, name=name, out_shape=jax.ShapeDtypeStruct(slots.shape[1:], F32),
                          in_specs=[pl.BlockSpec(memory_space=pltpu.VMEM)], out_specs=pl.BlockSpec(memory_space=pltpu.VMEM),
                          compiler_params=pltpu.CompilerParams(vmem_limit_bytes=VMEM_LIMIT))(slots)


def _chips(x, y):
    return [(1 - x, y), (x, 1 - y), (1 - x, 1 - y)]


class _Big:
    def __init__(self, name, w, m, v, ax):
        self.name, self.w, self.m, self.v, self.ax = name, w, m, v, ax
        sr, sc = w.shape
        self.R, self.C = (sr * N_CHIPS, sc) if ax == 0 else (sr, sc * N_CHIPS)
        self.sr, self.sc = sr, sc
        self.hr = sr // 2

    def slot(self, ref, q, h=None):
        if self.ax == 1:
            cols = pl.ds(pl.multiple_of(q * self.sc, 128), self.sc)
            return ref.at[:, cols] if h is None else ref.at[pl.ds(pl.multiple_of(h * self.hr, 16), self.hr), cols]
        if h is None:
            return ref.at[pl.ds(pl.multiple_of(q * self.sr, 16), self.sr), :]
        return ref.at[pl.ds(pl.multiple_of(q * self.sr + h * self.hr, 16), self.hr), :]

    def half(self, ref, h):
        return ref.at[pl.ds(pl.multiple_of(h * self.hr, 16), self.hr), :]

    @property
    def part_shape(self):
        return (self.hr, self.sc)

    def cast_into_full(self, qc):
        tr = _tile_rows(self.sr, self.sc)
        nb = self.sr // tr

        def body(qc_ref, x_ref, o_ref):
            o_ref[...] = x_ref[...].astype(BF16)

        if self.ax == 1:
            ospec = pl.BlockSpec((tr, self.sc), lambda i, qc_ref: (i, qc_ref[0]))
        else:
            ospec = pl.BlockSpec((tr, self.sc), lambda i, qc_ref: (qc_ref[0] * nb + i, 0))
        return pl.pallas_call(
            body, name=f"cast_{self.name}",
            grid_spec=pltpu.PrefetchScalarGridSpec(num_scalar_prefetch=1, grid=(nb,),
                                                   in_specs=[pl.BlockSpec((tr, self.sc), lambda i, qc_ref: (i, 0))],
                                                   out_specs=ospec),
            out_shape=jax.ShapeDtypeStruct((self.R, self.C), BF16), compiler_params=_cparams(("parallel",)))(qc, self.w)

    def gather_ici(self, full, piece=(0, 1)):
        k, n = piece
        pr = self.hr // n
        assert pr * n == self.hr and pr % 16 == 0

        def plan(ro, rw, new, x, y, c):
            q = 2 * x + y
            r0 = c * self.hr + k * pr
            if self.ax == 1:
                mine = rw[0].at[pl.ds(pl.multiple_of(r0, 16), pr), pl.ds(pl.multiple_of(q * self.sc, 128), self.sc)]
            else:
                mine = rw[0].at[pl.ds(pl.multiple_of(q * self.sr + r0, 16), pr), :]
            return [(mine, mine, (cx, cy, c)) for cx, cy in _chips(x, y)], []

        return _Carry([], [full], [], plan, 3, 0)

    def gather_d2d(self, full):
        def plan(ro, rw, new, x, y, c):
            remote = []
            for cx, cy in _chips(x, y):
                piece = self.slot(rw[0], 2 * cx + cy, c)
                remote.append((piece, piece, (x, y, 1 - c)))
            return remote, []

        return _Carry([], [full], [], plan, 3)

    def rs_pair(self, g16):
        def plan(ro, rw, new, x, y, c):
            sib = (x, y, 1 - c)
            if self.ax == 1:
                rows = pl.ds(pl.multiple_of((1 - c) * self.hr, 16), self.hr)
                return [(ro[0].at[rows, :], new[0].at[rows, :], sib)], []
            return [(self.slot(ro[0], q, 1 - c), self.slot(new[0], q, 1 - c), sib) for q in range(N_CHIPS)], []

        return _Carry([g16], [], [jax.ShapeDtypeStruct((self.R, self.C), BF16)], plan, 1 if self.ax == 1 else N_CHIPS)

    def _piece_map(self, nb):
        if self.ax == 1:
            return lambda i, q, c: (c * nb + i, q)
        return lambda i, q, c: ((q * 2 + c) * nb + i, 0)

    def rs_pairsum(self, tag, g32, recv1, qc):
        tr = _tile_rows(self.hr, self.sc)
        return _pair_sum(f"rs_pairsum_{tag}", g32, recv1, qc, part_shape=self.part_shape, tr=tr,
                         in_map=self._piece_map(self.hr // tr))

    def rs_ici(self, cs16):
        def plan(ro, rw, new, x, y, c):
            return [(ro[0].at[k], new[0].at[k], (cx, cy, c)) for k, (cx, cy) in enumerate(_chips(x, y))], []

        return _Carry([cs16], [], [jax.ShapeDtypeStruct((3,) + self.part_shape, BF16)], plan, 3)

    def rs_final(self, tag, g32, recv1, recv2, qc):
        tr = _tile_rows(self.hr, self.sc)
        nb = self.hr // tr
        in_map = self._piece_map(nb)
        out_map = lambda i, q, c: (c * nb + i, 0)
        return _final_sum(f"rs_final_{tag}", g32, recv1, recv2, qc, part_shape=self.part_shape, out_shape=(self.sr, self.sc),
                          tr=tr, in_map=in_map, out_map=out_map)

    def rs_share(self, ghalf):
        def plan(ro, rw, new, x, y, c):
            piece = self.half(rw[0], c)
            return [(piece, piece, (x, y, 1 - c))], []

        return _Carry([], [ghalf], [], plan, 1)


def _small_allgather(packed):
    nr = packed.shape[0]

    def plan(ro, rw, new, x, y, c):
        me = 4 * x + 2 * y + c
        remote = []
        for fx in (0, 1):
            for fy in (0, 1):
                for fc in (0, 1):
                    if fx or fy or fc:
                        dev = (1 - x if fx else x, 1 - y if fy else y, 1 - c if fc else c)
                        remote.append((ro[0], new[0].at[me], dev))
        return remote, [(ro[0], new[0].at[me])]

    return _Carry([packed], [], [jax.ShapeDtypeStruct((8, nr, 128), F32)], plan, 7, 1)


def _conv_w_allgather(padded, cs):
    def plan(ro, rw, new, x, y, c):
        cols = pl.ds(pl.multiple_of((2 * x + y) * cs, 128), cs)
        remote = [(ro[0], new[0].at[:, cols], (cx, cy, c)) for cx, cy in _chips(x, y)]
        return remote, [(ro[0], new[0].at[:, cols])]

    return _Carry([padded], [], [jax.ShapeDtypeStruct((HALO, cs * N_CHIPS), F32)], plan, 3, 1)


def _pick(n, want):
    if n <= want:
        return n
    for t in range(want, 15, -16):
        if t % 16 == 0 and n % t == 0:
            return t
    raise ValueError(f"no tile for {n} (want {want})")


def _tile_rows(nrows, ncols, budget=2 * 1024 * 1024):
    return _pick(nrows, max(16, (budget // (4 * ncols)) // 16 * 16))


def _pick128(n, want):
    if n <= want:
        return n
    for t in range(want, 127, -128):
        if n % t == 0:
            return t
    raise ValueError(f"no lane tile for {n} (want {want})")


def _pack_rows(parts):
    out, spans, r0 = [], [], 0
    for p in parts:
        flat = p.reshape(-1).astype(F32)
        n = flat.shape[0]
        rows = -(-n // 1024) * 8
        flat = jnp.pad(flat, (0, rows * 128 - n))
        out.append(flat.reshape(rows, 128))
        spans.append((r0, rows, n))
        r0 += rows
    return jnp.concatenate(out, axis=0), spans


def _unpack_rows(packed, spans, shapes):
    res = []
    for (r0, rows, n), shp in zip(spans, shapes, strict=True):
        res.append(packed[r0:r0 + rows].reshape(-1)[:n].reshape(shp))
    return res


def _ident(accs, ex):
    return [accs[0]]


def kernel(x, ffn1_w_gu, ffn1_w_down, ln1_g, ln1_b, w_in, b_in, sgu_ln_g, sgu_ln_b, sgu_w_s, sgu_b_s, w_a_proj, conv_w_dw, conv_b_dw, conv_ln_g, conv_ln_b, w_b_proj, w_out, ln2_g, ln2_b, ffn2_w_gu, ffn2_w_down, ln3_g, ln3_b, loss_target, m_ffn1_w_gu, m_ffn1_w_down, m_ln1_g, m_ln1_b, m_w_in, m_b_in, m_sgu_ln_g, m_sgu_ln_b, m_sgu_w_s, m_sgu_b_s, m_w_a_proj, m_conv_w_dw, m_conv_b_dw, m_conv_ln_g, m_conv_ln_b, m_w_b_proj, m_w_out, m_ln2_g, m_ln2_b, m_ffn2_w_gu, m_ffn2_w_down, m_ln3_g, m_ln3_b, v_ffn1_w_gu, v_ffn1_w_down, v_ln1_g, v_ln1_b, v_w_in, v_b_in, v_sgu_ln_g, v_sgu_ln_b, v_sgu_w_s, v_sgu_b_s, v_w_a_proj, v_conv_w_dw, v_conv_b_dw, v_conv_ln_g, v_conv_ln_b, v_w_b_proj, v_w_out, v_ln2_g, v_ln2_b, v_ffn2_w_gu, v_ffn2_w_down, v_ln3_g, v_ln3_b):
    args = dict(locals())
    assert x.shape[0] == 1 and ffn1_w_gu.shape[0] == 1
    T, D = x.shape[1], x.shape[2]
    F = ffn1_w_down.shape[1] * N_CHIPS
    W = sgu_ln_g.shape[1]
    KW = conv_w_dw.shape[1]
    assert KW - 1 <= HALO and T % SGU_BLOCK == 0

    mx, my, mc = lax.axis_index("x"), lax.axis_index("y"), lax.axis_index("c")
    q = 2 * mx + my
    qc = jnp.stack([q, mc]).astype(jnp.int32)

    big_names = [("ffn1_w_gu", 1), ("ffn1_w_down", 0), ("w_in", 1), ("w_a_proj", 1), ("w_b_proj", 1), ("w_out", 0),
                 ("ffn2_w_gu", 1), ("ffn2_w_down", 0)]
    B = {n: _Big(n, args[n][0], args["m_" + n][0], args["v_" + n][0], ax) for n, ax in big_names}
    own = {n: b.cast_into_full(qc) for n, b in B.items()}

    x2d = x[0]
    tgt = loss_target[0]
    tm_r = _pick(T, 256)
    tm_ln = _pick(T, 512)
    tm = _pick(T, 1024)
    tn = _pick128(D, 1024)
    nj = D // tn
    tng = _pick128(D, 512)
    njg = D // tng
    tnf = _pick128(F, 512)
    nf = F // tnf
    tnw = _pick128(W, 1024)
    tnd = _pick128(D, 512)
    tmw = _pick128(W, 512)
    SUB = 256

    def ffn_up(tag, xb_, wgu, carry=None):
        def epi(accs, ex):
            g, u = accs
            s = jax.nn.sigmoid(g)
            sg = g * s
            return [u * (s * (1.0 + g * (1.0 - s))), sg, sg * u]

        return _mm(f"{tag}_up", [(xb_, wgu, 0, 0, 0), (xb_, wgu, 0, 0, nf)], M=T, N=F, tm=tm, tn=tnf, tk=D, nk=1, epilogue=epi,
                   outs=[(F, BF16, 0)] * 3, sum_pairs=False, carry=carry, sub=SUB)

    def ffn_down(tag, act, wd, carry=None):
        return _mm(f"{tag}_down", [(act, wd, 0, 0, 0)], M=T, N=D, tm=tm, tn=_pick128(D, 512), tk=F, nk=1, epilogue=_ident,
                   outs=[(D, F32, 0)], carry=carry)

    def ffn_dact(tag, drh, wd, dgate_f, dup_f, carry=None):
        def epi(accs, ex):
            da = accs[0]
            return [da * ex[0].astype(F32), da * ex[1].astype(F32)]

        return _mm(f"{tag}_dact", [(drh, wd, 0, 0, 0)], tb=True, M=T, N=F, tm=_pick(T, 2048), tn=tnf, tk=D, nk=1, epilogue=epi,
                   outs=[(F, BF16, 0)] * 2, extras=[(dgate_f, "mn", 0), (dup_f, "mn", 0)], carry=carry, sub=SUB)

    def ffn_dwdown(tag, act, drh, carry=None):
        return _wgrad(f"{tag}_dwdown", act, drh, M=F, N=D, T=T, tm=tnf, tn=tnd, tk=T, carry=carry)

    def ffn_dwgate(tag, xb_, dg, carry=None):
        return _wgrad(f"{tag}_dwgate", xb_, dg, M=D, N=F, T=T, tm=tnd, tn=tnf, tk=T, ncols=2 * F, carry=carry)

    def ffn_dwup(tag, xb_, du, into, carry=None):
        return _wgrad(f"{tag}_dwup", xb_, du, M=D, N=F, T=T, tm=tnd, tn=tnf, tk=T, ncols=2 * F, into=into, joff=nf, carry=carry)

    def ffn_dx(tag, which, da, wgu, addends, carry=None):
        def epi(accs, ex):
            tot = accs[0] + ALPHA * ex[0]
            for e in ex[1:]:
                tot = tot + e
            return [tot]

        return _mm(f"{tag}_dx_{which}", [(da, wgu, 0, 1 if which == "up" else 0, 0)], tb=True, M=T, N=D, tm=tm,
                   tn=_pick128(D, 256), tk=F, nk=1, epilogue=epi if addends else _ident, outs=[(D, F32, 0)],
                   extras=[(a, "mn", 0) for a in addends], carry=carry)

    wdw_pad = jnp.pad(conv_w_dw[0], ((0, HALO - KW), (0, 0)))
    c0, un = _merge(B["ffn1_w_gu"].gather_ici(own["ffn1_w_gu"]), _conv_w_allgather(wdw_pad, conv_w_dw.shape[2]))
    (wgu1,), (wdw_full,) = un(_exchange("gather_first", c0))
    (xb,), (wgu1,) = _cast_bf16("cast_x", x2d, tm_r, carry=B["ffn1_w_gu"].gather_d2d(wgu1))

    c, un = _merge(B["ffn1_w_down"].gather_ici(own["ffn1_w_down"]), B["w_in"].gather_ici(own["w_in"]))
    (g1, u1, a1), co = ffn_up("ffn1", xb, wgu1, carry=c)
    (wd1,), (win,) = un(co)
    (wd1,) = _exchange("gather_d2d_ffn1_w_down", B["ffn1_w_down"].gather_d2d(wd1))
    b_gu2, b_d2 = B["ffn2_w_gu"], B["ffn2_w_down"]
    c, un = _merge(B["w_a_proj"].gather_ici(own["w_a_proj"]), B["w_b_proj"].gather_ici(own["w_b_proj"]),
                   B["w_out"].gather_ici(own["w_out"]), B["w_in"].gather_d2d(win), b_gu2.gather_ici(own["ffn2_w_gu"], (0, 4)))
    (fo1,), co = ffn_down("ffn1", a1, wd1, carry=c)
    (wa,), (wb,), (wout,), (win,), (wgu2,) = un(co)
    c, un = _merge(B["w_a_proj"].gather_d2d(wa), B["w_b_proj"].gather_d2d(wb), B["w_out"].gather_d2d(wout),
                   b_gu2.gather_ici(wgu2, (1, 4)))
    (x1b, xh1, rs1), co = _ln_fwd("ln1", x2d, fo1, ln1_g, ln1_b, 0.5, tm_ln, carry=c)
    (wa,), (wb,), (wout,), (wgu2,) = un(co)

    c, un = _merge(b_gu2.gather_ici(wgu2, (1, 2)), b_d2.gather_ici(own["ffn2_w_down"], (0, 2)))
    (proj,), co = _mm("in_proj", [(x1b, win, 0, 0, 0)], M=T, N=4 * D, tm=tm, tn=tn, tk=D, nk=1,
                      epilogue=lambda accs, ex: [accs[0] + ex[0]], outs=[(4 * D, F32, 0)], extras=[(b_in, "n", 0)], carry=c)
    (wgu2,), (wd2,) = un(co)
    wm = sgu_w_s[0]
    bst = sgu_b_s[0].T
    sa, z = _sgu_fwd("sgu_fwd", proj, sgu_ln_g, sgu_ln_b, wm, bst, tm_r)
    (zc, sb), (wd2,) = _conv_fwd("conv_fwd", z, wdw_full, KW, conv_b_dw, conv_ln_g, conv_ln_b, tm_r,
                                 carry=b_d2.gather_ici(wd2, (1, 2)))

    def epi_mix(accs, ex):
        ya_, yb_ = accs
        ga, gb = jax.nn.sigmoid(ex[0]), jax.nn.sigmoid(ex[1])
        return [ga * ya_ + gb * yb_, ga, gb, ya_ * (ga * (1.0 - ga)), yb_ * (gb * (1.0 - gb))]

    mixin, gate_a, gate_b, dlog_a, dlog_b = _mm(
        "branch_proj", [(sa, wa, 0, 0, 0), (sb, wb, 0, 0, 0)], M=T, N=D, tm=tm, tn=tng, tk=W, nk=1, epilogue=epi_mix,
        outs=[(D, BF16, 0)] * 5, sum_pairs=False, sub=SUB, extras=[(proj, "mn", 2 * njg), (proj, "mn", 3 * njg)])
    c, un = _merge(b_gu2.gather_d2d(wgu2), b_d2.gather_d2d(wd2))
    (mix,), co = _mm("out_proj", [(mixin, wout, 0, 0, 0)], M=T, N=D, tm=tm, tn=tn, tk=D, nk=1, epilogue=_ident,
                     outs=[(D, F32, 0)], carry=c)
    (wgu2,), (wd2,) = un(co)
    x2b, xh2, rs2 = _ln_fwd("ln2", xh1, mix, ln2_g, ln2_b, 1.0, tm_ln, prev=(ln1_g, ln1_b))
    g2, u2, a2 = ffn_up("ffn2", x2b, wgu2)
    (fo2,) = ffn_down("ffn2", a2, wd2)
    dr3, dr3h, loss_part, dln3_g, dln3_b = _ln3_loss("ln3_loss", xh2, fo2, ln3_g, ln3_b, tgt, tm_r, (ln2_g, ln2_b))

    b_gu2, b_d2 = B["ffn2_w_gu"], B["ffn2_w_down"]
    dg2, du2 = ffn_dact("ffn2", dr3h, wd2, g2, u2)
    dwd2 = ffn_dwdown("ffn2", a2, dr3h)
    dwgu2, (r1_d2,) = ffn_dwgate("ffn2", x2b, dg2, carry=b_d2.rs_pair(dwd2[1]))
    dwgu2 = ffn_dwup("ffn2", x2b, du2, dwgu2)
    cs_d2 = b_d2.rs_pairsum("ffn2_w_down", dwd2[0], r1_d2, qc)
    c, un = _merge(b_gu2.rs_pair(dwgu2[1]), b_d2.rs_ici(cs_d2))
    (dx2_gate,), co = ffn_dx("ffn2", "gate", dg2, wgu2, [], carry=c)
    (r1_gu2,), (r2_d2,) = un(co)
    (dx2,) = ffn_dx("ffn2", "up", du2, wgu2, [dr3, dx2_gate])
    cs_gu2 = b_gu2.rs_pairsum("ffn2_w_gu", dwgu2[0], r1_gu2, qc)
    gh_d2 = b_d2.rs_final("ffn2_w_down", dwd2[0], r1_d2, r2_d2, qc)

    dr2, dr2b, dln2_g, dln2_b = _ln_bwd("ln2_bwd", dx2, xh2, rs2, ln2_g, 1.0, tm_ln)

    def epi_dmix(accs, ex):
        tiles = [accs[0] * e.astype(F32) for e in ex]
        sums = [t.reshape(-1, SUBLANES, t.shape[1]).sum(axis=0) for t in tiles[2:]]
        return tiles + sums

    dya, dyb, dla, dlb, sla, slb = _mm(
        "out_proj_bwd", [(dr2b, wout, 0, 0, 0)], tb=True, M=T, N=D, tm=tm, tn=tng, tk=D, nk=1, epilogue=epi_dmix,
        outs=[(D, BF16, 0)] * 4 + [(D, F32, 0, SUBLANES)] * 2, sub=SUB,
        extras=[(gate_a, "mn", 0), (gate_b, "mn", 0), (dlog_a, "mn", 0), (dlog_b, "mn", 0)])
    dwout = _wgrad("dw_out", mixin, dr2b, M=D, N=D, T=T, tm=tnd, tn=tnd, tk=T)
    (dsa,) = _mm("a_proj_bwd", [(dya, wa, 0, 0, 0)], tb=True, M=T, N=W, tm=tm, tn=tnw, tk=D, nk=1, epilogue=_ident,
                 outs=[(W, F32, 0)])
    (dsb,) = _mm("b_proj_bwd", [(dyb, wb, 0, 0, 0)], tb=True, M=T, N=W, tm=tm, tn=tnw, tk=D, nk=1, epilogue=_ident,
                 outs=[(W, F32, 0)])
    dwa = _wgrad("dw_a_proj", sa, dya, M=W, N=D, T=T, tm=tmw, tn=tnd, tk=T)
    dwb = _wgrad("dw_b_proj", sb, dyb, M=W, N=D, T=T, tm=tmw, tn=tnd, tk=T)

    dpa, dwm, dbst, dsgu_g, dsgu_b, dbin_a = _sgu_bwd("sgu_bwd", proj, dsa, sgu_ln_g, sgu_ln_b, wm, bst, tm_r)
    dzc, dcln_g, dcln_b, dbdw = _conv_ln_bwd("conv_ln_bwd", dsb, zc, conv_ln_g, conv_ln_b, tm_r)
    dpb, dwdw, dbin_b = _conv_bwd("conv_bwd", dzc, z, proj, wdw_full, KW, tm_r)

    dps = [dpa, dpb, dla, dlb]
    db_in = jnp.concatenate([dbin_a, dbin_b, _colsum_rows("db_in_gate_a", sla, sla.shape[0]),
                             _colsum_rows("db_in_gate_b", slb, slb.shape[0])], axis=1)
    c, un = _merge(b_gu2.rs_ici(cs_gu2), b_d2.rs_share(gh_d2))
    (dx1,), co = _mm("in_proj_bwd", [(dp, win, 0, k, 0) for k, dp in enumerate(dps)], tb=True, M=T, N=D, tm=tm,
                     tn=_pick128(D, 256), tk=D, nk=1, epilogue=lambda accs, ex: [accs[0] + ALPHA * ex[0]], outs=[(D, F32, 0)],
                     extras=[(dr2, "mn", 0)], carry=c)
    (r2_gu2,), (g_d2,) = un(co)
    gh_gu2 = b_gu2.rs_final("ffn2_w_gu", dwgu2[0], r1_gu2, r2_gu2, qc)
    dwin = _wgrad("dw_in_0", x1b, dps[0], M=D, N=D, T=T, tm=tnd, tn=tnd, tk=T, ncols=4 * D)
    for k in range(1, 4):
        dwin = _wgrad(f"dw_in_{k}", x1b, dps[k], M=D, N=D, T=T, tm=tnd, tn=tnd, tk=T, ncols=4 * D, into=dwin,
                      joff=k * (D // tnd))

    mix_names = ["w_in", "w_a_proj", "w_b_proj", "w_out"]
    mix_grads = dict(zip(mix_names, [dwin, dwa, dwb, dwout], strict=True))
    c, un = _merge(*[B[n].rs_pair(mix_grads[n][1]) for n in mix_names])
    (dr1, dr1h, dln1_g, dln1_b), co = _ln_bwd("ln1_bwd", dx1, xh1, rs1, ln1_g, 0.5, tm_ln, carry=c)
    r1_mix = {n: r1 for n, (r1,) in zip(mix_names, un(co), strict=True)}
    cs_mix = {n: B[n].rs_pairsum(n, mix_grads[n][0], r1_mix[n], qc) for n in mix_names}

    small_names = ["ln1_g", "ln1_b", "b_in", "sgu_ln_g", "sgu_ln_b", "sgu_w_s", "sgu_b_s", "conv_w_dw", "conv_b_dw",
                   "conv_ln_g", "conv_ln_b", "ln2_g", "ln2_b", "ln3_g", "ln3_b"]
    small_parts = {"ln1_g": dln1_g, "ln1_b": dln1_b, "b_in": db_in, "sgu_ln_g": dsgu_g, "sgu_ln_b": dsgu_b, "sgu_w_s": dwm,
                   "sgu_b_s": dbst.T, "conv_w_dw": dwdw[:KW], "conv_b_dw": dbdw, "conv_ln_g": dcln_g, "conv_ln_b": dcln_b,
                   "ln2_g": dln2_g, "ln2_b": dln2_b, "ln3_g": dln3_g, "ln3_b": dln3_b}
    packed, spans = _pack_rows([small_parts[n] for n in small_names])

    b_gu1, b_d1 = B["ffn1_w_gu"], B["ffn1_w_down"]
    c, un = _merge(B["w_in"].rs_ici(cs_mix["w_in"]), _small_allgather(packed), b_gu2.rs_share(gh_gu2))
    (dg1, du1), co = ffn_dact("ffn1", dr1h, wd1, g1, u1, carry=c)
    (r2_win,), (small_slots,), (g_gu2,) = un(co)
    c, un = _merge(*[B[n].rs_ici(cs_mix[n]) for n in mix_names[1:]])
    dwgu1, co = ffn_dwgate("ffn1", xb, dg1, carry=c)
    r2_mix = [[r2_win]] + un(co)
    gh_mix = [B[n].rs_final(n, mix_grads[n][0], r1_mix[n], r2, qc) for n, (r2,) in zip(mix_names, r2_mix, strict=True)]
    dwgu1 = ffn_dwup("ffn1", xb, du1, dwgu1)
    c, un = _merge(*[B[n].rs_share(gh) for n, gh in zip(mix_names, gh_mix, strict=True)], b_gu1.rs_pair(dwgu1[1]))
    dwd1, co = ffn_dwdown("ffn1", a1, dr1h, carry=c)
    *g_mixs, (r1_gu1,) = un(co)
    g_mix = {n: g for n, (g,) in zip(mix_names, g_mixs, strict=True)}
    cs_gu1 = b_gu1.rs_pairsum("ffn1_w_gu", dwgu1[0], r1_gu1, qc)
    c, un = _merge(b_gu1.rs_ici(cs_gu1), b_d1.rs_pair(dwd1[1]))
    (dx_gate,), co = ffn_dx("ffn1", "gate", dg1, wgu1, [], carry=c)
    (r2_gu1,), (r1_d1,) = un(co)
    cs_d1 = b_d1.rs_pairsum("ffn1_w_down", dwd1[0], r1_d1, qc)
    gh_gu1 = b_gu1.rs_final("ffn1_w_gu", dwgu1[0], r1_gu1, r2_gu1, qc)
    c, un = _merge(b_d1.rs_ici(cs_d1), b_gu1.rs_share(gh_gu1))
    (dx,), co = ffn_dx("ffn1", "up", du1, wgu1, [dr1, dx_gate], carry=c)
    (r2_d1,), (g_gu1,) = un(co)
    gh_d1 = b_d1.rs_final("ffn1_w_down", dwd1[0], r1_d1, r2_d1, qc)

    grads = {"ffn1_w_gu": g_gu1, "ffn2_w_gu": g_gu2, "ffn2_w_down": g_d2, **g_mix}
    outs_g, outs_d, outs_m, outs_v = {}, {}, {}, {}

    def adamw_big(n, g, carry=None):
        b = B[n]
        return _adamw(f"adamw_{n}", b.w, g, b.m, b.v, _tile_rows(b.sr, b.sc, 1024 * 1024), carry=carry)

    upd = {}
    upd["w_a_proj"], (grads["ffn1_w_down"],) = adamw_big("w_a_proj", grads["w_a_proj"], carry=b_d1.rs_share(gh_d1))
    for n, _ in big_names:
        if n not in upd:
            upd[n] = adamw_big(n, grads[n])
        g_, d_, m_, v_ = upd[n]
        outs_g[n], outs_d[n], outs_m[n], outs_v[n] = g_[None], d_[None], m_[None], v_[None]
    gsum = _sum8("small_sum", small_slots)
    full_shapes = [args[n].shape if n != "conv_w_dw" else (1, KW, W) for n in small_names]
    gsmall = dict(zip(small_names, _unpack_rows(gsum, spans, full_shapes), strict=True))
    cs = conv_w_dw.shape[2]
    gsmall["conv_w_dw"] = lax.dynamic_slice_in_dim(gsmall["conv_w_dw"], q * cs, cs, axis=2)
    pw, spans2 = _pack_rows([args[n] for n in small_names])
    pg, _ = _pack_rows([gsmall[n] for n in small_names])
    pm, _ = _pack_rows([args["m_" + n] for n in small_names])
    pv, _ = _pack_rows([args["v_" + n] for n in small_names])
    _, pd, pmn, pvn = _adamw("adamw_small", pw, pg, pm, pv, pw.shape[0])
    shapes2 = [args[n].shape for n in small_names]
    for dst, src in ((outs_d, pd), (outs_m, pmn), (outs_v, pvn)):
        dst.update(zip(small_names, _unpack_rows(src, spans2, shapes2), strict=True))
    outs_g.update(gsmall)

    loss = lax.psum(loss_part[0, 0], ("x", "y", "c"))
    order = ["ffn1_w_gu", "ffn1_w_down", "ln1_g", "ln1_b", "w_in", "b_in", "sgu_ln_g", "sgu_ln_b", "sgu_w_s", "sgu_b_s",
             "w_a_proj", "conv_w_dw", "conv_b_dw", "conv_ln_g", "conv_ln_b", "w_b_proj", "w_out", "ln2_g", "ln2_b",
             "ffn2_w_gu", "ffn2_w_down", "ln3_g", "ln3_b"]
    return (loss, dx[None], *[outs_g[n] for n in order], *[outs_d[n] for n in order], *[outs_m[n] for n in order],
            *[outs_v[n] for n in order])
```

```python
import math

import jax
import jax.numpy as jnp
from jax import lax
from jax.experimental import pallas as pl
from jax.experimental.pallas import tpu as pltpu

BF16 = jnp.bfloat16
F32 = jnp.float32

LN_EPS = 1e-5
ALPHA = 2.0 ** 0.25
SGU_BLOCK = 128
SGU_CHUNK = 64
HALO = 32
SUBLANES = 8
LANES = 128
CONV_ROWS = 32
ADAM_LR = 0.001
ADAM_B1 = 0.9
ADAM_B2 = 0.999
ADAM_EPS = 1e-08
ADAM_WD = 0.01
ADAM_STEP = 10
N_CHIPS = 4
VMEM_LIMIT = 52 * 1024 * 1024
MESH = pl.DeviceIdType.MESH

_GELU_C0 = math.sqrt(2.0 / math.pi)
_GELU_C1 = 0.044715


def _cparams(sem):
    return pltpu.CompilerParams(dimension_semantics=sem, vmem_limit_bytes=VMEM_LIMIT)


def _gelu_parts(x):
    x2 = x * x
    t = jnp.tanh(_GELU_C0 * (x + _GELU_C1 * (x2 * x)))
    return 0.5 * (1.0 + t), t, x2


def _gelu(x):
    cdf, _, _ = _gelu_parts(x)
    return x * cdf


def _gelu_and_grad(x):
    cdf, t, x2 = _gelu_parts(x)
    grad = cdf + x * (0.5 * (1.0 - t * t)) * (_GELU_C0 * (1.0 + (3.0 * _GELU_C1) * x2))
    return x * cdf, grad


def _silu_grad(x):
    s = jax.nn.sigmoid(x)
    return s * (1.0 + x * (1.0 - s))


def _row_stats(x):
    mu = jnp.mean(x, axis=-1, keepdims=True)
    xc = x - mu
    var = jnp.mean(xc * xc, axis=-1, keepdims=True)
    rstd = lax.rsqrt(var + LN_EPS)
    return xc * rstd, rstd


def _ln_bwd_rows(dy, xhat, rstd, g):
    dxh = dy * g
    m1 = jnp.mean(dxh, axis=-1, keepdims=True)
    m2 = jnp.mean(dxh * xhat, axis=-1, keepdims=True)
    return rstd * (dxh - m1 - xhat * m2)


def _colsum(v):
    return jnp.sum(v, axis=0, keepdims=True)


class _Carry:
    def __init__(self, ro, rw, new, plan, n_remote, n_local=0):
        self.ro, self.rw, self.new, self.plan = list(ro), list(rw), list(new), plan
        self.n_remote, self.n_local = n_remote, n_local

    def sems(self):
        return [pltpu.SemaphoreType.DMA((self.n_remote,)), pltpu.SemaphoreType.DMA((self.n_remote,)),
                pltpu.SemaphoreType.DMA((max(self.n_local, 1),))]

    def copies(self, ro_refs, rw_refs, new_refs, send_sems, recv_sems, loc_sems):
        x, y, c = lax.axis_index("x"), lax.axis_index("y"), lax.axis_index("c")
        remote, local = self.plan(ro_refs, rw_refs, new_refs, x, y, c)
        assert len(remote) == self.n_remote and len(local) == self.n_local
        lcs = [pltpu.make_async_copy(s, d, loc_sems.at[k]) for k, (s, d) in enumerate(local)]
        rcs = [pltpu.make_async_remote_copy(src_ref=s, dst_ref=d, send_sem=send_sems.at[k], recv_sem=recv_sems.at[k],
                                            device_id=dev, device_id_type=MESH) for k, (s, d, dev) in enumerate(remote)]
        return lcs, rcs

    def out_shape(self):
        return [jax.ShapeDtypeStruct(a.shape, a.dtype) for a in self.rw] + self.new


def _start_all(lcs, rcs):
    for cp in lcs + rcs:
        cp.start()


def _wait_all(lcs, rcs):
    for cp in rcs:
        cp.wait_send()
    for cp in rcs:
        cp.wait_recv()
    for cp in lcs:
        cp.wait()


def _merge(*cs):
    ro = [a for c in cs for a in c.ro]
    rw = [a for c in cs for a in c.rw]
    new = [a for c in cs for a in c.new]

    def plan(ro_refs, rw_refs, new_refs, x, y, c):
        remote, local, a, b, d = [], [], 0, 0, 0
        for cc in cs:
            r, l = cc.plan(ro_refs[a:a + len(cc.ro)], rw_refs[b:b + len(cc.rw)], new_refs[d:d + len(cc.new)], x, y, c)
            a, b, d = a + len(cc.ro), b + len(cc.rw), d + len(cc.new)
            remote += r
            local += l
        return remote, local

    def unpack(couts):
        res, b, d = [], 0, len(rw)
        for cc in cs:
            res.append(list(couts[b:b + len(cc.rw)]) + list(couts[d:d + len(cc.new)]))
            b, d = b + len(cc.rw), d + len(cc.new)
        return res

    return _Carry(ro, rw, new, plan, sum(c.n_remote for c in cs), sum(c.n_local for c in cs)), unpack


_ANY = pl.BlockSpec(memory_space=pl.ANY)


def _exchange(name, carry):
    n_ro, n_rw, n_new = len(carry.ro), len(carry.rw), len(carry.new)

    def body(*refs):
        o0 = n_ro + n_rw
        lcs, rcs = carry.copies(refs[:n_ro], refs[o0:o0 + n_rw], refs[o0 + n_rw:o0 + n_rw + n_new], *refs[o0 + n_rw + n_new:])
        _start_all(lcs, rcs)
        _wait_all(lcs, rcs)

    return list(pl.pallas_call(
        body, name=name, in_specs=[_ANY] * (n_ro + n_rw), out_specs=[_ANY] * (n_rw + n_new), out_shape=carry.out_shape(),
        input_output_aliases={n_ro + k: k for k in range(n_rw)}, scratch_shapes=carry.sems())(*carry.ro, *carry.rw))


def _call(body, *, name, grid, in_specs, out_specs, out_shape, args, scratch=(), sem, carry=None, aliases=None):
    in_specs, out_specs, out_shape, scratch = list(in_specs), list(out_specs), list(out_shape), list(scratch)
    if carry is None:
        return list(pl.pallas_call(body, name=name, grid=grid, in_specs=in_specs, out_specs=out_specs, out_shape=out_shape,
                                   scratch_shapes=scratch, input_output_aliases=aliases or {},
                                   compiler_params=_cparams(sem))(*args))
    n_in, n_out, n_scr = len(in_specs), len(out_specs), len(scratch)
    n_ro, n_rw, n_new = len(carry.ro), len(carry.rw), len(carry.new)

    def wrapped(*refs):
        ins = refs[:n_in]
        ro_refs = refs[n_in:n_in + n_ro]
        o0 = n_in + n_ro + n_rw
        outs = refs[o0:o0 + n_out]
        rw_refs = refs[o0 + n_out:o0 + n_out + n_rw]
        new_refs = refs[o0 + n_out + n_rw:o0 + n_out + n_rw + n_new]
        s0 = o0 + n_out + n_rw + n_new
        scr = refs[s0:s0 + n_scr]
        sems = refs[s0 + n_scr:]
        first = pl.program_id(0) == 0
        last = pl.program_id(0) == grid[0] - 1
        for d in range(1, len(grid)):
            first = jnp.logical_and(first, pl.program_id(d) == 0)
            last = jnp.logical_and(last, pl.program_id(d) == grid[d] - 1)

        @pl.when(first)
        def _():
            _start_all(*carry.copies(ro_refs, rw_refs, new_refs, *sems))

        body(*ins, *outs, *scr)

        @pl.when(last)
        def _():
            _wait_all(*carry.copies(ro_refs, rw_refs, new_refs, *sems))

    al = dict(aliases or {})
    al.update({n_in + n_ro + k: n_out + k for k in range(n_rw)})
    res = pl.pallas_call(
        wrapped, name=name, grid=grid, in_specs=in_specs + [_ANY] * (n_ro + n_rw), out_specs=out_specs + [_ANY] * (n_rw + n_new),
        out_shape=out_shape + carry.out_shape(), scratch_shapes=scratch + carry.sems(), input_output_aliases=al,
        compiler_params=_cparams(("arbitrary",) * len(grid)))(*args, *carry.ro, *carry.rw)
    return list(res[:n_out]), list(res[n_out:])


def _mm(name, pairs, *, ta=False, tb=False, M, N, tm, tn, tk, nk, epilogue, outs, extras=(), sum_pairs=True, carry=None,
        sub=None):
    ni, nj = M // tm, N // tn
    assert ni * tm == M and nj * tn == N
    n_p = len(pairs)
    n_acc = 1 if sum_pairs else n_p
    in_specs, args = [], []
    for a, b, ak, bk, bj in pairs:
        if ta:
            in_specs.append(pl.BlockSpec((tk, tm), lambda i, j, k, ak=ak: (k + ak, i)))
        else:
            in_specs.append(pl.BlockSpec((tm, tk), lambda i, j, k, ak=ak: (i, k + ak)))
        if tb:
            in_specs.append(pl.BlockSpec((tn, tk), lambda i, j, k, bk=bk, bj=bj: (j + bj, k + bk)))
        else:
            in_specs.append(pl.BlockSpec((tk, tn), lambda i, j, k, bk=bk, bj=bj: (k + bk, j + bj)))
        args += [a, b]
    for arr, kind, jo in extras:
        if kind == "mn":
            in_specs.append(pl.BlockSpec((tm, tn), lambda i, j, k, jo=jo: (i, j + jo)))
        else:
            in_specs.append(pl.BlockSpec((1, tn), lambda i, j, k, jo=jo: (0, j + jo)))
        args.append(arr)
    out_rows = [o[3] if len(o) > 3 else tm for o in outs]
    out_specs = [pl.BlockSpec((r, tn), lambda i, j, k, jo=o[2]: (i, j + jo)) for o, r in zip(outs, out_rows, strict=True)]
    out_shape = [jax.ShapeDtypeStruct((ni * r, o[0]), o[1]) for o, r in zip(outs, out_rows, strict=True)]
    n_ex, n_out = len(extras), len(outs)
    dn = (((0 if ta else 1,), (1 if tb else 0,)), ((), ()))

    def body(*refs):
        ab = refs[: 2 * n_p]
        ex = refs[2 * n_p: 2 * n_p + n_ex]
        o0 = 2 * n_p + n_ex
        out_refs = refs[o0: o0 + n_out]
        acc_refs = refs[o0 + n_out:]

        def dots():
            res = []
            for p in range(n_p):
                a = ab[2 * p][...]
                b = ab[2 * p + 1][...]
                res.append(lax.dot_general(a.astype(BF16), b.astype(BF16), dn, preferred_element_type=F32))
            if sum_pairs:
                tot = res[0]
                for r in res[1:]:
                    tot = tot + r
                res = [tot]
            return res

        def finish(accs):
            tiles = epilogue(accs, [e[...] for e in ex])
            for r, t in zip(out_refs, tiles, strict=True):
                r[...] = t.astype(r.dtype)

        if nk == 1 and sub is not None and tn > sub:
            for s in range(tn // sub):
                cs = slice(s * sub, (s + 1) * sub)
                res = []
                for p in range(n_p):
                    b = ab[2 * p + 1][cs, :] if tb else ab[2 * p + 1][:, cs]
                    res.append(lax.dot_general(ab[2 * p][...].astype(BF16), b.astype(BF16), dn, preferred_element_type=F32))
                if sum_pairs:
                    tot = res[0]
                    for r in res[1:]:
                        tot = tot + r
                    res = [tot]
                tiles = epilogue(res, [e[:, cs] for e in ex])
                for r, t in zip(out_refs, tiles, strict=True):
                    r[:, cs] = t.astype(r.dtype)
        elif nk == 1:
            finish(dots())
        else:
            k = pl.program_id(2)
            d = dots()

            @pl.when(k == 0)
            def _():
                for r, v in zip(acc_refs, d, strict=True):
                    r[...] = v

            @pl.when(k > 0)
            def _():
                for r, v in zip(acc_refs, d, strict=True):
                    r[...] += v

            @pl.when(k == nk - 1)
            def _():
                finish([r[...] for r in acc_refs])

    scratch = [pltpu.VMEM((tm, tn), F32) for _ in range(n_acc)] if nk > 1 else []
    return _call(body, name=name, grid=(ni, nj, nk), in_specs=in_specs, out_specs=out_specs, out_shape=out_shape, args=args,
                 scratch=scratch, sem=("parallel", "parallel", "arbitrary"), carry=carry)


def _wgrad(name, a, b, *, M, N, T, tm, tn, tk, into=None, joff=0, ncols=None, carry=None):
    ncols = N if ncols is None else ncols
    ni, nj, nk = M // tm, N // tn, T // tk
    dn = (((0,), (0,)), ((), ()))

    def body(a_ref, b_ref, *rest):
        d = lax.dot_general(a_ref[...].astype(BF16), b_ref[...].astype(BF16), dn, preferred_element_type=F32)
        if nk == 1:
            of_ref, oh_ref = rest[-2:]
            of_ref[...] = d
            oh_ref[...] = d.astype(BF16)
            return
        of_ref, oh_ref, acc_ref = rest[-3:]
        k = pl.program_id(2)

        @pl.when(k == 0)
        def _():
            acc_ref[...] = d

        @pl.when(k > 0)
        def _():
            acc_ref[...] += d

        @pl.when(k == nk - 1)
        def _():
            of_ref[...] = acc_ref[...]
            oh_ref[...] = acc_ref[...].astype(BF16)

    ospec = pl.BlockSpec((tm, tn), lambda i, j, k: (i, j + joff))
    in_specs = [pl.BlockSpec((tk, tm), lambda i, j, k: (k, i)), pl.BlockSpec((tk, tn), lambda i, j, k: (k, j))]
    args, aliases = [a, b], None
    if into is not None:
        in_specs += [_ANY, _ANY]
        args += list(into)
        aliases = {2: 0, 3: 1}
    return _call(body, name=name, grid=(ni, nj, nk), in_specs=in_specs, out_specs=[ospec, ospec],
                 out_shape=[jax.ShapeDtypeStruct((M, ncols), F32), jax.ShapeDtypeStruct((M, ncols), BF16)], args=args,
                 scratch=[pltpu.VMEM((tm, tn), F32)] if nk > 1 else [], sem=("parallel", "parallel", "arbitrary"), carry=carry,
                 aliases=aliases)


def _rows(tm, c, cb=0):
    return pl.BlockSpec((tm, c), lambda i, cb=cb: (i, cb))


def _whole(shape):
    nd = len(shape)
    return pl.BlockSpec(shape, lambda i, nd=nd: (0,) * nd)


def _cast_bf16(name, x, tm, carry=None):
    t, d = x.shape

    def body(x_ref, o_ref):
        o_ref[...] = x_ref[...].astype(BF16)

    return _call(body, name=name, grid=(t // tm,), in_specs=[_rows(tm, d)], out_specs=[_rows(tm, d)],
                 out_shape=[jax.ShapeDtypeStruct((t, d), BF16)], args=[x], sem=("parallel",), carry=carry)


def _residual(x_ref, prev_refs):
    if not prev_refs:
        return x_ref[...]
    return x_ref[...] * prev_refs[0][...] + prev_refs[1][...]


def _ln_fwd(name, xres, f, g, b, cf, tm, carry=None, prev=()):
    t, d = xres.shape
    n_prev = len(prev)

    def body(x_ref, f_ref, g_ref, b_ref, *rest):
        yb_ref, xh_ref, rs_ref = rest[n_prev:]
        r = ALPHA * _residual(x_ref, rest[:n_prev]) + cf * f_ref[...]
        xhat, rstd = _row_stats(r)
        yb_ref[...] = (xhat * g_ref[...] + b_ref[...]).astype(BF16)
        xh_ref[...] = xhat
        rs_ref[...] = rstd

    return _call(
        body, name=name, grid=(t // tm,),
        in_specs=[_rows(tm, d), _rows(tm, d), _whole((1, d)), _whole((1, d))] + [_whole((1, d))] * n_prev,
        out_specs=[_rows(tm, d), _rows(tm, d), _rows(tm, 1)],
        out_shape=[jax.ShapeDtypeStruct((t, d), BF16), jax.ShapeDtypeStruct((t, d), F32), jax.ShapeDtypeStruct((t, 1), F32)],
        args=[xres, f, g, b, *prev], sem=("parallel",), carry=carry)


def _ln_bwd(name, dy, xhat, rstd, g, scale, tm, carry=None):
    t, d = dy.shape

    def body(dy_ref, xh_ref, rs_ref, g_ref, dr_ref, drb_ref, dg_ref, db_ref):
        i = pl.program_id(0)
        dy_v, xh = dy_ref[...], xh_ref[...]
        dr = _ln_bwd_rows(dy_v, xh, rs_ref[...], g_ref[...])
        dr_ref[...] = dr
        drb_ref[...] = (scale * dr).astype(BF16)

        @pl.when(i == 0)
        def _():
            dg_ref[...] = jnp.zeros_like(dg_ref)
            db_ref[...] = jnp.zeros_like(db_ref)

        dg_ref[...] += _colsum(dy_v * xh)
        db_ref[...] += _colsum(dy_v)

    return _call(
        body, name=name, grid=(t // tm,), in_specs=[_rows(tm, d), _rows(tm, d), _rows(tm, 1), _whole((1, d))],
        out_specs=[_rows(tm, d), _rows(tm, d), _whole((1, d)), _whole((1, d))],
        out_shape=[jax.ShapeDtypeStruct((t, d), F32), jax.ShapeDtypeStruct((t, d), BF16),
                   jax.ShapeDtypeStruct((1, d), F32), jax.ShapeDtypeStruct((1, d), F32)],
        args=[dy, xhat, rstd, g], sem=("arbitrary",), carry=carry)


def _ln3_loss(name, xres, f, g, b, target, tm, prev):
    t, d = xres.shape

    def body(x_ref, f_ref, g_ref, b_ref, tg_ref, pg_ref, pb_ref, dr_ref, drb_ref, loss_ref, dg_ref, db_ref):
        i = pl.program_id(0)
        r = ALPHA * _residual(x_ref, (pg_ref, pb_ref)) + 0.5 * f_ref[...]
        xhat, rstd = _row_stats(r)
        gv = g_ref[...]
        y = xhat * gv + b_ref[...]
        err = y - tg_ref[...]
        dy = err * (1.0 / d)
        dr = _ln_bwd_rows(dy, xhat, rstd, gv)
        dr_ref[...] = dr
        drb_ref[...] = (0.5 * dr).astype(BF16)
        part = 0.5 * jnp.sum(jnp.mean(err * err, axis=-1, keepdims=True), axis=0, keepdims=True)

        @pl.when(i == 0)
        def _():
            loss_ref[...] = jnp.zeros_like(loss_ref)
            dg_ref[...] = jnp.zeros_like(dg_ref)
            db_ref[...] = jnp.zeros_like(db_ref)

        loss_ref[...] += jnp.broadcast_to(part, loss_ref.shape)
        dg_ref[...] += _colsum(dy * xhat)
        db_ref[...] += _colsum(dy)

    return _call(
        body, name=name, grid=(t // tm,),
        in_specs=[_rows(tm, d), _rows(tm, d), _whole((1, d)), _whole((1, d)), _rows(tm, d), _whole((1, d)), _whole((1, d))],
        out_specs=[_rows(tm, d), _rows(tm, d), _whole((8, 128)), _whole((1, d)), _whole((1, d))],
        out_shape=[jax.ShapeDtypeStruct((t, d), F32), jax.ShapeDtypeStruct((t, d), BF16),
                   jax.ShapeDtypeStruct((8, 128), F32), jax.ShapeDtypeStruct((1, d), F32),
                   jax.ShapeDtypeStruct((1, d), F32)],
        args=[xres, f, g, b, target, *prev], sem=("arbitrary",))


def _colsum_rows(name, x, tm):
    t, d = x.shape

    def body(x_ref, o_ref):
        @pl.when(pl.program_id(0) == 0)
        def _():
            o_ref[...] = jnp.zeros_like(o_ref)

        o_ref[...] += _colsum(x_ref[...].astype(F32))

    return _call(body, name=name, grid=(t // tm,), in_specs=[_rows(tm, d)], out_specs=[_whole((1, d))],
                 out_shape=[jax.ShapeDtypeStruct((1, d), F32)], args=[x], sem=("arbitrary",))[0]


def _sgu_mask():
    sh = SGU_CHUNK.bit_length() - 1
    r = lax.shift_right_logical(lax.broadcasted_iota(jnp.int32, (SGU_BLOCK, SGU_BLOCK), 0), sh)
    c = lax.shift_right_logical(lax.broadcasted_iota(jnp.int32, (SGU_BLOCK, SGU_BLOCK), 1), sh)
    return c <= r


def _sgu_fwd(name, p, lng, lnb, wm, bst, tm):
    t = p.shape[0]
    n_grp, w = wm.shape[0], lng.shape[1]
    hd = w // n_grp
    nblk = tm // SGU_BLOCK

    def body(uv_ref, h_ref, g_ref, b_ref, wm_ref, bs_ref, sa_ref, z_ref, vn_s):
        xhat, _ = _row_stats(_gelu(uv_ref[:, w:]))
        vn_s[...] = (xhat * g_ref[...] + b_ref[...]).astype(BF16)
        mask = _sgu_mask()
        for h in range(n_grp):
            wh = jnp.where(mask, wm_ref[h], 0.0).astype(BF16)
            bcol = bs_ref[:, h:h + 1]
            cs = slice(h * hd, (h + 1) * hd)
            for n in range(nblk):
                rs = slice(n * SGU_BLOCK, (n + 1) * SGU_BLOCK)
                s = jnp.dot(wh, vn_s[rs, cs], preferred_element_type=F32) + bcol
                sa_ref[rs, cs] = (_gelu(uv_ref[rs, cs]) * s).astype(BF16)
        z_ref[...] = h_ref[:, :w] * jax.nn.sigmoid(h_ref[:, w:])

    return _call(
        body, name=name, grid=(t // tm,),
        in_specs=[_rows(tm, 2 * w, 0), _rows(tm, 2 * w, 1), _whole((1, w)), _whole((1, w)), _whole(wm.shape),
                  _whole(bst.shape)],
        out_specs=[_rows(tm, w), _rows(tm, w)],
        out_shape=[jax.ShapeDtypeStruct((t, w), BF16), jax.ShapeDtypeStruct((t, w), F32)],
        args=[p, p, lng, lnb, wm, bst], scratch=[pltpu.VMEM((tm, w), BF16)], sem=("parallel",))


def _sgu_bwd(name, p, dsa, lng, lnb, wm, bst, tm):
    t = p.shape[0]
    n_grp, w = wm.shape[0], lng.shape[1]
    hd = w // n_grp
    nblk = tm // SGU_BLOCK

    def body(uv_ref, dsa_ref, g_ref, b_ref, wm_ref, bs_ref, dp_ref, dwm_ref, dbs_ref, dg_ref, db_ref, dbin_ref,
             vn_s, ug_s, dvn_s, dug_s):
        i = pl.program_id(0)

        @pl.when(i == 0)
        def _():
            dwm_ref[...] = jnp.zeros_like(dwm_ref)
            dbs_ref[...] = jnp.zeros_like(dbs_ref)
            dg_ref[...] = jnp.zeros_like(dg_ref)
            db_ref[...] = jnp.zeros_like(db_ref)
            dbin_ref[...] = jnp.zeros_like(dbin_ref)

        ug, dgelu_u = _gelu_and_grad(uv_ref[:, :w])
        ug_s[...] = ug
        vg, dgelu_v = _gelu_and_grad(uv_ref[:, w:])
        xhat, rstd = _row_stats(vg)
        gv = g_ref[...]
        vn_s[...] = (xhat * gv + b_ref[...]).astype(BF16)
        mask = _sgu_mask()
        for h in range(n_grp):
            wh = jnp.where(mask, wm_ref[h], 0.0).astype(BF16)
            bcol = bs_ref[:, h:h + 1]
            cs = slice(h * hd, (h + 1) * hd)
            dw_h = jnp.zeros((SGU_BLOCK, SGU_BLOCK), F32)
            dbs_h = jnp.zeros((SGU_BLOCK, 1), F32)
            for n in range(nblk):
                rs = slice(n * SGU_BLOCK, (n + 1) * SGU_BLOCK)
                vblk = vn_s[rs, cs]
                s = jnp.dot(wh, vblk, preferred_element_type=F32) + bcol
                dsa_blk = dsa_ref[rs, cs]
                dug_s[rs, cs] = dsa_blk * s
                ds = dsa_blk * ug_s[rs, cs]
                dsb = ds.astype(BF16)
                dvn_s[rs, cs] = lax.dot_general(wh, dsb, (((0,), (0,)), ((), ())), preferred_element_type=F32)
                dw_h = dw_h + lax.dot_general(dsb, vblk, (((1,), (1,)), ((), ())), preferred_element_type=F32)
                dbs_h = dbs_h + jnp.sum(ds, axis=1, keepdims=True)
            dwm_ref[h] += jnp.where(mask, dw_h, 0.0)
            dbs_ref[:, h:h + 1] += dbs_h
        dvn = dvn_s[...]
        dg_ref[...] += _colsum(dvn * xhat)
        db_ref[...] += _colsum(dvn)
        dvg = _ln_bwd_rows(dvn, xhat, rstd, gv)
        du = dug_s[...] * dgelu_u
        dv = dvg * dgelu_v
        dp_ref[:, :w] = du.astype(BF16)
        dp_ref[:, w:] = dv.astype(BF16)
        dbin_ref[:, :w] += _colsum(du)
        dbin_ref[:, w:] += _colsum(dv)

    return _call(
        body, name=name, grid=(t // tm,),
        in_specs=[_rows(tm, 2 * w, 0), _rows(tm, w), _whole((1, w)), _whole((1, w)), _whole(wm.shape), _whole(bst.shape)],
        out_specs=[_rows(tm, 2 * w), _whole(wm.shape), _whole(bst.shape), _whole((1, w)), _whole((1, w)), _whole((1, 2 * w))],
        out_shape=[jax.ShapeDtypeStruct((t, 2 * w), BF16), jax.ShapeDtypeStruct(wm.shape, F32),
                   jax.ShapeDtypeStruct(bst.shape, F32), jax.ShapeDtypeStruct((1, w), F32), jax.ShapeDtypeStruct((1, w), F32),
                   jax.ShapeDtypeStruct((1, 2 * w), F32)],
        args=[p, dsa, lng, lnb, wm, bst],
        scratch=[pltpu.VMEM((tm, w), BF16), pltpu.VMEM((tm, w), F32), pltpu.VMEM((tm, w), F32), pltpu.VMEM((tm, w), F32)],
        sem=("arbitrary",))


def _halo_prev(tm, c):
    return pl.BlockSpec((HALO, c), lambda i: (jnp.maximum(i * (tm // HALO) - 1, 0), 0))


def _halo_next(tm, c, t):
    last = t // HALO - 1
    return pl.BlockSpec((HALO, c), lambda i: (jnp.minimum((i + 1) * (tm // HALO), last), 0))


def _shifted_copies(sh, n):
    for r in range(1, SUBLANES):
        sh[r, :n - SUBLANES, :] = sh[0, r:r + n - SUBLANES, :]


def _row_broadcasts(wb, w_ref, kw):
    for k in range(kw):
        wb[k] = jnp.broadcast_to(w_ref[k:k + 1, :], wb.shape[1:])


def _tap(sh, r0, o, rows, cols):
    return sh[o % SUBLANES, pl.ds(pl.multiple_of(r0 + (o - o % SUBLANES), SUBLANES), rows), cols]


def _conv_fwd(name, z, wdw, kw, bdw, lng, lnb, tm, carry=None):
    t, c = z.shape
    lead = HALO - (kw - 1)
    n = tm + HALO

    def body(zp_ref, z_ref, w_ref, bdw_ref, g_ref, b_ref, zc_ref, sb_ref, sh, wb):
        i = pl.program_id(0)
        sh[0, :HALO, :] = jnp.where(i > 0, zp_ref[...], 0.0)
        sh[0, HALO:, :] = z_ref[...]
        _shifted_copies(sh, n)
        _row_broadcasts(wb, w_ref, kw)
        bias = jnp.broadcast_to(bdw_ref[...], (SUBLANES, c))
        groups = CONV_ROWS // SUBLANES

        def chunk(ci, _):
            r0 = pl.multiple_of(ci * CONV_ROWS, CONV_ROWS)
            accs = [bias] * groups
            for k in range(kw):
                wk = wb[k]
                tp = _tap(sh, r0, lead + k, CONV_ROWS, slice(None))
                accs = [accs[g] + wk * tp[g * SUBLANES:(g + 1) * SUBLANES] for g in range(groups)]
            zc_ref[pl.ds(r0, CONV_ROWS), :] = jnp.concatenate(accs, axis=0)
            return 0

        lax.fori_loop(0, tm // CONV_ROWS, chunk, 0)
        xhat, _ = _row_stats(zc_ref[...])
        zn = xhat * g_ref[...] + b_ref[...]
        sb_ref[...] = (zn * jax.nn.sigmoid(zn)).astype(BF16)

    return _call(
        body, name=name, grid=(t // tm,),
        in_specs=[_halo_prev(tm, c), _rows(tm, c), _whole(wdw.shape), _whole((1, c)), _whole((1, c)), _whole((1, c))],
        out_specs=[_rows(tm, c), _rows(tm, c)],
        out_shape=[jax.ShapeDtypeStruct((t, c), F32), jax.ShapeDtypeStruct((t, c), BF16)],
        args=[z, z, wdw, bdw, lng, lnb], scratch=[pltpu.VMEM((SUBLANES, n, c), F32), pltpu.VMEM((HALO, SUBLANES, c), F32)],
        sem=("parallel",), carry=carry)


def _conv_ln_bwd(name, dsb, zc, lng, lnb, tm):
    t, c = zc.shape

    def body(dsb_ref, zc_ref, g_ref, b_ref, dzc_ref, dg_ref, db_ref, dbdw_ref):
        i = pl.program_id(0)
        xhat, rstd = _row_stats(zc_ref[...])
        gv = g_ref[...]
        zn = xhat * gv + b_ref[...]
        dzn = dsb_ref[...] * _silu_grad(zn)
        dzc = _ln_bwd_rows(dzn, xhat, rstd, gv)
        dzc_ref[...] = dzc

        @pl.when(i == 0)
        def _():
            dg_ref[...] = jnp.zeros_like(dg_ref)
            db_ref[...] = jnp.zeros_like(db_ref)
            dbdw_ref[...] = jnp.zeros_like(dbdw_ref)

        dg_ref[...] += _colsum(dzn * xhat)
        db_ref[...] += _colsum(dzn)
        dbdw_ref[...] += _colsum(dzc)

    return _call(
        body, name=name, grid=(t // tm,), in_specs=[_rows(tm, c), _rows(tm, c), _whole((1, c)), _whole((1, c))],
        out_specs=[_rows(tm, c), _whole((1, c)), _whole((1, c)), _whole((1, c))],
        out_shape=[jax.ShapeDtypeStruct((t, c), F32)] + [jax.ShapeDtypeStruct((1, c), F32)] * 3,
        args=[dsb, zc, lng, lnb], sem=("arbitrary",))


def _conv_bwd(name, dzc, z, p, wdw, kw, tm):
    t, c = z.shape
    n_i = t // tm
    n = tm + HALO

    def body(dzc_ref, dzn_ref, z_ref, h_ref, w_ref, dp_ref, dw_ref, dbin_ref, sh, dz_s, wb):
        i = pl.program_id(0)

        @pl.when(i == 0)
        def _():
            dw_ref[...] = jnp.zeros_like(dw_ref)

        sh[0, :tm, :] = dzc_ref[...]
        sh[0, tm:, :] = jnp.where(i < n_i - 1, dzn_ref[...], 0.0)
        _shifted_copies(sh, n)
        _row_broadcasts(wb, w_ref, kw)
        nv = 8
        rows = nv * SUBLANES
        ways = 2
        tap_groups = [range(0, kw // 2), range(kw // 2, kw)]
        for lc in range(c // LANES):
            cols = slice(lc * LANES, (lc + 1) * LANES)
            for gi, taps in enumerate(tap_groups):

                def rowv(rv, accs, cols=cols, taps=taps, first=gi == 0):
                    r0 = pl.multiple_of(rv * rows, rows)
                    zv = z_ref[pl.ds(r0, rows), cols]
                    zs = [zv[v * SUBLANES:(v + 1) * SUBLANES] for v in range(nv)]
                    dz = [[None] * ways for _ in range(nv)]
                    new = []
                    for ai, k in enumerate(taps):
                        s = _tap(sh, r0, kw - 1 - k, rows, cols)
                        wk = wb[k, :, cols]
                        j = ai % ways
                        acc = accs[ai]
                        for v in range(nv):
                            sv = s[v * SUBLANES:(v + 1) * SUBLANES]
                            dz[v][j] = wk * sv if dz[v][j] is None else dz[v][j] + wk * sv
                            acc = acc + zs[v] * sv
                        new.append(acc)
                    part = jnp.concatenate([dz[v][0] + dz[v][1] for v in range(nv)], axis=0)
                    dz_s[pl.ds(r0, rows), cols] = part if first else dz_s[pl.ds(r0, rows), cols] + part
                    return tuple(new)

                accs = lax.fori_loop(0, tm // rows, rowv, tuple(jnp.zeros((SUBLANES, LANES), F32) for _ in taps))
                for ai, k in enumerate(taps):
                    dw_ref[k:k + 1, cols] += _colsum(accs[ai])
        dz = dz_s[...]
        a, g = h_ref[:, :c], h_ref[:, c:]
        sg = jax.nn.sigmoid(g)
        da = dz * sg
        dg = dz * a * (sg * (1.0 - sg))
        dp_ref[:, :c] = da.astype(BF16)
        dp_ref[:, c:] = dg.astype(BF16)

        @pl.when(i == 0)
        def _():
            dbin_ref[...] = jnp.zeros_like(dbin_ref)

        dbin_ref[:, :c] += _colsum(da)
        dbin_ref[:, c:] += _colsum(dg)

    return _call(
        body, name=name, grid=(n_i,),
        in_specs=[_rows(tm, c), _halo_next(tm, c, t), _rows(tm, c), _rows(tm, 2 * c, 1), _whole(wdw.shape)],
        out_specs=[_rows(tm, 2 * c), _whole((HALO, c)), _whole((1, 2 * c))],
        out_shape=[jax.ShapeDtypeStruct((t, 2 * c), BF16), jax.ShapeDtypeStruct((HALO, c), F32),
                   jax.ShapeDtypeStruct((1, 2 * c), F32)],
        args=[dzc, dzc, z, p, wdw],
        scratch=[pltpu.VMEM((SUBLANES, n, c), F32), pltpu.VMEM((tm, c), F32), pltpu.VMEM((HALO, SUBLANES, c), F32)],
        sem=("arbitrary",))


def _adamw(name, w, g, m, v, tr, carry=None):
    r, c = w.shape
    c1 = 1.0 - ADAM_B1 ** ADAM_STEP
    c2 = 1.0 - ADAM_B2 ** ADAM_STEP

    def body(w_ref, g_ref, m_ref, v_ref, go_ref, d_ref, mo_ref, vo_ref):
        gv = g_ref[...]
        mn = ADAM_B1 * m_ref[...] + (1.0 - ADAM_B1) * gv
        vn = ADAM_B2 * v_ref[...] + (1.0 - ADAM_B2) * (gv * gv)
        go_ref[...] = gv
        d_ref[...] = -ADAM_LR * ((mn / c1) / (jnp.sqrt(vn / c2) + ADAM_EPS) + ADAM_WD * w_ref[...])
        mo_ref[...] = mn
        vo_ref[...] = vn

    spec = _rows(tr, c)
    return _call(body, name=name, grid=(r // tr,), in_specs=[spec] * 4, out_specs=[spec] * 4,
                 out_shape=[jax.ShapeDtypeStruct((r, c), F32)] * 4, args=[w, g, m, v], sem=("parallel",), carry=carry)


_RELATION_XOR = (2, 1, 3)


def _pair_sum(name, mine, recv, qc, *, part_shape, tr, in_map):
    pr, pc = part_shape

    def body(qc_ref, a_ref, b_ref, oh_ref):
        oh_ref[...] = (a_ref[...] + b_ref[...].astype(F32)).astype(BF16)

    def imap(k, i, qc_ref):
        part = qc_ref[0]
        for kk, m in enumerate(_RELATION_XOR):
            part = jnp.where(k == kk, jnp.bitwise_xor(qc_ref[0], m), part)
        return in_map(i, part, qc_ref[1])

    ispec = pl.BlockSpec((tr, pc), imap)
    ospec = pl.BlockSpec((None, tr, pc), lambda k, i, qc_ref: (k, i, 0))
    return pl.pallas_call(
        body, name=name,
        grid_spec=pltpu.PrefetchScalarGridSpec(num_scalar_prefetch=1, grid=(3, pr // tr), in_specs=[ispec, ispec],
                                               out_specs=ospec),
        out_shape=jax.ShapeDtypeStruct((3, pr, pc), BF16), compiler_params=_cparams(("parallel", "parallel")))(qc, mine, recv)


def _final_sum(name, mine, recv1, recv2, qc, *, part_shape, out_shape, tr, in_map, out_map):
    pr, pc = part_shape

    def body(qc_ref, a_ref, b_ref, r_ref, o_ref):
        own = a_ref[...] + b_ref[...].astype(F32)
        o_ref[...] = ((own + r_ref[0].astype(F32)) + r_ref[1].astype(F32)) + r_ref[2].astype(F32)

    ispec = pl.BlockSpec((tr, pc), lambda i, qc_ref: in_map(i, qc_ref[0], qc_ref[1]))
    return pl.pallas_call(
        body, name=name,
        grid_spec=pltpu.PrefetchScalarGridSpec(
            num_scalar_prefetch=1, grid=(pr // tr,),
            in_specs=[ispec, ispec, pl.BlockSpec((3, tr, pc), lambda i, qc_ref: (0, i, 0))],
            out_specs=pl.BlockSpec((tr, pc), lambda i, qc_ref: out_map(i, qc_ref[0], qc_ref[1]))),
        out_shape=jax.ShapeDtypeStruct(out_shape, F32), compiler_params=_cparams(("parallel",)))(qc, mine, recv1, recv2)


def _sum8(name, slots):
    def body(s_ref, o_ref):
        acc = s_ref[0]
        for d in range(1, 8):
            acc = acc + s_ref[d]
        o_ref[...] = acc

    return pl.pallas_call(body, name=name, out_shape=jax.ShapeDtypeStruct(slots.shape[1:], F32),
                          in_specs=[pl.BlockSpec(memory_space=pltpu.VMEM)], out_specs=pl.BlockSpec(memory_space=pltpu.VMEM),
                          compiler_params=pltpu.CompilerParams(vmem_limit_bytes=VMEM_LIMIT))(slots)


def _chips(x, y):
    return [(1 - x, y), (x, 1 - y), (1 - x, 1 - y)]


class _Big:
    def __init__(self, name, w, m, v, ax):
        self.name, self.w, self.m, self.v, self.ax = name, w, m, v, ax
        sr, sc = w.shape
        self.R, self.C = (sr * N_CHIPS, sc) if ax == 0 else (sr, sc * N_CHIPS)
        self.sr, self.sc = sr, sc
        self.hr = sr // 2

    def slot(self, ref, q, h=None):
        if self.ax == 1:
            cols = pl.ds(pl.multiple_of(q * self.sc, 128), self.sc)
            return ref.at[:, cols] if h is None else ref.at[pl.ds(pl.multiple_of(h * self.hr, 16), self.hr), cols]
        if h is None:
            return ref.at[pl.ds(pl.multiple_of(q * self.sr, 16), self.sr), :]
        return ref.at[pl.ds(pl.multiple_of(q * self.sr + h * self.hr, 16), self.hr), :]

    def half(self, ref, h):
        return ref.at[pl.ds(pl.multiple_of(h * self.hr, 16), self.hr), :]

    @property
    def part_shape(self):
        return (self.hr, self.sc)

    def cast_into_full(self, qc):
        tr = _tile_rows(self.sr, self.sc)
        nb = self.sr // tr

        def body(qc_ref, x_ref, o_ref):
            o_ref[...] = x_ref[...].astype(BF16)

        if self.ax == 1:
            ospec = pl.BlockSpec((tr, self.sc), lambda i, qc_ref: (i, qc_ref[0]))
        else:
            ospec = pl.BlockSpec((tr, self.sc), lambda i, qc_ref: (qc_ref[0] * nb + i, 0))
        return pl.pallas_call(
            body, name=f"cast_{self.name}",
            grid_spec=pltpu.PrefetchScalarGridSpec(num_scalar_prefetch=1, grid=(nb,),
                                                   in_specs=[pl.BlockSpec((tr, self.sc), lambda i, qc_ref: (i, 0))],
                                                   out_specs=ospec),
            out_shape=jax.ShapeDtypeStruct((self.R, self.C), BF16), compiler_params=_cparams(("parallel",)))(qc, self.w)

    def gather_ici(self, full, piece=(0, 1)):
        k, n = piece
        pr = self.hr // n
        assert pr * n == self.hr and pr % 16 == 0

        def plan(ro, rw, new, x, y, c):
            q = 2 * x + y
            r0 = c * self.hr + k * pr
            if self.ax == 1:
                mine = rw[0].at[pl.ds(pl.multiple_of(r0, 16), pr), pl.ds(pl.multiple_of(q * self.sc, 128), self.sc)]
            else:
                mine = rw[0].at[pl.ds(pl.multiple_of(q * self.sr + r0, 16), pr), :]
            return [(mine, mine, (cx, cy, c)) for cx, cy in _chips(x, y)], []

        return _Carry([], [full], [], plan, 3, 0)

    def gather_d2d(self, full):
        def plan(ro, rw, new, x, y, c):
            remote = []
            for cx, cy in _chips(x, y):
                piece = self.slot(rw[0], 2 * cx + cy, c)
                remote.append((piece, piece, (x, y, 1 - c)))
            return remote, []

        return _Carry([], [full], [], plan, 3)

    def rs_pair(self, g16):
        def plan(ro, rw, new, x, y, c):
            sib = (x, y, 1 - c)
            if self.ax == 1:
                rows = pl.ds(pl.multiple_of((1 - c) * self.hr, 16), self.hr)
                return [(ro[0].at[rows, :], new[0].at[rows, :], sib)], []
            return [(self.slot(ro[0], q, 1 - c), self.slot(new[0], q, 1 - c), sib) for q in range(N_CHIPS)], []

        return _Carry([g16], [], [jax.ShapeDtypeStruct((self.R, self.C), BF16)], plan, 1 if self.ax == 1 else N_CHIPS)

    def _piece_map(self, nb):
        if self.ax == 1:
            return lambda i, q, c: (c * nb + i, q)
        return lambda i, q, c: ((q * 2 + c) * nb + i, 0)

    def rs_pairsum(self, tag, g32, recv1, qc):
        tr = _tile_rows(self.hr, self.sc)
        return _pair_sum(f"rs_pairsum_{tag}", g32, recv1, qc, part_shape=self.part_shape, tr=tr,
                         in_map=self._piece_map(self.hr // tr))

    def rs_ici(self, cs16):
        def plan(ro, rw, new, x, y, c):
            return [(ro[0].at[k], new[0].at[k], (cx, cy, c)) for k, (cx, cy) in enumerate(_chips(x, y))], []

        return _Carry([cs16], [], [jax.ShapeDtypeStruct((3,) + self.part_shape, BF16)], plan, 3)

    def rs_final(self, tag, g32, recv1, recv2, qc):
        tr = _tile_rows(self.hr, self.sc)
        nb = self.hr // tr
        in_map = self._piece_map(nb)
        out_map = lambda i, q, c: (c * nb + i, 0)
        return _final_sum(f"rs_final_{tag}", g32, recv1, recv2, qc, part_shape=self.part_shape, out_shape=(self.sr, self.sc),
                          tr=tr, in_map=in_map, out_map=out_map)

    def rs_share(self, ghalf):
        def plan(ro, rw, new, x, y, c):
            piece = self.half(rw[0], c)
            return [(piece, piece, (x, y, 1 - c))], []

        return _Carry([], [ghalf], [], plan, 1)


def _small_allgather(packed):
    nr = packed.shape[0]

    def plan(ro, rw, new, x, y, c):
        me = 4 * x + 2 * y + c
        remote = []
        for fx in (0, 1):
            for fy in (0, 1):
                for fc in (0, 1):
                    if fx or fy or fc:
                        dev = (1 - x if fx else x, 1 - y if fy else y, 1 - c if fc else c)
                        remote.append((ro[0], new[0].at[me], dev))
        return remote, [(ro[0], new[0].at[me])]

    return _Carry([packed], [], [jax.ShapeDtypeStruct((8, nr, 128), F32)], plan, 7, 1)


def _conv_w_allgather(padded, cs):
    def plan(ro, rw, new, x, y, c):
        cols = pl.ds(pl.multiple_of((2 * x + y) * cs, 128), cs)
        remote = [(ro[0], new[0].at[:, cols], (cx, cy, c)) for cx, cy in _chips(x, y)]
        return remote, [(ro[0], new[0].at[:, cols])]

    return _Carry([padded], [], [jax.ShapeDtypeStruct((HALO, cs * N_CHIPS), F32)], plan, 3, 1)


def _pick(n, want):
    if n <= want:
        return n
    for t in range(want, 15, -16):
        if t % 16 == 0 and n % t == 0:
            return t
    raise ValueError(f"no tile for {n} (want {want})")


def _tile_rows(nrows, ncols, budget=2 * 1024 * 1024):
    return _pick(nrows, max(16, (budget // (4 * ncols)) // 16 * 16))


def _pick128(n, want):
    if n <= want:
        return n
    for t in range(want, 127, -128):
        if n % t == 0:
            return t
    raise ValueError(f"no lane tile for {n} (want {want})")


def _pack_rows(parts):
    out, spans, r0 = [], [], 0
    for p in parts:
        flat = p.reshape(-1).astype(F32)
        n = flat.shape[0]
        rows = -(-n // 1024) * 8
        flat = jnp.pad(flat, (0, rows * 128 - n))
        out.append(flat.reshape(rows, 128))
        spans.append((r0, rows, n))
        r0 += rows
    return jnp.concatenate(out, axis=0), spans


def _unpack_rows(packed, spans, shapes):
    res = []
    for (r0, rows, n), shp in zip(spans, shapes, strict=True):
        res.append(packed[r0:r0 + rows].reshape(-1)[:n].reshape(shp))
    return res


def _ident(accs, ex):
    return [accs[0]]


def kernel(x, ffn1_w_gu, ffn1_w_down, ln1_g, ln1_b, w_in, b_in, sgu_ln_g, sgu_ln_b, sgu_w_s, sgu_b_s, w_a_proj, conv_w_dw, conv_b_dw, conv_ln_g, conv_ln_b, w_b_proj, w_out, ln2_g, ln2_b, ffn2_w_gu, ffn2_w_down, ln3_g, ln3_b, loss_target, m_ffn1_w_gu, m_ffn1_w_down, m_ln1_g, m_ln1_b, m_w_in, m_b_in, m_sgu_ln_g, m_sgu_ln_b, m_sgu_w_s, m_sgu_b_s, m_w_a_proj, m_conv_w_dw, m_conv_b_dw, m_conv_ln_g, m_conv_ln_b, m_w_b_proj, m_w_out, m_ln2_g, m_ln2_b, m_ffn2_w_gu, m_ffn2_w_down, m_ln3_g, m_ln3_b, v_ffn1_w_gu, v_ffn1_w_down, v_ln1_g, v_ln1_b, v_w_in, v_b_in, v_sgu_ln_g, v_sgu_ln_b, v_sgu_w_s, v_sgu_b_s, v_w_a_proj, v_conv_w_dw, v_conv_b_dw, v_conv_ln_g, v_conv_ln_b, v_w_b_proj, v_w_out, v_ln2_g, v_ln2_b, v_ffn2_w_gu, v_ffn2_w_down, v_ln3_g, v_ln3_b):
    args = dict(locals())
    assert x.shape[0] == 1 and ffn1_w_gu.shape[0] == 1
    T, D = x.shape[1], x.shape[2]
    F = ffn1_w_down.shape[1] * N_CHIPS
    W = sgu_ln_g.shape[1]
    KW = conv_w_dw.shape[1]
    assert KW - 1 <= HALO and T % SGU_BLOCK == 0

    mx, my, mc = lax.axis_index("x"), lax.axis_index("y"), lax.axis_index("c")
    q = 2 * mx + my
    qc = jnp.stack([q, mc]).astype(jnp.int32)

    big_names = [("ffn1_w_gu", 1), ("ffn1_w_down", 0), ("w_in", 1), ("w_a_proj", 1), ("w_b_proj", 1), ("w_out", 0),
                 ("ffn2_w_gu", 1), ("ffn2_w_down", 0)]
    B = {n: _Big(n, args[n][0], args["m_" + n][0], args["v_" + n][0], ax) for n, ax in big_names}
    own = {n: b.cast_into_full(qc) for n, b in B.items()}

    x2d = x[0]
    tgt = loss_target[0]
    tm_r = _pick(T, 256)
    tm_ln = _pick(T, 512)
    tm = _pick(T, 1024)
    tn = _pick128(D, 1024)
    nj = D // tn
    tng = _pick128(D, 512)
    njg = D // tng
    tnf = _pick128(F, 512)
    nf = F // tnf
    tnw = _pick128(W, 1024)
    tnd = _pick128(D, 512)
    tmw = _pick128(W, 512)
    SUB = 256

    def ffn_up(tag, xb_, wgu, carry=None):
        def epi(accs, ex):
            g, u = accs
            s = jax.nn.sigmoid(g)
            sg = g * s
            return [u * (s * (1.0 + g * (1.0 - s))), sg, sg * u]

        return _mm(f"{tag}_up", [(xb_, wgu, 0, 0, 0), (xb_, wgu, 0, 0, nf)], M=T, N=F, tm=tm, tn=tnf, tk=D, nk=1, epilogue=epi,
                   outs=[(F, BF16, 0)] * 3, sum_pairs=False, carry=carry, sub=SUB)

    def ffn_down(tag, act, wd, carry=None):
        return _mm(f"{tag}_down", [(act, wd, 0, 0, 0)], M=T, N=D, tm=tm, tn=_pick128(D, 512), tk=F, nk=1, epilogue=_ident,
                   outs=[(D, F32, 0)], carry=carry)

    def ffn_dact(tag, drh, wd, dgate_f, dup_f, carry=None):
        def epi(accs, ex):
            da = accs[0]
            return [da * ex[0].astype(F32), da * ex[1].astype(F32)]

        return _mm(f"{tag}_dact", [(drh, wd, 0, 0, 0)], tb=True, M=T, N=F, tm=_pick(T, 2048), tn=tnf, tk=D, nk=1, epilogue=epi,
                   outs=[(F, BF16, 0)] * 2, extras=[(dgate_f, "mn", 0), (dup_f, "mn", 0)], carry=carry, sub=SUB)

    def ffn_dwdown(tag, act, drh, carry=None):
        return _wgrad(f"{tag}_dwdown", act, drh, M=F, N=D, T=T, tm=tnf, tn=tnd, tk=T, carry=carry)

    def ffn_dwgate(tag, xb_, dg, carry=None):
        return _wgrad(f"{tag}_dwgate", xb_, dg, M=D, N=F, T=T, tm=tnd, tn=tnf, tk=T, ncols=2 * F, carry=carry)

    def ffn_dwup(tag, xb_, du, into, carry=None):
        return _wgrad(f"{tag}_dwup", xb_, du, M=D, N=F, T=T, tm=tnd, tn=tnf, tk=T, ncols=2 * F, into=into, joff=nf, carry=carry)

    def ffn_dx(tag, which, da, wgu, addends, carry=None):
        def epi(accs, ex):
            tot = accs[0] + ALPHA * ex[0]
            for e in ex[1:]:
                tot = tot + e
            return [tot]

        return _mm(f"{tag}_dx_{which}", [(da, wgu, 0, 1 if which == "up" else 0, 0)], tb=True, M=T, N=D, tm=tm,
                   tn=_pick128(D, 256), tk=F, nk=1, epilogue=epi if addends else _ident, outs=[(D, F32, 0)],
                   extras=[(a, "mn", 0) for a in addends], carry=carry)

    wdw_pad = jnp.pad(conv_w_dw[0], ((0, HALO - KW), (0, 0)))
    c0, un = _merge(B["ffn1_w_gu"].gather_ici(own["ffn1_w_gu"]), _conv_w_allgather(wdw_pad, conv_w_dw.shape[2]))
    (xb,), co = _cast_bf16("cast_x", x2d, tm_r, carry=c0)
    (wgu1,), (wdw_full,) = un(co)
    (wgu1,) = _exchange("gather_d2d_ffn1_w_gu", B["ffn1_w_gu"].gather_d2d(wgu1))

    c, un = _merge(B["ffn1_w_down"].gather_ici(own["ffn1_w_down"]), B["w_in"].gather_ici(own["w_in"]))
    (g1, u1, a1), co = ffn_up("ffn1", xb, wgu1, carry=c)
    (wd1,), (win,) = un(co)
    (wd1,) = _exchange("gather_d2d_ffn1_w_down", B["ffn1_w_down"].gather_d2d(wd1))
    b_gu2, b_d2 = B["ffn2_w_gu"], B["ffn2_w_down"]
    c, un = _merge(B["w_a_proj"].gather_ici(own["w_a_proj"]), B["w_b_proj"].gather_ici(own["w_b_proj"]),
                   B["w_out"].gather_ici(own["w_out"]), B["w_in"].gather_d2d(win), b_gu2.gather_ici(own["ffn2_w_gu"], (0, 4)))
    (fo1,), co = ffn_down("ffn1", a1, wd1, carry=c)
    (wa,), (wb,), (wout,), (win,), (wgu2,) = un(co)
    c, un = _merge(B["w_a_proj"].gather_d2d(wa), B["w_b_proj"].gather_d2d(wb), B["w_out"].gather_d2d(wout),
                   b_gu2.gather_ici(wgu2, (1, 4)))
    (x1b, xh1, rs1), co = _ln_fwd("ln1", x2d, fo1, ln1_g, ln1_b, 0.5, tm_ln, carry=c)
    (wa,), (wb,), (wout,), (wgu2,) = un(co)

    c, un = _merge(b_gu2.gather_ici(wgu2, (1, 2)), b_d2.gather_ici(own["ffn2_w_down"], (0, 2)))
    (proj,), co = _mm("in_proj", [(x1b, win, 0, 0, 0)], M=T, N=4 * D, tm=tm, tn=tn, tk=D, nk=1,
                      epilogue=lambda accs, ex: [accs[0] + ex[0]], outs=[(4 * D, F32, 0)], extras=[(b_in, "n", 0)], carry=c)
    (wgu2,), (wd2,) = un(co)
    wm = sgu_w_s[0]
    bst = sgu_b_s[0].T
    sa, z = _sgu_fwd("sgu_fwd", proj, sgu_ln_g, sgu_ln_b, wm, bst, tm_r)
    (zc, sb), (wd2,) = _conv_fwd("conv_fwd", z, wdw_full, KW, conv_b_dw, conv_ln_g, conv_ln_b, tm_r,
                                 carry=b_d2.gather_ici(wd2, (1, 2)))

    def epi_mix(accs, ex):
        ya_, yb_ = accs
        ga, gb = jax.nn.sigmoid(ex[0]), jax.nn.sigmoid(ex[1])
        return [ga * ya_ + gb * yb_, ga, gb, ya_ * (ga * (1.0 - ga)), yb_ * (gb * (1.0 - gb))]

    mixin, gate_a, gate_b, dlog_a, dlog_b = _mm(
        "branch_proj", [(sa, wa, 0, 0, 0), (sb, wb, 0, 0, 0)], M=T, N=D, tm=tm, tn=tng, tk=W, nk=1, epilogue=epi_mix,
        outs=[(D, BF16, 0)] * 5, sum_pairs=False, sub=SUB, extras=[(proj, "mn", 2 * njg), (proj, "mn", 3 * njg)])
    c, un = _merge(b_gu2.gather_d2d(wgu2), b_d2.gather_d2d(wd2))
    (mix,), co = _mm("out_proj", [(mixin, wout, 0, 0, 0)], M=T, N=D, tm=tm, tn=tn, tk=D, nk=1, epilogue=_ident,
                     outs=[(D, F32, 0)], carry=c)
    (wgu2,), (wd2,) = un(co)
    x2b, xh2, rs2 = _ln_fwd("ln2", xh1, mix, ln2_g, ln2_b, 1.0, tm_ln, prev=(ln1_g, ln1_b))
    g2, u2, a2 = ffn_up("ffn2", x2b, wgu2)
    (fo2,) = ffn_down("ffn2", a2, wd2)
    dr3, dr3h, loss_part, dln3_g, dln3_b = _ln3_loss("ln3_loss", xh2, fo2, ln3_g, ln3_b, tgt, tm_r, (ln2_g, ln2_b))

    b_gu2, b_d2 = B["ffn2_w_gu"], B["ffn2_w_down"]
    dg2, du2 = ffn_dact("ffn2", dr3h, wd2, g2, u2)
    dwd2 = ffn_dwdown("ffn2", a2, dr3h)
    dwgu2, (r1_d2,) = ffn_dwgate("ffn2", x2b, dg2, carry=b_d2.rs_pair(dwd2[1]))
    dwgu2 = ffn_dwup("ffn2", x2b, du2, dwgu2)
    cs_d2 = b_d2.rs_pairsum("ffn2_w_down", dwd2[0], r1_d2, qc)
    c, un = _merge(b_gu2.rs_pair(dwgu2[1]), b_d2.rs_ici(cs_d2))
    (dx2_gate,), co = ffn_dx("ffn2", "gate", dg2, wgu2, [], carry=c)
    (r1_gu2,), (r2_d2,) = un(co)
    (dx2,) = ffn_dx("ffn2", "up", du2, wgu2, [dr3, dx2_gate])
    cs_gu2 = b_gu2.rs_pairsum("ffn2_w_gu", dwgu2[0], r1_gu2, qc)
    gh_d2 = b_d2.rs_final("ffn2_w_down", dwd2[0], r1_d2, r2_d2, qc)

    dr2, dr2b, dln2_g, dln2_b = _ln_bwd("ln2_bwd", dx2, xh2, rs2, ln2_g, 1.0, tm_ln)

    def epi_dmix(accs, ex):
        tiles = [accs[0] * e.astype(F32) for e in ex]
        sums = [t.reshape(-1, SUBLANES, t.shape[1]).sum(axis=0) for t in tiles[2:]]
        return tiles + sums

    dya, dyb, dla, dlb, sla, slb = _mm(
        "out_proj_bwd", [(dr2b, wout, 0, 0, 0)], tb=True, M=T, N=D, tm=tm, tn=tng, tk=D, nk=1, epilogue=epi_dmix,
        outs=[(D, BF16, 0)] * 4 + [(D, F32, 0, SUBLANES)] * 2, sub=SUB,
        extras=[(gate_a, "mn", 0), (gate_b, "mn", 0), (dlog_a, "mn", 0), (dlog_b, "mn", 0)])
    dwout = _wgrad("dw_out", mixin, dr2b, M=D, N=D, T=T, tm=tnd, tn=tnd, tk=T)
    (dsa,) = _mm("a_proj_bwd", [(dya, wa, 0, 0, 0)], tb=True, M=T, N=W, tm=tm, tn=tnw, tk=D, nk=1, epilogue=_ident,
                 outs=[(W, F32, 0)])
    (dsb,) = _mm("b_proj_bwd", [(dyb, wb, 0, 0, 0)], tb=True, M=T, N=W, tm=tm, tn=tnw, tk=D, nk=1, epilogue=_ident,
                 outs=[(W, F32, 0)])
    dwa = _wgrad("dw_a_proj", sa, dya, M=W, N=D, T=T, tm=tmw, tn=tnd, tk=T)
    dwb = _wgrad("dw_b_proj", sb, dyb, M=W, N=D, T=T, tm=tmw, tn=tnd, tk=T)

    dpa, dwm, dbst, dsgu_g, dsgu_b, dbin_a = _sgu_bwd("sgu_bwd", proj, dsa, sgu_ln_g, sgu_ln_b, wm, bst, tm_r)
    dzc, dcln_g, dcln_b, dbdw = _conv_ln_bwd("conv_ln_bwd", dsb, zc, conv_ln_g, conv_ln_b, tm_r)
    dpb, dwdw, dbin_b = _conv_bwd("conv_bwd", dzc, z, proj, wdw_full, KW, tm_r)

    dps = [dpa, dpb, dla, dlb]
    db_in = jnp.concatenate([dbin_a, dbin_b, _colsum_rows("db_in_gate_a", sla, sla.shape[0]),
                             _colsum_rows("db_in_gate_b", slb, slb.shape[0])], axis=1)
    c, un = _merge(b_gu2.rs_ici(cs_gu2), b_d2.rs_share(gh_d2))
    (dx1,), co = _mm("in_proj_bwd", [(dp, win, 0, k, 0) for k, dp in enumerate(dps)], tb=True, M=T, N=D, tm=tm,
                     tn=_pick128(D, 256), tk=D, nk=1, epilogue=lambda accs, ex: [accs[0] + ALPHA * ex[0]], outs=[(D, F32, 0)],
                     extras=[(dr2, "mn", 0)], carry=c)
    (r2_gu2,), (g_d2,) = un(co)
    gh_gu2 = b_gu2.rs_final("ffn2_w_gu", dwgu2[0], r1_gu2, r2_gu2, qc)
    dwin = _wgrad("dw_in_0", x1b, dps[0], M=D, N=D, T=T, tm=tnd, tn=tnd, tk=T, ncols=4 * D)
    for k in range(1, 4):
        dwin = _wgrad(f"dw_in_{k}", x1b, dps[k], M=D, N=D, T=T, tm=tnd, tn=tnd, tk=T, ncols=4 * D, into=dwin,
                      joff=k * (D // tnd))

    mix_names = ["w_in", "w_a_proj", "w_b_proj", "w_out"]
    mix_grads = dict(zip(mix_names, [dwin, dwa, dwb, dwout], strict=True))
    c, un = _merge(*[B[n].rs_pair(mix_grads[n][1]) for n in mix_names])
    (dr1, dr1h, dln1_g, dln1_b), co = _ln_bwd("ln1_bwd", dx1, xh1, rs1, ln1_g, 0.5, tm_ln, carry=c)
    r1_mix = {n: r1 for n, (r1,) in zip(mix_names, un(co), strict=True)}
    cs_mix = {n: B[n].rs_pairsum(n, mix_grads[n][0], r1_mix[n], qc) for n in mix_names}

    small_names = ["ln1_g", "ln1_b", "b_in", "sgu_ln_g", "sgu_ln_b", "sgu_w_s", "sgu_b_s", "conv_w_dw", "conv_b_dw",
                   "conv_ln_g", "conv_ln_b", "ln2_g", "ln2_b", "ln3_g", "ln3_b"]
    small_parts = {"ln1_g": dln1_g, "ln1_b": dln1_b, "b_in": db_in, "sgu_ln_g": dsgu_g, "sgu_ln_b": dsgu_b, "sgu_w_s": dwm,
                   "sgu_b_s": dbst.T, "conv_w_dw": dwdw[:KW], "conv_b_dw": dbdw, "conv_ln_g": dcln_g, "conv_ln_b": dcln_b,
                   "ln2_g": dln2_g, "ln2_b": dln2_b, "ln3_g": dln3_g, "ln3_b": dln3_b}
    packed, spans = _pack_rows([small_parts[n] for n in small_names])

    b_gu1, b_d1 = B["ffn1_w_gu"], B["ffn1_w_down"]
    c, un = _merge(B["w_in"].rs_ici(cs_mix["w_in"]), _small_allgather(packed), b_gu2.rs_share(gh_gu2))
    (dg1, du1), co = ffn_dact("ffn1", dr1h, wd1, g1, u1, carry=c)
    (r2_win,), (small_slots,), (g_gu2,) = un(co)
    c, un = _merge(*[B[n].rs_ici(cs_mix[n]) for n in mix_names[1:]])
    dwgu1, co = ffn_dwgate("ffn1", xb, dg1, carry=c)
    r2_mix = [[r2_win]] + un(co)
    gh_mix = [B[n].rs_final(n, mix_grads[n][0], r1_mix[n], r2, qc) for n, (r2,) in zip(mix_names, r2_mix, strict=True)]
    dwgu1 = ffn_dwup("ffn1", xb, du1, dwgu1)
    c, un = _merge(*[B[n].rs_share(gh) for n, gh in zip(mix_names, gh_mix, strict=True)], b_gu1.rs_pair(dwgu1[1]))
    dwd1, co = ffn_dwdown("ffn1", a1, dr1h, carry=c)
    *g_mixs, (r1_gu1,) = un(co)
    g_mix = {n: g for n, (g,) in zip(mix_names, g_mixs, strict=True)}
    cs_gu1 = b_gu1.rs_pairsum("ffn1_w_gu", dwgu1[0], r1_gu1, qc)
    c, un = _merge(b_gu1.rs_ici(cs_gu1), b_d1.rs_pair(dwd1[1]))
    (dx_gate,), co = ffn_dx("ffn1", "gate", dg1, wgu1, [], carry=c)
    (r2_gu1,), (r1_d1,) = un(co)
    cs_d1 = b_d1.rs_pairsum("ffn1_w_down", dwd1[0], r1_d1, qc)
    gh_gu1 = b_gu1.rs_final("ffn1_w_gu", dwgu1[0], r1_gu1, r2_gu1, qc)
    c, un = _merge(b_d1.rs_ici(cs_d1), b_gu1.rs_share(gh_gu1))
    (dx,), co = ffn_dx("ffn1", "up", du1, wgu1, [dr1, dx_gate], carry=c)
    (r2_d1,), (g_gu1,) = un(co)
    gh_d1 = b_d1.rs_final("ffn1_w_down", dwd1[0], r1_d1, r2_d1, qc)

    grads = {"ffn1_w_gu": g_gu1, "ffn2_w_gu": g_gu2, "ffn2_w_down": g_d2, **g_mix}
    outs_g, outs_d, outs_m, outs_v = {}, {}, {}, {}

    def adamw_big(n, g, carry=None):
        b = B[n]
        return _adamw(f"adamw_{n}", b.w, g, b.m, b.v, _tile_rows(b.sr, b.sc, 1024 * 1024), carry=carry)

    upd = {}
    upd["w_a_proj"], (grads["ffn1_w_down"],) = adamw_big("w_a_proj", grads["w_a_proj"], carry=b_d1.rs_share(gh_d1))
    for n, _ in big_names:
        if n not in upd:
            upd[n] = adamw_big(n, grads[n])
        g_, d_, m_, v_ = upd[n]
        outs_g[n], outs_d[n], outs_m[n], outs_v[n] = g_[None], d_[None], m_[None], v_[None]
    gsum = _sum8("small_sum", small_slots)
    full_shapes = [args[n].shape if n != "conv_w_dw" else (1, KW, W) for n in small_names]
    gsmall = dict(zip(small_names, _unpack_rows(gsum, spans, full_shapes), strict=True))
    cs = conv_w_dw.shape[2]
    gsmall["conv_w_dw"] = lax.dynamic_slice_in_dim(gsmall["conv_w_dw"], q * cs, cs, axis=2)
    pw, spans2 = _pack_rows([args[n] for n in small_names])
    pg, _ = _pack_rows([gsmall[n] for n in small_names])
    pm, _ = _pack_rows([args["m_" + n] for n in small_names])
    pv, _ = _pack_rows([args["v_" + n] for n in small_names])
    _, pd, pmn, pvn = _adamw("adamw_small", pw, pg, pm, pv, pw.shape[0])
    shapes2 = [args[n].shape for n in small_names]
    for dst, src in ((outs_d, pd), (outs_m, pmn), (outs_v, pvn)):
        dst.update(zip(small_names, _unpack_rows(src, spans2, shapes2), strict=True))
    outs_g.update(gsmall)

    loss = lax.psum(loss_part[0, 0], ("x", "y", "c"))
    order = ["ffn1_w_gu", "ffn1_w_down", "ln1_g", "ln1_b", "w_in", "b_in", "sgu_ln_g", "sgu_ln_b", "sgu_w_s", "sgu_b_s",
             "w_a_proj", "conv_w_dw", "conv_b_dw", "conv_ln_g", "conv_ln_b", "w_b_proj", "w_out", "ln2_g", "ln2_b",
             "ffn2_w_gu", "ffn2_w_down", "ln3_g", "ln3_b"]
    return (loss, dx[None], *[outs_g[n] for n in order], *[outs_d[n] for n in order], *[outs_m[n] for n in order],
            *[outs_v[n] for n in order])
```

```python
import math

import jax
import jax.numpy as jnp
from jax import lax
from jax.experimental import pallas as pl
from jax.experimental.pallas import tpu as pltpu

BF16 = jnp.bfloat16
F32 = jnp.float32

LN_EPS = 1e-5
ALPHA = 2.0 ** 0.25
SGU_BLOCK = 128
SGU_CHUNK = 64
HALO = 32
SUBLANES = 8
LANES = 128
CONV_ROWS = 32
ADAM_LR = 0.001
ADAM_B1 = 0.9
ADAM_B2 = 0.999
ADAM_EPS = 1e-08
ADAM_WD = 0.01
ADAM_STEP = 10
N_CHIPS = 4
VMEM_LIMIT = 52 * 1024 * 1024
MESH = pl.DeviceIdType.MESH

_GELU_C0 = math.sqrt(2.0 / math.pi)
_GELU_C1 = 0.044715


def _cparams(sem):
    return pltpu.CompilerParams(dimension_semantics=sem, vmem_limit_bytes=VMEM_LIMIT)


def _gelu_parts(x):
    x2 = x * x
    t = jnp.tanh(_GELU_C0 * (x + _GELU_C1 * (x2 * x)))
    return 0.5 * (1.0 + t), t, x2


def _gelu(x):
    cdf, _, _ = _gelu_parts(x)
    return x * cdf


def _gelu_and_grad(x):
    cdf, t, x2 = _gelu_parts(x)
    grad = cdf + x * (0.5 * (1.0 - t * t)) * (_GELU_C0 * (1.0 + (3.0 * _GELU_C1) * x2))
    return x * cdf, grad


def _silu_grad(x):
    s = jax.nn.sigmoid(x)
    return s * (1.0 + x * (1.0 - s))


def _row_stats(x):
    mu = jnp.mean(x, axis=-1, keepdims=True)
    xc = x - mu
    var = jnp.mean(xc * xc, axis=-1, keepdims=True)
    rstd = lax.rsqrt(var + LN_EPS)
    return xc * rstd, rstd


def _ln_bwd_rows(dy, xhat, rstd, g):
    dxh = dy * g
    m1 = jnp.mean(dxh, axis=-1, keepdims=True)
    m2 = jnp.mean(dxh * xhat, axis=-1, keepdims=True)
    return rstd * (dxh - m1 - xhat * m2)


def _colsum(v):
    return jnp.sum(v, axis=0, keepdims=True)


class _Carry:
    def __init__(self, ro, rw, new, plan, n_remote, n_local=0):
        self.ro, self.rw, self.new, self.plan = list(ro), list(rw), list(new), plan
        self.n_remote, self.n_local = n_remote, n_local

    def sems(self):
        return [pltpu.SemaphoreType.DMA((self.n_remote,)), pltpu.SemaphoreType.DMA((self.n_remote,)),
                pltpu.SemaphoreType.DMA((max(self.n_local, 1),))]

    def copies(self, ro_refs, rw_refs, new_refs, send_sems, recv_sems, loc_sems):
        x, y, c = lax.axis_index("x"), lax.axis_index("y"), lax.axis_index("c")
        remote, local = self.plan(ro_refs, rw_refs, new_refs, x, y, c)
        assert len(remote) == self.n_remote and len(local) == self.n_local
        lcs = [pltpu.make_async_copy(s, d, loc_sems.at[k]) for k, (s, d) in enumerate(local)]
        rcs = [pltpu.make_async_remote_copy(src_ref=s, dst_ref=d, send_sem=send_sems.at[k], recv_sem=recv_sems.at[k],
                                            device_id=dev, device_id_type=MESH) for k, (s, d, dev) in enumerate(remote)]
        return lcs, rcs

    def out_shape(self):
        return [jax.ShapeDtypeStruct(a.shape, a.dtype) for a in self.rw] + self.new


def _start_all(lcs, rcs):
    for cp in lcs + rcs:
        cp.start()


def _wait_all(lcs, rcs):
    for cp in rcs:
        cp.wait_send()
    for cp in rcs:
        cp.wait_recv()
    for cp in lcs:
        cp.wait()


def _merge(*cs):
    ro = [a for c in cs for a in c.ro]
    rw = [a for c in cs for a in c.rw]
    new = [a for c in cs for a in c.new]

    def plan(ro_refs, rw_refs, new_refs, x, y, c):
        remote, local, a, b, d = [], [], 0, 0, 0
        for cc in cs:
            r, l = cc.plan(ro_refs[a:a + len(cc.ro)], rw_refs[b:b + len(cc.rw)], new_refs[d:d + len(cc.new)], x, y, c)
            a, b, d = a + len(cc.ro), b + len(cc.rw), d + len(cc.new)
            remote += r
            local += l
        return remote, local

    def unpack(couts):
        res, b, d = [], 0, len(rw)
        for cc in cs:
            res.append(list(couts[b:b + len(cc.rw)]) + list(couts[d:d + len(cc.new)]))
            b, d = b + len(cc.rw), d + len(cc.new)
        return res

    return _Carry(ro, rw, new, plan, sum(c.n_remote for c in cs), sum(c.n_local for c in cs)), unpack


_ANY = pl.BlockSpec(memory_space=pl.ANY)


def _exchange(name, carry):
    n_ro, n_rw, n_new = len(carry.ro), len(carry.rw), len(carry.new)

    def body(*refs):
        o0 = n_ro + n_rw
        lcs, rcs = carry.copies(refs[:n_ro], refs[o0:o0 + n_rw], refs[o0 + n_rw:o0 + n_rw + n_new], *refs[o0 + n_rw + n_new:])
        _start_all(lcs, rcs)
        _wait_all(lcs, rcs)

    return list(pl.pallas_call(
        body, name=name, in_specs=[_ANY] * (n_ro + n_rw), out_specs=[_ANY] * (n_rw + n_new), out_shape=carry.out_shape(),
        input_output_aliases={n_ro + k: k for k in range(n_rw)}, scratch_shapes=carry.sems())(*carry.ro, *carry.rw))


def _call(body, *, name, grid, in_specs, out_specs, out_shape, args, scratch=(), sem, carry=None, aliases=None):
    in_specs, out_specs, out_shape, scratch = list(in_specs), list(out_specs), list(out_shape), list(scratch)
    if carry is None:
        return list(pl.pallas_call(body, name=name, grid=grid, in_specs=in_specs, out_specs=out_specs, out_shape=out_shape,
                                   scratch_shapes=scratch, input_output_aliases=aliases or {},
                                   compiler_params=_cparams(sem))(*args))
    n_in, n_out, n_scr = len(in_specs), len(out_specs), len(scratch)
    n_ro, n_rw, n_new = len(carry.ro), len(carry.rw), len(carry.new)

    def wrapped(*refs):
        ins = refs[:n_in]
        ro_refs = refs[n_in:n_in + n_ro]
        o0 = n_in + n_ro + n_rw
        outs = refs[o0:o0 + n_out]
        rw_refs = refs[o0 + n_out:o0 + n_out + n_rw]
        new_refs = refs[o0 + n_out + n_rw:o0 + n_out + n_rw + n_new]
        s0 = o0 + n_out + n_rw + n_new
        scr = refs[s0:s0 + n_scr]
        sems = refs[s0 + n_scr:]
        first = pl.program_id(0) == 0
        last = pl.program_id(0) == grid[0] - 1
        for d in range(1, len(grid)):
            first = jnp.logical_and(first, pl.program_id(d) == 0)
            last = jnp.logical_and(last, pl.program_id(d) == grid[d] - 1)

        @pl.when(first)
        def _():
            _start_all(*carry.copies(ro_refs, rw_refs, new_refs, *sems))

        body(*ins, *outs, *scr)

        @pl.when(last)
        def _():
            _wait_all(*carry.copies(ro_refs, rw_refs, new_refs, *sems))

    al = dict(aliases or {})
    al.update({n_in + n_ro + k: n_out + k for k in range(n_rw)})
    res = pl.pallas_call(
        wrapped, name=name, grid=grid, in_specs=in_specs + [_ANY] * (n_ro + n_rw), out_specs=out_specs + [_ANY] * (n_rw + n_new),
        out_shape=out_shape + carry.out_shape(), scratch_shapes=scratch + carry.sems(), input_output_aliases=al,
        compiler_params=_cparams(("arbitrary",) * len(grid)))(*args, *carry.ro, *carry.rw)
    return list(res[:n_out]), list(res[n_out:])


def _mm(name, pairs, *, ta=False, tb=False, M, N, tm, tn, tk, nk, epilogue, outs, extras=(), sum_pairs=True, carry=None,
        sub=None):
    ni, nj = M // tm, N // tn
    assert ni * tm == M and nj * tn == N
    n_p = len(pairs)
    n_acc = 1 if sum_pairs else n_p
    in_specs, args = [], []
    for a, b, ak, bk, bj in pairs:
        if ta:
            in_specs.append(pl.BlockSpec((tk, tm), lambda i, j, k, ak=ak: (k + ak, i)))
        else:
            in_specs.append(pl.BlockSpec((tm, tk), lambda i, j, k, ak=ak: (i, k + ak)))
        if tb:
            in_specs.append(pl.BlockSpec((tn, tk), lambda i, j, k, bk=bk, bj=bj: (j + bj, k + bk)))
        else:
            in_specs.append(pl.BlockSpec((tk, tn), lambda i, j, k, bk=bk, bj=bj: (k + bk, j + bj)))
        args += [a, b]
    for arr, kind, jo in extras:
        if kind == "mn":
            in_specs.append(pl.BlockSpec((tm, tn), lambda i, j, k, jo=jo: (i, j + jo)))
        else:
            in_specs.append(pl.BlockSpec((1, tn), lambda i, j, k, jo=jo: (0, j + jo)))
        args.append(arr)
    out_rows = [o[3] if len(o) > 3 else tm for o in outs]
    out_specs = [pl.BlockSpec((r, tn), lambda i, j, k, jo=o[2]: (i, j + jo)) for o, r in zip(outs, out_rows, strict=True)]
    out_shape = [jax.ShapeDtypeStruct((ni * r, o[0]), o[1]) for o, r in zip(outs, out_rows, strict=True)]
    n_ex, n_out = len(extras), len(outs)
    dn = (((0 if ta else 1,), (1 if tb else 0,)), ((), ()))

    def body(*refs):
        ab = refs[: 2 * n_p]
        ex = refs[2 * n_p: 2 * n_p + n_ex]
        o0 = 2 * n_p + n_ex
        out_refs = refs[o0: o0 + n_out]
        acc_refs = refs[o0 + n_out:]

        def dots():
            res = []
            for p in range(n_p):
                a = ab[2 * p][...]
                b = ab[2 * p + 1][...]
                res.append(lax.dot_general(a.astype(BF16), b.astype(BF16), dn, preferred_element_type=F32))
            if sum_pairs:
                tot = res[0]
                for r in res[1:]:
                    tot = tot + r
                res = [tot]
            return res

        def finish(accs):
            tiles = epilogue(accs, [e[...] for e in ex])
            for r, t in zip(out_refs, tiles, strict=True):
                r[...] = t.astype(r.dtype)

        if nk == 1 and sub is not None and tn > sub:
            for s in range(tn // sub):
                cs = slice(s * sub, (s + 1) * sub)
                res = []
                for p in range(n_p):
                    b = ab[2 * p + 1][cs, :] if tb else ab[2 * p + 1][:, cs]
                    res.append(lax.dot_general(ab[2 * p][...].astype(BF16), b.astype(BF16), dn, preferred_element_type=F32))
                if sum_pairs:
                    tot = res[0]
                    for r in res[1:]:
                        tot = tot + r
                    res = [tot]
                tiles = epilogue(res, [e[:, cs] for e in ex])
                for r, t in zip(out_refs, tiles, strict=True):
                    r[:, cs] = t.astype(r.dtype)
        elif nk == 1:
            finish(dots())
        else:
            k = pl.program_id(2)
            d = dots()

            @pl.when(k == 0)
            def _():
                for r, v in zip(acc_refs, d, strict=True):
                    r[...] = v

            @pl.when(k > 0)
            def _():
                for r, v in zip(acc_refs, d, strict=True):
                    r[...] += v

            @pl.when(k == nk - 1)
            def _():
                finish([r[...] for r in acc_refs])

    scratch = [pltpu.VMEM((tm, tn), F32) for _ in range(n_acc)] if nk > 1 else []
    return _call(body, name=name, grid=(ni, nj, nk), in_specs=in_specs, out_specs=out_specs, out_shape=out_shape, args=args,
                 scratch=scratch, sem=("parallel", "parallel", "arbitrary"), carry=carry)


def _wgrad(name, a, b, *, M, N, T, tm, tn, tk, into=None, joff=0, ncols=None, carry=None):
    ncols = N if ncols is None else ncols
    ni, nj, nk = M // tm, N // tn, T // tk
    dn = (((0,), (0,)), ((), ()))

    def body(a_ref, b_ref, *rest):
        d = lax.dot_general(a_ref[...].astype(BF16), b_ref[...].astype(BF16), dn, preferred_element_type=F32)
        if nk == 1:
            of_ref, oh_ref = rest[-2:]
            of_ref[...] = d
            oh_ref[...] = d.astype(BF16)
            return
        of_ref, oh_ref, acc_ref = rest[-3:]
        k = pl.program_id(2)

        @pl.when(k == 0)
        def _():
            acc_ref[...] = d

        @pl.when(k > 0)
        def _():
            acc_ref[...] += d

        @pl.when(k == nk - 1)
        def _():
            of_ref[...] = acc_ref[...]
            oh_ref[...] = acc_ref[...].astype(BF16)

    ospec = pl.BlockSpec((tm, tn), lambda i, j, k: (i, j + joff))
    in_specs = [pl.BlockSpec((tk, tm), lambda i, j, k: (k, i)), pl.BlockSpec((tk, tn), lambda i, j, k: (k, j))]
    args, aliases = [a, b], None
    if into is not None:
        in_specs += [_ANY, _ANY]
        args += list(into)
        aliases = {2: 0, 3: 1}
    return _call(body, name=name, grid=(ni, nj, nk), in_specs=in_specs, out_specs=[ospec, ospec],
                 out_shape=[jax.ShapeDtypeStruct((M, ncols), F32), jax.ShapeDtypeStruct((M, ncols), BF16)], args=args,
                 scratch=[pltpu.VMEM((tm, tn), F32)] if nk > 1 else [], sem=("parallel", "parallel", "arbitrary"), carry=carry,
                 aliases=aliases)


def _rows(tm, c, cb=0):
    return pl.BlockSpec((tm, c), lambda i, cb=cb: (i, cb))


def _whole(shape):
    nd = len(shape)
    return pl.BlockSpec(shape, lambda i, nd=nd: (0,) * nd)


def _cast_bf16(name, x, tm, carry=None):
    t, d = x.shape

    def body(x_ref, o_ref):
        o_ref[...] = x_ref[...].astype(BF16)

    return _call(body, name=name, grid=(t // tm,), in_specs=[_rows(tm, d)], out_specs=[_rows(tm, d)],
                 out_shape=[jax.ShapeDtypeStruct((t, d), BF16)], args=[x], sem=("parallel",), carry=carry)


def _residual(x_ref, prev_refs):
    if not prev_refs:
        return x_ref[...]
    return x_ref[...] * prev_refs[0][...] + prev_refs[1][...]


def _ln_fwd(name, xres, f, g, b, cf, tm, carry=None, prev=()):
    t, d = xres.shape
    n_prev = len(prev)

    def body(x_ref, f_ref, g_ref, b_ref, *rest):
        yb_ref, xh_ref, rs_ref = rest[n_prev:]
        r = ALPHA * _residual(x_ref, rest[:n_prev]) + cf * f_ref[...]
        xhat, rstd = _row_stats(r)
        yb_ref[...] = (xhat * g_ref[...] + b_ref[...]).astype(BF16)
        xh_ref[...] = xhat
        rs_ref[...] = rstd

    return _call(
        body, name=name, grid=(t // tm,),
        in_specs=[_rows(tm, d), _rows(tm, d), _whole((1, d)), _whole((1, d))] + [_whole((1, d))] * n_prev,
        out_specs=[_rows(tm, d), _rows(tm, d), _rows(tm, 1)],
        out_shape=[jax.ShapeDtypeStruct((t, d), BF16), jax.ShapeDtypeStruct((t, d), F32), jax.ShapeDtypeStruct((t, 1), F32)],
        args=[xres, f, g, b, *prev], sem=("parallel",), carry=carry)


def _ln_bwd(name, dy, xhat, rstd, g, scale, tm, carry=None):
    t, d = dy.shape

    def body(dy_ref, xh_ref, rs_ref, g_ref, dr_ref, drb_ref, dg_ref, db_ref):
        i = pl.program_id(0)
        dy_v, xh = dy_ref[...], xh_ref[...]
        dr = _ln_bwd_rows(dy_v, xh, rs_ref[...], g_ref[...])
        dr_ref[...] = dr
        drb_ref[...] = (scale * dr).astype(BF16)

        @pl.when(i == 0)
        def _():
            dg_ref[...] = jnp.zeros_like(dg_ref)
            db_ref[...] = jnp.zeros_like(db_ref)

        dg_ref[...] += _colsum(dy_v * xh)
        db_ref[...] += _colsum(dy_v)

    return _call(
        body, name=name, grid=(t // tm,), in_specs=[_rows(tm, d), _rows(tm, d), _rows(tm, 1), _whole((1, d))],
        out_specs=[_rows(tm, d), _rows(tm, d), _whole((1, d)), _whole((1, d))],
        out_shape=[jax.ShapeDtypeStruct((t, d), F32), jax.ShapeDtypeStruct((t, d), BF16),
                   jax.ShapeDtypeStruct((1, d), F32), jax.ShapeDtypeStruct((1, d), F32)],
        args=[dy, xhat, rstd, g], sem=("arbitrary",), carry=carry)


def _ln3_loss(name, xres, f, g, b, target, tm, prev):
    t, d = xres.shape

    def body(x_ref, f_ref, g_ref, b_ref, tg_ref, pg_ref, pb_ref, dr_ref, drb_ref, loss_ref, dg_ref, db_ref):
        i = pl.program_id(0)
        r = ALPHA * _residual(x_ref, (pg_ref, pb_ref)) + 0.5 * f_ref[...]
        xhat, rstd = _row_stats(r)
        gv = g_ref[...]
        y = xhat * gv + b_ref[...]
        err = y - tg_ref[...]
        dy = err * (1.0 / d)
        dr = _ln_bwd_rows(dy, xhat, rstd, gv)
        dr_ref[...] = dr
        drb_ref[...] = (0.5 * dr).astype(BF16)
        part = 0.5 * jnp.sum(jnp.mean(err * err, axis=-1, keepdims=True), axis=0, keepdims=True)

        @pl.when(i == 0)
        def _():
            loss_ref[...] = jnp.zeros_like(loss_ref)
            dg_ref[...] = jnp.zeros_like(dg_ref)
            db_ref[...] = jnp.zeros_like(db_ref)

        loss_ref[...] += jnp.broadcast_to(part, loss_ref.shape)
        dg_ref[...] += _colsum(dy * xhat)
        db_ref[...] += _colsum(dy)

    return _call(
        body, name=name, grid=(t // tm,),
        in_specs=[_rows(tm, d), _rows(tm, d), _whole((1, d)), _whole((1, d)), _rows(tm, d), _whole((1, d)), _whole((1, d))],
        out_specs=[_rows(tm, d), _rows(tm, d), _whole((8, 128)), _whole((1, d)), _whole((1, d))],
        out_shape=[jax.ShapeDtypeStruct((t, d), F32), jax.ShapeDtypeStruct((t, d), BF16),
                   jax.ShapeDtypeStruct((8, 128), F32), jax.ShapeDtypeStruct((1, d), F32),
                   jax.ShapeDtypeStruct((1, d), F32)],
        args=[xres, f, g, b, target, *prev], sem=("arbitrary",))


def _colsum_rows(name, x, tm):
    t, d = x.shape

    def body(x_ref, o_ref):
        @pl.when(pl.program_id(0) == 0)
        def _():
            o_ref[...] = jnp.zeros_like(o_ref)

        o_ref[...] += _colsum(x_ref[...].astype(F32))

    return _call(body, name=name, grid=(t // tm,), in_specs=[_rows(tm, d)], out_specs=[_whole((1, d))],
                 out_shape=[jax.ShapeDtypeStruct((1, d), F32)], args=[x], sem=("arbitrary",))[0]


def _sgu_mask():
    sh = SGU_CHUNK.bit_length() - 1
    r = lax.shift_right_logical(lax.broadcasted_iota(jnp.int32, (SGU_BLOCK, SGU_BLOCK), 0), sh)
    c = lax.shift_right_logical(lax.broadcasted_iota(jnp.int32, (SGU_BLOCK, SGU_BLOCK), 1), sh)
    return c <= r


def _sgu_fwd(name, p, lng, lnb, wm, bst, tm):
    t = p.shape[0]
    n_grp, w = wm.shape[0], lng.shape[1]
    hd = w // n_grp
    nblk = tm // SGU_BLOCK

    def body(uv_ref, h_ref, g_ref, b_ref, wm_ref, bs_ref, sa_ref, z_ref, vn_s):
        xhat, _ = _row_stats(_gelu(uv_ref[:, w:]))
        vn_s[...] = (xhat * g_ref[...] + b_ref[...]).astype(BF16)
        mask = _sgu_mask()
        for h in range(n_grp):
            wh = jnp.where(mask, wm_ref[h], 0.0).astype(BF16)
            bcol = bs_ref[:, h:h + 1]
            cs = slice(h * hd, (h + 1) * hd)
            for n in range(nblk):
                rs = slice(n * SGU_BLOCK, (n + 1) * SGU_BLOCK)
                s = jnp.dot(wh, vn_s[rs, cs], preferred_element_type=F32) + bcol
                sa_ref[rs, cs] = (_gelu(uv_ref[rs, cs]) * s).astype(BF16)
        z_ref[...] = h_ref[:, :w] * jax.nn.sigmoid(h_ref[:, w:])

    return _call(
        body, name=name, grid=(t // tm,),
        in_specs=[_rows(tm, 2 * w, 0), _rows(tm, 2 * w, 1), _whole((1, w)), _whole((1, w)), _whole(wm.shape),
                  _whole(bst.shape)],
        out_specs=[_rows(tm, w), _rows(tm, w)],
        out_shape=[jax.ShapeDtypeStruct((t, w), BF16), jax.ShapeDtypeStruct((t, w), F32)],
        args=[p, p, lng, lnb, wm, bst], scratch=[pltpu.VMEM((tm, w), BF16)], sem=("parallel",))


def _sgu_bwd(name, p, dsa, lng, lnb, wm, bst, tm):
    t = p.shape[0]
    n_grp, w = wm.shape[0], lng.shape[1]
    hd = w // n_grp
    nblk = tm // SGU_BLOCK

    def body(uv_ref, dsa_ref, g_ref, b_ref, wm_ref, bs_ref, dp_ref, dwm_ref, dbs_ref, dg_ref, db_ref, dbin_ref,
             vn_s, ug_s, dvn_s, dug_s):
        i = pl.program_id(0)

        @pl.when(i == 0)
        def _():
            dwm_ref[...] = jnp.zeros_like(dwm_ref)
            dbs_ref[...] = jnp.zeros_like(dbs_ref)
            dg_ref[...] = jnp.zeros_like(dg_ref)
            db_ref[...] = jnp.zeros_like(db_ref)
            dbin_ref[...] = jnp.zeros_like(dbin_ref)

        ug, dgelu_u = _gelu_and_grad(uv_ref[:, :w])
        ug_s[...] = ug
        vg, dgelu_v = _gelu_and_grad(uv_ref[:, w:])
        xhat, rstd = _row_stats(vg)
        gv = g_ref[...]
        vn_s[...] = (xhat * gv + b_ref[...]).astype(BF16)
        mask = _sgu_mask()
        for h in range(n_grp):
            wh = jnp.where(mask, wm_ref[h], 0.0).astype(BF16)
            bcol = bs_ref[:, h:h + 1]
            cs = slice(h * hd, (h + 1) * hd)
            dw_h = jnp.zeros((SGU_BLOCK, SGU_BLOCK), F32)
            dbs_h = jnp.zeros((SGU_BLOCK, 1), F32)
            for n in range(nblk):
                rs = slice(n * SGU_BLOCK, (n + 1) * SGU_BLOCK)
                vblk = vn_s[rs, cs]
                s = jnp.dot(wh, vblk, preferred_element_type=F32) + bcol
                dsa_blk = dsa_ref[rs, cs]
                dug_s[rs, cs] = dsa_blk * s
                ds = dsa_blk * ug_s[rs, cs]
                dsb = ds.astype(BF16)
                dvn_s[rs, cs] = lax.dot_general(wh, dsb, (((0,), (0,)), ((), ())), preferred_element_type=F32)
                dw_h = dw_h + lax.dot_general(dsb, vblk, (((1,), (1,)), ((), ())), preferred_element_type=F32)
                dbs_h = dbs_h + jnp.sum(ds, axis=1, keepdims=True)
            dwm_ref[h] += jnp.where(mask, dw_h, 0.0)
            dbs_ref[:, h:h + 1] += dbs_h
        dvn = dvn_s[...]
        dg_ref[...] += _colsum(dvn * xhat)
        db_ref[...] += _colsum(dvn)
        dvg = _ln_bwd_rows(dvn, xhat, rstd, gv)
        du = dug_s[...] * dgelu_u
        dv = dvg * dgelu_v
        dp_ref[:, :w] = du.astype(BF16)
        dp_ref[:, w:] = dv.astype(BF16)
        dbin_ref[:, :w] += _colsum(du)
        dbin_ref[:, w:] += _colsum(dv)

    return _call(
        body, name=name, grid=(t // tm,),
        in_specs=[_rows(tm, 2 * w, 0), _rows(tm, w), _whole((1, w)), _whole((1, w)), _whole(wm.shape), _whole(bst.shape)],
        out_specs=[_rows(tm, 2 * w), _whole(wm.shape), _whole(bst.shape), _whole((1, w)), _whole((1, w)), _whole((1, 2 * w))],
        out_shape=[jax.ShapeDtypeStruct((t, 2 * w), BF16), jax.ShapeDtypeStruct(wm.shape, F32),
                   jax.ShapeDtypeStruct(bst.shape, F32), jax.ShapeDtypeStruct((1, w), F32), jax.ShapeDtypeStruct((1, w), F32),
                   jax.ShapeDtypeStruct((1, 2 * w), F32)],
        args=[p, dsa, lng, lnb, wm, bst],
        scratch=[pltpu.VMEM((tm, w), BF16), pltpu.VMEM((tm, w), F32), pltpu.VMEM((tm, w), F32), pltpu.VMEM((tm, w), F32)],
        sem=("arbitrary",))


def _halo_prev(tm, c):
    return pl.BlockSpec((HALO, c), lambda i: (jnp.maximum(i * (tm // HALO) - 1, 0), 0))


def _halo_next(tm, c, t):
    last = t // HALO - 1
    return pl.BlockSpec((HALO, c), lambda i: (jnp.minimum((i + 1) * (tm // HALO), last), 0))


def _shifted_copies(sh, n):
    for r in range(1, SUBLANES):
        sh[r, :n - SUBLANES, :] = sh[0, r:r + n - SUBLANES, :]


def _row_broadcasts(wb, w_ref, kw):
    for k in range(kw):
        wb[k] = jnp.broadcast_to(w_ref[k:k + 1, :], wb.shape[1:])


def _tap(sh, r0, o, rows, cols):
    return sh[o % SUBLANES, pl.ds(pl.multiple_of(r0 + (o - o % SUBLANES), SUBLANES), rows), cols]


def _conv_fwd(name, z, wdw, kw, bdw, lng, lnb, tm, carry=None):
    t, c = z.shape
    lead = HALO - (kw - 1)
    n = tm + HALO

    def body(zp_ref, z_ref, w_ref, bdw_ref, g_ref, b_ref, zc_ref, sb_ref, sh, wb):
        i = pl.program_id(0)
        sh[0, :HALO, :] = jnp.where(i > 0, zp_ref[...], 0.0)
        sh[0, HALO:, :] = z_ref[...]
        _shifted_copies(sh, n)
        _row_broadcasts(wb, w_ref, kw)
        bias = jnp.broadcast_to(bdw_ref[...], (SUBLANES, c))
        groups = CONV_ROWS // SUBLANES

        def chunk(ci, _):
            r0 = pl.multiple_of(ci * CONV_ROWS, CONV_ROWS)
            accs = [bias] * groups
            for k in range(kw):
                wk = wb[k]
                tp = _tap(sh, r0, lead + k, CONV_ROWS, slice(None))
                accs = [accs[g] + wk * tp[g * SUBLANES:(g + 1) * SUBLANES] for g in range(groups)]
            zc_ref[pl.ds(r0, CONV_ROWS), :] = jnp.concatenate(accs, axis=0)
            return 0

        lax.fori_loop(0, tm // CONV_ROWS, chunk, 0)
        xhat, _ = _row_stats(zc_ref[...])
        zn = xhat * g_ref[...] + b_ref[...]
        sb_ref[...] = (zn * jax.nn.sigmoid(zn)).astype(BF16)

    return _call(
        body, name=name, grid=(t // tm,),
        in_specs=[_halo_prev(tm, c), _rows(tm, c), _whole(wdw.shape), _whole((1, c)), _whole((1, c)), _whole((1, c))],
        out_specs=[_rows(tm, c), _rows(tm, c)],
        out_shape=[jax.ShapeDtypeStruct((t, c), F32), jax.ShapeDtypeStruct((t, c), BF16)],
        args=[z, z, wdw, bdw, lng, lnb], scratch=[pltpu.VMEM((SUBLANES, n, c), F32), pltpu.VMEM((HALO, SUBLANES, c), F32)],
        sem=("parallel",), carry=carry)


def _conv_ln_bwd(name, dsb, zc, lng, lnb, tm):
    t, c = zc.shape

    def body(dsb_ref, zc_ref, g_ref, b_ref, dzc_ref, dg_ref, db_ref, dbdw_ref):
        i = pl.program_id(0)
        xhat, rstd = _row_stats(zc_ref[...])
        gv = g_ref[...]
        zn = xhat * gv + b_ref[...]
        dzn = dsb_ref[...] * _silu_grad(zn)
        dzc = _ln_bwd_rows(dzn, xhat, rstd, gv)
        dzc_ref[...] = dzc

        @pl.when(i == 0)
        def _():
            dg_ref[...] = jnp.zeros_like(dg_ref)
            db_ref[...] = jnp.zeros_like(db_ref)
            dbdw_ref[...] = jnp.zeros_like(dbdw_ref)

        dg_ref[...] += _colsum(dzn * xhat)
        db_ref[...] += _colsum(dzn)
        dbdw_ref[...] += _colsum(dzc)

    return _call(
        body, name=name, grid=(t // tm,), in_specs=[_rows(tm, c), _rows(tm, c), _whole((1, c)), _whole((1, c))],
        out_specs=[_rows(tm, c), _whole((1, c)), _whole((1, c)), _whole((1, c))],
        out_shape=[jax.ShapeDtypeStruct((t, c), F32)] + [jax.ShapeDtypeStruct((1, c), F32)] * 3,
        args=[dsb, zc, lng, lnb], sem=("arbitrary",))


def _conv_bwd(name, dzc, z, p, wdw, kw, tm):
    t, c = z.shape
    n_i = t // tm
    n = tm + HALO

    def body(dzc_ref, dzn_ref, z_ref, h_ref, w_ref, dp_ref, dw_ref, dbin_ref, sh, dz_s, wb):
        i = pl.program_id(0)

        @pl.when(i == 0)
        def _():
            dw_ref[...] = jnp.zeros_like(dw_ref)

        sh[0, :tm, :] = dzc_ref[...]
        sh[0, tm:, :] = jnp.where(i < n_i - 1, dzn_ref[...], 0.0)
        _shifted_copies(sh, n)
        _row_broadcasts(wb, w_ref, kw)
        nv = 8
        rows = nv * SUBLANES
        ways = 2
        tap_groups = [range(0, kw // 2), range(kw // 2, kw)]
        for lc in range(c // LANES):
            cols = slice(lc * LANES, (lc + 1) * LANES)
            for gi, taps in enumerate(tap_groups):

                def rowv(rv, accs, cols=cols, taps=taps, first=gi == 0):
                    r0 = pl.multiple_of(rv * rows, rows)
                    zv = z_ref[pl.ds(r0, rows), cols]
                    zs = [zv[v * SUBLANES:(v + 1) * SUBLANES] for v in range(nv)]
                    dz = [[None] * ways for _ in range(nv)]
                    new = []
                    for ai, k in enumerate(taps):
                        s = _tap(sh, r0, kw - 1 - k, rows, cols)
                        wk = wb[k, :, cols]
                        j = ai % ways
                        acc = accs[ai]
                        for v in range(nv):
                            sv = s[v * SUBLANES:(v + 1) * SUBLANES]
                            dz[v][j] = wk * sv if dz[v][j] is None else dz[v][j] + wk * sv
                            acc = acc + zs[v] * sv
                        new.append(acc)
                    part = jnp.concatenate([dz[v][0] + dz[v][1] for v in range(nv)], axis=0)
                    dz_s[pl.ds(r0, rows), cols] = part if first else dz_s[pl.ds(r0, rows), cols] + part
                    return tuple(new)

                accs = lax.fori_loop(0, tm // rows, rowv, tuple(jnp.zeros((SUBLANES, LANES), F32) for _ in taps))
                for ai, k in enumerate(taps):
                    dw_ref[k:k + 1, cols] += _colsum(accs[ai])
        dz = dz_s[...]
        a, g = h_ref[:, :c], h_ref[:, c:]
        sg = jax.nn.sigmoid(g)
        da = dz * sg
        dg = dz * a * (sg * (1.0 - sg))
        dp_ref[:, :c] = da.astype(BF16)
        dp_ref[:, c:] = dg.astype(BF16)

        @pl.when(i == 0)
        def _():
            dbin_ref[...] = jnp.zeros_like(dbin_ref)

        dbin_ref[:, :c] += _colsum(da)
        dbin_ref[:, c:] += _colsum(dg)

    return _call(
        body, name=name, grid=(n_i,),
        in_specs=[_rows(tm, c), _halo_next(tm, c, t), _rows(tm, c), _rows(tm, 2 * c, 1), _whole(wdw.shape)],
        out_specs=[_rows(tm, 2 * c), _whole((HALO, c)), _whole((1, 2 * c))],
        out_shape=[jax.ShapeDtypeStruct((t, 2 * c), BF16), jax.ShapeDtypeStruct((HALO, c), F32),
                   jax.ShapeDtypeStruct((1, 2 * c), F32)],
        args=[dzc, dzc, z, p, wdw],
        scratch=[pltpu.VMEM((SUBLANES, n, c), F32), pltpu.VMEM((tm, c), F32), pltpu.VMEM((HALO, SUBLANES, c), F32)],
        sem=("arbitrary",))


def _adamw(name, w, g, m, v, tr, carry=None):
    r, c = w.shape
    c1 = 1.0 - ADAM_B1 ** ADAM_STEP
    c2 = 1.0 - ADAM_B2 ** ADAM_STEP

    def body(w_ref, g_ref, m_ref, v_ref, go_ref, d_ref, mo_ref, vo_ref):
        gv = g_ref[...]
        mn = ADAM_B1 * m_ref[...] + (1.0 - ADAM_B1) * gv
        vn = ADAM_B2 * v_ref[...] + (1.0 - ADAM_B2) * (gv * gv)
        go_ref[...] = gv
        d_ref[...] = -ADAM_LR * ((mn / c1) / (jnp.sqrt(vn / c2) + ADAM_EPS) + ADAM_WD * w_ref[...])
        mo_ref[...] = mn
        vo_ref[...] = vn

    spec = _rows(tr, c)
    return _call(body, name=name, grid=(r // tr,), in_specs=[spec] * 4, out_specs=[spec] * 4,
                 out_shape=[jax.ShapeDtypeStruct((r, c), F32)] * 4, args=[w, g, m, v], sem=("parallel",), carry=carry)


_RELATION_XOR = (2, 1, 3)


def _pair_sum(name, mine, recv, qc, *, part_shape, tr, in_map):
    pr, pc = part_shape

    def body(qc_ref, a_ref, b_ref, oh_ref):
        oh_ref[...] = (a_ref[...] + b_ref[...].astype(F32)).astype(BF16)

    def imap(k, i, qc_ref):
        part = qc_ref[0]
        for kk, m in enumerate(_RELATION_XOR):
            part = jnp.where(k == kk, jnp.bitwise_xor(qc_ref[0], m), part)
        return in_map(i, part, qc_ref[1])

    ispec = pl.BlockSpec((tr, pc), imap)
    ospec = pl.BlockSpec((None, tr, pc), lambda k, i, qc_ref: (k, i, 0))
    return pl.pallas_call(
        body, name=name,
        grid_spec=pltpu.PrefetchScalarGridSpec(num_scalar_prefetch=1, grid=(3, pr // tr), in_specs=[ispec, ispec],
                                               out_specs=ospec),
        out_shape=jax.ShapeDtypeStruct((3, pr, pc), BF16), compiler_params=_cparams(("parallel", "parallel")))(qc, mine, recv)


def _final_sum(name, mine, recv1, recv2, qc, *, part_shape, out_shape, tr, in_map, out_map):
    pr, pc = part_shape

    def body(qc_ref, a_ref, b_ref, r_ref, o_ref):
        own = a_ref[...] + b_ref[...].astype(F32)
        o_ref[...] = ((own + r_ref[0].astype(F32)) + r_ref[1].astype(F32)) + r_ref[2].astype(F32)

    ispec = pl.BlockSpec((tr, pc), lambda i, qc_ref: in_map(i, qc_ref[0], qc_ref[1]))
    return pl.pallas_call(
        body, name=name,
        grid_spec=pltpu.PrefetchScalarGridSpec(
            num_scalar_prefetch=1, grid=(pr // tr,),
            in_specs=[ispec, ispec, pl.BlockSpec((3, tr, pc), lambda i, qc_ref: (0, i, 0))],
            out_specs=pl.BlockSpec((tr, pc), lambda i, qc_ref: out_map(i, qc_ref[0], qc_ref[1]))),
        out_shape=jax.ShapeDtypeStruct(out_shape, F32), compiler_params=_cparams(("parallel",)))(qc, mine, recv1, recv2)


def _sum8(name, slots):
    def body(s_ref, o_ref):
        acc = s_ref[0]
        for d in range(1, 8):
            acc = acc + s_ref[d]
        o_ref[...] = acc

    return pl.pallas_call(body, name=name, out_shape=jax.ShapeDtypeStruct(slots.shape[1:], F32),
                          in_specs=[pl.BlockSpec(memory_space=pltpu.VMEM)], out_specs=pl.BlockSpec(memory_space=pltpu.VMEM),
                          compiler_params=pltpu.CompilerParams(vmem_limit_bytes=VMEM_LIMIT))(slots)


def _chips(x, y):
    return [(1 - x, y), (x, 1 - y), (1 - x, 1 - y)]


class _Big:
    def __init__(self, name, w, m, v, ax):
        self.name, self.w, self.m, self.v, self.ax = name, w, m, v, ax
        sr, sc = w.shape
        self.R, self.C = (sr * N_CHIPS, sc) if ax == 0 else (sr, sc * N_CHIPS)
        self.sr, self.sc = sr, sc
        self.hr = sr // 2

    def slot(self, ref, q, h=None):
        if self.ax == 1:
            cols = pl.ds(pl.multiple_of(q * self.sc, 128), self.sc)
            return ref.at[:, cols] if h is None else ref.at[pl.ds(pl.multiple_of(h * self.hr, 16), self.hr), cols]
        if h is None:
            return ref.at[pl.ds(pl.multiple_of(q * self.sr, 16), self.sr), :]
        return ref.at[pl.ds(pl.multiple_of(q * self.sr + h * self.hr, 16), self.hr), :]

    def half(self, ref, h):
        return ref.at[pl.ds(pl.multiple_of(h * self.hr, 16), self.hr), :]

    @property
    def part_shape(self):
        return (self.hr, self.sc)

    def cast_into_full(self, qc):
        tr = _tile_rows(self.sr, self.sc)
        nb = self.sr // tr

        def body(qc_ref, x_ref, o_ref):
            o_ref[...] = x_ref[...].astype(BF16)

        if self.ax == 1:
            ospec = pl.BlockSpec((tr, self.sc), lambda i, qc_ref: (i, qc_ref[0]))
        else:
            ospec = pl.BlockSpec((tr, self.sc), lambda i, qc_ref: (qc_ref[0] * nb + i, 0))
        return pl.pallas_call(
            body, name=f"cast_{self.name}",
            grid_spec=pltpu.PrefetchScalarGridSpec(num_scalar_prefetch=1, grid=(nb,),
                                                   in_specs=[pl.BlockSpec((tr, self.sc), lambda i, qc_ref: (i, 0))],
                                                   out_specs=ospec),
            out_shape=jax.ShapeDtypeStruct((self.R, self.C), BF16), compiler_params=_cparams(("parallel",)))(qc, self.w)

    def gather_ici(self, full, piece=(0, 1)):
        k, n = piece
        pr = self.hr // n
        assert pr * n == self.hr and pr % 16 == 0

        def plan(ro, rw, new, x, y, c):
            q = 2 * x + y
            r0 = c * self.hr + k * pr
            if self.ax == 1:
                mine = rw[0].at[pl.ds(pl.multiple_of(r0, 16), pr), pl.ds(pl.multiple_of(q * self.sc, 128), self.sc)]
            else:
                mine = rw[0].at[pl.ds(pl.multiple_of(q * self.sr + r0, 16), pr), :]
            return [(mine, mine, (cx, cy, c)) for cx, cy in _chips(x, y)], []

        return _Carry([], [full], [], plan, 3, 0)

    def gather_d2d(self, full):
        def plan(ro, rw, new, x, y, c):
            remote = []
            for cx, cy in _chips(x, y):
                piece = self.slot(rw[0], 2 * cx + cy, c)
                remote.append((piece, piece, (x, y, 1 - c)))
            return remote, []

        return _Carry([], [full], [], plan, 3)

    def rs_pair(self, g16):
        def plan(ro, rw, new, x, y, c):
            sib = (x, y, 1 - c)
            if self.ax == 1:
                rows = pl.ds(pl.multiple_of((1 - c) * self.hr, 16), self.hr)
                return [(ro[0].at[rows, :], new[0].at[rows, :], sib)], []
            return [(self.slot(ro[0], q, 1 - c), self.slot(new[0], q, 1 - c), sib) for q in range(N_CHIPS)], []

        return _Carry([g16], [], [jax.ShapeDtypeStruct((self.R, self.C), BF16)], plan, 1 if self.ax == 1 else N_CHIPS)

    def _piece_map(self, nb):
        if self.ax == 1:
            return lambda i, q, c: (c * nb + i, q)
        return lambda i, q, c: ((q * 2 + c) * nb + i, 0)

    def rs_pairsum(self, tag, g32, recv1, qc):
        tr = _tile_rows(self.hr, self.sc)
        return _pair_sum(f"rs_pairsum_{tag}", g32, recv1, qc, part_shape=self.part_shape, tr=tr,
                         in_map=self._piece_map(self.hr // tr))

    def rs_ici(self, cs16):
        def plan(ro, rw, new, x, y, c):
            return [(ro[0].at[k], new[0].at[k], (cx, cy, c)) for k, (cx, cy) in enumerate(_chips(x, y))], []

        return _Carry([cs16], [], [jax.ShapeDtypeStruct((3,) + self.part_shape, BF16)], plan, 3)

    def rs_final(self, tag, g32, recv1, recv2, qc):
        tr = _tile_rows(self.hr, self.sc)
        nb = self.hr // tr
        in_map = self._piece_map(nb)
        out_map = lambda i, q, c: (c * nb + i, 0)
        return _final_sum(f"rs_final_{tag}", g32, recv1, recv2, qc, part_shape=self.part_shape, out_shape=(self.sr, self.sc),
                          tr=tr, in_map=in_map, out_map=out_map)

    def rs_share(self, ghalf):
        def plan(ro, rw, new, x, y, c):
            piece = self.half(rw[0], c)
            return [(piece, piece, (x, y, 1 - c))], []

        return _Carry([], [ghalf], [], plan, 1)


def _small_allgather(packed):
    nr = packed.shape[0]

    def plan(ro, rw, new, x, y, c):
        me = 4 * x + 2 * y + c
        remote = []
        for fx in (0, 1):
            for fy in (0, 1):
                for fc in (0, 1):
                    if fx or fy or fc:
                        dev = (1 - x if fx else x, 1 - y if fy else y, 1 - c if fc else c)
                        remote.append((ro[0], new[0].at[me], dev))
        return remote, [(ro[0], new[0].at[me])]

    return _Carry([packed], [], [jax.ShapeDtypeStruct((8, nr, 128), F32)], plan, 7, 1)


def _conv_w_allgather(padded, cs):
    def plan(ro, rw, new, x, y, c):
        cols = pl.ds(pl.multiple_of((2 * x + y) * cs, 128), cs)
        remote = [(ro[0], new[0].at[:, cols], (cx, cy, c)) for cx, cy in _chips(x, y)]
        return remote, [(ro[0], new[0].at[:, cols])]

    return _Carry([padded], [], [jax.ShapeDtypeStruct((HALO, cs * N_CHIPS), F32)], plan, 3, 1)


def _pick(n, want):
    if n <= want:
        return n
    for t in range(want, 15, -16):
        if t % 16 == 0 and n % t == 0:
            return t
    raise ValueError(f"no tile for {n} (want {want})")


def _tile_rows(nrows, ncols, budget=2 * 1024 * 1024):
    return _pick(nrows, max(16, (budget // (4 * ncols)) // 16 * 16))


def _pick128(n, want):
    if n <= want:
        return n
    for t in range(want, 127, -128):
        if n % t == 0:
            return t
    raise ValueError(f"no lane tile for {n} (want {want})")


def _pack_rows(parts):
    out, spans, r0 = [], [], 0
    for p in parts:
        flat = p.reshape(-1).astype(F32)
        n = flat.shape[0]
        rows = -(-n // 1024) * 8
        flat = jnp.pad(flat, (0, rows * 128 - n))
        out.append(flat.reshape(rows, 128))
        spans.append((r0, rows, n))
        r0 += rows
    return jnp.concatenate(out, axis=0), spans


def _unpack_rows(packed, spans, shapes):
    res = []
    for (r0, rows, n), shp in zip(spans, shapes, strict=True):
        res.append(packed[r0:r0 + rows].reshape(-1)[:n].reshape(shp))
    return res


def _ident(accs, ex):
    return [accs[0]]


def kernel(x, ffn1_w_gu, ffn1_w_down, ln1_g, ln1_b, w_in, b_in, sgu_ln_g, sgu_ln_b, sgu_w_s, sgu_b_s, w_a_proj, conv_w_dw, conv_b_dw, conv_ln_g, conv_ln_b, w_b_proj, w_out, ln2_g, ln2_b, ffn2_w_gu, ffn2_w_down, ln3_g, ln3_b, loss_target, m_ffn1_w_gu, m_ffn1_w_down, m_ln1_g, m_ln1_b, m_w_in, m_b_in, m_sgu_ln_g, m_sgu_ln_b, m_sgu_w_s, m_sgu_b_s, m_w_a_proj, m_conv_w_dw, m_conv_b_dw, m_conv_ln_g, m_conv_ln_b, m_w_b_proj, m_w_out, m_ln2_g, m_ln2_b, m_ffn2_w_gu, m_ffn2_w_down, m_ln3_g, m_ln3_b, v_ffn1_w_gu, v_ffn1_w_down, v_ln1_g, v_ln1_b, v_w_in, v_b_in, v_sgu_ln_g, v_sgu_ln_b, v_sgu_w_s, v_sgu_b_s, v_w_a_proj, v_conv_w_dw, v_conv_b_dw, v_conv_ln_g, v_conv_ln_b, v_w_b_proj, v_w_out, v_ln2_g, v_ln2_b, v_ffn2_w_gu, v_ffn2_w_down, v_ln3_g, v_ln3_b):
    args = dict(locals())
    assert x.shape[0] == 1 and ffn1_w_gu.shape[0] == 1
    T, D = x.shape[1], x.shape[2]
    F = ffn1_w_down.shape[1] * N_CHIPS
    W = sgu_ln_g.shape[1]
    KW = conv_w_dw.shape[1]
    assert KW - 1 <= HALO and T % SGU_BLOCK == 0

    mx, my, mc = lax.axis_index("x"), lax.axis_index("y"), lax.axis_index("c")
    q = 2 * mx + my
    qc = jnp.stack([q, mc]).astype(jnp.int32)

    big_names = [("ffn1_w_gu", 1), ("ffn1_w_down", 0), ("w_in", 1), ("w_a_proj", 1), ("w_b_proj", 1), ("w_out", 0),
                 ("ffn2_w_gu", 1), ("ffn2_w_down", 0)]
    B = {n: _Big(n, args[n][0], args["m_" + n][0], args["v_" + n][0], ax) for n, ax in big_names}
    own = {n: b.cast_into_full(qc) for n, b in B.items()}

    x2d = x[0]
    tgt = loss_target[0]
    tm_r = _pick(T, 256)
    tm_ln = _pick(T, 512)
    tm = _pick(T, 1024)
    tn = _pick128(D, 1024)
    nj = D // tn
    tng = _pick128(D, 512)
    njg = D // tng
    tnf = _pick128(F, 512)
    nf = F // tnf
    tnw = _pick128(W, 1024)
    tnd = _pick128(D, 512)
    tmw = _pick128(W, 512)
    SUB = 256

    def ffn_up(tag, xb_, wgu, carry=None):
        def epi(accs, ex):
            g, u = accs
            s = jax.nn.sigmoid(g)
            sg = g * s
            return [u * (s * (1.0 + g * (1.0 - s))), sg, sg * u]

        return _mm(f"{tag}_up", [(xb_, wgu, 0, 0, 0), (xb_, wgu, 0, 0, nf)], M=T, N=F, tm=tm, tn=tnf, tk=D, nk=1, epilogue=epi,
                   outs=[(F, BF16, 0)] * 3, sum_pairs=False, carry=carry, sub=SUB)

    def ffn_down(tag, act, wd, carry=None):
        return _mm(f"{tag}_down", [(act, wd, 0, 0, 0)], M=T, N=D, tm=tm, tn=_pick128(D, 512), tk=F, nk=1, epilogue=_ident,
                   outs=[(D, F32, 0)], carry=carry)

    def ffn_dact(tag, drh, wd, dgate_f, dup_f, carry=None):
        def epi(accs, ex):
            da = accs[0]
            return [da * ex[0].astype(F32), da * ex[1].astype(F32)]

        return _mm(f"{tag}_dact", [(drh, wd, 0, 0, 0)], tb=True, M=T, N=F, tm=_pick(T, 2048), tn=tnf, tk=D, nk=1, epilogue=epi,
                   outs=[(F, BF16, 0)] * 2, extras=[(dgate_f, "mn", 0), (dup_f, "mn", 0)], carry=carry, sub=SUB)

    def ffn_dwdown(tag, act, drh, carry=None):
        return _wgrad(f"{tag}_dwdown", act, drh, M=F, N=D, T=T, tm=tnf, tn=tnd, tk=T, carry=carry)

    def ffn_dwgate(tag, xb_, dg, carry=None):
        return _wgrad(f"{tag}_dwgate", xb_, dg, M=D, N=F, T=T, tm=tnd, tn=tnf, tk=T, ncols=2 * F, carry=carry)

    def ffn_dwup(tag, xb_, du, into, carry=None):
        return _wgrad(f"{tag}_dwup", xb_, du, M=D, N=F, T=T, tm=tnd, tn=tnf, tk=T, ncols=2 * F, into=into, joff=nf, carry=carry)

    def ffn_dx(tag, which, da, wgu, addends, carry=None):
        def epi(accs, ex):
            tot = accs[0] + ALPHA * ex[0]
            for e in ex[1:]:
                tot = tot + e
            return [tot]

        return _mm(f"{tag}_dx_{which}", [(da, wgu, 0, 1 if which == "up" else 0, 0)], tb=True, M=T, N=D, tm=tm,
                   tn=_pick128(D, 256), tk=F, nk=1, epilogue=epi if addends else _ident, outs=[(D, F32, 0)],
                   extras=[(a, "mn", 0) for a in addends], carry=carry)

    wdw_pad = jnp.pad(conv_w_dw[0], ((0, HALO - KW), (0, 0)))
    c0, un = _merge(B["ffn1_w_gu"].gather_ici(own["ffn1_w_gu"]), _conv_w_allgather(wdw_pad, conv_w_dw.shape[2]))
    (xb,), co = _cast_bf16("cast_x", x2d, tm_r, carry=c0)
    (wgu1,), (wdw_full,) = un(co)
    (wgu1,) = _exchange("gather_d2d_ffn1_w_gu", B["ffn1_w_gu"].gather_d2d(wgu1))

    c, un = _merge(B["ffn1_w_down"].gather_ici(own["ffn1_w_down"]), B["w_in"].gather_ici(own["w_in"]))
    (g1, u1, a1), co = ffn_up("ffn1", xb, wgu1, carry=c)
    (wd1,), (win,) = un(co)
    (wd1,) = _exchange("gather_d2d_ffn1_w_down", B["ffn1_w_down"].gather_d2d(wd1))
    b_gu2, b_d2 = B["ffn2_w_gu"], B["ffn2_w_down"]
    c, un = _merge(B["w_a_proj"].gather_ici(own["w_a_proj"]), B["w_b_proj"].gather_ici(own["w_b_proj"]),
                   B["w_out"].gather_ici(own["w_out"]), B["w_in"].gather_d2d(win), b_gu2.gather_ici(own["ffn2_w_gu"], (0, 4)))
    (fo1,), co = ffn_down("ffn1", a1, wd1, carry=c)
    (wa,), (wb,), (wout,), (win,), (wgu2,) = un(co)
    c, un = _merge(B["w_a_proj"].gather_d2d(wa), B["w_b_proj"].gather_d2d(wb), B["w_out"].gather_d2d(wout),
                   b_gu2.gather_ici(wgu2, (1, 4)))
    (x1b, xh1, rs1), co = _ln_fwd("ln1", x2d, fo1, ln1_g, ln1_b, 0.5, tm_ln, carry=c)
    (wa,), (wb,), (wout,), (wgu2,) = un(co)

    c, un = _merge(b_gu2.gather_ici(wgu2, (1, 2)), b_d2.gather_ici(own["ffn2_w_down"], (0, 2)))
    (proj,), co = _mm("in_proj", [(x1b, win, 0, 0, 0)], M=T, N=4 * D, tm=tm, tn=tn, tk=D, nk=1,
                      epilogue=lambda accs, ex: [accs[0] + ex[0]], outs=[(4 * D, F32, 0)], extras=[(b_in, "n", 0)], carry=c)
    (wgu2,), (wd2,) = un(co)
    wm = sgu_w_s[0]
    bst = sgu_b_s[0].T
    sa, z = _sgu_fwd("sgu_fwd", proj, sgu_ln_g, sgu_ln_b, wm, bst, tm_r)
    (zc, sb), (wd2,) = _conv_fwd("conv_fwd", z, wdw_full, KW, conv_b_dw, conv_ln_g, conv_ln_b, tm_r,
                                 carry=b_d2.gather_ici(wd2, (1, 2)))

    def epi_mix(accs, ex):
        ya_, yb_ = accs
        ga, gb = jax.nn.sigmoid(ex[0]), jax.nn.sigmoid(ex[1])
        return [ga * ya_ + gb * yb_, ga, gb, ya_ * (ga * (1.0 - ga)), yb_ * (gb * (1.0 - gb))]

    mixin, gate_a, gate_b, dlog_a, dlog_b = _mm(
        "branch_proj", [(sa, wa, 0, 0, 0), (sb, wb, 0, 0, 0)], M=T, N=D, tm=tm, tn=tng, tk=W, nk=1, epilogue=epi_mix,
        outs=[(D, BF16, 0)] * 5, sum_pairs=False, sub=SUB, extras=[(proj, "mn", 2 * njg), (proj, "mn", 3 * njg)])
    c, un = _merge(b_gu2.gather_d2d(wgu2), b_d2.gather_d2d(wd2))
    (mix,), co = _mm("out_proj", [(mixin, wout, 0, 0, 0)], M=T, N=D, tm=tm, tn=tn, tk=D, nk=1, epilogue=_ident,
                     outs=[(D, F32, 0)], carry=c)
    (wgu2,), (wd2,) = un(co)
    x2b, xh2, rs2 = _ln_fwd("ln2", xh1, mix, ln2_g, ln2_b, 1.0, tm_ln, prev=(ln1_g, ln1_b))
    g2, u2, a2 = ffn_up("ffn2", x2b, wgu2)
    (fo2,) = ffn_down("ffn2", a2, wd2)
    dr3, dr3h, loss_part, dln3_g, dln3_b = _ln3_loss("ln3_loss", xh2, fo2, ln3_g, ln3_b, tgt, tm_r, (ln2_g, ln2_b))

    b_gu2, b_d2 = B["ffn2_w_gu"], B["ffn2_w_down"]
    dg2, du2 = ffn_dact("ffn2", dr3h, wd2, g2, u2)
    dwd2 = ffn_dwdown("ffn2", a2, dr3h)
    dwgu2, (r1_d2,) = ffn_dwgate("ffn2", x2b, dg2, carry=b_d2.rs_pair(dwd2[1]))
    dwgu2 = ffn_dwup("ffn2", x2b, du2, dwgu2)
    cs_d2 = b_d2.rs_pairsum("ffn2_w_down", dwd2[0], r1_d2, qc)
    c, un = _merge(b_gu2.rs_pair(dwgu2[1]), b_d2.rs_ici(cs_d2))
    (dx2_gate,), co = ffn_dx("ffn2", "gate", dg2, wgu2, [], carry=c)
    (r1_gu2,), (r2_d2,) = un(co)
    (dx2,) = ffn_dx("ffn2", "up", du2, wgu2, [dr3, dx2_gate])
    cs_gu2 = b_gu2.rs_pairsum("ffn2_w_gu", dwgu2[0], r1_gu2, qc)
    gh_d2 = b_d2.rs_final("ffn2_w_down", dwd2[0], r1_d2, r2_d2, qc)

    dr2, dr2b, dln2_g, dln2_b = _ln_bwd("ln2_bwd", dx2, xh2, rs2, ln2_g, 1.0, tm_ln)

    def epi_dmix(accs, ex):
        tiles = [accs[0] * e.astype(F32) for e in ex]
        sums = [t.reshape(-1, SUBLANES, t.shape[1]).sum(axis=0) for t in tiles[2:]]
        return tiles + sums

    dya, dyb, dla, dlb, sla, slb = _mm(
        "out_proj_bwd", [(dr2b, wout, 0, 0, 0)], tb=True, M=T, N=D, tm=tm, tn=tng, tk=D, nk=1, epilogue=epi_dmix,
        outs=[(D, BF16, 0)] * 4 + [(D, F32, 0, SUBLANES)] * 2, sub=SUB,
        extras=[(gate_a, "mn", 0), (gate_b, "mn", 0), (dlog_a, "mn", 0), (dlog_b, "mn", 0)])
    dwout = _wgrad("dw_out", mixin, dr2b, M=D, N=D, T=T, tm=tnd, tn=tnd, tk=T)
    (dsa,) = _mm("a_proj_bwd", [(dya, wa, 0, 0, 0)], tb=True, M=T, N=W, tm=tm, tn=tnw, tk=D, nk=1, epilogue=_ident,
                 outs=[(W, F32, 0)])
    (dsb,) = _mm("b_proj_bwd", [(dyb, wb, 0, 0, 0)], tb=True, M=T, N=W, tm=tm, tn=tnw, tk=D, nk=1, epilogue=_ident,
                 outs=[(W, F32, 0)])
    dwa = _wgrad("dw_a_proj", sa, dya, M=W, N=D, T=T, tm=tmw, tn=tnd, tk=T)
    dwb = _wgrad("dw_b_proj", sb, dyb, M=W, N=D, T=T, tm=tmw, tn=tnd, tk=T)

    dpa, dwm, dbst, dsgu_g, dsgu_b, dbin_a = _sgu_bwd("sgu_bwd", proj, dsa, sgu_ln_g, sgu_ln_b, wm, bst, tm_r)
    dzc, dcln_g, dcln_b, dbdw = _conv_ln_bwd("conv_ln_bwd", dsb, zc, conv_ln_g, conv_ln_b, tm_r)
    dpb, dwdw, dbin_b = _conv_bwd("conv_bwd", dzc, z, proj, wdw_full, KW, tm_r)

    dps = [dpa, dpb, dla, dlb]
    db_in = jnp.concatenate([dbin_a, dbin_b, _colsum_rows("db_in_gate_a", sla, sla.shape[0]),
                             _colsum_rows("db_in_gate_b", slb, slb.shape[0])], axis=1)
    c, un = _merge(b_gu2.rs_ici(cs_gu2), b_d2.rs_share(gh_d2))
    (dx1,), co = _mm("in_proj_bwd", [(dp, win, 0, k, 0) for k, dp in enumerate(dps)], tb=True, M=T, N=D, tm=tm,
                     tn=_pick128(D, 256), tk=D, nk=1, epilogue=lambda accs, ex: [accs[0] + ALPHA * ex[0]], outs=[(D, F32, 0)],
                     extras=[(dr2, "mn", 0)], carry=c)
    (r2_gu2,), (g_d2,) = un(co)
    gh_gu2 = b_gu2.rs_final("ffn2_w_gu", dwgu2[0], r1_gu2, r2_gu2, qc)
    dwin = _wgrad("dw_in_0", x1b, dps[0], M=D, N=D, T=T, tm=tnd, tn=tnd, tk=T, ncols=4 * D)
    for k in range(1, 4):
        dwin = _wgrad(f"dw_in_{k}", x1b, dps[k], M=D, N=D, T=T, tm=tnd, tn=tnd, tk=T, ncols=4 * D, into=dwin,
                      joff=k * (D // tnd))

    mix_names = ["w_in", "w_a_proj", "w_b_proj", "w_out"]
    mix_grads = dict(zip(mix_names, [dwin, dwa, dwb, dwout], strict=True))
    c, un = _merge(*[B[n].rs_pair(mix_grads[n][1]) for n in mix_names])
    (dr1, dr1h, dln1_g, dln1_b), co = _ln_bwd("ln1_bwd", dx1, xh1, rs1, ln1_g, 0.5, tm_ln, carry=c)
    r1_mix = {n: r1 for n, (r1,) in zip(mix_names, un(co), strict=True)}
    cs_mix = {n: B[n].rs_pairsum(n, mix_grads[n][0], r1_mix[n], qc) for n in mix_names}

    small_names = ["ln1_g", "ln1_b", "b_in", "sgu_ln_g", "sgu_ln_b", "sgu_w_s", "sgu_b_s", "conv_w_dw", "conv_b_dw",
                   "conv_ln_g", "conv_ln_b", "ln2_g", "ln2_b", "ln3_g", "ln3_b"]
    small_parts = {"ln1_g": dln1_g, "ln1_b": dln1_b, "b_in": db_in, "sgu_ln_g": dsgu_g, "sgu_ln_b": dsgu_b, "sgu_w_s": dwm,
                   "sgu_b_s": dbst.T, "conv_w_dw": dwdw[:KW], "conv_b_dw": dbdw, "conv_ln_g": dcln_g, "conv_ln_b": dcln_b,
                   "ln2_g": dln2_g, "ln2_b": dln2_b, "ln3_g": dln3_g, "ln3_b": dln3_b}
    packed, spans = _pack_rows([small_parts[n] for n in small_names])

    b_gu1, b_d1 = B["ffn1_w_gu"], B["ffn1_w_down"]
    c, un = _merge(B["w_in"].rs_ici(cs_mix["w_in"]), _small_allgather(packed), b_gu2.rs_share(gh_gu2))
    (dg1, du1), co = ffn_dact("ffn1", dr1h, wd1, g1, u1, carry=c)
    (r2_win,), (small_slots,), (g_gu2,) = un(co)
    c, un = _merge(*[B[n].rs_ici(cs_mix[n]) for n in mix_names[1:]])
    dwgu1, co = ffn_dwgate("ffn1", xb, dg1, carry=c)
    r2_mix = [[r2_win]] + un(co)
    gh_mix = [B[n].rs_final(n, mix_grads[n][0], r1_mix[n], r2, qc) for n, (r2,) in zip(mix_names, r2_mix, strict=True)]
    dwgu1 = ffn_dwup("ffn1", xb, du1, dwgu1)
    dwd1, (r1_gu1,) = ffn_dwdown("ffn1", a1, dr1h, carry=b_gu1.rs_pair(dwgu1[1]))
    cs_gu1 = b_gu1.rs_pairsum("ffn1_w_gu", dwgu1[0], r1_gu1, qc)
    c, un = _merge(b_gu1.rs_ici(cs_gu1), b_d1.rs_pair(dwd1[1]),
                   *[B[n].rs_share(gh) for n, gh in zip(mix_names, gh_mix, strict=True)])
    (dx_gate,), co = ffn_dx("ffn1", "gate", dg1, wgu1, [], carry=c)
    (r2_gu1,), (r1_d1,), *g_mixs = un(co)
    g_mix = {n: g for n, (g,) in zip(mix_names, g_mixs, strict=True)}
    cs_d1 = b_d1.rs_pairsum("ffn1_w_down", dwd1[0], r1_d1, qc)
    gh_gu1 = b_gu1.rs_final("ffn1_w_gu", dwgu1[0], r1_gu1, r2_gu1, qc)
    c, un = _merge(b_d1.rs_ici(cs_d1), b_gu1.rs_share(gh_gu1))
    (dx,), co = ffn_dx("ffn1", "up", du1, wgu1, [dr1, dx_gate], carry=c)
    (r2_d1,), (g_gu1,) = un(co)
    gh_d1 = b_d1.rs_final("ffn1_w_down", dwd1[0], r1_d1, r2_d1, qc)

    grads = {"ffn1_w_gu": g_gu1, "ffn2_w_gu": g_gu2, "ffn2_w_down": g_d2, **g_mix}
    outs_g, outs_d, outs_m, outs_v = {}, {}, {}, {}

    def adamw_big(n, g, carry=None):
        b = B[n]
        return _adamw(f"adamw_{n}", b.w, g, b.m, b.v, _tile_rows(b.sr, b.sc), carry=carry)

    upd = {}
    upd["w_a_proj"], (grads["ffn1_w_down"],) = adamw_big("w_a_proj", grads["w_a_proj"], carry=b_d1.rs_share(gh_d1))
    for n, _ in big_names:
        if n not in upd:
            upd[n] = adamw_big(n, grads[n])
        g_, d_, m_, v_ = upd[n]
        outs_g[n], outs_d[n], outs_m[n], outs_v[n] = g_[None], d_[None], m_[None], v_[None]
    gsum = _sum8("small_sum", small_slots)
    full_shapes = [args[n].shape if n != "conv_w_dw" else (1, KW, W) for n in small_names]
    gsmall = dict(zip(small_names, _unpack_rows(gsum, spans, full_shapes), strict=True))
    cs = conv_w_dw.shape[2]
    gsmall["conv_w_dw"] = lax.dynamic_slice_in_dim(gsmall["conv_w_dw"], q * cs, cs, axis=2)
    pw, spans2 = _pack_rows([args[n] for n in small_names])
    pg, _ = _pack_rows([gsmall[n] for n in small_names])
    pm, _ = _pack_rows([args["m_" + n] for n in small_names])
    pv, _ = _pack_rows([args["v_" + n] for n in small_names])
    _, pd, pmn, pvn = _adamw("adamw_small", pw, pg, pm, pv, pw.shape[0])
    shapes2 = [args[n].shape for n in small_names]
    for dst, src in ((outs_d, pd), (outs_m, pmn), (outs_v, pvn)):
        dst.update(zip(small_names, _unpack_rows(src, spans2, shapes2), strict=True))
    outs_g.update(gsmall)

    loss = lax.psum(loss_part[0, 0], ("x", "y", "c"))
    order = ["ffn1_w_gu", "ffn1_w_down", "ln1_g", "ln1_b", "w_in", "b_in", "sgu_ln_g", "sgu_ln_b", "sgu_w_s", "sgu_b_s",
             "w_a_proj", "conv_w_dw", "conv_b_dw", "conv_ln_g", "conv_ln_b", "w_b_proj", "w_out", "ln2_g", "ln2_b",
             "ffn2_w_gu", "ffn2_w_down", "ln3_g", "ln3_b"]
    return (loss, dx[None], *[outs_g[n] for n in order], *[outs_d[n] for n in order], *[outs_m[n] for n in order],
            *[outs_v[n] for n in order])
```

```python
import math

import jax
import jax.numpy as jnp
from jax import lax
from jax.experimental import pallas as pl
from jax.experimental.pallas import tpu as pltpu

BF16 = jnp.bfloat16
F32 = jnp.float32

LN_EPS = 1e-5
ALPHA = 2.0 ** 0.25
SGU_BLOCK = 128
SGU_CHUNK = 64
HALO = 32
SUBLANES = 8
LANES = 128
CONV_ROWS = 32
ADAM_LR = 0.001
ADAM_B1 = 0.9
ADAM_B2 = 0.999
ADAM_EPS = 1e-08
ADAM_WD = 0.01
ADAM_STEP = 10
N_CHIPS = 4
VMEM_LIMIT = 52 * 1024 * 1024
MESH = pl.DeviceIdType.MESH

_GELU_C0 = math.sqrt(2.0 / math.pi)
_GELU_C1 = 0.044715


def _cparams(sem):
    return pltpu.CompilerParams(dimension_semantics=sem, vmem_limit_bytes=VMEM_LIMIT)


def _gelu_parts(x):
    x2 = x * x
    t = jnp.tanh(_GELU_C0 * (x + _GELU_C1 * (x2 * x)))
    return 0.5 * (1.0 + t), t, x2


def _gelu(x):
    cdf, _, _ = _gelu_parts(x)
    return x * cdf


def _gelu_and_grad(x):
    cdf, t, x2 = _gelu_parts(x)
    grad = cdf + x * (0.5 * (1.0 - t * t)) * (_GELU_C0 * (1.0 + (3.0 * _GELU_C1) * x2))
    return x * cdf, grad


def _silu_grad(x):
    s = jax.nn.sigmoid(x)
    return s * (1.0 + x * (1.0 - s))


def _row_stats(x):
    mu = jnp.mean(x, axis=-1, keepdims=True)
    xc = x - mu
    var = jnp.mean(xc * xc, axis=-1, keepdims=True)
    rstd = lax.rsqrt(var + LN_EPS)
    return xc * rstd, rstd


def _ln_bwd_rows(dy, xhat, rstd, g):
    dxh = dy * g
    m1 = jnp.mean(dxh, axis=-1, keepdims=True)
    m2 = jnp.mean(dxh * xhat, axis=-1, keepdims=True)
    return rstd * (dxh - m1 - xhat * m2)


def _colsum(v):
    return jnp.sum(v, axis=0, keepdims=True)


class _Carry:
    def __init__(self, ro, rw, new, plan, n_remote, n_local=0):
        self.ro, self.rw, self.new, self.plan = list(ro), list(rw), list(new), plan
        self.n_remote, self.n_local = n_remote, n_local

    def sems(self):
        return [pltpu.SemaphoreType.DMA((self.n_remote,)), pltpu.SemaphoreType.DMA((self.n_remote,)),
                pltpu.SemaphoreType.DMA((max(self.n_local, 1),))]

    def copies(self, ro_refs, rw_refs, new_refs, send_sems, recv_sems, loc_sems):
        x, y, c = lax.axis_index("x"), lax.axis_index("y"), lax.axis_index("c")
        remote, local = self.plan(ro_refs, rw_refs, new_refs, x, y, c)
        assert len(remote) == self.n_remote and len(local) == self.n_local
        lcs = [pltpu.make_async_copy(s, d, loc_sems.at[k]) for k, (s, d) in enumerate(local)]
        rcs = [pltpu.make_async_remote_copy(src_ref=s, dst_ref=d, send_sem=send_sems.at[k], recv_sem=recv_sems.at[k],
                                            device_id=dev, device_id_type=MESH) for k, (s, d, dev) in enumerate(remote)]
        return lcs, rcs

    def out_shape(self):
        return [jax.ShapeDtypeStruct(a.shape, a.dtype) for a in self.rw] + self.new


def _start_all(lcs, rcs):
    for cp in lcs + rcs:
        cp.start()


def _wait_all(lcs, rcs):
    for cp in rcs:
        cp.wait_send()
    for cp in rcs:
        cp.wait_recv()
    for cp in lcs:
        cp.wait()


def _merge(*cs):
    ro = [a for c in cs for a in c.ro]
    rw = [a for c in cs for a in c.rw]
    new = [a for c in cs for a in c.new]

    def plan(ro_refs, rw_refs, new_refs, x, y, c):
        remote, local, a, b, d = [], [], 0, 0, 0
        for cc in cs:
            r, l = cc.plan(ro_refs[a:a + len(cc.ro)], rw_refs[b:b + len(cc.rw)], new_refs[d:d + len(cc.new)], x, y, c)
            a, b, d = a + len(cc.ro), b + len(cc.rw), d + len(cc.new)
            remote += r
            local += l
        return remote, local

    def unpack(couts):
        res, b, d = [], 0, len(rw)
        for cc in cs:
            res.append(list(couts[b:b + len(cc.rw)]) + list(couts[d:d + len(cc.new)]))
            b, d = b + len(cc.rw), d + len(cc.new)
        return res

    return _Carry(ro, rw, new, plan, sum(c.n_remote for c in cs), sum(c.n_local for c in cs)), unpack


_ANY = pl.BlockSpec(memory_space=pl.ANY)


def _exchange(name, carry):
    n_ro, n_rw, n_new = len(carry.ro), len(carry.rw), len(carry.new)

    def body(*refs):
        o0 = n_ro + n_rw
        lcs, rcs = carry.copies(refs[:n_ro], refs[o0:o0 + n_rw], refs[o0 + n_rw:o0 + n_rw + n_new], *refs[o0 + n_rw + n_new:])
        _start_all(lcs, rcs)
        _wait_all(lcs, rcs)

    return list(pl.pallas_call(
        body, name=name, in_specs=[_ANY] * (n_ro + n_rw), out_specs=[_ANY] * (n_rw + n_new), out_shape=carry.out_shape(),
        input_output_aliases={n_ro + k: k for k in range(n_rw)}, scratch_shapes=carry.sems())(*carry.ro, *carry.rw))


def _call(body, *, name, grid, in_specs, out_specs, out_shape, args, scratch=(), sem, carry=None, aliases=None):
    in_specs, out_specs, out_shape, scratch = list(in_specs), list(out_specs), list(out_shape), list(scratch)
    if carry is None:
        return list(pl.pallas_call(body, name=name, grid=grid, in_specs=in_specs, out_specs=out_specs, out_shape=out_shape,
                                   scratch_shapes=scratch, input_output_aliases=aliases or {},
                                   compiler_params=_cparams(sem))(*args))
    n_in, n_out, n_scr = len(in_specs), len(out_specs), len(scratch)
    n_ro, n_rw, n_new = len(carry.ro), len(carry.rw), len(carry.new)

    def wrapped(*refs):
        ins = refs[:n_in]
        ro_refs = refs[n_in:n_in + n_ro]
        o0 = n_in + n_ro + n_rw
        outs = refs[o0:o0 + n_out]
        rw_refs = refs[o0 + n_out:o0 + n_out + n_rw]
        new_refs = refs[o0 + n_out + n_rw:o0 + n_out + n_rw + n_new]
        s0 = o0 + n_out + n_rw + n_new
        scr = refs[s0:s0 + n_scr]
        sems = refs[s0 + n_scr:]
        first = pl.program_id(0) == 0
        last = pl.program_id(0) == grid[0] - 1
        for d in range(1, len(grid)):
            first = jnp.logical_and(first, pl.program_id(d) == 0)
            last = jnp.logical_and(last, pl.program_id(d) == grid[d] - 1)

        @pl.when(first)
        def _():
            _start_all(*carry.copies(ro_refs, rw_refs, new_refs, *sems))

        body(*ins, *outs, *scr)

        @pl.when(last)
        def _():
            _wait_all(*carry.copies(ro_refs, rw_refs, new_refs, *sems))

    al = dict(aliases or {})
    al.update({n_in + n_ro + k: n_out + k for k in range(n_rw)})
    res = pl.pallas_call(
        wrapped, name=name, grid=grid, in_specs=in_specs + [_ANY] * (n_ro + n_rw), out_specs=out_specs + [_ANY] * (n_rw + n_new),
        out_shape=out_shape + carry.out_shape(), scratch_shapes=scratch + carry.sems(), input_output_aliases=al,
        compiler_params=_cparams(("arbitrary",) * len(grid)))(*args, *carry.ro, *carry.rw)
    return list(res[:n_out]), list(res[n_out:])


def _mm(name, pairs, *, ta=False, tb=False, M, N, tm, tn, tk, nk, epilogue, outs, extras=(), sum_pairs=True, carry=None,
        sub=None):
    ni, nj = M // tm, N // tn
    assert ni * tm == M and nj * tn == N
    n_p = len(pairs)
    n_acc = 1 if sum_pairs else n_p
    in_specs, args = [], []
    for a, b, ak, bk, bj in pairs:
        if ta:
            in_specs.append(pl.BlockSpec((tk, tm), lambda i, j, k, ak=ak: (k + ak, i)))
        else:
            in_specs.append(pl.BlockSpec((tm, tk), lambda i, j, k, ak=ak: (i, k + ak)))
        if tb:
            in_specs.append(pl.BlockSpec((tn, tk), lambda i, j, k, bk=bk, bj=bj: (j + bj, k + bk)))
        else:
            in_specs.append(pl.BlockSpec((tk, tn), lambda i, j, k, bk=bk, bj=bj: (k + bk, j + bj)))
        args += [a, b]
    for arr, kind, jo in extras:
        if kind == "mn":
            in_specs.append(pl.BlockSpec((tm, tn), lambda i, j, k, jo=jo: (i, j + jo)))
        else:
            in_specs.append(pl.BlockSpec((1, tn), lambda i, j, k, jo=jo: (0, j + jo)))
        args.append(arr)
    out_rows = [o[3] if len(o) > 3 else tm for o in outs]
    out_specs = [pl.BlockSpec((r, tn), lambda i, j, k, jo=o[2]: (i, j + jo)) for o, r in zip(outs, out_rows, strict=True)]
    out_shape = [jax.ShapeDtypeStruct((ni * r, o[0]), o[1]) for o, r in zip(outs, out_rows, strict=True)]
    n_ex, n_out = len(extras), len(outs)
    dn = (((0 if ta else 1,), (1 if tb else 0,)), ((), ()))

    def body(*refs):
        ab = refs[: 2 * n_p]
        ex = refs[2 * n_p: 2 * n_p + n_ex]
        o0 = 2 * n_p + n_ex
        out_refs = refs[o0: o0 + n_out]
        acc_refs = refs[o0 + n_out:]

        def dots():
            res = []
            for p in range(n_p):
                a = ab[2 * p][...]
                b = ab[2 * p + 1][...]
                res.append(lax.dot_general(a.astype(BF16), b.astype(BF16), dn, preferred_element_type=F32))
            if sum_pairs:
                tot = res[0]
                for r in res[1:]:
                    tot = tot + r
                res = [tot]
            return res

        def finish(accs):
            tiles = epilogue(accs, [e[...] for e in ex])
            for r, t in zip(out_refs, tiles, strict=True):
                r[...] = t.astype(r.dtype)

        if nk == 1 and sub is not None and tn > sub:
            for s in range(tn // sub):
                cs = slice(s * sub, (s + 1) * sub)
                res = []
                for p in range(n_p):
                    b = ab[2 * p + 1][cs, :] if tb else ab[2 * p + 1][:, cs]
                    res.append(lax.dot_general(ab[2 * p][...].astype(BF16), b.astype(BF16), dn, preferred_element_type=F32))
                if sum_pairs:
                    tot = res[0]
                    for r in res[1:]:
                        tot = tot + r
                    res = [tot]
                tiles = epilogue(res, [e[:, cs] for e in ex])
                for r, t in zip(out_refs, tiles, strict=True):
                    r[:, cs] = t.astype(r.dtype)
        elif nk == 1:
            finish(dots())
        else:
            k = pl.program_id(2)
            d = dots()

            @pl.when(k == 0)
            def _():
                for r, v in zip(acc_refs, d, strict=True):
                    r[...] = v

            @pl.when(k > 0)
            def _():
                for r, v in zip(acc_refs, d, strict=True):
                    r[...] += v

            @pl.when(k == nk - 1)
            def _():
                finish([r[...] for r in acc_refs])

    scratch = [pltpu.VMEM((tm, tn), F32) for _ in range(n_acc)] if nk > 1 else []
    return _call(body, name=name, grid=(ni, nj, nk), in_specs=in_specs, out_specs=out_specs, out_shape=out_shape, args=args,
                 scratch=scratch, sem=("parallel", "parallel", "arbitrary"), carry=carry)


def _wgrad(name, a, b, *, M, N, T, tm, tn, tk, into=None, joff=0, ncols=None, carry=None):
    ncols = N if ncols is None else ncols
    ni, nj, nk = M // tm, N // tn, T // tk
    dn = (((0,), (0,)), ((), ()))

    def body(a_ref, b_ref, *rest):
        d = lax.dot_general(a_ref[...].astype(BF16), b_ref[...].astype(BF16), dn, preferred_element_type=F32)
        if nk == 1:
            of_ref, oh_ref = rest[-2:]
            of_ref[...] = d
            oh_ref[...] = d.astype(BF16)
            return
        of_ref, oh_ref, acc_ref = rest[-3:]
        k = pl.program_id(2)

        @pl.when(k == 0)
        def _():
            acc_ref[...] = d

        @pl.when(k > 0)
        def _():
            acc_ref[...] += d

        @pl.when(k == nk - 1)
        def _():
            of_ref[...] = acc_ref[...]
            oh_ref[...] = acc_ref[...].astype(BF16)

    ospec = pl.BlockSpec((tm, tn), lambda i, j, k: (i, j + joff))
    in_specs = [pl.BlockSpec((tk, tm), lambda i, j, k: (k, i)), pl.BlockSpec((tk, tn), lambda i, j, k: (k, j))]
    args, aliases = [a, b], None
    if into is not None:
        in_specs += [_ANY, _ANY]
        args += list(into)
        aliases = {2: 0, 3: 1}
    return _call(body, name=name, grid=(ni, nj, nk), in_specs=in_specs, out_specs=[ospec, ospec],
                 out_shape=[jax.ShapeDtypeStruct((M, ncols), F32), jax.ShapeDtypeStruct((M, ncols), BF16)], args=args,
                 scratch=[pltpu.VMEM((tm, tn), F32)] if nk > 1 else [], sem=("parallel", "parallel", "arbitrary"), carry=carry,
                 aliases=aliases)


def _rows(tm, c, cb=0):
    return pl.BlockSpec((tm, c), lambda i, cb=cb: (i, cb))


def _whole(shape):
    nd = len(shape)
    return pl.BlockSpec(shape, lambda i, nd=nd: (0,) * nd)


def _cast_bf16(name, x, tm, carry=None):
    t, d = x.shape

    def body(x_ref, o_ref):
        o_ref[...] = x_ref[...].astype(BF16)

    return _call(body, name=name, grid=(t // tm,), in_specs=[_rows(tm, d)], out_specs=[_rows(tm, d)],
                 out_shape=[jax.ShapeDtypeStruct((t, d), BF16)], args=[x], sem=("parallel",), carry=carry)


def _residual(x_ref, prev_refs):
    if not prev_refs:
        return x_ref[...]
    return x_ref[...] * prev_refs[0][...] + prev_refs[1][...]


def _ln_fwd(name, xres, f, g, b, cf, tm, carry=None, prev=()):
    t, d = xres.shape
    n_prev = len(prev)

    def body(x_ref, f_ref, g_ref, b_ref, *rest):
        yb_ref, xh_ref, rs_ref = rest[n_prev:]
        r = ALPHA * _residual(x_ref, rest[:n_prev]) + cf * f_ref[...]
        xhat, rstd = _row_stats(r)
        yb_ref[...] = (xhat * g_ref[...] + b_ref[...]).astype(BF16)
        xh_ref[...] = xhat
        rs_ref[...] = rstd

    return _call(
        body, name=name, grid=(t // tm,),
        in_specs=[_rows(tm, d), _rows(tm, d), _whole((1, d)), _whole((1, d))] + [_whole((1, d))] * n_prev,
        out_specs=[_rows(tm, d), _rows(tm, d), _rows(tm, 1)],
        out_shape=[jax.ShapeDtypeStruct((t, d), BF16), jax.ShapeDtypeStruct((t, d), F32), jax.ShapeDtypeStruct((t, 1), F32)],
        args=[xres, f, g, b, *prev], sem=("parallel",), carry=carry)


def _ln_bwd(name, dy, xhat, rstd, g, scale, tm, carry=None):
    t, d = dy.shape

    def body(dy_ref, xh_ref, rs_ref, g_ref, dr_ref, drb_ref, dg_ref, db_ref):
        i = pl.program_id(0)
        dy_v, xh = dy_ref[...], xh_ref[...]
        dr = _ln_bwd_rows(dy_v, xh, rs_ref[...], g_ref[...])
        dr_ref[...] = dr
        drb_ref[...] = (scale * dr).astype(BF16)

        @pl.when(i == 0)
        def _():
            dg_ref[...] = jnp.zeros_like(dg_ref)
            db_ref[...] = jnp.zeros_like(db_ref)

        dg_ref[...] += _colsum(dy_v * xh)
        db_ref[...] += _colsum(dy_v)

    return _call(
        body, name=name, grid=(t // tm,), in_specs=[_rows(tm, d), _rows(tm, d), _rows(tm, 1), _whole((1, d))],
        out_specs=[_rows(tm, d), _rows(tm, d), _whole((1, d)), _whole((1, d))],
        out_shape=[jax.ShapeDtypeStruct((t, d), F32), jax.ShapeDtypeStruct((t, d), BF16),
                   jax.ShapeDtypeStruct((1, d), F32), jax.ShapeDtypeStruct((1, d), F32)],
        args=[dy, xhat, rstd, g], sem=("arbitrary",), carry=carry)


def _ln3_loss(name, xres, f, g, b, target, tm, prev):
    t, d = xres.shape

    def body(x_ref, f_ref, g_ref, b_ref, tg_ref, pg_ref, pb_ref, dr_ref, drb_ref, loss_ref, dg_ref, db_ref):
        i = pl.program_id(0)
        r = ALPHA * _residual(x_ref, (pg_ref, pb_ref)) + 0.5 * f_ref[...]
        xhat, rstd = _row_stats(r)
        gv = g_ref[...]
        y = xhat * gv + b_ref[...]
        err = y - tg_ref[...]
        dy = err * (1.0 / d)
        dr = _ln_bwd_rows(dy, xhat, rstd, gv)
        dr_ref[...] = dr
        drb_ref[...] = (0.5 * dr).astype(BF16)
        part = 0.5 * jnp.sum(jnp.mean(err * err, axis=-1, keepdims=True), axis=0, keepdims=True)

        @pl.when(i == 0)
        def _():
            loss_ref[...] = jnp.zeros_like(loss_ref)
            dg_ref[...] = jnp.zeros_like(dg_ref)
            db_ref[...] = jnp.zeros_like(db_ref)

        loss_ref[...] += jnp.broadcast_to(part, loss_ref.shape)
        dg_ref[...] += _colsum(dy * xhat)
        db_ref[...] += _colsum(dy)

    return _call(
        body, name=name, grid=(t // tm,),
        in_specs=[_rows(tm, d), _rows(tm, d), _whole((1, d)), _whole((1, d)), _rows(tm, d), _whole((1, d)), _whole((1, d))],
        out_specs=[_rows(tm, d), _rows(tm, d), _whole((8, 128)), _whole((1, d)), _whole((1, d))],
        out_shape=[jax.ShapeDtypeStruct((t, d), F32), jax.ShapeDtypeStruct((t, d), BF16),
                   jax.ShapeDtypeStruct((8, 128), F32), jax.ShapeDtypeStruct((1, d), F32),
                   jax.ShapeDtypeStruct((1, d), F32)],
        args=[xres, f, g, b, target, *prev], sem=("arbitrary",))


def _colsum_rows(name, x, tm):
    t, d = x.shape

    def body(x_ref, o_ref):
        @pl.when(pl.program_id(0) == 0)
        def _():
            o_ref[...] = jnp.zeros_like(o_ref)

        o_ref[...] += _colsum(x_ref[...].astype(F32))

    return _call(body, name=name, grid=(t // tm,), in_specs=[_rows(tm, d)], out_specs=[_whole((1, d))],
                 out_shape=[jax.ShapeDtypeStruct((1, d), F32)], args=[x], sem=("arbitrary",))[0]


def _sgu_mask():
    sh = SGU_CHUNK.bit_length() - 1
    r = lax.shift_right_logical(lax.broadcasted_iota(jnp.int32, (SGU_BLOCK, SGU_BLOCK), 0), sh)
    c = lax.shift_right_logical(lax.broadcasted_iota(jnp.int32, (SGU_BLOCK, SGU_BLOCK), 1), sh)
    return c <= r


def _sgu_fwd(name, p, lng, lnb, wm, bst, tm):
    t = p.shape[0]
    n_grp, w = wm.shape[0], lng.shape[1]
    hd = w // n_grp
    nblk = tm // SGU_BLOCK

    def body(uv_ref, h_ref, g_ref, b_ref, wm_ref, bs_ref, sa_ref, z_ref, vn_s):
        xhat, _ = _row_stats(_gelu(uv_ref[:, w:]))
        vn_s[...] = (xhat * g_ref[...] + b_ref[...]).astype(BF16)
        mask = _sgu_mask()
        for h in range(n_grp):
            wh = jnp.where(mask, wm_ref[h], 0.0).astype(BF16)
            bcol = bs_ref[:, h:h + 1]
            cs = slice(h * hd, (h + 1) * hd)
            for n in range(nblk):
                rs = slice(n * SGU_BLOCK, (n + 1) * SGU_BLOCK)
                s = jnp.dot(wh, vn_s[rs, cs], preferred_element_type=F32) + bcol
                sa_ref[rs, cs] = (_gelu(uv_ref[rs, cs]) * s).astype(BF16)
        z_ref[...] = h_ref[:, :w] * jax.nn.sigmoid(h_ref[:, w:])

    return _call(
        body, name=name, grid=(t // tm,),
        in_specs=[_rows(tm, 2 * w, 0), _rows(tm, 2 * w, 1), _whole((1, w)), _whole((1, w)), _whole(wm.shape),
                  _whole(bst.shape)],
        out_specs=[_rows(tm, w), _rows(tm, w)],
        out_shape=[jax.ShapeDtypeStruct((t, w), BF16), jax.ShapeDtypeStruct((t, w), F32)],
        args=[p, p, lng, lnb, wm, bst], scratch=[pltpu.VMEM((tm, w), BF16)], sem=("parallel",))


def _sgu_bwd(name, p, dsa, lng, lnb, wm, bst, tm):
    t = p.shape[0]
    n_grp, w = wm.shape[0], lng.shape[1]
    hd = w // n_grp
    nblk = tm // SGU_BLOCK

    def body(uv_ref, dsa_ref, g_ref, b_ref, wm_ref, bs_ref, dp_ref, dwm_ref, dbs_ref, dg_ref, db_ref, dbin_ref,
             vn_s, ug_s, dvn_s, dug_s):
        i = pl.program_id(0)

        @pl.when(i == 0)
        def _():
            dwm_ref[...] = jnp.zeros_like(dwm_ref)
            dbs_ref[...] = jnp.zeros_like(dbs_ref)
            dg_ref[...] = jnp.zeros_like(dg_ref)
            db_ref[...] = jnp.zeros_like(db_ref)
            dbin_ref[...] = jnp.zeros_like(dbin_ref)

        ug, dgelu_u = _gelu_and_grad(uv_ref[:, :w])
        ug_s[...] = ug
        vg, dgelu_v = _gelu_and_grad(uv_ref[:, w:])
        xhat, rstd = _row_stats(vg)
        gv = g_ref[...]
        vn_s[...] = (xhat * gv + b_ref[...]).astype(BF16)
        mask = _sgu_mask()
        for h in range(n_grp):
            wh = jnp.where(mask, wm_ref[h], 0.0).astype(BF16)
            bcol = bs_ref[:, h:h + 1]
            cs = slice(h * hd, (h + 1) * hd)
            dw_h = jnp.zeros((SGU_BLOCK, SGU_BLOCK), F32)
            dbs_h = jnp.zeros((SGU_BLOCK, 1), F32)
            for n in range(nblk):
                rs = slice(n * SGU_BLOCK, (n + 1) * SGU_BLOCK)
                vblk = vn_s[rs, cs]
                s = jnp.dot(wh, vblk, preferred_element_type=F32) + bcol
                dsa_blk = dsa_ref[rs, cs]
                dug_s[rs, cs] = dsa_blk * s
                ds = dsa_blk * ug_s[rs, cs]
                dsb = ds.astype(BF16)
                dvn_s[rs, cs] = lax.dot_general(wh, dsb, (((0,), (0,)), ((), ())), preferred_element_type=F32)
                dw_h = dw_h + lax.dot_general(dsb, vblk, (((1,), (1,)), ((), ())), preferred_element_type=F32)
                dbs_h = dbs_h + jnp.sum(ds, axis=1, keepdims=True)
            dwm_ref[h] += jnp.where(mask, dw_h, 0.0)
            dbs_ref[:, h:h + 1] += dbs_h
        dvn = dvn_s[...]
        dg_ref[...] += _colsum(dvn * xhat)
        db_ref[...] += _colsum(dvn)
        dvg = _ln_bwd_rows(dvn, xhat, rstd, gv)
        du = dug_s[...] * dgelu_u
        dv = dvg * dgelu_v
        dp_ref[:, :w] = du.astype(BF16)
        dp_ref[:, w:] = dv.astype(BF16)
        dbin_ref[:, :w] += _colsum(du)
        dbin_ref[:, w:] += _colsum(dv)

    return _call(
        body, name=name, grid=(t // tm,),
        in_specs=[_rows(tm, 2 * w, 0), _rows(tm, w), _whole((1, w)), _whole((1, w)), _whole(wm.shape), _whole(bst.shape)],
        out_specs=[_rows(tm, 2 * w), _whole(wm.shape), _whole(bst.shape), _whole((1, w)), _whole((1, w)), _whole((1, 2 * w))],
        out_shape=[jax.ShapeDtypeStruct((t, 2 * w), BF16), jax.ShapeDtypeStruct(wm.shape, F32),
                   jax.ShapeDtypeStruct(bst.shape, F32), jax.ShapeDtypeStruct((1, w), F32), jax.ShapeDtypeStruct((1, w), F32),
                   jax.ShapeDtypeStruct((1, 2 * w), F32)],
        args=[p, dsa, lng, lnb, wm, bst],
        scratch=[pltpu.VMEM((tm, w), BF16), pltpu.VMEM((tm, w), F32), pltpu.VMEM((tm, w), F32), pltpu.VMEM((tm, w), F32)],
        sem=("arbitrary",))


def _halo_prev(tm, c):
    return pl.BlockSpec((HALO, c), lambda i: (jnp.maximum(i * (tm // HALO) - 1, 0), 0))


def _halo_next(tm, c, t):
    last = t // HALO - 1
    return pl.BlockSpec((HALO, c), lambda i: (jnp.minimum((i + 1) * (tm // HALO), last), 0))


def _shifted_copies(sh, n):
    for r in range(1, SUBLANES):
        sh[r, :n - SUBLANES, :] = sh[0, r:r + n - SUBLANES, :]


def _row_broadcasts(wb, w_ref, kw):
    for k in range(kw):
        wb[k] = jnp.broadcast_to(w_ref[k:k + 1, :], wb.shape[1:])


def _tap(sh, r0, o, rows, cols):
    return sh[o % SUBLANES, pl.ds(pl.multiple_of(r0 + (o - o % SUBLANES), SUBLANES), rows), cols]


def _conv_fwd(name, z, wdw, kw, bdw, lng, lnb, tm, carry=None):
    t, c = z.shape
    lead = HALO - (kw - 1)
    n = tm + HALO

    def body(zp_ref, z_ref, w_ref, bdw_ref, g_ref, b_ref, zc_ref, sb_ref, sh, wb):
        i = pl.program_id(0)
        sh[0, :HALO, :] = jnp.where(i > 0, zp_ref[...], 0.0)
        sh[0, HALO:, :] = z_ref[...]
        _shifted_copies(sh, n)
        _row_broadcasts(wb, w_ref, kw)
        bias = jnp.broadcast_to(bdw_ref[...], (SUBLANES, c))
        groups = CONV_ROWS // SUBLANES

        def chunk(ci, _):
            r0 = pl.multiple_of(ci * CONV_ROWS, CONV_ROWS)
            accs = [bias] * groups
            for k in range(kw):
                wk = wb[k]
                tp = _tap(sh, r0, lead + k, CONV_ROWS, slice(None))
                accs = [accs[g] + wk * tp[g * SUBLANES:(g + 1) * SUBLANES] for g in range(groups)]
            zc_ref[pl.ds(r0, CONV_ROWS), :] = jnp.concatenate(accs, axis=0)
            return 0

        lax.fori_loop(0, tm // CONV_ROWS, chunk, 0)
        xhat, _ = _row_stats(zc_ref[...])
        zn = xhat * g_ref[...] + b_ref[...]
        sb_ref[...] = (zn * jax.nn.sigmoid(zn)).astype(BF16)

    return _call(
        body, name=name, grid=(t // tm,),
        in_specs=[_halo_prev(tm, c), _rows(tm, c), _whole(wdw.shape), _whole((1, c)), _whole((1, c)), _whole((1, c))],
        out_specs=[_rows(tm, c), _rows(tm, c)],
        out_shape=[jax.ShapeDtypeStruct((t, c), F32), jax.ShapeDtypeStruct((t, c), BF16)],
        args=[z, z, wdw, bdw, lng, lnb], scratch=[pltpu.VMEM((SUBLANES, n, c), F32), pltpu.VMEM((HALO, SUBLANES, c), F32)],
        sem=("parallel",), carry=carry)


def _conv_ln_bwd(name, dsb, zc, lng, lnb, tm):
    t, c = zc.shape

    def body(dsb_ref, zc_ref, g_ref, b_ref, dzc_ref, dg_ref, db_ref, dbdw_ref):
        i = pl.program_id(0)
        xhat, rstd = _row_stats(zc_ref[...])
        gv = g_ref[...]
        zn = xhat * gv + b_ref[...]
        dzn = dsb_ref[...] * _silu_grad(zn)
        dzc = _ln_bwd_rows(dzn, xhat, rstd, gv)
        dzc_ref[...] = dzc

        @pl.when(i == 0)
        def _():
            dg_ref[...] = jnp.zeros_like(dg_ref)
            db_ref[...] = jnp.zeros_like(db_ref)
            dbdw_ref[...] = jnp.zeros_like(dbdw_ref)

        dg_ref[...] += _colsum(dzn * xhat)
        db_ref[...] += _colsum(dzn)
        dbdw_ref[...] += _colsum(dzc)

    return _call(
        body, name=name, grid=(t // tm,), in_specs=[_rows(tm, c), _rows(tm, c), _whole((1, c)), _whole((1, c))],
        out_specs=[_rows(tm, c), _whole((1, c)), _whole((1, c)), _whole((1, c))],
        out_shape=[jax.ShapeDtypeStruct((t, c), F32)] + [jax.ShapeDtypeStruct((1, c), F32)] * 3,
        args=[dsb, zc, lng, lnb], sem=("arbitrary",))


def _conv_bwd(name, dsb, zc, z, p, wdw, kw, lng, lnb, tm):
    t, c = z.shape
    n_i = t // tm
    n = tm + HALO

    def body(dsb_ref, dsbn_ref, zc_ref, zcn_ref, z_ref, h_ref, w_ref, g_ref, b_ref, dp_ref, dw_ref, dbin_ref, dg_ref, db_ref,
             dbdw_ref, sh, dz_s, wb):
        i = pl.program_id(0)

        @pl.when(i == 0)
        def _():
            dw_ref[...] = jnp.zeros_like(dw_ref)
            dg_ref[...] = jnp.zeros_like(dg_ref)
            db_ref[...] = jnp.zeros_like(db_ref)
            dbdw_ref[...] = jnp.zeros_like(dbdw_ref)

        gv, bv = g_ref[...], b_ref[...]

        def conv_out_grad(dsb_v, zc_v):
            xhat, rstd = _row_stats(zc_v)
            dzn = dsb_v * _silu_grad(xhat * gv + bv)
            return _ln_bwd_rows(dzn, xhat, rstd, gv), dzn, xhat

        dzc, dzn, xhat = conv_out_grad(dsb_ref[...], zc_ref[...])
        dg_ref[...] += _colsum(dzn * xhat)
        db_ref[...] += _colsum(dzn)
        dbdw_ref[...] += _colsum(dzc)
        sh[0, :tm, :] = dzc
        sh[0, tm:, :] = jnp.where(i < n_i - 1, conv_out_grad(dsbn_ref[...], zcn_ref[...])[0], 0.0)
        _shifted_copies(sh, n)
        _row_broadcasts(wb, w_ref, kw)
        nv = 8
        rows = nv * SUBLANES
        ways = 2
        tap_groups = [range(0, kw // 2), range(kw // 2, kw)]
        for lc in range(c // LANES):
            cols = slice(lc * LANES, (lc + 1) * LANES)
            for gi, taps in enumerate(tap_groups):

                def rowv(rv, accs, cols=cols, taps=taps, first=gi == 0):
                    r0 = pl.multiple_of(rv * rows, rows)
                    zv = z_ref[pl.ds(r0, rows), cols]
                    zs = [zv[v * SUBLANES:(v + 1) * SUBLANES] for v in range(nv)]
                    dz = [[None] * ways for _ in range(nv)]
                    new = []
                    for ai, k in enumerate(taps):
                        s = _tap(sh, r0, kw - 1 - k, rows, cols)
                        wk = wb[k, :, cols]
                        j = ai % ways
                        acc = accs[ai]
                        for v in range(nv):
                            sv = s[v * SUBLANES:(v + 1) * SUBLANES]
                            dz[v][j] = wk * sv if dz[v][j] is None else dz[v][j] + wk * sv
                            acc = acc + zs[v] * sv
                        new.append(acc)
                    part = jnp.concatenate([dz[v][0] + dz[v][1] for v in range(nv)], axis=0)
                    dz_s[pl.ds(r0, rows), cols] = part if first else dz_s[pl.ds(r0, rows), cols] + part
                    return tuple(new)

                accs = lax.fori_loop(0, tm // rows, rowv, tuple(jnp.zeros((SUBLANES, LANES), F32) for _ in taps))
                for ai, k in enumerate(taps):
                    dw_ref[k:k + 1, cols] += _colsum(accs[ai])
        dz = dz_s[...]
        a, g = h_ref[:, :c], h_ref[:, c:]
        sg = jax.nn.sigmoid(g)
        da = dz * sg
        dg = dz * a * (sg * (1.0 - sg))
        dp_ref[:, :c] = da.astype(BF16)
        dp_ref[:, c:] = dg.astype(BF16)

        @pl.when(i == 0)
        def _():
            dbin_ref[...] = jnp.zeros_like(dbin_ref)

        dbin_ref[:, :c] += _colsum(da)
        dbin_ref[:, c:] += _colsum(dg)

    return _call(
        body, name=name, grid=(n_i,),
        in_specs=[_rows(tm, c), _halo_next(tm, c, t), _rows(tm, c), _halo_next(tm, c, t), _rows(tm, c), _rows(tm, 2 * c, 1),
                  _whole(wdw.shape), _whole((1, c)), _whole((1, c))],
        out_specs=[_rows(tm, 2 * c), _whole((HALO, c)), _whole((1, 2 * c)), _whole((1, c)), _whole((1, c)), _whole((1, c))],
        out_shape=[jax.ShapeDtypeStruct((t, 2 * c), BF16), jax.ShapeDtypeStruct((HALO, c), F32),
                   jax.ShapeDtypeStruct((1, 2 * c), F32)] + [jax.ShapeDtypeStruct((1, c), F32)] * 3,
        args=[dsb, dsb, zc, zc, z, p, wdw, lng, lnb],
        scratch=[pltpu.VMEM((SUBLANES, n, c), F32), pltpu.VMEM((tm, c), F32), pltpu.VMEM((HALO, SUBLANES, c), F32)],
        sem=("arbitrary",))


def _adamw(name, w, g, m, v, tr, carry=None):
    r, c = w.shape
    c1 = 1.0 - ADAM_B1 ** ADAM_STEP
    c2 = 1.0 - ADAM_B2 ** ADAM_STEP

    def body(w_ref, g_ref, m_ref, v_ref, go_ref, d_ref, mo_ref, vo_ref):
        gv = g_ref[...]
        mn = ADAM_B1 * m_ref[...] + (1.0 - ADAM_B1) * gv
        vn = ADAM_B2 * v_ref[...] + (1.0 - ADAM_B2) * (gv * gv)
        go_ref[...] = gv
        d_ref[...] = -ADAM_LR * ((mn / c1) / (jnp.sqrt(vn / c2) + ADAM_EPS) + ADAM_WD * w_ref[...])
        mo_ref[...] = mn
        vo_ref[...] = vn

    spec = _rows(tr, c)
    return _call(body, name=name, grid=(r // tr,), in_specs=[spec] * 4, out_specs=[spec] * 4,
                 out_shape=[jax.ShapeDtypeStruct((r, c), F32)] * 4, args=[w, g, m, v], sem=("parallel",), carry=carry)


_RELATION_XOR = (2, 1, 3)


def _pair_sum(name, mine, recv, qc, *, part_shape, tr, in_map):
    pr, pc = part_shape

    def body(qc_ref, a_ref, b_ref, oh_ref):
        oh_ref[...] = (a_ref[...] + b_ref[...].astype(F32)).astype(BF16)

    def imap(k, i, qc_ref):
        part = qc_ref[0]
        for kk, m in enumerate(_RELATION_XOR):
            part = jnp.where(k == kk, jnp.bitwise_xor(qc_ref[0], m), part)
        return in_map(i, part, qc_ref[1])

    ispec = pl.BlockSpec((tr, pc), imap)
    ospec = pl.BlockSpec((None, tr, pc), lambda k, i, qc_ref: (k, i, 0))
    return pl.pallas_call(
        body, name=name,
        grid_spec=pltpu.PrefetchScalarGridSpec(num_scalar_prefetch=1, grid=(3, pr // tr), in_specs=[ispec, ispec],
                                               out_specs=ospec),
        out_shape=jax.ShapeDtypeStruct((3, pr, pc), BF16), compiler_params=_cparams(("parallel", "parallel")))(qc, mine, recv)


def _final_sum(name, mine, recv1, recv2, qc, *, part_shape, out_shape, tr, in_map, out_map):
    pr, pc = part_shape

    def body(qc_ref, a_ref, b_ref, r_ref, o_ref):
        own = a_ref[...] + b_ref[...].astype(F32)
        o_ref[...] = ((own + r_ref[0].astype(F32)) + r_ref[1].astype(F32)) + r_ref[2].astype(F32)

    ispec = pl.BlockSpec((tr, pc), lambda i, qc_ref: in_map(i, qc_ref[0], qc_ref[1]))
    return pl.pallas_call(
        body, name=name,
        grid_spec=pltpu.PrefetchScalarGridSpec(
            num_scalar_prefetch=1, grid=(pr // tr,),
            in_specs=[ispec, ispec, pl.BlockSpec((3, tr, pc), lambda i, qc_ref: (0, i, 0))],
            out_specs=pl.BlockSpec((tr, pc), lambda i, qc_ref: out_map(i, qc_ref[0], qc_ref[1]))),
        out_shape=jax.ShapeDtypeStruct(out_shape, F32), compiler_params=_cparams(("parallel",)))(qc, mine, recv1, recv2)


def _sum8(name, slots):
    def body(s_ref, o_ref):
        acc = s_ref[0]
        for d in range(1, 8):
            acc = acc + s_ref[d]
        o_ref[...] = acc

    return pl.pallas_call(body, name=name, out_shape=jax.ShapeDtypeStruct(slots.shape[1:], F32),
                          in_specs=[pl.BlockSpec(memory_space=pltpu.VMEM)], out_specs=pl.BlockSpec(memory_space=pltpu.VMEM),
                          compiler_params=pltpu.CompilerParams(vmem_limit_bytes=VMEM_LIMIT))(slots)


def _chips(x, y):
    return [(1 - x, y), (x, 1 - y), (1 - x, 1 - y)]


class _Big:
    def __init__(self, name, w, m, v, ax):
        self.name, self.w, self.m, self.v, self.ax = name, w, m, v, ax
        sr, sc = w.shape
        self.R, self.C = (sr * N_CHIPS, sc) if ax == 0 else (sr, sc * N_CHIPS)
        self.sr, self.sc = sr, sc
        self.hr = sr // 2

    def slot(self, ref, q, h=None):
        if self.ax == 1:
            cols = pl.ds(pl.multiple_of(q * self.sc, 128), self.sc)
            return ref.at[:, cols] if h is None else ref.at[pl.ds(pl.multiple_of(h * self.hr, 16), self.hr), cols]
        if h is None:
            return ref.at[pl.ds(pl.multiple_of(q * self.sr, 16), self.sr), :]
        return ref.at[pl.ds(pl.multiple_of(q * self.sr + h * self.hr, 16), self.hr), :]

    def half(self, ref, h):
        return ref.at[pl.ds(pl.multiple_of(h * self.hr, 16), self.hr), :]

    @property
    def part_shape(self):
        return (self.hr, self.sc)

    def cast_into_full(self, qc):
        tr = _tile_rows(self.sr, self.sc)
        nb = self.sr // tr

        def body(qc_ref, x_ref, o_ref):
            o_ref[...] = x_ref[...].astype(BF16)

        if self.ax == 1:
            ospec = pl.BlockSpec((tr, self.sc), lambda i, qc_ref: (i, qc_ref[0]))
        else:
            ospec = pl.BlockSpec((tr, self.sc), lambda i, qc_ref: (qc_ref[0] * nb + i, 0))
        return pl.pallas_call(
            body, name=f"cast_{self.name}",
            grid_spec=pltpu.PrefetchScalarGridSpec(num_scalar_prefetch=1, grid=(nb,),
                                                   in_specs=[pl.BlockSpec((tr, self.sc), lambda i, qc_ref: (i, 0))],
                                                   out_specs=ospec),
            out_shape=jax.ShapeDtypeStruct((self.R, self.C), BF16), compiler_params=_cparams(("parallel",)))(qc, self.w)

    def gather_ici(self, full, piece=(0, 1)):
        k, n = piece
        pr = self.hr // n
        assert pr * n == self.hr and pr % 16 == 0

        def plan(ro, rw, new, x, y, c):
            q = 2 * x + y
            r0 = c * self.hr + k * pr
            if self.ax == 1:
                mine = rw[0].at[pl.ds(pl.multiple_of(r0, 16), pr), pl.ds(pl.multiple_of(q * self.sc, 128), self.sc)]
            else:
                mine = rw[0].at[pl.ds(pl.multiple_of(q * self.sr + r0, 16), pr), :]
            return [(mine, mine, (cx, cy, c)) for cx, cy in _chips(x, y)], []

        return _Carry([], [full], [], plan, 3, 0)

    def gather_d2d(self, full):
        def plan(ro, rw, new, x, y, c):
            remote = []
            for cx, cy in _chips(x, y):
                piece = self.slot(rw[0], 2 * cx + cy, c)
                remote.append((piece, piece, (x, y, 1 - c)))
            return remote, []

        return _Carry([], [full], [], plan, 3)

    def rs_pair(self, g16):
        def plan(ro, rw, new, x, y, c):
            sib = (x, y, 1 - c)
            if self.ax == 1:
                rows = pl.ds(pl.multiple_of((1 - c) * self.hr, 16), self.hr)
                return [(ro[0].at[rows, :], new[0].at[rows, :], sib)], []
            return [(self.slot(ro[0], q, 1 - c), self.slot(new[0], q, 1 - c), sib) for q in range(N_CHIPS)], []

        return _Carry([g16], [], [jax.ShapeDtypeStruct((self.R, self.C), BF16)], plan, 1 if self.ax == 1 else N_CHIPS)

    def _piece_map(self, nb):
        if self.ax == 1:
            return lambda i, q, c: (c * nb + i, q)
        return lambda i, q, c: ((q * 2 + c) * nb + i, 0)

    def rs_pairsum(self, tag, g32, recv1, qc):
        tr = _tile_rows(self.hr, self.sc)
        return _pair_sum(f"rs_pairsum_{tag}", g32, recv1, qc, part_shape=self.part_shape, tr=tr,
                         in_map=self._piece_map(self.hr // tr))

    def rs_ici(self, cs16):
        def plan(ro, rw, new, x, y, c):
            return [(ro[0].at[k], new[0].at[k], (cx, cy, c)) for k, (cx, cy) in enumerate(_chips(x, y))], []

        return _Carry([cs16], [], [jax.ShapeDtypeStruct((3,) + self.part_shape, BF16)], plan, 3)

    def rs_final(self, tag, g32, recv1, recv2, qc):
        tr = _tile_rows(self.hr, self.sc)
        nb = self.hr // tr
        in_map = self._piece_map(nb)
        out_map = lambda i, q, c: (c * nb + i, 0)
        return _final_sum(f"rs_final_{tag}", g32, recv1, recv2, qc, part_shape=self.part_shape, out_shape=(self.sr, self.sc),
                          tr=tr, in_map=in_map, out_map=out_map)

    def rs_share(self, ghalf):
        def plan(ro, rw, new, x, y, c):
            piece = self.half(rw[0], c)
            return [(piece, piece, (x, y, 1 - c))], []

        return _Carry([], [ghalf], [], plan, 1)


def _small_allgather(packed):
    nr = packed.shape[0]

    def plan(ro, rw, new, x, y, c):
        me = 4 * x + 2 * y + c
        remote = []
        for fx in (0, 1):
            for fy in (0, 1):
                for fc in (0, 1):
                    if fx or fy or fc:
                        dev = (1 - x if fx else x, 1 - y if fy else y, 1 - c if fc else c)
                        remote.append((ro[0], new[0].at[me], dev))
        return remote, [(ro[0], new[0].at[me])]

    return _Carry([packed], [], [jax.ShapeDtypeStruct((8, nr, 128), F32)], plan, 7, 1)


def _conv_w_allgather(padded, cs):
    def plan(ro, rw, new, x, y, c):
        cols = pl.ds(pl.multiple_of((2 * x + y) * cs, 128), cs)
        remote = [(ro[0], new[0].at[:, cols], (cx, cy, c)) for cx, cy in _chips(x, y)]
        return remote, [(ro[0], new[0].at[:, cols])]

    return _Carry([padded], [], [jax.ShapeDtypeStruct((HALO, cs * N_CHIPS), F32)], plan, 3, 1)


def _pick(n, want):
    if n <= want:
        return n
    for t in range(want, 15, -16):
        if t % 16 == 0 and n % t == 0:
            return t
    raise ValueError(f"no tile for {n} (want {want})")


def _tile_rows(nrows, ncols, budget=2 * 1024 * 1024):
    return _pick(nrows, max(16, (budget // (4 * ncols)) // 16 * 16))


def _pick128(n, want):
    if n <= want:
        return n
    for t in range(want, 127, -128):
        if n % t == 0:
            return t
    raise ValueError(f"no lane tile for {n} (want {want})")


def _pack_rows(parts):
    out, spans, r0 = [], [], 0
    for p in parts:
        flat = p.reshape(-1).astype(F32)
        n = flat.shape[0]
        rows = -(-n // 1024) * 8
        flat = jnp.pad(flat, (0, rows * 128 - n))
        out.append(flat.reshape(rows, 128))
        spans.append((r0, rows, n))
        r0 += rows
    return jnp.concatenate(out, axis=0), spans


def _unpack_rows(packed, spans, shapes):
    res = []
    for (r0, rows, n), shp in zip(spans, shapes, strict=True):
        res.append(packed[r0:r0 + rows].reshape(-1)[:n].reshape(shp))
    return res


def _ident(accs, ex):
    return [accs[0]]


def kernel(x, ffn1_w_gu, ffn1_w_down, ln1_g, ln1_b, w_in, b_in, sgu_ln_g, sgu_ln_b, sgu_w_s, sgu_b_s, w_a_proj, conv_w_dw, conv_b_dw, conv_ln_g, conv_ln_b, w_b_proj, w_out, ln2_g, ln2_b, ffn2_w_gu, ffn2_w_down, ln3_g, ln3_b, loss_target, m_ffn1_w_gu, m_ffn1_w_down, m_ln1_g, m_ln1_b, m_w_in, m_b_in, m_sgu_ln_g, m_sgu_ln_b, m_sgu_w_s, m_sgu_b_s, m_w_a_proj, m_conv_w_dw, m_conv_b_dw, m_conv_ln_g, m_conv_ln_b, m_w_b_proj, m_w_out, m_ln2_g, m_ln2_b, m_ffn2_w_gu, m_ffn2_w_down, m_ln3_g, m_ln3_b, v_ffn1_w_gu, v_ffn1_w_down, v_ln1_g, v_ln1_b, v_w_in, v_b_in, v_sgu_ln_g, v_sgu_ln_b, v_sgu_w_s, v_sgu_b_s, v_w_a_proj, v_conv_w_dw, v_conv_b_dw, v_conv_ln_g, v_conv_ln_b, v_w_b_proj, v_w_out, v_ln2_g, v_ln2_b, v_ffn2_w_gu, v_ffn2_w_down, v_ln3_g, v_ln3_b):
    args = dict(locals())
    assert x.shape[0] == 1 and ffn1_w_gu.shape[0] == 1
    T, D = x.shape[1], x.shape[2]
    F = ffn1_w_down.shape[1] * N_CHIPS
    W = sgu_ln_g.shape[1]
    KW = conv_w_dw.shape[1]
    assert KW - 1 <= HALO and T % SGU_BLOCK == 0

    mx, my, mc = lax.axis_index("x"), lax.axis_index("y"), lax.axis_index("c")
    q = 2 * mx + my
    qc = jnp.stack([q, mc]).astype(jnp.int32)

    big_names = [("ffn1_w_gu", 1), ("ffn1_w_down", 0), ("w_in", 1), ("w_a_proj", 1), ("w_b_proj", 1), ("w_out", 0),
                 ("ffn2_w_gu", 1), ("ffn2_w_down", 0)]
    B = {n: _Big(n, args[n][0], args["m_" + n][0], args["v_" + n][0], ax) for n, ax in big_names}
    own = {n: b.cast_into_full(qc) for n, b in B.items()}

    x2d = x[0]
    tgt = loss_target[0]
    tm_r = _pick(T, 256)
    tm_ln = _pick(T, 512)
    tm = _pick(T, 1024)
    tn = _pick128(D, 1024)
    nj = D // tn
    tng = _pick128(D, 512)
    njg = D // tng
    tnf = _pick128(F, 512)
    nf = F // tnf
    tnw = _pick128(W, 1024)
    tnd = _pick128(D, 512)
    tmw = _pick128(W, 512)
    SUB = 256

    def ffn_up(tag, xb_, wgu, carry=None):
        def epi(accs, ex):
            g, u = accs
            s = jax.nn.sigmoid(g)
            sg = g * s
            return [u * (s * (1.0 + g * (1.0 - s))), sg, sg * u]

        return _mm(f"{tag}_up", [(xb_, wgu, 0, 0, 0), (xb_, wgu, 0, 0, nf)], M=T, N=F, tm=tm, tn=tnf, tk=D, nk=1, epilogue=epi,
                   outs=[(F, BF16, 0)] * 3, sum_pairs=False, carry=carry, sub=SUB)

    def ffn_down(tag, act, wd, carry=None):
        return _mm(f"{tag}_down", [(act, wd, 0, 0, 0)], M=T, N=D, tm=tm, tn=_pick128(D, 512), tk=F, nk=1, epilogue=_ident,
                   outs=[(D, F32, 0)], carry=carry)

    def ffn_dact(tag, drh, wd, dgate_f, dup_f, carry=None):
        def epi(accs, ex):
            da = accs[0]
            return [da * ex[0].astype(F32), da * ex[1].astype(F32)]

        return _mm(f"{tag}_dact", [(drh, wd, 0, 0, 0)], tb=True, M=T, N=F, tm=_pick(T, 2048), tn=tnf, tk=D, nk=1, epilogue=epi,
                   outs=[(F, BF16, 0)] * 2, extras=[(dgate_f, "mn", 0), (dup_f, "mn", 0)], carry=carry, sub=SUB)

    def ffn_dwdown(tag, act, drh, carry=None):
        return _wgrad(f"{tag}_dwdown", act, drh, M=F, N=D, T=T, tm=tnf, tn=tnd, tk=T, carry=carry)

    def ffn_dwgate(tag, xb_, dg, carry=None):
        return _wgrad(f"{tag}_dwgate", xb_, dg, M=D, N=F, T=T, tm=tnd, tn=tnf, tk=T, ncols=2 * F, carry=carry)

    def ffn_dwup(tag, xb_, du, into, carry=None):
        return _wgrad(f"{tag}_dwup", xb_, du, M=D, N=F, T=T, tm=tnd, tn=tnf, tk=T, ncols=2 * F, into=into, joff=nf, carry=carry)

    def ffn_dx(tag, which, da, wgu, addends, carry=None):
        def epi(accs, ex):
            tot = accs[0] + ALPHA * ex[0]
            for e in ex[1:]:
                tot = tot + e
            return [tot]

        return _mm(f"{tag}_dx_{which}", [(da, wgu, 0, 1 if which == "up" else 0, 0)], tb=True, M=T, N=D, tm=tm,
                   tn=_pick128(D, 256), tk=F, nk=1, epilogue=epi if addends else _ident, outs=[(D, F32, 0)],
                   extras=[(a, "mn", 0) for a in addends], carry=carry)

    wdw_pad = jnp.pad(conv_w_dw[0], ((0, HALO - KW), (0, 0)))
    c0, un = _merge(B["ffn1_w_gu"].gather_ici(own["ffn1_w_gu"]), _conv_w_allgather(wdw_pad, conv_w_dw.shape[2]))
    (xb,), co = _cast_bf16("cast_x", x2d, tm_r, carry=c0)
    (wgu1,), (wdw_full,) = un(co)
    (wgu1,) = _exchange("gather_d2d_ffn1_w_gu", B["ffn1_w_gu"].gather_d2d(wgu1))

    c, un = _merge(B["ffn1_w_down"].gather_ici(own["ffn1_w_down"]), B["w_in"].gather_ici(own["w_in"]))
    (g1, u1, a1), co = ffn_up("ffn1", xb, wgu1, carry=c)
    (wd1,), (win,) = un(co)
    (wd1,) = _exchange("gather_d2d_ffn1_w_down", B["ffn1_w_down"].gather_d2d(wd1))
    b_gu2, b_d2 = B["ffn2_w_gu"], B["ffn2_w_down"]
    c, un = _merge(B["w_a_proj"].gather_ici(own["w_a_proj"]), B["w_b_proj"].gather_ici(own["w_b_proj"]),
                   B["w_out"].gather_ici(own["w_out"]), B["w_in"].gather_d2d(win), b_gu2.gather_ici(own["ffn2_w_gu"], (0, 4)))
    (fo1,), co = ffn_down("ffn1", a1, wd1, carry=c)
    (wa,), (wb,), (wout,), (win,), (wgu2,) = un(co)
    c, un = _merge(B["w_a_proj"].gather_d2d(wa), B["w_b_proj"].gather_d2d(wb), B["w_out"].gather_d2d(wout),
                   b_gu2.gather_ici(wgu2, (1, 4)))
    (x1b, xh1, rs1), co = _ln_fwd("ln1", x2d, fo1, ln1_g, ln1_b, 0.5, tm_ln, carry=c)
    (wa,), (wb,), (wout,), (wgu2,) = un(co)

    c, un = _merge(b_gu2.gather_ici(wgu2, (1, 2)), b_d2.gather_ici(own["ffn2_w_down"], (0, 2)))
    (proj,), co = _mm("in_proj", [(x1b, win, 0, 0, 0)], M=T, N=4 * D, tm=tm, tn=tn, tk=D, nk=1,
                      epilogue=lambda accs, ex: [accs[0] + ex[0]], outs=[(4 * D, F32, 0)], extras=[(b_in, "n", 0)], carry=c)
    (wgu2,), (wd2,) = un(co)
    wm = sgu_w_s[0]
    bst = sgu_b_s[0].T
    sa, z = _sgu_fwd("sgu_fwd", proj, sgu_ln_g, sgu_ln_b, wm, bst, tm_r)
    (zc, sb), (wd2,) = _conv_fwd("conv_fwd", z, wdw_full, KW, conv_b_dw, conv_ln_g, conv_ln_b, tm_r,
                                 carry=b_d2.gather_ici(wd2, (1, 2)))

    def epi_mix(accs, ex):
        ya_, yb_ = accs
        ga, gb = jax.nn.sigmoid(ex[0]), jax.nn.sigmoid(ex[1])
        return [ga * ya_ + gb * yb_, ga, gb, ya_ * (ga * (1.0 - ga)), yb_ * (gb * (1.0 - gb))]

    mixin, gate_a, gate_b, dlog_a, dlog_b = _mm(
        "branch_proj", [(sa, wa, 0, 0, 0), (sb, wb, 0, 0, 0)], M=T, N=D, tm=tm, tn=tng, tk=W, nk=1, epilogue=epi_mix,
        outs=[(D, BF16, 0)] * 5, sum_pairs=False, sub=SUB, extras=[(proj, "mn", 2 * njg), (proj, "mn", 3 * njg)])
    c, un = _merge(b_gu2.gather_d2d(wgu2), b_d2.gather_d2d(wd2))
    (mix,), co = _mm("out_proj", [(mixin, wout, 0, 0, 0)], M=T, N=D, tm=tm, tn=tn, tk=D, nk=1, epilogue=_ident,
                     outs=[(D, F32, 0)], carry=c)
    (wgu2,), (wd2,) = un(co)
    x2b, xh2, rs2 = _ln_fwd("ln2", xh1, mix, ln2_g, ln2_b, 1.0, tm_ln, prev=(ln1_g, ln1_b))
    g2, u2, a2 = ffn_up("ffn2", x2b, wgu2)
    (fo2,) = ffn_down("ffn2", a2, wd2)
    dr3, dr3h, loss_part, dln3_g, dln3_b = _ln3_loss("ln3_loss", xh2, fo2, ln3_g, ln3_b, tgt, tm_r, (ln2_g, ln2_b))

    b_gu2, b_d2 = B["ffn2_w_gu"], B["ffn2_w_down"]
    dg2, du2 = ffn_dact("ffn2", dr3h, wd2, g2, u2)
    dwd2 = ffn_dwdown("ffn2", a2, dr3h)
    dwgu2, (r1_d2,) = ffn_dwgate("ffn2", x2b, dg2, carry=b_d2.rs_pair(dwd2[1]))
    dwgu2 = ffn_dwup("ffn2", x2b, du2, dwgu2)
    cs_d2 = b_d2.rs_pairsum("ffn2_w_down", dwd2[0], r1_d2, qc)
    c, un = _merge(b_gu2.rs_pair(dwgu2[1]), b_d2.rs_ici(cs_d2))
    (dx2_gate,), co = ffn_dx("ffn2", "gate", dg2, wgu2, [], carry=c)
    (r1_gu2,), (r2_d2,) = un(co)
    (dx2,) = ffn_dx("ffn2", "up", du2, wgu2, [dr3, dx2_gate])
    cs_gu2 = b_gu2.rs_pairsum("ffn2_w_gu", dwgu2[0], r1_gu2, qc)
    gh_d2 = b_d2.rs_final("ffn2_w_down", dwd2[0], r1_d2, r2_d2, qc)

    dr2, dr2b, dln2_g, dln2_b = _ln_bwd("ln2_bwd", dx2, xh2, rs2, ln2_g, 1.0, tm_ln)

    def epi_dmix(accs, ex):
        tiles = [accs[0] * e.astype(F32) for e in ex]
        sums = [t.reshape(-1, SUBLANES, t.shape[1]).sum(axis=0) for t in tiles[2:]]
        return tiles + sums

    dya, dyb, dla, dlb, sla, slb = _mm(
        "out_proj_bwd", [(dr2b, wout, 0, 0, 0)], tb=True, M=T, N=D, tm=tm, tn=tng, tk=D, nk=1, epilogue=epi_dmix,
        outs=[(D, BF16, 0)] * 4 + [(D, F32, 0, SUBLANES)] * 2, sub=SUB,
        extras=[(gate_a, "mn", 0), (gate_b, "mn", 0), (dlog_a, "mn", 0), (dlog_b, "mn", 0)])
    dwout = _wgrad("dw_out", mixin, dr2b, M=D, N=D, T=T, tm=tnd, tn=tnd, tk=T)
    (dsa,) = _mm("a_proj_bwd", [(dya, wa, 0, 0, 0)], tb=True, M=T, N=W, tm=tm, tn=tnw, tk=D, nk=1, epilogue=_ident,
                 outs=[(W, F32, 0)])
    (dsb,) = _mm("b_proj_bwd", [(dyb, wb, 0, 0, 0)], tb=True, M=T, N=W, tm=tm, tn=tnw, tk=D, nk=1, epilogue=_ident,
                 outs=[(W, F32, 0)])
    dwa = _wgrad("dw_a_proj", sa, dya, M=W, N=D, T=T, tm=tmw, tn=tnd, tk=T)
    dwb = _wgrad("dw_b_proj", sb, dyb, M=W, N=D, T=T, tm=tmw, tn=tnd, tk=T)

    dpa, dwm, dbst, dsgu_g, dsgu_b, dbin_a = _sgu_bwd("sgu_bwd", proj, dsa, sgu_ln_g, sgu_ln_b, wm, bst, tm_r)
    dpb, dwdw, dbin_b, dcln_g, dcln_b, dbdw = _conv_bwd("conv_bwd", dsb, zc, z, proj, wdw_full, KW, conv_ln_g, conv_ln_b, tm_r)

    dps = [dpa, dpb, dla, dlb]
    db_in = jnp.concatenate([dbin_a, dbin_b, _colsum_rows("db_in_gate_a", sla, sla.shape[0]),
                             _colsum_rows("db_in_gate_b", slb, slb.shape[0])], axis=1)
    c, un = _merge(b_gu2.rs_ici(cs_gu2), b_d2.rs_share(gh_d2))
    (dx1,), co = _mm("in_proj_bwd", [(dp, win, 0, k, 0) for k, dp in enumerate(dps)], tb=True, M=T, N=D, tm=tm,
                     tn=_pick128(D, 256), tk=D, nk=1, epilogue=lambda accs, ex: [accs[0] + ALPHA * ex[0]], outs=[(D, F32, 0)],
                     extras=[(dr2, "mn", 0)], carry=c)
    (r2_gu2,), (g_d2,) = un(co)
    gh_gu2 = b_gu2.rs_final("ffn2_w_gu", dwgu2[0], r1_gu2, r2_gu2, qc)
    dwin = _wgrad("dw_in_0", x1b, dps[0], M=D, N=D, T=T, tm=tnd, tn=tnd, tk=T, ncols=4 * D)
    for k in range(1, 4):
        dwin = _wgrad(f"dw_in_{k}", x1b, dps[k], M=D, N=D, T=T, tm=tnd, tn=tnd, tk=T, ncols=4 * D, into=dwin,
                      joff=k * (D // tnd))

    mix_names = ["w_in", "w_a_proj", "w_b_proj", "w_out"]
    mix_grads = dict(zip(mix_names, [dwin, dwa, dwb, dwout], strict=True))
    c, un = _merge(*[B[n].rs_pair(mix_grads[n][1]) for n in mix_names])
    (dr1, dr1h, dln1_g, dln1_b), co = _ln_bwd("ln1_bwd", dx1, xh1, rs1, ln1_g, 0.5, tm_ln, carry=c)
    r1_mix = {n: r1 for n, (r1,) in zip(mix_names, un(co), strict=True)}
    cs_mix = {n: B[n].rs_pairsum(n, mix_grads[n][0], r1_mix[n], qc) for n in mix_names}

    small_names = ["ln1_g", "ln1_b", "b_in", "sgu_ln_g", "sgu_ln_b", "sgu_w_s", "sgu_b_s", "conv_w_dw", "conv_b_dw",
                   "conv_ln_g", "conv_ln_b", "ln2_g", "ln2_b", "ln3_g", "ln3_b"]
    small_parts = {"ln1_g": dln1_g, "ln1_b": dln1_b, "b_in": db_in, "sgu_ln_g": dsgu_g, "sgu_ln_b": dsgu_b, "sgu_w_s": dwm,
                   "sgu_b_s": dbst.T, "conv_w_dw": dwdw[:KW], "conv_b_dw": dbdw, "conv_ln_g": dcln_g, "conv_ln_b": dcln_b,
                   "ln2_g": dln2_g, "ln2_b": dln2_b, "ln3_g": dln3_g, "ln3_b": dln3_b}
    packed, spans = _pack_rows([small_parts[n] for n in small_names])

    b_gu1, b_d1 = B["ffn1_w_gu"], B["ffn1_w_down"]
    c, un = _merge(B["w_in"].rs_ici(cs_mix["w_in"]), _small_allgather(packed), b_gu2.rs_share(gh_gu2))
    (dg1, du1), co = ffn_dact("ffn1", dr1h, wd1, g1, u1, carry=c)
    (r2_win,), (small_slots,), (g_gu2,) = un(co)
    c, un = _merge(*[B[n].rs_ici(cs_mix[n]) for n in mix_names[1:]])
    dwgu1, co = ffn_dwgate("ffn1", xb, dg1, carry=c)
    r2_mix = [[r2_win]] + un(co)
    gh_mix = [B[n].rs_final(n, mix_grads[n][0], r1_mix[n], r2, qc) for n, (r2,) in zip(mix_names, r2_mix, strict=True)]
    dwgu1 = ffn_dwup("ffn1", xb, du1, dwgu1)
    dwd1, (r1_gu1,) = ffn_dwdown("ffn1", a1, dr1h, carry=b_gu1.rs_pair(dwgu1[1]))
    cs_gu1 = b_gu1.rs_pairsum("ffn1_w_gu", dwgu1[0], r1_gu1, qc)
    c, un = _merge(b_gu1.rs_ici(cs_gu1), b_d1.rs_pair(dwd1[1]),
                   *[B[n].rs_share(gh) for n, gh in zip(mix_names, gh_mix, strict=True)])
    (dx_gate,), co = ffn_dx("ffn1", "gate", dg1, wgu1, [], carry=c)
    (r2_gu1,), (r1_d1,), *g_mixs = un(co)
    g_mix = {n: g for n, (g,) in zip(mix_names, g_mixs, strict=True)}
    cs_d1 = b_d1.rs_pairsum("ffn1_w_down", dwd1[0], r1_d1, qc)
    gh_gu1 = b_gu1.rs_final("ffn1_w_gu", dwgu1[0], r1_gu1, r2_gu1, qc)
    c, un = _merge(b_d1.rs_ici(cs_d1), b_gu1.rs_share(gh_gu1))
    (dx,), co = ffn_dx("ffn1", "up", du1, wgu1, [dr1, dx_gate], carry=c)
    (r2_d1,), (g_gu1,) = un(co)
    gh_d1 = b_d1.rs_final("ffn1_w_down", dwd1[0], r1_d1, r2_d1, qc)

    grads = {"ffn1_w_gu": g_gu1, "ffn2_w_gu": g_gu2, "ffn2_w_down": g_d2, **g_mix}
    outs_g, outs_d, outs_m, outs_v = {}, {}, {}, {}

    def adamw_big(n, g, carry=None):
        b = B[n]
        return _adamw(f"adamw_{n}", b.w, g, b.m, b.v, _tile_rows(b.sr, b.sc), carry=carry)

    upd = {}
    upd["w_a_proj"], (grads["ffn1_w_down"],) = adamw_big("w_a_proj", grads["w_a_proj"], carry=b_d1.rs_share(gh_d1))
    for n, _ in big_names:
        if n not in upd:
            upd[n] = adamw_big(n, grads[n])
        g_, d_, m_, v_ = upd[n]
        outs_g[n], outs_d[n], outs_m[n], outs_v[n] = g_[None], d_[None], m_[None], v_[None]
    gsum = _sum8("small_sum", small_slots)
    full_shapes = [args[n].shape if n != "conv_w_dw" else (1, KW, W) for n in small_names]
    gsmall = dict(zip(small_names, _unpack_rows(gsum, spans, full_shapes), strict=True))
    cs = conv_w_dw.shape[2]
    gsmall["conv_w_dw"] = lax.dynamic_slice_in_dim(gsmall["conv_w_dw"], q * cs, cs, axis=2)
    pw, spans2 = _pack_rows([args[n] for n in small_names])
    pg, _ = _pack_rows([gsmall[n] for n in small_names])
    pm, _ = _pack_rows([args["m_" + n] for n in small_names])
    pv, _ = _pack_rows([args["v_" + n] for n in small_names])
    _, pd, pmn, pvn = _adamw("adamw_small", pw, pg, pm, pv, pw.shape[0])
    shapes2 = [args[n].shape for n in small_names]
    for dst, src in ((outs_d, pd), (outs_m, pmn), (outs_v, pvn)):
        dst.update(zip(small_names, _unpack_rows(src, spans2, shapes2), strict=True))
    outs_g.update(gsmall)

    loss = lax.psum(loss_part[0, 0], ("x", "y", "c"))
    order = ["ffn1_w_gu", "ffn1_w_down", "ln1_g", "ln1_b", "w_in", "b_in", "sgu_ln_g", "sgu_ln_b", "sgu_w_s", "sgu_b_s",
             "w_a_proj", "conv_w_dw", "conv_b_dw", "conv_ln_g", "conv_ln_b", "w_b_proj", "w_out", "ln2_g", "ln2_b",
             "ffn2_w_gu", "ffn2_w_down", "ln3_g", "ln3_b"]
    return (loss, dx[None], *[outs_g[n] for n in order], *[outs_d[n] for n in order], *[outs_m[n] for n in order],
            *[outs_v[n] for n in order])
```

```python
import math

import jax
import jax.numpy as jnp
from jax import lax
from jax.experimental import pallas as pl
from jax.experimental.pallas import tpu as pltpu

BF16 = jnp.bfloat16
F32 = jnp.float32

LN_EPS = 1e-5
ALPHA = 2.0 ** 0.25
SGU_BLOCK = 128
SGU_CHUNK = 64
HALO = 32
SUBLANES = 8
LANES = 128
CONV_ROWS = 32
ADAM_LR = 0.001
ADAM_B1 = 0.9
ADAM_B2 = 0.999
ADAM_EPS = 1e-08
ADAM_WD = 0.01
ADAM_STEP = 10
N_CHIPS = 4
VMEM_LIMIT = 52 * 1024 * 1024
MESH = pl.DeviceIdType.MESH

_GELU_C0 = math.sqrt(2.0 / math.pi)
_GELU_C1 = 0.044715


def _cparams(sem):
    return pltpu.CompilerParams(dimension_semantics=sem, vmem_limit_bytes=VMEM_LIMIT)


def _gelu_parts(x):
    x2 = x * x
    t = jnp.tanh(_GELU_C0 * (x + _GELU_C1 * (x2 * x)))
    return 0.5 * (1.0 + t), t, x2


def _gelu(x):
    cdf, _, _ = _gelu_parts(x)
    return x * cdf


def _gelu_and_grad(x):
    cdf, t, x2 = _gelu_parts(x)
    grad = cdf + x * (0.5 * (1.0 - t * t)) * (_GELU_C0 * (1.0 + (3.0 * _GELU_C1) * x2))
    return x * cdf, grad


def _silu_grad(x):
    s = jax.nn.sigmoid(x)
    return s * (1.0 + x * (1.0 - s))


def _row_stats(x):
    mu = jnp.mean(x, axis=-1, keepdims=True)
    xc = x - mu
    var = jnp.mean(xc * xc, axis=-1, keepdims=True)
    rstd = lax.rsqrt(var + LN_EPS)
    return xc * rstd, rstd


def _ln_bwd_rows(dy, xhat, rstd, g):
    dxh = dy * g
    m1 = jnp.mean(dxh, axis=-1, keepdims=True)
    m2 = jnp.mean(dxh * xhat, axis=-1, keepdims=True)
    return rstd * (dxh - m1 - xhat * m2)


def _colsum(v):
    return jnp.sum(v, axis=0, keepdims=True)


class _Carry:
    def __init__(self, ro, rw, new, plan, n_remote, n_local=0):
        self.ro, self.rw, self.new, self.plan = list(ro), list(rw), list(new), plan
        self.n_remote, self.n_local = n_remote, n_local

    def sems(self):
        return [pltpu.SemaphoreType.DMA((self.n_remote,)), pltpu.SemaphoreType.DMA((self.n_remote,)),
                pltpu.SemaphoreType.DMA((max(self.n_local, 1),))]

    def copies(self, ro_refs, rw_refs, new_refs, send_sems, recv_sems, loc_sems):
        x, y, c = lax.axis_index("x"), lax.axis_index("y"), lax.axis_index("c")
        remote, local = self.plan(ro_refs, rw_refs, new_refs, x, y, c)
        assert len(remote) == self.n_remote and len(local) == self.n_local
        lcs = [pltpu.make_async_copy(s, d, loc_sems.at[k]) for k, (s, d) in enumerate(local)]
        rcs = [pltpu.make_async_remote_copy(src_ref=s, dst_ref=d, send_sem=send_sems.at[k], recv_sem=recv_sems.at[k],
                                            device_id=dev, device_id_type=MESH) for k, (s, d, dev) in enumerate(remote)]
        return lcs, rcs

    def out_shape(self):
        return [jax.ShapeDtypeStruct(a.shape, a.dtype) for a in self.rw] + self.new


def _start_all(lcs, rcs):
    for cp in lcs + rcs:
        cp.start()


def _wait_all(lcs, rcs):
    for cp in rcs:
        cp.wait_send()
    for cp in rcs:
        cp.wait_recv()
    for cp in lcs:
        cp.wait()


def _merge(*cs):
    ro = [a for c in cs for a in c.ro]
    rw = [a for c in cs for a in c.rw]
    new = [a for c in cs for a in c.new]

    def plan(ro_refs, rw_refs, new_refs, x, y, c):
        remote, local, a, b, d = [], [], 0, 0, 0
        for cc in cs:
            r, l = cc.plan(ro_refs[a:a + len(cc.ro)], rw_refs[b:b + len(cc.rw)], new_refs[d:d + len(cc.new)], x, y, c)
            a, b, d = a + len(cc.ro), b + len(cc.rw), d + len(cc.new)
            remote += r
            local += l
        return remote, local

    def unpack(couts):
        res, b, d = [], 0, len(rw)
        for cc in cs:
            res.append(list(couts[b:b + len(cc.rw)]) + list(couts[d:d + len(cc.new)]))
            b, d = b + len(cc.rw), d + len(cc.new)
        return res

    return _Carry(ro, rw, new, plan, sum(c.n_remote for c in cs), sum(c.n_local for c in cs)), unpack


_ANY = pl.BlockSpec(memory_space=pl.ANY)


def _exchange(name, carry):
    n_ro, n_rw, n_new = len(carry.ro), len(carry.rw), len(carry.new)

    def body(*refs):
        o0 = n_ro + n_rw
        lcs, rcs = carry.copies(refs[:n_ro], refs[o0:o0 + n_rw], refs[o0 + n_rw:o0 + n_rw + n_new], *refs[o0 + n_rw + n_new:])
        _start_all(lcs, rcs)
        _wait_all(lcs, rcs)

    return list(pl.pallas_call(
        body, name=name, in_specs=[_ANY] * (n_ro + n_rw), out_specs=[_ANY] * (n_rw + n_new), out_shape=carry.out_shape(),
        input_output_aliases={n_ro + k: k for k in range(n_rw)}, scratch_shapes=carry.sems())(*carry.ro, *carry.rw))


def _call(body, *, name, grid, in_specs, out_specs, out_shape, args, scratch=(), sem, carry=None, aliases=None):
    in_specs, out_specs, out_shape, scratch = list(in_specs), list(out_specs), list(out_shape), list(scratch)
    if carry is None:
        return list(pl.pallas_call(body, name=name, grid=grid, in_specs=in_specs, out_specs=out_specs, out_shape=out_shape,
                                   scratch_shapes=scratch, input_output_aliases=aliases or {},
                                   compiler_params=_cparams(sem))(*args))
    n_in, n_out, n_scr = len(in_specs), len(out_specs), len(scratch)
    n_ro, n_rw, n_new = len(carry.ro), len(carry.rw), len(carry.new)

    def wrapped(*refs):
        ins = refs[:n_in]
        ro_refs = refs[n_in:n_in + n_ro]
        o0 = n_in + n_ro + n_rw
        outs = refs[o0:o0 + n_out]
        rw_refs = refs[o0 + n_out:o0 + n_out + n_rw]
        new_refs = refs[o0 + n_out + n_rw:o0 + n_out + n_rw + n_new]
        s0 = o0 + n_out + n_rw + n_new
        scr = refs[s0:s0 + n_scr]
        sems = refs[s0 + n_scr:]
        first = pl.program_id(0) == 0
        last = pl.program_id(0) == grid[0] - 1
        for d in range(1, len(grid)):
            first = jnp.logical_and(first, pl.program_id(d) == 0)
            last = jnp.logical_and(last, pl.program_id(d) == grid[d] - 1)

        @pl.when(first)
        def _():
            _start_all(*carry.copies(ro_refs, rw_refs, new_refs, *sems))

        body(*ins, *outs, *scr)

        @pl.when(last)
        def _():
            _wait_all(*carry.copies(ro_refs, rw_refs, new_refs, *sems))

    al = dict(aliases or {})
    al.update({n_in + n_ro + k: n_out + k for k in range(n_rw)})
    res = pl.pallas_call(
        wrapped, name=name, grid=grid, in_specs=in_specs + [_ANY] * (n_ro + n_rw), out_specs=out_specs + [_ANY] * (n_rw + n_new),
        out_shape=out_shape + carry.out_shape(), scratch_shapes=scratch + carry.sems(), input_output_aliases=al,
        compiler_params=_cparams(("arbitrary",) * len(grid)))(*args, *carry.ro, *carry.rw)
    return list(res[:n_out]), list(res[n_out:])


def _mm(name, pairs, *, ta=False, tb=False, M, N, tm, tn, tk, nk, epilogue, outs, extras=(), sum_pairs=True, carry=None,
        sub=None):
    ni, nj = M // tm, N // tn
    assert ni * tm == M and nj * tn == N
    n_p = len(pairs)
    n_acc = 1 if sum_pairs else n_p
    in_specs, args = [], []
    for a, b, ak, bk, bj in pairs:
        if ta:
            in_specs.append(pl.BlockSpec((tk, tm), lambda i, j, k, ak=ak: (k + ak, i)))
        else:
            in_specs.append(pl.BlockSpec((tm, tk), lambda i, j, k, ak=ak: (i, k + ak)))
        if tb:
            in_specs.append(pl.BlockSpec((tn, tk), lambda i, j, k, bk=bk, bj=bj: (j + bj, k + bk)))
        else:
            in_specs.append(pl.BlockSpec((tk, tn), lambda i, j, k, bk=bk, bj=bj: (k + bk, j + bj)))
        args += [a, b]
    for arr, kind, jo in extras:
        if kind == "mn":
            in_specs.append(pl.BlockSpec((tm, tn), lambda i, j, k, jo=jo: (i, j + jo)))
        else:
            in_specs.append(pl.BlockSpec((1, tn), lambda i, j, k, jo=jo: (0, j + jo)))
        args.append(arr)
    out_rows = [o[3] if len(o) > 3 else tm for o in outs]
    out_specs = [pl.BlockSpec((r, tn), lambda i, j, k, jo=o[2]: (i, j + jo)) for o, r in zip(outs, out_rows, strict=True)]
    out_shape = [jax.ShapeDtypeStruct((ni * r, o[0]), o[1]) for o, r in zip(outs, out_rows, strict=True)]
    n_ex, n_out = len(extras), len(outs)
    dn = (((0 if ta else 1,), (1 if tb else 0,)), ((), ()))

    def body(*refs):
        ab = refs[: 2 * n_p]
        ex = refs[2 * n_p: 2 * n_p + n_ex]
        o0 = 2 * n_p + n_ex
        out_refs = refs[o0: o0 + n_out]
        acc_refs = refs[o0 + n_out:]

        def dots():
            res = []
            for p in range(n_p):
                a = ab[2 * p][...]
                b = ab[2 * p + 1][...]
                res.append(lax.dot_general(a.astype(BF16), b.astype(BF16), dn, preferred_element_type=F32))
            if sum_pairs:
                tot = res[0]
                for r in res[1:]:
                    tot = tot + r
                res = [tot]
            return res

        def finish(accs):
            tiles = epilogue(accs, [e[...] for e in ex])
            for r, t in zip(out_refs, tiles, strict=True):
                r[...] = t.astype(r.dtype)

        if nk == 1 and sub is not None and tn > sub:
            for s in range(tn // sub):
                cs = slice(s * sub, (s + 1) * sub)
                res = []
                for p in range(n_p):
                    b = ab[2 * p + 1][cs, :] if tb else ab[2 * p + 1][:, cs]
                    res.append(lax.dot_general(ab[2 * p][...].astype(BF16), b.astype(BF16), dn, preferred_element_type=F32))
                if sum_pairs:
                    tot = res[0]
                    for r in res[1:]:
                        tot = tot + r
                    res = [tot]
                tiles = epilogue(res, [e[:, cs] for e in ex])
                for r, t in zip(out_refs, tiles, strict=True):
                    r[:, cs] = t.astype(r.dtype)
        elif nk == 1:
            finish(dots())
        else:
            k = pl.program_id(2)
            d = dots()

            @pl.when(k == 0)
            def _():
                for r, v in zip(acc_refs, d, strict=True):
                    r[...] = v

            @pl.when(k > 0)
            def _():
                for r, v in zip(acc_refs, d, strict=True):
                    r[...] += v

            @pl.when(k == nk - 1)
            def _():
                finish([r[...] for r in acc_refs])

    scratch = [pltpu.VMEM((tm, tn), F32) for _ in range(n_acc)] if nk > 1 else []
    return _call(body, name=name, grid=(ni, nj, nk), in_specs=in_specs, out_specs=out_specs, out_shape=out_shape, args=args,
                 scratch=scratch, sem=("parallel", "parallel", "arbitrary"), carry=carry)


def _wgrad(name, a, b, *, M, N, T, tm, tn, tk, into=None, joff=0, ncols=None, carry=None):
    ncols = N if ncols is None else ncols
    ni, nj, nk = M // tm, N // tn, T // tk
    dn = (((0,), (0,)), ((), ()))

    def body(a_ref, b_ref, *rest):
        d = lax.dot_general(a_ref[...].astype(BF16), b_ref[...].astype(BF16), dn, preferred_element_type=F32)
        if nk == 1:
            of_ref, oh_ref = rest[-2:]
            of_ref[...] = d
            oh_ref[...] = d.astype(BF16)
            return
        of_ref, oh_ref, acc_ref = rest[-3:]
        k = pl.program_id(2)

        @pl.when(k == 0)
        def _():
            acc_ref[...] = d

        @pl.when(k > 0)
        def _():
            acc_ref[...] += d

        @pl.when(k == nk - 1)
        def _():
            of_ref[...] = acc_ref[...]
            oh_ref[...] = acc_ref[...].astype(BF16)

    ospec = pl.BlockSpec((tm, tn), lambda i, j, k: (i, j + joff))
    in_specs = [pl.BlockSpec((tk, tm), lambda i, j, k: (k, i)), pl.BlockSpec((tk, tn), lambda i, j, k: (k, j))]
    args, aliases = [a, b], None
    if into is not None:
        in_specs += [_ANY, _ANY]
        args += list(into)
        aliases = {2: 0, 3: 1}
    return _call(body, name=name, grid=(ni, nj, nk), in_specs=in_specs, out_specs=[ospec, ospec],
                 out_shape=[jax.ShapeDtypeStruct((M, ncols), F32), jax.ShapeDtypeStruct((M, ncols), BF16)], args=args,
                 scratch=[pltpu.VMEM((tm, tn), F32)] if nk > 1 else [], sem=("parallel", "parallel", "arbitrary"), carry=carry,
                 aliases=aliases)


def _rows(tm, c, cb=0):
    return pl.BlockSpec((tm, c), lambda i, cb=cb: (i, cb))


def _whole(shape):
    nd = len(shape)
    return pl.BlockSpec(shape, lambda i, nd=nd: (0,) * nd)


def _cast_bf16(name, x, tm, carry=None):
    t, d = x.shape

    def body(x_ref, o_ref):
        o_ref[...] = x_ref[...].astype(BF16)

    return _call(body, name=name, grid=(t // tm,), in_specs=[_rows(tm, d)], out_specs=[_rows(tm, d)],
                 out_shape=[jax.ShapeDtypeStruct((t, d), BF16)], args=[x], sem=("parallel",), carry=carry)


def _residual(x_ref, prev_refs):
    if not prev_refs:
        return x_ref[...]
    return x_ref[...] * prev_refs[0][...] + prev_refs[1][...]


def _ln_fwd(name, xres, f, g, b, cf, tm, carry=None, prev=()):
    t, d = xres.shape
    n_prev = len(prev)

    def body(x_ref, f_ref, g_ref, b_ref, *rest):
        yb_ref, xh_ref, rs_ref = rest[n_prev:]
        r = ALPHA * _residual(x_ref, rest[:n_prev]) + cf * f_ref[...]
        xhat, rstd = _row_stats(r)
        yb_ref[...] = (xhat * g_ref[...] + b_ref[...]).astype(BF16)
        xh_ref[...] = xhat
        rs_ref[...] = rstd

    return _call(
        body, name=name, grid=(t // tm,),
        in_specs=[_rows(tm, d), _rows(tm, d), _whole((1, d)), _whole((1, d))] + [_whole((1, d))] * n_prev,
        out_specs=[_rows(tm, d), _rows(tm, d), _rows(tm, 1)],
        out_shape=[jax.ShapeDtypeStruct((t, d), BF16), jax.ShapeDtypeStruct((t, d), F32), jax.ShapeDtypeStruct((t, 1), F32)],
        args=[xres, f, g, b, *prev], sem=("parallel",), carry=carry)


def _ln_bwd(name, dy, xhat, rstd, g, scale, tm, carry=None):
    t, d = dy.shape

    def body(dy_ref, xh_ref, rs_ref, g_ref, dr_ref, drb_ref, dg_ref, db_ref):
        i = pl.program_id(0)
        dy_v, xh = dy_ref[...], xh_ref[...]
        dr = _ln_bwd_rows(dy_v, xh, rs_ref[...], g_ref[...])
        dr_ref[...] = dr
        drb_ref[...] = (scale * dr).astype(BF16)

        @pl.when(i == 0)
        def _():
            dg_ref[...] = jnp.zeros_like(dg_ref)
            db_ref[...] = jnp.zeros_like(db_ref)

        dg_ref[...] += _colsum(dy_v * xh)
        db_ref[...] += _colsum(dy_v)

    return _call(
        body, name=name, grid=(t // tm,), in_specs=[_rows(tm, d), _rows(tm, d), _rows(tm, 1), _whole((1, d))],
        out_specs=[_rows(tm, d), _rows(tm, d), _whole((1, d)), _whole((1, d))],
        out_shape=[jax.ShapeDtypeStruct((t, d), F32), jax.ShapeDtypeStruct((t, d), BF16),
                   jax.ShapeDtypeStruct((1, d), F32), jax.ShapeDtypeStruct((1, d), F32)],
        args=[dy, xhat, rstd, g], sem=("arbitrary",), carry=carry)


def _ln3_loss(name, xres, f, g, b, target, tm, prev):
    t, d = xres.shape

    def body(x_ref, f_ref, g_ref, b_ref, tg_ref, pg_ref, pb_ref, dr_ref, drb_ref, loss_ref, dg_ref, db_ref):
        i = pl.program_id(0)
        r = ALPHA * _residual(x_ref, (pg_ref, pb_ref)) + 0.5 * f_ref[...]
        xhat, rstd = _row_stats(r)
        gv = g_ref[...]
        y = xhat * gv + b_ref[...]
        err = y - tg_ref[...]
        dy = err * (1.0 / d)
        dr = _ln_bwd_rows(dy, xhat, rstd, gv)
        dr_ref[...] = dr
        drb_ref[...] = (0.5 * dr).astype(BF16)
        part = 0.5 * jnp.sum(jnp.mean(err * err, axis=-1, keepdims=True), axis=0, keepdims=True)

        @pl.when(i == 0)
        def _():
            loss_ref[...] = jnp.zeros_like(loss_ref)
            dg_ref[...] = jnp.zeros_like(dg_ref)
            db_ref[...] = jnp.zeros_like(db_ref)

        loss_ref[...] += jnp.broadcast_to(part, loss_ref.shape)
        dg_ref[...] += _colsum(dy * xhat)
        db_ref[...] += _colsum(dy)

    return _call(
        body, name=name, grid=(t // tm,),
        in_specs=[_rows(tm, d), _rows(tm, d), _whole((1, d)), _whole((1, d)), _rows(tm, d), _whole((1, d)), _whole((1, d))],
        out_specs=[_rows(tm, d), _rows(tm, d), _whole((8, 128)), _whole((1, d)), _whole((1, d))],
        out_shape=[jax.ShapeDtypeStruct((t, d), F32), jax.ShapeDtypeStruct((t, d), BF16),
                   jax.ShapeDtypeStruct((8, 128), F32), jax.ShapeDtypeStruct((1, d), F32),
                   jax.ShapeDtypeStruct((1, d), F32)],
        args=[xres, f, g, b, target, *prev], sem=("arbitrary",))


def _colsum_rows(name, x, tm):
    t, d = x.shape

    def body(x_ref, o_ref):
        @pl.when(pl.program_id(0) == 0)
        def _():
            o_ref[...] = jnp.zeros_like(o_ref)

        o_ref[...] += _colsum(x_ref[...].astype(F32))

    return _call(body, name=name, grid=(t // tm,), in_specs=[_rows(tm, d)], out_specs=[_whole((1, d))],
                 out_shape=[jax.ShapeDtypeStruct((1, d), F32)], args=[x], sem=("arbitrary",))[0]


def _sgu_mask():
    sh = SGU_CHUNK.bit_length() - 1
    r = lax.shift_right_logical(lax.broadcasted_iota(jnp.int32, (SGU_BLOCK, SGU_BLOCK), 0), sh)
    c = lax.shift_right_logical(lax.broadcasted_iota(jnp.int32, (SGU_BLOCK, SGU_BLOCK), 1), sh)
    return c <= r


def _sgu_fwd(name, p, lng, lnb, wm, bst, tm):
    t = p.shape[0]
    n_grp, w = wm.shape[0], lng.shape[1]
    hd = w // n_grp
    nblk = tm // SGU_BLOCK

    def body(uv_ref, h_ref, g_ref, b_ref, wm_ref, bs_ref, sa_ref, z_ref, vn_s):
        xhat, _ = _row_stats(_gelu(uv_ref[:, w:]))
        vn_s[...] = (xhat * g_ref[...] + b_ref[...]).astype(BF16)
        mask = _sgu_mask()
        for h in range(n_grp):
            wh = jnp.where(mask, wm_ref[h], 0.0).astype(BF16)
            bcol = bs_ref[:, h:h + 1]
            cs = slice(h * hd, (h + 1) * hd)
            for n in range(nblk):
                rs = slice(n * SGU_BLOCK, (n + 1) * SGU_BLOCK)
                s = jnp.dot(wh, vn_s[rs, cs], preferred_element_type=F32) + bcol
                sa_ref[rs, cs] = (_gelu(uv_ref[rs, cs]) * s).astype(BF16)
        z_ref[...] = h_ref[:, :w] * jax.nn.sigmoid(h_ref[:, w:])

    return _call(
        body, name=name, grid=(t // tm,),
        in_specs=[_rows(tm, 2 * w, 0), _rows(tm, 2 * w, 1), _whole((1, w)), _whole((1, w)), _whole(wm.shape),
                  _whole(bst.shape)],
        out_specs=[_rows(tm, w), _rows(tm, w)],
        out_shape=[jax.ShapeDtypeStruct((t, w), BF16), jax.ShapeDtypeStruct((t, w), F32)],
        args=[p, p, lng, lnb, wm, bst], scratch=[pltpu.VMEM((tm, w), BF16)], sem=("parallel",))


def _sgu_bwd(name, p, dsa, lng, lnb, wm, bst, tm):
    t = p.shape[0]
    n_grp, w = wm.shape[0], lng.shape[1]
    hd = w // n_grp
    nblk = tm // SGU_BLOCK

    def body(uv_ref, dsa_ref, g_ref, b_ref, wm_ref, bs_ref, dp_ref, dwm_ref, dbs_ref, dg_ref, db_ref, dbin_ref,
             vn_s, ug_s, dvn_s, dug_s):
        i = pl.program_id(0)

        @pl.when(i == 0)
        def _():
            dwm_ref[...] = jnp.zeros_like(dwm_ref)
            dbs_ref[...] = jnp.zeros_like(dbs_ref)
            dg_ref[...] = jnp.zeros_like(dg_ref)
            db_ref[...] = jnp.zeros_like(db_ref)
            dbin_ref[...] = jnp.zeros_like(dbin_ref)

        ug, dgelu_u = _gelu_and_grad(uv_ref[:, :w])
        ug_s[...] = ug
        vg, dgelu_v = _gelu_and_grad(uv_ref[:, w:])
        xhat, rstd = _row_stats(vg)
        gv = g_ref[...]
        vn_s[...] = (xhat * gv + b_ref[...]).astype(BF16)
        mask = _sgu_mask()
        for h in range(n_grp):
            wh = jnp.where(mask, wm_ref[h], 0.0).astype(BF16)
            bcol = bs_ref[:, h:h + 1]
            cs = slice(h * hd, (h + 1) * hd)
            dw_h = jnp.zeros((SGU_BLOCK, SGU_BLOCK), F32)
            dbs_h = jnp.zeros((SGU_BLOCK, 1), F32)
            for n in range(nblk):
                rs = slice(n * SGU_BLOCK, (n + 1) * SGU_BLOCK)
                vblk = vn_s[rs, cs]
                s = jnp.dot(wh, vblk, preferred_element_type=F32) + bcol
                dsa_blk = dsa_ref[rs, cs]
                dug_s[rs, cs] = dsa_blk * s
                ds = dsa_blk * ug_s[rs, cs]
                dsb = ds.astype(BF16)
                dvn_s[rs, cs] = lax.dot_general(wh, dsb, (((0,), (0,)), ((), ())), preferred_element_type=F32)
                dw_h = dw_h + lax.dot_general(dsb, vblk, (((1,), (1,)), ((), ())), preferred_element_type=F32)
                dbs_h = dbs_h + jnp.sum(ds, axis=1, keepdims=True)
            dwm_ref[h] += jnp.where(mask, dw_h, 0.0)
            dbs_ref[:, h:h + 1] += dbs_h
        dvn = dvn_s[...]
        dg_ref[...] += _colsum(dvn * xhat)
        db_ref[...] += _colsum(dvn)
        dvg = _ln_bwd_rows(dvn, xhat, rstd, gv)
        du = dug_s[...] * dgelu_u
        dv = dvg * dgelu_v
        dp_ref[:, :w] = du.astype(BF16)
        dp_ref[:, w:] = dv.astype(BF16)
        dbin_ref[:, :w] += _colsum(du)
        dbin_ref[:, w:] += _colsum(dv)

    return _call(
        body, name=name, grid=(t // tm,),
        in_specs=[_rows(tm, 2 * w, 0), _rows(tm, w), _whole((1, w)), _whole((1, w)), _whole(wm.shape), _whole(bst.shape)],
        out_specs=[_rows(tm, 2 * w), _whole(wm.shape), _whole(bst.shape), _whole((1, w)), _whole((1, w)), _whole((1, 2 * w))],
        out_shape=[jax.ShapeDtypeStruct((t, 2 * w), BF16), jax.ShapeDtypeStruct(wm.shape, F32),
                   jax.ShapeDtypeStruct(bst.shape, F32), jax.ShapeDtypeStruct((1, w), F32), jax.ShapeDtypeStruct((1, w), F32),
                   jax.ShapeDtypeStruct((1, 2 * w), F32)],
        args=[p, dsa, lng, lnb, wm, bst],
        scratch=[pltpu.VMEM((tm, w), BF16), pltpu.VMEM((tm, w), F32), pltpu.VMEM((tm, w), F32), pltpu.VMEM((tm, w), F32)],
        sem=("arbitrary",))


def _halo_prev(tm, c):
    return pl.BlockSpec((HALO, c), lambda i: (jnp.maximum(i * (tm // HALO) - 1, 0), 0))


def _halo_next(tm, c, t):
    last = t // HALO - 1
    return pl.BlockSpec((HALO, c), lambda i: (jnp.minimum((i + 1) * (tm // HALO), last), 0))


def _shifted_copies(sh, n):
    for r in range(1, SUBLANES):
        sh[r, :n - SUBLANES, :] = sh[0, r:r + n - SUBLANES, :]


def _row_broadcasts(wb, w_ref, kw):
    for k in range(kw):
        wb[k] = jnp.broadcast_to(w_ref[k:k + 1, :], wb.shape[1:])


def _tap(sh, r0, o, rows, cols):
    return sh[o % SUBLANES, pl.ds(pl.multiple_of(r0 + (o - o % SUBLANES), SUBLANES), rows), cols]


def _conv_fwd(name, z, wdw, kw, bdw, lng, lnb, tm, carry=None):
    t, c = z.shape
    lead = HALO - (kw - 1)
    n = tm + HALO

    def body(zp_ref, z_ref, w_ref, bdw_ref, g_ref, b_ref, zc_ref, sb_ref, sh, wb):
        i = pl.program_id(0)
        sh[0, :HALO, :] = jnp.where(i > 0, zp_ref[...], 0.0)
        sh[0, HALO:, :] = z_ref[...]
        _shifted_copies(sh, n)
        _row_broadcasts(wb, w_ref, kw)
        bias = jnp.broadcast_to(bdw_ref[...], (SUBLANES, c))
        groups = CONV_ROWS // SUBLANES

        def chunk(ci, _):
            r0 = pl.multiple_of(ci * CONV_ROWS, CONV_ROWS)
            accs = [bias] * groups
            for k in range(kw):
                wk = wb[k]
                tp = _tap(sh, r0, lead + k, CONV_ROWS, slice(None))
                accs = [accs[g] + wk * tp[g * SUBLANES:(g + 1) * SUBLANES] for g in range(groups)]
            zc_ref[pl.ds(r0, CONV_ROWS), :] = jnp.concatenate(accs, axis=0)
            return 0

        lax.fori_loop(0, tm // CONV_ROWS, chunk, 0)
        xhat, _ = _row_stats(zc_ref[...])
        zn = xhat * g_ref[...] + b_ref[...]
        sb_ref[...] = (zn * jax.nn.sigmoid(zn)).astype(BF16)

    return _call(
        body, name=name, grid=(t // tm,),
        in_specs=[_halo_prev(tm, c), _rows(tm, c), _whole(wdw.shape), _whole((1, c)), _whole((1, c)), _whole((1, c))],
        out_specs=[_rows(tm, c), _rows(tm, c)],
        out_shape=[jax.ShapeDtypeStruct((t, c), F32), jax.ShapeDtypeStruct((t, c), BF16)],
        args=[z, z, wdw, bdw, lng, lnb], scratch=[pltpu.VMEM((SUBLANES, n, c), F32), pltpu.VMEM((HALO, SUBLANES, c), F32)],
        sem=("parallel",), carry=carry)


def _conv_ln_bwd(name, dsb, zc, lng, lnb, tm):
    t, c = zc.shape

    def body(dsb_ref, zc_ref, g_ref, b_ref, dzc_ref, dg_ref, db_ref, dbdw_ref):
        i = pl.program_id(0)
        xhat, rstd = _row_stats(zc_ref[...])
        gv = g_ref[...]
        zn = xhat * gv + b_ref[...]
        dzn = dsb_ref[...] * _silu_grad(zn)
        dzc = _ln_bwd_rows(dzn, xhat, rstd, gv)
        dzc_ref[...] = dzc

        @pl.when(i == 0)
        def _():
            dg_ref[...] = jnp.zeros_like(dg_ref)
            db_ref[...] = jnp.zeros_like(db_ref)
            dbdw_ref[...] = jnp.zeros_like(dbdw_ref)

        dg_ref[...] += _colsum(dzn * xhat)
        db_ref[...] += _colsum(dzn)
        dbdw_ref[...] += _colsum(dzc)

    return _call(
        body, name=name, grid=(t // tm,), in_specs=[_rows(tm, c), _rows(tm, c), _whole((1, c)), _whole((1, c))],
        out_specs=[_rows(tm, c), _whole((1, c)), _whole((1, c)), _whole((1, c))],
        out_shape=[jax.ShapeDtypeStruct((t, c), F32)] + [jax.ShapeDtypeStruct((1, c), F32)] * 3,
        args=[dsb, zc, lng, lnb], sem=("arbitrary",))


def _conv_bwd(name, dsb, zc, z, p, wdw, kw, lng, lnb, tm):
    t, c = z.shape
    n_i = t // tm
    n = tm + HALO

    def body(dsb_ref, dsbn_ref, zc_ref, zcn_ref, z_ref, h_ref, w_ref, g_ref, b_ref, dp_ref, dw_ref, dbin_ref, dg_ref, db_ref,
             dbdw_ref, sh, dz_s, wb):
        i = pl.program_id(0)

        @pl.when(i == 0)
        def _():
            dw_ref[...] = jnp.zeros_like(dw_ref)
            dg_ref[...] = jnp.zeros_like(dg_ref)
            db_ref[...] = jnp.zeros_like(db_ref)
            dbdw_ref[...] = jnp.zeros_like(dbdw_ref)

        gv, bv = g_ref[...], b_ref[...]

        def conv_out_grad(dsb_v, zc_v):
            xhat, rstd = _row_stats(zc_v)
            dzn = dsb_v * _silu_grad(xhat * gv + bv)
            return _ln_bwd_rows(dzn, xhat, rstd, gv), dzn, xhat

        dzc, dzn, xhat = conv_out_grad(dsb_ref[...], zc_ref[...])
        dg_ref[...] += _colsum(dzn * xhat)
        db_ref[...] += _colsum(dzn)
        dbdw_ref[...] += _colsum(dzc)
        sh[0, :tm, :] = dzc
        sh[0, tm:, :] = jnp.where(i < n_i - 1, conv_out_grad(dsbn_ref[...], zcn_ref[...])[0], 0.0)
        _shifted_copies(sh, n)
        _row_broadcasts(wb, w_ref, kw)
        nv = 8
        rows = nv * SUBLANES
        ways = 2
        tap_groups = [range(0, kw // 2), range(kw // 2, kw)]
        for lc in range(c // LANES):
            cols = slice(lc * LANES, (lc + 1) * LANES)
            for gi, taps in enumerate(tap_groups):

                def rowv(rv, accs, cols=cols, taps=taps, first=gi == 0):
                    r0 = pl.multiple_of(rv * rows, rows)
                    zv = z_ref[pl.ds(r0, rows), cols]
                    zs = [zv[v * SUBLANES:(v + 1) * SUBLANES] for v in range(nv)]
                    dz = [[None] * ways for _ in range(nv)]
                    new = []
                    for ai, k in enumerate(taps):
                        s = _tap(sh, r0, kw - 1 - k, rows, cols)
                        wk = wb[k, :, cols]
                        j = ai % ways
                        acc = accs[ai]
                        for v in range(nv):
                            sv = s[v * SUBLANES:(v + 1) * SUBLANES]
                            dz[v][j] = wk * sv if dz[v][j] is None else dz[v][j] + wk * sv
                            acc = acc + zs[v] * sv
                        new.append(acc)
                    part = jnp.concatenate([dz[v][0] + dz[v][1] for v in range(nv)], axis=0)
                    dz_s[pl.ds(r0, rows), cols] = part if first else dz_s[pl.ds(r0, rows), cols] + part
                    return tuple(new)

                accs = lax.fori_loop(0, tm // rows, rowv, tuple(jnp.zeros((SUBLANES, LANES), F32) for _ in taps))
                for ai, k in enumerate(taps):
                    dw_ref[k:k + 1, cols] += _colsum(accs[ai])
        dz = dz_s[...]
        a, g = h_ref[:, :c], h_ref[:, c:]
        sg = jax.nn.sigmoid(g)
        da = dz * sg
        dg = dz * a * (sg * (1.0 - sg))
        dp_ref[:, :c] = da.astype(BF16)
        dp_ref[:, c:] = dg.astype(BF16)

        @pl.when(i == 0)
        def _():
            dbin_ref[...] = jnp.zeros_like(dbin_ref)

        dbin_ref[:, :c] += _colsum(da)
        dbin_ref[:, c:] += _colsum(dg)

    return _call(
        body, name=name, grid=(n_i,),
        in_specs=[_rows(tm, c), _halo_next(tm, c, t), _rows(tm, c), _halo_next(tm, c, t), _rows(tm, c), _rows(tm, 2 * c, 1),
                  _whole(wdw.shape), _whole((1, c)), _whole((1, c))],
        out_specs=[_rows(tm, 2 * c), _whole((HALO, c)), _whole((1, 2 * c)), _whole((1, c)), _whole((1, c)), _whole((1, c))],
        out_shape=[jax.ShapeDtypeStruct((t, 2 * c), BF16), jax.ShapeDtypeStruct((HALO, c), F32),
                   jax.ShapeDtypeStruct((1, 2 * c), F32)] + [jax.ShapeDtypeStruct((1, c), F32)] * 3,
        args=[dsb, dsb, zc, zc, z, p, wdw, lng, lnb],
        scratch=[pltpu.VMEM((SUBLANES, n, c), F32), pltpu.VMEM((tm, c), F32), pltpu.VMEM((HALO, SUBLANES, c), F32)],
        sem=("arbitrary",))


def _adamw(name, w, g, m, v, tr, carry=None):
    r, c = w.shape
    c1 = 1.0 - ADAM_B1 ** ADAM_STEP
    c2 = 1.0 - ADAM_B2 ** ADAM_STEP

    def body(w_ref, g_ref, m_ref, v_ref, go_ref, d_ref, mo_ref, vo_ref):
        gv = g_ref[...]
        mn = ADAM_B1 * m_ref[...] + (1.0 - ADAM_B1) * gv
        vn = ADAM_B2 * v_ref[...] + (1.0 - ADAM_B2) * (gv * gv)
        go_ref[...] = gv
        d_ref[...] = -ADAM_LR * ((mn / c1) / (jnp.sqrt(vn / c2) + ADAM_EPS) + ADAM_WD * w_ref[...])
        mo_ref[...] = mn
        vo_ref[...] = vn

    spec = _rows(tr, c)
    return _call(body, name=name, grid=(r // tr,), in_specs=[spec] * 4, out_specs=[spec] * 4,
                 out_shape=[jax.ShapeDtypeStruct((r, c), F32)] * 4, args=[w, g, m, v], sem=("parallel",), carry=carry)


_RELATION_XOR = (2, 1, 3)


def _pair_sum(name, mine, recv, qc, *, part_shape, tr, in_map):
    pr, pc = part_shape

    def body(qc_ref, a_ref, b_ref, oh_ref):
        oh_ref[...] = (a_ref[...] + b_ref[...].astype(F32)).astype(BF16)

    def imap(k, i, qc_ref):
        part = qc_ref[0]
        for kk, m in enumerate(_RELATION_XOR):
            part = jnp.where(k == kk, jnp.bitwise_xor(qc_ref[0], m), part)
        return in_map(i, part, qc_ref[1])

    ispec = pl.BlockSpec((tr, pc), imap)
    ospec = pl.BlockSpec((None, tr, pc), lambda k, i, qc_ref: (k, i, 0))
    return pl.pallas_call(
        body, name=name,
        grid_spec=pltpu.PrefetchScalarGridSpec(num_scalar_prefetch=1, grid=(3, pr // tr), in_specs=[ispec, ispec],
                                               out_specs=ospec),
        out_shape=jax.ShapeDtypeStruct((3, pr, pc), BF16), compiler_params=_cparams(("parallel", "parallel")))(qc, mine, recv)


def _final_sum(name, mine, recv1, recv2, qc, *, part_shape, out_shape, tr, in_map, out_map):
    pr, pc = part_shape

    def body(qc_ref, a_ref, b_ref, r_ref, o_ref):
        own = a_ref[...] + b_ref[...].astype(F32)
        o_ref[...] = ((own + r_ref[0].astype(F32)) + r_ref[1].astype(F32)) + r_ref[2].astype(F32)

    ispec = pl.BlockSpec((tr, pc), lambda i, qc_ref: in_map(i, qc_ref[0], qc_ref[1]))
    return pl.pallas_call(
        body, name=name,
        grid_spec=pltpu.PrefetchScalarGridSpec(
            num_scalar_prefetch=1, grid=(pr // tr,),
            in_specs=[ispec, ispec, pl.BlockSpec((3, tr, pc), lambda i, qc_ref: (0, i, 0))],
            out_specs=pl.BlockSpec((tr, pc), lambda i, qc_ref: out_map(i, qc_ref[0], qc_ref[1]))),
        out_shape=jax.ShapeDtypeStruct(out_shape, F32), compiler_params=_cparams(("parallel",)))(qc, mine, recv1, recv2)


def _sum8(name, slots):
    def body(s_ref, o_ref):
        acc = s_ref[0]
        for d in range(1, 8):
            acc = acc + s_ref[d]
        o_ref[...] = acc

    return pl.pallas_call(body, name=name, out_shape=jax.ShapeDtypeStruct(slots.shape[1:], F32),
                          in_specs=[pl.BlockSpec(memory_space=pltpu.VMEM)], out_specs=pl.BlockSpec(memory_space=pltpu.VMEM),
                          compiler_params=pltpu.CompilerParams(vmem_limit_bytes=VMEM_LIMIT))(slots)


def _chips(x, y):
    return [(1 - x, y), (x, 1 - y), (1 - x, 1 - y)]


class _Big:
    def __init__(self, name, w, m, v, ax):
        self.name, self.w, self.m, self.v, self.ax = name, w, m, v, ax
        sr, sc = w.shape
        self.R, self.C = (sr * N_CHIPS, sc) if ax == 0 else (sr, sc * N_CHIPS)
        self.sr, self.sc = sr, sc
        self.hr = sr // 2

    def slot(self, ref, q, h=None):
        if self.ax == 1:
            cols = pl.ds(pl.multiple_of(q * self.sc, 128), self.sc)
            return ref.at[:, cols] if h is None else ref.at[pl.ds(pl.multiple_of(h * self.hr, 16), self.hr), cols]
        if h is None:
            return ref.at[pl.ds(pl.multiple_of(q * self.sr, 16), self.sr), :]
        return ref.at[pl.ds(pl.multiple_of(q * self.sr + h * self.hr, 16), self.hr), :]

    def half(self, ref, h):
        return ref.at[pl.ds(pl.multiple_of(h * self.hr, 16), self.hr), :]

    @property
    def part_shape(self):
        return (self.hr, self.sc)

    def cast_into_full(self, qc):
        tr = _tile_rows(self.sr, self.sc)
        nb = self.sr // tr

        def body(qc_ref, x_ref, o_ref):
            o_ref[...] = x_ref[...].astype(BF16)

        if self.ax == 1:
            ospec = pl.BlockSpec((tr, self.sc), lambda i, qc_ref: (i, qc_ref[0]))
        else:
            ospec = pl.BlockSpec((tr, self.sc), lambda i, qc_ref: (qc_ref[0] * nb + i, 0))
        return pl.pallas_call(
            body, name=f"cast_{self.name}",
            grid_spec=pltpu.PrefetchScalarGridSpec(num_scalar_prefetch=1, grid=(nb,),
                                                   in_specs=[pl.BlockSpec((tr, self.sc), lambda i, qc_ref: (i, 0))],
                                                   out_specs=ospec),
            out_shape=jax.ShapeDtypeStruct((self.R, self.C), BF16), compiler_params=_cparams(("parallel",)))(qc, self.w)

    def gather_ici(self, full, piece=(0, 1)):
        k, n = piece
        pr = self.hr // n
        assert pr * n == self.hr and pr % 16 == 0

        def plan(ro, rw, new, x, y, c):
            q = 2 * x + y
            r0 = c * self.hr + k * pr
            if self.ax == 1:
                mine = rw[0].at[pl.ds(pl.multiple_of(r0, 16), pr), pl.ds(pl.multiple_of(q * self.sc, 128), self.sc)]
            else:
                mine = rw[0].at[pl.ds(pl.multiple_of(q * self.sr + r0, 16), pr), :]
            return [(mine, mine, (cx, cy, c)) for cx, cy in _chips(x, y)], []

        return _Carry([], [full], [], plan, 3, 0)

    def gather_d2d(self, full):
        def plan(ro, rw, new, x, y, c):
            remote = []
            for cx, cy in _chips(x, y):
                piece = self.slot(rw[0], 2 * cx + cy, c)
                remote.append((piece, piece, (x, y, 1 - c)))
            return remote, []

        return _Carry([], [full], [], plan, 3)

    def rs_pair(self, g16):
        def plan(ro, rw, new, x, y, c):
            sib = (x, y, 1 - c)
            if self.ax == 1:
                rows = pl.ds(pl.multiple_of((1 - c) * self.hr, 16), self.hr)
                return [(ro[0].at[rows, :], new[0].at[rows, :], sib)], []
            return [(self.slot(ro[0], q, 1 - c), self.slot(new[0], q, 1 - c), sib) for q in range(N_CHIPS)], []

        return _Carry([g16], [], [jax.ShapeDtypeStruct((self.R, self.C), BF16)], plan, 1 if self.ax == 1 else N_CHIPS)

    def _piece_map(self, nb):
        if self.ax == 1:
            return lambda i, q, c: (c * nb + i, q)
        return lambda i, q, c: ((q * 2 + c) * nb + i, 0)

    def rs_pairsum(self, tag, g32, recv1, qc):
        tr = _tile_rows(self.hr, self.sc)
        return _pair_sum(f"rs_pairsum_{tag}", g32, recv1, qc, part_shape=self.part_shape, tr=tr,
                         in_map=self._piece_map(self.hr // tr))

    def rs_ici(self, cs16):
        def plan(ro, rw, new, x, y, c):
            return [(ro[0].at[k], new[0].at[k], (cx, cy, c)) for k, (cx, cy) in enumerate(_chips(x, y))], []

        return _Carry([cs16], [], [jax.ShapeDtypeStruct((3,) + self.part_shape, BF16)], plan, 3)

    def rs_final(self, tag, g32, recv1, recv2, qc):
        tr = _tile_rows(self.hr, self.sc)
        nb = self.hr // tr
        in_map = self._piece_map(nb)
        out_map = lambda i, q, c: (c * nb + i, 0)
        return _final_sum(f"rs_final_{tag}", g32, recv1, recv2, qc, part_shape=self.part_shape, out_shape=(self.sr, self.sc),
                          tr=tr, in_map=in_map, out_map=out_map)

    def rs_share(self, ghalf):
        def plan(ro, rw, new, x, y, c):
            piece = self.half(rw[0], c)
            return [(piece, piece, (x, y, 1 - c))], []

        return _Carry([], [ghalf], [], plan, 1)


def _small_allgather(packed):
    nr = packed.shape[0]

    def plan(ro, rw, new, x, y, c):
        me = 4 * x + 2 * y + c
        remote = []
        for fx in (0, 1):
            for fy in (0, 1):
                for fc in (0, 1):
                    if fx or fy or fc:
                        dev = (1 - x if fx else x, 1 - y if fy else y, 1 - c if fc else c)
                        remote.append((ro[0], new[0].at[me], dev))
        return remote, [(ro[0], new[0].at[me])]

    return _Carry([packed], [], [jax.ShapeDtypeStruct((8, nr, 128), F32)], plan, 7, 1)


def _conv_w_allgather(padded, cs):
    def plan(ro, rw, new, x, y, c):
        cols = pl.ds(pl.multiple_of((2 * x + y) * cs, 128), cs)
        remote = [(ro[0], new[0].at[:, cols], (cx, cy, c)) for cx, cy in _chips(x, y)]
        return remote, [(ro[0], new[0].at[:, cols])]

    return _Carry([padded], [], [jax.ShapeDtypeStruct((HALO, cs * N_CHIPS), F32)], plan, 3, 1)


def _pick(n, want):
    if n <= want:
        return n
    for t in range(want, 15, -16):
        if t % 16 == 0 and n % t == 0:
            return t
    raise ValueError(f"no tile for {n} (want {want})")


def _tile_rows(nrows, ncols, budget=2 * 1024 * 1024):
    return _pick(nrows, max(16, (budget // (4 * ncols)) // 16 * 16))


def _pick128(n, want):
    if n <= want:
        return n
    for t in range(want, 127, -128):
        if n % t == 0:
            return t
    raise ValueError(f"no lane tile for {n} (want {want})")


def _pack_rows(parts):
    out, spans, r0 = [], [], 0
    for p in parts:
        flat = p.reshape(-1).astype(F32)
        n = flat.shape[0]
        rows = -(-n // 1024) * 8
        flat = jnp.pad(flat, (0, rows * 128 - n))
        out.append(flat.reshape(rows, 128))
        spans.append((r0, rows, n))
        r0 += rows
    return jnp.concatenate(out, axis=0), spans


def _unpack_rows(packed, spans, shapes):
    res = []
    for (r0, rows, n), shp in zip(spans, shapes, strict=True):
        res.append(packed[r0:r0 + rows].reshape(-1)[:n].reshape(shp))
    return res


def _ident(accs, ex):
    return [accs[0]]


def kernel(x, ffn1_w_gu, ffn1_w_down, ln1_g, ln1_b, w_in, b_in, sgu_ln_g, sgu_ln_b, sgu_w_s, sgu_b_s, w_a_proj, conv_w_dw, conv_b_dw, conv_ln_g, conv_ln_b, w_b_proj, w_out, ln2_g, ln2_b, ffn2_w_gu, ffn2_w_down, ln3_g, ln3_b, loss_target, m_ffn1_w_gu, m_ffn1_w_down, m_ln1_g, m_ln1_b, m_w_in, m_b_in, m_sgu_ln_g, m_sgu_ln_b, m_sgu_w_s, m_sgu_b_s, m_w_a_proj, m_conv_w_dw, m_conv_b_dw, m_conv_ln_g, m_conv_ln_b, m_w_b_proj, m_w_out, m_ln2_g, m_ln2_b, m_ffn2_w_gu, m_ffn2_w_down, m_ln3_g, m_ln3_b, v_ffn1_w_gu, v_ffn1_w_down, v_ln1_g, v_ln1_b, v_w_in, v_b_in, v_sgu_ln_g, v_sgu_ln_b, v_sgu_w_s, v_sgu_b_s, v_w_a_proj, v_conv_w_dw, v_conv_b_dw, v_conv_ln_g, v_conv_ln_b, v_w_b_proj, v_w_out, v_ln2_g, v_ln2_b, v_ffn2_w_gu, v_ffn2_w_down, v_ln3_g, v_ln3_b):
    args = dict(locals())
    assert x.shape[0] == 1 and ffn1_w_gu.shape[0] == 1
    T, D = x.shape[1], x.shape[2]
    F = ffn1_w_down.shape[1] * N_CHIPS
    W = sgu_ln_g.shape[1]
    KW = conv_w_dw.shape[1]
    assert KW - 1 <= HALO and T % SGU_BLOCK == 0

    mx, my, mc = lax.axis_index("x"), lax.axis_index("y"), lax.axis_index("c")
    q = 2 * mx + my
    qc = jnp.stack([q, mc]).astype(jnp.int32)

    big_names = [("ffn1_w_gu", 1), ("ffn1_w_down", 0), ("w_in", 1), ("w_a_proj", 1), ("w_b_proj", 1), ("w_out", 0),
                 ("ffn2_w_gu", 1), ("ffn2_w_down", 0)]
    B = {n: _Big(n, args[n][0], args["m_" + n][0], args["v_" + n][0], ax) for n, ax in big_names}
    own = {n: b.cast_into_full(qc) for n, b in B.items()}

    x2d = x[0]
    tgt = loss_target[0]
    tm_r = _pick(T, 256)
    tm_ln = _pick(T, 512)
    tm = _pick(T, 1024)
    tn = _pick128(D, 1024)
    nj = D // tn
    tng = _pick128(D, 512)
    njg = D // tng
    tnf = _pick128(F, 512)
    nf = F // tnf
    tnw = _pick128(W, 1024)
    tnd = _pick128(D, 512)
    tmw = _pick128(W, 512)
    SUB = 256

    def ffn_up(tag, xb_, wgu, carry=None):
        def epi(accs, ex):
            g, u = accs
            s = jax.nn.sigmoid(g)
            sg = g * s
            return [u * (s * (1.0 + g * (1.0 - s))), sg, sg * u]

        return _mm(f"{tag}_up", [(xb_, wgu, 0, 0, 0), (xb_, wgu, 0, 0, nf)], M=T, N=F, tm=tm, tn=tnf, tk=D, nk=1, epilogue=epi,
                   outs=[(F, BF16, 0)] * 3, sum_pairs=False, carry=carry, sub=SUB)

    def ffn_down(tag, act, wd, carry=None):
        return _mm(f"{tag}_down", [(act, wd, 0, 0, 0)], M=T, N=D, tm=tm, tn=_pick128(D, 512), tk=F, nk=1, epilogue=_ident,
                   outs=[(D, F32, 0)], carry=carry)

    def ffn_dact(tag, drh, wd, dgate_f, dup_f, carry=None):
        def epi(accs, ex):
            da = accs[0]
            return [da * ex[0].astype(F32), da * ex[1].astype(F32)]

        return _mm(f"{tag}_dact", [(drh, wd, 0, 0, 0)], tb=True, M=T, N=F, tm=_pick(T, 2048), tn=tnf, tk=D, nk=1, epilogue=epi,
                   outs=[(F, BF16, 0)] * 2, extras=[(dgate_f, "mn", 0), (dup_f, "mn", 0)], carry=carry, sub=SUB)

    def ffn_dwdown(tag, act, drh, carry=None):
        return _wgrad(f"{tag}_dwdown", act, drh, M=F, N=D, T=T, tm=tnf, tn=tnd, tk=T, carry=carry)

    def ffn_dwgate(tag, xb_, dg, carry=None):
        return _wgrad(f"{tag}_dwgate", xb_, dg, M=D, N=F, T=T, tm=tnd, tn=tnf, tk=T, ncols=2 * F, carry=carry)

    def ffn_dwup(tag, xb_, du, into, carry=None):
        return _wgrad(f"{tag}_dwup", xb_, du, M=D, N=F, T=T, tm=tnd, tn=tnf, tk=T, ncols=2 * F, into=into, joff=nf, carry=carry)

    def ffn_dx(tag, which, da, wgu, addends, carry=None):
        def epi(accs, ex):
            tot = accs[0] + ALPHA * ex[0]
            for e in ex[1:]:
                tot = tot + e
            return [tot]

        return _mm(f"{tag}_dx_{which}", [(da, wgu, 0, 1 if which == "up" else 0, 0)], tb=True, M=T, N=D, tm=tm,
                   tn=_pick128(D, 256), tk=F, nk=1, epilogue=epi if addends else _ident, outs=[(D, F32, 0)],
                   extras=[(a, "mn", 0) for a in addends], carry=carry)

    wdw_pad = jnp.pad(conv_w_dw[0], ((0, HALO - KW), (0, 0)))
    c0, un = _merge(B["ffn1_w_gu"].gather_ici(own["ffn1_w_gu"]), _conv_w_allgather(wdw_pad, conv_w_dw.shape[2]))
    (xb,), co = _cast_bf16("cast_x", x2d, tm_r, carry=c0)
    (wgu1,), (wdw_full,) = un(co)
    (wgu1,) = _exchange("gather_d2d_ffn1_w_gu", B["ffn1_w_gu"].gather_d2d(wgu1))

    c, un = _merge(B["ffn1_w_down"].gather_ici(own["ffn1_w_down"]), B["w_in"].gather_ici(own["w_in"]))
    (g1, u1, a1), co = ffn_up("ffn1", xb, wgu1, carry=c)
    (wd1,), (win,) = un(co)
    (wd1,) = _exchange("gather_d2d_ffn1_w_down", B["ffn1_w_down"].gather_d2d(wd1))
    b_gu2, b_d2 = B["ffn2_w_gu"], B["ffn2_w_down"]
    c, un = _merge(B["w_a_proj"].gather_ici(own["w_a_proj"]), B["w_b_proj"].gather_ici(own["w_b_proj"]),
                   B["w_out"].gather_ici(own["w_out"]), B["w_in"].gather_d2d(win), b_gu2.gather_ici(own["ffn2_w_gu"], (0, 4)))
    (fo1,), co = ffn_down("ffn1", a1, wd1, carry=c)
    (wa,), (wb,), (wout,), (win,), (wgu2,) = un(co)
    c, un = _merge(B["w_a_proj"].gather_d2d(wa), B["w_b_proj"].gather_d2d(wb), B["w_out"].gather_d2d(wout),
                   b_gu2.gather_ici(wgu2, (1, 4)))
    (x1b, xh1, rs1), co = _ln_fwd("ln1", x2d, fo1, ln1_g, ln1_b, 0.5, tm_ln, carry=c)
    (wa,), (wb,), (wout,), (wgu2,) = un(co)

    c, un = _merge(b_gu2.gather_ici(wgu2, (1, 2)), b_d2.gather_ici(own["ffn2_w_down"], (0, 2)))
    (proj,), co = _mm("in_proj", [(x1b, win, 0, 0, 0)], M=T, N=4 * D, tm=tm, tn=tn, tk=D, nk=1,
                      epilogue=lambda accs, ex: [accs[0] + ex[0]], outs=[(4 * D, F32, 0)], extras=[(b_in, "n", 0)], carry=c)
    (wgu2,), (wd2,) = un(co)
    wm = sgu_w_s[0]
    bst = sgu_b_s[0].T
    sa, z = _sgu_fwd("sgu_fwd", proj, sgu_ln_g, sgu_ln_b, wm, bst, tm_ln)
    (zc, sb), (wd2,) = _conv_fwd("conv_fwd", z, wdw_full, KW, conv_b_dw, conv_ln_g, conv_ln_b, tm_r,
                                 carry=b_d2.gather_ici(wd2, (1, 2)))

    def epi_mix(accs, ex):
        ya_, yb_ = accs
        ga, gb = jax.nn.sigmoid(ex[0]), jax.nn.sigmoid(ex[1])
        return [ga * ya_ + gb * yb_, ga, gb, ya_ * (ga * (1.0 - ga)), yb_ * (gb * (1.0 - gb))]

    mixin, gate_a, gate_b, dlog_a, dlog_b = _mm(
        "branch_proj", [(sa, wa, 0, 0, 0), (sb, wb, 0, 0, 0)], M=T, N=D, tm=tm, tn=tng, tk=W, nk=1, epilogue=epi_mix,
        outs=[(D, BF16, 0)] * 5, sum_pairs=False, sub=SUB, extras=[(proj, "mn", 2 * njg), (proj, "mn", 3 * njg)])
    c, un = _merge(b_gu2.gather_d2d(wgu2), b_d2.gather_d2d(wd2))
    (mix,), co = _mm("out_proj", [(mixin, wout, 0, 0, 0)], M=T, N=D, tm=tm, tn=tn, tk=D, nk=1, epilogue=_ident,
                     outs=[(D, F32, 0)], carry=c)
    (wgu2,), (wd2,) = un(co)
    x2b, xh2, rs2 = _ln_fwd("ln2", xh1, mix, ln2_g, ln2_b, 1.0, tm_ln, prev=(ln1_g, ln1_b))
    g2, u2, a2 = ffn_up("ffn2", x2b, wgu2)
    (fo2,) = ffn_down("ffn2", a2, wd2)
    dr3, dr3h, loss_part, dln3_g, dln3_b = _ln3_loss("ln3_loss", xh2, fo2, ln3_g, ln3_b, tgt, tm_r, (ln2_g, ln2_b))

    b_gu2, b_d2 = B["ffn2_w_gu"], B["ffn2_w_down"]
    dg2, du2 = ffn_dact("ffn2", dr3h, wd2, g2, u2)
    dwd2 = ffn_dwdown("ffn2", a2, dr3h)
    dwgu2, (r1_d2,) = ffn_dwgate("ffn2", x2b, dg2, carry=b_d2.rs_pair(dwd2[1]))
    dwgu2 = ffn_dwup("ffn2", x2b, du2, dwgu2)
    cs_d2 = b_d2.rs_pairsum("ffn2_w_down", dwd2[0], r1_d2, qc)
    c, un = _merge(b_gu2.rs_pair(dwgu2[1]), b_d2.rs_ici(cs_d2))
    (dx2_gate,), co = ffn_dx("ffn2", "gate", dg2, wgu2, [], carry=c)
    (r1_gu2,), (r2_d2,) = un(co)
    (dx2,) = ffn_dx("ffn2", "up", du2, wgu2, [dr3, dx2_gate])
    cs_gu2 = b_gu2.rs_pairsum("ffn2_w_gu", dwgu2[0], r1_gu2, qc)
    gh_d2 = b_d2.rs_final("ffn2_w_down", dwd2[0], r1_d2, r2_d2, qc)

    dr2, dr2b, dln2_g, dln2_b = _ln_bwd("ln2_bwd", dx2, xh2, rs2, ln2_g, 1.0, tm_ln)

    def epi_dmix(accs, ex):
        tiles = [accs[0] * e.astype(F32) for e in ex]
        sums = [t.reshape(-1, SUBLANES, t.shape[1]).sum(axis=0) for t in tiles[2:]]
        return tiles + sums

    dya, dyb, dla, dlb, sla, slb = _mm(
        "out_proj_bwd", [(dr2b, wout, 0, 0, 0)], tb=True, M=T, N=D, tm=tm, tn=tng, tk=D, nk=1, epilogue=epi_dmix,
        outs=[(D, BF16, 0)] * 4 + [(D, F32, 0, SUBLANES)] * 2, sub=SUB,
        extras=[(gate_a, "mn", 0), (gate_b, "mn", 0), (dlog_a, "mn", 0), (dlog_b, "mn", 0)])
    dwout = _wgrad("dw_out", mixin, dr2b, M=D, N=D, T=T, tm=tnd, tn=tnd, tk=T)
    (dsa,) = _mm("a_proj_bwd", [(dya, wa, 0, 0, 0)], tb=True, M=T, N=W, tm=tm, tn=tnw, tk=D, nk=1, epilogue=_ident,
                 outs=[(W, F32, 0)])
    (dsb,) = _mm("b_proj_bwd", [(dyb, wb, 0, 0, 0)], tb=True, M=T, N=W, tm=tm, tn=tnw, tk=D, nk=1, epilogue=_ident,
                 outs=[(W, F32, 0)])
    dwa = _wgrad("dw_a_proj", sa, dya, M=W, N=D, T=T, tm=tmw, tn=tnd, tk=T)
    dwb = _wgrad("dw_b_proj", sb, dyb, M=W, N=D, T=T, tm=tmw, tn=tnd, tk=T)

    dpa, dwm, dbst, dsgu_g, dsgu_b, dbin_a = _sgu_bwd("sgu_bwd", proj, dsa, sgu_ln_g, sgu_ln_b, wm, bst, tm_ln)
    dpb, dwdw, dbin_b, dcln_g, dcln_b, dbdw = _conv_bwd("conv_bwd", dsb, zc, z, proj, wdw_full, KW, conv_ln_g, conv_ln_b, tm_r)

    dps = [dpa, dpb, dla, dlb]
    db_in = jnp.concatenate([dbin_a, dbin_b, _colsum_rows("db_in_gate_a", sla, sla.shape[0]),
                             _colsum_rows("db_in_gate_b", slb, slb.shape[0])], axis=1)
    c, un = _merge(b_gu2.rs_ici(cs_gu2), b_d2.rs_share(gh_d2))
    (dx1,), co = _mm("in_proj_bwd", [(dp, win, 0, k, 0) for k, dp in enumerate(dps)], tb=True, M=T, N=D, tm=tm,
                     tn=_pick128(D, 256), tk=D, nk=1, epilogue=lambda accs, ex: [accs[0] + ALPHA * ex[0]], outs=[(D, F32, 0)],
                     extras=[(dr2, "mn", 0)], carry=c)
    (r2_gu2,), (g_d2,) = un(co)
    gh_gu2 = b_gu2.rs_final("ffn2_w_gu", dwgu2[0], r1_gu2, r2_gu2, qc)
    dwin = _wgrad("dw_in_0", x1b, dps[0], M=D, N=D, T=T, tm=tnd, tn=tnd, tk=T, ncols=4 * D)
    for k in range(1, 4):
        dwin = _wgrad(f"dw_in_{k}", x1b, dps[k], M=D, N=D, T=T, tm=tnd, tn=tnd, tk=T, ncols=4 * D, into=dwin,
                      joff=k * (D // tnd))

    mix_names = ["w_in", "w_a_proj", "w_b_proj", "w_out"]
    mix_grads = dict(zip(mix_names, [dwin, dwa, dwb, dwout], strict=True))
    c, un = _merge(*[B[n].rs_pair(mix_grads[n][1]) for n in mix_names])
    (dr1, dr1h, dln1_g, dln1_b), co = _ln_bwd("ln1_bwd", dx1, xh1, rs1, ln1_g, 0.5, tm_ln, carry=c)
    r1_mix = {n: r1 for n, (r1,) in zip(mix_names, un(co), strict=True)}
    cs_mix = {n: B[n].rs_pairsum(n, mix_grads[n][0], r1_mix[n], qc) for n in mix_names}

    small_names = ["ln1_g", "ln1_b", "b_in", "sgu_ln_g", "sgu_ln_b", "sgu_w_s", "sgu_b_s", "conv_w_dw", "conv_b_dw",
                   "conv_ln_g", "conv_ln_b", "ln2_g", "ln2_b", "ln3_g", "ln3_b"]
    small_parts = {"ln1_g": dln1_g, "ln1_b": dln1_b, "b_in": db_in, "sgu_ln_g": dsgu_g, "sgu_ln_b": dsgu_b, "sgu_w_s": dwm,
                   "sgu_b_s": dbst.T, "conv_w_dw": dwdw[:KW], "conv_b_dw": dbdw, "conv_ln_g": dcln_g, "conv_ln_b": dcln_b,
                   "ln2_g": dln2_g, "ln2_b": dln2_b, "ln3_g": dln3_g, "ln3_b": dln3_b}
    packed, spans = _pack_rows([small_parts[n] for n in small_names])

    b_gu1, b_d1 = B["ffn1_w_gu"], B["ffn1_w_down"]
    c, un = _merge(B["w_in"].rs_ici(cs_mix["w_in"]), _small_allgather(packed), b_gu2.rs_share(gh_gu2))
    (dg1, du1), co = ffn_dact("ffn1", dr1h, wd1, g1, u1, carry=c)
    (r2_win,), (small_slots,), (g_gu2,) = un(co)
    c, un = _merge(*[B[n].rs_ici(cs_mix[n]) for n in mix_names[1:]])
    dwgu1, co = ffn_dwgate("ffn1", xb, dg1, carry=c)
    r2_mix = [[r2_win]] + un(co)
    gh_mix = [B[n].rs_final(n, mix_grads[n][0], r1_mix[n], r2, qc) for n, (r2,) in zip(mix_names, r2_mix, strict=True)]
    dwgu1 = ffn_dwup("ffn1", xb, du1, dwgu1)
    dwd1, (r1_gu1,) = ffn_dwdown("ffn1", a1, dr1h, carry=b_gu1.rs_pair(dwgu1[1]))
    cs_gu1 = b_gu1.rs_pairsum("ffn1_w_gu", dwgu1[0], r1_gu1, qc)
    c, un = _merge(b_gu1.rs_ici(cs_gu1), b_d1.rs_pair(dwd1[1]),
                   *[B[n].rs_share(gh) for n, gh in zip(mix_names, gh_mix, strict=True)])
    (dx_gate,), co = ffn_dx("ffn1", "gate", dg1, wgu1, [], carry=c)
    (r2_gu1,), (r1_d1,), *g_mixs = un(co)
    g_mix = {n: g for n, (g,) in zip(mix_names, g_mixs, strict=True)}
    cs_d1 = b_d1.rs_pairsum("ffn1_w_down", dwd1[0], r1_d1, qc)
    gh_gu1 = b_gu1.rs_final("ffn1_w_gu", dwgu1[0], r1_gu1, r2_gu1, qc)
    c, un = _merge(b_d1.rs_ici(cs_d1), b_gu1.rs_share(gh_gu1))
    (dx,), co = ffn_dx("ffn1", "up", du1, wgu1, [dr1, dx_gate], carry=c)
    (r2_d1,), (g_gu1,) = un(co)
    gh_d1 = b_d1.rs_final("ffn1_w_down", dwd1[0], r1_d1, r2_d1, qc)

    grads = {"ffn1_w_gu": g_gu1, "ffn2_w_gu": g_gu2, "ffn2_w_down": g_d2, **g_mix}
    outs_g, outs_d, outs_m, outs_v = {}, {}, {}, {}

    def adamw_big(n, g, carry=None):
        b = B[n]
        return _adamw(f"adamw_{n}", b.w, g, b.m, b.v, _tile_rows(b.sr, b.sc), carry=carry)

    upd = {}
    upd["w_a_proj"], (grads["ffn1_w_down"],) = adamw_big("w_a_proj", grads["w_a_proj"], carry=b_d1.rs_share(gh_d1))
    for n, _ in big_names:
        if n not in upd:
            upd[n] = adamw_big(n, grads[n])
        g_, d_, m_, v_ = upd[n]
        outs_g[n], outs_d[n], outs_m[n], outs_v[n] = g_[None], d_[None], m_[None], v_[None]
    gsum = _sum8("small_sum", small_slots)
    full_shapes = [args[n].shape if n != "conv_w_dw" else (1, KW, W) for n in small_names]
    gsmall = dict(zip(small_names, _unpack_rows(gsum, spans, full_shapes), strict=True))
    cs = conv_w_dw.shape[2]
    gsmall["conv_w_dw"] = lax.dynamic_slice_in_dim(gsmall["conv_w_dw"], q * cs, cs, axis=2)
    pw, spans2 = _pack_rows([args[n] for n in small_names])
    pg, _ = _pack_rows([gsmall[n] for n in small_names])
    pm, _ = _pack_rows([args["m_" + n] for n in small_names])
    pv, _ = _pack_rows([args["v_" + n] for n in small_names])
    _, pd, pmn, pvn = _adamw("adamw_small", pw, pg, pm, pv, pw.shape[0])
    shapes2 = [args[n].shape for n in small_names]
    for dst, src in ((outs_d, pd), (outs_m, pmn), (outs_v, pvn)):
        dst.update(zip(small_names, _unpack_rows(src, spans2, shapes2), strict=True))
    outs_g.update(gsmall)

    loss = lax.psum(loss_part[0, 0], ("x", "y", "c"))
    order = ["ffn1_w_gu", "ffn1_w_down", "ln1_g", "ln1_b", "w_in", "b_in", "sgu_ln_g", "sgu_ln_b", "sgu_w_s", "sgu_b_s",
             "w_a_proj", "conv_w_dw", "conv_b_dw", "conv_ln_g", "conv_ln_b", "w_b_proj", "w_out", "ln2_g", "ln2_b",
             "ffn2_w_gu", "ffn2_w_down", "ln3_g", "ln3_b"]
    return (loss, dx[None], *[outs_g[n] for n in order], *[outs_d[n] for n in order], *[outs_m[n] for n in order],
            *[outs_v[n] for n in order])
```
